```python
import jax, jax.numpy as jnp
from jax import lax
import numpy as np

D_MODEL = 1024
BATCH = 8
SEQ = 8192
DEPTH = 1

N_MEM = 256
MLA_HEADS = 8
MLA_NOPE = 64
MLA_ROPE = 32
MLA_V = 64
MLA_Q_RANK = 256
MLA_KV_RANK = 128
MLA_WIDTH = MLA_HEADS * MLA_V
RET_HEADS = 4
RET_DK = 128
RET_DV = 128
RET_CHUNK = 128
RET_WIDTH = RET_HEADS * RET_DV
X_HEADS = 4
X_HEAD_DIM = D_MODEL // X_HEADS
D_FF = 2816
CONV_W = 3
Q_BLOCK = 128
ROPE_THETA = 10000.0
EPS = 1e-6
IN_WIDTH = MLA_Q_RANK + MLA_KV_RANK + MLA_ROPE + 2 * RET_HEADS * RET_DK + 2 * RET_WIDTH + 2 * D_MODEL

kernel_name = "hybrid_mla_retention_gated_block"


def rms_norm(x, g):
    xf = x.astype(jnp.float32)
    y = xf * lax.rsqrt(jnp.mean(xf * xf, axis=-1, keepdims=True) + EPS)
    return (y * g.astype(jnp.float32)).astype(x.dtype)


def rotary(x, positions):
    d = x.shape[-1]
    half = d // 2
    inv = ROPE_THETA ** (-jnp.arange(half, dtype=jnp.float32) / half)
    ang = positions.astype(jnp.float32)[:, :, None] * inv
    cos = jnp.cos(ang)[:, :, None, :]
    sin = jnp.sin(ang)[:, :, None, :]
    x1 = x[..., :half].astype(jnp.float32)
    x2 = x[..., half:].astype(jnp.float32)
    return jnp.concatenate([x1 * cos - x2 * sin, x2 * cos + x1 * sin], axis=-1).astype(x.dtype)


def causal_block_attention(q, k, v, scale):
    B, H, S, dq = q.shape
    dv = v.shape[-1]
    nb = S // Q_BLOCK
    qb = q.reshape(B, H, nb, Q_BLOCK, dq).transpose(2, 0, 1, 3, 4)
    kpos = jnp.arange(S)
    neg = jnp.finfo(jnp.float32).min

    def one(args):
        q_blk, i = args
        s = jnp.einsum('bhqd,bhkd->bhqk', q_blk, k).astype(jnp.float32) * scale
        qpos = i * Q_BLOCK + jnp.arange(Q_BLOCK)
        s = jnp.where(kpos[None, :] <= qpos[:, None], s, neg)
        p = jax.nn.softmax(s, axis=-1).astype(v.dtype)
        return jnp.einsum('bhqk,bhkd->bhqd', p, v)

    o = lax.map(one, (qb, jnp.arange(nb)))
    return o.transpose(1, 2, 0, 3, 4).reshape(B, H, S, dv)


def retention_chunkwise(q, k, v):
    B, S, H, dk = q.shape
    dv = v.shape[-1]
    C = RET_CHUNK
    nc = S // C
    log_g = jnp.log1p(-jnp.exp2(-5.0 - jnp.arange(H, dtype=jnp.float32)))
    idx = jnp.arange(C, dtype=jnp.float32)
    rel = idx[:, None] - idx[None, :]
    dmask = jnp.where(rel >= 0, jnp.exp(log_g[:, None, None] * jnp.maximum(rel, 0.0)), 0.0)
    zeta = jnp.exp(log_g[None, :] * (C - 1.0 - idx)[:, None])
    xi = jnp.exp(log_g[None, :] * (idx + 1.0)[:, None])
    chunk_decay = jnp.exp(log_g * C)

    qc = q.reshape(B, nc, C, H, dk)
    kc = k.reshape(B, nc, C, H, dk)
    vc = v.reshape(B, nc, C, H, dv)
    s = jnp.einsum('bcnhd,bcmhd->bchnm', qc, kc) * dmask.astype(q.dtype)[None, None]
    inner = jnp.einsum('bchnm,bcmhe->bcnhe', s, vc)
    kv = jnp.einsum('bcmhd,bcmhe->cbhde', kc * zeta.astype(q.dtype)[None, None, :, :, None], vc).astype(jnp.float32)

    def step(R, kv_c):
        return R * chunk_decay[None, :, None, None] + kv_c, R

    _, R_prev = lax.scan(step, jnp.zeros((B, H, dk, dv), jnp.float32), kv)
    cross = jnp.einsum('bcnhd,cbhde->bcnhe', qc * xi.astype(q.dtype)[None, None, :, :, None], R_prev.astype(q.dtype))
    return (inner + cross).reshape(B, S, H, dv)


def head_group_norm(y, g):
    B, S, H, d = y.shape
    yf = y.astype(jnp.float32)
    mu = jnp.mean(yf, axis=-1, keepdims=True)
    var = jnp.mean(jnp.square(yf - mu), axis=-1, keepdims=True)
    yn = ((yf - mu) * lax.rsqrt(var + EPS)).reshape(B, S, H * d)
    return (yn * g.astype(jnp.float32)).astype(y.dtype)


def causal_dwconv(u, w, b):
    C = u.shape[-1]
    y = lax.conv_general_dilated(u, w[:, None, :].astype(u.dtype), window_strides=(1,),
                                 padding=[(CONV_W - 1, 0)],
                                 dimension_numbers=('NWC', 'WIO', 'NWC'),
                                 feature_group_count=C)
    return y + b


def _fwd_setup_inputs(seed: int = 0) -> dict:
    key = jax.random.key(seed)
    ks = jax.random.split(key, 32)
    f32 = jnp.float32
    L = DEPTH

    def w(k, shape, fan_in):
        return jax.random.normal(k, shape, f32) * (fan_in ** -0.5)

    def gain(k, shape):
        return 1.0 + 0.05 * jax.random.normal(k, shape, f32)

    x = jax.random.normal(ks[0], (BATCH, SEQ, D_MODEL), f32)
    mem = jax.random.normal(ks[1], (BATCH, N_MEM, D_MODEL), f32)
    positions = jnp.broadcast_to(jnp.arange(SEQ, dtype=jnp.int32)[None, :], (BATCH, SEQ))
    return {
        "x": x,
        "mem": mem,
        "positions": positions,
        "g_mix": gain(ks[2], (L, D_MODEL)),
        "w_in": w(ks[3], (L, D_MODEL, IN_WIDTH), D_MODEL),
        "b_gate": 0.01 * jax.random.normal(ks[4], (L, 2 * D_MODEL), f32),
        "g_q_lat": gain(ks[5], (L, MLA_Q_RANK)),
        "w_uq": w(ks[6], (L, MLA_Q_RANK, MLA_HEADS * (MLA_NOPE + MLA_ROPE)), MLA_Q_RANK),
        "g_kv_lat": gain(ks[7], (L, MLA_KV_RANK)),
        "w_ukv": w(ks[8], (L, MLA_KV_RANK, MLA_HEADS * (MLA_NOPE + MLA_V)), MLA_KV_RANK),
        "w_proj_mla": w(ks[9], (L, MLA_WIDTH, D_MODEL), MLA_WIDTH),
        "g_ret": gain(ks[10], (L, RET_WIDTH)),
        "w_proj_ret": w(ks[11], (L, RET_WIDTH, D_MODEL), RET_WIDTH),
        "w_out": w(ks[12], (L, D_MODEL, D_MODEL), D_MODEL),
        "g_cross": gain(ks[13], (L, D_MODEL)),
        "g_mem": gain(ks[14], (L, D_MODEL)),
        "w_xq": w(ks[15], (L, D_MODEL, D_MODEL), D_MODEL),
        "w_xkv": w(ks[16], (L, D_MODEL, 2 * D_MODEL), D_MODEL),
        "w_xo": w(ks[17], (L, D_MODEL, D_MODEL), D_MODEL),
        "g_ffn": gain(ks[18], (L, D_MODEL)),
        "w_up": w(ks[19], (L, D_MODEL, 2 * D_FF), D_MODEL),
        "w_conv": w(ks[20], (L, CONV_W, 2 * D_FF), CONV_W),
        "b_conv": 0.01 * jax.random.normal(ks[21], (L, 2 * D_FF), f32),
        "w_down": w(ks[22], (L, D_FF, D_MODEL), D_FF),
        "g_final": gain(ks[23], (D_MODEL,)),
    }


def _fwd_reference(x, mem, positions, g_mix, w_in, b_gate, g_q_lat, w_uq, g_kv_lat, w_ukv, w_proj_mla,
              g_ret, w_proj_ret, w_out, g_cross, g_mem, w_xq, w_xkv, w_xo, g_ffn, w_up, w_conv,
              b_conv, w_down, g_final):
    B, S, D = x.shape
    M = mem.shape[1]
    sizes = (MLA_Q_RANK, MLA_KV_RANK, MLA_ROPE, RET_HEADS * RET_DK, RET_HEADS * RET_DK,
             RET_WIDTH, RET_WIDTH, 2 * D_MODEL)
    cuts = []
    acc = 0
    for sz in sizes[:-1]:
        acc += sz
        cuts.append(acc)
    h = x
    for l in range(DEPTH):
        u = rms_norm(h, g_mix[l])
        z = u @ w_in[l]
        c_q, c_kv, k_r, rq, rk, rv, rg, gates = jnp.split(z, cuts, axis=-1)

        q = (rms_norm(c_q, g_q_lat[l]) @ w_uq[l]).reshape(B, S, MLA_HEADS, MLA_NOPE + MLA_ROPE)
        q_nope, q_rope = q[..., :MLA_NOPE], rotary(q[..., MLA_NOPE:], positions)
        kv = (rms_norm(c_kv, g_kv_lat[l]) @ w_ukv[l]).reshape(B, S, MLA_HEADS, MLA_NOPE + MLA_V)
        k_nope, v_a = kv[..., :MLA_NOPE], kv[..., MLA_NOPE:]
        k_rope = jnp.broadcast_to(rotary(k_r[:, :, None, :], positions), (B, S, MLA_HEADS, MLA_ROPE))
        q_full = jnp.concatenate([q_nope, q_rope], axis=-1).transpose(0, 2, 1, 3)
        k_full = jnp.concatenate([k_nope, k_rope], axis=-1).transpose(0, 2, 1, 3)
        o_a = causal_block_attention(q_full, k_full, v_a.transpose(0, 2, 1, 3),
                                     (MLA_NOPE + MLA_ROPE) ** -0.5)
        y_a = o_a.transpose(0, 2, 1, 3).reshape(B, S, MLA_WIDTH) @ w_proj_mla[l]

        rq = rotary(rq.reshape(B, S, RET_HEADS, RET_DK), positions)
        rk = rotary(rk.reshape(B, S, RET_HEADS, RET_DK), positions) * (RET_DK ** -0.5)
        rv = rv.reshape(B, S, RET_HEADS, RET_DV)
        y_ret = head_group_norm(retention_chunkwise(rq, rk, rv), g_ret[l])
        y_r = (jax.nn.silu(rg) * y_ret) @ w_proj_ret[l]

        g_a, g_r = jnp.split(jax.nn.sigmoid(gates + b_gate[l]), 2, axis=-1)
        h = h + (g_a * y_a + g_r * y_r) @ w_out[l]

        xq = (rms_norm(h, g_cross[l]) @ w_xq[l]).reshape(B, S, X_HEADS, X_HEAD_DIM)
        mkv = (rms_norm(mem, g_mem[l]) @ w_xkv[l]).reshape(B, M, 2, X_HEADS, X_HEAD_DIM)
        mk, mv = mkv[:, :, 0], mkv[:, :, 1]
        s = jnp.einsum('bshd,bmhd->bhsm', xq, mk).astype(jnp.float32) * (X_HEAD_DIM ** -0.5)
        p = jax.nn.softmax(s, axis=-1).astype(mv.dtype)
        xo = jnp.einsum('bhsm,bmhd->bshd', p, mv).reshape(B, S, D)
        h = h + xo @ w_xo[l]

        up = causal_dwconv(rms_norm(h, g_ffn[l]) @ w_up[l], w_conv[l], b_conv[l])
        a, b = up[..., :D_FF], up[..., D_FF:]
        h = h + (jax.nn.silu(a) * b) @ w_down[l]
    return rms_norm(h, g_final)


import jax as _jax
import jax.numpy as _jnp

TWIN_FORMAT = 'train_step'
FWD_PARAMS = ['x', 'mem', 'positions', 'g_mix', 'w_in', 'b_gate', 'g_q_lat', 'w_uq', 'g_kv_lat', 'w_ukv', 'w_proj_mla', 'g_ret', 'w_proj_ret', 'w_out', 'g_cross', 'g_mem', 'w_xq', 'w_xkv', 'w_xo', 'g_ffn', 'w_up', 'w_conv', 'b_conv', 'w_down', 'g_final']
TWIN_WEIGHTS = ['g_mix', 'w_in', 'b_gate', 'g_q_lat', 'w_uq', 'g_kv_lat', 'w_ukv', 'w_proj_mla', 'g_ret', 'w_proj_ret', 'w_out', 'g_cross', 'g_mem', 'w_xq', 'w_xkv', 'w_xo', 'g_ffn', 'w_up', 'w_conv', 'b_conv', 'w_down', 'g_final']
TWIN_DIFF_INPUT = 'x'
TWIN_INPUTS = ['x', 'mem', 'positions', 'g_mix', 'w_in', 'b_gate', 'g_q_lat', 'w_uq', 'g_kv_lat', 'w_ukv', 'w_proj_mla', 'g_ret', 'w_proj_ret', 'w_out', 'g_cross', 'g_mem', 'w_xq', 'w_xkv', 'w_xo', 'g_ffn', 'w_up', 'w_conv', 'b_conv', 'w_down', 'g_final', 'loss_target', 'm_g_mix', 'm_w_in', 'm_b_gate', 'm_g_q_lat', 'm_w_uq', 'm_g_kv_lat', 'm_w_ukv', 'm_w_proj_mla', 'm_g_ret', 'm_w_proj_ret', 'm_w_out', 'm_g_cross', 'm_g_mem', 'm_w_xq', 'm_w_xkv', 'm_w_xo', 'm_g_ffn', 'm_w_up', 'm_w_conv', 'm_b_conv', 'm_w_down', 'm_g_final', 'v_g_mix', 'v_w_in', 'v_b_gate', 'v_g_q_lat', 'v_w_uq', 'v_g_kv_lat', 'v_w_ukv', 'v_w_proj_mla', 'v_g_ret', 'v_w_proj_ret', 'v_w_out', 'v_g_cross', 'v_g_mem', 'v_w_xq', 'v_w_xkv', 'v_w_xo', 'v_g_ffn', 'v_w_up', 'v_w_conv', 'v_b_conv', 'v_w_down', 'v_g_final']
TWIN_OUTPUTS = ['loss', 'grad_x', 'grad_g_mix', 'grad_w_in', 'grad_b_gate', 'grad_g_q_lat', 'grad_w_uq', 'grad_g_kv_lat', 'grad_w_ukv', 'grad_w_proj_mla', 'grad_g_ret', 'grad_w_proj_ret', 'grad_w_out', 'grad_g_cross', 'grad_g_mem', 'grad_w_xq', 'grad_w_xkv', 'grad_w_xo', 'grad_g_ffn', 'grad_w_up', 'grad_w_conv', 'grad_b_conv', 'grad_w_down', 'grad_g_final', 'delta_g_mix', 'delta_w_in', 'delta_b_gate', 'delta_g_q_lat', 'delta_w_uq', 'delta_g_kv_lat', 'delta_w_ukv', 'delta_w_proj_mla', 'delta_g_ret', 'delta_w_proj_ret', 'delta_w_out', 'delta_g_cross', 'delta_g_mem', 'delta_w_xq', 'delta_w_xkv', 'delta_w_xo', 'delta_g_ffn', 'delta_w_up', 'delta_w_conv', 'delta_b_conv', 'delta_w_down', 'delta_g_final', 'new_m_g_mix', 'new_m_w_in', 'new_m_b_gate', 'new_m_g_q_lat', 'new_m_w_uq', 'new_m_g_kv_lat', 'new_m_w_ukv', 'new_m_w_proj_mla', 'new_m_g_ret', 'new_m_w_proj_ret', 'new_m_w_out', 'new_m_g_cross', 'new_m_g_mem', 'new_m_w_xq', 'new_m_w_xkv', 'new_m_w_xo', 'new_m_g_ffn', 'new_m_w_up', 'new_m_w_conv', 'new_m_b_conv', 'new_m_w_down', 'new_m_g_final', 'new_v_g_mix', 'new_v_w_in', 'new_v_b_gate', 'new_v_g_q_lat', 'new_v_w_uq', 'new_v_g_kv_lat', 'new_v_w_ukv', 'new_v_w_proj_mla', 'new_v_g_ret', 'new_v_w_proj_ret', 'new_v_w_out', 'new_v_g_cross', 'new_v_g_mem', 'new_v_w_xq', 'new_v_w_xkv', 'new_v_w_xo', 'new_v_g_ffn', 'new_v_w_up', 'new_v_w_conv', 'new_v_b_conv', 'new_v_w_down', 'new_v_g_final']
TWIN_LEAF_KINDS = {'loss': 'loss', 'grad_x': 'grad_x', 'grad_g_mix': 'grad_w', 'grad_w_in': 'grad_w', 'grad_b_gate': 'grad_w', 'grad_g_q_lat': 'grad_w', 'grad_w_uq': 'grad_w', 'grad_g_kv_lat': 'grad_w', 'grad_w_ukv': 'grad_w', 'grad_w_proj_mla': 'grad_w', 'grad_g_ret': 'grad_w', 'grad_w_proj_ret': 'grad_w', 'grad_w_out': 'grad_w', 'grad_g_cross': 'grad_w', 'grad_g_mem': 'grad_w', 'grad_w_xq': 'grad_w', 'grad_w_xkv': 'grad_w', 'grad_w_xo': 'grad_w', 'grad_g_ffn': 'grad_w', 'grad_w_up': 'grad_w', 'grad_w_conv': 'grad_w', 'grad_b_conv': 'grad_w', 'grad_w_down': 'grad_w', 'grad_g_final': 'grad_w', 'delta_g_mix': 'delta_w', 'delta_w_in': 'delta_w', 'delta_b_gate': 'delta_w', 'delta_g_q_lat': 'delta_w', 'delta_w_uq': 'delta_w', 'delta_g_kv_lat': 'delta_w', 'delta_w_ukv': 'delta_w', 'delta_w_proj_mla': 'delta_w', 'delta_g_ret': 'delta_w', 'delta_w_proj_ret': 'delta_w', 'delta_w_out': 'delta_w', 'delta_g_cross': 'delta_w', 'delta_g_mem': 'delta_w', 'delta_w_xq': 'delta_w', 'delta_w_xkv': 'delta_w', 'delta_w_xo': 'delta_w', 'delta_g_ffn': 'delta_w', 'delta_w_up': 'delta_w', 'delta_w_conv': 'delta_w', 'delta_b_conv': 'delta_w', 'delta_w_down': 'delta_w', 'delta_g_final': 'delta_w', 'new_m_g_mix': 'new_m', 'new_m_w_in': 'new_m', 'new_m_b_gate': 'new_m', 'new_m_g_q_lat': 'new_m', 'new_m_w_uq': 'new_m', 'new_m_g_kv_lat': 'new_m', 'new_m_w_ukv': 'new_m', 'new_m_w_proj_mla': 'new_m', 'new_m_g_ret': 'new_m', 'new_m_w_proj_ret': 'new_m', 'new_m_w_out': 'new_m', 'new_m_g_cross': 'new_m', 'new_m_g_mem': 'new_m', 'new_m_w_xq': 'new_m', 'new_m_w_xkv': 'new_m', 'new_m_w_xo': 'new_m', 'new_m_g_ffn': 'new_m', 'new_m_w_up': 'new_m', 'new_m_w_conv': 'new_m', 'new_m_b_conv': 'new_m', 'new_m_w_down': 'new_m', 'new_m_g_final': 'new_m', 'new_v_g_mix': 'new_v', 'new_v_w_in': 'new_v', 'new_v_b_gate': 'new_v', 'new_v_g_q_lat': 'new_v', 'new_v_w_uq': 'new_v', 'new_v_g_kv_lat': 'new_v', 'new_v_w_ukv': 'new_v', 'new_v_w_proj_mla': 'new_v', 'new_v_g_ret': 'new_v', 'new_v_w_proj_ret': 'new_v', 'new_v_w_out': 'new_v', 'new_v_g_cross': 'new_v', 'new_v_g_mem': 'new_v', 'new_v_w_xq': 'new_v', 'new_v_w_xkv': 'new_v', 'new_v_w_xo': 'new_v', 'new_v_g_ffn': 'new_v', 'new_v_w_up': 'new_v', 'new_v_w_conv': 'new_v', 'new_v_b_conv': 'new_v', 'new_v_w_down': 'new_v', 'new_v_g_final': 'new_v'}


def _forward(args):
    return _fwd_reference(*[args[k] for k in FWD_PARAMS])


def _output_shape():
    def fwd():
        inp = _fwd_setup_inputs(0)
        return _fwd_reference(*[inp[k] for k in FWD_PARAMS])
    out = _jax.eval_shape(fwd)
    return out.shape, out.dtype

N_MICROBATCH = 1
ADAM_LR = 0.001
ADAM_B1 = 0.9
ADAM_B2 = 0.999
ADAM_EPS = 1e-08
ADAM_WD = 0.01
ADAM_STEP = 10
PER_EXAMPLE_BATCH_AXIS = {'x': 0, 'mem': 0, 'positions': 0, 'loss_target': 0}
SHARED_INPUTS = []
_WEIGHT_DTYPES = {'g_mix': _jnp.float32, 'w_in': _jnp.float32, 'b_gate': _jnp.float32, 'g_q_lat': _jnp.float32, 'w_uq': _jnp.float32, 'g_kv_lat': _jnp.float32, 'w_ukv': _jnp.float32, 'w_proj_mla': _jnp.float32, 'g_ret': _jnp.float32, 'w_proj_ret': _jnp.float32, 'w_out': _jnp.float32, 'g_cross': _jnp.float32, 'g_mem': _jnp.float32, 'w_xq': _jnp.float32, 'w_xkv': _jnp.float32, 'w_xo': _jnp.float32, 'g_ffn': _jnp.float32, 'w_up': _jnp.float32, 'w_conv': _jnp.float32, 'b_conv': _jnp.float32, 'w_down': _jnp.float32, 'g_final': _jnp.float32}
MOMENT_SCALE = {'g_mix': 1.877938e-01, 'w_in': 8.849858e-02, 'b_gate': 2.565314e-02, 'g_q_lat': 5.080646e-02, 'w_uq': 3.070515e-02, 'g_kv_lat': 1.326461e-01, 'w_ukv': 3.889917e-02, 'w_proj_mla': 3.136953e-02, 'g_ret': 1.193723e-01, 'w_proj_ret': 8.769227e-02, 'w_out': 9.170557e-02, 'g_cross': 3.041439e-02, 'g_mem': 4.228031e-02, 'w_xq': 2.860313e-02, 'w_xkv': 2.878374e-02, 'w_xo': 2.923095e-02, 'g_ffn': 1.840108e-01, 'w_up': 7.741735e-02, 'w_conv': 7.765780e-02, 'b_conv': 7.867990e-02, 'w_down': 1.288778e-01, 'g_final': 6.395020e+01}


def _to_microbatches(a, axis):
    t = _jnp.moveaxis(a, axis, 0)
    t = t.reshape((N_MICROBATCH, t.shape[0] // N_MICROBATCH) + t.shape[1:])
    return _jnp.moveaxis(t, 1, axis + 1)


def setup_inputs(seed: int = 0) -> dict:
    inp = _fwd_setup_inputs(seed)
    key = _jax.random.fold_in(_jax.random.key(seed), 7919)
    shape, _ = _output_shape()
    out = dict(inp)
    out["loss_target"] = _jax.random.normal(_jax.random.fold_in(key, 0), shape, _jnp.float32)
    for i, name in enumerate(TWIN_WEIGHTS):
        w = inp[name].astype(_jnp.float32)
        if MOMENT_SCALE is None:
            s = _jnp.sqrt(_jnp.mean(_jnp.square(w)) + 1e-30)
        else:
            s = MOMENT_SCALE[name]
        km, kv = _jax.random.split(_jax.random.fold_in(key, i + 1))
        out[name] = w
        out["m_" + name] = s * _jax.random.normal(km, w.shape, _jnp.float32)
        out["v_" + name] = (s * s) * _jax.random.uniform(kv, w.shape, _jnp.float32, 0.5, 1.5)
    if N_MICROBATCH > 1:
        for name, axis in PER_EXAMPLE_BATCH_AXIS.items():
            out[name] = _to_microbatches(out[name], axis)
    return {'x': out['x'], 'mem': out['mem'], 'positions': out['positions'], 'g_mix': out['g_mix'], 'w_in': out['w_in'], 'b_gate': out['b_gate'], 'g_q_lat': out['g_q_lat'], 'w_uq': out['w_uq'], 'g_kv_lat': out['g_kv_lat'], 'w_ukv': out['w_ukv'], 'w_proj_mla': out['w_proj_mla'], 'g_ret': out['g_ret'], 'w_proj_ret': out['w_proj_ret'], 'w_out': out['w_out'], 'g_cross': out['g_cross'], 'g_mem': out['g_mem'], 'w_xq': out['w_xq'], 'w_xkv': out['w_xkv'], 'w_xo': out['w_xo'], 'g_ffn': out['g_ffn'], 'w_up': out['w_up'], 'w_conv': out['w_conv'], 'b_conv': out['b_conv'], 'w_down': out['w_down'], 'g_final': out['g_final'], 'loss_target': out['loss_target'], 'm_g_mix': out['m_g_mix'], 'm_w_in': out['m_w_in'], 'm_b_gate': out['m_b_gate'], 'm_g_q_lat': out['m_g_q_lat'], 'm_w_uq': out['m_w_uq'], 'm_g_kv_lat': out['m_g_kv_lat'], 'm_w_ukv': out['m_w_ukv'], 'm_w_proj_mla': out['m_w_proj_mla'], 'm_g_ret': out['m_g_ret'], 'm_w_proj_ret': out['m_w_proj_ret'], 'm_w_out': out['m_w_out'], 'm_g_cross': out['m_g_cross'], 'm_g_mem': out['m_g_mem'], 'm_w_xq': out['m_w_xq'], 'm_w_xkv': out['m_w_xkv'], 'm_w_xo': out['m_w_xo'], 'm_g_ffn': out['m_g_ffn'], 'm_w_up': out['m_w_up'], 'm_w_conv': out['m_w_conv'], 'm_b_conv': out['m_b_conv'], 'm_w_down': out['m_w_down'], 'm_g_final': out['m_g_final'], 'v_g_mix': out['v_g_mix'], 'v_w_in': out['v_w_in'], 'v_b_gate': out['v_b_gate'], 'v_g_q_lat': out['v_g_q_lat'], 'v_w_uq': out['v_w_uq'], 'v_g_kv_lat': out['v_g_kv_lat'], 'v_w_ukv': out['v_w_ukv'], 'v_w_proj_mla': out['v_w_proj_mla'], 'v_g_ret': out['v_g_ret'], 'v_w_proj_ret': out['v_w_proj_ret'], 'v_w_out': out['v_w_out'], 'v_g_cross': out['v_g_cross'], 'v_g_mem': out['v_g_mem'], 'v_w_xq': out['v_w_xq'], 'v_w_xkv': out['v_w_xkv'], 'v_w_xo': out['v_w_xo'], 'v_g_ffn': out['v_g_ffn'], 'v_w_up': out['v_w_up'], 'v_w_conv': out['v_w_conv'], 'v_b_conv': out['v_b_conv'], 'v_w_down': out['v_w_down'], 'v_g_final': out['v_g_final']}


def _loss(weights, diff, rest, loss_target):
    with _jax.named_scope("forward"):
        args = {**rest, TWIN_DIFF_INPUT: diff, **{k: w.astype(_WEIGHT_DTYPES[k]) for k, w in weights.items()}}
        y = _forward(args)
    with _jax.named_scope("loss_head"):
        err = _jnp.square(y.astype(_jnp.float32) - loss_target)
        return 0.5 * _jnp.sum(_jnp.mean(err, axis=-1)) if err.ndim else 0.5 * err


def _adamw(w, g, m, v):
    m = ADAM_B1 * m + (1.0 - ADAM_B1) * g
    v = ADAM_B2 * v + (1.0 - ADAM_B2) * _jnp.square(g)
    m_hat = m / (1.0 - ADAM_B1 ** ADAM_STEP)
    v_hat = v / (1.0 - ADAM_B2 ** ADAM_STEP)
    delta = -ADAM_LR * (m_hat / (_jnp.sqrt(v_hat) + ADAM_EPS) + ADAM_WD * w)
    return delta, m, v


def reference(x, mem, positions, g_mix, w_in, b_gate, g_q_lat, w_uq, g_kv_lat, w_ukv, w_proj_mla, g_ret, w_proj_ret, w_out, g_cross, g_mem, w_xq, w_xkv, w_xo, g_ffn, w_up, w_conv, b_conv, w_down, g_final, loss_target, m_g_mix, m_w_in, m_b_gate, m_g_q_lat, m_w_uq, m_g_kv_lat, m_w_ukv, m_w_proj_mla, m_g_ret, m_w_proj_ret, m_w_out, m_g_cross, m_g_mem, m_w_xq, m_w_xkv, m_w_xo, m_g_ffn, m_w_up, m_w_conv, m_b_conv, m_w_down, m_g_final, v_g_mix, v_w_in, v_b_gate, v_g_q_lat, v_w_uq, v_g_kv_lat, v_w_ukv, v_w_proj_mla, v_g_ret, v_w_proj_ret, v_w_out, v_g_cross, v_g_mem, v_w_xq, v_w_xkv, v_w_xo, v_g_ffn, v_w_up, v_w_conv, v_b_conv, v_w_down, v_g_final):
    given = dict(x=x, mem=mem, positions=positions, g_mix=g_mix, w_in=w_in, b_gate=b_gate, g_q_lat=g_q_lat, w_uq=w_uq, g_kv_lat=g_kv_lat, w_ukv=w_ukv, w_proj_mla=w_proj_mla, g_ret=g_ret, w_proj_ret=w_proj_ret, w_out=w_out, g_cross=g_cross, g_mem=g_mem, w_xq=w_xq, w_xkv=w_xkv, w_xo=w_xo, g_ffn=g_ffn, w_up=w_up, w_conv=w_conv, b_conv=b_conv, w_down=w_down, g_final=g_final, loss_target=loss_target, m_g_mix=m_g_mix, m_w_in=m_w_in, m_b_gate=m_b_gate, m_g_q_lat=m_g_q_lat, m_w_uq=m_w_uq, m_g_kv_lat=m_g_kv_lat, m_w_ukv=m_w_ukv, m_w_proj_mla=m_w_proj_mla, m_g_ret=m_g_ret, m_w_proj_ret=m_w_proj_ret, m_w_out=m_w_out, m_g_cross=m_g_cross, m_g_mem=m_g_mem, m_w_xq=m_w_xq, m_w_xkv=m_w_xkv, m_w_xo=m_w_xo, m_g_ffn=m_g_ffn, m_w_up=m_w_up, m_w_conv=m_w_conv, m_b_conv=m_b_conv, m_w_down=m_w_down, m_g_final=m_g_final, v_g_mix=v_g_mix, v_w_in=v_w_in, v_b_gate=v_b_gate, v_g_q_lat=v_g_q_lat, v_w_uq=v_w_uq, v_g_kv_lat=v_g_kv_lat, v_w_ukv=v_w_ukv, v_w_proj_mla=v_w_proj_mla, v_g_ret=v_g_ret, v_w_proj_ret=v_w_proj_ret, v_w_out=v_w_out, v_g_cross=v_g_cross, v_g_mem=v_g_mem, v_w_xq=v_w_xq, v_w_xkv=v_w_xkv, v_w_xo=v_w_xo, v_g_ffn=v_g_ffn, v_w_up=v_w_up, v_w_conv=v_w_conv, v_b_conv=v_b_conv, v_w_down=v_w_down, v_g_final=v_g_final)
    weights = {n: given[n] for n in TWIN_WEIGHTS}
    shared = {n: given[n] for n in SHARED_INPUTS}
    per_example = {n: given[n] for n in ['x', 'mem', 'positions']}
    grad_fn = _jax.value_and_grad(_loss, argnums=(0, 1))

    def one_microbatch(ex, loss_target):
        ex = dict(ex)
        diff = ex.pop(TWIN_DIFF_INPUT)
        return grad_fn(weights, diff, {**shared, **ex}, loss_target)

    if N_MICROBATCH == 1:
        loss, (grad_w, grad_x) = one_microbatch(per_example, given["loss_target"])
    else:
        def body(carry, xs):
            loss_sum, grad_sum = carry
            l_k, (gw_k, gx_k) = one_microbatch(xs[0], xs[1])
            with _jax.named_scope("update"):
                return (loss_sum + l_k, _jax.tree.map(_jnp.add, grad_sum, gw_k)), gx_k

        init = (_jnp.zeros((), _jnp.float32), _jax.tree.map(_jnp.zeros_like, weights))
        (loss, grad_w), grad_x = _jax.lax.scan(body, init, (per_example, given["loss_target"]))
    with _jax.named_scope("update"):
        delta_w, new_m, new_v = {}, {}, {}
        for n in TWIN_WEIGHTS:
            delta_w[n], new_m[n], new_v[n] = _adamw(weights[n], grad_w[n], given["m_" + n], given["v_" + n])
    return (loss, grad_x, *[grad_w[n] for n in TWIN_WEIGHTS], *[delta_w[n] for n in TWIN_WEIGHTS],
            *[new_m[n] for n in TWIN_WEIGHTS], *[new_v[n] for n in TWIN_WEIGHTS])
```

```python
import functools
import math

import jax
import jax.numpy as jnp
import numpy as np
from jax import lax
from jax.experimental import pallas as pl
from jax.experimental.pallas import tpu as pltpu

F32 = jnp.float32
BF16 = jnp.bfloat16

D = 1024
MLA_H, MLA_NOPE, MLA_ROPE, MLA_V = 8, 64, 32, 64
MLA_QR, MLA_KVR = 256, 128
RET_H, RET_D, RET_C = 4, 128, 128
X_H, X_HD = 4, 256
D_FF = 2816
THETA = 10000.0
EPS = 1e-6
HP = 128
ZW = 4608
N_DEV = 8
LANES = 1024

ADAM_LR, ADAM_B1, ADAM_B2, ADAM_EPS, ADAM_WD, ADAM_STEP = 0.001, 0.9, 0.999, 1e-08, 0.01, 10

VMEM_LIMIT = 56 * 1024 * 1024
MESH = pl.DeviceIdType.MESH


def _cp(n_axes):
    return pltpu.CompilerParams(dimension_semantics=("arbitrary",) * n_axes, vmem_limit_bytes=VMEM_LIMIT)


def _pick(n, cap, mult=128):
    best = None
    for t in range(mult, min(n, cap) + 1, mult):
        if n % t == 0:
            best = t
    return best if best is not None else n


def _dot(a, b):
    return jnp.dot(a, b, preferred_element_type=F32)


def _dot_nt(a, b):
    return lax.dot_general(a, b, (((1,), (1,)), ((), ())), preferred_element_type=F32)


def _dot_tn(a, b):
    return lax.dot_general(a, b, (((0,), (0,)), ((), ())), preferred_element_type=F32)


def matmul(a, b, *, name, tb=False, res=None, out_dtype=F32, tm=1024, tn=512):
    M, K = a.shape
    N = b.shape[0] if tb else b.shape[1]
    tm = _pick(M, tm, 8)
    tn = _pick(N, tn)

    def body(*refs):
        if res is None:
            a_ref, b_ref, o_ref = refs
        else:
            a_ref, b_ref, r_ref, o_ref = refs
        av = a_ref[...].astype(BF16)
        bv = b_ref[...].astype(BF16)
        acc = _dot_nt(av, bv) if tb else _dot(av, bv)
        if res is not None:
            acc = acc + r_ref[...].astype(F32)
        o_ref[...] = acc.astype(o_ref.dtype)

    in_specs = [
        pl.BlockSpec((tm, K), lambda i, j: (i, 0)),
        pl.BlockSpec((tn, K), lambda i, j: (j, 0)) if tb else pl.BlockSpec((K, tn), lambda i, j: (0, j)),
    ]
    args = [a, b]
    if res is not None:
        in_specs.append(pl.BlockSpec((tm, tn), lambda i, j: (i, j)))
        args.append(res)
    return pl.pallas_call(
        body,
        name=name,
        grid=(M // tm, N // tn),
        in_specs=in_specs,
        out_specs=pl.BlockSpec((tm, tn), lambda i, j: (i, j)),
        out_shape=jax.ShapeDtypeStruct((M, N), out_dtype),
        compiler_params=_cp(2),
    )(*args)


def matmul_tn(a, b, *, name, tm=1024, tn=1024, tk=1024):
    R, M = a.shape
    N = b.shape[1]
    tm = _pick(M, tm)
    tn = _pick(N, tn)
    tk = _pick(R, tk, 16)
    nk = R // tk

    def body(a_ref, b_ref, o_ref, acc_ref):
        k = pl.program_id(2)

        @pl.when(k == 0)
        def _():
            acc_ref[...] = jnp.zeros_like(acc_ref)

        acc_ref[...] += _dot_tn(a_ref[...].astype(BF16), b_ref[...].astype(BF16))

        @pl.when(k == nk - 1)
        def _():
            o_ref[...] = acc_ref[...]

    return pl.pallas_call(
        body,
        name=name,
        grid=(M // tm, N // tn, nk),
        in_specs=[pl.BlockSpec((tk, tm), lambda i, j, k: (k, i)), pl.BlockSpec((tk, tn), lambda i, j, k: (k, j))],
        out_specs=pl.BlockSpec((tm, tn), lambda i, j, k: (i, j)),
        out_shape=jax.ShapeDtypeStruct((M, N), F32),
        scratch_shapes=[pltpu.VMEM((tm, tn), F32)],
        compiler_params=_cp(3),
    )(a, b)


def rowwise(fn, rows, consts, out_rows, out_accs, *, tile, name):
    T = rows[0][0].shape[0]
    nt = T // tile
    n_r, n_c, n_o, n_a = len(rows), len(consts), len(out_rows), len(out_accs)

    def body(*refs):
        ins = [r[...] for r in refs[: n_r + n_c]]
        outs = fn(*ins)
        if not isinstance(outs, (tuple, list)):
            outs = (outs,)
        o_refs = refs[n_r + n_c : n_r + n_c + n_o]
        a_refs = refs[n_r + n_c + n_o :]
        for o_ref, o in zip(o_refs, outs[:n_o]):
            o_ref[...] = o.astype(o_ref.dtype)
        if n_a:
            first = pl.program_id(0) == 0

            @pl.when(first)
            def _():
                for a_ref, o in zip(a_refs, outs[n_o:]):
                    a_ref[...] = o.astype(a_ref.dtype)

            @pl.when(jnp.logical_not(first))
            def _():
                for a_ref, o in zip(a_refs, outs[n_o:]):
                    a_ref[...] += o.astype(a_ref.dtype)

    in_specs = []
    args = []
    for arr, win in rows:
        if win is None:
            in_specs.append(pl.BlockSpec((tile, arr.shape[1]), lambda i: (i, 0)))
        else:
            w, cb = win
            in_specs.append(pl.BlockSpec((tile, w), functools.partial(lambda i, cb: (i, cb), cb=cb)))
        args.append(arr)
    for c in consts:
        in_specs.append(pl.BlockSpec(c.shape, functools.partial(lambda i, nd: (0,) * nd, nd=c.ndim)))
        args.append(c)
    out_specs = [pl.BlockSpec((tile, w), lambda i: (i, 0)) for w, _ in out_rows]
    out_shape = [jax.ShapeDtypeStruct((T, w), dt) for w, dt in out_rows]
    for shp, dt in out_accs:
        out_specs.append(pl.BlockSpec(shp, functools.partial(lambda i, nd: (0,) * nd, nd=len(shp))))
        out_shape.append(jax.ShapeDtypeStruct(shp, dt))
    return pl.pallas_call(
        body,
        name=name,
        grid=(nt,),
        in_specs=in_specs,
        out_specs=out_specs,
        out_shape=out_shape,
        compiler_params=_cp(1),
    )(*args)


def _rms(x, g):
    r = lax.rsqrt(jnp.mean(x * x, axis=-1, keepdims=True) + EPS)
    return x * r * g


def _rms_bwd(x, g, dy):
    r = lax.rsqrt(jnp.mean(x * x, axis=-1, keepdims=True) + EPS)
    xh = x * r
    dg = jnp.sum(dy * xh, axis=0, keepdims=True)
    dxh = dy * g
    dx = r * (dxh - xh * jnp.mean(dxh * xh, axis=-1, keepdims=True))
    return dx, dg


def _sigmoid(x):
    return 1.0 / (1.0 + jnp.exp(-x))


def _rot_mla(x, c, s1, s2):
    n = x.shape[1] // HP
    outs = []
    for h in range(n):
        xh = x[:, h * HP : (h + 1) * HP]
        outs.append(xh * c + pltpu.roll(xh, HP - 16, 1) * s1 + pltpu.roll(xh, 16, 1) * s2)
    return outs[0] if n == 1 else jnp.concatenate(outs, axis=1)


def _rot_ret(x, c, s):
    n = x.shape[1] // RET_D
    outs = []
    for h in range(n):
        xh = x[:, h * RET_D : (h + 1) * RET_D]
        outs.append(xh * c + pltpu.roll(xh, RET_D // 2, 1) * s)
    return outs[0] if n == 1 else jnp.concatenate(outs, axis=1)


def rot_tables(pos_f, *, tile):
    lane_np = np.arange(128)
    inv_m = (jnp.asarray(THETA, F32) ** (-jnp.asarray(lane_np & 15, F32) / 16.0)).reshape(1, 128)
    inv_r = (jnp.asarray(THETA, F32) ** (-jnp.asarray(lane_np & 63, F32) / 64.0)).reshape(1, 128)

    def fn(p, im, ir):
        lane = lax.broadcasted_iota(jnp.int32, p.shape, 1)
        ang = p * im
        cm = jnp.where((lane >= 64) & (lane < 96), jnp.cos(ang), 1.0)
        sn = jnp.sin(ang)
        s1 = jnp.where((lane >= 64) & (lane < 80), -sn, 0.0)
        s2 = jnp.where((lane >= 80) & (lane < 96), sn, 0.0)
        angr = p * ir
        snr = jnp.sin(angr)
        return cm, s1, s2, jnp.cos(angr), jnp.where(lane < 64, -snr, snr)

    return rowwise(fn, [(pos_f, None)], [inv_m, inv_r], [(128, F32)] * 5, [], tile=tile, name="rot_tables")


MLA_SCALE = (MLA_NOPE + MLA_ROPE) ** -0.5
NEG = -1e30


def _tri_mask(n, lower_rows_ge_cols=True):
    r = lax.broadcasted_iota(jnp.int32, (n, n), 0)
    c = lax.broadcasted_iota(jnp.int32, (n, n), 1)
    return r >= c if lower_rows_ge_cols else c >= r


def mla_fwd(q, k, v, *, tq):
    T = q.shape[0]
    nq = T // tq

    def body(q_ref, k_ref, v_ref, o_ref, lse_ref, m_sc, l_sc, acc_sc):
        i = pl.program_id(1)
        qv = q_ref[...]
        m_sc[...] = jnp.full(m_sc.shape, NEG, F32)
        l_sc[...] = jnp.zeros(l_sc.shape, F32)
        acc_sc[...] = jnp.zeros(acc_sc.shape, F32)

        def block(j, masked):
            off = pl.multiple_of(j * tq, tq)
            kb = k_ref[pl.ds(off, tq), :]
            vb = v_ref[pl.ds(off, tq), :]
            s = _dot_nt(qv, kb) * MLA_SCALE
            if masked:
                s = jnp.where(_tri_mask(tq), s, NEG)
            m_prev = m_sc[...]
            m_new = jnp.maximum(m_prev, jnp.max(s, axis=-1, keepdims=True))
            alpha = jnp.exp(m_prev - m_new)
            p = jnp.exp(s - m_new)
            l_sc[...] = alpha * l_sc[...] + jnp.sum(p, axis=-1, keepdims=True)
            acc_sc[...] = alpha * acc_sc[...] + _dot(p.astype(BF16), vb)
            m_sc[...] = m_new

        def loop_body(j, carry):
            block(j, False)
            return carry

        lax.fori_loop(0, i, loop_body, 0)
        block(i, True)
        l = l_sc[...]
        o_ref[...] = (acc_sc[...] / l).astype(o_ref.dtype)
        lse_ref[...] = jnp.broadcast_to(m_sc[...] + jnp.log(l), lse_ref.shape)

    return pl.pallas_call(
        body,
        name="mla_fwd",
        grid=(MLA_H, nq),
        in_specs=[
            pl.BlockSpec((tq, HP), lambda h, i: (i, h)),
            pl.BlockSpec((T, HP), lambda h, i: (0, h)),
            pl.BlockSpec((T, HP), lambda h, i: (0, h)),
        ],
        out_specs=[pl.BlockSpec((tq, HP), lambda h, i: (i, h)), pl.BlockSpec((tq, HP), lambda h, i: (i, h))],
        out_shape=[jax.ShapeDtypeStruct((T, MLA_H * HP), BF16), jax.ShapeDtypeStruct((T, MLA_H * HP), F32)],
        scratch_shapes=[pltpu.VMEM((tq, 1), F32), pltpu.VMEM((tq, 1), F32), pltpu.VMEM((tq, HP), F32)],
        compiler_params=_cp(2),
    )(q, k, v)


def mla_prep(o, do, lse, *, tq):
    T = o.shape[0]
    nq = T // tq

    def body(o_ref, do_ref, lse_ref, delta_ref, lse_row_ref, delta_row_ref):
        d = jnp.sum(o_ref[...].astype(F32) * do_ref[...].astype(F32), axis=-1, keepdims=True)
        db = jnp.broadcast_to(d, (tq, HP))
        delta_ref[...] = db
        lse_row_ref[...] = lse_ref[...].T[0:1, :]
        delta_row_ref[...] = db.T[0:1, :]

    blk = pl.BlockSpec((tq, HP), lambda h, i: (i, h))
    row = pl.BlockSpec((None, None, 1, tq), lambda h, i: (h, i, 0, 0))
    return pl.pallas_call(
        body,
        name="mla_prep",
        grid=(MLA_H, nq),
        in_specs=[blk, blk, blk],
        out_specs=[blk, row, row],
        out_shape=[
            jax.ShapeDtypeStruct((T, MLA_H * HP), F32),
            jax.ShapeDtypeStruct((MLA_H, nq, 1, tq), F32),
            jax.ShapeDtypeStruct((MLA_H, nq, 1, tq), F32),
        ],
        compiler_params=_cp(2),
    )(o, do, lse)


def mla_dq(q, k, v, do, lse, delta, *, tq):
    T = q.shape[0]
    nq = T // tq

    def body(q_ref, k_ref, v_ref, do_ref, lse_ref, delta_ref, dq_ref, acc_sc):
        i = pl.program_id(1)
        qv = q_ref[...]
        dov = do_ref[...]
        lse_c = lse_ref[:, 0:1]
        delta_c = delta_ref[:, 0:1]
        acc_sc[...] = jnp.zeros(acc_sc.shape, F32)

        def block(j, masked):
            off = pl.multiple_of(j * tq, tq)
            kb = k_ref[pl.ds(off, tq), :]
            vb = v_ref[pl.ds(off, tq), :]
            s = _dot_nt(qv, kb) * MLA_SCALE
            if masked:
                s = jnp.where(_tri_mask(tq), s, NEG)
            p = jnp.exp(s - lse_c)
            dp = _dot_nt(dov, vb)
            ds = p * (dp - delta_c) * MLA_SCALE
            acc_sc[...] += _dot(ds.astype(BF16), kb)

        def loop_body(j, carry):
            block(j, False)
            return carry

        lax.fori_loop(0, i, loop_body, 0)
        block(i, True)
        dq_ref[...] = acc_sc[...].astype(dq_ref.dtype)

    blk = pl.BlockSpec((tq, HP), lambda h, i: (i, h))
    full = pl.BlockSpec((T, HP), lambda h, i: (0, h))
    return pl.pallas_call(
        body,
        name="mla_dq",
        grid=(MLA_H, nq),
        in_specs=[blk, full, full, blk, blk, blk],
        out_specs=blk,
        out_shape=jax.ShapeDtypeStruct((T, MLA_H * HP), BF16),
        scratch_shapes=[pltpu.VMEM((tq, HP), F32)],
        compiler_params=_cp(2),
    )(q, k, v, do, lse, delta)


def mla_dkv(q, k, v, do, lse_row, delta_row, *, tq):
    T = q.shape[0]
    nq = T // tq

    def body(q_ref, k_ref, v_ref, do_ref, lse_ref, delta_ref, dk_ref, dv_ref, dk_sc, dv_sc):
        j = pl.program_id(1)
        kv = k_ref[...]
        vv = v_ref[...]
        dk_sc[...] = jnp.zeros(dk_sc.shape, F32)
        dv_sc[...] = jnp.zeros(dv_sc.shape, F32)

        def block(i, masked):
            off = pl.multiple_of(i * tq, tq)
            qb = q_ref[pl.ds(off, tq), :]
            dob = do_ref[pl.ds(off, tq), :]
            st = _dot_nt(kv, qb) * MLA_SCALE
            if masked:
                st = jnp.where(_tri_mask(tq, False), st, NEG)
            pt = jnp.exp(st - lse_ref[i])
            dv_sc[...] += _dot(pt.astype(BF16), dob)
            dpt = _dot_nt(vv, dob)
            dst = pt * (dpt - delta_ref[i]) * MLA_SCALE
            dk_sc[...] += _dot(dst.astype(BF16), qb)

        block(j, True)

        def loop_body(i, carry):
            block(i, False)
            return carry

        lax.fori_loop(j + 1, nq, loop_body, 0)
        dk_ref[...] = dk_sc[...].astype(dk_ref.dtype)
        dv_ref[...] = dv_sc[...].astype(dv_ref.dtype)

    blk = pl.BlockSpec((tq, HP), lambda h, j: (j, h))
    full = pl.BlockSpec((T, HP), lambda h, j: (0, h))
    rows = pl.BlockSpec((None, nq, 1, tq), lambda h, j: (h, 0, 0, 0))
    return pl.pallas_call(
        body,
        name="mla_dkv",
        grid=(MLA_H, nq),
        in_specs=[full, blk, blk, full, rows, rows],
        out_specs=[blk, blk],
        out_shape=[jax.ShapeDtypeStruct((T, MLA_H * HP), BF16)] * 2,
        scratch_shapes=[pltpu.VMEM((tq, HP), F32), pltpu.VMEM((tq, HP), F32)],
        compiler_params=_cp(2),
    )(q, k, v, do, lse_row, delta_row)


def _ret_consts():
    h = jnp.arange(RET_H, dtype=F32)
    log_g = jnp.log1p(-jnp.exp2(-5.0 - h))
    idx = jnp.arange(RET_C, dtype=F32)
    rel = idx[:, None] - idx[None, :]
    dmask = jnp.where(rel >= 0, jnp.exp(log_g[:, None, None] * jnp.maximum(rel, 0.0)), 0.0)
    zeta = jnp.exp(log_g[:, None] * (RET_C - 1.0 - idx)[None, :])
    xi = jnp.exp(log_g[:, None] * (idx + 1.0)[None, :])
    decay = jnp.exp(log_g * RET_C)
    zb = jnp.broadcast_to(zeta[:, :, None], (RET_H, RET_C, RET_D))
    xb = jnp.broadcast_to(xi[:, :, None], (RET_H, RET_C, RET_D))
    db = jnp.broadcast_to(decay[:, None, None], (RET_H, RET_C, RET_D))
    return dmask.astype(F32), zb.astype(F32), xb.astype(F32), db.astype(F32)


def ret_fwd(rq, rk, rv, consts, *, rb):
    T = rq.shape[0]
    nb = T // rb
    ncb = rb // RET_C

    def body(q_ref, k_ref, v_ref, dm_ref, z_ref, x_ref, dc_ref, o_ref, st_ref, r_sc):
        @pl.when(pl.program_id(1) == 0)
        def _():
            r_sc[...] = jnp.zeros(r_sc.shape, F32)

        dm, zt, xi, dc = dm_ref[...], z_ref[...], x_ref[...], dc_ref[...]
        for c in range(ncb):
            sl = slice(c * RET_C, (c + 1) * RET_C)
            q, k, v = q_ref[sl, :], k_ref[sl, :], v_ref[sl, :]
            r = r_sc[...]
            rbf = r.astype(BF16)
            st_ref[sl, :] = rbf
            s = _dot_nt(q, k) * dm
            inner = _dot(s.astype(BF16), v)
            cross = _dot((q.astype(F32) * xi).astype(BF16), rbf)
            o_ref[sl, :] = inner + cross
            kz = (k.astype(F32) * zt).T.astype(BF16)
            r_sc[...] = r * dc + _dot(kz, v)

    blk = pl.BlockSpec((rb, RET_D), lambda h, b: (b, h))
    cst = pl.BlockSpec((None, RET_C, RET_D), lambda h, b: (h, 0, 0))
    return pl.pallas_call(
        body,
        name="ret_fwd",
        grid=(RET_H, nb),
        in_specs=[blk, blk, blk, cst, cst, cst, cst],
        out_specs=[blk, blk],
        out_shape=[jax.ShapeDtypeStruct((T, RET_H * RET_D), F32), jax.ShapeDtypeStruct((T, RET_H * RET_D), BF16)],
        scratch_shapes=[pltpu.VMEM((RET_D, RET_D), F32)],
        compiler_params=_cp(2),
    )(rq, rk, rv, *consts)


def ret_bwd(rq, rk, rv, st, dret, consts, *, rb):
    T = rq.shape[0]
    nb = T // rb
    ncb = rb // RET_C

    def body(q_ref, k_ref, v_ref, st_ref, do_ref, dm_ref, z_ref, x_ref, dc_ref, dq_ref, dk_ref, dv_ref, g_sc):
        @pl.when(pl.program_id(1) == 0)
        def _():
            g_sc[...] = jnp.zeros(g_sc.shape, F32)

        dm, zt, xi, dc = dm_ref[...], z_ref[...], x_ref[...], dc_ref[...]
        for c in reversed(range(ncb)):
            sl = slice(c * RET_C, (c + 1) * RET_C)
            q, k, v, rp = q_ref[sl, :], k_ref[sl, :], v_ref[sl, :], st_ref[sl, :]
            dob = do_ref[sl, :].astype(BF16)
            qf, kf = q.astype(F32), k.astype(F32)
            gn = g_sc[...]
            gnb = gn.astype(BF16)
            s = _dot_nt(q, k) * dm
            ds = _dot_nt(dob, v) * dm
            dq = _dot(ds.astype(BF16), k) + _dot_nt(dob, rp) * xi
            dk = _dot(ds.T.astype(BF16), q) + _dot_nt(v, gnb) * zt
            dv = _dot(s.T.astype(BF16), dob) + _dot((kf * zt).astype(BF16), gnb)
            dq_ref[sl, :] = dq.astype(dq_ref.dtype)
            dk_ref[sl, :] = dk.astype(dk_ref.dtype)
            dv_ref[sl, :] = dv.astype(dv_ref.dtype)
            g_sc[...] = _dot((qf * xi).T.astype(BF16), dob) + dc * gn

    blk = pl.BlockSpec((rb, RET_D), lambda h, b: (nb - 1 - b, h))
    cst = pl.BlockSpec((None, RET_C, RET_D), lambda h, b: (h, 0, 0))
    return pl.pallas_call(
        body,
        name="ret_bwd",
        grid=(RET_H, nb),
        in_specs=[blk, blk, blk, blk, blk, cst, cst, cst, cst],
        out_specs=[blk, blk, blk],
        out_shape=[jax.ShapeDtypeStruct((T, RET_H * RET_D), F32)] * 3,
        scratch_shapes=[pltpu.VMEM((RET_D, RET_D), F32)],
        compiler_params=_cp(2),
    )(rq, rk, rv, st, dret, *consts)


HALO = 16


def conv_act_fwd(up_pre, w_conv, b_conv, *, tile, cw):
    T = up_pre.shape[0]
    nt = T // tile
    ncol = D_FF // cw
    hb = tile // HALO

    def body(pa_ref, a_ref, pb_ref, b_ref, wa_ref, wb_ref, ba_ref, bb_ref, o_ref):
        i = pl.program_id(1)
        keep = (i > 0).astype(F32)

        def conv(prev_ref, cur_ref, w_ref, bias_ref):
            ext = jnp.concatenate([prev_ref[...].astype(F32) * keep, cur_ref[...].astype(F32)], axis=0)
            w = w_ref[...]
            y = ext * w[2:3, :] + pltpu.roll(ext, 1, 0) * w[1:2, :] + pltpu.roll(ext, 2, 0) * w[0:1, :] + bias_ref[...]
            return y[HALO:, :]

        a = conv(pa_ref, a_ref, wa_ref, ba_ref)
        b = conv(pb_ref, b_ref, wb_ref, bb_ref)
        o_ref[...] = (a * _sigmoid(a) * b).astype(o_ref.dtype)

    prev_a = pl.BlockSpec((HALO, cw), lambda j, i: (jnp.maximum(i * hb - 1, 0), j))
    cur_a = pl.BlockSpec((tile, cw), lambda j, i: (i, j))
    prev_b = pl.BlockSpec((HALO, cw), lambda j, i: (jnp.maximum(i * hb - 1, 0), j + ncol))
    cur_b = pl.BlockSpec((tile, cw), lambda j, i: (i, j + ncol))
    w_a = pl.BlockSpec((3, cw), lambda j, i: (0, j))
    w_b = pl.BlockSpec((3, cw), lambda j, i: (0, j + ncol))
    bias_a = pl.BlockSpec((1, cw), lambda j, i: (0, j))
    bias_b = pl.BlockSpec((1, cw), lambda j, i: (0, j + ncol))
    return pl.pallas_call(
        body,
        name="conv_act_fwd",
        grid=(ncol, nt),
        in_specs=[prev_a, cur_a, prev_b, cur_b, w_a, w_b, bias_a, bias_b],
        out_specs=pl.BlockSpec((tile, cw), lambda j, i: (i, j)),
        out_shape=jax.ShapeDtypeStruct((T, D_FF), BF16),
        compiler_params=_cp(2),
    )(up_pre, up_pre, up_pre, up_pre, w_conv, w_conv, b_conv, b_conv)


def conv_act_bwd(up_pre, dact, w_conv, b_conv, *, tile, cw):
    T = up_pre.shape[0]
    nt = T // tile
    ncol = D_FF // cw
    hb = tile // HALO
    ext_rows = tile + 2 * HALO

    def body(pa_ref, a_ref, na_ref, pb_ref, b_ref, nb_ref, d_ref, nd_ref, wa_ref, wb_ref, ba_ref, bb_ref,
             dxa_ref, dxb_ref, sa_ref, sb_ref):
        i = pl.program_id(1)
        keep_p = (i > 0).astype(F32)
        keep_n = (i < nt - 1).astype(F32)

        def ext_of(prev_ref, cur_ref, next_ref):
            return jnp.concatenate(
                [prev_ref[...].astype(F32) * keep_p, cur_ref[...].astype(F32), next_ref[...].astype(F32) * keep_n], axis=0)

        def conv(ext, w, bias):
            return ext * w[2:3, :] + pltpu.roll(ext, 1, 0) * w[1:2, :] + pltpu.roll(ext, 2, 0) * w[0:1, :] + bias

        xa = ext_of(pa_ref, a_ref, na_ref)
        xb = ext_of(pb_ref, b_ref, nb_ref)
        wa, wb = wa_ref[...], wb_ref[...]
        a = conv(xa, wa, ba_ref[...])
        b = conv(xb, wb, bb_ref[...])
        dy = jnp.concatenate(
            [jnp.zeros((HALO, cw), F32), d_ref[...].astype(F32), nd_ref[...].astype(F32) * keep_n], axis=0)
        sg = _sigmoid(a)
        da = dy * b * (sg * (1.0 + a * (1.0 - sg)))
        db = dy * (a * sg)

        def back(dup, ext, w, dx_ref, s_ref):
            dx = dup * w[2:3, :] + pltpu.roll(dup, ext_rows - 1, 0) * w[1:2, :] + pltpu.roll(dup, ext_rows - 2, 0) * w[0:1, :]
            dx_ref[...] = dx[HALO:HALO + tile, :].astype(dx_ref.dtype)
            dc = dup[HALO:HALO + tile, :]
            r2 = jnp.sum(dc * ext[HALO:HALO + tile, :], axis=0, keepdims=True)
            r1 = jnp.sum(dc * pltpu.roll(ext, 1, 0)[HALO:HALO + tile, :], axis=0, keepdims=True)
            r0 = jnp.sum(dc * pltpu.roll(ext, 2, 0)[HALO:HALO + tile, :], axis=0, keepdims=True)
            rb = jnp.sum(dc, axis=0, keepdims=True)
            row = lax.broadcasted_iota(jnp.int32, (8, cw), 0)
            upd = (jnp.where(row == 0, r0, 0.0) + jnp.where(row == 1, r1, 0.0) + jnp.where(row == 2, r2, 0.0)
                   + jnp.where(row == 3, rb, 0.0))

            @pl.when(i == 0)
            def _():
                s_ref[...] = upd

            @pl.when(i > 0)
            def _():
                s_ref[...] += upd

        back(da, xa, wa, dxa_ref, sa_ref)
        back(db, xb, wb, dxb_ref, sb_ref)

    def prev_of(shift):
        return pl.BlockSpec((HALO, cw), lambda j, i: (jnp.maximum(i * hb - 1, 0), j + shift))

    def next_of(shift):
        return pl.BlockSpec((HALO, cw), lambda j, i: (jnp.minimum((i + 1) * hb, nt * hb - 1), j + shift))

    def cur_of(shift):
        return pl.BlockSpec((tile, cw), lambda j, i: (i, j + shift))

    def row_of(rows, shift):
        return pl.BlockSpec((rows, cw), lambda j, i: (0, j + shift))

    return pl.pallas_call(
        body,
        name="conv_act_bwd",
        grid=(ncol, nt),
        in_specs=[prev_of(0), cur_of(0), next_of(0), prev_of(ncol), cur_of(ncol), next_of(ncol), cur_of(0), next_of(0),
                  row_of(3, 0), row_of(3, ncol), row_of(1, 0), row_of(1, ncol)],
        out_specs=[cur_of(0), cur_of(0), row_of(8, 0), row_of(8, 0)],
        out_shape=[jax.ShapeDtypeStruct((T, D_FF), BF16), jax.ShapeDtypeStruct((T, D_FF), BF16),
                   jax.ShapeDtypeStruct((8, D_FF), F32), jax.ShapeDtypeStruct((8, D_FF), F32)],
        compiler_params=_cp(2),
    )(up_pre, up_pre, up_pre, up_pre, up_pre, up_pre, dact, dact, w_conv, w_conv, b_conv, b_conv)


def _peer(m):
    x, y, c = lax.axis_index("x"), lax.axis_index("y"), lax.axis_index("c")
    mx, my, mc = (m >> 2) & 1, (m >> 1) & 1, m & 1
    px = 1 - x if mx else x
    py = 1 - y if my else y
    pc = 1 - c if mc else c
    return (px, py, pc), 4 * px + 2 * py + pc


def all_gather_packed(shard):
    R, W = shard.shape

    def body(src_ref, out_ref, send_sems, recv_sems, local_sem):
        x, y, c = lax.axis_index("x"), lax.axis_index("y"), lax.axis_index("c")
        me = 4 * x + 2 * y + c
        mine = pltpu.make_async_copy(src_ref, out_ref.at[me], local_sem)
        mine.start()
        copies = []
        for m in range(1, N_DEV):
            peer, _ = _peer(m)
            cp = pltpu.make_async_remote_copy(
                src_ref=src_ref, dst_ref=out_ref.at[me], send_sem=send_sems.at[m - 1], recv_sem=recv_sems.at[m - 1],
                device_id=peer, device_id_type=MESH)
            cp.start()
            copies.append(cp)
        for m in range(1, N_DEV):
            _, plin = _peer(m)
            pltpu.make_async_remote_copy(
                src_ref=src_ref, dst_ref=out_ref.at[plin], send_sem=send_sems.at[m - 1], recv_sem=recv_sems.at[m - 1],
                device_id=_peer(m)[0], device_id_type=MESH).wait_recv()
        for cp in copies:
            cp.wait_send()
        mine.wait()

    return pl.pallas_call(
        body,
        name="all_gather_packed",
        in_specs=[pl.BlockSpec(memory_space=pl.ANY)],
        out_specs=pl.BlockSpec(memory_space=pl.ANY),
        out_shape=jax.ShapeDtypeStruct((N_DEV, R, W), shard.dtype),
        scratch_shapes=[pltpu.SemaphoreType.DMA((N_DEV - 1,)), pltpu.SemaphoreType.DMA((N_DEV - 1,)),
                        pltpu.SemaphoreType.DMA],
    )(shard)


def exchange_grads(big, small):
    _, R, W = big.shape
    S = small.shape[0]

    def body(big_ref, small_ref, rb_ref, rs_ref, send_b, recv_b, send_s, recv_s):
        copies = []
        for m in range(1, N_DEV):
            peer, plin = _peer(m)
            cb = pltpu.make_async_remote_copy(
                src_ref=big_ref.at[plin], dst_ref=rb_ref.at[m - 1], send_sem=send_b.at[m - 1], recv_sem=recv_b.at[m - 1],
                device_id=peer, device_id_type=MESH)
            cs = pltpu.make_async_remote_copy(
                src_ref=small_ref, dst_ref=rs_ref.at[m - 1], send_sem=send_s.at[m - 1], recv_sem=recv_s.at[m - 1],
                device_id=peer, device_id_type=MESH)
            cb.start()
            cs.start()
            copies += [cb, cs]
        for cp in copies:
            cp.wait_recv()
        for cp in copies:
            cp.wait_send()

    return pl.pallas_call(
        body,
        name="exchange_grads",
        in_specs=[pl.BlockSpec(memory_space=pl.ANY), pl.BlockSpec(memory_space=pl.ANY)],
        out_specs=[pl.BlockSpec(memory_space=pl.ANY), pl.BlockSpec(memory_space=pl.ANY)],
        out_shape=[jax.ShapeDtypeStruct((N_DEV - 1, R, W), big.dtype), jax.ShapeDtypeStruct((N_DEV - 1, S, W), F32)],
        scratch_shapes=[pltpu.SemaphoreType.DMA((N_DEV - 1,))] * 4,
    )(big, small)


def _adamw(w, g, m, v):
    m = ADAM_B1 * m + (1.0 - ADAM_B1) * g
    v = ADAM_B2 * v + (1.0 - ADAM_B2) * (g * g)
    m_hat = m / (1.0 - ADAM_B1 ** ADAM_STEP)
    v_hat = v / (1.0 - ADAM_B2 ** ADAM_STEP)
    delta = -ADAM_LR * (m_hat / (jnp.sqrt(v_hat) + ADAM_EPS) + ADAM_WD * w)
    return delta, m, v


def reduce_adamw_big(own, recv, w, m, v, *, tile):
    R, W = own.shape

    def body(own_ref, recv_ref, w_ref, m_ref, v_ref, g_out, d_out, m_out, v_out):
        g = own_ref[...]
        for k in range(N_DEV - 1):
            g = g + recv_ref[k].astype(F32)
        d, mn, vn = _adamw(w_ref[...], g, m_ref[...], v_ref[...])
        g_out[...] = g
        d_out[...] = d
        m_out[...] = mn
        v_out[...] = vn

    blk = pl.BlockSpec((tile, W), lambda i: (i, 0))
    return pl.pallas_call(
        body,
        name="reduce_adamw_big",
        grid=(R // tile,),
        in_specs=[blk, pl.BlockSpec((N_DEV - 1, tile, W), lambda i: (0, i, 0)), blk, blk, blk],
        out_specs=[blk, blk, blk, blk],
        out_shape=[jax.ShapeDtypeStruct((R, W), F32)] * 4,
        compiler_params=_cp(1),
    )(own, recv, w, m, v)


def reduce_adamw_small(own, recv, w, m, v):
    S, W = own.shape

    def body(own_ref, recv_ref, w_ref, m_ref, v_ref, g_out, d_out, m_out, v_out, all_sc):
        me = 4 * lax.axis_index("x") + 2 * lax.axis_index("y") + lax.axis_index("c")
        all_sc[0] = own_ref[...]
        for k in range(N_DEV - 1):
            all_sc[k + 1] = recv_ref[k]
        g = all_sc[jnp.bitwise_xor(me, 0)]
        for s in range(1, N_DEV):
            g = g + all_sc[jnp.bitwise_xor(me, s)]
        d, mn, vn = _adamw(w_ref[...], g, m_ref[...], v_ref[...])
        g_out[...] = g
        d_out[...] = d
        m_out[...] = mn
        v_out[...] = vn

    vm = pl.BlockSpec(memory_space=pltpu.VMEM)
    return pl.pallas_call(
        body,
        name="reduce_adamw_small",
        in_specs=[vm] * 5,
        out_specs=[vm] * 4,
        out_shape=[jax.ShapeDtypeStruct((S, W), F32)] * 4,
        scratch_shapes=[pltpu.VMEM((N_DEV, S, W), F32)],
    )(own, recv, w, m, v)


BIG = [
    ("w_in", (1024, 564), "col"), ("w_uq", (256, 96), "col"), ("w_ukv", (128, 128), "col"),
    ("w_proj_mla", (512, 128), "col"), ("w_proj_ret", (512, 128), "col"), ("w_out", (128, 1024), "row"),
    ("w_xq", (128, 1024), "row"), ("w_xkv", (1024, 256), "col"), ("w_xo", (128, 1024), "row"),
    ("w_up", (1024, 704), "col"), ("w_conv", (3, 704), "col"), ("w_down", (352, 1024), "row"),
]
BIG_ROWS = 2432
SMALL = [
    ("g_mix", (1, 1024), 1), ("b_gate", (1, 2048), 2), ("g_q_lat", (1, 256), 1), ("g_kv_lat", (1, 128), 1),
    ("g_ret", (1, 512), 1), ("g_cross", (1, 1024), 1), ("g_mem", (1, 1024), 1), ("g_ffn", (1, 1024), 1),
    ("b_conv", (1, 5632), 6), ("g_final", (1024,), 1),
]
SMALL_ROWS = 16


def _pack_big(parts):
    flat = [parts[n].reshape(-1) for n, _, _ in BIG]
    used = sum(int(np.prod(s)) for _, s, _ in BIG)
    flat.append(jnp.zeros((BIG_ROWS * LANES - used,), flat[0].dtype))
    return jnp.concatenate(flat).reshape(BIG_ROWS, LANES)


def _unpack_big(packed):
    flat = packed.reshape(-1)
    out, off = {}, 0
    for n, s, _ in BIG:
        k = int(np.prod(s))
        out[n] = flat[off:off + k].reshape((1,) + s)
        off += k
    return out


def _gathered_full(gathered):
    flat = gathered.reshape(N_DEV, -1)
    out, off = {}, 0
    for n, s, kind in BIG:
        k = int(np.prod(s))
        piece = flat[:, off:off + k].reshape((N_DEV,) + s)
        if kind == "col":
            out[n] = piece.transpose(1, 0, 2).reshape(s[0], N_DEV * s[1])
        else:
            out[n] = piece.reshape(N_DEV * s[0], s[1])
        off += k
    return out


def _full_to_slots(fulls, dtype):
    flat = []
    for n, s, kind in BIG:
        g = fulls[n].astype(dtype)
        if kind == "col":
            flat.append(g.reshape(s[0], N_DEV, s[1]).transpose(1, 0, 2).reshape(N_DEV, -1))
        else:
            flat.append(g.reshape(N_DEV, -1))
    used = sum(int(np.prod(s)) for _, s, _ in BIG)
    flat.append(jnp.zeros((N_DEV, BIG_ROWS * LANES - used), dtype))
    return jnp.concatenate(flat, axis=1).reshape(N_DEV, BIG_ROWS, LANES)


def _pack_small(parts):
    rows = []
    for n, _, r in SMALL:
        v = parts[n].reshape(-1).astype(F32)
        rows.append(jnp.pad(v, (0, r * LANES - v.shape[0])).reshape(r, LANES))
    return jnp.concatenate(rows, axis=0)


def _unpack_small(packed):
    out, off = {}, 0
    for n, s, r in SMALL:
        k = int(np.prod(s))
        out[n] = packed[off:off + r].reshape(-1)[:k].reshape(s)
        off += r
    return out


def _cast_rows(a, dtype, *, tile, name):
    return rowwise(lambda v: v, [(a, None)], [], [(a.shape[1], dtype)], [], tile=tile, name=name)[0]


def kernel(x, mem, positions, g_mix, w_in, b_gate, g_q_lat, w_uq, g_kv_lat, w_ukv, w_proj_mla, g_ret, w_proj_ret, w_out, g_cross, g_mem, w_xq, w_xkv, w_xo, g_ffn, w_up, w_conv, b_conv, w_down, g_final, loss_target, m_g_mix, m_w_in, m_b_gate, m_g_q_lat, m_w_uq, m_g_kv_lat, m_w_ukv, m_w_proj_mla, m_g_ret, m_w_proj_ret, m_w_out, m_g_cross, m_g_mem, m_w_xq, m_w_xkv, m_w_xo, m_g_ffn, m_w_up, m_w_conv, m_b_conv, m_w_down, m_g_final, v_g_mix, v_w_in, v_b_gate, v_g_q_lat, v_w_uq, v_g_kv_lat, v_w_ukv, v_w_proj_mla, v_g_ret, v_w_proj_ret, v_w_out, v_g_cross, v_g_mem, v_w_xq, v_w_xkv, v_w_xo, v_g_ffn, v_w_up, v_w_conv, v_b_conv, v_w_down, v_g_final):
    args = dict(locals())
    T = x.shape[1]
    M = mem.shape[1]
    tile = min(256, T)
    tq = min(512, T)
    rb = min(1024, T)

    xs = x[0]
    tgt = loss_target[0]
    mems = mem[0]

    w_pack = _pack_big({n: args[n] for n, _, _ in BIG})
    m_pack = _pack_big({n: args["m_" + n] for n, _, _ in BIG})
    v_pack = _pack_big({n: args["v_" + n] for n, _, _ in BIG})
    gathered = all_gather_packed(_cast_rows(w_pack, BF16, tile=128, name="cast_weights"))
    W = _gathered_full(gathered)

    wi = W["w_in"]
    zc = lambda n: jnp.zeros((D, n), BF16)
    wz = jnp.concatenate([wi[:, 416:928], wi[:, 928:1440], wi[:, 1440:1952], wi[:, 1952:2464], wi[:, 2464:4512],
                          wi[:, 0:256], wi[:, 256:384], zc(64), wi[:, 384:416], zc(32)], axis=1)
    wq = jnp.pad(W["w_uq"].reshape(MLA_QR, MLA_H, MLA_NOPE + MLA_ROPE),
                 ((0, 0), (0, 0), (0, HP - MLA_NOPE - MLA_ROPE))).reshape(MLA_QR, MLA_H * HP)
    wkv = W["w_ukv"].reshape(MLA_KVR, MLA_H, MLA_NOPE + MLA_V)
    wk = jnp.pad(wkv[:, :, :MLA_NOPE], ((0, 0), (0, 0), (0, HP - MLA_NOPE))).reshape(MLA_KVR, MLA_H * HP)
    wv = jnp.pad(wkv[:, :, MLA_NOPE:], ((0, 0), (0, 0), (0, HP - MLA_V))).reshape(MLA_KVR, MLA_H * HP)
    wpa = jnp.pad(W["w_proj_mla"].reshape(MLA_H, MLA_V, D), ((0, 0), (0, HP - MLA_V), (0, 0))).reshape(MLA_H * HP, D)
    wpr, wo, wxq, wxkv, wxo = W["w_proj_ret"], W["w_out"], W["w_xq"], W["w_xkv"], W["w_xo"]
    wup, wdn = W["w_up"], W["w_down"]
    wcv = W["w_conv"].astype(F32)

    pos_f = jnp.broadcast_to(positions[0].astype(F32)[:, None], (T, 128))
    cm, s1, s2, cr, sr = rot_tables(pos_f, tile=tile)
    rconsts = _ret_consts()

    u = rowwise(lambda xv, g: _rms(xv, g), [(xs, None)], [g_mix], [(D, BF16)], [], tile=tile, name="norm_mix")[0]
    z = matmul(u, wz, name="mm_z")

    def lat_fwd(zl, cmv, s1v, s2v, gq, gkv, wqv, wkv_, wvv):
        cq = _rms(zl[:, 0:256], gq).astype(BF16)
        ckv = _rms(zl[:, 256:384], gkv).astype(BF16)
        qv = _rot_mla(_dot(cq, wqv), cmv, s1v, s2v)
        kr = _rot_mla(zl[:, 384:512], cmv, s1v, s2v)
        kn = _dot(ckv, wkv_)
        kv_ = jnp.concatenate([kn[:, h * HP:(h + 1) * HP] + kr for h in range(MLA_H)], axis=1)
        vv = _dot(ckv, wvv)
        return qv, kv_, vv

    q_a, k_a, v_a = rowwise(
        lat_fwd, [(z, (512, 8)), (cm, None), (s1, None), (s2, None)], [g_q_lat, g_kv_lat, wq, wk, wv],
        [(MLA_H * HP, BF16)] * 3, [], tile=tile, name="lat_fwd")

    def retprep_fwd(zr, crv, srv):
        rqv = _rot_ret(zr[:, 0:512], crv, srv)
        rkv = _rot_ret(zr[:, 512:1024], crv, srv) * (RET_D ** -0.5)
        return rqv, rkv, zr[:, 1024:1536]

    rq, rk, rv = rowwise(retprep_fwd, [(z, (2048, 0)), (cr, None), (sr, None)], [], [(512, BF16)] * 3, [],
                         tile=tile, name="retprep_fwd")

    o_a, lse = mla_fwd(q_a, k_a, v_a, tq=tq)
    ret, rstate = ret_fwd(rq, rk, rv, rconsts, rb=rb)

    def gn_parts(r):
        outs = []
        for h in range(RET_H):
            rh = r[:, h * RET_D:(h + 1) * RET_D]
            mu = jnp.mean(rh, axis=-1, keepdims=True)
            dlt = rh - mu
            rstd = lax.rsqrt(jnp.mean(dlt * dlt, axis=-1, keepdims=True) + EPS)
            outs.append((dlt * rstd, rstd))
        return outs

    def mix_fwd(ov, rv_, rg, gt, wpav, wprv, gr, bg):
        ya = _dot(ov, wpav)
        xh = jnp.concatenate([p[0] for p in gn_parts(rv_)], axis=1)
        t = rg * _sigmoid(rg) * (xh * gr)
        yr = _dot(t.astype(BF16), wprv)
        ga = _sigmoid(gt[:, :D] + bg[:, :D])
        gr_ = _sigmoid(gt[:, D:] + bg[:, D:])
        return ga * ya + gr_ * yr

    mix = rowwise(mix_fwd, [(o_a, None), (ret, None), (z, (512, 3)), (z, (2048, 1))], [wpa, wpr, g_ret, b_gate],
                  [(D, BF16)], [], tile=tile, name="mix_fwd")[0]
    h1 = matmul(mix, wo, res=xs, name="mm_out")
    n2 = rowwise(lambda hv, g: _rms(hv, g), [(h1, None)], [g_cross], [(D, BF16)], [], tile=tile, name="norm_cross")[0]
    xq = matmul(n2, wxq, out_dtype=BF16, name="mm_xq")
    mn = rowwise(lambda mv_, g: _rms(mv_, g), [(mems, None)], [g_mem], [(D, BF16)], [], tile=min(tile, M), name="norm_mem")[0]
    mkv = matmul(mn, wxkv, out_dtype=BF16, name="mm_mkv")

    x_scale = X_HD ** -0.5

    def xattn_fwd(xqv, mkvv):
        outs = []
        for h in range(X_H):
            sl = slice(h * X_HD, (h + 1) * X_HD)
            s = _dot_nt(xqv[:, sl], mkvv[:, sl]) * x_scale
            s = s - jnp.max(s, axis=-1, keepdims=True)
            e = jnp.exp(s)
            p = e / jnp.sum(e, axis=-1, keepdims=True)
            outs.append(_dot(p.astype(BF16), mkvv[:, D + h * X_HD:D + (h + 1) * X_HD]))
        return jnp.concatenate(outs, axis=1)

    xo = rowwise(xattn_fwd, [(xq, None)], [mkv], [(D, BF16)], [], tile=tile, name="xattn_fwd")[0]
    h2 = matmul(xo, wxo, res=h1, name="mm_xo")
    n3 = rowwise(lambda hv, g: _rms(hv, g), [(h2, None)], [g_ffn], [(D, BF16)], [], tile=tile, name="norm_ffn")[0]
    up_pre = matmul(n3, wup, out_dtype=BF16, name="mm_up")
    cw = D_FF // 2
    act = conv_act_fwd(up_pre, wcv, b_conv, tile=tile, cw=cw)
    h3 = matmul(act, wdn, res=h2, name="mm_down")

    def loss_fn(hv, tv, g):
        y = _rms(hv, g)
        err = y - tv
        part = 0.5 * jnp.sum(jnp.sum(err * err, axis=-1, keepdims=True) / D, axis=0, keepdims=True)
        dx, dg = _rms_bwd(hv, g, err / D)
        return dx, dg, jnp.broadcast_to(part, (8, 128))

    g_fin2 = g_final.reshape(1, D)
    dh3, dg_final, loss_acc = rowwise(loss_fn, [(h3, None), (tgt, None)], [g_fin2], [(D, F32)],
                                      [((1, D), F32), ((8, 128), F32)], tile=tile, name="loss_bwd")
    loss = lax.psum(loss_acc[0, 0], ("x", "y", "c"))

    dact = matmul(dh3, wdn, tb=True, out_dtype=BF16, name="mm_dact")
    dw_down = matmul_tn(act, dh3, name="mm_dw_down", tm=1408)
    dup_a, dup_b, cs_a, cs_b = conv_act_bwd(up_pre, dact, wcv, b_conv, tile=tile, cw=cw)
    dup = jnp.concatenate([dup_a, dup_b], axis=1)
    dw_conv = jnp.concatenate([cs_a[0:3], cs_b[0:3]], axis=1)
    db_conv = jnp.concatenate([cs_a[3:4], cs_b[3:4]], axis=1)
    dn3 = matmul(dup, wup, tb=True, name="mm_dn3", tm=512)
    dw_up = matmul_tn(n3, dup, name="mm_dw_up", tn=1408)

    def norm_bwd(hv, dyv, drv, g):
        dx, dg = _rms_bwd(hv, g, dyv)
        return dx + drv, dg

    dh2, dg_ffn = rowwise(norm_bwd, [(h2, None), (dn3, None), (dh3, None)], [g_ffn], [(D, F32)], [((1, D), F32)],
                          tile=tile, name="norm_ffn_bwd")
    dxo = matmul(dh2, wxo, tb=True, out_dtype=BF16, name="mm_dxo")
    dw_xo = matmul_tn(xo, dh2, name="mm_dw_xo")

    def xattn_bwd(xqv, dxov, mkvv):
        dxq, dmk, dmv = [], [], []
        for h in range(X_H):
            sl = slice(h * X_HD, (h + 1) * X_HD)
            slv = slice(D + h * X_HD, D + (h + 1) * X_HD)
            s = _dot_nt(xqv[:, sl], mkvv[:, sl]) * x_scale
            s = s - jnp.max(s, axis=-1, keepdims=True)
            e = jnp.exp(s)
            p = e / jnp.sum(e, axis=-1, keepdims=True)
            dp = _dot_nt(dxov[:, sl], mkvv[:, slv])
            ds = (p * (dp - jnp.sum(dp * p, axis=-1, keepdims=True)) * x_scale).astype(BF16)
            dxq.append(_dot(ds, mkvv[:, sl]))
            dmk.append(_dot_tn(ds, xqv[:, sl]))
            dmv.append(_dot_tn(p.astype(BF16), dxov[:, sl]))
        return jnp.concatenate(dxq, axis=1), jnp.concatenate(dmk + dmv, axis=1)

    dxq, dmkv = rowwise(xattn_bwd, [(xq, None), (dxo, None)], [mkv], [(D, BF16)], [((M, 2 * D), F32)],
                        tile=tile, name="xattn_bwd")
    dn2 = matmul(dxq, wxq, tb=True, name="mm_dn2")
    dw_xq = matmul_tn(n2, dxq, name="mm_dw_xq")
    dh1, dg_cross = rowwise(norm_bwd, [(h1, None), (dn2, None), (dh2, None)], [g_cross], [(D, F32)], [((1, D), F32)],
                            tile=tile, name="norm_cross_bwd")
    dw_xkv = matmul_tn(mn, dmkv, name="mm_dw_xkv", tk=M)
    dmn = matmul(dmkv, wxkv, tb=True, name="mm_dmn", tm=M)
    dg_mem = rowwise(lambda mv_, dyv, g: _rms_bwd(mv_, g, dyv)[1], [(mems, None), (dmn, None)], [g_mem], [],
                     [((1, D), F32)], tile=min(tile, M), name="norm_mem_bwd")[0]

    dmix = matmul(dh1, wo, tb=True, out_dtype=BF16, name="mm_dmix")
    dw_out = matmul_tn(mix, dh1, name="mm_dw_out")

    def mix_bwd(ov, rv_, rg, gt, dmv_, wpav, wprv, gr, bg):
        dm_ = dmv_.astype(F32)
        ya = _dot(ov, wpav)
        parts = gn_parts(rv_)
        xh = jnp.concatenate([p[0] for p in parts], axis=1)
        yn = xh * gr
        sg = _sigmoid(rg)
        sl_ = rg * sg
        t = (sl_ * yn).astype(BF16)
        yr = _dot(t, wprv)
        ga = _sigmoid(gt[:, :D] + bg[:, :D])
        gr_ = _sigmoid(gt[:, D:] + bg[:, D:])
        dgates = jnp.concatenate([dm_ * ya * ga * (1.0 - ga), dm_ * yr * gr_ * (1.0 - gr_)], axis=1)
        dya = (dm_ * ga).astype(BF16)
        dyr = (dm_ * gr_).astype(BF16)
        do_ = _dot_nt(dya, wpav)
        dwpa = _dot_tn(ov, dya)
        dt = _dot_nt(dyr, wprv)
        dwpr = _dot_tn(t, dyr)
        drg = dt * yn * (sg * (1.0 + rg * (1.0 - sg)))
        dyn = dt * sl_
        dgr = jnp.sum(dyn * xh, axis=0, keepdims=True)
        dxh = dyn * gr
        drets = []
        for h in range(RET_H):
            sl = slice(h * RET_D, (h + 1) * RET_D)
            xhh, rstd = parts[h]
            dxhh = dxh[:, sl]
            drets.append(rstd * (dxhh - jnp.mean(dxhh, axis=-1, keepdims=True)
                                 - xhh * jnp.mean(dxhh * xhh, axis=-1, keepdims=True)))
        dret = jnp.concatenate(drets, axis=1)
        dbg = jnp.sum(dgates, axis=0, keepdims=True)
        return do_, dret, drg, dgates, dwpa, dwpr, dgr, dbg

    do_a, dret, drg, dgates, dwpa, dw_proj_ret, dg_ret, db_gate = rowwise(
        mix_bwd, [(o_a, None), (ret, None), (z, (512, 3)), (z, (2048, 1)), (dmix, None)], [wpa, wpr, g_ret, b_gate],
        [(MLA_H * HP, BF16), (512, F32), (512, BF16), (2 * D, BF16)],
        [((MLA_H * HP, D), F32), ((512, D), F32), ((1, 512), F32), ((1, 2 * D), F32)], tile=tile, name="mix_bwd")

    delta, lse_row, delta_row = mla_prep(o_a, do_a, lse, tq=tq)
    dq_a = mla_dq(q_a, k_a, v_a, do_a, lse, delta, tq=tq)
    dk_a, dv_a = mla_dkv(q_a, k_a, v_a, do_a, lse_row, delta_row, tq=tq)
    drq_r, drk_r, drv = ret_bwd(rq, rk, rv, rstate, dret, rconsts, rb=rb)

    def lat_bwd(zl, cmv, s1v, s2v, dqv, dkv_, dvv, gq, gkv, wqv, wkv_, wvv):
        cqf, ckvf = zl[:, 0:256], zl[:, 256:384]
        cq = _rms(cqf, gq).astype(BF16)
        ckv = _rms(ckvf, gkv).astype(BF16)
        dq_pre = _rot_mla(dqv.astype(F32), cmv, -s1v, -s2v).astype(BF16)
        dkf = dkv_.astype(F32)
        dkr = dkf[:, 0:HP]
        for h in range(1, MLA_H):
            dkr = dkr + dkf[:, h * HP:(h + 1) * HP]
        lane = lax.broadcasted_iota(jnp.int32, dkr.shape, 1)
        dzk = _rot_mla(jnp.where((lane >= 64) & (lane < 96), dkr, 0.0), cmv, -s1v, -s2v)
        dkb = dkv_.astype(BF16)
        dvb = dvv.astype(BF16)
        dcq_n = _dot_nt(dq_pre, wqv)
        dckv_n = _dot_nt(dkb, wkv_) + _dot_nt(dvb, wvv)
        dwq = _dot_tn(cq, dq_pre)
        dwk = _dot_tn(ckv, dkb)
        dwv = _dot_tn(ckv, dvb)
        dcq, dgq = _rms_bwd(cqf, gq, dcq_n)
        dckv, dgkv = _rms_bwd(ckvf, gkv, dckv_n)
        return jnp.concatenate([dcq, dckv, dzk], axis=1), dwq, dwk, dwv, dgq, dgkv

    dz_lat, dwq, dwk, dwv, dg_q_lat, dg_kv_lat = rowwise(
        lat_bwd, [(z, (512, 8)), (cm, None), (s1, None), (s2, None), (dq_a, None), (dk_a, None), (dv_a, None)],
        [g_q_lat, g_kv_lat, wq, wk, wv], [(512, BF16)],
        [((MLA_QR, MLA_H * HP), F32), ((MLA_KVR, MLA_H * HP), F32), ((MLA_KVR, MLA_H * HP), F32),
         ((1, MLA_QR), F32), ((1, MLA_KVR), F32)], tile=tile, name="lat_bwd")

    def dz_assemble(dq_, dk_, dv_, drg_, dgt, dzl, crv, srv):
        a = _rot_ret(dq_, crv, -srv)
        b = _rot_ret(dk_, crv, -srv) * (RET_D ** -0.5)
        return jnp.concatenate([a, b, dv_, drg_.astype(F32), dgt.astype(F32), dzl.astype(F32)], axis=1)

    dz = rowwise(dz_assemble, [(drq_r, None), (drk_r, None), (drv, None), (drg, None), (dgates, None), (dz_lat, None),
                               (cr, None), (sr, None)], [], [(ZW, BF16)], [], tile=tile, name="dz_assemble")[0]
    du = matmul(dz, wz, tb=True, name="mm_du", tm=512)
    dwz = matmul_tn(u, dz, name="mm_dw_z")

    def norm_mix_bwd(xv, dyv, drv_, g):
        dx, dg = _rms_bwd(xv, g, dyv)
        return dx + drv_, dg

    grad_x, dg_mix = rowwise(norm_mix_bwd, [(xs, None), (du, None), (dh1, None)], [g_mix], [(D, F32)], [((1, D), F32)],
                             tile=tile, name="norm_mix_bwd")

    dw_in = jnp.concatenate([dwz[:, 4096:4352], dwz[:, 4352:4480], dwz[:, 4544:4576], dwz[:, 0:512], dwz[:, 512:1024],
                             dwz[:, 1024:1536], dwz[:, 1536:2048], dwz[:, 2048:4096]], axis=1)
    dw_uq = dwq.reshape(MLA_QR, MLA_H, HP)[:, :, :MLA_NOPE + MLA_ROPE].reshape(MLA_QR, -1)
    dw_ukv = jnp.concatenate([dwk.reshape(MLA_KVR, MLA_H, HP)[:, :, :MLA_NOPE],
                              dwv.reshape(MLA_KVR, MLA_H, HP)[:, :, :MLA_V]], axis=2).reshape(MLA_KVR, -1)
    dw_proj_mla = dwpa.reshape(MLA_H, HP, D)[:, :MLA_V, :].reshape(MLA_H * MLA_V, D)
    fulls = {"w_in": dw_in, "w_uq": dw_uq, "w_ukv": dw_ukv, "w_proj_mla": dw_proj_mla, "w_proj_ret": dw_proj_ret,
             "w_out": dw_out, "w_xq": dw_xq, "w_xkv": dw_xkv, "w_xo": dw_xo, "w_up": dw_up, "w_conv": dw_conv,
             "w_down": dw_down}
    slots = _full_to_slots(fulls, F32)
    slots_bf = _cast_rows(slots.reshape(N_DEV * BIG_ROWS, LANES), BF16, tile=128, name="cast_grads").reshape(
        N_DEV, BIG_ROWS, LANES)
    small_own = _pack_small({"g_mix": dg_mix, "b_gate": db_gate, "g_q_lat": dg_q_lat, "g_kv_lat": dg_kv_lat,
                             "g_ret": dg_ret, "g_cross": dg_cross, "g_mem": dg_mem, "g_ffn": dg_ffn,
                             "b_conv": db_conv, "g_final": dg_final})
    recv_big, recv_small = exchange_grads(slots_bf, small_own)

    me = 4 * lax.axis_index("x") + 2 * lax.axis_index("y") + lax.axis_index("c")
    own_big = lax.dynamic_index_in_dim(slots, me, axis=0, keepdims=False)
    g_big, d_big, nm_big, nv_big = reduce_adamw_big(own_big, recv_big, w_pack, m_pack, v_pack, tile=128)
    sw = _pack_small({n: args[n] for n, _, _ in SMALL})
    sm = _pack_small({n: args["m_" + n] for n, _, _ in SMALL})
    sv = _pack_small({n: args["v_" + n] for n, _, _ in SMALL})
    g_sm, d_sm, nm_sm, nv_sm = reduce_adamw_small(small_own, recv_small, sw, sm, sv)

    order = ["g_mix", "w_in", "b_gate", "g_q_lat", "w_uq", "g_kv_lat", "w_ukv", "w_proj_mla", "g_ret", "w_proj_ret",
             "w_out", "g_cross", "g_mem", "w_xq", "w_xkv", "w_xo", "g_ffn", "w_up", "w_conv", "b_conv", "w_down",
             "g_final"]
    outs = [loss, grad_x[None]]
    for big_p, small_p in ((g_big, g_sm), (d_big, d_sm), (nm_big, nm_sm), (nv_big, nv_sm)):
        b = _unpack_big(big_p)
        s = _unpack_small(small_p)
        outs += [b[n] if n in b else s[n] for n in order]
    return tuple(outs)
```

```python
import functools
import math

import jax
import jax.numpy as jnp
import numpy as np
from jax import lax
from jax.experimental import pallas as pl
from jax.experimental.pallas import tpu as pltpu

F32 = jnp.float32
BF16 = jnp.bfloat16

D = 1024
MLA_H, MLA_NOPE, MLA_ROPE, MLA_V = 8, 64, 32, 64
MLA_QR, MLA_KVR = 256, 128
RET_H, RET_D, RET_C = 4, 128, 128
X_H, X_HD = 4, 256
D_FF = 2816
IN_W = 4512
THETA = 10000.0
EPS = 1e-6
HP = 128
ZW = 4608
N_DEV = 8

ADAM_LR, ADAM_B1, ADAM_B2, ADAM_EPS, ADAM_WD, ADAM_STEP = 0.001, 0.9, 0.999, 1e-08, 0.01, 10

VMEM_LIMIT = 56 * 1024 * 1024
MESH = pl.DeviceIdType.MESH
VM = pl.BlockSpec(memory_space=pltpu.VMEM)
ANY = pl.BlockSpec(memory_space=pl.ANY)


def _cp(n_axes):
    return pltpu.CompilerParams(dimension_semantics=("arbitrary",) * n_axes, vmem_limit_bytes=VMEM_LIMIT)


def _pick(n, cap, mult=128):
    best = None
    for t in range(mult, min(n, cap) + 1, mult):
        if n % t == 0:
            best = t
    return best if best is not None else n


def _dot(a, b):
    return jnp.dot(a, b, preferred_element_type=F32)


def _dot_nt(a, b):
    return lax.dot_general(a, b, (((1,), (1,)), ((), ())), preferred_element_type=F32)


def _dot_tn(a, b):
    return lax.dot_general(a, b, (((0,), (0,)), ((), ())), preferred_element_type=F32)


def _sds(shape, dtype):
    return jax.ShapeDtypeStruct(shape, dtype)


def matmul(a, b, *, name, tb=False, res=None, out_dtype=F32, tm=1024, tn=512):
    M, K = a.shape
    N = b.shape[0] if tb else b.shape[1]
    tm = _pick(M, tm, 8)
    tn = _pick(N, tn)

    def body(*refs):
        if res is None:
            a_ref, b_ref, o_ref = refs
        else:
            a_ref, b_ref, r_ref, o_ref = refs
        av = a_ref[...].astype(BF16)
        bv = b_ref[...].astype(BF16)
        acc = _dot_nt(av, bv) if tb else _dot(av, bv)
        if res is not None:
            acc = acc + r_ref[...].astype(F32)
        o_ref[...] = acc.astype(o_ref.dtype)

    in_specs = [
        pl.BlockSpec((tm, K), lambda i, j: (i, 0)),
        pl.BlockSpec((tn, K), lambda i, j: (j, 0)) if tb else pl.BlockSpec((K, tn), lambda i, j: (0, j)),
    ]
    args = [a, b]
    if res is not None:
        in_specs.append(pl.BlockSpec((tm, tn), lambda i, j: (i, j)))
        args.append(res)
    return pl.pallas_call(
        body,
        name=name,
        grid=(M // tm, N // tn),
        in_specs=in_specs,
        out_specs=pl.BlockSpec((tm, tn), lambda i, j: (i, j)),
        out_shape=_sds((M, N), out_dtype),
        compiler_params=_cp(2),
    )(*args)


def matmul2_tb(a1, a2, b, *, name, tm=512, tn=512):
    M, K1 = a1.shape
    N = b.shape[0]
    tm = _pick(M, tm, 8)
    tn = _pick(N, tn)

    def body(a1_ref, a2_ref, b1_ref, b2_ref, o_ref):
        o_ref[...] = (_dot_nt(a1_ref[...].astype(BF16), b1_ref[...].astype(BF16))
                      + _dot_nt(a2_ref[...].astype(BF16), b2_ref[...].astype(BF16)))

    return pl.pallas_call(
        body,
        name=name,
        grid=(M // tm, N // tn),
        in_specs=[pl.BlockSpec((tm, K1), lambda i, j: (i, 0)), pl.BlockSpec((tm, K1), lambda i, j: (i, 0)),
                  pl.BlockSpec((tn, K1), lambda i, j: (j, 0)), pl.BlockSpec((tn, K1), lambda i, j: (j, 1))],
        out_specs=pl.BlockSpec((tm, tn), lambda i, j: (i, j)),
        out_shape=_sds((M, N), F32),
        compiler_params=_cp(2),
    )(a1, a2, b, b)


def matmul_tn(a, b, *, name, tm=1024, tn=1024, tk=1024):
    R, M = a.shape
    N = b.shape[1]
    tm = _pick(M, tm)
    tn = _pick(N, tn)
    tk = _pick(R, tk, 16)
    nk = R // tk

    def body(a_ref, b_ref, o_ref, acc_ref):
        k = pl.program_id(2)

        @pl.when(k == 0)
        def _():
            acc_ref[...] = jnp.zeros_like(acc_ref)

        acc_ref[...] += _dot_tn(a_ref[...].astype(BF16), b_ref[...].astype(BF16))

        @pl.when(k == nk - 1)
        def _():
            o_ref[...] = acc_ref[...]

    return pl.pallas_call(
        body,
        name=name,
        grid=(M // tm, N // tn, nk),
        in_specs=[pl.BlockSpec((tk, tm), lambda i, j, k: (k, i)), pl.BlockSpec((tk, tn), lambda i, j, k: (k, j))],
        out_specs=pl.BlockSpec((tm, tn), lambda i, j, k: (i, j)),
        out_shape=_sds((M, N), F32),
        scratch_shapes=[pltpu.VMEM((tm, tn), F32)],
        compiler_params=_cp(3),
    )(a, b)


def rowwise(fn, rows, consts, out_rows, out_accs, *, tile, name):
    T = rows[0][0].shape[0]
    nt = T // tile
    n_r, n_c, n_o, n_a = len(rows), len(consts), len(out_rows), len(out_accs)

    def body(*refs):
        ins = [r[...] for r in refs[: n_r + n_c]]
        outs = fn(*ins)
        if not isinstance(outs, (tuple, list)):
            outs = (outs,)
        o_refs = refs[n_r + n_c : n_r + n_c + n_o]
        a_refs = refs[n_r + n_c + n_o :]
        for o_ref, o in zip(o_refs, outs[:n_o]):
            o_ref[...] = o.astype(o_ref.dtype)
        if n_a:
            first = pl.program_id(0) == 0

            @pl.when(first)
            def _():
                for a_ref, o in zip(a_refs, outs[n_o:]):
                    a_ref[...] = o.astype(a_ref.dtype)

            @pl.when(jnp.logical_not(first))
            def _():
                for a_ref, o in zip(a_refs, outs[n_o:]):
                    a_ref[...] += o.astype(a_ref.dtype)

    in_specs = []
    args = []
    for arr, win in rows:
        if win is None:
            in_specs.append(pl.BlockSpec((tile, arr.shape[1]), lambda i: (i, 0)))
        else:
            w, cb = win
            in_specs.append(pl.BlockSpec((tile, w), functools.partial(lambda i, cb: (i, cb), cb=cb)))
        args.append(arr)
    for c in consts:
        in_specs.append(pl.BlockSpec(c.shape, functools.partial(lambda i, nd: (0,) * nd, nd=c.ndim)))
        args.append(c)
    out_specs = [pl.BlockSpec((tile, w), lambda i: (i, 0)) for w, _ in out_rows]
    out_shape = [_sds((T, w), dt) for w, dt in out_rows]
    for shp, dt in out_accs:
        out_specs.append(pl.BlockSpec(shp, functools.partial(lambda i, nd: (0,) * nd, nd=len(shp))))
        out_shape.append(_sds(shp, dt))
    return pl.pallas_call(
        body,
        name=name,
        grid=(nt,),
        in_specs=in_specs,
        out_specs=out_specs,
        out_shape=out_shape,
        compiler_params=_cp(1),
    )(*args)


def _rms(x, g):
    r = lax.rsqrt(jnp.mean(x * x, axis=-1, keepdims=True) + EPS)
    return x * r * g


def _rms_bwd(x, g, dy):
    r = lax.rsqrt(jnp.mean(x * x, axis=-1, keepdims=True) + EPS)
    xh = x * r
    dg = jnp.sum(dy * xh, axis=0, keepdims=True)
    dxh = dy * g
    dx = r * (dxh - xh * jnp.mean(dxh * xh, axis=-1, keepdims=True))
    return dx, dg


def _sigmoid(x):
    return 1.0 / (1.0 + jnp.exp(-x))


def _rot_mla(x, c, s1, s2):
    n = x.shape[1] // HP
    outs = []
    for h in range(n):
        xh = x[:, h * HP : (h + 1) * HP]
        outs.append(xh * c + pltpu.roll(xh, HP - 16, 1) * s1 + pltpu.roll(xh, 16, 1) * s2)
    return outs[0] if n == 1 else jnp.concatenate(outs, axis=1)


def _rot_ret(x, c, s):
    n = x.shape[1] // RET_D
    outs = []
    for h in range(n):
        xh = x[:, h * RET_D : (h + 1) * RET_D]
        outs.append(xh * c + pltpu.roll(xh, RET_D // 2, 1) * s)
    return outs[0] if n == 1 else jnp.concatenate(outs, axis=1)


def rot_tables(pos_f, *, tile):
    lane_np = np.arange(128)
    inv_m = (jnp.asarray(THETA, F32) ** (-jnp.asarray(lane_np & 15, F32) / 16.0)).reshape(1, 128)
    inv_r = (jnp.asarray(THETA, F32) ** (-jnp.asarray(lane_np & 63, F32) / 64.0)).reshape(1, 128)

    def fn(p, im, ir):
        lane = lax.broadcasted_iota(jnp.int32, p.shape, 1)
        ang = p * im
        cm = jnp.where((lane >= 64) & (lane < 96), jnp.cos(ang), 1.0)
        sn = jnp.sin(ang)
        s1 = jnp.where((lane >= 64) & (lane < 80), -sn, 0.0)
        s2 = jnp.where((lane >= 80) & (lane < 96), sn, 0.0)
        angr = p * ir
        snr = jnp.sin(angr)
        return cm, s1, s2, jnp.cos(angr), jnp.where(lane < 64, -snr, snr)

    return rowwise(fn, [(pos_f, None)], [inv_m, inv_r], [(128, F32)] * 5, [], tile=tile, name="rot_tables")


MLA_SCALE = (MLA_NOPE + MLA_ROPE) ** -0.5
MLA_C2 = MLA_SCALE * math.log2(math.e)
ONE_LANE = MLA_V
NEG = -1e30


def _tri_mask(n, lower_rows_ge_cols=True):
    r = lax.broadcasted_iota(jnp.int32, (n, n), 0)
    c = lax.broadcasted_iota(jnp.int32, (n, n), 1)
    return r >= c if lower_rows_ge_cols else c >= r


def mla_fwd(q, k, v, *, tq):
    T = q.shape[0]
    nq = T // tq
    rep = tq // HP

    def body(q_ref, k_ref, v_ref, o_ref, lse_ref, m_sc, acc_sc):
        i = pl.program_id(1)
        qv = q_ref[...]
        m_sc[...] = jnp.full(m_sc.shape, NEG, F32)
        acc_sc[...] = jnp.zeros(acc_sc.shape, F32)

        def block(j, masked):
            off = pl.multiple_of(j * tq, tq)
            kb = k_ref[pl.ds(off, tq), :]
            vb = v_ref[pl.ds(off, tq), :]
            s = _dot_nt(qv, kb) * MLA_C2
            if masked:
                s = jnp.where(_tri_mask(tq), s, NEG)
            m_prev = m_sc[...]
            m_next = jnp.maximum(m_prev, jnp.max(s, axis=-1, keepdims=True))
            p = jnp.exp2(s - jnp.tile(m_next, (1, rep)))
            alpha = jnp.exp2(m_prev - m_next)
            acc_sc[...] = alpha * acc_sc[...] + _dot(p.astype(BF16), vb)
            m_sc[...] = m_next

        def loop_body(j, carry):
            block(j, False)
            return carry

        lax.fori_loop(0, i, loop_body, 0)
        block(i, True)
        acc = acc_sc[...]
        l = acc[:, ONE_LANE:ONE_LANE + 1]
        o_ref[...] = (acc / l).astype(o_ref.dtype)
        lse_ref[...] = m_sc[...] + jnp.log(l) * math.log2(math.e)

    return pl.pallas_call(
        body,
        name="mla_fwd",
        grid=(MLA_H, nq),
        in_specs=[
            pl.BlockSpec((tq, HP), lambda h, i: (i, h)),
            pl.BlockSpec((T, HP), lambda h, i: (0, h)),
            pl.BlockSpec((T, HP), lambda h, i: (0, h)),
        ],
        out_specs=[pl.BlockSpec((tq, HP), lambda h, i: (i, h)), pl.BlockSpec((tq, HP), lambda h, i: (i, h))],
        out_shape=[_sds((T, MLA_H * HP), BF16), _sds((T, MLA_H * HP), F32)],
        scratch_shapes=[pltpu.VMEM((tq, HP), F32), pltpu.VMEM((tq, HP), F32)],
        compiler_params=_cp(2),
    )(q, k, v)


def mla_prep(o, do, lse, *, tq):
    T = o.shape[0]
    nq = T // tq

    def body(o_ref, do_ref, lse_ref, delta_ref, lse_row_ref, delta_row_ref):
        d = jnp.sum(o_ref[...].astype(F32) * do_ref[...].astype(F32), axis=-1, keepdims=True)
        db = jnp.broadcast_to(d, (tq, HP))
        delta_ref[...] = db
        lse_row_ref[...] = lse_ref[...].T[0:1, :]
        delta_row_ref[...] = db.T[0:1, :]

    blk = pl.BlockSpec((tq, HP), lambda h, i: (i, h))
    row = pl.BlockSpec((None, None, 1, tq), lambda h, i: (h, i, 0, 0))
    return pl.pallas_call(
        body,
        name="mla_prep",
        grid=(MLA_H, nq),
        in_specs=[blk, blk, blk],
        out_specs=[blk, row, row],
        out_shape=[_sds((T, MLA_H * HP), F32), _sds((MLA_H, nq, 1, tq), F32), _sds((MLA_H, nq, 1, tq), F32)],
        compiler_params=_cp(2),
    )(o, do, lse)


def mla_dq(q, k, v, do, lse, delta, *, tq):
    T = q.shape[0]
    nq = T // tq
    rep = tq // HP

    def body(q_ref, k_ref, v_ref, do_ref, lse_ref, delta_ref, dq_ref, acc_sc):
        i = pl.program_id(1)
        qv = q_ref[...]
        dov = do_ref[...]
        lse_t = jnp.tile(lse_ref[...], (1, rep))
        delta_t = jnp.tile(delta_ref[...], (1, rep))
        acc_sc[...] = jnp.zeros(acc_sc.shape, F32)

        def block(j, masked):
            off = pl.multiple_of(j * tq, tq)
            kb = k_ref[pl.ds(off, tq), :]
            vb = v_ref[pl.ds(off, tq), :]
            s = _dot_nt(qv, kb) * MLA_C2
            if masked:
                s = jnp.where(_tri_mask(tq), s, NEG)
            p = jnp.exp2(s - lse_t)
            dp = _dot_nt(dov, vb)
            ds = p * (dp - delta_t) * MLA_SCALE
            acc_sc[...] += _dot(ds.astype(BF16), kb)

        def loop_body(j, carry):
            block(j, False)
            return carry

        lax.fori_loop(0, i, loop_body, 0)
        block(i, True)
        dq_ref[...] = acc_sc[...].astype(dq_ref.dtype)

    blk = pl.BlockSpec((tq, HP), lambda h, i: (i, h))
    full = pl.BlockSpec((T, HP), lambda h, i: (0, h))
    return pl.pallas_call(
        body,
        name="mla_dq",
        grid=(MLA_H, nq),
        in_specs=[blk, full, full, blk, blk, blk],
        out_specs=blk,
        out_shape=_sds((T, MLA_H * HP), BF16),
        scratch_shapes=[pltpu.VMEM((tq, HP), F32)],
        compiler_params=_cp(2),
    )(q, k, v, do, lse, delta)


def mla_dkv(q, k, v, do, lse_row, delta_row, *, tq):
    T = q.shape[0]
    nq = T // tq

    def body(q_ref, k_ref, v_ref, do_ref, lse_ref, delta_ref, dk_ref, dv_ref, dk_sc, dv_sc):
        j = pl.program_id(1)
        kv = k_ref[...]
        vv = v_ref[...]
        dk_sc[...] = jnp.zeros(dk_sc.shape, F32)
        dv_sc[...] = jnp.zeros(dv_sc.shape, F32)

        def block(i, masked):
            off = pl.multiple_of(i * tq, tq)
            qb = q_ref[pl.ds(off, tq), :]
            dob = do_ref[pl.ds(off, tq), :]
            st = _dot_nt(kv, qb) * MLA_C2
            if masked:
                st = jnp.where(_tri_mask(tq, False), st, NEG)
            pt = jnp.exp2(st - lse_ref[i])
            dv_sc[...] += _dot(pt.astype(BF16), dob)
            dpt = _dot_nt(vv, dob)
            dst = pt * (dpt - delta_ref[i]) * MLA_SCALE
            dk_sc[...] += _dot(dst.astype(BF16), qb)

        block(j, True)

        def loop_body(i, carry):
            block(i, False)
            return carry

        lax.fori_loop(j + 1, nq, loop_body, 0)
        dk_ref[...] = dk_sc[...].astype(dk_ref.dtype)
        dv_ref[...] = dv_sc[...].astype(dv_ref.dtype)

    blk = pl.BlockSpec((tq, HP), lambda h, j: (j, h))
    full = pl.BlockSpec((T, HP), lambda h, j: (0, h))
    rows = pl.BlockSpec((None, nq, 1, tq), lambda h, j: (h, 0, 0, 0))
    return pl.pallas_call(
        body,
        name="mla_dkv",
        grid=(MLA_H, nq),
        in_specs=[full, blk, blk, full, rows, rows],
        out_specs=[blk, blk],
        out_shape=[_sds((T, MLA_H * HP), BF16)] * 2,
        scratch_shapes=[pltpu.VMEM((tq, HP), F32), pltpu.VMEM((tq, HP), F32)],
        compiler_params=_cp(2),
    )(q, k, v, do, lse_row, delta_row)


def _ret_consts():
    h = jnp.arange(RET_H, dtype=F32)
    log_g = jnp.log1p(-jnp.exp2(-5.0 - h))
    idx = jnp.arange(RET_C, dtype=F32)
    rel = idx[:, None] - idx[None, :]
    dmask = jnp.where(rel >= 0, jnp.exp(log_g[:, None, None] * jnp.maximum(rel, 0.0)), 0.0)
    zeta = jnp.exp(log_g[:, None] * (RET_C - 1.0 - idx)[None, :])
    xi = jnp.exp(log_g[:, None] * (idx + 1.0)[None, :])
    decay = jnp.exp(log_g * RET_C)
    zb = jnp.broadcast_to(zeta[:, :, None], (RET_H, RET_C, RET_D))
    xb = jnp.broadcast_to(xi[:, :, None], (RET_H, RET_C, RET_D))
    db = jnp.broadcast_to(decay[:, None, None], (RET_H, RET_C, RET_D))
    return dmask.astype(F32), zb.astype(F32), xb.astype(F32), db.astype(F32)


def ret_fwd(rq, rk, rv, consts, *, rb):
    T = rq.shape[0]
    nb = T // rb
    ncb = rb // RET_C

    def body(q_ref, k_ref, v_ref, dm_ref, z_ref, x_ref, dc_ref, o_ref, st_ref, r_sc):
        @pl.when(pl.program_id(1) == 0)
        def _():
            r_sc[...] = jnp.zeros(r_sc.shape, F32)

        dm, zt, xi, dc = dm_ref[...], z_ref[...], x_ref[...], dc_ref[...]
        for c in range(ncb):
            sl = slice(c * RET_C, (c + 1) * RET_C)
            q, k, v = q_ref[sl, :], k_ref[sl, :], v_ref[sl, :]
            r = r_sc[...]
            rbf = r.astype(BF16)
            st_ref[sl, :] = rbf
            s = _dot_nt(q, k) * dm
            inner = _dot(s.astype(BF16), v)
            cross = _dot((q.astype(F32) * xi).astype(BF16), rbf)
            o_ref[sl, :] = inner + cross
            kz = (k.astype(F32) * zt).T.astype(BF16)
            r_sc[...] = r * dc + _dot(kz, v)

    blk = pl.BlockSpec((rb, RET_D), lambda h, b: (b, h))
    cst = pl.BlockSpec((None, RET_C, RET_D), lambda h, b: (h, 0, 0))
    return pl.pallas_call(
        body,
        name="ret_fwd",
        grid=(RET_H, nb),
        in_specs=[blk, blk, blk, cst, cst, cst, cst],
        out_specs=[blk, blk],
        out_shape=[_sds((T, RET_H * RET_D), F32), _sds((T, RET_H * RET_D), BF16)],
        scratch_shapes=[pltpu.VMEM((RET_D, RET_D), F32)],
        compiler_params=_cp(2),
    )(rq, rk, rv, *consts)


def ret_bwd(rq, rk, rv, st, dret, consts, *, rb):
    T = rq.shape[0]
    nb = T // rb
    ncb = rb // RET_C

    def body(q_ref, k_ref, v_ref, st_ref, do_ref, dm_ref, z_ref, x_ref, dc_ref, dq_ref, dk_ref, dv_ref, g_sc):
        @pl.when(pl.program_id(1) == 0)
        def _():
            g_sc[...] = jnp.zeros(g_sc.shape, F32)

        dm, zt, xi, dc = dm_ref[...], z_ref[...], x_ref[...], dc_ref[...]
        for c in reversed(range(ncb)):
            sl = slice(c * RET_C, (c + 1) * RET_C)
            q, k, v, rp = q_ref[sl, :], k_ref[sl, :], v_ref[sl, :], st_ref[sl, :]
            dob = do_ref[sl, :].astype(BF16)
            qf, kf = q.astype(F32), k.astype(F32)
            gn = g_sc[...]
            gnb = gn.astype(BF16)
            s = _dot_nt(q, k) * dm
            ds = _dot_nt(dob, v) * dm
            dq = _dot(ds.astype(BF16), k) + _dot_nt(dob, rp) * xi
            dk = _dot(ds.T.astype(BF16), q) + _dot_nt(v, gnb) * zt
            dv = _dot(s.T.astype(BF16), dob) + _dot((kf * zt).astype(BF16), gnb)
            dq_ref[sl, :] = dq.astype(dq_ref.dtype)
            dk_ref[sl, :] = dk.astype(dk_ref.dtype)
            dv_ref[sl, :] = dv.astype(dv_ref.dtype)
            g_sc[...] = _dot((qf * xi).T.astype(BF16), dob) + dc * gn

    blk = pl.BlockSpec((rb, RET_D), lambda h, b: (nb - 1 - b, h))
    cst = pl.BlockSpec((None, RET_C, RET_D), lambda h, b: (h, 0, 0))
    return pl.pallas_call(
        body,
        name="ret_bwd",
        grid=(RET_H, nb),
        in_specs=[blk, blk, blk, blk, blk, cst, cst, cst, cst],
        out_specs=[blk, blk, blk],
        out_shape=[_sds((T, RET_H * RET_D), F32)] * 3,
        scratch_shapes=[pltpu.VMEM((RET_D, RET_D), F32)],
        compiler_params=_cp(2),
    )(rq, rk, rv, st, dret, *consts)


HALO = 16


def conv_act_fwd(up_pre, w_conv, b_conv, *, tile, cw):
    T = up_pre.shape[0]
    nt = T // tile
    ncol = D_FF // cw
    hb = tile // HALO

    def body(pa_ref, a_ref, pb_ref, b_ref, wa_ref, wb_ref, ba_ref, bb_ref, o_ref):
        i = pl.program_id(1)
        keep = (i > 0).astype(F32)

        def conv(prev_ref, cur_ref, w_ref, bias_ref):
            ext = jnp.concatenate([prev_ref[...].astype(F32) * keep, cur_ref[...].astype(F32)], axis=0)
            w = w_ref[...]
            y = ext * w[2:3, :] + pltpu.roll(ext, 1, 0) * w[1:2, :] + pltpu.roll(ext, 2, 0) * w[0:1, :] + bias_ref[...]
            return y[HALO:, :]

        a = conv(pa_ref, a_ref, wa_ref, ba_ref)
        b = conv(pb_ref, b_ref, wb_ref, bb_ref)
        o_ref[...] = (a * _sigmoid(a) * b).astype(o_ref.dtype)

    prev_a = pl.BlockSpec((HALO, cw), lambda j, i: (jnp.maximum(i * hb - 1, 0), j))
    cur_a = pl.BlockSpec((tile, cw), lambda j, i: (i, j))
    prev_b = pl.BlockSpec((HALO, cw), lambda j, i: (jnp.maximum(i * hb - 1, 0), j + ncol))
    cur_b = pl.BlockSpec((tile, cw), lambda j, i: (i, j + ncol))
    w_a = pl.BlockSpec((3, cw), lambda j, i: (0, j))
    w_b = pl.BlockSpec((3, cw), lambda j, i: (0, j + ncol))
    bias_a = pl.BlockSpec((1, cw), lambda j, i: (0, j))
    bias_b = pl.BlockSpec((1, cw), lambda j, i: (0, j + ncol))
    return pl.pallas_call(
        body,
        name="conv_act_fwd",
        grid=(ncol, nt),
        in_specs=[prev_a, cur_a, prev_b, cur_b, w_a, w_b, bias_a, bias_b],
        out_specs=pl.BlockSpec((tile, cw), lambda j, i: (i, j)),
        out_shape=_sds((T, D_FF), BF16),
        compiler_params=_cp(2),
    )(up_pre, up_pre, up_pre, up_pre, w_conv, w_conv, b_conv, b_conv)


def conv_act_bwd(up_pre, dact, w_conv, b_conv, *, tile, cw):
    T = up_pre.shape[0]
    nt = T // tile
    ncol = D_FF // cw
    hb = tile // HALO
    ext_rows = tile + 2 * HALO

    def body(pa_ref, a_ref, na_ref, pb_ref, b_ref, nb_ref, d_ref, nd_ref, wa_ref, wb_ref, ba_ref, bb_ref,
             dxa_ref, dxb_ref, sa_ref, sb_ref):
        i = pl.program_id(1)
        keep_p = (i > 0).astype(F32)
        keep_n = (i < nt - 1).astype(F32)

        def ext_of(prev_ref, cur_ref, next_ref):
            return jnp.concatenate(
                [prev_ref[...].astype(F32) * keep_p, cur_ref[...].astype(F32), next_ref[...].astype(F32) * keep_n], axis=0)

        def conv(ext, w, bias):
            return ext * w[2:3, :] + pltpu.roll(ext, 1, 0) * w[1:2, :] + pltpu.roll(ext, 2, 0) * w[0:1, :] + bias

        xa = ext_of(pa_ref, a_ref, na_ref)
        xb = ext_of(pb_ref, b_ref, nb_ref)
        wa, wb = wa_ref[...], wb_ref[...]
        a = conv(xa, wa, ba_ref[...])
        b = conv(xb, wb, bb_ref[...])
        dy = jnp.concatenate(
            [jnp.zeros((HALO, cw), F32), d_ref[...].astype(F32), nd_ref[...].astype(F32) * keep_n], axis=0)
        sg = _sigmoid(a)
        da = dy * b * (sg * (1.0 + a * (1.0 - sg)))
        db = dy * (a * sg)

        def back(dup, ext, w, dx_ref, s_ref):
            dx = dup * w[2:3, :] + pltpu.roll(dup, ext_rows - 1, 0) * w[1:2, :] + pltpu.roll(dup, ext_rows - 2, 0) * w[0:1, :]
            dx_ref[...] = dx[HALO:HALO + tile, :].astype(dx_ref.dtype)
            dc = dup[HALO:HALO + tile, :]
            r2 = jnp.sum(dc * ext[HALO:HALO + tile, :], axis=0, keepdims=True)
            r1 = jnp.sum(dc * pltpu.roll(ext, 1, 0)[HALO:HALO + tile, :], axis=0, keepdims=True)
            r0 = jnp.sum(dc * pltpu.roll(ext, 2, 0)[HALO:HALO + tile, :], axis=0, keepdims=True)
            rb = jnp.sum(dc, axis=0, keepdims=True)
            row = lax.broadcasted_iota(jnp.int32, (8, cw), 0)
            upd = (jnp.where(row == 0, r0, 0.0) + jnp.where(row == 1, r1, 0.0) + jnp.where(row == 2, r2, 0.0)
                   + jnp.where(row == 3, rb, 0.0))

            @pl.when(i == 0)
            def _():
                s_ref[...] = upd

            @pl.when(i > 0)
            def _():
                s_ref[...] += upd

        back(da, xa, wa, dxa_ref, sa_ref)
        back(db, xb, wb, dxb_ref, sb_ref)

    def prev_of(shift):
        return pl.BlockSpec((HALO, cw), lambda j, i: (jnp.maximum(i * hb - 1, 0), j + shift))

    def next_of(shift):
        return pl.BlockSpec((HALO, cw), lambda j, i: (jnp.minimum((i + 1) * hb, nt * hb - 1), j + shift))

    def cur_of(shift):
        return pl.BlockSpec((tile, cw), lambda j, i: (i, j + shift))

    def row_of(rows, shift):
        return pl.BlockSpec((rows, cw), lambda j, i: (0, j + shift))

    return pl.pallas_call(
        body,
        name="conv_act_bwd",
        grid=(ncol, nt),
        in_specs=[prev_of(0), cur_of(0), next_of(0), prev_of(ncol), cur_of(ncol), next_of(ncol), cur_of(0), next_of(0),
                  row_of(3, 0), row_of(3, ncol), row_of(1, 0), row_of(1, ncol)],
        out_specs=[cur_of(0), cur_of(0), row_of(8, 0), row_of(8, 0)],
        out_shape=[_sds((T, D_FF), BF16), _sds((T, D_FF), BF16), _sds((8, D_FF), F32), _sds((8, D_FF), F32)],
        compiler_params=_cp(2),
    )(up_pre, up_pre, up_pre, up_pre, up_pre, up_pre, dact, dact, w_conv, w_conv, b_conv, b_conv)


A_W = 1536
A_IN, A_UP, A_XKV, A_END = 0, 564, 1268, 1524
NS_IN, NS_UP, NS_XKV, NS_UQ = 564, 704, 256, 96
B_ROWS = 736
B_OUT, B_XQ, B_XO, B_DN = 0, 128, 256, 384
RS_DN = 352
C_ROWS = 1408
C_UKV, C_PMLA, C_PRET, C_UQ = 0, 128, 640, 1152
D_SHAPE = (8, 768)
WZ_RUNS = ((416, 4096, 0), (0, 384, 4096), (384, 32, 4544))
WZ_ZERO = ((4480, 4544), (4576, 4608))


def _pieces(orig_start, length, dst_start, ns):
    out, c, d, end = [], orig_start, dst_start, orig_start + length
    while c < end:
        j, off = c // ns, c % ns
        ln = min(ns - off, end - c)
        out.append((j, off, d, ln))
        c += ln
        d += ln
    return out


def pack_a(w_in, w_up, w_xkv, *, tile):
    def body(a_ref, b_ref, c_ref, o_ref):
        o_ref[:, A_IN:A_UP] = a_ref[...].astype(BF16)
        o_ref[:, A_UP:A_XKV] = b_ref[...].astype(BF16)
        o_ref[:, A_XKV:A_END] = c_ref[...].astype(BF16)
        o_ref[:, A_END:A_W] = jnp.zeros((tile, A_W - A_END), BF16)

    return pl.pallas_call(
        body,
        name="pack_a",
        grid=(D // tile,),
        in_specs=[pl.BlockSpec((tile, NS_IN), lambda i: (i, 0)), pl.BlockSpec((tile, NS_UP), lambda i: (i, 0)),
                  pl.BlockSpec((tile, NS_XKV), lambda i: (i, 0))],
        out_specs=pl.BlockSpec((tile, A_W), lambda i: (i, 0)),
        out_shape=_sds((D, A_W), BF16),
        compiler_params=_cp(1),
    )(w_in, w_up, w_xkv)


def pack_bcd(w_out, w_xq, w_xo, w_down, w_ukv, w_pmla, w_pret, w_uq, w_conv):
    def body(o_ref, xq_ref, xo_ref, dn_ref, ukv_ref, pm_ref, pr_ref, uq_ref, cv_ref, b_ref, c_ref, d_ref):
        b_ref[B_OUT:B_XQ, :] = o_ref[...].astype(BF16)
        b_ref[B_XQ:B_XO, :] = xq_ref[...].astype(BF16)
        b_ref[B_XO:B_DN, :] = xo_ref[...].astype(BF16)
        b_ref[B_DN:B_ROWS, :] = dn_ref[...].astype(BF16)
        c_ref[C_UKV:C_PMLA, :] = ukv_ref[...].astype(BF16)
        c_ref[C_PMLA:C_PRET, :] = pm_ref[...].astype(BF16)
        c_ref[C_PRET:C_UQ, :] = pr_ref[...].astype(BF16)
        c_ref[C_UQ:C_ROWS, 0:NS_UQ] = uq_ref[...].astype(BF16)
        c_ref[C_UQ:C_ROWS, NS_UQ:HP] = jnp.zeros((C_ROWS - C_UQ, HP - NS_UQ), BF16)
        d_ref[...] = jnp.zeros(D_SHAPE, F32)
        d_ref[0:3, 0:NS_UP] = cv_ref[...]

    return pl.pallas_call(
        body,
        name="pack_bcd",
        in_specs=[VM] * 9,
        out_specs=[VM] * 3,
        out_shape=[_sds((B_ROWS, D), BF16), _sds((C_ROWS, HP), BF16), _sds(D_SHAPE, F32)],
        compiler_params=pltpu.CompilerParams(vmem_limit_bytes=VMEM_LIMIT),
    )(w_out, w_xq, w_xo, w_down, w_ukv, w_pmla, w_pret, w_uq, w_conv)


def assemble_a(ga, *, tile):
    def body(g_ref, wz_ref, wup_ref, wxkv_ref):
        for lo, hi in WZ_ZERO:
            wz_ref[:, lo:hi] = jnp.zeros((tile, hi - lo), BF16)
        for os_, ln_, ds_ in WZ_RUNS:
            for j, off, d, ln in _pieces(os_, ln_, ds_, NS_IN):
                wz_ref[:, d:d + ln] = g_ref[j, :, A_IN + off:A_IN + off + ln]
        for j in range(N_DEV):
            wup_ref[:, j * NS_UP:(j + 1) * NS_UP] = g_ref[j, :, A_UP:A_XKV]
            wxkv_ref[:, j * NS_XKV:(j + 1) * NS_XKV] = g_ref[j, :, A_XKV:A_END]

    return pl.pallas_call(
        body,
        name="assemble_a",
        grid=(D // tile,),
        in_specs=[pl.BlockSpec((N_DEV, tile, A_W), lambda i: (0, i, 0))],
        out_specs=[pl.BlockSpec((tile, ZW), lambda i: (i, 0)), pl.BlockSpec((tile, 2 * D_FF), lambda i: (i, 0)),
                   pl.BlockSpec((tile, 2 * D), lambda i: (i, 0))],
        out_shape=[_sds((D, ZW), BF16), _sds((D, 2 * D_FF), BF16), _sds((D, 2 * D), BF16)],
        compiler_params=_cp(1),
    )(ga)


def assemble_bcd(gb, gc, gd):
    def body(gb_ref, gc_ref, gd_ref, wo_ref, wxq_ref, wxo_ref, wdn_ref, wk_ref, wv_ref, wpa_ref, wpr_ref, wq_ref, wc_ref):
        half = jnp.zeros((MLA_KVR, HP - MLA_NOPE), BF16)
        for j in range(N_DEV):
            wo_ref[j * 128:(j + 1) * 128, :] = gb_ref[j, B_OUT:B_XQ, :]
            wxq_ref[j * 128:(j + 1) * 128, :] = gb_ref[j, B_XQ:B_XO, :]
            wxo_ref[j * 128:(j + 1) * 128, :] = gb_ref[j, B_XO:B_DN, :]
            wdn_ref[j * RS_DN:(j + 1) * RS_DN, :] = gb_ref[j, B_DN:B_ROWS, :]
            wk_ref[:, j * HP:j * HP + MLA_NOPE] = gc_ref[j, C_UKV:C_PMLA, 0:MLA_NOPE]
            wk_ref[:, j * HP + MLA_NOPE:(j + 1) * HP] = half
            wv_ref[:, j * HP:j * HP + MLA_V] = gc_ref[j, C_UKV:C_PMLA, MLA_NOPE:HP]
            wv_ref[:, j * HP + MLA_V:(j + 1) * HP] = half
            wpr_ref[:, j * 128:(j + 1) * 128] = gc_ref[j, C_PRET:C_UQ, :]
            wq_ref[:, j * HP:j * HP + NS_UQ] = gc_ref[j, C_UQ:C_ROWS, 0:NS_UQ]
            wq_ref[:, j * HP + NS_UQ:(j + 1) * HP] = jnp.zeros((MLA_QR, HP - NS_UQ), BF16)
            wc_ref[:, j * NS_UP:(j + 1) * NS_UP] = gd_ref[j, 0:3, 0:NS_UP]
            for h in range(MLA_H):
                wpa_ref[h * HP:h * HP + MLA_V, j * 128:(j + 1) * 128] = gc_ref[j, C_PMLA + h * MLA_V:C_PMLA + (h + 1) * MLA_V, :]
        for h in range(MLA_H):
            wpa_ref[h * HP + MLA_V:(h + 1) * HP, :] = jnp.zeros((HP - MLA_V, D), BF16)

    return pl.pallas_call(
        body,
        name="assemble_bcd",
        in_specs=[VM] * 3,
        out_specs=[VM] * 10,
        out_shape=[_sds((D, D), BF16), _sds((D, D), BF16), _sds((D, D), BF16), _sds((D_FF, D), BF16),
                   _sds((MLA_KVR, MLA_H * HP), BF16), _sds((MLA_KVR, MLA_H * HP), BF16), _sds((MLA_H * HP, D), BF16),
                   _sds((RET_H * RET_D, D), BF16), _sds((MLA_QR, MLA_H * HP), BF16), _sds((3, 2 * D_FF), F32)],
        compiler_params=pltpu.CompilerParams(vmem_limit_bytes=VMEM_LIMIT),
    )(gb, gc, gd)


def slots_a(dwz, dwup_a, dwup_b, dwxkv, *, tile):
    per_half = D_FF // NS_UP

    def body(dz_ref, ua_ref, ub_ref, x_ref, s_ref):
        for os_, ln_, ds_ in WZ_RUNS:
            for j, off, d, ln in _pieces(os_, ln_, ds_, NS_IN):
                s_ref[j, :, A_IN + off:A_IN + off + ln] = dz_ref[:, d:d + ln].astype(BF16)
        for j in range(N_DEV):
            src = ua_ref if j < per_half else ub_ref
            c0 = (j % per_half) * NS_UP
            s_ref[j, :, A_UP:A_XKV] = src[:, c0:c0 + NS_UP].astype(BF16)
            s_ref[j, :, A_XKV:A_END] = x_ref[:, j * NS_XKV:(j + 1) * NS_XKV].astype(BF16)
            s_ref[j, :, A_END:A_W] = jnp.zeros((tile, A_W - A_END), BF16)

    return pl.pallas_call(
        body,
        name="slots_a",
        grid=(D // tile,),
        in_specs=[pl.BlockSpec((tile, ZW), lambda i: (i, 0)), pl.BlockSpec((tile, D_FF), lambda i: (i, 0)),
                  pl.BlockSpec((tile, D_FF), lambda i: (i, 0)), pl.BlockSpec((tile, 2 * D), lambda i: (i, 0))],
        out_specs=pl.BlockSpec((N_DEV, tile, A_W), lambda i: (0, i, 0)),
        out_shape=_sds((N_DEV, D, A_W), BF16),
        compiler_params=_cp(1),
    )(dwz, dwup_a, dwup_b, dwxkv)


def slots_bc(dwo, dwxq, dwxo, dwdn, dwk, dwv, dwpa, dwpr, dwq):
    def body(o_ref, xq_ref, xo_ref, dn_ref, k_ref, v_ref, pa_ref, pr_ref, q_ref, b_ref, c_ref):
        b_ref[B_OUT:B_XQ, :] = o_ref[...].astype(BF16)
        b_ref[B_XQ:B_XO, :] = xq_ref[...].astype(BF16)
        b_ref[B_XO:B_DN, :] = xo_ref[...].astype(BF16)
        b_ref[B_DN:B_ROWS, :] = dn_ref[...].astype(BF16)
        c_ref[C_UKV:C_PMLA, 0:MLA_NOPE] = k_ref[:, 0:MLA_NOPE].astype(BF16)
        c_ref[C_UKV:C_PMLA, MLA_NOPE:HP] = v_ref[:, 0:MLA_V].astype(BF16)
        for h in range(MLA_H):
            c_ref[C_PMLA + h * MLA_V:C_PMLA + (h + 1) * MLA_V, :] = pa_ref[h * HP:h * HP + MLA_V, :].astype(BF16)
        c_ref[C_PRET:C_UQ, :] = pr_ref[...].astype(BF16)
        c_ref[C_UQ:C_ROWS, 0:NS_UQ] = q_ref[:, 0:NS_UQ].astype(BF16)
        c_ref[C_UQ:C_ROWS, NS_UQ:HP] = jnp.zeros((C_ROWS - C_UQ, HP - NS_UQ), BF16)

    rows128 = pl.BlockSpec((128, D), lambda j: (j, 0))

    def cols(r):
        return pl.BlockSpec((r, 128), lambda j: (0, j))

    return pl.pallas_call(
        body,
        name="slots_bc",
        grid=(N_DEV,),
        in_specs=[rows128, rows128, rows128, pl.BlockSpec((RS_DN, D), lambda j: (j, 0)),
                  cols(MLA_KVR), cols(MLA_KVR), cols(MLA_H * HP), cols(RET_H * RET_D), cols(MLA_QR)],
        out_specs=[pl.BlockSpec((None, B_ROWS, D), lambda j: (j, 0, 0)), pl.BlockSpec((None, C_ROWS, HP), lambda j: (j, 0, 0))],
        out_shape=[_sds((N_DEV, B_ROWS, D), BF16), _sds((N_DEV, C_ROWS, HP), BF16)],
        compiler_params=_cp(1),
    )(dwo, dwxq, dwxo, dwdn, dwk, dwv, dwpa, dwpr, dwq)


SMALL = (("g_mix", 1024, 0), ("b_gate", 2048, 1), ("g_q_lat", 256, 3), ("g_kv_lat", 128, 4), ("g_ret", 512, 5),
         ("g_cross", 1024, 6), ("g_mem", 1024, 7), ("g_ffn", 1024, 8), ("b_conv", 5632, 9), ("g_final", 1024, 15))
S_ROWS = 16


def _flat_pieces(n, row0):
    return [(row0 + k // D, k, min(D, n - k)) for k in range(0, n, D)]


def _peer(m):
    x, y, c = lax.axis_index("x"), lax.axis_index("y"), lax.axis_index("c")
    mx, my, mc = (m >> 2) & 1, (m >> 1) & 1, m & 1
    px = 1 - x if mx else x
    py = 1 - y if my else y
    pc = 1 - c if mc else c
    return (px, py, pc), 4 * px + 2 * py + pc


def _my_id():
    return 4 * lax.axis_index("x") + 2 * lax.axis_index("y") + lax.axis_index("c")


def all_gather_groups(parts):
    n = len(parts)

    def body(*refs):
        srcs, outs = refs[:n], refs[n:2 * n]
        send_sems, recv_sems, local_sems = refs[2 * n:]
        me = _my_id()
        mine = []
        for g in range(n):
            cp = pltpu.make_async_copy(srcs[g], outs[g].at[me], local_sems.at[g])
            cp.start()
            mine.append(cp)
        sends = []
        for m in range(1, N_DEV):
            peer, plin = _peer(m)
            for g in range(n):
                cp = pltpu.make_async_remote_copy(
                    src_ref=srcs[g], dst_ref=outs[g].at[me], send_sem=send_sems.at[g, m - 1],
                    recv_sem=recv_sems.at[g, m - 1], device_id=peer, device_id_type=MESH)
                cp.start()
                sends.append(cp)
        for m in range(1, N_DEV):
            peer, plin = _peer(m)
            for g in range(n):
                pltpu.make_async_remote_copy(
                    src_ref=srcs[g], dst_ref=outs[g].at[plin], send_sem=send_sems.at[g, m - 1],
                    recv_sem=recv_sems.at[g, m - 1], device_id=peer, device_id_type=MESH).wait_recv()
        for cp in sends:
            cp.wait_send()
        for cp in mine:
            cp.wait()

    return pl.pallas_call(
        body,
        name="all_gather_groups",
        in_specs=[ANY] * n,
        out_specs=[ANY] * n,
        out_shape=[_sds((N_DEV,) + p.shape, p.dtype) for p in parts],
        scratch_shapes=[pltpu.SemaphoreType.DMA((n, N_DEV - 1)), pltpu.SemaphoreType.DMA((n, N_DEV - 1)),
                        pltpu.SemaphoreType.DMA((n,))],
    )(*parts)


def exchange_grads(sa, sb, sc, cs_a, cs_b, smalls):
    ns = len(smalls)

    def body(*refs):
        sa_ref, sb_ref, sc_ref, ca_ref, cb_ref = refs[:5]
        small_refs = refs[5:5 + ns]
        ra_ref, rb_ref, rc_ref, rd_ref, rs_ref, dsl_ref, own_ref = refs[5 + ns:12 + ns]
        send_sems, recv_sems = refs[12 + ns:]
        own_ref[...] = jnp.zeros(own_ref.shape, F32)
        for (name, n, row0), g_ref in zip(SMALL, small_refs):
            if name == "b_conv":
                continue
            for r, c0, ln in _flat_pieces(n, row0):
                own_ref[r:r + 1, 0:ln] = g_ref[:, c0:c0 + ln]
        row0 = dict((s[0], s[2]) for s in SMALL)["b_conv"]
        for half, c_ref in enumerate((ca_ref, cb_ref)):
            k = half * D_FF
            end = k + D_FF
            while k < end:
                r, lane = row0 + k // D, k % D
                ln = min(D - lane, end - k)
                own_ref[r:r + 1, lane:lane + ln] = c_ref[3:4, k - half * D_FF:k - half * D_FF + ln]
                k += ln
        dsl_ref[...] = jnp.zeros(dsl_ref.shape, F32)
        per_half = D_FF // NS_UP
        for j in range(N_DEV):
            c_ref = ca_ref if j < per_half else cb_ref
            c0 = (j % per_half) * NS_UP
            dsl_ref[j, 0:3, 0:NS_UP] = c_ref[0:3, c0:c0 + NS_UP]
        copies = []
        for m in range(1, N_DEV):
            peer, plin = _peer(m)
            pairs = ((sa_ref.at[plin], ra_ref), (sb_ref.at[plin], rb_ref), (sc_ref.at[plin], rc_ref),
                     (dsl_ref.at[plin], rd_ref), (own_ref, rs_ref))
            for g, (src, dst) in enumerate(pairs):
                cp = pltpu.make_async_remote_copy(
                    src_ref=src, dst_ref=dst.at[m - 1], send_sem=send_sems.at[g, m - 1], recv_sem=recv_sems.at[g, m - 1],
                    device_id=peer, device_id_type=MESH)
                cp.start()
                copies.append(cp)
        for cp in copies:
            cp.wait_recv()
        for cp in copies:
            cp.wait_send()

    n1 = N_DEV - 1
    return pl.pallas_call(
        body,
        name="exchange_grads",
        in_specs=[ANY, ANY, ANY] + [VM] * (2 + ns),
        out_specs=[ANY, ANY, ANY, VM, VM, VM, VM],
        out_shape=[_sds((n1,) + sa.shape[1:], BF16), _sds((n1,) + sb.shape[1:], BF16), _sds((n1,) + sc.shape[1:], BF16),
                   _sds((n1,) + D_SHAPE, F32), _sds((n1, S_ROWS, D), F32), _sds((N_DEV,) + D_SHAPE, F32),
                   _sds((S_ROWS, D), F32)],
        scratch_shapes=[pltpu.SemaphoreType.DMA((5, n1)), pltpu.SemaphoreType.DMA((5, n1))],
        compiler_params=pltpu.CompilerParams(vmem_limit_bytes=VMEM_LIMIT),
    )(sa, sb, sc, cs_a, cs_b, *smalls)


def _adamw(w, g, m, v):
    m = ADAM_B1 * m + (1.0 - ADAM_B1) * g
    v = ADAM_B2 * v + (1.0 - ADAM_B2) * (g * g)
    m_hat = m / (1.0 - ADAM_B1 ** ADAM_STEP)
    v_hat = v / (1.0 - ADAM_B2 ** ADAM_STEP)
    delta = -ADAM_LR * (m_hat / (jnp.sqrt(v_hat) + ADAM_EPS) + ADAM_WD * w)
    return delta, m, v


def _sum_slots(own_ref, recv_ref):
    g = own_ref[...].astype(F32)
    for k in range(N_DEV - 1):
        g = g + recv_ref[k].astype(F32)
    return g


def _apply(g, refs, outs):
    d, mn, vn = _adamw(refs[0][...], g, refs[1][...], refs[2][...])
    outs[0][...] = g
    outs[1][...] = d
    outs[2][...] = mn
    outs[3][...] = vn


def adam_a(own, recv, wmv, *, tile):
    widths = (NS_IN, NS_UP, NS_XKV)
    starts = (A_IN, A_UP, A_XKV)

    def body(*refs):
        own_ref, recv_ref = refs[:2]
        ins, outs = refs[2:11], refs[11:]
        g = _sum_slots(own_ref, recv_ref)
        for t in range(3):
            _apply(g[:, starts[t]:starts[t] + widths[t]], ins[3 * t:3 * t + 3], outs[4 * t:4 * t + 4])

    def blk(w):
        return pl.BlockSpec((tile, w), lambda i: (i, 0))

    return pl.pallas_call(
        body,
        name="adam_a",
        grid=(D // tile,),
        in_specs=[blk(A_W), pl.BlockSpec((N_DEV - 1, tile, A_W), lambda i: (0, i, 0))] + [blk(w) for w in widths for _ in range(3)],
        out_specs=[blk(w) for w in widths for _ in range(4)],
        out_shape=[_sds((D, w), F32) for w in widths for _ in range(4)],
        compiler_params=_cp(1),
    )(own, recv, *wmv)


def adam_b(own, recv, wmv):
    spans = ((B_OUT, B_XQ), (B_XQ, B_XO), (B_XO, B_DN), (B_DN, B_ROWS))

    def body(*refs):
        own_ref, recv_ref = refs[:2]
        ins, outs = refs[2:14], refs[14:]
        for t, (lo, hi) in enumerate(spans):
            g = own_ref[lo:hi, :].astype(F32)
            for k in range(N_DEV - 1):
                g = g + recv_ref[k, lo:hi, :].astype(F32)
            _apply(g, ins[3 * t:3 * t + 3], outs[4 * t:4 * t + 4])

    return pl.pallas_call(
        body,
        name="adam_b",
        in_specs=[VM] * 14,
        out_specs=[VM] * 16,
        out_shape=[_sds((hi - lo, D), F32) for lo, hi in spans for _ in range(4)],
        compiler_params=pltpu.CompilerParams(vmem_limit_bytes=VMEM_LIMIT),
    )(own, recv, *wmv)


def adam_c(own, recv, wmv):
    spans = ((C_UKV, C_PMLA, HP), (C_PMLA, C_PRET, HP), (C_PRET, C_UQ, HP), (C_UQ, C_ROWS, NS_UQ))

    def body(*refs):
        own_ref, recv_ref = refs[:2]
        ins, outs = refs[2:14], refs[14:]
        for t, (lo, hi, w) in enumerate(spans):
            g = own_ref[lo:hi, :].astype(F32)
            for k in range(N_DEV - 1):
                g = g + recv_ref[k, lo:hi, :].astype(F32)
            _apply(g[:, 0:w], ins[3 * t:3 * t + 3], outs[4 * t:4 * t + 4])

    return pl.pallas_call(
        body,
        name="adam_c",
        in_specs=[VM] * 14,
        out_specs=[VM] * 16,
        out_shape=[_sds((hi - lo, w), F32) for lo, hi, w in spans for _ in range(4)],
        compiler_params=pltpu.CompilerParams(vmem_limit_bytes=VMEM_LIMIT),
    )(own, recv, *wmv)


def adam_small(own_s, recv_s, dslots, recv_d, wmv_small, wmv_conv):
    ns = len(SMALL)

    def body(*refs):
        own_ref, rs_ref, dsl_ref, rd_ref = refs[:4]
        ins = refs[4:4 + 3 * ns + 3]
        outs = refs[4 + 3 * ns + 3:4 + 3 * ns + 3 + 4 * ns + 4]
        all_sc = refs[-1]
        me = _my_id()
        all_sc[0] = own_ref[...]
        for k in range(N_DEV - 1):
            all_sc[k + 1] = rs_ref[k]
        g = all_sc[jnp.bitwise_xor(me, 0)]
        for s in range(1, N_DEV):
            g = g + all_sc[jnp.bitwise_xor(me, s)]
        all_sc[0] = g
        for t, (name, n, row0) in enumerate(SMALL):
            pieces = [all_sc[0, r:r + 1, 0:ln] for r, _, ln in _flat_pieces(n, row0)]
            gt = pieces[0] if len(pieces) == 1 else jnp.concatenate(pieces, axis=1)
            _apply(gt, ins[3 * t:3 * t + 3], outs[4 * t:4 * t + 4])
        gc = dsl_ref[me]
        for k in range(N_DEV - 1):
            gc = gc + rd_ref[k]
        _apply(gc[0:3, 0:NS_UP], ins[3 * ns:3 * ns + 3], outs[4 * ns:4 * ns + 4])

    out_shape = [_sds((1, n), F32) for _, n, _ in SMALL for _ in range(4)] + [_sds((3, NS_UP), F32)] * 4
    return pl.pallas_call(
        body,
        name="adam_small",
        in_specs=[VM] * (4 + 3 * ns + 3),
        out_specs=[VM] * len(out_shape),
        out_shape=out_shape,
        scratch_shapes=[pltpu.VMEM((N_DEV, S_ROWS, D), F32)],
    )(own_s, recv_s, dslots, recv_d, *wmv_small, *wmv_conv)


def kernel(x, mem, positions, g_mix, w_in, b_gate, g_q_lat, w_uq, g_kv_lat, w_ukv, w_proj_mla, g_ret, w_proj_ret, w_out, g_cross, g_mem, w_xq, w_xkv, w_xo, g_ffn, w_up, w_conv, b_conv, w_down, g_final, loss_target, m_g_mix, m_w_in, m_b_gate, m_g_q_lat, m_w_uq, m_g_kv_lat, m_w_ukv, m_w_proj_mla, m_g_ret, m_w_proj_ret, m_w_out, m_g_cross, m_g_mem, m_w_xq, m_w_xkv, m_w_xo, m_g_ffn, m_w_up, m_w_conv, m_b_conv, m_w_down, m_g_final, v_g_mix, v_w_in, v_b_gate, v_g_q_lat, v_w_uq, v_g_kv_lat, v_w_ukv, v_w_proj_mla, v_g_ret, v_w_proj_ret, v_w_out, v_g_cross, v_g_mem, v_w_xq, v_w_xkv, v_w_xo, v_g_ffn, v_w_up, v_w_conv, v_b_conv, v_w_down, v_g_final):
    args = dict(locals())
    T = x.shape[1]
    M = mem.shape[1]
    tile = min(256, T)
    tq = min(512, T)
    rb = min(1024, T)

    xs = x[0]
    tgt = loss_target[0]
    mems = mem[0]

    def shard(name, prefix=""):
        a = args[prefix + name]
        return a.reshape(a.shape[-2:]) if a.ndim >= 2 else a.reshape(1, -1)

    pa = pack_a(shard("w_in"), shard("w_up"), shard("w_xkv"), tile=tile)
    pb, pc, pd = pack_bcd(shard("w_out"), shard("w_xq"), shard("w_xo"), shard("w_down"), shard("w_ukv"),
                          shard("w_proj_mla"), shard("w_proj_ret"), shard("w_uq"), shard("w_conv"))
    ga, gb, gc, gd = all_gather_groups([pa, pb, pc, pd])
    wz, wup, wxkv = assemble_a(ga, tile=tile)
    wo, wxq, wxo, wdn, wk, wv, wpa, wpr, wq, wcv = assemble_bcd(gb, gc, gd)

    pos_f = jnp.broadcast_to(positions[0].astype(F32)[:, None], (T, 128))
    cm, s1, s2, cr, sr = rot_tables(pos_f, tile=tile)
    rconsts = _ret_consts()

    u = rowwise(lambda xv, g: _rms(xv, g), [(xs, None)], [g_mix], [(D, BF16)], [], tile=tile, name="norm_mix")[0]
    z = matmul(u, wz, name="mm_z")

    def lat_fwd(zl, cmv, s1v, s2v, gq, gkv, wqv, wkv_, wvv):
        cq = _rms(zl[:, 0:256], gq).astype(BF16)
        ckv = _rms(zl[:, 256:384], gkv).astype(BF16)
        qv = _rot_mla(_dot(cq, wqv), cmv, s1v, s2v)
        kr = _rot_mla(zl[:, 384:512], cmv, s1v, s2v)
        kn = _dot(ckv, wkv_)
        kv_ = jnp.concatenate([kn[:, h * HP:(h + 1) * HP] + kr for h in range(MLA_H)], axis=1)
        vv = _dot(ckv, wvv)
        lane = lax.broadcasted_iota(jnp.int32, vv.shape, 1)
        vv = jnp.where((lane & (HP - 1)) == ONE_LANE, 1.0, vv)
        return qv, kv_, vv

    q_a, k_a, v_a = rowwise(
        lat_fwd, [(z, (512, 8)), (cm, None), (s1, None), (s2, None)], [g_q_lat, g_kv_lat, wq, wk, wv],
        [(MLA_H * HP, BF16)] * 3, [], tile=tile, name="lat_fwd")

    def retprep_fwd(zr, crv, srv):
        rqv = _rot_ret(zr[:, 0:512], crv, srv)
        rkv = _rot_ret(zr[:, 512:1024], crv, srv) * (RET_D ** -0.5)
        return rqv, rkv, zr[:, 1024:1536]

    rq, rk, rv = rowwise(retprep_fwd, [(z, (2048, 0)), (cr, None), (sr, None)], [], [(512, BF16)] * 3, [],
                         tile=tile, name="retprep_fwd")

    o_a, lse = mla_fwd(q_a, k_a, v_a, tq=tq)
    ret, rstate = ret_fwd(rq, rk, rv, rconsts, rb=rb)

    def gn_parts(r):
        outs = []
        for h in range(RET_H):
            rh = r[:, h * RET_D:(h + 1) * RET_D]
            mu = jnp.mean(rh, axis=-1, keepdims=True)
            dlt = rh - mu
            rstd = lax.rsqrt(jnp.mean(dlt * dlt, axis=-1, keepdims=True) + EPS)
            outs.append((dlt * rstd, rstd))
        return outs

    def mix_fwd(ov, rv_, rg, gt, wpav, wprv, gr, bg):
        ya = _dot(ov, wpav)
        xh = jnp.concatenate([p[0] for p in gn_parts(rv_)], axis=1)
        t = rg * _sigmoid(rg) * (xh * gr)
        yr = _dot(t.astype(BF16), wprv)
        ga_ = _sigmoid(gt[:, :D] + bg[:, :D])
        gr_ = _sigmoid(gt[:, D:] + bg[:, D:])
        return ga_ * ya + gr_ * yr

    mix = rowwise(mix_fwd, [(o_a, None), (ret, None), (z, (512, 3)), (z, (2048, 1))], [wpa, wpr, g_ret, b_gate],
                  [(D, BF16)], [], tile=tile, name="mix_fwd")[0]
    h1 = matmul(mix, wo, res=xs, name="mm_out")
    n2 = rowwise(lambda hv, g: _rms(hv, g), [(h1, None)], [g_cross], [(D, BF16)], [], tile=tile, name="norm_cross")[0]
    xq = matmul(n2, wxq, out_dtype=BF16, name="mm_xq")
    mn = rowwise(lambda mv_, g: _rms(mv_, g), [(mems, None)], [g_mem], [(D, BF16)], [], tile=min(tile, M), name="norm_mem")[0]
    mkv = matmul(mn, wxkv, out_dtype=BF16, name="mm_mkv")

    x_scale = X_HD ** -0.5

    def xattn_fwd(xqv, mkvv):
        outs = []
        for h in range(X_H):
            sl = slice(h * X_HD, (h + 1) * X_HD)
            s = _dot_nt(xqv[:, sl], mkvv[:, sl]) * x_scale
            s = s - jnp.max(s, axis=-1, keepdims=True)
            e = jnp.exp(s)
            p = e / jnp.sum(e, axis=-1, keepdims=True)
            outs.append(_dot(p.astype(BF16), mkvv[:, D + h * X_HD:D + (h + 1) * X_HD]))
        return jnp.concatenate(outs, axis=1)

    xo = rowwise(xattn_fwd, [(xq, None)], [mkv], [(D, BF16)], [], tile=tile, name="xattn_fwd")[0]
    h2 = matmul(xo, wxo, res=h1, name="mm_xo")
    n3 = rowwise(lambda hv, g: _rms(hv, g), [(h2, None)], [g_ffn], [(D, BF16)], [], tile=tile, name="norm_ffn")[0]
    up_pre = matmul(n3, wup, out_dtype=BF16, name="mm_up")
    cw = D_FF // 2
    act = conv_act_fwd(up_pre, wcv, b_conv, tile=tile, cw=cw)
    h3 = matmul(act, wdn, res=h2, name="mm_down")

    def loss_fn(hv, tv, g):
        y = _rms(hv, g)
        err = y - tv
        part = 0.5 * jnp.sum(jnp.sum(err * err, axis=-1, keepdims=True) / D, axis=0, keepdims=True)
        dx, dg = _rms_bwd(hv, g, err / D)
        return dx, dg, jnp.broadcast_to(part, (8, 128))

    g_fin2 = g_final.reshape(1, D)
    dh3, dg_final, loss_acc = rowwise(loss_fn, [(h3, None), (tgt, None)], [g_fin2], [(D, F32)],
                                      [((1, D), F32), ((8, 128), F32)], tile=tile, name="loss_bwd")
    loss = lax.psum(loss_acc[0, 0], ("x", "y", "c"))

    dact = matmul(dh3, wdn, tb=True, out_dtype=BF16, name="mm_dact")
    dw_down = matmul_tn(act, dh3, name="mm_dw_down", tm=1408)
    dup_a, dup_b, cs_a, cs_b = conv_act_bwd(up_pre, dact, wcv, b_conv, tile=tile, cw=cw)
    dn3 = matmul2_tb(dup_a, dup_b, wup, name="mm_dn3")
    dw_up_a = matmul_tn(n3, dup_a, name="mm_dw_up_a", tn=1408)
    dw_up_b = matmul_tn(n3, dup_b, name="mm_dw_up_b", tn=1408)

    def norm_bwd(hv, dyv, drv, g):
        dx, dg = _rms_bwd(hv, g, dyv)
        return dx + drv, dg

    dh2, dg_ffn = rowwise(norm_bwd, [(h2, None), (dn3, None), (dh3, None)], [g_ffn], [(D, F32)], [((1, D), F32)],
                          tile=tile, name="norm_ffn_bwd")
    dxo = matmul(dh2, wxo, tb=True, out_dtype=BF16, name="mm_dxo")
    dw_xo = matmul_tn(xo, dh2, name="mm_dw_xo")

    def xattn_bwd(xqv, dxov, mkvv):
        dxq, dmk, dmv = [], [], []
        for h in range(X_H):
            sl = slice(h * X_HD, (h + 1) * X_HD)
            slv = slice(D + h * X_HD, D + (h + 1) * X_HD)
            s = _dot_nt(xqv[:, sl], mkvv[:, sl]) * x_scale
            s = s - jnp.max(s, axis=-1, keepdims=True)
            e = jnp.exp(s)
            p = e / jnp.sum(e, axis=-1, keepdims=True)
            dp = _dot_nt(dxov[:, sl], mkvv[:, slv])
            ds = (p * (dp - jnp.sum(dp * p, axis=-1, keepdims=True)) * x_scale).astype(BF16)
            dxq.append(_dot(ds, mkvv[:, sl]))
            dmk.append(_dot_tn(ds, xqv[:, sl]))
            dmv.append(_dot_tn(p.astype(BF16), dxov[:, sl]))
        return jnp.concatenate(dxq, axis=1), jnp.concatenate(dmk + dmv, axis=1)

    dxq, dmkv = rowwise(xattn_bwd, [(xq, None), (dxo, None)], [mkv], [(D, BF16)], [((M, 2 * D), F32)],
                        tile=tile, name="xattn_bwd")
    dn2 = matmul(dxq, wxq, tb=True, name="mm_dn2")
    dw_xq = matmul_tn(n2, dxq, name="mm_dw_xq")
    dh1, dg_cross = rowwise(norm_bwd, [(h1, None), (dn2, None), (dh2, None)], [g_cross], [(D, F32)], [((1, D), F32)],
                            tile=tile, name="norm_cross_bwd")
    dw_xkv = matmul_tn(mn, dmkv, name="mm_dw_xkv", tk=M)
    dmn = matmul(dmkv, wxkv, tb=True, name="mm_dmn", tm=M)
    dg_mem = rowwise(lambda mv_, dyv, g: _rms_bwd(mv_, g, dyv)[1], [(mems, None), (dmn, None)], [g_mem], [],
                     [((1, D), F32)], tile=min(tile, M), name="norm_mem_bwd")[0]

    dmix = matmul(dh1, wo, tb=True, out_dtype=BF16, name="mm_dmix")
    dw_out = matmul_tn(mix, dh1, name="mm_dw_out")

    def mix_bwd(ov, rv_, rg, gt, dmv_, wpav, wprv, gr, bg):
        dm_ = dmv_.astype(F32)
        ya = _dot(ov, wpav)
        parts = gn_parts(rv_)
        xh = jnp.concatenate([p[0] for p in parts], axis=1)
        yn = xh * gr
        sg = _sigmoid(rg)
        sl_ = rg * sg
        t = (sl_ * yn).astype(BF16)
        yr = _dot(t, wprv)
        ga_ = _sigmoid(gt[:, :D] + bg[:, :D])
        gr_ = _sigmoid(gt[:, D:] + bg[:, D:])
        dgates = jnp.concatenate([dm_ * ya * ga_ * (1.0 - ga_), dm_ * yr * gr_ * (1.0 - gr_)], axis=1)
        dya = (dm_ * ga_).astype(BF16)
        dyr = (dm_ * gr_).astype(BF16)
        do_ = _dot_nt(dya, wpav)
        dwpa_ = _dot_tn(ov, dya)
        dt = _dot_nt(dyr, wprv)
        dwpr_ = _dot_tn(t, dyr)
        drg_ = dt * yn * (sg * (1.0 + rg * (1.0 - sg)))
        dyn = dt * sl_
        dgr = jnp.sum(dyn * xh, axis=0, keepdims=True)
        dxh = dyn * gr
        drets = []
        for h in range(RET_H):
            sl = slice(h * RET_D, (h + 1) * RET_D)
            xhh, rstd = parts[h]
            dxhh = dxh[:, sl]
            drets.append(rstd * (dxhh - jnp.mean(dxhh, axis=-1, keepdims=True)
                                 - xhh * jnp.mean(dxhh * xhh, axis=-1, keepdims=True)))
        dret_ = jnp.concatenate(drets, axis=1)
        dbg = jnp.sum(dgates, axis=0, keepdims=True)
        return do_, dret_, drg_, dgates, dwpa_, dwpr_, dgr, dbg

    do_a, dret, drg, dgates, dwpa, dw_proj_ret, dg_ret, db_gate = rowwise(
        mix_bwd, [(o_a, None), (ret, None), (z, (512, 3)), (z, (2048, 1)), (dmix, None)], [wpa, wpr, g_ret, b_gate],
        [(MLA_H * HP, BF16), (512, F32), (512, BF16), (2 * D, BF16)],
        [((MLA_H * HP, D), F32), ((512, D), F32), ((1, 512), F32), ((1, 2 * D), F32)], tile=tile, name="mix_bwd")

    delta, lse_row, delta_row = mla_prep(o_a, do_a, lse, tq=tq)
    dq_a = mla_dq(q_a, k_a, v_a, do_a, lse, delta, tq=tq)
    dk_a, dv_a = mla_dkv(q_a, k_a, v_a, do_a, lse_row, delta_row, tq=tq)
    drq_r, drk_r, drv = ret_bwd(rq, rk, rv, rstate, dret, rconsts, rb=rb)

    def lat_bwd(zl, cmv, s1v, s2v, dqv, dkv_, dvv, gq, gkv, wqv, wkv_, wvv):
        cqf, ckvf = zl[:, 0:256], zl[:, 256:384]
        cq = _rms(cqf, gq).astype(BF16)
        ckv = _rms(ckvf, gkv).astype(BF16)
        dq_pre = _rot_mla(dqv.astype(F32), cmv, -s1v, -s2v).astype(BF16)
        dkf = dkv_.astype(F32)
        dkr = dkf[:, 0:HP]
        for h in range(1, MLA_H):
            dkr = dkr + dkf[:, h * HP:(h + 1) * HP]
        lane = lax.broadcasted_iota(jnp.int32, dkr.shape, 1)
        dzk = _rot_mla(jnp.where((lane >= 64) & (lane < 96), dkr, 0.0), cmv, -s1v, -s2v)
        dkb = dkv_.astype(BF16)
        dvb = dvv.astype(BF16)
        dcq_n = _dot_nt(dq_pre, wqv)
        dckv_n = _dot_nt(dkb, wkv_) + _dot_nt(dvb, wvv)
        dwq_ = _dot_tn(cq, dq_pre)
        dwk_ = _dot_tn(ckv, dkb)
        dwv_ = _dot_tn(ckv, dvb)
        dcq, dgq = _rms_bwd(cqf, gq, dcq_n)
        dckv, dgkv = _rms_bwd(ckvf, gkv, dckv_n)
        return jnp.concatenate([dcq, dckv, dzk], axis=1), dwq_, dwk_, dwv_, dgq, dgkv

    dz_lat, dwq, dwk, dwv, dg_q_lat, dg_kv_lat = rowwise(
        lat_bwd, [(z, (512, 8)), (cm, None), (s1, None), (s2, None), (dq_a, None), (dk_a, None), (dv_a, None)],
        [g_q_lat, g_kv_lat, wq, wk, wv], [(512, BF16)],
        [((MLA_QR, MLA_H * HP), F32), ((MLA_KVR, MLA_H * HP), F32), ((MLA_KVR, MLA_H * HP), F32),
         ((1, MLA_QR), F32), ((1, MLA_KVR), F32)], tile=tile, name="lat_bwd")

    def dz_assemble(dq_, dk_, dv_, drg_, dgt, dzl, crv, srv):
        a = _rot_ret(dq_, crv, -srv)
        b = _rot_ret(dk_, crv, -srv) * (RET_D ** -0.5)
        return jnp.concatenate([a, b, dv_, drg_.astype(F32), dgt.astype(F32), dzl.astype(F32)], axis=1)

    dz = rowwise(dz_assemble, [(drq_r, None), (drk_r, None), (drv, None), (drg, None), (dgates, None), (dz_lat, None),
                               (cr, None), (sr, None)], [], [(ZW, BF16)], [], tile=tile, name="dz_assemble")[0]
    du = matmul(dz, wz, tb=True, name="mm_du", tm=512)
    dwz = matmul_tn(u, dz, name="mm_dw_z")

    grad_x, dg_mix = rowwise(norm_bwd, [(xs, None), (du, None), (dh1, None)], [g_mix], [(D, F32)], [((1, D), F32)],
                             tile=tile, name="norm_mix_bwd")

    sa = slots_a(dwz, dw_up_a, dw_up_b, dw_xkv, tile=tile)
    sb, sc = slots_bc(dw_out, dw_xq, dw_xo, dw_down, dwk, dwv, dwpa, dw_proj_ret, dwq)
    small_grads = {"g_mix": dg_mix, "b_gate": db_gate, "g_q_lat": dg_q_lat, "g_kv_lat": dg_kv_lat, "g_ret": dg_ret,
                   "g_cross": dg_cross, "g_mem": dg_mem, "g_ffn": dg_ffn, "b_conv": dg_ffn, "g_final": dg_final}
    ra, rb_, rc, rd, rs, dslots, own_s = exchange_grads(sa, sb, sc, cs_a, cs_b, [small_grads[n] for n, _, _ in SMALL])

    me = _my_id()
    own_a = lax.dynamic_index_in_dim(sa, me, axis=0, keepdims=False)
    own_b = lax.dynamic_index_in_dim(sb, me, axis=0, keepdims=False)
    own_c = lax.dynamic_index_in_dim(sc, me, axis=0, keepdims=False)

    def wmv(names):
        return [shard(n, p) for n in names for p in ("", "m_", "v_")]

    res = {}
    names_a = ("w_in", "w_up", "w_xkv")
    names_b = ("w_out", "w_xq", "w_xo", "w_down")
    names_c = ("w_ukv", "w_proj_mla", "w_proj_ret", "w_uq")
    names_s = tuple(n for n, _, _ in SMALL) + ("w_conv",)
    for names, outs in ((names_a, adam_a(own_a, ra, wmv(names_a), tile=128)),
                        (names_b, adam_b(own_b, rb_, wmv(names_b))),
                        (names_c, adam_c(own_c, rc, wmv(names_c))),
                        (names_s, adam_small(own_s, rs, dslots, rd, wmv(names_s[:-1]), wmv(names_s[-1:])))):
        for t, n in enumerate(names):
            res[n] = outs[4 * t:4 * t + 4]

    order = ["g_mix", "w_in", "b_gate", "g_q_lat", "w_uq", "g_kv_lat", "w_ukv", "w_proj_mla", "g_ret", "w_proj_ret",
             "w_out", "g_cross", "g_mem", "w_xq", "w_xkv", "w_xo", "g_ffn", "w_up", "w_conv", "b_conv", "w_down",
             "g_final"]
    outs = [loss, grad_x[None]]
    for kind in range(4):
        outs += [res[n][kind].reshape(args[n].shape) for n in order]
    return tuple(outs)
```

```python
import functools
import math

import jax
import jax.numpy as jnp
import numpy as np
from jax import lax
from jax.experimental import pallas as pl
from jax.experimental.pallas import tpu as pltpu

F32 = jnp.float32
BF16 = jnp.bfloat16

D = 1024
MLA_H, MLA_NOPE, MLA_ROPE, MLA_V = 8, 64, 32, 64
MLA_QR, MLA_KVR = 256, 128
RET_H, RET_D, RET_C = 4, 128, 128
X_H, X_HD = 4, 256
D_FF = 2816
IN_W = 4512
THETA = 10000.0
EPS = 1e-6
HP = 128
ZW = 4608
N_DEV = 8

ADAM_LR, ADAM_B1, ADAM_B2, ADAM_EPS, ADAM_WD, ADAM_STEP = 0.001, 0.9, 0.999, 1e-08, 0.01, 10

VMEM_LIMIT = 56 * 1024 * 1024
MESH = pl.DeviceIdType.MESH
VM = pl.BlockSpec(memory_space=pltpu.VMEM)
ANY = pl.BlockSpec(memory_space=pl.ANY)


def _cp(n_axes):
    return pltpu.CompilerParams(dimension_semantics=("arbitrary",) * n_axes, vmem_limit_bytes=VMEM_LIMIT)


def _pick(n, cap, mult=128):
    best = None
    for t in range(mult, min(n, cap) + 1, mult):
        if n % t == 0:
            best = t
    return best if best is not None else n


def _dot(a, b):
    return jnp.dot(a, b, preferred_element_type=F32)


def _dot_nt(a, b):
    return lax.dot_general(a, b, (((1,), (1,)), ((), ())), preferred_element_type=F32)


def _dot_tn(a, b):
    return lax.dot_general(a, b, (((0,), (0,)), ((), ())), preferred_element_type=F32)


def _sds(shape, dtype):
    return jax.ShapeDtypeStruct(shape, dtype)


def matmul(a, b, *, name, tb=False, res=None, out_dtype=F32, tm=1024, tn=512):
    M, K = a.shape
    N = b.shape[0] if tb else b.shape[1]
    tm = _pick(M, tm, 8)
    tn = _pick(N, tn)

    def body(*refs):
        if res is None:
            a_ref, b_ref, o_ref = refs
        else:
            a_ref, b_ref, r_ref, o_ref = refs
        av = a_ref[...].astype(BF16)
        bv = b_ref[...].astype(BF16)
        acc = _dot_nt(av, bv) if tb else _dot(av, bv)
        if res is not None:
            acc = acc + r_ref[...].astype(F32)
        o_ref[...] = acc.astype(o_ref.dtype)

    in_specs = [
        pl.BlockSpec((tm, K), lambda i, j: (i, 0)),
        pl.BlockSpec((tn, K), lambda i, j: (j, 0)) if tb else pl.BlockSpec((K, tn), lambda i, j: (0, j)),
    ]
    args = [a, b]
    if res is not None:
        in_specs.append(pl.BlockSpec((tm, tn), lambda i, j: (i, j)))
        args.append(res)
    return pl.pallas_call(
        body,
        name=name,
        grid=(M // tm, N // tn),
        in_specs=in_specs,
        out_specs=pl.BlockSpec((tm, tn), lambda i, j: (i, j)),
        out_shape=_sds((M, N), out_dtype),
        compiler_params=_cp(2),
    )(*args)


def matmul2_tb(a1, a2, b, *, name, tm=512, tn=512):
    M, K1 = a1.shape
    N = b.shape[0]
    tm = _pick(M, tm, 8)
    tn = _pick(N, tn)

    def body(a1_ref, a2_ref, b1_ref, b2_ref, o_ref):
        o_ref[...] = (_dot_nt(a1_ref[...].astype(BF16), b1_ref[...].astype(BF16))
                      + _dot_nt(a2_ref[...].astype(BF16), b2_ref[...].astype(BF16)))

    return pl.pallas_call(
        body,
        name=name,
        grid=(M // tm, N // tn),
        in_specs=[pl.BlockSpec((tm, K1), lambda i, j: (i, 0)), pl.BlockSpec((tm, K1), lambda i, j: (i, 0)),
                  pl.BlockSpec((tn, K1), lambda i, j: (j, 0)), pl.BlockSpec((tn, K1), lambda i, j: (j, 1))],
        out_specs=pl.BlockSpec((tm, tn), lambda i, j: (i, j)),
        out_shape=_sds((M, N), F32),
        compiler_params=_cp(2),
    )(a1, a2, b, b)


def matmul_tn(a, b, *, name, tm=1024, tn=1024, tk=1024):
    R, M = a.shape
    N = b.shape[1]
    tm = _pick(M, tm)
    tn = _pick(N, tn)
    tk = _pick(R, tk, 16)
    nk = R // tk

    def body(a_ref, b_ref, o_ref, acc_ref):
        k = pl.program_id(2)

        @pl.when(k == 0)
        def _():
            acc_ref[...] = jnp.zeros_like(acc_ref)

        acc_ref[...] += _dot_tn(a_ref[...].astype(BF16), b_ref[...].astype(BF16))

        @pl.when(k == nk - 1)
        def _():
            o_ref[...] = acc_ref[...]

    return pl.pallas_call(
        body,
        name=name,
        grid=(M // tm, N // tn, nk),
        in_specs=[pl.BlockSpec((tk, tm), lambda i, j, k: (k, i)), pl.BlockSpec((tk, tn), lambda i, j, k: (k, j))],
        out_specs=pl.BlockSpec((tm, tn), lambda i, j, k: (i, j)),
        out_shape=_sds((M, N), F32),
        scratch_shapes=[pltpu.VMEM((tm, tn), F32)],
        compiler_params=_cp(3),
    )(a, b)


def rowwise(fn, rows, consts, out_rows, out_accs, *, tile, name):
    T = rows[0][0].shape[0]
    nt = T // tile
    n_r, n_c, n_o, n_a = len(rows), len(consts), len(out_rows), len(out_accs)

    def body(*refs):
        ins = [r[...] for r in refs[: n_r + n_c]]
        outs = fn(*ins)
        if not isinstance(outs, (tuple, list)):
            outs = (outs,)
        o_refs = refs[n_r + n_c : n_r + n_c + n_o]
        a_refs = refs[n_r + n_c + n_o :]
        for o_ref, o in zip(o_refs, outs[:n_o]):
            o_ref[...] = o.astype(o_ref.dtype)
        if n_a:
            first = pl.program_id(0) == 0

            @pl.when(first)
            def _():
                for a_ref, o in zip(a_refs, outs[n_o:]):
                    a_ref[...] = o.astype(a_ref.dtype)

            @pl.when(jnp.logical_not(first))
            def _():
                for a_ref, o in zip(a_refs, outs[n_o:]):
                    a_ref[...] += o.astype(a_ref.dtype)

    in_specs = []
    args = []
    for arr, win in rows:
        if win is None:
            in_specs.append(pl.BlockSpec((tile, arr.shape[1]), lambda i: (i, 0)))
        else:
            w, cb = win
            in_specs.append(pl.BlockSpec((tile, w), functools.partial(lambda i, cb: (i, cb), cb=cb)))
        args.append(arr)
    for c in consts:
        in_specs.append(pl.BlockSpec(c.shape, functools.partial(lambda i, nd: (0,) * nd, nd=c.ndim)))
        args.append(c)
    out_specs = [pl.BlockSpec((tile, w), lambda i: (i, 0)) for w, _ in out_rows]
    out_shape = [_sds((T, w), dt) for w, dt in out_rows]
    for shp, dt in out_accs:
        out_specs.append(pl.BlockSpec(shp, functools.partial(lambda i, nd: (0,) * nd, nd=len(shp))))
        out_shape.append(_sds(shp, dt))
    return pl.pallas_call(
        body,
        name=name,
        grid=(nt,),
        in_specs=in_specs,
        out_specs=out_specs,
        out_shape=out_shape,
        compiler_params=_cp(1),
    )(*args)


def _rms(x, g):
    r = lax.rsqrt(jnp.mean(x * x, axis=-1, keepdims=True) + EPS)
    return x * r * g


def _rms_bwd(x, g, dy):
    r = lax.rsqrt(jnp.mean(x * x, axis=-1, keepdims=True) + EPS)
    xh = x * r
    dg = jnp.sum(dy * xh, axis=0, keepdims=True)
    dxh = dy * g
    dx = r * (dxh - xh * jnp.mean(dxh * xh, axis=-1, keepdims=True))
    return dx, dg


def _sigmoid(x):
    return 1.0 / (1.0 + jnp.exp(-x))


def _rot_mla(x, c, s1, s2):
    n = x.shape[1] // HP
    outs = []
    for h in range(n):
        xh = x[:, h * HP : (h + 1) * HP]
        outs.append(xh * c + pltpu.roll(xh, HP - 16, 1) * s1 + pltpu.roll(xh, 16, 1) * s2)
    return outs[0] if n == 1 else jnp.concatenate(outs, axis=1)


def _rot_ret(x, c, s):
    n = x.shape[1] // RET_D
    outs = []
    for h in range(n):
        xh = x[:, h * RET_D : (h + 1) * RET_D]
        outs.append(xh * c + pltpu.roll(xh, RET_D // 2, 1) * s)
    return outs[0] if n == 1 else jnp.concatenate(outs, axis=1)


def rot_tables(pos_f, *, tile):
    lane_np = np.arange(128)
    inv_m = (jnp.asarray(THETA, F32) ** (-jnp.asarray(lane_np & 15, F32) / 16.0)).reshape(1, 128)
    inv_r = (jnp.asarray(THETA, F32) ** (-jnp.asarray(lane_np & 63, F32) / 64.0)).reshape(1, 128)

    def fn(p, im, ir):
        lane = lax.broadcasted_iota(jnp.int32, p.shape, 1)
        ang = p * im
        cm = jnp.where((lane >= 64) & (lane < 96), jnp.cos(ang), 1.0)
        sn = jnp.sin(ang)
        s1 = jnp.where((lane >= 64) & (lane < 80), -sn, 0.0)
        s2 = jnp.where((lane >= 80) & (lane < 96), sn, 0.0)
        angr = p * ir
        snr = jnp.sin(angr)
        return cm, s1, s2, jnp.cos(angr), jnp.where(lane < 64, -snr, snr)

    return rowwise(fn, [(pos_f, None)], [inv_m, inv_r], [(128, F32)] * 5, [], tile=tile, name="rot_tables")


MLA_SCALE = (MLA_NOPE + MLA_ROPE) ** -0.5
MLA_C2 = MLA_SCALE * math.log2(math.e)
ONE_LANE = MLA_V
NEG = -1e30


def _tri_mask(n, lower_rows_ge_cols=True):
    r = lax.broadcasted_iota(jnp.int32, (n, n), 0)
    c = lax.broadcasted_iota(jnp.int32, (n, n), 1)
    return r >= c if lower_rows_ge_cols else c >= r


def mla_fwd(q, k, v, *, tq):
    T = q.shape[0]
    nq = T // tq
    rep = tq // HP

    def body(q_ref, k_ref, v_ref, o_ref, lse_ref, m_sc, acc_sc):
        i = pl.program_id(1)
        qv = q_ref[...]
        m_sc[...] = jnp.full(m_sc.shape, NEG, F32)
        acc_sc[...] = jnp.zeros(acc_sc.shape, F32)

        def block(j, masked):
            off = pl.multiple_of(j * tq, tq)
            kb = k_ref[pl.ds(off, tq), :]
            vb = v_ref[pl.ds(off, tq), :]
            s = _dot_nt(qv, kb) * MLA_C2
            if masked:
                s = jnp.where(_tri_mask(tq), s, NEG)
            m_prev = m_sc[...]
            m_next = jnp.maximum(m_prev, jnp.max(s, axis=-1, keepdims=True))
            p = jnp.exp2(s - jnp.tile(m_next, (1, rep)))
            alpha = jnp.exp2(m_prev - m_next)
            acc_sc[...] = alpha * acc_sc[...] + _dot(p.astype(BF16), vb)
            m_sc[...] = m_next

        def loop_body(j, carry):
            block(j, False)
            return carry

        lax.fori_loop(0, i, loop_body, 0)
        block(i, True)
        acc = acc_sc[...]
        l = acc[:, ONE_LANE:ONE_LANE + 1]
        o_ref[...] = (acc / l).astype(o_ref.dtype)
        lse_ref[...] = m_sc[...] + jnp.log(l) * math.log2(math.e)

    return pl.pallas_call(
        body,
        name="mla_fwd",
        grid=(MLA_H, nq),
        in_specs=[
            pl.BlockSpec((tq, HP), lambda h, i: (i, h)),
            pl.BlockSpec((T, HP), lambda h, i: (0, h)),
            pl.BlockSpec((T, HP), lambda h, i: (0, h)),
        ],
        out_specs=[pl.BlockSpec((tq, HP), lambda h, i: (i, h)), pl.BlockSpec((tq, HP), lambda h, i: (i, h))],
        out_shape=[_sds((T, MLA_H * HP), BF16), _sds((T, MLA_H * HP), F32)],
        scratch_shapes=[pltpu.VMEM((tq, HP), F32), pltpu.VMEM((tq, HP), F32)],
        compiler_params=_cp(2),
    )(q, k, v)


def mla_prep(o, do, lse, *, tq):
    T = o.shape[0]
    nq = T // tq

    def body(o_ref, do_ref, lse_ref, lse_row_ref, delta_row_ref):
        d = jnp.sum(o_ref[...].astype(F32) * do_ref[...].astype(F32), axis=-1, keepdims=True)
        lse_row_ref[...] = lse_ref[...].T[0:1, :]
        delta_row_ref[...] = jnp.broadcast_to(d, (tq, HP)).T[0:1, :]

    blk = pl.BlockSpec((tq, HP), lambda h, i: (i, h))
    row = pl.BlockSpec((None, None, 1, tq), lambda h, i: (h, i, 0, 0))
    return pl.pallas_call(
        body,
        name="mla_prep",
        grid=(MLA_H, nq),
        in_specs=[blk, blk, blk],
        out_specs=[row, row],
        out_shape=[_sds((MLA_H, nq, 1, tq), F32), _sds((MLA_H, nq, 1, tq), F32)],
        compiler_params=_cp(2),
    )(o, do, lse)


def mla_bwd(q, k, v, do, lse_row, delta_row, *, tq):
    T = q.shape[0]
    nq = T // tq

    def body(q_ref, k_ref, v_ref, do_ref, lse_ref, delta_ref, dq_ref, dk_ref, dv_ref, dk_sc, dv_sc):
        j = pl.program_id(1)
        kv = k_ref[...]
        vv = v_ref[...]
        dk_sc[...] = jnp.zeros(dk_sc.shape, F32)
        dv_sc[...] = jnp.zeros(dv_sc.shape, F32)

        @pl.when(j == 0)
        def _():
            dq_ref[...] = jnp.zeros(dq_ref.shape, F32)

        def block(i, masked):
            off = pl.multiple_of(i * tq, tq)
            qb = q_ref[pl.ds(off, tq), :]
            dob = do_ref[pl.ds(off, tq), :]
            st = _dot_nt(kv, qb) * MLA_C2
            if masked:
                st = jnp.where(_tri_mask(tq, False), st, NEG)
            pt = jnp.exp2(st - lse_ref[i])
            dv_sc[...] += _dot(pt.astype(BF16), dob)
            dpt = _dot_nt(vv, dob)
            dst = (pt * (dpt - delta_ref[i]) * MLA_SCALE).astype(BF16)
            dk_sc[...] += _dot(dst, qb)
            dq_ref[pl.ds(off, tq), :] += _dot_tn(dst, kv)

        block(j, True)

        def loop_body(i, carry):
            block(i, False)
            return carry

        lax.fori_loop(j + 1, nq, loop_body, 0)
        dk_ref[...] = dk_sc[...].astype(dk_ref.dtype)
        dv_ref[...] = dv_sc[...].astype(dv_ref.dtype)

    blk = pl.BlockSpec((tq, HP), lambda h, j: (j, h))
    full = pl.BlockSpec((T, HP), lambda h, j: (0, h))
    rows = pl.BlockSpec((None, nq, 1, tq), lambda h, j: (h, 0, 0, 0))
    return pl.pallas_call(
        body,
        name="mla_bwd",
        grid=(MLA_H, nq),
        in_specs=[full, blk, blk, full, rows, rows],
        out_specs=[full, blk, blk],
        out_shape=[_sds((T, MLA_H * HP), F32), _sds((T, MLA_H * HP), BF16), _sds((T, MLA_H * HP), BF16)],
        scratch_shapes=[pltpu.VMEM((tq, HP), F32), pltpu.VMEM((tq, HP), F32)],
        compiler_params=_cp(2),
    )(q, k, v, do, lse_row, delta_row)


def _ret_consts():
    h = jnp.arange(RET_H, dtype=F32)
    log_g = jnp.log1p(-jnp.exp2(-5.0 - h))
    idx = jnp.arange(RET_C, dtype=F32)
    rel = idx[:, None] - idx[None, :]
    dmask = jnp.where(rel >= 0, jnp.exp(log_g[:, None, None] * jnp.maximum(rel, 0.0)), 0.0)
    zeta = jnp.exp(log_g[:, None] * (RET_C - 1.0 - idx)[None, :])
    xi = jnp.exp(log_g[:, None] * (idx + 1.0)[None, :])
    decay = jnp.exp(log_g * RET_C)
    zb = jnp.broadcast_to(zeta[:, :, None], (RET_H, RET_C, RET_D))
    xb = jnp.broadcast_to(xi[:, :, None], (RET_H, RET_C, RET_D))
    db = jnp.broadcast_to(decay[:, None, None], (RET_H, RET_C, RET_D))
    return dmask.astype(F32), zb.astype(F32), xb.astype(F32), db.astype(F32)


def ret_fwd(rq, rk, rv, consts, *, rb):
    T = rq.shape[0]
    nb = T // rb
    ncb = rb // RET_C

    def body(q_ref, k_ref, v_ref, dm_ref, z_ref, x_ref, dc_ref, o_ref, st_ref, r_sc):
        @pl.when(pl.program_id(1) == 0)
        def _():
            r_sc[...] = jnp.zeros(r_sc.shape, F32)

        dm, zt, xi, dc = dm_ref[...], z_ref[...], x_ref[...], dc_ref[...]
        for c in range(ncb):
            sl = slice(c * RET_C, (c + 1) * RET_C)
            q, k, v = q_ref[sl, :], k_ref[sl, :], v_ref[sl, :]
            r = r_sc[...]
            rbf = r.astype(BF16)
            st_ref[sl, :] = rbf
            s = _dot_nt(q, k) * dm
            inner = _dot(s.astype(BF16), v)
            cross = _dot((q.astype(F32) * xi).astype(BF16), rbf)
            o_ref[sl, :] = inner + cross
            kz = (k.astype(F32) * zt).T.astype(BF16)
            r_sc[...] = r * dc + _dot(kz, v)

    blk = pl.BlockSpec((rb, RET_D), lambda h, b: (b, h))
    cst = pl.BlockSpec((None, RET_C, RET_D), lambda h, b: (h, 0, 0))
    return pl.pallas_call(
        body,
        name="ret_fwd",
        grid=(RET_H, nb),
        in_specs=[blk, blk, blk, cst, cst, cst, cst],
        out_specs=[blk, blk],
        out_shape=[_sds((T, RET_H * RET_D), F32), _sds((T, RET_H * RET_D), BF16)],
        scratch_shapes=[pltpu.VMEM((RET_D, RET_D), F32)],
        compiler_params=_cp(2),
    )(rq, rk, rv, *consts)


def ret_bwd(rq, rk, rv, st, dret, consts, *, rb):
    T = rq.shape[0]
    nb = T // rb
    ncb = rb // RET_C

    def body(q_ref, k_ref, v_ref, st_ref, do_ref, dm_ref, z_ref, x_ref, dc_ref, dq_ref, dk_ref, dv_ref, g_sc):
        @pl.when(pl.program_id(1) == 0)
        def _():
            g_sc[...] = jnp.zeros(g_sc.shape, F32)

        dm, zt, xi, dc = dm_ref[...], z_ref[...], x_ref[...], dc_ref[...]
        for c in reversed(range(ncb)):
            sl = slice(c * RET_C, (c + 1) * RET_C)
            q, k, v, rp = q_ref[sl, :], k_ref[sl, :], v_ref[sl, :], st_ref[sl, :]
            dob = do_ref[sl, :].astype(BF16)
            qf, kf = q.astype(F32), k.astype(F32)
            gn = g_sc[...]
            gnb = gn.astype(BF16)
            s = _dot_nt(q, k) * dm
            ds = _dot_nt(dob, v) * dm
            dq = _dot(ds.astype(BF16), k) + _dot_nt(dob, rp) * xi
            dk = _dot(ds.T.astype(BF16), q) + _dot_nt(v, gnb) * zt
            dv = _dot(s.T.astype(BF16), dob) + _dot((kf * zt).astype(BF16), gnb)
            dq_ref[sl, :] = dq.astype(dq_ref.dtype)
            dk_ref[sl, :] = dk.astype(dk_ref.dtype)
            dv_ref[sl, :] = dv.astype(dv_ref.dtype)
            g_sc[...] = _dot((qf * xi).T.astype(BF16), dob) + dc * gn

    blk = pl.BlockSpec((rb, RET_D), lambda h, b: (nb - 1 - b, h))
    cst = pl.BlockSpec((None, RET_C, RET_D), lambda h, b: (h, 0, 0))
    return pl.pallas_call(
        body,
        name="ret_bwd",
        grid=(RET_H, nb),
        in_specs=[blk, blk, blk, blk, blk, cst, cst, cst, cst],
        out_specs=[blk, blk, blk],
        out_shape=[_sds((T, RET_H * RET_D), F32)] * 3,
        scratch_shapes=[pltpu.VMEM((RET_D, RET_D), F32)],
        compiler_params=_cp(2),
    )(rq, rk, rv, st, dret, *consts)


HALO = 16


def conv_act_fwd(up_pre, w_conv, b_conv, *, tile, cw):
    T = up_pre.shape[0]
    nt = T // tile
    ncol = D_FF // cw
    hb = tile // HALO

    def body(pa_ref, a_ref, pb_ref, b_ref, wa_ref, wb_ref, ba_ref, bb_ref, o_ref):
        i = pl.program_id(1)
        keep = (i > 0).astype(F32)

        def conv(prev_ref, cur_ref, w_ref, bias_ref):
            ext = jnp.concatenate([prev_ref[...].astype(F32) * keep, cur_ref[...].astype(F32)], axis=0)
            w = w_ref[...]
            y = ext * w[2:3, :] + pltpu.roll(ext, 1, 0) * w[1:2, :] + pltpu.roll(ext, 2, 0) * w[0:1, :] + bias_ref[...]
            return y[HALO:, :]

        a = conv(pa_ref, a_ref, wa_ref, ba_ref)
        b = conv(pb_ref, b_ref, wb_ref, bb_ref)
        o_ref[...] = (a * _sigmoid(a) * b).astype(o_ref.dtype)

    prev_a = pl.BlockSpec((HALO, cw), lambda j, i: (jnp.maximum(i * hb - 1, 0), j))
    cur_a = pl.BlockSpec((tile, cw), lambda j, i: (i, j))
    prev_b = pl.BlockSpec((HALO, cw), lambda j, i: (jnp.maximum(i * hb - 1, 0), j + ncol))
    cur_b = pl.BlockSpec((tile, cw), lambda j, i: (i, j + ncol))
    w_a = pl.BlockSpec((3, cw), lambda j, i: (0, j))
    w_b = pl.BlockSpec((3, cw), lambda j, i: (0, j + ncol))
    bias_a = pl.BlockSpec((1, cw), lambda j, i: (0, j))
    bias_b = pl.BlockSpec((1, cw), lambda j, i: (0, j + ncol))
    return pl.pallas_call(
        body,
        name="conv_act_fwd",
        grid=(ncol, nt),
        in_specs=[prev_a, cur_a, prev_b, cur_b, w_a, w_b, bias_a, bias_b],
        out_specs=pl.BlockSpec((tile, cw), lambda j, i: (i, j)),
        out_shape=_sds((T, D_FF), BF16),
        compiler_params=_cp(2),
    )(up_pre, up_pre, up_pre, up_pre, w_conv, w_conv, b_conv, b_conv)


def conv_act_bwd(up_pre, dact, w_conv, b_conv, *, tile, cw):
    T = up_pre.shape[0]
    nt = T // tile
    ncol = D_FF // cw
    hb = tile // HALO
    ext_rows = tile + 2 * HALO

    def body(pa_ref, a_ref, na_ref, pb_ref, b_ref, nb_ref, d_ref, nd_ref, wa_ref, wb_ref, ba_ref, bb_ref,
             dxa_ref, dxb_ref, sa_ref, sb_ref):
        i = pl.program_id(1)
        keep_p = (i > 0).astype(F32)
        keep_n = (i < nt - 1).astype(F32)

        def ext_of(prev_ref, cur_ref, next_ref):
            return jnp.concatenate(
                [prev_ref[...].astype(F32) * keep_p, cur_ref[...].astype(F32), next_ref[...].astype(F32) * keep_n], axis=0)

        def conv(ext, w, bias):
            return ext * w[2:3, :] + pltpu.roll(ext, 1, 0) * w[1:2, :] + pltpu.roll(ext, 2, 0) * w[0:1, :] + bias

        xa = ext_of(pa_ref, a_ref, na_ref)
        xb = ext_of(pb_ref, b_ref, nb_ref)
        wa, wb = wa_ref[...], wb_ref[...]
        a = conv(xa, wa, ba_ref[...])
        b = conv(xb, wb, bb_ref[...])
        dy = jnp.concatenate(
            [jnp.zeros((HALO, cw), F32), d_ref[...].astype(F32), nd_ref[...].astype(F32) * keep_n], axis=0)
        sg = _sigmoid(a)
        da = dy * b * (sg * (1.0 + a * (1.0 - sg)))
        db = dy * (a * sg)

        def back(dup, ext, w, dx_ref, s_ref):
            dx = dup * w[2:3, :] + pltpu.roll(dup, ext_rows - 1, 0) * w[1:2, :] + pltpu.roll(dup, ext_rows - 2, 0) * w[0:1, :]
            dx_ref[...] = dx[HALO:HALO + tile, :].astype(dx_ref.dtype)
            dc = dup[HALO:HALO + tile, :]
            r2 = jnp.sum(dc * ext[HALO:HALO + tile, :], axis=0, keepdims=True)
            r1 = jnp.sum(dc * pltpu.roll(ext, 1, 0)[HALO:HALO + tile, :], axis=0, keepdims=True)
            r0 = jnp.sum(dc * pltpu.roll(ext, 2, 0)[HALO:HALO + tile, :], axis=0, keepdims=True)
            rb = jnp.sum(dc, axis=0, keepdims=True)
            row = lax.broadcasted_iota(jnp.int32, (8, cw), 0)
            upd = (jnp.where(row == 0, r0, 0.0) + jnp.where(row == 1, r1, 0.0) + jnp.where(row == 2, r2, 0.0)
                   + jnp.where(row == 3, rb, 0.0))

            @pl.when(i == 0)
            def _():
                s_ref[...] = upd

            @pl.when(i > 0)
            def _():
                s_ref[...] += upd

        back(da, xa, wa, dxa_ref, sa_ref)
        back(db, xb, wb, dxb_ref, sb_ref)

    def prev_of(shift):
        return pl.BlockSpec((HALO, cw), lambda j, i: (jnp.maximum(i * hb - 1, 0), j + shift))

    def next_of(shift):
        return pl.BlockSpec((HALO, cw), lambda j, i: (jnp.minimum((i + 1) * hb, nt * hb - 1), j + shift))

    def cur_of(shift):
        return pl.BlockSpec((tile, cw), lambda j, i: (i, j + shift))

    def row_of(rows, shift):
        return pl.BlockSpec((rows, cw), lambda j, i: (0, j + shift))

    return pl.pallas_call(
        body,
        name="conv_act_bwd",
        grid=(ncol, nt),
        in_specs=[prev_of(0), cur_of(0), next_of(0), prev_of(ncol), cur_of(ncol), next_of(ncol), cur_of(0), next_of(0),
                  row_of(3, 0), row_of(3, ncol), row_of(1, 0), row_of(1, ncol)],
        out_specs=[cur_of(0), cur_of(0), row_of(8, 0), row_of(8, 0)],
        out_shape=[_sds((T, D_FF), BF16), _sds((T, D_FF), BF16), _sds((8, D_FF), F32), _sds((8, D_FF), F32)],
        compiler_params=_cp(2),
    )(up_pre, up_pre, up_pre, up_pre, up_pre, up_pre, dact, dact, w_conv, w_conv, b_conv, b_conv)


A_W = 1536
A_IN, A_UP, A_XKV, A_END = 0, 564, 1268, 1524
NS_IN, NS_UP, NS_XKV, NS_UQ = 564, 704, 256, 96
B_ROWS = 736
B_OUT, B_XQ, B_XO, B_DN = 0, 128, 256, 384
RS_DN = 352
C_ROWS = 1408
C_UKV, C_PMLA, C_PRET, C_UQ = 0, 128, 640, 1152
D_SHAPE = (8, 768)
WZ_RUNS = ((416, 4096, 0), (0, 384, 4096), (384, 32, 4544))
WZ_ZERO = ((4480, 4544), (4576, 4608))


def _pieces(orig_start, length, dst_start, ns):
    out, c, d, end = [], orig_start, dst_start, orig_start + length
    while c < end:
        j, off = c // ns, c % ns
        ln = min(ns - off, end - c)
        out.append((j, off, d, ln))
        c += ln
        d += ln
    return out


def pack_a(w_in, w_up, w_xkv, *, tile):
    def body(a_ref, b_ref, c_ref, o_ref):
        o_ref[:, A_IN:A_UP] = a_ref[...].astype(BF16)
        o_ref[:, A_UP:A_XKV] = b_ref[...].astype(BF16)
        o_ref[:, A_XKV:A_END] = c_ref[...].astype(BF16)
        o_ref[:, A_END:A_W] = jnp.zeros((tile, A_W - A_END), BF16)

    return pl.pallas_call(
        body,
        name="pack_a",
        grid=(D // tile,),
        in_specs=[pl.BlockSpec((tile, NS_IN), lambda i: (i, 0)), pl.BlockSpec((tile, NS_UP), lambda i: (i, 0)),
                  pl.BlockSpec((tile, NS_XKV), lambda i: (i, 0))],
        out_specs=pl.BlockSpec((tile, A_W), lambda i: (i, 0)),
        out_shape=_sds((D, A_W), BF16),
        compiler_params=_cp(1),
    )(w_in, w_up, w_xkv)


def pack_bcd(w_out, w_xq, w_xo, w_down, w_ukv, w_pmla, w_pret, w_uq, w_conv):
    def body(o_ref, xq_ref, xo_ref, dn_ref, ukv_ref, pm_ref, pr_ref, uq_ref, cv_ref, b_ref, c_ref, d_ref):
        b_ref[B_OUT:B_XQ, :] = o_ref[...].astype(BF16)
        b_ref[B_XQ:B_XO, :] = xq_ref[...].astype(BF16)
        b_ref[B_XO:B_DN, :] = xo_ref[...].astype(BF16)
        b_ref[B_DN:B_ROWS, :] = dn_ref[...].astype(BF16)
        c_ref[C_UKV:C_PMLA, :] = ukv_ref[...].astype(BF16)
        c_ref[C_PMLA:C_PRET, :] = pm_ref[...].astype(BF16)
        c_ref[C_PRET:C_UQ, :] = pr_ref[...].astype(BF16)
        c_ref[C_UQ:C_ROWS, 0:NS_UQ] = uq_ref[...].astype(BF16)
        c_ref[C_UQ:C_ROWS, NS_UQ:HP] = jnp.zeros((C_ROWS - C_UQ, HP - NS_UQ), BF16)
        d_ref[...] = jnp.zeros(D_SHAPE, F32)
        d_ref[0:3, 0:NS_UP] = cv_ref[...]

    return pl.pallas_call(
        body,
        name="pack_bcd",
        in_specs=[VM] * 9,
        out_specs=[VM] * 3,
        out_shape=[_sds((B_ROWS, D), BF16), _sds((C_ROWS, HP), BF16), _sds(D_SHAPE, F32)],
        compiler_params=pltpu.CompilerParams(vmem_limit_bytes=VMEM_LIMIT),
    )(w_out, w_xq, w_xo, w_down, w_ukv, w_pmla, w_pret, w_uq, w_conv)


def assemble_a(ga, *, tile):
    def body(g_ref, wz_ref, wup_ref, wxkv_ref):
        for lo, hi in WZ_ZERO:
            wz_ref[:, lo:hi] = jnp.zeros((tile, hi - lo), BF16)
        for os_, ln_, ds_ in WZ_RUNS:
            for j, off, d, ln in _pieces(os_, ln_, ds_, NS_IN):
                wz_ref[:, d:d + ln] = g_ref[j, :, A_IN + off:A_IN + off + ln]
        for j in range(N_DEV):
            wup_ref[:, j * NS_UP:(j + 1) * NS_UP] = g_ref[j, :, A_UP:A_XKV]
            wxkv_ref[:, j * NS_XKV:(j + 1) * NS_XKV] = g_ref[j, :, A_XKV:A_END]

    return pl.pallas_call(
        body,
        name="assemble_a",
        grid=(D // tile,),
        in_specs=[pl.BlockSpec((N_DEV, tile, A_W), lambda i: (0, i, 0))],
        out_specs=[pl.BlockSpec((tile, ZW), lambda i: (i, 0)), pl.BlockSpec((tile, 2 * D_FF), lambda i: (i, 0)),
                   pl.BlockSpec((tile, 2 * D), lambda i: (i, 0))],
        out_shape=[_sds((D, ZW), BF16), _sds((D, 2 * D_FF), BF16), _sds((D, 2 * D), BF16)],
        compiler_params=_cp(1),
    )(ga)


def assemble_bcd(gb, gc, gd):
    def body(gb_ref, gc_ref, gd_ref, wo_ref, wxq_ref, wxo_ref, wdn_ref, wk_ref, wv_ref, wpa_ref, wpr_ref, wq_ref, wc_ref):
        half = jnp.zeros((MLA_KVR, HP - MLA_NOPE), BF16)
        for j in range(N_DEV):
            wo_ref[j * 128:(j + 1) * 128, :] = gb_ref[j, B_OUT:B_XQ, :]
            wxq_ref[j * 128:(j + 1) * 128, :] = gb_ref[j, B_XQ:B_XO, :]
            wxo_ref[j * 128:(j + 1) * 128, :] = gb_ref[j, B_XO:B_DN, :]
            wdn_ref[j * RS_DN:(j + 1) * RS_DN, :] = gb_ref[j, B_DN:B_ROWS, :]
            wk_ref[:, j * HP:j * HP + MLA_NOPE] = gc_ref[j, C_UKV:C_PMLA, 0:MLA_NOPE]
            wk_ref[:, j * HP + MLA_NOPE:(j + 1) * HP] = half
            wv_ref[:, j * HP:j * HP + MLA_V] = gc_ref[j, C_UKV:C_PMLA, MLA_NOPE:HP]
            wv_ref[:, j * HP + MLA_V:(j + 1) * HP] = half
            wpr_ref[:, j * 128:(j + 1) * 128] = gc_ref[j, C_PRET:C_UQ, :]
            wq_ref[:, j * HP:j * HP + NS_UQ] = gc_ref[j, C_UQ:C_ROWS, 0:NS_UQ]
            wq_ref[:, j * HP + NS_UQ:(j + 1) * HP] = jnp.zeros((MLA_QR, HP - NS_UQ), BF16)
            wc_ref[:, j * NS_UP:(j + 1) * NS_UP] = gd_ref[j, 0:3, 0:NS_UP]
            for h in range(MLA_H):
                wpa_ref[h * HP:h * HP + MLA_V, j * 128:(j + 1) * 128] = gc_ref[j, C_PMLA + h * MLA_V:C_PMLA + (h + 1) * MLA_V, :]
        for h in range(MLA_H):
            wpa_ref[h * HP + MLA_V:(h + 1) * HP, :] = jnp.zeros((HP - MLA_V, D), BF16)

    return pl.pallas_call(
        body,
        name="assemble_bcd",
        in_specs=[VM] * 3,
        out_specs=[VM] * 10,
        out_shape=[_sds((D, D), BF16), _sds((D, D), BF16), _sds((D, D), BF16), _sds((D_FF, D), BF16),
                   _sds((MLA_KVR, MLA_H * HP), BF16), _sds((MLA_KVR, MLA_H * HP), BF16), _sds((MLA_H * HP, D), BF16),
                   _sds((RET_H * RET_D, D), BF16), _sds((MLA_QR, MLA_H * HP), BF16), _sds((3, 2 * D_FF), F32)],
        compiler_params=pltpu.CompilerParams(vmem_limit_bytes=VMEM_LIMIT),
    )(gb, gc, gd)


def slots_a(dwz, dwup_a, dwup_b, dwxkv, *, tile):
    per_half = D_FF // NS_UP

    def body(dz_ref, ua_ref, ub_ref, x_ref, s_ref):
        for os_, ln_, ds_ in WZ_RUNS:
            for j, off, d, ln in _pieces(os_, ln_, ds_, NS_IN):
                s_ref[j, :, A_IN + off:A_IN + off + ln] = dz_ref[:, d:d + ln].astype(BF16)
        for j in range(N_DEV):
            src = ua_ref if j < per_half else ub_ref
            c0 = (j % per_half) * NS_UP
            s_ref[j, :, A_UP:A_XKV] = src[:, c0:c0 + NS_UP].astype(BF16)
            s_ref[j, :, A_XKV:A_END] = x_ref[:, j * NS_XKV:(j + 1) * NS_XKV].astype(BF16)
            s_ref[j, :, A_END:A_W] = jnp.zeros((tile, A_W - A_END), BF16)

    return pl.pallas_call(
        body,
        name="slots_a",
        grid=(D // tile,),
        in_specs=[pl.BlockSpec((tile, ZW), lambda i: (i, 0)), pl.BlockSpec((tile, D_FF), lambda i: (i, 0)),
                  pl.BlockSpec((tile, D_FF), lambda i: (i, 0)), pl.BlockSpec((tile, 2 * D), lambda i: (i, 0))],
        out_specs=pl.BlockSpec((N_DEV, tile, A_W), lambda i: (0, i, 0)),
        out_shape=_sds((N_DEV, D, A_W), BF16),
        compiler_params=_cp(1),
    )(dwz, dwup_a, dwup_b, dwxkv)


def slots_bc(dwo, dwxq, dwxo, dwdn, dwk, dwv, dwpa, dwpr, dwq):
    def body(o_ref, xq_ref, xo_ref, dn_ref, k_ref, v_ref, pa_ref, pr_ref, q_ref, b_ref, c_ref):
        b_ref[B_OUT:B_XQ, :] = o_ref[...].astype(BF16)
        b_ref[B_XQ:B_XO, :] = xq_ref[...].astype(BF16)
        b_ref[B_XO:B_DN, :] = xo_ref[...].astype(BF16)
        b_ref[B_DN:B_ROWS, :] = dn_ref[...].astype(BF16)
        c_ref[C_UKV:C_PMLA, 0:MLA_NOPE] = k_ref[:, 0:MLA_NOPE].astype(BF16)
        c_ref[C_UKV:C_PMLA, MLA_NOPE:HP] = v_ref[:, 0:MLA_V].astype(BF16)
        for h in range(MLA_H):
            c_ref[C_PMLA + h * MLA_V:C_PMLA + (h + 1) * MLA_V, :] = pa_ref[h * HP:h * HP + MLA_V, :].astype(BF16)
        c_ref[C_PRET:C_UQ, :] = pr_ref[...].astype(BF16)
        c_ref[C_UQ:C_ROWS, 0:NS_UQ] = q_ref[:, 0:NS_UQ].astype(BF16)
        c_ref[C_UQ:C_ROWS, NS_UQ:HP] = jnp.zeros((C_ROWS - C_UQ, HP - NS_UQ), BF16)

    rows128 = pl.BlockSpec((128, D), lambda j: (j, 0))

    def cols(r):
        return pl.BlockSpec((r, 128), lambda j: (0, j))

    return pl.pallas_call(
        body,
        name="slots_bc",
        grid=(N_DEV,),
        in_specs=[rows128, rows128, rows128, pl.BlockSpec((RS_DN, D), lambda j: (j, 0)),
                  cols(MLA_KVR), cols(MLA_KVR), cols(MLA_H * HP), cols(RET_H * RET_D), cols(MLA_QR)],
        out_specs=[pl.BlockSpec((None, B_ROWS, D), lambda j: (j, 0, 0)), pl.BlockSpec((None, C_ROWS, HP), lambda j: (j, 0, 0))],
        out_shape=[_sds((N_DEV, B_ROWS, D), BF16), _sds((N_DEV, C_ROWS, HP), BF16)],
        compiler_params=_cp(1),
    )(dwo, dwxq, dwxo, dwdn, dwk, dwv, dwpa, dwpr, dwq)


SMALL = (("g_mix", 1024, 0), ("b_gate", 2048, 1), ("g_q_lat", 256, 3), ("g_kv_lat", 128, 4), ("g_ret", 512, 5),
         ("g_cross", 1024, 6), ("g_mem", 1024, 7), ("g_ffn", 1024, 8), ("b_conv", 5632, 9), ("g_final", 1024, 15))
S_ROWS = 16


def _flat_pieces(n, row0):
    return [(row0 + k // D, k, min(D, n - k)) for k in range(0, n, D)]


def _peer(m):
    x, y, c = lax.axis_index("x"), lax.axis_index("y"), lax.axis_index("c")
    mx, my, mc = (m >> 2) & 1, (m >> 1) & 1, m & 1
    px = 1 - x if mx else x
    py = 1 - y if my else y
    pc = 1 - c if mc else c
    return (px, py, pc), 4 * px + 2 * py + pc


def _my_id():
    return 4 * lax.axis_index("x") + 2 * lax.axis_index("y") + lax.axis_index("c")


def all_gather_groups(parts):
    n = len(parts)

    def body(*refs):
        srcs, outs = refs[:n], refs[n:2 * n]
        send_sems, recv_sems, local_sems = refs[2 * n:]
        me = _my_id()
        mine = []
        for g in range(n):
            cp = pltpu.make_async_copy(srcs[g], outs[g].at[me], local_sems.at[g])
            cp.start()
            mine.append(cp)
        sends = []
        for m in range(1, N_DEV):
            peer, plin = _peer(m)
            for g in range(n):
                cp = pltpu.make_async_remote_copy(
                    src_ref=srcs[g], dst_ref=outs[g].at[me], send_sem=send_sems.at[g, m - 1],
                    recv_sem=recv_sems.at[g, m - 1], device_id=peer, device_id_type=MESH)
                cp.start()
                sends.append(cp)
        for m in range(1, N_DEV):
            peer, plin = _peer(m)
            for g in range(n):
                pltpu.make_async_remote_copy(
                    src_ref=srcs[g], dst_ref=outs[g].at[plin], send_sem=send_sems.at[g, m - 1],
                    recv_sem=recv_sems.at[g, m - 1], device_id=peer, device_id_type=MESH).wait_recv()
        for cp in sends:
            cp.wait_send()
        for cp in mine:
            cp.wait()

    return pl.pallas_call(
        body,
        name="all_gather_groups",
        in_specs=[ANY] * n,
        out_specs=[ANY] * n,
        out_shape=[_sds((N_DEV,) + p.shape, p.dtype) for p in parts],
        scratch_shapes=[pltpu.SemaphoreType.DMA((n, N_DEV - 1)), pltpu.SemaphoreType.DMA((n, N_DEV - 1)),
                        pltpu.SemaphoreType.DMA((n,))],
    )(*parts)


def exchange_grads(sa, sb, sc, cs_a, cs_b, smalls):
    ns = len(smalls)

    def body(*refs):
        sa_ref, sb_ref, sc_ref, ca_ref, cb_ref = refs[:5]
        small_refs = refs[5:5 + ns]
        ra_ref, rb_ref, rc_ref, rd_ref, rs_ref, dsl_ref, own_ref = refs[5 + ns:12 + ns]
        send_sems, recv_sems = refs[12 + ns:]
        own_ref[...] = jnp.zeros(own_ref.shape, F32)
        for (name, n, row0), g_ref in zip(SMALL, small_refs):
            if name == "b_conv":
                continue
            for r, c0, ln in _flat_pieces(n, row0):
                own_ref[r:r + 1, 0:ln] = g_ref[:, c0:c0 + ln]
        row0 = dict((s[0], s[2]) for s in SMALL)["b_conv"]
        for half, c_ref in enumerate((ca_ref, cb_ref)):
            k = half * D_FF
            end = k + D_FF
            while k < end:
                r, lane = row0 + k // D, k % D
                ln = min(D - lane, end - k)
                own_ref[r:r + 1, lane:lane + ln] = c_ref[3:4, k - half * D_FF:k - half * D_FF + ln]
                k += ln
        dsl_ref[...] = jnp.zeros(dsl_ref.shape, F32)
        per_half = D_FF // NS_UP
        for j in range(N_DEV):
            c_ref = ca_ref if j < per_half else cb_ref
            c0 = (j % per_half) * NS_UP
            dsl_ref[j, 0:3, 0:NS_UP] = c_ref[0:3, c0:c0 + NS_UP]
        copies = []
        for m in range(1, N_DEV):
            peer, plin = _peer(m)
            pairs = ((sa_ref.at[plin], ra_ref), (sb_ref.at[plin], rb_ref), (sc_ref.at[plin], rc_ref),
                     (dsl_ref.at[plin], rd_ref), (own_ref, rs_ref))
            for g, (src, dst) in enumerate(pairs):
                cp = pltpu.make_async_remote_copy(
                    src_ref=src, dst_ref=dst.at[m - 1], send_sem=send_sems.at[g, m - 1], recv_sem=recv_sems.at[g, m - 1],
                    device_id=peer, device_id_type=MESH)
                cp.start()
                copies.append(cp)
        for cp in copies:
            cp.wait_recv()
        for cp in copies:
            cp.wait_send()

    n1 = N_DEV - 1
    return pl.pallas_call(
        body,
        name="exchange_grads",
        in_specs=[ANY, ANY, ANY] + [VM] * (2 + ns),
        out_specs=[ANY, ANY, ANY, VM, VM, VM, VM],
        out_shape=[_sds((n1,) + sa.shape[1:], BF16), _sds((n1,) + sb.shape[1:], BF16), _sds((n1,) + sc.shape[1:], BF16),
                   _sds((n1,) + D_SHAPE, F32), _sds((n1, S_ROWS, D), F32), _sds((N_DEV,) + D_SHAPE, F32),
                   _sds((S_ROWS, D), F32)],
        scratch_shapes=[pltpu.SemaphoreType.DMA((5, n1)), pltpu.SemaphoreType.DMA((5, n1))],
        compiler_params=pltpu.CompilerParams(vmem_limit_bytes=VMEM_LIMIT),
    )(sa, sb, sc, cs_a, cs_b, *smalls)


def _adamw(w, g, m, v):
    m = ADAM_B1 * m + (1.0 - ADAM_B1) * g
    v = ADAM_B2 * v + (1.0 - ADAM_B2) * (g * g)
    m_hat = m / (1.0 - ADAM_B1 ** ADAM_STEP)
    v_hat = v / (1.0 - ADAM_B2 ** ADAM_STEP)
    delta = -ADAM_LR * (m_hat / (jnp.sqrt(v_hat) + ADAM_EPS) + ADAM_WD * w)
    return delta, m, v


def _sum_slots(own_ref, recv_ref):
    g = own_ref[...].astype(F32)
    for k in range(N_DEV - 1):
        g = g + recv_ref[k].astype(F32)
    return g


def _apply(g, refs, outs):
    d, mn, vn = _adamw(refs[0][...], g, refs[1][...], refs[2][...])
    outs[0][...] = g
    outs[1][...] = d
    outs[2][...] = mn
    outs[3][...] = vn


def adam_a(own, recv, wmv, *, tile):
    widths = (NS_IN, NS_UP, NS_XKV)
    starts = (A_IN, A_UP, A_XKV)

    def body(*refs):
        own_ref, recv_ref = refs[:2]
        ins, outs = refs[2:11], refs[11:]
        g = _sum_slots(own_ref, recv_ref)
        for t in range(3):
            _apply(g[:, starts[t]:starts[t] + widths[t]], ins[3 * t:3 * t + 3], outs[4 * t:4 * t + 4])

    def blk(w):
        return pl.BlockSpec((tile, w), lambda i: (i, 0))

    return pl.pallas_call(
        body,
        name="adam_a",
        grid=(D // tile,),
        in_specs=[blk(A_W), pl.BlockSpec((N_DEV - 1, tile, A_W), lambda i: (0, i, 0))] + [blk(w) for w in widths for _ in range(3)],
        out_specs=[blk(w) for w in widths for _ in range(4)],
        out_shape=[_sds((D, w), F32) for w in widths for _ in range(4)],
        compiler_params=_cp(1),
    )(own, recv, *wmv)


def adam_b(own, recv, wmv):
    spans = ((B_OUT, B_XQ), (B_XQ, B_XO), (B_XO, B_DN), (B_DN, B_ROWS))

    def body(*refs):
        own_ref, recv_ref = refs[:2]
        ins, outs = refs[2:14], refs[14:]
        for t, (lo, hi) in enumerate(spans):
            g = own_ref[lo:hi, :].astype(F32)
            for k in range(N_DEV - 1):
                g = g + recv_ref[k, lo:hi, :].astype(F32)
            _apply(g, ins[3 * t:3 * t + 3], outs[4 * t:4 * t + 4])

    return pl.pallas_call(
        body,
        name="adam_b",
        in_specs=[VM] * 14,
        out_specs=[VM] * 16,
        out_shape=[_sds((hi - lo, D), F32) for lo, hi in spans for _ in range(4)],
        compiler_params=pltpu.CompilerParams(vmem_limit_bytes=VMEM_LIMIT),
    )(own, recv, *wmv)


def adam_c(own, recv, wmv):
    spans = ((C_UKV, C_PMLA, HP), (C_PMLA, C_PRET, HP), (C_PRET, C_UQ, HP), (C_UQ, C_ROWS, NS_UQ))

    def body(*refs):
        own_ref, recv_ref = refs[:2]
        ins, outs = refs[2:14], refs[14:]
        for t, (lo, hi, w) in enumerate(spans):
            g = own_ref[lo:hi, :].astype(F32)
            for k in range(N_DEV - 1):
                g = g + recv_ref[k, lo:hi, :].astype(F32)
            _apply(g[:, 0:w], ins[3 * t:3 * t + 3], outs[4 * t:4 * t + 4])

    return pl.pallas_call(
        body,
        name="adam_c",
        in_specs=[VM] * 14,
        out_specs=[VM] * 16,
        out_shape=[_sds((hi - lo, w), F32) for lo, hi, w in spans for _ in range(4)],
        compiler_params=pltpu.CompilerParams(vmem_limit_bytes=VMEM_LIMIT),
    )(own, recv, *wmv)


def adam_small(own_s, recv_s, dslots, recv_d, wmv_small, wmv_conv):
    ns = len(SMALL)

    def body(*refs):
        own_ref, rs_ref, dsl_ref, rd_ref = refs[:4]
        ins = refs[4:4 + 3 * ns + 3]
        outs = refs[4 + 3 * ns + 3:4 + 3 * ns + 3 + 4 * ns + 4]
        all_sc = refs[-1]
        me = _my_id()
        all_sc[0] = own_ref[...]
        for k in range(N_DEV - 1):
            all_sc[k + 1] = rs_ref[k]
        g = all_sc[jnp.bitwise_xor(me, 0)]
        for s in range(1, N_DEV):
            g = g + all_sc[jnp.bitwise_xor(me, s)]
        all_sc[0] = g
        for t, (name, n, row0) in enumerate(SMALL):
            pieces = [all_sc[0, r:r + 1, 0:ln] for r, _, ln in _flat_pieces(n, row0)]
            gt = pieces[0] if len(pieces) == 1 else jnp.concatenate(pieces, axis=1)
            _apply(gt, ins[3 * t:3 * t + 3], outs[4 * t:4 * t + 4])
        gc = dsl_ref[me]
        for k in range(N_DEV - 1):
            gc = gc + rd_ref[k]
        _apply(gc[0:3, 0:NS_UP], ins[3 * ns:3 * ns + 3], outs[4 * ns:4 * ns + 4])

    out_shape = [_sds((1, n), F32) for _, n, _ in SMALL for _ in range(4)] + [_sds((3, NS_UP), F32)] * 4
    return pl.pallas_call(
        body,
        name="adam_small",
        in_specs=[VM] * (4 + 3 * ns + 3),
        out_specs=[VM] * len(out_shape),
        out_shape=out_shape,
        scratch_shapes=[pltpu.VMEM((N_DEV, S_ROWS, D), F32)],
    )(own_s, recv_s, dslots, recv_d, *wmv_small, *wmv_conv)


def kernel(x, mem, positions, g_mix, w_in, b_gate, g_q_lat, w_uq, g_kv_lat, w_ukv, w_proj_mla, g_ret, w_proj_ret, w_out, g_cross, g_mem, w_xq, w_xkv, w_xo, g_ffn, w_up, w_conv, b_conv, w_down, g_final, loss_target, m_g_mix, m_w_in, m_b_gate, m_g_q_lat, m_w_uq, m_g_kv_lat, m_w_ukv, m_w_proj_mla, m_g_ret, m_w_proj_ret, m_w_out, m_g_cross, m_g_mem, m_w_xq, m_w_xkv, m_w_xo, m_g_ffn, m_w_up, m_w_conv, m_b_conv, m_w_down, m_g_final, v_g_mix, v_w_in, v_b_gate, v_g_q_lat, v_w_uq, v_g_kv_lat, v_w_ukv, v_w_proj_mla, v_g_ret, v_w_proj_ret, v_w_out, v_g_cross, v_g_mem, v_w_xq, v_w_xkv, v_w_xo, v_g_ffn, v_w_up, v_w_conv, v_b_conv, v_w_down, v_g_final):
    args = dict(locals())
    T = x.shape[1]
    M = mem.shape[1]
    tile = min(256, T)
    tq = min(512, T)
    rb = min(1024, T)

    xs = x[0]
    tgt = loss_target[0]
    mems = mem[0]

    def shard(name, prefix=""):
        a = args[prefix + name]
        return a.reshape(a.shape[-2:]) if a.ndim >= 2 else a.reshape(1, -1)

    pa = pack_a(shard("w_in"), shard("w_up"), shard("w_xkv"), tile=tile)
    pb, pc, pd = pack_bcd(shard("w_out"), shard("w_xq"), shard("w_xo"), shard("w_down"), shard("w_ukv"),
                          shard("w_proj_mla"), shard("w_proj_ret"), shard("w_uq"), shard("w_conv"))
    ga, gb, gc, gd = all_gather_groups([pa, pb, pc, pd])
    wz, wup, wxkv = assemble_a(ga, tile=tile)
    wo, wxq, wxo, wdn, wk, wv, wpa, wpr, wq, wcv = assemble_bcd(gb, gc, gd)

    pos_f = jnp.broadcast_to(positions[0].astype(F32)[:, None], (T, 128))
    cm, s1, s2, cr, sr = rot_tables(pos_f, tile=tile)
    rconsts = _ret_consts()

    u = rowwise(lambda xv, g: _rms(xv, g), [(xs, None)], [g_mix], [(D, BF16)], [], tile=tile, name="norm_mix")[0]
    z = matmul(u, wz, name="mm_z")

    def lat_fwd(zl, cmv, s1v, s2v, gq, gkv, wqv, wkv_, wvv):
        cq = _rms(zl[:, 0:256], gq).astype(BF16)
        ckv = _rms(zl[:, 256:384], gkv).astype(BF16)
        qv = _rot_mla(_dot(cq, wqv), cmv, s1v, s2v)
        kr = _rot_mla(zl[:, 384:512], cmv, s1v, s2v)
        kn = _dot(ckv, wkv_)
        kv_ = jnp.concatenate([kn[:, h * HP:(h + 1) * HP] + kr for h in range(MLA_H)], axis=1)
        vv = _dot(ckv, wvv)
        lane = lax.broadcasted_iota(jnp.int32, vv.shape, 1)
        vv = jnp.where((lane & (HP - 1)) == ONE_LANE, 1.0, vv)
        return qv, kv_, vv

    q_a, k_a, v_a = rowwise(
        lat_fwd, [(z, (512, 8)), (cm, None), (s1, None), (s2, None)], [g_q_lat, g_kv_lat, wq, wk, wv],
        [(MLA_H * HP, BF16)] * 3, [], tile=tile, name="lat_fwd")

    def retprep_fwd(zr, crv, srv):
        rqv = _rot_ret(zr[:, 0:512], crv, srv)
        rkv = _rot_ret(zr[:, 512:1024], crv, srv) * (RET_D ** -0.5)
        return rqv, rkv, zr[:, 1024:1536]

    rq, rk, rv = rowwise(retprep_fwd, [(z, (2048, 0)), (cr, None), (sr, None)], [], [(512, BF16)] * 3, [],
                         tile=tile, name="retprep_fwd")

    o_a, lse = mla_fwd(q_a, k_a, v_a, tq=tq)
    ret, rstate = ret_fwd(rq, rk, rv, rconsts, rb=rb)

    def gn_parts(r):
        outs = []
        for h in range(RET_H):
            rh = r[:, h * RET_D:(h + 1) * RET_D]
            mu = jnp.mean(rh, axis=-1, keepdims=True)
            dlt = rh - mu
            rstd = lax.rsqrt(jnp.mean(dlt * dlt, axis=-1, keepdims=True) + EPS)
            outs.append((dlt * rstd, rstd))
        return outs

    def mix_fwd(ov, rv_, rg, gt, wpav, wprv, gr, bg):
        ya = _dot(ov, wpav)
        xh = jnp.concatenate([p[0] for p in gn_parts(rv_)], axis=1)
        t = rg * _sigmoid(rg) * (xh * gr)
        yr = _dot(t.astype(BF16), wprv)
        ga_ = _sigmoid(gt[:, :D] + bg[:, :D])
        gr_ = _sigmoid(gt[:, D:] + bg[:, D:])
        return ga_ * ya + gr_ * yr

    mix = rowwise(mix_fwd, [(o_a, None), (ret, None), (z, (512, 3)), (z, (2048, 1))], [wpa, wpr, g_ret, b_gate],
                  [(D, BF16)], [], tile=tile, name="mix_fwd")[0]
    h1 = matmul(mix, wo, res=xs, name="mm_out")
    n2 = rowwise(lambda hv, g: _rms(hv, g), [(h1, None)], [g_cross], [(D, BF16)], [], tile=tile, name="norm_cross")[0]
    xq = matmul(n2, wxq, out_dtype=BF16, name="mm_xq")
    mn = rowwise(lambda mv_, g: _rms(mv_, g), [(mems, None)], [g_mem], [(D, BF16)], [], tile=min(tile, M), name="norm_mem")[0]
    mkv = matmul(mn, wxkv, out_dtype=BF16, name="mm_mkv")

    x_scale = X_HD ** -0.5

    def xattn_fwd(xqv, mkvv):
        outs = []
        for h in range(X_H):
            sl = slice(h * X_HD, (h + 1) * X_HD)
            s = _dot_nt(xqv[:, sl], mkvv[:, sl]) * x_scale
            s = s - jnp.max(s, axis=-1, keepdims=True)
            e = jnp.exp(s)
            p = e / jnp.sum(e, axis=-1, keepdims=True)
            outs.append(_dot(p.astype(BF16), mkvv[:, D + h * X_HD:D + (h + 1) * X_HD]))
        return jnp.concatenate(outs, axis=1)

    xo = rowwise(xattn_fwd, [(xq, None)], [mkv], [(D, BF16)], [], tile=tile, name="xattn_fwd")[0]
    h2 = matmul(xo, wxo, res=h1, name="mm_xo")
    n3 = rowwise(lambda hv, g: _rms(hv, g), [(h2, None)], [g_ffn], [(D, BF16)], [], tile=tile, name="norm_ffn")[0]
    up_pre = matmul(n3, wup, out_dtype=BF16, name="mm_up")
    cw = D_FF // 2
    act = conv_act_fwd(up_pre, wcv, b_conv, tile=tile, cw=cw)
    h3 = matmul(act, wdn, res=h2, name="mm_down")

    def loss_fn(hv, tv, g):
        y = _rms(hv, g)
        err = y - tv
        part = 0.5 * jnp.sum(jnp.sum(err * err, axis=-1, keepdims=True) / D, axis=0, keepdims=True)
        dx, dg = _rms_bwd(hv, g, err / D)
        return dx, dg, jnp.broadcast_to(part, (8, 128))

    g_fin2 = g_final.reshape(1, D)
    dh3, dg_final, loss_acc = rowwise(loss_fn, [(h3, None), (tgt, None)], [g_fin2], [(D, F32)],
                                      [((1, D), F32), ((8, 128), F32)], tile=tile, name="loss_bwd")
    loss = lax.psum(loss_acc[0, 0], ("x", "y", "c"))

    dact = matmul(dh3, wdn, tb=True, out_dtype=BF16, name="mm_dact")
    dw_down = matmul_tn(act, dh3, name="mm_dw_down", tm=1408)
    dup_a, dup_b, cs_a, cs_b = conv_act_bwd(up_pre, dact, wcv, b_conv, tile=tile, cw=cw)
    dn3 = matmul2_tb(dup_a, dup_b, wup, name="mm_dn3")
    dw_up_a = matmul_tn(n3, dup_a, name="mm_dw_up_a", tn=1408)
    dw_up_b = matmul_tn(n3, dup_b, name="mm_dw_up_b", tn=1408)

    def norm_bwd(hv, dyv, drv, g):
        dx, dg = _rms_bwd(hv, g, dyv)
        return dx + drv, dg

    dh2, dg_ffn = rowwise(norm_bwd, [(h2, None), (dn3, None), (dh3, None)], [g_ffn], [(D, F32)], [((1, D), F32)],
                          tile=tile, name="norm_ffn_bwd")
    dxo = matmul(dh2, wxo, tb=True, out_dtype=BF16, name="mm_dxo")
    dw_xo = matmul_tn(xo, dh2, name="mm_dw_xo")

    def xattn_bwd(xqv, dxov, mkvv):
        dxq, dmk, dmv = [], [], []
        for h in range(X_H):
            sl = slice(h * X_HD, (h + 1) * X_HD)
            slv = slice(D + h * X_HD, D + (h + 1) * X_HD)
            s = _dot_nt(xqv[:, sl], mkvv[:, sl]) * x_scale
            s = s - jnp.max(s, axis=-1, keepdims=True)
            e = jnp.exp(s)
            p = e / jnp.sum(e, axis=-1, keepdims=True)
            dp = _dot_nt(dxov[:, sl], mkvv[:, slv])
            ds = (p * (dp - jnp.sum(dp * p, axis=-1, keepdims=True)) * x_scale).astype(BF16)
            dxq.append(_dot(ds, mkvv[:, sl]))
            dmk.append(_dot_tn(ds, xqv[:, sl]))
            dmv.append(_dot_tn(p.astype(BF16), dxov[:, sl]))
        return jnp.concatenate(dxq, axis=1), jnp.concatenate(dmk + dmv, axis=1)

    dxq, dmkv = rowwise(xattn_bwd, [(xq, None), (dxo, None)], [mkv], [(D, BF16)], [((M, 2 * D), F32)],
                        tile=tile, name="xattn_bwd")
    dn2 = matmul(dxq, wxq, tb=True, name="mm_dn2")
    dw_xq = matmul_tn(n2, dxq, name="mm_dw_xq")
    dh1, dg_cross = rowwise(norm_bwd, [(h1, None), (dn2, None), (dh2, None)], [g_cross], [(D, F32)], [((1, D), F32)],
                            tile=tile, name="norm_cross_bwd")
    dw_xkv = matmul_tn(mn, dmkv, name="mm_dw_xkv", tk=M)
    dmn = matmul(dmkv, wxkv, tb=True, name="mm_dmn", tm=M)
    dg_mem = rowwise(lambda mv_, dyv, g: _rms_bwd(mv_, g, dyv)[1], [(mems, None), (dmn, None)], [g_mem], [],
                     [((1, D), F32)], tile=min(tile, M), name="norm_mem_bwd")[0]

    dmix = matmul(dh1, wo, tb=True, out_dtype=BF16, name="mm_dmix")
    dw_out = matmul_tn(mix, dh1, name="mm_dw_out")

    def mix_bwd(ov, rv_, rg, gt, dmv_, wpav, wprv, gr, bg):
        dm_ = dmv_.astype(F32)
        ya = _dot(ov, wpav)
        parts = gn_parts(rv_)
        xh = jnp.concatenate([p[0] for p in parts], axis=1)
        yn = xh * gr
        sg = _sigmoid(rg)
        sl_ = rg * sg
        t = (sl_ * yn).astype(BF16)
        yr = _dot(t, wprv)
        ga_ = _sigmoid(gt[:, :D] + bg[:, :D])
        gr_ = _sigmoid(gt[:, D:] + bg[:, D:])
        dgates = jnp.concatenate([dm_ * ya * ga_ * (1.0 - ga_), dm_ * yr * gr_ * (1.0 - gr_)], axis=1)
        dya = (dm_ * ga_).astype(BF16)
        dyr = (dm_ * gr_).astype(BF16)
        do_ = _dot_nt(dya, wpav)
        dwpa_ = _dot_tn(ov, dya)
        dt = _dot_nt(dyr, wprv)
        dwpr_ = _dot_tn(t, dyr)
        drg_ = dt * yn * (sg * (1.0 + rg * (1.0 - sg)))
        dyn = dt * sl_
        dgr = jnp.sum(dyn * xh, axis=0, keepdims=True)
        dxh = dyn * gr
        drets = []
        for h in range(RET_H):
            sl = slice(h * RET_D, (h + 1) * RET_D)
            xhh, rstd = parts[h]
            dxhh = dxh[:, sl]
            drets.append(rstd * (dxhh - jnp.mean(dxhh, axis=-1, keepdims=True)
                                 - xhh * jnp.mean(dxhh * xhh, axis=-1, keepdims=True)))
        dret_ = jnp.concatenate(drets, axis=1)
        dbg = jnp.sum(dgates, axis=0, keepdims=True)
        return do_, dret_, drg_, dgates, dwpa_, dwpr_, dgr, dbg

    do_a, dret, drg, dgates, dwpa, dw_proj_ret, dg_ret, db_gate = rowwise(
        mix_bwd, [(o_a, None), (ret, None), (z, (512, 3)), (z, (2048, 1)), (dmix, None)], [wpa, wpr, g_ret, b_gate],
        [(MLA_H * HP, BF16), (512, F32), (512, BF16), (2 * D, BF16)],
        [((MLA_H * HP, D), F32), ((512, D), F32), ((1, 512), F32), ((1, 2 * D), F32)], tile=tile, name="mix_bwd")

    lse_row, delta_row = mla_prep(o_a, do_a, lse, tq=tq)
    dq_a, dk_a, dv_a = mla_bwd(q_a, k_a, v_a, do_a, lse_row, delta_row, tq=tq)
    drq_r, drk_r, drv = ret_bwd(rq, rk, rv, rstate, dret, rconsts, rb=rb)

    def lat_bwd(zl, cmv, s1v, s2v, dqv, dkv_, dvv, gq, gkv, wqv, wkv_, wvv):
        cqf, ckvf = zl[:, 0:256], zl[:, 256:384]
        cq = _rms(cqf, gq).astype(BF16)
        ckv = _rms(ckvf, gkv).astype(BF16)
        dq_pre = _rot_mla(dqv.astype(F32), cmv, -s1v, -s2v).astype(BF16)
        dkf = dkv_.astype(F32)
        dkr = dkf[:, 0:HP]
        for h in range(1, MLA_H):
            dkr = dkr + dkf[:, h * HP:(h + 1) * HP]
        lane = lax.broadcasted_iota(jnp.int32, dkr.shape, 1)
        dzk = _rot_mla(jnp.where((lane >= 64) & (lane < 96), dkr, 0.0), cmv, -s1v, -s2v)
        dkb = dkv_.astype(BF16)
        dvb = dvv.astype(BF16)
        dcq_n = _dot_nt(dq_pre, wqv)
        dckv_n = _dot_nt(dkb, wkv_) + _dot_nt(dvb, wvv)
        dwq_ = _dot_tn(cq, dq_pre)
        dwk_ = _dot_tn(ckv, dkb)
        dwv_ = _dot_tn(ckv, dvb)
        dcq, dgq = _rms_bwd(cqf, gq, dcq_n)
        dckv, dgkv = _rms_bwd(ckvf, gkv, dckv_n)
        return jnp.concatenate([dcq, dckv, dzk], axis=1), dwq_, dwk_, dwv_, dgq, dgkv

    dz_lat, dwq, dwk, dwv, dg_q_lat, dg_kv_lat = rowwise(
        lat_bwd, [(z, (512, 8)), (cm, None), (s1, None), (s2, None), (dq_a, None), (dk_a, None), (dv_a, None)],
        [g_q_lat, g_kv_lat, wq, wk, wv], [(512, BF16)],
        [((MLA_QR, MLA_H * HP), F32), ((MLA_KVR, MLA_H * HP), F32), ((MLA_KVR, MLA_H * HP), F32),
         ((1, MLA_QR), F32), ((1, MLA_KVR), F32)], tile=tile, name="lat_bwd")

    def dz_assemble(dq_, dk_, dv_, drg_, dgt, dzl, crv, srv):
        a = _rot_ret(dq_, crv, -srv)
        b = _rot_ret(dk_, crv, -srv) * (RET_D ** -0.5)
        return jnp.concatenate([a, b, dv_, drg_.astype(F32), dgt.astype(F32), dzl.astype(F32)], axis=1)

    dz = rowwise(dz_assemble, [(drq_r, None), (drk_r, None), (drv, None), (drg, None), (dgates, None), (dz_lat, None),
                               (cr, None), (sr, None)], [], [(ZW, BF16)], [], tile=tile, name="dz_assemble")[0]
    du = matmul(dz, wz, tb=True, name="mm_du", tm=512)
    dwz = matmul_tn(u, dz, name="mm_dw_z")

    grad_x, dg_mix = rowwise(norm_bwd, [(xs, None), (du, None), (dh1, None)], [g_mix], [(D, F32)], [((1, D), F32)],
                             tile=tile, name="norm_mix_bwd")

    sa = slots_a(dwz, dw_up_a, dw_up_b, dw_xkv, tile=tile)
    sb, sc = slots_bc(dw_out, dw_xq, dw_xo, dw_down, dwk, dwv, dwpa, dw_proj_ret, dwq)
    small_grads = {"g_mix": dg_mix, "b_gate": db_gate, "g_q_lat": dg_q_lat, "g_kv_lat": dg_kv_lat, "g_ret": dg_ret,
                   "g_cross": dg_cross, "g_mem": dg_mem, "g_ffn": dg_ffn, "b_conv": dg_ffn, "g_final": dg_final}
    ra, rb_, rc, rd, rs, dslots, own_s = exchange_grads(sa, sb, sc, cs_a, cs_b, [small_grads[n] for n, _, _ in SMALL])

    me = _my_id()
    own_a = lax.dynamic_index_in_dim(sa, me, axis=0, keepdims=False)
    own_b = lax.dynamic_index_in_dim(sb, me, axis=0, keepdims=False)
    own_c = lax.dynamic_index_in_dim(sc, me, axis=0, keepdims=False)

    def wmv(names):
        return [shard(n, p) for n in names for p in ("", "m_", "v_")]

    res = {}
    names_a = ("w_in", "w_up", "w_xkv")
    names_b = ("w_out", "w_xq", "w_xo", "w_down")
    names_c = ("w_ukv", "w_proj_mla", "w_proj_ret", "w_uq")
    names_s = tuple(n for n, _, _ in SMALL) + ("w_conv",)
    for names, outs in ((names_a, adam_a(own_a, ra, wmv(names_a), tile=128)),
                        (names_b, adam_b(own_b, rb_, wmv(names_b))),
                        (names_c, adam_c(own_c, rc, wmv(names_c))),
                        (names_s, adam_small(own_s, rs, dslots, rd, wmv(names_s[:-1]), wmv(names_s[-1:])))):
        for t, n in enumerate(names):
            res[n] = outs[4 * t:4 * t + 4]

    order = ["g_mix", "w_in", "b_gate", "g_q_lat", "w_uq", "g_kv_lat", "w_ukv", "w_proj_mla", "g_ret", "w_proj_ret",
             "w_out", "g_cross", "g_mem", "w_xq", "w_xkv", "w_xo", "g_ffn", "w_up", "w_conv", "b_conv", "w_down",
             "g_final"]
    outs = [loss, grad_x[None]]
    for kind in range(4):
        outs += [res[n][kind].reshape(args[n].shape) for n in order]
    return tuple(outs)
```

```python
import functools
import math

import jax
import jax.numpy as jnp
import numpy as np
from jax import lax
from jax.experimental import pallas as pl
from jax.experimental.pallas import tpu as pltpu

F32 = jnp.float32
BF16 = jnp.bfloat16

D = 1024
MLA_H, MLA_NOPE, MLA_ROPE, MLA_V = 8, 64, 32, 64
MLA_QR, MLA_KVR = 256, 128
RET_H, RET_D, RET_C = 4, 128, 128
X_H, X_HD = 4, 256
D_FF = 2816
THETA = 10000.0
EPS = 1e-6
HP = 128
ZW = 4608
N_DEV = 8

ADAM_LR, ADAM_B1, ADAM_B2, ADAM_EPS, ADAM_WD, ADAM_STEP = 0.001, 0.9, 0.999, 1e-08, 0.01, 10

VMEM_LIMIT = 56 * 1024 * 1024
MESH = pl.DeviceIdType.MESH
VM = pl.BlockSpec(memory_space=pltpu.VMEM)
ANY = pl.BlockSpec(memory_space=pl.ANY)


def _cp(n_axes):
    return pltpu.CompilerParams(dimension_semantics=("arbitrary",) * n_axes, vmem_limit_bytes=VMEM_LIMIT)


def _cp0():
    return pltpu.CompilerParams(vmem_limit_bytes=VMEM_LIMIT)


def _pick(n, cap, mult=128):
    best = None
    for t in range(mult, min(n, cap) + 1, mult):
        if n % t == 0:
            best = t
    return best if best is not None else n


def _dot(a, b):
    return jnp.dot(a, b, preferred_element_type=F32)


def _dot_nt(a, b):
    return lax.dot_general(a, b, (((1,), (1,)), ((), ())), preferred_element_type=F32)


def _dot_tn(a, b):
    return lax.dot_general(a, b, (((0,), (0,)), ((), ())), preferred_element_type=F32)


def _sds(shape, dtype):
    return jax.ShapeDtypeStruct(shape, dtype)


def matmul(a, b, *, name, tb=False, res=None, out_dtype=F32, tm=1024, tn=512):
    M, K = a.shape
    N = b.shape[0] if tb else b.shape[1]
    tm = _pick(M, tm, 8)
    tn = _pick(N, tn)

    def body(*refs):
        if res is None:
            a_ref, b_ref, o_ref = refs
        else:
            a_ref, b_ref, r_ref, o_ref = refs
        av = a_ref[...].astype(BF16)
        bv = b_ref[...].astype(BF16)
        acc = _dot_nt(av, bv) if tb else _dot(av, bv)
        if res is not None:
            acc = acc + r_ref[...].astype(F32)
        o_ref[...] = acc.astype(o_ref.dtype)

    in_specs = [
        pl.BlockSpec((tm, K), lambda i, j: (i, 0)),
        pl.BlockSpec((tn, K), lambda i, j: (j, 0)) if tb else pl.BlockSpec((K, tn), lambda i, j: (0, j)),
    ]
    args = [a, b]
    if res is not None:
        in_specs.append(pl.BlockSpec((tm, tn), lambda i, j: (i, j)))
        args.append(res)
    return pl.pallas_call(
        body,
        name=name,
        grid=(M // tm, N // tn),
        in_specs=in_specs,
        out_specs=pl.BlockSpec((tm, tn), lambda i, j: (i, j)),
        out_shape=_sds((M, N), out_dtype),
        compiler_params=_cp(2),
    )(*args)


def matmul2_tb(a1, a2, b, *, name, tm=512, tn=512):
    M, K1 = a1.shape
    N = b.shape[0]
    tm = _pick(M, tm, 8)
    tn = _pick(N, tn)

    def body(a1_ref, a2_ref, b1_ref, b2_ref, o_ref):
        o_ref[...] = (_dot_nt(a1_ref[...].astype(BF16), b1_ref[...].astype(BF16))
                      + _dot_nt(a2_ref[...].astype(BF16), b2_ref[...].astype(BF16)))

    return pl.pallas_call(
        body,
        name=name,
        grid=(M // tm, N // tn),
        in_specs=[pl.BlockSpec((tm, K1), lambda i, j: (i, 0)), pl.BlockSpec((tm, K1), lambda i, j: (i, 0)),
                  pl.BlockSpec((tn, K1), lambda i, j: (j, 0)), pl.BlockSpec((tn, K1), lambda i, j: (j, 1))],
        out_specs=pl.BlockSpec((tm, tn), lambda i, j: (i, j)),
        out_shape=_sds((M, N), F32),
        compiler_params=_cp(2),
    )(a1, a2, b, b)


def matmul_tn(a, b, *, name, tm=1024, tn=1024, tk=1024):
    R, M = a.shape
    N = b.shape[1]
    tm = _pick(M, tm)
    tn = _pick(N, tn)
    tk = _pick(R, tk, 16)
    nk = R // tk

    def body(a_ref, b_ref, o_ref, acc_ref):
        k = pl.program_id(2)

        @pl.when(k == 0)
        def _():
            acc_ref[...] = jnp.zeros_like(acc_ref)

        acc_ref[...] += _dot_tn(a_ref[...].astype(BF16), b_ref[...].astype(BF16))

        @pl.when(k == nk - 1)
        def _():
            o_ref[...] = acc_ref[...]

    return pl.pallas_call(
        body,
        name=name,
        grid=(M // tm, N // tn, nk),
        in_specs=[pl.BlockSpec((tk, tm), lambda i, j, k: (k, i)), pl.BlockSpec((tk, tn), lambda i, j, k: (k, j))],
        out_specs=pl.BlockSpec((tm, tn), lambda i, j, k: (i, j)),
        out_shape=_sds((M, N), F32),
        scratch_shapes=[pltpu.VMEM((tm, tn), F32)],
        compiler_params=_cp(3),
    )(a, b)


def rowwise(fn, rows, consts, out_rows, out_accs, *, tile, name):
    T = rows[0][0].shape[0]
    nt = T // tile
    n_r, n_c, n_o, n_a = len(rows), len(consts), len(out_rows), len(out_accs)

    def body(*refs):
        ins = [r[...] for r in refs[: n_r + n_c]]
        outs = fn(*ins)
        if not isinstance(outs, (tuple, list)):
            outs = (outs,)
        o_refs = refs[n_r + n_c : n_r + n_c + n_o]
        a_refs = refs[n_r + n_c + n_o :]
        for o_ref, o in zip(o_refs, outs[:n_o]):
            o_ref[...] = o.astype(o_ref.dtype)
        if n_a:
            first = pl.program_id(0) == 0

            @pl.when(first)
            def _():
                for a_ref, o in zip(a_refs, outs[n_o:]):
                    a_ref[...] = o.astype(a_ref.dtype)

            @pl.when(jnp.logical_not(first))
            def _():
                for a_ref, o in zip(a_refs, outs[n_o:]):
                    a_ref[...] += o.astype(a_ref.dtype)

    in_specs = []
    args = []
    for arr, win in rows:
        if win is None:
            in_specs.append(pl.BlockSpec((tile, arr.shape[1]), lambda i: (i, 0)))
        else:
            w, cb = win
            in_specs.append(pl.BlockSpec((tile, w), functools.partial(lambda i, cb: (i, cb), cb=cb)))
        args.append(arr)
    for c in consts:
        in_specs.append(pl.BlockSpec(c.shape, functools.partial(lambda i, nd: (0,) * nd, nd=c.ndim)))
        args.append(c)
    out_specs = [pl.BlockSpec((tile, w), lambda i: (i, 0)) for w, _ in out_rows]
    out_shape = [_sds((T, w), dt) for w, dt in out_rows]
    for shp, dt in out_accs:
        out_specs.append(pl.BlockSpec(shp, functools.partial(lambda i, nd: (0,) * nd, nd=len(shp))))
        out_shape.append(_sds(shp, dt))
    return pl.pallas_call(
        body,
        name=name,
        grid=(nt,),
        in_specs=in_specs,
        out_specs=out_specs,
        out_shape=out_shape,
        compiler_params=_cp(1),
    )(*args)


def _rms(x, g):
    r = lax.rsqrt(jnp.mean(x * x, axis=-1, keepdims=True) + EPS)
    return x * r * g


def _rms_bwd(x, g, dy):
    r = lax.rsqrt(jnp.mean(x * x, axis=-1, keepdims=True) + EPS)
    xh = x * r
    dg = jnp.sum(dy * xh, axis=0, keepdims=True)
    dxh = dy * g
    dx = r * (dxh - xh * jnp.mean(dxh * xh, axis=-1, keepdims=True))
    return dx, dg


def _sigmoid(x):
    return 1.0 / (1.0 + jnp.exp(-x))


def _rot_mla(x, c, s1, s2):
    n = x.shape[1] // HP
    outs = []
    for h in range(n):
        xh = x[:, h * HP : (h + 1) * HP]
        outs.append(xh * c + pltpu.roll(xh, HP - 16, 1) * s1 + pltpu.roll(xh, 16, 1) * s2)
    return outs[0] if n == 1 else jnp.concatenate(outs, axis=1)


def _rot_ret(x, c, s):
    n = x.shape[1] // RET_D
    outs = []
    for h in range(n):
        xh = x[:, h * RET_D : (h + 1) * RET_D]
        outs.append(xh * c + pltpu.roll(xh, RET_D // 2, 1) * s)
    return outs[0] if n == 1 else jnp.concatenate(outs, axis=1)


def rot_tables(pos_f, *, tile):
    lane_np = np.arange(128)
    inv_m = (jnp.asarray(THETA, F32) ** (-jnp.asarray(lane_np & 15, F32) / 16.0)).reshape(1, 128)
    inv_r = (jnp.asarray(THETA, F32) ** (-jnp.asarray(lane_np & 63, F32) / 64.0)).reshape(1, 128)

    def fn(p, im, ir):
        lane = lax.broadcasted_iota(jnp.int32, p.shape, 1)
        ang = p * im
        cm = jnp.where((lane >= 64) & (lane < 96), jnp.cos(ang), 1.0)
        sn = jnp.sin(ang)
        s1 = jnp.where((lane >= 64) & (lane < 80), -sn, 0.0)
        s2 = jnp.where((lane >= 80) & (lane < 96), sn, 0.0)
        angr = p * ir
        snr = jnp.sin(angr)
        return cm, s1, s2, jnp.cos(angr), jnp.where(lane < 64, -snr, snr)

    return rowwise(fn, [(pos_f, None)], [inv_m, inv_r], [(128, F32)] * 5, [], tile=tile, name="rot_tables")


def _peer(m):
    x, y, c = lax.axis_index("x"), lax.axis_index("y"), lax.axis_index("c")
    mx, my, mc = (m >> 2) & 1, (m >> 1) & 1, m & 1
    px = 1 - x if mx else x
    py = 1 - y if my else y
    pc = 1 - c if mc else c
    return (px, py, pc), 4 * px + 2 * py + pc


def _my_id():
    return 4 * lax.axis_index("x") + 2 * lax.axis_index("y") + lax.axis_index("c")


def _gather_copies(srcs, outs, send_sems, recv_sems, local_sems, arriving=False):
    me = _my_id()
    copies = []
    if not arriving:
        for g, (src, out) in enumerate(zip(srcs, outs)):
            copies.append((pltpu.make_async_copy(src, out.at[me], local_sems.at[g]), False))
    for m in range(1, N_DEV):
        peer, plin = _peer(m)
        for g, (src, out) in enumerate(zip(srcs, outs)):
            copies.append((pltpu.make_async_remote_copy(
                src_ref=src, dst_ref=out.at[plin if arriving else me], send_sem=send_sems.at[g, m - 1],
                recv_sem=recv_sems.at[g, m - 1], device_id=peer, device_id_type=MESH), True))
    return copies


def _gather_start(*a):
    for cp, _ in _gather_copies(*a):
        cp.start()


def _gather_wait(*a):
    for cp, _ in _gather_copies(*a, arriving=True):
        cp.wait_recv()
    for cp, remote in _gather_copies(*a):
        if remote:
            cp.wait_send()
        else:
            cp.wait()


def _gather_scratch(n):
    return [pltpu.SemaphoreType.DMA((n, N_DEV - 1)), pltpu.SemaphoreType.DMA((n, N_DEV - 1)), pltpu.SemaphoreType.DMA((n,))]


def _exchange_copies(srcs, dsts, send_sems, recv_sems):
    copies = []
    for m in range(1, N_DEV):
        peer, plin = _peer(m)
        for g, ((src, per_peer), dst) in enumerate(zip(srcs, dsts)):
            copies.append(pltpu.make_async_remote_copy(
                src_ref=src.at[plin] if per_peer else src, dst_ref=dst.at[m - 1], send_sem=send_sems.at[g, m - 1],
                recv_sem=recv_sems.at[g, m - 1], device_id=peer, device_id_type=MESH))
    return copies


def _exchange_start(*a):
    for cp in _exchange_copies(*a):
        cp.start()


def _exchange_wait(*a):
    copies = _exchange_copies(*a)
    for cp in copies:
        cp.wait_recv()
    for cp in copies:
        cp.wait_send()


def _exchange_scratch(n):
    return [pltpu.SemaphoreType.DMA((n, N_DEV - 1)), pltpu.SemaphoreType.DMA((n, N_DEV - 1))]


def all_gather_groups(parts, *, name):
    n = len(parts)

    def body(*refs):
        a = (refs[:n], refs[n:2 * n]) + tuple(refs[2 * n:])
        _gather_start(*a)
        _gather_wait(*a)

    return pl.pallas_call(
        body,
        name=name,
        in_specs=[ANY] * n,
        out_specs=[ANY] * n,
        out_shape=[_sds((N_DEV,) + p.shape, p.dtype) for p in parts],
        scratch_shapes=_gather_scratch(n),
    )(*parts)


MLA_SCALE = (MLA_NOPE + MLA_ROPE) ** -0.5
MLA_C2 = MLA_SCALE * math.log2(math.e)
ONE_LANE = MLA_V
NEG = -1e30


def _tri_mask(n, lower_rows_ge_cols=True):
    r = lax.broadcasted_iota(jnp.int32, (n, n), 0)
    c = lax.broadcasted_iota(jnp.int32, (n, n), 1)
    return r >= c if lower_rows_ge_cols else c >= r


def mla_fwd(q, k, v, gather, *, tq):
    T = q.shape[0]
    nq = T // tq
    rep = tq // HP
    ng = len(gather)

    def body(*refs):
        q_ref, k_ref, v_ref = refs[:3]
        srcs = refs[3:3 + ng]
        o_ref, lse_ref = refs[3 + ng:5 + ng]
        outs = refs[5 + ng:5 + 2 * ng]
        m_sc, acc_sc = refs[5 + 2 * ng:7 + 2 * ng]
        comm = (srcs, outs) + tuple(refs[7 + 2 * ng:])
        h, i = pl.program_id(0), pl.program_id(1)

        @pl.when((h == 0) & (i == 0))
        def _():
            _gather_start(*comm)

        qv = q_ref[...]
        m_sc[...] = jnp.full(m_sc.shape, NEG, F32)
        acc_sc[...] = jnp.zeros(acc_sc.shape, F32)

        def block(j, masked):
            off = pl.multiple_of(j * tq, tq)
            kb = k_ref[pl.ds(off, tq), :]
            vb = v_ref[pl.ds(off, tq), :]
            s = _dot_nt(qv, kb) * MLA_C2
            if masked:
                s = jnp.where(_tri_mask(tq), s, NEG)
            m_prev = m_sc[...]
            m_next = jnp.maximum(m_prev, jnp.max(s, axis=-1, keepdims=True))
            p = jnp.exp2(s - jnp.tile(m_next, (1, rep)))
            alpha = jnp.exp2(m_prev - m_next)
            acc_sc[...] = alpha * acc_sc[...] + _dot(p.astype(BF16), vb)
            m_sc[...] = m_next

        def loop_body(j, carry):
            block(j, False)
            return carry

        lax.fori_loop(0, i, loop_body, 0)
        block(i, True)
        acc = acc_sc[...]
        l = acc[:, ONE_LANE:ONE_LANE + 1]
        o_ref[...] = (acc / l).astype(o_ref.dtype)
        lse_ref[...] = m_sc[...] + jnp.log(l) * math.log2(math.e)

        @pl.when((h == MLA_H - 1) & (i == nq - 1))
        def _():
            _gather_wait(*comm)

    blk = pl.BlockSpec((tq, HP), lambda h, i: (i, h))
    full = pl.BlockSpec((T, HP), lambda h, i: (0, h))
    return pl.pallas_call(
        body,
        name="mla_fwd",
        grid=(MLA_H, nq),
        in_specs=[blk, full, full] + [ANY] * ng,
        out_specs=[blk, blk] + [ANY] * ng,
        out_shape=[_sds((T, MLA_H * HP), BF16), _sds((T, MLA_H * HP), F32)]
        + [_sds((N_DEV,) + p.shape, p.dtype) for p in gather],
        scratch_shapes=[pltpu.VMEM((tq, HP), F32), pltpu.VMEM((tq, HP), F32)] + _gather_scratch(ng),
        compiler_params=_cp(2),
    )(q, k, v, *gather)


def mla_prep(o, do, lse, *, tq):
    T = o.shape[0]
    nq = T // tq

    def body(o_ref, do_ref, lse_ref, lse_row_ref, delta_row_ref):
        for h in range(MLA_H):
            sl = slice(h * HP, (h + 1) * HP)
            d = jnp.sum(o_ref[:, sl].astype(F32) * do_ref[:, sl].astype(F32), axis=-1, keepdims=True)
            lse_row_ref[h] = lse_ref[:, sl].T[0:1, :]
            delta_row_ref[h] = jnp.broadcast_to(d, (tq, HP)).T[0:1, :]

    blk = pl.BlockSpec((tq, MLA_H * HP), lambda i: (i, 0))
    row = pl.BlockSpec((MLA_H, None, 1, tq), lambda i: (0, i, 0, 0))
    return pl.pallas_call(
        body,
        name="mla_prep",
        grid=(nq,),
        in_specs=[blk, blk, blk],
        out_specs=[row, row],
        out_shape=[_sds((MLA_H, nq, 1, tq), F32), _sds((MLA_H, nq, 1, tq), F32)],
        compiler_params=_cp(1),
    )(o, do, lse)


def mla_bwd(q, k, v, do, lse_row, delta_row, slots, *, tq):
    T = q.shape[0]
    nq = T // tq
    ns = len(slots)

    def body(*refs):
        q_ref, k_ref, v_ref, do_ref, lse_ref, delta_ref = refs[:6]
        srcs = [(r, True) for r in refs[6:6 + ns]]
        dq_ref, dk_ref, dv_ref = refs[6 + ns:9 + ns]
        dsts = refs[9 + ns:9 + 2 * ns]
        dk_sc, dv_sc = refs[9 + 2 * ns:11 + 2 * ns]
        comm = (srcs, dsts) + tuple(refs[11 + 2 * ns:])
        h, j = pl.program_id(0), pl.program_id(1)

        @pl.when((h == 0) & (j == 0))
        def _():
            _exchange_start(*comm)

        kv = k_ref[...]
        vv = v_ref[...]
        dk_sc[...] = jnp.zeros(dk_sc.shape, F32)
        dv_sc[...] = jnp.zeros(dv_sc.shape, F32)

        @pl.when(j == 0)
        def _():
            dq_ref[...] = jnp.zeros(dq_ref.shape, F32)

        def block(i, masked):
            off = pl.multiple_of(i * tq, tq)
            qb = q_ref[pl.ds(off, tq), :]
            dob = do_ref[pl.ds(off, tq), :]
            st = _dot_nt(kv, qb) * MLA_C2
            if masked:
                st = jnp.where(_tri_mask(tq, False), st, NEG)
            pt = jnp.exp2(st - lse_ref[i])
            dv_sc[...] += _dot(pt.astype(BF16), dob)
            dpt = _dot_nt(vv, dob)
            dst = (pt * (dpt - delta_ref[i]) * MLA_SCALE).astype(BF16)
            dk_sc[...] += _dot(dst, qb)
            dq_ref[pl.ds(off, tq), :] += _dot_tn(dst, kv)

        block(j, True)

        def loop_body(i, carry):
            block(i, False)
            return carry

        lax.fori_loop(j + 1, nq, loop_body, 0)
        dk_ref[...] = dk_sc[...].astype(dk_ref.dtype)
        dv_ref[...] = dv_sc[...].astype(dv_ref.dtype)

        @pl.when((h == MLA_H - 1) & (j == nq - 1))
        def _():
            _exchange_wait(*comm)

    blk = pl.BlockSpec((tq, HP), lambda h, j: (j, h))
    full = pl.BlockSpec((T, HP), lambda h, j: (0, h))
    rows = pl.BlockSpec((None, nq, 1, tq), lambda h, j: (h, 0, 0, 0))
    return pl.pallas_call(
        body,
        name="mla_bwd",
        grid=(MLA_H, nq),
        in_specs=[full, blk, blk, full, rows, rows] + [ANY] * ns,
        out_specs=[full, blk, blk] + [ANY] * ns,
        out_shape=[_sds((T, MLA_H * HP), F32), _sds((T, MLA_H * HP), BF16), _sds((T, MLA_H * HP), BF16)]
        + [_sds((N_DEV - 1,) + s.shape[1:], s.dtype) for s in slots],
        scratch_shapes=[pltpu.VMEM((tq, HP), F32), pltpu.VMEM((tq, HP), F32)] + _exchange_scratch(ns),
        compiler_params=_cp(2),
    )(q, k, v, do, lse_row, delta_row, *slots)


def _ret_consts():
    h = jnp.arange(RET_H, dtype=F32)
    log_g = jnp.log1p(-jnp.exp2(-5.0 - h))
    idx = jnp.arange(RET_C, dtype=F32)
    rel = idx[:, None] - idx[None, :]
    dmask = jnp.where(rel >= 0, jnp.exp(log_g[:, None, None] * jnp.maximum(rel, 0.0)), 0.0)
    zeta = jnp.exp(log_g[:, None] * (RET_C - 1.0 - idx)[None, :])
    xi = jnp.exp(log_g[:, None] * (idx + 1.0)[None, :])
    decay = jnp.exp(log_g * RET_C)
    zb = jnp.broadcast_to(zeta[:, :, None], (RET_H, RET_C, RET_D))
    xb = jnp.broadcast_to(xi[:, :, None], (RET_H, RET_C, RET_D))
    db = jnp.broadcast_to(decay[:, None, None], (RET_H, RET_C, RET_D))
    return dmask.astype(F32), zb.astype(F32), xb.astype(F32), db.astype(F32)


def ret_fwd(rq, rk, rv, consts, *, rb):
    T = rq.shape[0]
    nb = T // rb
    ncb = rb // RET_C

    def body(q_ref, k_ref, v_ref, dm_ref, z_ref, x_ref, dc_ref, o_ref, st_ref, r_sc):
        @pl.when(pl.program_id(1) == 0)
        def _():
            r_sc[...] = jnp.zeros(r_sc.shape, F32)

        dm, zt, xi, dc = dm_ref[...], z_ref[...], x_ref[...], dc_ref[...]
        for c in range(ncb):
            sl = slice(c * RET_C, (c + 1) * RET_C)
            q, k, v = q_ref[sl, :], k_ref[sl, :], v_ref[sl, :]
            r = r_sc[...]
            rbf = r.astype(BF16)
            st_ref[sl, :] = rbf
            s = _dot_nt(q, k) * dm
            inner = _dot(s.astype(BF16), v)
            cross = _dot((q.astype(F32) * xi).astype(BF16), rbf)
            o_ref[sl, :] = inner + cross
            kz = (k.astype(F32) * zt).T.astype(BF16)
            r_sc[...] = r * dc + _dot(kz, v)

    blk = pl.BlockSpec((rb, RET_D), lambda h, b: (b, h))
    cst = pl.BlockSpec((None, RET_C, RET_D), lambda h, b: (h, 0, 0))
    return pl.pallas_call(
        body,
        name="ret_fwd",
        grid=(RET_H, nb),
        in_specs=[blk, blk, blk, cst, cst, cst, cst],
        out_specs=[blk, blk],
        out_shape=[_sds((T, RET_H * RET_D), F32), _sds((T, RET_H * RET_D), BF16)],
        scratch_shapes=[pltpu.VMEM((RET_D, RET_D), F32)],
        compiler_params=_cp(2),
    )(rq, rk, rv, *consts)


def ret_bwd(rq, rk, rv, st, dret, consts, *, rb):
    T = rq.shape[0]
    nb = T // rb
    ncb = rb // RET_C

    def body(q_ref, k_ref, v_ref, st_ref, do_ref, dm_ref, z_ref, x_ref, dc_ref, dq_ref, dk_ref, dv_ref, g_sc):
        @pl.when(pl.program_id(1) == 0)
        def _():
            g_sc[...] = jnp.zeros(g_sc.shape, F32)

        dm, zt, xi, dc = dm_ref[...], z_ref[...], x_ref[...], dc_ref[...]
        for c in reversed(range(ncb)):
            sl = slice(c * RET_C, (c + 1) * RET_C)
            q, k, v, rp = q_ref[sl, :], k_ref[sl, :], v_ref[sl, :], st_ref[sl, :]
            dob = do_ref[sl, :].astype(BF16)
            qf, kf = q.astype(F32), k.astype(F32)
            gn = g_sc[...]
            gnb = gn.astype(BF16)
            s = _dot_nt(q, k) * dm
            ds = _dot_nt(dob, v) * dm
            dq = _dot(ds.astype(BF16), k) + _dot_nt(dob, rp) * xi
            dk = _dot(ds.T.astype(BF16), q) + _dot_nt(v, gnb) * zt
            dv = _dot(s.T.astype(BF16), dob) + _dot((kf * zt).astype(BF16), gnb)
            dq_ref[sl, :] = dq.astype(dq_ref.dtype)
            dk_ref[sl, :] = dk.astype(dk_ref.dtype)
            dv_ref[sl, :] = dv.astype(dv_ref.dtype)
            g_sc[...] = _dot((qf * xi).T.astype(BF16), dob) + dc * gn

    blk = pl.BlockSpec((rb, RET_D), lambda h, b: (nb - 1 - b, h))
    cst = pl.BlockSpec((None, RET_C, RET_D), lambda h, b: (h, 0, 0))
    return pl.pallas_call(
        body,
        name="ret_bwd",
        grid=(RET_H, nb),
        in_specs=[blk, blk, blk, blk, blk, cst, cst, cst, cst],
        out_specs=[blk, blk, blk],
        out_shape=[_sds((T, RET_H * RET_D), F32)] * 3,
        scratch_shapes=[pltpu.VMEM((RET_D, RET_D), F32)],
        compiler_params=_cp(2),
    )(rq, rk, rv, st, dret, *consts)


HALO = 16


def conv_act_fwd(up_pre, w_conv, b_conv, *, tile, cw):
    T = up_pre.shape[0]
    nt = T // tile
    ncol = D_FF // cw
    hb = tile // HALO

    def body(pa_ref, a_ref, pb_ref, b_ref, wa_ref, wb_ref, ba_ref, bb_ref, o_ref):
        i = pl.program_id(1)
        keep = (i > 0).astype(F32)

        def conv(prev_ref, cur_ref, w_ref, bias_ref):
            ext = jnp.concatenate([prev_ref[...].astype(F32) * keep, cur_ref[...].astype(F32)], axis=0)
            w = w_ref[...]
            y = ext * w[2:3, :] + pltpu.roll(ext, 1, 0) * w[1:2, :] + pltpu.roll(ext, 2, 0) * w[0:1, :] + bias_ref[...]
            return y[HALO:, :]

        a = conv(pa_ref, a_ref, wa_ref, ba_ref)
        b = conv(pb_ref, b_ref, wb_ref, bb_ref)
        o_ref[...] = (a * _sigmoid(a) * b).astype(o_ref.dtype)

    prev_a = pl.BlockSpec((HALO, cw), lambda j, i: (jnp.maximum(i * hb - 1, 0), j))
    cur_a = pl.BlockSpec((tile, cw), lambda j, i: (i, j))
    prev_b = pl.BlockSpec((HALO, cw), lambda j, i: (jnp.maximum(i * hb - 1, 0), j + ncol))
    cur_b = pl.BlockSpec((tile, cw), lambda j, i: (i, j + ncol))
    w_a = pl.BlockSpec((3, cw), lambda j, i: (0, j))
    w_b = pl.BlockSpec((3, cw), lambda j, i: (0, j + ncol))
    bias_a = pl.BlockSpec((1, cw), lambda j, i: (0, j))
    bias_b = pl.BlockSpec((1, cw), lambda j, i: (0, j + ncol))
    return pl.pallas_call(
        body,
        name="conv_act_fwd",
        grid=(ncol, nt),
        in_specs=[prev_a, cur_a, prev_b, cur_b, w_a, w_b, bias_a, bias_b],
        out_specs=pl.BlockSpec((tile, cw), lambda j, i: (i, j)),
        out_shape=_sds((T, D_FF), BF16),
        compiler_params=_cp(2),
    )(up_pre, up_pre, up_pre, up_pre, w_conv, w_conv, b_conv, b_conv)


def conv_act_bwd(up_pre, dact, w_conv, b_conv, *, tile, cw):
    T = up_pre.shape[0]
    nt = T // tile
    ncol = D_FF // cw
    hb = tile // HALO
    ext_rows = tile + 2 * HALO

    def body(pa_ref, a_ref, na_ref, pb_ref, b_ref, nb_ref, d_ref, nd_ref, wa_ref, wb_ref, ba_ref, bb_ref,
             dxa_ref, dxb_ref, sa_ref, sb_ref):
        i = pl.program_id(1)
        keep_p = (i > 0).astype(F32)
        keep_n = (i < nt - 1).astype(F32)

        def ext_of(prev_ref, cur_ref, next_ref):
            return jnp.concatenate(
                [prev_ref[...].astype(F32) * keep_p, cur_ref[...].astype(F32), next_ref[...].astype(F32) * keep_n], axis=0)

        def conv(ext, w, bias):
            return ext * w[2:3, :] + pltpu.roll(ext, 1, 0) * w[1:2, :] + pltpu.roll(ext, 2, 0) * w[0:1, :] + bias

        xa = ext_of(pa_ref, a_ref, na_ref)
        xb = ext_of(pb_ref, b_ref, nb_ref)
        wa, wb = wa_ref[...], wb_ref[...]
        a = conv(xa, wa, ba_ref[...])
        b = conv(xb, wb, bb_ref[...])
        dy = jnp.concatenate(
            [jnp.zeros((HALO, cw), F32), d_ref[...].astype(F32), nd_ref[...].astype(F32) * keep_n], axis=0)
        sg = _sigmoid(a)
        da = dy * b * (sg * (1.0 + a * (1.0 - sg)))
        db = dy * (a * sg)

        def back(dup, ext, w, dx_ref, s_ref):
            dx = dup * w[2:3, :] + pltpu.roll(dup, ext_rows - 1, 0) * w[1:2, :] + pltpu.roll(dup, ext_rows - 2, 0) * w[0:1, :]
            dx_ref[...] = dx[HALO:HALO + tile, :].astype(dx_ref.dtype)
            dc = dup[HALO:HALO + tile, :]
            r2 = jnp.sum(dc * ext[HALO:HALO + tile, :], axis=0, keepdims=True)
            r1 = jnp.sum(dc * pltpu.roll(ext, 1, 0)[HALO:HALO + tile, :], axis=0, keepdims=True)
            r0 = jnp.sum(dc * pltpu.roll(ext, 2, 0)[HALO:HALO + tile, :], axis=0, keepdims=True)
            rb = jnp.sum(dc, axis=0, keepdims=True)
            row = lax.broadcasted_iota(jnp.int32, (8, cw), 0)
            upd = (jnp.where(row == 0, r0, 0.0) + jnp.where(row == 1, r1, 0.0) + jnp.where(row == 2, r2, 0.0)
                   + jnp.where(row == 3, rb, 0.0))

            @pl.when(i == 0)
            def _():
                s_ref[...] = upd

            @pl.when(i > 0)
            def _():
                s_ref[...] += upd

        back(da, xa, wa, dxa_ref, sa_ref)
        back(db, xb, wb, dxb_ref, sb_ref)

    def prev_of(shift):
        return pl.BlockSpec((HALO, cw), lambda j, i: (jnp.maximum(i * hb - 1, 0), j + shift))

    def next_of(shift):
        return pl.BlockSpec((HALO, cw), lambda j, i: (jnp.minimum((i + 1) * hb, nt * hb - 1), j + shift))

    def cur_of(shift):
        return pl.BlockSpec((tile, cw), lambda j, i: (i, j + shift))

    def row_of(rows, shift):
        return pl.BlockSpec((rows, cw), lambda j, i: (0, j + shift))

    return pl.pallas_call(
        body,
        name="conv_act_bwd",
        grid=(ncol, nt),
        in_specs=[prev_of(0), cur_of(0), next_of(0), prev_of(ncol), cur_of(ncol), next_of(ncol), cur_of(0), next_of(0),
                  row_of(3, 0), row_of(3, ncol), row_of(1, 0), row_of(1, ncol)],
        out_specs=[cur_of(0), cur_of(0), row_of(8, 0), row_of(8, 0)],
        out_shape=[_sds((T, D_FF), BF16), _sds((T, D_FF), BF16), _sds((8, D_FF), F32), _sds((8, D_FF), F32)],
        compiler_params=_cp(2),
    )(up_pre, up_pre, up_pre, up_pre, up_pre, up_pre, dact, dact, w_conv, w_conv, b_conv, b_conv)


NS_IN, NS_UP, NS_XKV, NS_UQ = 564, 704, 256, 96
E2_ROWS, E2_UQ = 384, 128
L1_W = NS_UP + NS_XKV
L2_ROWS = 736
L2_OUT, L2_XQ, L2_XO, L2_DN = 0, 128, 256, 384
RS_DN = 352
L3_ROWS, L3_PRET = 1024, 512
L4_SHAPE = (8, 768)
WZ_RUNS = ((416, 4096, 0), (0, 384, 4096), (384, 32, 4544))
WZ_ZERO = ((4480, 4544), (4576, 4608))


def _pieces(orig_start, length, dst_start, ns):
    out, c, d, end = [], orig_start, dst_start, orig_start + length
    while c < end:
        j, off = c // ns, c % ns
        ln = min(ns - off, end - c)
        out.append((j, off, d, ln))
        c += ln
        d += ln
    return out


def pack_early(w_in, w_ukv, w_uq):
    def body(in_ref, ukv_ref, uq_ref, e1_ref, e2_ref):
        e1_ref[...] = in_ref[...].astype(BF16)
        e2_ref[0:E2_UQ, :] = ukv_ref[...].astype(BF16)
        e2_ref[E2_UQ:E2_ROWS, 0:NS_UQ] = uq_ref[...].astype(BF16)
        e2_ref[E2_UQ:E2_ROWS, NS_UQ:HP] = jnp.zeros((E2_ROWS - E2_UQ, HP - NS_UQ), BF16)

    return pl.pallas_call(
        body,
        name="pack_early",
        in_specs=[VM] * 3,
        out_specs=[VM] * 2,
        out_shape=[_sds((D, NS_IN), BF16), _sds((E2_ROWS, HP), BF16)],
        compiler_params=_cp0(),
    )(w_in, w_ukv, w_uq)


def pack_late(w_up, w_xkv, w_out, w_xq, w_xo, w_down, w_pmla, w_pret, w_conv):
    def body(up_ref, xkv_ref, o_ref, xq_ref, xo_ref, dn_ref, pm_ref, pr_ref, cv_ref, l1_ref, l2_ref, l3_ref, l4_ref):
        l1_ref[:, 0:NS_UP] = up_ref[...].astype(BF16)
        l1_ref[:, NS_UP:L1_W] = xkv_ref[...].astype(BF16)
        l2_ref[L2_OUT:L2_XQ, :] = o_ref[...].astype(BF16)
        l2_ref[L2_XQ:L2_XO, :] = xq_ref[...].astype(BF16)
        l2_ref[L2_XO:L2_DN, :] = xo_ref[...].astype(BF16)
        l2_ref[L2_DN:L2_ROWS, :] = dn_ref[...].astype(BF16)
        l3_ref[0:L3_PRET, :] = pm_ref[...].astype(BF16)
        l3_ref[L3_PRET:L3_ROWS, :] = pr_ref[...].astype(BF16)
        l4_ref[...] = jnp.zeros(L4_SHAPE, F32)
        l4_ref[0:3, 0:NS_UP] = cv_ref[...]

    return pl.pallas_call(
        body,
        name="pack_late",
        in_specs=[VM] * 9,
        out_specs=[VM] * 4,
        out_shape=[_sds((D, L1_W), BF16), _sds((L2_ROWS, D), BF16), _sds((L3_ROWS, HP), BF16), _sds(L4_SHAPE, F32)],
        compiler_params=_cp0(),
    )(w_up, w_xkv, w_out, w_xq, w_xo, w_down, w_pmla, w_pret, w_conv)


def assemble_early(g1, g2, *, tile):
    def body(g1_ref, g2_ref, wz_ref, wk_ref, wv_ref, wq_ref):
        for lo, hi in WZ_ZERO:
            wz_ref[:, lo:hi] = jnp.zeros((tile, hi - lo), BF16)
        for os_, ln_, ds_ in WZ_RUNS:
            for j, off, d, ln in _pieces(os_, ln_, ds_, NS_IN):
                wz_ref[:, d:d + ln] = g1_ref[j, :, off:off + ln]

        @pl.when(pl.program_id(0) == 0)
        def _():
            half = jnp.zeros((MLA_KVR, HP - MLA_NOPE), BF16)
            for j in range(N_DEV):
                wk_ref[:, j * HP:j * HP + MLA_NOPE] = g2_ref[j, 0:E2_UQ, 0:MLA_NOPE]
                wk_ref[:, j * HP + MLA_NOPE:(j + 1) * HP] = half
                wv_ref[:, j * HP:j * HP + MLA_V] = g2_ref[j, 0:E2_UQ, MLA_NOPE:HP]
                wv_ref[:, j * HP + MLA_V:(j + 1) * HP] = half
                wq_ref[:, j * HP:j * HP + NS_UQ] = g2_ref[j, E2_UQ:E2_ROWS, 0:NS_UQ]
                wq_ref[:, j * HP + NS_UQ:(j + 1) * HP] = jnp.zeros((MLA_QR, HP - NS_UQ), BF16)

    def whole(r):
        return pl.BlockSpec((r, MLA_H * HP), lambda i: (0, 0))

    return pl.pallas_call(
        body,
        name="assemble_early",
        grid=(D // tile,),
        in_specs=[pl.BlockSpec((N_DEV, tile, NS_IN), lambda i: (0, i, 0)),
                  pl.BlockSpec((N_DEV, E2_ROWS, HP), lambda i: (0, 0, 0))],
        out_specs=[pl.BlockSpec((tile, ZW), lambda i: (i, 0)), whole(MLA_KVR), whole(MLA_KVR), whole(MLA_QR)],
        out_shape=[_sds((D, ZW), BF16), _sds((MLA_KVR, MLA_H * HP), BF16), _sds((MLA_KVR, MLA_H * HP), BF16),
                   _sds((MLA_QR, MLA_H * HP), BF16)],
        compiler_params=_cp(1),
    )(g1, g2)


def assemble_l1(g1, *, tile):
    def body(g_ref, wup_ref, wxkv_ref):
        for j in range(N_DEV):
            wup_ref[:, j * NS_UP:(j + 1) * NS_UP] = g_ref[j, :, 0:NS_UP]
            wxkv_ref[:, j * NS_XKV:(j + 1) * NS_XKV] = g_ref[j, :, NS_UP:L1_W]

    return pl.pallas_call(
        body,
        name="assemble_l1",
        grid=(D // tile,),
        in_specs=[pl.BlockSpec((N_DEV, tile, L1_W), lambda i: (0, i, 0))],
        out_specs=[pl.BlockSpec((tile, 2 * D_FF), lambda i: (i, 0)), pl.BlockSpec((tile, 2 * D), lambda i: (i, 0))],
        out_shape=[_sds((D, 2 * D_FF), BF16), _sds((D, 2 * D), BF16)],
        compiler_params=_cp(1),
    )(g1)


def assemble_l234(g2, g3, g4):
    def body(g2_ref, g3_ref, g4_ref, wo_ref, wxq_ref, wxo_ref, wdn_ref, wpa_ref, wpr_ref, wc_ref):
        for j in range(N_DEV):
            wo_ref[j * 128:(j + 1) * 128, :] = g2_ref[j, L2_OUT:L2_XQ, :]
            wxq_ref[j * 128:(j + 1) * 128, :] = g2_ref[j, L2_XQ:L2_XO, :]
            wxo_ref[j * 128:(j + 1) * 128, :] = g2_ref[j, L2_XO:L2_DN, :]
            wdn_ref[j * RS_DN:(j + 1) * RS_DN, :] = g2_ref[j, L2_DN:L2_ROWS, :]
            wpr_ref[:, j * 128:(j + 1) * 128] = g3_ref[j, L3_PRET:L3_ROWS, :]
            wc_ref[:, j * NS_UP:(j + 1) * NS_UP] = g4_ref[j, 0:3, 0:NS_UP]
            for h in range(MLA_H):
                wpa_ref[h * HP:h * HP + MLA_V, j * 128:(j + 1) * 128] = g3_ref[j, h * MLA_V:(h + 1) * MLA_V, :]
        for h in range(MLA_H):
            wpa_ref[h * HP + MLA_V:(h + 1) * HP, :] = jnp.zeros((HP - MLA_V, D), BF16)

    return pl.pallas_call(
        body,
        name="assemble_l234",
        in_specs=[VM] * 3,
        out_specs=[VM] * 7,
        out_shape=[_sds((D, D), BF16), _sds((D, D), BF16), _sds((D, D), BF16), _sds((D_FF, D), BF16),
                   _sds((MLA_H * HP, D), BF16), _sds((RET_H * RET_D, D), BF16), _sds((3, 2 * D_FF), F32)],
        compiler_params=_cp0(),
    )(g2, g3, g4)


def slots_l1(dwup_a, dwup_b, dwxkv, *, tile):
    per_half = D_FF // NS_UP

    def body(ua_ref, ub_ref, x_ref, s_ref):
        for j in range(N_DEV):
            src = ua_ref if j < per_half else ub_ref
            c0 = (j % per_half) * NS_UP
            s_ref[j, :, 0:NS_UP] = src[:, c0:c0 + NS_UP].astype(BF16)
            s_ref[j, :, NS_UP:L1_W] = x_ref[:, j * NS_XKV:(j + 1) * NS_XKV].astype(BF16)

    return pl.pallas_call(
        body,
        name="slots_l1",
        grid=(D // tile,),
        in_specs=[pl.BlockSpec((tile, D_FF), lambda i: (i, 0)), pl.BlockSpec((tile, D_FF), lambda i: (i, 0)),
                  pl.BlockSpec((tile, 2 * D), lambda i: (i, 0))],
        out_specs=pl.BlockSpec((N_DEV, tile, L1_W), lambda i: (0, i, 0)),
        out_shape=_sds((N_DEV, D, L1_W), BF16),
        compiler_params=_cp(1),
    )(dwup_a, dwup_b, dwxkv)


def slots_l23(dwo, dwxq, dwxo, dwdn, dwpa, dwpr):
    def body(o_ref, xq_ref, xo_ref, dn_ref, pa_ref, pr_ref, l2_ref, l3_ref):
        l2_ref[L2_OUT:L2_XQ, :] = o_ref[...].astype(BF16)
        l2_ref[L2_XQ:L2_XO, :] = xq_ref[...].astype(BF16)
        l2_ref[L2_XO:L2_DN, :] = xo_ref[...].astype(BF16)
        l2_ref[L2_DN:L2_ROWS, :] = dn_ref[...].astype(BF16)
        for h in range(MLA_H):
            l3_ref[h * MLA_V:(h + 1) * MLA_V, :] = pa_ref[h * HP:h * HP + MLA_V, :].astype(BF16)
        l3_ref[L3_PRET:L3_ROWS, :] = pr_ref[...].astype(BF16)

    rows128 = pl.BlockSpec((128, D), lambda j: (j, 0))

    def cols(r):
        return pl.BlockSpec((r, 128), lambda j: (0, j))

    return pl.pallas_call(
        body,
        name="slots_l23",
        grid=(N_DEV,),
        in_specs=[rows128, rows128, rows128, pl.BlockSpec((RS_DN, D), lambda j: (j, 0)),
                  cols(MLA_H * HP), cols(RET_H * RET_D)],
        out_specs=[pl.BlockSpec((None, L2_ROWS, D), lambda j: (j, 0, 0)),
                   pl.BlockSpec((None, L3_ROWS, HP), lambda j: (j, 0, 0))],
        out_shape=[_sds((N_DEV, L2_ROWS, D), BF16), _sds((N_DEV, L3_ROWS, HP), BF16)],
        compiler_params=_cp(1),
    )(dwo, dwxq, dwxo, dwdn, dwpa, dwpr)


def slots_early(dwz, dwk, dwv, dwq, *, tile):
    def body(dz_ref, k_ref, v_ref, q_ref, s1_ref, s2_ref):
        for os_, ln_, ds_ in WZ_RUNS:
            for j, off, d, ln in _pieces(os_, ln_, ds_, NS_IN):
                s1_ref[j, :, off:off + ln] = dz_ref[:, d:d + ln].astype(BF16)

        @pl.when(pl.program_id(0) == 0)
        def _():
            for j in range(N_DEV):
                s2_ref[j, 0:E2_UQ, 0:MLA_NOPE] = k_ref[:, j * HP:j * HP + MLA_NOPE].astype(BF16)
                s2_ref[j, 0:E2_UQ, MLA_NOPE:HP] = v_ref[:, j * HP:j * HP + MLA_V].astype(BF16)
                s2_ref[j, E2_UQ:E2_ROWS, 0:NS_UQ] = q_ref[:, j * HP:j * HP + NS_UQ].astype(BF16)
                s2_ref[j, E2_UQ:E2_ROWS, NS_UQ:HP] = jnp.zeros((E2_ROWS - E2_UQ, HP - NS_UQ), BF16)

    def whole(r):
        return pl.BlockSpec((r, MLA_H * HP), lambda i: (0, 0))

    return pl.pallas_call(
        body,
        name="slots_early",
        grid=(D // tile,),
        in_specs=[pl.BlockSpec((tile, ZW), lambda i: (i, 0)), whole(MLA_KVR), whole(MLA_KVR), whole(MLA_QR)],
        out_specs=[pl.BlockSpec((N_DEV, tile, NS_IN), lambda i: (0, i, 0)),
                   pl.BlockSpec((N_DEV, E2_ROWS, HP), lambda i: (0, 0, 0))],
        out_shape=[_sds((N_DEV, D, NS_IN), BF16), _sds((N_DEV, E2_ROWS, HP), BF16)],
        compiler_params=_cp(1),
    )(dwz, dwk, dwv, dwq)


SMALL = (("g_mix", 1024, 0), ("b_gate", 2048, 1), ("g_q_lat", 256, 3), ("g_kv_lat", 128, 4), ("g_ret", 512, 5),
         ("g_cross", 1024, 6), ("g_mem", 1024, 7), ("g_ffn", 1024, 8), ("b_conv", 5632, 9), ("g_final", 1024, 15))
SMALL_DIRECT = tuple(s for s in SMALL if s[0] != "b_conv")
S_ROWS = 16


def _flat_pieces(n, row0):
    return [(row0 + k // D, k, min(D, n - k)) for k in range(0, n, D)]


def exchange_late(s1, s2, cs_a, cs_b, smalls):
    ns = len(smalls)

    def body(*refs):
        s1_ref, s2_ref, ca_ref, cb_ref = refs[:4]
        small_refs = refs[4:4 + ns]
        r1_ref, r2_ref, rd_ref, rs_ref, dsl_ref, own_ref = refs[4 + ns:10 + ns]
        sems = refs[10 + ns:]
        own_ref[...] = jnp.zeros(own_ref.shape, F32)
        for (name, n, row0), g_ref in zip(SMALL_DIRECT, small_refs):
            for r, c0, ln in _flat_pieces(n, row0):
                own_ref[r:r + 1, 0:ln] = g_ref[:, c0:c0 + ln]
        row0 = dict((s[0], s[2]) for s in SMALL)["b_conv"]
        for half, c_ref in enumerate((ca_ref, cb_ref)):
            k = half * D_FF
            end = k + D_FF
            while k < end:
                r, lane = row0 + k // D, k % D
                ln = min(D - lane, end - k)
                own_ref[r:r + 1, lane:lane + ln] = c_ref[3:4, k - half * D_FF:k - half * D_FF + ln]
                k += ln
        dsl_ref[...] = jnp.zeros(dsl_ref.shape, F32)
        per_half = D_FF // NS_UP
        for j in range(N_DEV):
            c_ref = ca_ref if j < per_half else cb_ref
            c0 = (j % per_half) * NS_UP
            dsl_ref[j, 0:3, 0:NS_UP] = c_ref[0:3, c0:c0 + NS_UP]
        comm = ([(s1_ref, True), (s2_ref, True), (dsl_ref, True), (own_ref, False)],
                [r1_ref, r2_ref, rd_ref, rs_ref]) + tuple(sems)
        _exchange_start(*comm)
        _exchange_wait(*comm)

    n1 = N_DEV - 1
    return pl.pallas_call(
        body,
        name="exchange_late",
        in_specs=[ANY, ANY] + [VM] * (2 + ns),
        out_specs=[ANY, ANY, VM, VM, VM, VM],
        out_shape=[_sds((n1,) + s1.shape[1:], BF16), _sds((n1,) + s2.shape[1:], BF16), _sds((n1,) + L4_SHAPE, F32),
                   _sds((n1, S_ROWS, D), F32), _sds((N_DEV,) + L4_SHAPE, F32), _sds((S_ROWS, D), F32)],
        scratch_shapes=_exchange_scratch(4),
        compiler_params=_cp0(),
    )(s1, s2, cs_a, cs_b, *smalls)


def _adamw(w, g, m, v):
    m = ADAM_B1 * m + (1.0 - ADAM_B1) * g
    v = ADAM_B2 * v + (1.0 - ADAM_B2) * (g * g)
    m_hat = m / (1.0 - ADAM_B1 ** ADAM_STEP)
    v_hat = v / (1.0 - ADAM_B2 ** ADAM_STEP)
    delta = -ADAM_LR * (m_hat / (jnp.sqrt(v_hat) + ADAM_EPS) + ADAM_WD * w)
    return delta, m, v


def _apply(g, refs, outs):
    d, mn, vn = _adamw(refs[0][...], g, refs[1][...], refs[2][...])
    outs[0][...] = g
    outs[1][...] = d
    outs[2][...] = mn
    outs[3][...] = vn


def adam_cols(own, recv, wmv, spans, *, name, tile):
    R, W = own.shape
    nw = len(spans)

    def body(*refs):
        own_ref, recv_ref = refs[:2]
        ins, outs = refs[2:2 + 3 * nw], refs[2 + 3 * nw:]
        g = own_ref[...].astype(F32)
        for k in range(N_DEV - 1):
            g = g + recv_ref[k].astype(F32)
        for t, (lo, hi) in enumerate(spans):
            _apply(g[:, lo:hi], ins[3 * t:3 * t + 3], outs[4 * t:4 * t + 4])

    def blk(w):
        return pl.BlockSpec((tile, w), lambda i: (i, 0))

    widths = [hi - lo for lo, hi in spans]
    return pl.pallas_call(
        body,
        name=name,
        grid=(R // tile,),
        in_specs=[blk(W), pl.BlockSpec((N_DEV - 1, tile, W), lambda i: (0, i, 0))] + [blk(w) for w in widths for _ in range(3)],
        out_specs=[blk(w) for w in widths for _ in range(4)],
        out_shape=[_sds((R, w), F32) for w in widths for _ in range(4)],
        compiler_params=_cp(1),
    )(own, recv, *wmv)


def adam_rows(own, recv, wmv, spans, *, name):
    nw = len(spans)

    def body(*refs):
        own_ref, recv_ref = refs[:2]
        ins, outs = refs[2:2 + 3 * nw], refs[2 + 3 * nw:]
        for t, (lo, hi, w) in enumerate(spans):
            g = own_ref[lo:hi, :].astype(F32)
            for k in range(N_DEV - 1):
                g = g + recv_ref[k, lo:hi, :].astype(F32)
            _apply(g[:, 0:w], ins[3 * t:3 * t + 3], outs[4 * t:4 * t + 4])

    return pl.pallas_call(
        body,
        name=name,
        in_specs=[VM] * (2 + 3 * nw),
        out_specs=[VM] * (4 * nw),
        out_shape=[_sds((hi - lo, w), F32) for lo, hi, w in spans for _ in range(4)],
        compiler_params=_cp0(),
    )(own, recv, *wmv)


def adam_small(own_s, recv_s, dslots, recv_d, wmv_small, wmv_conv):
    ns = len(SMALL)

    def body(*refs):
        own_ref, rs_ref, dsl_ref, rd_ref = refs[:4]
        ins = refs[4:4 + 3 * ns + 3]
        outs = refs[4 + 3 * ns + 3:4 + 3 * ns + 3 + 4 * ns + 4]
        all_sc = refs[-1]
        me = _my_id()
        all_sc[0] = own_ref[...]
        for k in range(N_DEV - 1):
            all_sc[k + 1] = rs_ref[k]
        g = all_sc[jnp.bitwise_xor(me, 0)]
        for s in range(1, N_DEV):
            g = g + all_sc[jnp.bitwise_xor(me, s)]
        all_sc[0] = g
        for t, (name, n, row0) in enumerate(SMALL):
            pieces = [all_sc[0, r:r + 1, 0:ln] for r, _, ln in _flat_pieces(n, row0)]
            gt = pieces[0] if len(pieces) == 1 else jnp.concatenate(pieces, axis=1)
            _apply(gt, ins[3 * t:3 * t + 3], outs[4 * t:4 * t + 4])
        gc = dsl_ref[me]
        for k in range(N_DEV - 1):
            gc = gc + rd_ref[k]
        _apply(gc[0:3, 0:NS_UP], ins[3 * ns:3 * ns + 3], outs[4 * ns:4 * ns + 4])

    out_shape = [_sds((1, n), F32) for _, n, _ in SMALL for _ in range(4)] + [_sds((3, NS_UP), F32)] * 4
    return pl.pallas_call(
        body,
        name="adam_small",
        in_specs=[VM] * (4 + 3 * ns + 3),
        out_specs=[VM] * len(out_shape),
        out_shape=out_shape,
        scratch_shapes=[pltpu.VMEM((N_DEV, S_ROWS, D), F32)],
    )(own_s, recv_s, dslots, recv_d, *wmv_small, *wmv_conv)


def kernel(x, mem, positions, g_mix, w_in, b_gate, g_q_lat, w_uq, g_kv_lat, w_ukv, w_proj_mla, g_ret, w_proj_ret, w_out, g_cross, g_mem, w_xq, w_xkv, w_xo, g_ffn, w_up, w_conv, b_conv, w_down, g_final, loss_target, m_g_mix, m_w_in, m_b_gate, m_g_q_lat, m_w_uq, m_g_kv_lat, m_w_ukv, m_w_proj_mla, m_g_ret, m_w_proj_ret, m_w_out, m_g_cross, m_g_mem, m_w_xq, m_w_xkv, m_w_xo, m_g_ffn, m_w_up, m_w_conv, m_b_conv, m_w_down, m_g_final, v_g_mix, v_w_in, v_b_gate, v_g_q_lat, v_w_uq, v_g_kv_lat, v_w_ukv, v_w_proj_mla, v_g_ret, v_w_proj_ret, v_w_out, v_g_cross, v_g_mem, v_w_xq, v_w_xkv, v_w_xo, v_g_ffn, v_w_up, v_w_conv, v_b_conv, v_w_down, v_g_final):
    args = dict(locals())
    T = x.shape[1]
    M = mem.shape[1]
    tile = min(256, T)
    tq = min(512, T)
    rb = min(1024, T)

    xs = x[0]
    tgt = loss_target[0]
    mems = mem[0]

    def shard(name, prefix=""):
        a = args[prefix + name]
        return a.reshape(a.shape[-2:]) if a.ndim >= 2 else a.reshape(1, -1)

    e1, e2 = pack_early(shard("w_in"), shard("w_ukv"), shard("w_uq"))
    ge1, ge2 = all_gather_groups([e1, e2], name="gather_early")
    wz, wk, wv, wq = assemble_early(ge1, ge2, tile=tile)
    late_parts = pack_late(shard("w_up"), shard("w_xkv"), shard("w_out"), shard("w_xq"), shard("w_xo"), shard("w_down"),
                           shard("w_proj_mla"), shard("w_proj_ret"), shard("w_conv"))

    pos_f = jnp.broadcast_to(positions[0].astype(F32)[:, None], (T, 128))
    cm, s1, s2, cr, sr = rot_tables(pos_f, tile=tile)
    rconsts = _ret_consts()

    u = rowwise(lambda xv, g: _rms(xv, g), [(xs, None)], [g_mix], [(D, BF16)], [], tile=tile, name="norm_mix")[0]
    z = matmul(u, wz, name="mm_z")

    def lat_fwd(zl, cmv, s1v, s2v, gq, gkv, wqv, wkv_, wvv):
        cq = _rms(zl[:, 0:256], gq).astype(BF16)
        ckv = _rms(zl[:, 256:384], gkv).astype(BF16)
        qv = _rot_mla(_dot(cq, wqv), cmv, s1v, s2v)
        kr = _rot_mla(zl[:, 384:512], cmv, s1v, s2v)
        kn = _dot(ckv, wkv_)
        kv_ = jnp.concatenate([kn[:, h * HP:(h + 1) * HP] + kr for h in range(MLA_H)], axis=1)
        vv = _dot(ckv, wvv)
        lane = lax.broadcasted_iota(jnp.int32, vv.shape, 1)
        vv = jnp.where((lane & (HP - 1)) == ONE_LANE, 1.0, vv)
        return qv, kv_, vv

    q_a, k_a, v_a = rowwise(
        lat_fwd, [(z, (512, 8)), (cm, None), (s1, None), (s2, None)], [g_q_lat, g_kv_lat, wq, wk, wv],
        [(MLA_H * HP, BF16)] * 3, [], tile=tile, name="lat_fwd")

    def retprep_fwd(zr, crv, srv):
        rqv = _rot_ret(zr[:, 0:512], crv, srv)
        rkv = _rot_ret(zr[:, 512:1024], crv, srv) * (RET_D ** -0.5)
        return rqv, rkv, zr[:, 1024:1536]

    rq, rk, rv = rowwise(retprep_fwd, [(z, (2048, 0)), (cr, None), (sr, None)], [], [(512, BF16)] * 3, [],
                         tile=tile, name="retprep_fwd")

    o_a, lse, gl1, gl2, gl3, gl4 = mla_fwd(q_a, k_a, v_a, list(late_parts), tq=tq)
    wup, wxkv = assemble_l1(gl1, tile=tile)
    wo, wxq, wxo, wdn, wpa, wpr, wcv = assemble_l234(gl2, gl3, gl4)
    ret, rstate = ret_fwd(rq, rk, rv, rconsts, rb=rb)

    def gn_parts(r):
        outs = []
        for h in range(RET_H):
            rh = r[:, h * RET_D:(h + 1) * RET_D]
            mu = jnp.mean(rh, axis=-1, keepdims=True)
            dlt = rh - mu
            rstd = lax.rsqrt(jnp.mean(dlt * dlt, axis=-1, keepdims=True) + EPS)
            outs.append((dlt * rstd, rstd))
        return outs

    def mix_fwd(ov, rv_, rg, gt, wpav, wprv, gr, bg):
        ya = _dot(ov, wpav)
        xh = jnp.concatenate([p[0] for p in gn_parts(rv_)], axis=1)
        t = rg * _sigmoid(rg) * (xh * gr)
        yr = _dot(t.astype(BF16), wprv)
        ga_ = _sigmoid(gt[:, :D] + bg[:, :D])
        gr_ = _sigmoid(gt[:, D:] + bg[:, D:])
        return ga_ * ya + gr_ * yr

    mix = rowwise(mix_fwd, [(o_a, None), (ret, None), (z, (512, 3)), (z, (2048, 1))], [wpa, wpr, g_ret, b_gate],
                  [(D, BF16)], [], tile=tile, name="mix_fwd")[0]
    h1 = matmul(mix, wo, res=xs, name="mm_out")
    n2 = rowwise(lambda hv, g: _rms(hv, g), [(h1, None)], [g_cross], [(D, BF16)], [], tile=tile, name="norm_cross")[0]
    xq = matmul(n2, wxq, out_dtype=BF16, name="mm_xq")
    mn = rowwise(lambda mv_, g: _rms(mv_, g), [(mems, None)], [g_mem], [(D, BF16)], [], tile=min(tile, M), name="norm_mem")[0]
    mkv = matmul(mn, wxkv, out_dtype=BF16, name="mm_mkv")

    x_scale = X_HD ** -0.5

    def xattn_fwd(xqv, mkvv):
        outs = []
        for h in range(X_H):
            sl = slice(h * X_HD, (h + 1) * X_HD)
            s = _dot_nt(xqv[:, sl], mkvv[:, sl]) * x_scale
            s = s - jnp.max(s, axis=-1, keepdims=True)
            e = jnp.exp(s)
            p = e / jnp.sum(e, axis=-1, keepdims=True)
            outs.append(_dot(p.astype(BF16), mkvv[:, D + h * X_HD:D + (h + 1) * X_HD]))
        return jnp.concatenate(outs, axis=1)

    xo = rowwise(xattn_fwd, [(xq, None)], [mkv], [(D, BF16)], [], tile=tile, name="xattn_fwd")[0]
    h2 = matmul(xo, wxo, res=h1, name="mm_xo")
    n3 = rowwise(lambda hv, g: _rms(hv, g), [(h2, None)], [g_ffn], [(D, BF16)], [], tile=tile, name="norm_ffn")[0]
    up_pre = matmul(n3, wup, out_dtype=BF16, name="mm_up")
    cw = D_FF // 2
    act = conv_act_fwd(up_pre, wcv, b_conv, tile=tile, cw=cw)
    h3 = matmul(act, wdn, res=h2, name="mm_down")

    def loss_fn(hv, tv, g):
        y = _rms(hv, g)
        err = y - tv
        part = 0.5 * jnp.sum(jnp.sum(err * err, axis=-1, keepdims=True) / D, axis=0, keepdims=True)
        dx, dg = _rms_bwd(hv, g, err / D)
        return dx, dg, jnp.broadcast_to(part, (8, 128))

    g_fin2 = g_final.reshape(1, D)
    dh3, dg_final, loss_acc = rowwise(loss_fn, [(h3, None), (tgt, None)], [g_fin2], [(D, F32)],
                                      [((1, D), F32), ((8, 128), F32)], tile=tile, name="loss_bwd")
    loss = lax.psum(loss_acc[0, 0], ("x", "y", "c"))

    dact = matmul(dh3, wdn, tb=True, out_dtype=BF16, name="mm_dact")
    dw_down = matmul_tn(act, dh3, name="mm_dw_down", tm=1408)
    dup_a, dup_b, cs_a, cs_b = conv_act_bwd(up_pre, dact, wcv, b_conv, tile=tile, cw=cw)
    dn3 = matmul2_tb(dup_a, dup_b, wup, name="mm_dn3")
    dw_up_a = matmul_tn(n3, dup_a, name="mm_dw_up_a", tn=1408)
    dw_up_b = matmul_tn(n3, dup_b, name="mm_dw_up_b", tn=1408)

    def norm_bwd(hv, dyv, drv, g):
        dx, dg = _rms_bwd(hv, g, dyv)
        return dx + drv, dg

    dh2, dg_ffn = rowwise(norm_bwd, [(h2, None), (dn3, None), (dh3, None)], [g_ffn], [(D, F32)], [((1, D), F32)],
                          tile=tile, name="norm_ffn_bwd")
    dxo = matmul(dh2, wxo, tb=True, out_dtype=BF16, name="mm_dxo")
    dw_xo = matmul_tn(xo, dh2, name="mm_dw_xo")

    def xattn_bwd(xqv, dxov, mkvv):
        dxq, dmk, dmv = [], [], []
        for h in range(X_H):
            sl = slice(h * X_HD, (h + 1) * X_HD)
            slv = slice(D + h * X_HD, D + (h + 1) * X_HD)
            s = _dot_nt(xqv[:, sl], mkvv[:, sl]) * x_scale
            s = s - jnp.max(s, axis=-1, keepdims=True)
            e = jnp.exp(s)
            p = e / jnp.sum(e, axis=-1, keepdims=True)
            dp = _dot_nt(dxov[:, sl], mkvv[:, slv])
            ds = (p * (dp - jnp.sum(dp * p, axis=-1, keepdims=True)) * x_scale).astype(BF16)
            dxq.append(_dot(ds, mkvv[:, sl]))
            dmk.append(_dot_tn(ds, xqv[:, sl]))
            dmv.append(_dot_tn(p.astype(BF16), dxov[:, sl]))
        return jnp.concatenate(dxq, axis=1), jnp.concatenate(dmk + dmv, axis=1)

    dxq, dmkv = rowwise(xattn_bwd, [(xq, None), (dxo, None)], [mkv], [(D, BF16)], [((M, 2 * D), F32)],
                        tile=tile, name="xattn_bwd")
    dn2 = matmul(dxq, wxq, tb=True, name="mm_dn2")
    dw_xq = matmul_tn(n2, dxq, name="mm_dw_xq")
    dh1, dg_cross = rowwise(norm_bwd, [(h1, None), (dn2, None), (dh2, None)], [g_cross], [(D, F32)], [((1, D), F32)],
                            tile=tile, name="norm_cross_bwd")
    dw_xkv = matmul_tn(mn, dmkv, name="mm_dw_xkv", tk=M)
    dmn = matmul(dmkv, wxkv, tb=True, name="mm_dmn", tm=M)
    dg_mem = rowwise(lambda mv_, dyv, g: _rms_bwd(mv_, g, dyv)[1], [(mems, None), (dmn, None)], [g_mem], [],
                     [((1, D), F32)], tile=min(tile, M), name="norm_mem_bwd")[0]

    dmix = matmul(dh1, wo, tb=True, out_dtype=BF16, name="mm_dmix")
    dw_out = matmul_tn(mix, dh1, name="mm_dw_out")

    def mix_bwd(ov, rv_, rg, gt, dmv_, wpav, wprv, gr, bg):
        dm_ = dmv_.astype(F32)
        ya = _dot(ov, wpav)
        parts = gn_parts(rv_)
        xh = jnp.concatenate([p[0] for p in parts], axis=1)
        yn = xh * gr
        sg = _sigmoid(rg)
        sl_ = rg * sg
        t = (sl_ * yn).astype(BF16)
        yr = _dot(t, wprv)
        ga_ = _sigmoid(gt[:, :D] + bg[:, :D])
        gr_ = _sigmoid(gt[:, D:] + bg[:, D:])
        dgates = jnp.concatenate([dm_ * ya * ga_ * (1.0 - ga_), dm_ * yr * gr_ * (1.0 - gr_)], axis=1)
        dya = (dm_ * ga_).astype(BF16)
        dyr = (dm_ * gr_).astype(BF16)
        do_ = _dot_nt(dya, wpav)
        dwpa_ = _dot_tn(ov, dya)
        dt = _dot_nt(dyr, wprv)
        dwpr_ = _dot_tn(t, dyr)
        drg_ = dt * yn * (sg * (1.0 + rg * (1.0 - sg)))
        dyn = dt * sl_
        dgr = jnp.sum(dyn * xh, axis=0, keepdims=True)
        dxh = dyn * gr
        drets = []
        for h in range(RET_H):
            sl = slice(h * RET_D, (h + 1) * RET_D)
            xhh, rstd = parts[h]
            dxhh = dxh[:, sl]
            drets.append(rstd * (dxhh - jnp.mean(dxhh, axis=-1, keepdims=True)
                                 - xhh * jnp.mean(dxhh * xhh, axis=-1, keepdims=True)))
        dret_ = jnp.concatenate(drets, axis=1)
        dbg = jnp.sum(dgates, axis=0, keepdims=True)
        return do_, dret_, drg_, dgates, dwpa_, dwpr_, dgr, dbg

    do_a, dret, drg, dgates, dwpa, dw_proj_ret, dg_ret, db_gate = rowwise(
        mix_bwd, [(o_a, None), (ret, None), (z, (512, 3)), (z, (2048, 1)), (dmix, None)], [wpa, wpr, g_ret, b_gate],
        [(MLA_H * HP, BF16), (512, F32), (512, BF16), (2 * D, BF16)],
        [((MLA_H * HP, D), F32), ((512, D), F32), ((1, 512), F32), ((1, 2 * D), F32)], tile=tile, name="mix_bwd")

    sl1 = slots_l1(dw_up_a, dw_up_b, dw_xkv, tile=tile)
    sl2, sl3 = slots_l23(dw_out, dw_xq, dw_xo, dw_down, dwpa, dw_proj_ret)
    lse_row, delta_row = mla_prep(o_a, do_a, lse, tq=tq)
    dq_a, dk_a, dv_a, rl1, rl2, rl3 = mla_bwd(q_a, k_a, v_a, do_a, lse_row, delta_row, [sl1, sl2, sl3], tq=tq)
    drq_r, drk_r, drv = ret_bwd(rq, rk, rv, rstate, dret, rconsts, rb=rb)

    def lat_bwd(zl, cmv, s1v, s2v, dqv, dkv_, dvv, gq, gkv, wqv, wkv_, wvv):
        cqf, ckvf = zl[:, 0:256], zl[:, 256:384]
        cq = _rms(cqf, gq).astype(BF16)
        ckv = _rms(ckvf, gkv).astype(BF16)
        dq_pre = _rot_mla(dqv.astype(F32), cmv, -s1v, -s2v).astype(BF16)
        dkf = dkv_.astype(F32)
        dkr = dkf[:, 0:HP]
        for h in range(1, MLA_H):
            dkr = dkr + dkf[:, h * HP:(h + 1) * HP]
        lane = lax.broadcasted_iota(jnp.int32, dkr.shape, 1)
        dzk = _rot_mla(jnp.where((lane >= 64) & (lane < 96), dkr, 0.0), cmv, -s1v, -s2v)
        dkb = dkv_.astype(BF16)
        dvb = dvv.astype(BF16)
        dcq_n = _dot_nt(dq_pre, wqv)
        dckv_n = _dot_nt(dkb, wkv_) + _dot_nt(dvb, wvv)
        dwq_ = _dot_tn(cq, dq_pre)
        dwk_ = _dot_tn(ckv, dkb)
        dwv_ = _dot_tn(ckv, dvb)
        dcq, dgq = _rms_bwd(cqf, gq, dcq_n)
        dckv, dgkv = _rms_bwd(ckvf, gkv, dckv_n)
        return jnp.concatenate([dcq, dckv, dzk], axis=1), dwq_, dwk_, dwv_, dgq, dgkv

    dz_lat, dwq, dwk, dwv, dg_q_lat, dg_kv_lat = rowwise(
        lat_bwd, [(z, (512, 8)), (cm, None), (s1, None), (s2, None), (dq_a, None), (dk_a, None), (dv_a, None)],
        [g_q_lat, g_kv_lat, wq, wk, wv], [(512, BF16)],
        [((MLA_QR, MLA_H * HP), F32), ((MLA_KVR, MLA_H * HP), F32), ((MLA_KVR, MLA_H * HP), F32),
         ((1, MLA_QR), F32), ((1, MLA_KVR), F32)], tile=tile, name="lat_bwd")

    def dz_assemble(dq_, dk_, dv_, drg_, dgt, dzl, crv, srv):
        a = _rot_ret(dq_, crv, -srv)
        b = _rot_ret(dk_, crv, -srv) * (RET_D ** -0.5)
        return jnp.concatenate([a, b, dv_, drg_.astype(F32), dgt.astype(F32), dzl.astype(F32)], axis=1)

    dz = rowwise(dz_assemble, [(drq_r, None), (drk_r, None), (drv, None), (drg, None), (dgates, None), (dz_lat, None),
                               (cr, None), (sr, None)], [], [(ZW, BF16)], [], tile=tile, name="dz_assemble")[0]
    du = matmul(dz, wz, tb=True, name="mm_du", tm=512)
    dwz = matmul_tn(u, dz, name="mm_dw_z")

    grad_x, dg_mix = rowwise(norm_bwd, [(xs, None), (du, None), (dh1, None)], [g_mix], [(D, F32)], [((1, D), F32)],
                             tile=tile, name="norm_mix_bwd")

    se1, se2 = slots_early(dwz, dwk, dwv, dwq, tile=tile)
    small_grads = {"g_mix": dg_mix, "b_gate": db_gate, "g_q_lat": dg_q_lat, "g_kv_lat": dg_kv_lat, "g_ret": dg_ret,
                   "g_cross": dg_cross, "g_mem": dg_mem, "g_ffn": dg_ffn, "g_final": dg_final}
    re1, re2, rd, rs, dslots, own_s = exchange_late(se1, se2, cs_a, cs_b, [small_grads[n] for n, _, _ in SMALL_DIRECT])

    me = _my_id()

    def own(slots):
        return lax.dynamic_index_in_dim(slots, me, axis=0, keepdims=False)

    def wmv(names):
        return [shard(n, p) for n in names for p in ("", "m_", "v_")]

    names_s = tuple(n for n, _, _ in SMALL)
    groups = (
        (("w_in",), adam_cols(own(se1), re1, wmv(("w_in",)), ((0, NS_IN),), name="adam_e1", tile=128)),
        (("w_ukv", "w_uq"), adam_rows(own(se2), re2, wmv(("w_ukv", "w_uq")),
                                      ((0, E2_UQ, HP), (E2_UQ, E2_ROWS, NS_UQ)), name="adam_e2")),
        (("w_up", "w_xkv"), adam_cols(own(sl1), rl1, wmv(("w_up", "w_xkv")), ((0, NS_UP), (NS_UP, L1_W)),
                                      name="adam_l1", tile=128)),
        (("w_out", "w_xq", "w_xo", "w_down"),
         adam_rows(own(sl2), rl2, wmv(("w_out", "w_xq", "w_xo", "w_down")),
                   ((L2_OUT, L2_XQ, D), (L2_XQ, L2_XO, D), (L2_XO, L2_DN, D), (L2_DN, L2_ROWS, D)), name="adam_l2")),
        (("w_proj_mla", "w_proj_ret"), adam_rows(own(sl3), rl3, wmv(("w_proj_mla", "w_proj_ret")),
                                                 ((0, L3_PRET, HP), (L3_PRET, L3_ROWS, HP)), name="adam_l3")),
        (names_s + ("w_conv",), adam_small(own_s, rs, dslots, rd, wmv(names_s), wmv(("w_conv",)))),
    )
    res = {}
    for names, outs_ in groups:
        for t, n in enumerate(names):
            res[n] = outs_[4 * t:4 * t + 4]

    order = ["g_mix", "w_in", "b_gate", "g_q_lat", "w_uq", "g_kv_lat", "w_ukv", "w_proj_mla", "g_ret", "w_proj_ret",
             "w_out", "g_cross", "g_mem", "w_xq", "w_xkv", "w_xo", "g_ffn", "w_up", "w_conv", "b_conv", "w_down",
             "g_final"]
    outs = [loss, grad_x[None]]
    for kind in range(4):
        outs += [res[n][kind].reshape(args[n].shape) for n in order]
    return tuple(outs)
```

```python
import functools
import math

import jax
import jax.numpy as jnp
import numpy as np
from jax import lax
from jax.experimental import pallas as pl
from jax.experimental.pallas import tpu as pltpu

F32 = jnp.float32
BF16 = jnp.bfloat16

D = 1024
MLA_H, MLA_NOPE, MLA_ROPE, MLA_V = 8, 64, 32, 64
MLA_QR, MLA_KVR = 256, 128
RET_H, RET_D, RET_C = 4, 128, 128
X_H, X_HD = 4, 256
D_FF = 2816
THETA = 10000.0
EPS = 1e-6
HP = 128
ZW = 4608
N_DEV = 8

ADAM_LR, ADAM_B1, ADAM_B2, ADAM_EPS, ADAM_WD, ADAM_STEP = 0.001, 0.9, 0.999, 1e-08, 0.01, 10

VMEM_LIMIT = 56 * 1024 * 1024
MESH = pl.DeviceIdType.MESH
VM = pl.BlockSpec(memory_space=pltpu.VMEM)
ANY = pl.BlockSpec(memory_space=pl.ANY)


def _cp(n_axes):
    return pltpu.CompilerParams(dimension_semantics=("arbitrary",) * n_axes, vmem_limit_bytes=VMEM_LIMIT)


def _cp0():
    return pltpu.CompilerParams(vmem_limit_bytes=VMEM_LIMIT)


def _pick(n, cap, mult=128):
    best = None
    for t in range(mult, min(n, cap) + 1, mult):
        if n % t == 0:
            best = t
    return best if best is not None else n


def _dot(a, b):
    return jnp.dot(a, b, preferred_element_type=F32)


def _dot_nt(a, b):
    return lax.dot_general(a, b, (((1,), (1,)), ((), ())), preferred_element_type=F32)


def _dot_tn(a, b):
    return lax.dot_general(a, b, (((0,), (0,)), ((), ())), preferred_element_type=F32)


def _sds(shape, dtype):
    return jax.ShapeDtypeStruct(shape, dtype)


def matmul(a, b, *, name, tb=False, res=None, out_dtype=F32, tm=1024, tn=1024, exchange=()):
    M, K = a.shape
    N = b.shape[0] if tb else b.shape[1]
    tm = _pick(M, tm, 8)
    tn = _pick(N, tn)
    gi, gj = M // tm, N // tn
    ns = len(exchange)
    n_in = 2 if res is None else 3

    def body(*refs):
        a_ref, b_ref = refs[:2]
        o_ref = refs[n_in + ns]
        if ns:
            comm = ([(r, True) for r in refs[n_in:n_in + ns]], refs[n_in + ns + 1:n_in + 2 * ns + 1]) + tuple(
                refs[n_in + 2 * ns + 1:])
            i, j = pl.program_id(0), pl.program_id(1)

            @pl.when((i == 0) & (j == 0))
            def _():
                _exchange_start(*comm)

        av = a_ref[...].astype(BF16)
        bv = b_ref[...].astype(BF16)
        acc = _dot_nt(av, bv) if tb else _dot(av, bv)
        if res is not None:
            acc = acc + refs[2][...].astype(F32)
        o_ref[...] = acc.astype(o_ref.dtype)
        if ns:

            @pl.when((i == gi - 1) & (j == gj - 1))
            def _():
                _exchange_wait(*comm)

    in_specs = [
        pl.BlockSpec((tm, K), lambda i, j: (i, 0)),
        pl.BlockSpec((tn, K), lambda i, j: (j, 0)) if tb else pl.BlockSpec((K, tn), lambda i, j: (0, j)),
    ]
    args = [a, b]
    if res is not None:
        in_specs.append(pl.BlockSpec((tm, tn), lambda i, j: (i, j)))
        args.append(res)
    out_spec = pl.BlockSpec((tm, tn), lambda i, j: (i, j))
    out_shape = _sds((M, N), out_dtype)
    return pl.pallas_call(
        body,
        name=name,
        grid=(gi, gj),
        in_specs=in_specs + [ANY] * ns,
        out_specs=[out_spec] + [ANY] * ns if ns else out_spec,
        out_shape=[out_shape] + [_sds((N_DEV - 1,) + s.shape[1:], s.dtype) for s in exchange] if ns else out_shape,
        scratch_shapes=_exchange_scratch(ns) if ns else [],
        compiler_params=_cp(2),
    )(*args, *exchange)


def matmul2_tb(a1, a2, b, *, name, tm=512, tn=512):
    M, K1 = a1.shape
    N = b.shape[0]
    tm = _pick(M, tm, 8)
    tn = _pick(N, tn)

    def body(a1_ref, a2_ref, b1_ref, b2_ref, o_ref):
        o_ref[...] = (_dot_nt(a1_ref[...].astype(BF16), b1_ref[...].astype(BF16))
                      + _dot_nt(a2_ref[...].astype(BF16), b2_ref[...].astype(BF16)))

    return pl.pallas_call(
        body,
        name=name,
        grid=(M // tm, N // tn),
        in_specs=[pl.BlockSpec((tm, K1), lambda i, j: (i, 0)), pl.BlockSpec((tm, K1), lambda i, j: (i, 0)),
                  pl.BlockSpec((tn, K1), lambda i, j: (j, 0)), pl.BlockSpec((tn, K1), lambda i, j: (j, 1))],
        out_specs=pl.BlockSpec((tm, tn), lambda i, j: (i, j)),
        out_shape=_sds((M, N), F32),
        compiler_params=_cp(2),
    )(a1, a2, b, b)


def matmul_tn(a, b, *, name, tm=1024, tn=1024, tk=1024):
    R, M = a.shape
    N = b.shape[1]
    tm = _pick(M, tm)
    tn = _pick(N, tn)
    tk = _pick(R, tk, 16)
    nk = R // tk

    def body(a_ref, b_ref, o_ref, acc_ref):
        k = pl.program_id(2)

        @pl.when(k == 0)
        def _():
            acc_ref[...] = jnp.zeros_like(acc_ref)

        acc_ref[...] += _dot_tn(a_ref[...].astype(BF16), b_ref[...].astype(BF16))

        @pl.when(k == nk - 1)
        def _():
            o_ref[...] = acc_ref[...]

    return pl.pallas_call(
        body,
        name=name,
        grid=(M // tm, N // tn, nk),
        in_specs=[pl.BlockSpec((tk, tm), lambda i, j, k: (k, i)), pl.BlockSpec((tk, tn), lambda i, j, k: (k, j))],
        out_specs=pl.BlockSpec((tm, tn), lambda i, j, k: (i, j)),
        out_shape=_sds((M, N), F32),
        scratch_shapes=[pltpu.VMEM((tm, tn), F32)],
        compiler_params=_cp(3),
    )(a, b)


def rowwise(fn, rows, consts, out_rows, out_accs, *, tile, name):
    T = rows[0][0].shape[0]
    nt = T // tile
    n_r, n_c, n_o, n_a = len(rows), len(consts), len(out_rows), len(out_accs)

    def body(*refs):
        ins = [r[...] for r in refs[: n_r + n_c]]
        outs = fn(*ins)
        if not isinstance(outs, (tuple, list)):
            outs = (outs,)
        o_refs = refs[n_r + n_c : n_r + n_c + n_o]
        a_refs = refs[n_r + n_c + n_o :]
        for o_ref, o in zip(o_refs, outs[:n_o]):
            o_ref[...] = o.astype(o_ref.dtype)
        if n_a:
            first = pl.program_id(0) == 0

            @pl.when(first)
            def _():
                for a_ref, o in zip(a_refs, outs[n_o:]):
                    a_ref[...] = o.astype(a_ref.dtype)

            @pl.when(jnp.logical_not(first))
            def _():
                for a_ref, o in zip(a_refs, outs[n_o:]):
                    a_ref[...] += o.astype(a_ref.dtype)

    in_specs = []
    args = []
    for arr, win in rows:
        if win is None:
            in_specs.append(pl.BlockSpec((tile, arr.shape[1]), lambda i: (i, 0)))
        else:
            w, cb = win
            in_specs.append(pl.BlockSpec((tile, w), functools.partial(lambda i, cb: (i, cb), cb=cb)))
        args.append(arr)
    for c in consts:
        in_specs.append(pl.BlockSpec(c.shape, functools.partial(lambda i, nd: (0,) * nd, nd=c.ndim)))
        args.append(c)
    out_specs = [pl.BlockSpec((tile, w), lambda i: (i, 0)) for w, _ in out_rows]
    out_shape = [_sds((T, w), dt) for w, dt in out_rows]
    for shp, dt in out_accs:
        out_specs.append(pl.BlockSpec(shp, functools.partial(lambda i, nd: (0,) * nd, nd=len(shp))))
        out_shape.append(_sds(shp, dt))
    return pl.pallas_call(
        body,
        name=name,
        grid=(nt,),
        in_specs=in_specs,
        out_specs=out_specs,
        out_shape=out_shape,
        compiler_params=_cp(1),
    )(*args)


def _rms(x, g):
    r = lax.rsqrt(jnp.mean(x * x, axis=-1, keepdims=True) + EPS)
    return x * r * g


def _rms_bwd(x, g, dy):
    r = lax.rsqrt(jnp.mean(x * x, axis=-1, keepdims=True) + EPS)
    xh = x * r
    dg = jnp.sum(dy * xh, axis=0, keepdims=True)
    dxh = dy * g
    dx = r * (dxh - xh * jnp.mean(dxh * xh, axis=-1, keepdims=True))
    return dx, dg


def _sigmoid(x):
    return 0.5 * jnp.tanh(0.5 * x) + 0.5


def _rot_mla(x, c, s1, s2):
    n = x.shape[1] // HP
    outs = []
    for h in range(n):
        xh = x[:, h * HP : (h + 1) * HP]
        outs.append(xh * c + pltpu.roll(xh, HP - 16, 1) * s1 + pltpu.roll(xh, 16, 1) * s2)
    return outs[0] if n == 1 else jnp.concatenate(outs, axis=1)


def _rot_ret(x, c, s):
    n = x.shape[1] // RET_D
    outs = []
    for h in range(n):
        xh = x[:, h * RET_D : (h + 1) * RET_D]
        outs.append(xh * c + pltpu.roll(xh, RET_D // 2, 1) * s)
    return outs[0] if n == 1 else jnp.concatenate(outs, axis=1)


def rot_tables(pos_f, *, tile):
    lane_np = np.arange(128)
    inv_m = (jnp.asarray(THETA, F32) ** (-jnp.asarray(lane_np & 15, F32) / 16.0)).reshape(1, 128)
    inv_r = (jnp.asarray(THETA, F32) ** (-jnp.asarray(lane_np & 63, F32) / 64.0)).reshape(1, 128)

    def fn(p, im, ir):
        lane = lax.broadcasted_iota(jnp.int32, p.shape, 1)
        ang = p * im
        cm = jnp.where((lane >= 64) & (lane < 96), jnp.cos(ang), 1.0)
        sn = jnp.sin(ang)
        s1 = jnp.where((lane >= 64) & (lane < 80), -sn, 0.0)
        s2 = jnp.where((lane >= 80) & (lane < 96), sn, 0.0)
        angr = p * ir
        snr = jnp.sin(angr)
        return cm, s1, s2, jnp.cos(angr), jnp.where(lane < 64, -snr, snr)

    return rowwise(fn, [(pos_f, None)], [inv_m, inv_r], [(128, F32)] * 5, [], tile=tile, name="rot_tables")


def _peer(m):
    x, y, c = lax.axis_index("x"), lax.axis_index("y"), lax.axis_index("c")
    mx, my, mc = (m >> 2) & 1, (m >> 1) & 1, m & 1
    px = 1 - x if mx else x
    py = 1 - y if my else y
    pc = 1 - c if mc else c
    return (px, py, pc), 4 * px + 2 * py + pc


def _my_id():
    return 4 * lax.axis_index("x") + 2 * lax.axis_index("y") + lax.axis_index("c")


def _gather_copies(srcs, outs, send_sems, recv_sems, local_sems, arriving=False):
    me = _my_id()
    copies = []
    if not arriving:
        for g, (src, out) in enumerate(zip(srcs, outs)):
            copies.append((pltpu.make_async_copy(src, out.at[me], local_sems.at[g]), False))
    for m in range(1, N_DEV):
        peer, plin = _peer(m)
        for g, (src, out) in enumerate(zip(srcs, outs)):
            copies.append((pltpu.make_async_remote_copy(
                src_ref=src, dst_ref=out.at[plin if arriving else me], send_sem=send_sems.at[g, m - 1],
                recv_sem=recv_sems.at[g, m - 1], device_id=peer, device_id_type=MESH), True))
    return copies


def _gather_start(*a):
    for cp, _ in _gather_copies(*a):
        cp.start()


def _gather_wait(*a):
    for cp, _ in _gather_copies(*a, arriving=True):
        cp.wait_recv()
    for cp, remote in _gather_copies(*a):
        if remote:
            cp.wait_send()
        else:
            cp.wait()


def _gather_scratch(n):
    return [pltpu.SemaphoreType.DMA((n, N_DEV - 1)), pltpu.SemaphoreType.DMA((n, N_DEV - 1)), pltpu.SemaphoreType.DMA((n,))]


def _exchange_copies(srcs, dsts, send_sems, recv_sems):
    copies = []
    for m in range(1, N_DEV):
        peer, plin = _peer(m)
        for g, ((src, per_peer), dst) in enumerate(zip(srcs, dsts)):
            copies.append(pltpu.make_async_remote_copy(
                src_ref=src.at[plin] if per_peer else src, dst_ref=dst.at[m - 1], send_sem=send_sems.at[g, m - 1],
                recv_sem=recv_sems.at[g, m - 1], device_id=peer, device_id_type=MESH))
    return copies


def _exchange_start(*a):
    for cp in _exchange_copies(*a):
        cp.start()


def _exchange_wait(*a):
    copies = _exchange_copies(*a)
    for cp in copies:
        cp.wait_recv()
    for cp in copies:
        cp.wait_send()


def _exchange_scratch(n):
    return [pltpu.SemaphoreType.DMA((n, N_DEV - 1)), pltpu.SemaphoreType.DMA((n, N_DEV - 1))]


def all_gather_groups(parts, *, name):
    n = len(parts)

    def body(*refs):
        a = (refs[:n], refs[n:2 * n]) + tuple(refs[2 * n:])
        _gather_start(*a)
        _gather_wait(*a)

    return pl.pallas_call(
        body,
        name=name,
        in_specs=[ANY] * n,
        out_specs=[ANY] * n,
        out_shape=[_sds((N_DEV,) + p.shape, p.dtype) for p in parts],
        scratch_shapes=_gather_scratch(n),
    )(*parts)


MLA_SCALE = (MLA_NOPE + MLA_ROPE) ** -0.5
MLA_C2 = MLA_SCALE * math.log2(math.e)
ONE_LANE = MLA_V
NEG = -1e30


def _tri_mask(n, lower_rows_ge_cols=True):
    r = lax.broadcasted_iota(jnp.int32, (n, n), 0)
    c = lax.broadcasted_iota(jnp.int32, (n, n), 1)
    return r >= c if lower_rows_ge_cols else c >= r


def mla_fwd(q, k, v, gather, *, tq):
    T = q.shape[0]
    nq = T // tq
    rep = tq // HP
    ng = len(gather)

    def body(*refs):
        q_ref, k_ref, v_ref = refs[:3]
        srcs = refs[3:3 + ng]
        o_ref, lse_ref = refs[3 + ng:5 + ng]
        outs = refs[5 + ng:5 + 2 * ng]
        m_sc, acc_sc = refs[5 + 2 * ng:7 + 2 * ng]
        comm = (srcs, outs) + tuple(refs[7 + 2 * ng:])
        h, i = pl.program_id(0), pl.program_id(1)

        @pl.when((h == 0) & (i == 0))
        def _():
            _gather_start(*comm)

        qv = q_ref[...]
        m_sc[...] = jnp.full(m_sc.shape, NEG, F32)
        acc_sc[...] = jnp.zeros(acc_sc.shape, F32)

        def block(j, masked):
            off = pl.multiple_of(j * tq, tq)
            kb = k_ref[pl.ds(off, tq), :]
            vb = v_ref[pl.ds(off, tq), :]
            s = _dot_nt(qv, kb) * MLA_C2
            if masked:
                s = jnp.where(_tri_mask(tq), s, NEG)
            m_prev = m_sc[...]
            m_next = jnp.maximum(m_prev, jnp.max(s, axis=-1, keepdims=True))
            p = jnp.exp2(s - jnp.tile(m_next, (1, rep)))
            alpha = jnp.exp2(m_prev - m_next)
            acc_sc[...] = alpha * acc_sc[...] + _dot(p.astype(BF16), vb)
            m_sc[...] = m_next

        def loop_body(j, carry):
            block(j, False)
            return carry

        lax.fori_loop(0, i, loop_body, 0)
        block(i, True)
        acc = acc_sc[...]
        l = acc[:, ONE_LANE:ONE_LANE + 1]
        o_ref[...] = (acc / l).astype(o_ref.dtype)
        lse_ref[...] = m_sc[...] + jnp.log(l) * math.log2(math.e)

        @pl.when((h == MLA_H - 1) & (i == nq - 1))
        def _():
            _gather_wait(*comm)

    blk = pl.BlockSpec((tq, HP), lambda h, i: (i, h))
    full = pl.BlockSpec((T, HP), lambda h, i: (0, h))
    return pl.pallas_call(
        body,
        name="mla_fwd",
        grid=(MLA_H, nq),
        in_specs=[blk, full, full] + [ANY] * ng,
        out_specs=[blk, blk] + [ANY] * ng,
        out_shape=[_sds((T, MLA_H * HP), BF16), _sds((T, MLA_H * HP), F32)]
        + [_sds((N_DEV,) + p.shape, p.dtype) for p in gather],
        scratch_shapes=[pltpu.VMEM((tq, HP), F32), pltpu.VMEM((tq, HP), F32)] + _gather_scratch(ng),
        compiler_params=_cp(2),
    )(q, k, v, *gather)


def mla_prep(o, do, lse, *, tq):
    T = o.shape[0]
    nq = T // tq

    def body(o_ref, do_ref, lse_ref, lse_row_ref, delta_row_ref):
        for h in range(MLA_H):
            sl = slice(h * HP, (h + 1) * HP)
            d = jnp.sum(o_ref[:, sl].astype(F32) * do_ref[:, sl].astype(F32), axis=-1, keepdims=True)
            lse_row_ref[h] = lse_ref[:, sl].T[0:1, :]
            delta_row_ref[h] = jnp.broadcast_to(d, (tq, HP)).T[0:1, :]

    blk = pl.BlockSpec((tq, MLA_H * HP), lambda i: (i, 0))
    row = pl.BlockSpec((MLA_H, None, 1, tq), lambda i: (0, i, 0, 0))
    return pl.pallas_call(
        body,
        name="mla_prep",
        grid=(nq,),
        in_specs=[blk, blk, blk],
        out_specs=[row, row],
        out_shape=[_sds((MLA_H, nq, 1, tq), F32), _sds((MLA_H, nq, 1, tq), F32)],
        compiler_params=_cp(1),
    )(o, do, lse)


def mla_bwd(q, k, v, do, lse_row, delta_row, slots, *, tq):
    T = q.shape[0]
    nq = T // tq
    ns = len(slots)

    def body(*refs):
        q_ref, k_ref, v_ref, do_ref, lse_ref, delta_ref = refs[:6]
        srcs = [(r, True) for r in refs[6:6 + ns]]
        dq_ref, dk_ref, dv_ref = refs[6 + ns:9 + ns]
        dsts = refs[9 + ns:9 + 2 * ns]
        dk_sc, dv_sc = refs[9 + 2 * ns:11 + 2 * ns]
        comm = (srcs, dsts) + tuple(refs[11 + 2 * ns:])
        h, j = pl.program_id(0), pl.program_id(1)

        @pl.when((h == 0) & (j == 0))
        def _():
            _exchange_start(*comm)

        kv = k_ref[...]
        vv = v_ref[...]
        dk_sc[...] = jnp.zeros(dk_sc.shape, F32)
        dv_sc[...] = jnp.zeros(dv_sc.shape, F32)

        @pl.when(j == 0)
        def _():
            dq_ref[...] = jnp.zeros(dq_ref.shape, F32)

        def block(i, masked):
            off = pl.multiple_of(i * tq, tq)
            qb = q_ref[pl.ds(off, tq), :]
            dob = do_ref[pl.ds(off, tq), :]
            st = _dot_nt(kv, qb) * MLA_C2
            if masked:
                st = jnp.where(_tri_mask(tq, False), st, NEG)
            pt = jnp.exp2(st - lse_ref[i])
            dv_sc[...] += _dot(pt.astype(BF16), dob)
            dpt = _dot_nt(vv, dob)
            dst = (pt * (dpt - delta_ref[i]) * MLA_SCALE).astype(BF16)
            dk_sc[...] += _dot(dst, qb)
            dq_ref[pl.ds(off, tq), :] += _dot_tn(dst, kv)

        block(j, True)

        def loop_body(i, carry):
            block(i, False)
            return carry

        lax.fori_loop(j + 1, nq, loop_body, 0)
        dk_ref[...] = dk_sc[...].astype(dk_ref.dtype)
        dv_ref[...] = dv_sc[...].astype(dv_ref.dtype)

        @pl.when((h == MLA_H - 1) & (j == nq - 1))
        def _():
            _exchange_wait(*comm)

    blk = pl.BlockSpec((tq, HP), lambda h, j: (j, h))
    full = pl.BlockSpec((T, HP), lambda h, j: (0, h))
    rows = pl.BlockSpec((None, nq, 1, tq), lambda h, j: (h, 0, 0, 0))
    return pl.pallas_call(
        body,
        name="mla_bwd",
        grid=(MLA_H, nq),
        in_specs=[full, blk, blk, full, rows, rows] + [ANY] * ns,
        out_specs=[full, blk, blk] + [ANY] * ns,
        out_shape=[_sds((T, MLA_H * HP), F32), _sds((T, MLA_H * HP), BF16), _sds((T, MLA_H * HP), BF16)]
        + [_sds((N_DEV - 1,) + s.shape[1:], s.dtype) for s in slots],
        scratch_shapes=[pltpu.VMEM((tq, HP), F32), pltpu.VMEM((tq, HP), F32)] + _exchange_scratch(ns),
        compiler_params=_cp(2),
    )(q, k, v, do, lse_row, delta_row, *slots)


def _ret_consts():
    h = jnp.arange(RET_H, dtype=F32)
    log_g = jnp.log1p(-jnp.exp2(-5.0 - h))
    idx = jnp.arange(RET_C, dtype=F32)
    rel = idx[:, None] - idx[None, :]
    dmask = jnp.where(rel >= 0, jnp.exp(log_g[:, None, None] * jnp.maximum(rel, 0.0)), 0.0)
    zeta = jnp.exp(log_g[:, None] * (RET_C - 1.0 - idx)[None, :])
    xi = jnp.exp(log_g[:, None] * (idx + 1.0)[None, :])
    decay = jnp.exp(log_g * RET_C)
    zb = jnp.broadcast_to(zeta[:, :, None], (RET_H, RET_C, RET_D))
    xb = jnp.broadcast_to(xi[:, :, None], (RET_H, RET_C, RET_D))
    db = jnp.broadcast_to(decay[:, None, None], (RET_H, RET_C, RET_D))
    return dmask.astype(F32), zb.astype(F32), xb.astype(F32), db.astype(F32)


def ret_fwd(rq, rk, rv, consts, *, rb):
    T = rq.shape[0]
    nb = T // rb
    ncb = rb // RET_C

    def body(q_ref, k_ref, v_ref, dm_ref, z_ref, x_ref, dc_ref, o_ref, st_ref, r_sc):
        @pl.when(pl.program_id(1) == 0)
        def _():
            r_sc[...] = jnp.zeros(r_sc.shape, F32)

        dm, zt, xi, dc = dm_ref[...], z_ref[...], x_ref[...], dc_ref[...]
        for c in range(ncb):
            sl = slice(c * RET_C, (c + 1) * RET_C)
            q, k, v = q_ref[sl, :], k_ref[sl, :], v_ref[sl, :]
            r = r_sc[...]
            rbf = r.astype(BF16)
            st_ref[sl, :] = rbf
            s = _dot_nt(q, k) * dm
            inner = _dot(s.astype(BF16), v)
            cross = _dot((q.astype(F32) * xi).astype(BF16), rbf)
            o_ref[sl, :] = inner + cross
            kz = (k.astype(F32) * zt).T.astype(BF16)
            r_sc[...] = r * dc + _dot(kz, v)

    blk = pl.BlockSpec((rb, RET_D), lambda h, b: (b, h))
    cst = pl.BlockSpec((None, RET_C, RET_D), lambda h, b: (h, 0, 0))
    return pl.pallas_call(
        body,
        name="ret_fwd",
        grid=(RET_H, nb),
        in_specs=[blk, blk, blk, cst, cst, cst, cst],
        out_specs=[blk, blk],
        out_shape=[_sds((T, RET_H * RET_D), F32), _sds((T, RET_H * RET_D), BF16)],
        scratch_shapes=[pltpu.VMEM((RET_D, RET_D), F32)],
        compiler_params=_cp(2),
    )(rq, rk, rv, *consts)


def ret_bwd(rq, rk, rv, st, dret, consts, *, rb):
    T = rq.shape[0]
    nb = T // rb
    ncb = rb // RET_C

    def body(q_ref, k_ref, v_ref, st_ref, do_ref, dm_ref, z_ref, x_ref, dc_ref, dq_ref, dk_ref, dv_ref, g_sc):
        @pl.when(pl.program_id(1) == 0)
        def _():
            g_sc[...] = jnp.zeros(g_sc.shape, F32)

        dm, zt, xi, dc = dm_ref[...], z_ref[...], x_ref[...], dc_ref[...]
        for c in reversed(range(ncb)):
            sl = slice(c * RET_C, (c + 1) * RET_C)
            q, k, v, rp = q_ref[sl, :], k_ref[sl, :], v_ref[sl, :], st_ref[sl, :]
            dob = do_ref[sl, :].astype(BF16)
            qf, kf = q.astype(F32), k.astype(F32)
            gn = g_sc[...]
            gnb = gn.astype(BF16)
            s = _dot_nt(q, k) * dm
            ds = _dot_nt(dob, v) * dm
            dq = _dot(ds.astype(BF16), k) + _dot_nt(dob, rp) * xi
            dk = _dot(ds.T.astype(BF16), q) + _dot_nt(v, gnb) * zt
            dv = _dot(s.T.astype(BF16), dob) + _dot((kf * zt).astype(BF16), gnb)
            dq_ref[sl, :] = dq.astype(dq_ref.dtype)
            dk_ref[sl, :] = dk.astype(dk_ref.dtype)
            dv_ref[sl, :] = dv.astype(dv_ref.dtype)
            g_sc[...] = _dot((qf * xi).T.astype(BF16), dob) + dc * gn

    blk = pl.BlockSpec((rb, RET_D), lambda h, b: (nb - 1 - b, h))
    cst = pl.BlockSpec((None, RET_C, RET_D), lambda h, b: (h, 0, 0))
    return pl.pallas_call(
        body,
        name="ret_bwd",
        grid=(RET_H, nb),
        in_specs=[blk, blk, blk, blk, blk, cst, cst, cst, cst],
        out_specs=[blk, blk, blk],
        out_shape=[_sds((T, RET_H * RET_D), F32)] * 3,
        scratch_shapes=[pltpu.VMEM((RET_D, RET_D), F32)],
        compiler_params=_cp(2),
    )(rq, rk, rv, st, dret, *consts)


HALO = 16


def conv_act_fwd(up_pre, w_conv, b_conv, *, tile, cw):
    T = up_pre.shape[0]
    nt = T // tile
    ncol = D_FF // cw
    hb = tile // HALO

    def body(pa_ref, a_ref, pb_ref, b_ref, wa_ref, wb_ref, ba_ref, bb_ref, o_ref):
        i = pl.program_id(1)
        keep = (i > 0).astype(F32)

        def conv(prev_ref, cur_ref, w_ref, bias_ref):
            ext = jnp.concatenate([prev_ref[...].astype(F32) * keep, cur_ref[...].astype(F32)], axis=0)
            w = w_ref[...]
            y = ext * w[2:3, :] + pltpu.roll(ext, 1, 0) * w[1:2, :] + pltpu.roll(ext, 2, 0) * w[0:1, :] + bias_ref[...]
            return y[HALO:, :]

        a = conv(pa_ref, a_ref, wa_ref, ba_ref)
        b = conv(pb_ref, b_ref, wb_ref, bb_ref)
        o_ref[...] = (a * _sigmoid(a) * b).astype(o_ref.dtype)

    prev_a = pl.BlockSpec((HALO, cw), lambda j, i: (jnp.maximum(i * hb - 1, 0), j))
    cur_a = pl.BlockSpec((tile, cw), lambda j, i: (i, j))
    prev_b = pl.BlockSpec((HALO, cw), lambda j, i: (jnp.maximum(i * hb - 1, 0), j + ncol))
    cur_b = pl.BlockSpec((tile, cw), lambda j, i: (i, j + ncol))
    w_a = pl.BlockSpec((3, cw), lambda j, i: (0, j))
    w_b = pl.BlockSpec((3, cw), lambda j, i: (0, j + ncol))
    bias_a = pl.BlockSpec((1, cw), lambda j, i: (0, j))
    bias_b = pl.BlockSpec((1, cw), lambda j, i: (0, j + ncol))
    return pl.pallas_call(
        body,
        name="conv_act_fwd",
        grid=(ncol, nt),
        in_specs=[prev_a, cur_a, prev_b, cur_b, w_a, w_b, bias_a, bias_b],
        out_specs=pl.BlockSpec((tile, cw), lambda j, i: (i, j)),
        out_shape=_sds((T, D_FF), BF16),
        compiler_params=_cp(2),
    )(up_pre, up_pre, up_pre, up_pre, w_conv, w_conv, b_conv, b_conv)


def conv_act_bwd(up_pre, dact, w_conv, b_conv, *, tile, cw):
    T = up_pre.shape[0]
    nt = T // tile
    ncol = D_FF // cw
    hb = tile // HALO
    ext_rows = tile + 2 * HALO

    def body(pa_ref, a_ref, na_ref, pb_ref, b_ref, nb_ref, d_ref, nd_ref, wa_ref, wb_ref, ba_ref, bb_ref,
             dxa_ref, dxb_ref, sa_ref, sb_ref):
        i = pl.program_id(1)
        keep_p = (i > 0).astype(F32)
        keep_n = (i < nt - 1).astype(F32)

        def ext_of(prev_ref, cur_ref, next_ref):
            return jnp.concatenate(
                [prev_ref[...].astype(F32) * keep_p, cur_ref[...].astype(F32), next_ref[...].astype(F32) * keep_n], axis=0)

        def conv(ext, w, bias):
            return ext * w[2:3, :] + pltpu.roll(ext, 1, 0) * w[1:2, :] + pltpu.roll(ext, 2, 0) * w[0:1, :] + bias

        xa = ext_of(pa_ref, a_ref, na_ref)
        xb = ext_of(pb_ref, b_ref, nb_ref)
        wa, wb = wa_ref[...], wb_ref[...]
        a = conv(xa, wa, ba_ref[...])
        b = conv(xb, wb, bb_ref[...])
        dy = jnp.concatenate(
            [jnp.zeros((HALO, cw), F32), d_ref[...].astype(F32), nd_ref[...].astype(F32) * keep_n], axis=0)
        sg = _sigmoid(a)
        da = dy * b * (sg * (1.0 + a * (1.0 - sg)))
        db = dy * (a * sg)

        def back(dup, ext, w, dx_ref, s_ref):
            dx = dup * w[2:3, :] + pltpu.roll(dup, ext_rows - 1, 0) * w[1:2, :] + pltpu.roll(dup, ext_rows - 2, 0) * w[0:1, :]
            dx_ref[...] = dx[HALO:HALO + tile, :].astype(dx_ref.dtype)
            dc = dup[HALO:HALO + tile, :]
            r2 = jnp.sum(dc * ext[HALO:HALO + tile, :], axis=0, keepdims=True)
            r1 = jnp.sum(dc * pltpu.roll(ext, 1, 0)[HALO:HALO + tile, :], axis=0, keepdims=True)
            r0 = jnp.sum(dc * pltpu.roll(ext, 2, 0)[HALO:HALO + tile, :], axis=0, keepdims=True)
            rb = jnp.sum(dc, axis=0, keepdims=True)
            row = lax.broadcasted_iota(jnp.int32, (8, cw), 0)
            upd = (jnp.where(row == 0, r0, 0.0) + jnp.where(row == 1, r1, 0.0) + jnp.where(row == 2, r2, 0.0)
                   + jnp.where(row == 3, rb, 0.0))

            @pl.when(i == 0)
            def _():
                s_ref[...] = upd

            @pl.when(i > 0)
            def _():
                s_ref[...] += upd

        back(da, xa, wa, dxa_ref, sa_ref)
        back(db, xb, wb, dxb_ref, sb_ref)

    def prev_of(shift):
        return pl.BlockSpec((HALO, cw), lambda j, i: (jnp.maximum(i * hb - 1, 0), j + shift))

    def next_of(shift):
        return pl.BlockSpec((HALO, cw), lambda j, i: (jnp.minimum((i + 1) * hb, nt * hb - 1), j + shift))

    def cur_of(shift):
        return pl.BlockSpec((tile, cw), lambda j, i: (i, j + shift))

    def row_of(rows, shift):
        return pl.BlockSpec((rows, cw), lambda j, i: (0, j + shift))

    return pl.pallas_call(
        body,
        name="conv_act_bwd",
        grid=(ncol, nt),
        in_specs=[prev_of(0), cur_of(0), next_of(0), prev_of(ncol), cur_of(ncol), next_of(ncol), cur_of(0), next_of(0),
                  row_of(3, 0), row_of(3, ncol), row_of(1, 0), row_of(1, ncol)],
        out_specs=[cur_of(0), cur_of(0), row_of(8, 0), row_of(8, 0)],
        out_shape=[_sds((T, D_FF), BF16), _sds((T, D_FF), BF16), _sds((8, D_FF), F32), _sds((8, D_FF), F32)],
        compiler_params=_cp(2),
    )(up_pre, up_pre, up_pre, up_pre, up_pre, up_pre, dact, dact, w_conv, w_conv, b_conv, b_conv)


NS_IN, NS_UP, NS_XKV, NS_UQ = 564, 704, 256, 96
E2_ROWS, E2_UQ = 384, 128
L1_W = NS_UP + NS_XKV
L2_ROWS = 736
L2_OUT, L2_XQ, L2_XO, L2_DN = 0, 128, 256, 384
RS_DN = 352
L3_ROWS, L3_PRET = 1024, 512
L4_SHAPE = (8, 768)
WZ_RUNS = ((416, 4096, 0), (0, 384, 4096), (384, 32, 4544))
WZ_ZERO = ((4480, 4544), (4576, 4608))


def _pieces(orig_start, length, dst_start, ns):
    out, c, d, end = [], orig_start, dst_start, orig_start + length
    while c < end:
        j, off = c // ns, c % ns
        ln = min(ns - off, end - c)
        out.append((j, off, d, ln))
        c += ln
        d += ln
    return out


def pack_early(w_in, w_ukv, w_uq):
    def body(in_ref, ukv_ref, uq_ref, e1_ref, e2_ref):
        e1_ref[...] = in_ref[...].astype(BF16)
        e2_ref[0:E2_UQ, :] = ukv_ref[...].astype(BF16)
        e2_ref[E2_UQ:E2_ROWS, 0:NS_UQ] = uq_ref[...].astype(BF16)
        e2_ref[E2_UQ:E2_ROWS, NS_UQ:HP] = jnp.zeros((E2_ROWS - E2_UQ, HP - NS_UQ), BF16)

    return pl.pallas_call(
        body,
        name="pack_early",
        in_specs=[VM] * 3,
        out_specs=[VM] * 2,
        out_shape=[_sds((D, NS_IN), BF16), _sds((E2_ROWS, HP), BF16)],
        compiler_params=_cp0(),
    )(w_in, w_ukv, w_uq)


def pack_late(w_up, w_xkv, w_out, w_xq, w_xo, w_down, w_pmla, w_pret, w_conv):
    def body(up_ref, xkv_ref, o_ref, xq_ref, xo_ref, dn_ref, pm_ref, pr_ref, cv_ref, l1_ref, l2_ref, l3_ref, l4_ref):
        l1_ref[:, 0:NS_UP] = up_ref[...].astype(BF16)
        l1_ref[:, NS_UP:L1_W] = xkv_ref[...].astype(BF16)
        l2_ref[L2_OUT:L2_XQ, :] = o_ref[...].astype(BF16)
        l2_ref[L2_XQ:L2_XO, :] = xq_ref[...].astype(BF16)
        l2_ref[L2_XO:L2_DN, :] = xo_ref[...].astype(BF16)
        l2_ref[L2_DN:L2_ROWS, :] = dn_ref[...].astype(BF16)
        l3_ref[0:L3_PRET, :] = pm_ref[...].astype(BF16)
        l3_ref[L3_PRET:L3_ROWS, :] = pr_ref[...].astype(BF16)
        l4_ref[...] = jnp.zeros(L4_SHAPE, F32)
        l4_ref[0:3, 0:NS_UP] = cv_ref[...]

    return pl.pallas_call(
        body,
        name="pack_late",
        in_specs=[VM] * 9,
        out_specs=[VM] * 4,
        out_shape=[_sds((D, L1_W), BF16), _sds((L2_ROWS, D), BF16), _sds((L3_ROWS, HP), BF16), _sds(L4_SHAPE, F32)],
        compiler_params=_cp0(),
    )(w_up, w_xkv, w_out, w_xq, w_xo, w_down, w_pmla, w_pret, w_conv)


def assemble_early(g1, g2, *, tile):
    def body(g1_ref, g2_ref, wz_ref, wk_ref, wv_ref, wq_ref):
        for lo, hi in WZ_ZERO:
            wz_ref[:, lo:hi] = jnp.zeros((tile, hi - lo), BF16)
        for os_, ln_, ds_ in WZ_RUNS:
            for j, off, d, ln in _pieces(os_, ln_, ds_, NS_IN):
                wz_ref[:, d:d + ln] = g1_ref[j, :, off:off + ln]

        @pl.when(pl.program_id(0) == 0)
        def _():
            half = jnp.zeros((MLA_KVR, HP - MLA_NOPE), BF16)
            for j in range(N_DEV):
                wk_ref[:, j * HP:j * HP + MLA_NOPE] = g2_ref[j, 0:E2_UQ, 0:MLA_NOPE]
                wk_ref[:, j * HP + MLA_NOPE:(j + 1) * HP] = half
                wv_ref[:, j * HP:j * HP + MLA_V] = g2_ref[j, 0:E2_UQ, MLA_NOPE:HP]
                wv_ref[:, j * HP + MLA_V:(j + 1) * HP] = half
                wq_ref[:, j * HP:j * HP + NS_UQ] = g2_ref[j, E2_UQ:E2_ROWS, 0:NS_UQ]
                wq_ref[:, j * HP + NS_UQ:(j + 1) * HP] = jnp.zeros((MLA_QR, HP - NS_UQ), BF16)

    def whole(r):
        return pl.BlockSpec((r, MLA_H * HP), lambda i: (0, 0))

    return pl.pallas_call(
        body,
        name="assemble_early",
        grid=(D // tile,),
        in_specs=[pl.BlockSpec((N_DEV, tile, NS_IN), lambda i: (0, i, 0)),
                  pl.BlockSpec((N_DEV, E2_ROWS, HP), lambda i: (0, 0, 0))],
        out_specs=[pl.BlockSpec((tile, ZW), lambda i: (i, 0)), whole(MLA_KVR), whole(MLA_KVR), whole(MLA_QR)],
        out_shape=[_sds((D, ZW), BF16), _sds((MLA_KVR, MLA_H * HP), BF16), _sds((MLA_KVR, MLA_H * HP), BF16),
                   _sds((MLA_QR, MLA_H * HP), BF16)],
        compiler_params=_cp(1),
    )(g1, g2)


def assemble_l1(g1, *, tile):
    def body(g_ref, wup_ref, wxkv_ref):
        for j in range(N_DEV):
            wup_ref[:, j * NS_UP:(j + 1) * NS_UP] = g_ref[j, :, 0:NS_UP]
            wxkv_ref[:, j * NS_XKV:(j + 1) * NS_XKV] = g_ref[j, :, NS_UP:L1_W]

    return pl.pallas_call(
        body,
        name="assemble_l1",
        grid=(D // tile,),
        in_specs=[pl.BlockSpec((N_DEV, tile, L1_W), lambda i: (0, i, 0))],
        out_specs=[pl.BlockSpec((tile, 2 * D_FF), lambda i: (i, 0)), pl.BlockSpec((tile, 2 * D), lambda i: (i, 0))],
        out_shape=[_sds((D, 2 * D_FF), BF16), _sds((D, 2 * D), BF16)],
        compiler_params=_cp(1),
    )(g1)


def assemble_l234(g2, g3, g4):
    def body(g2_ref, g3_ref, g4_ref, wo_ref, wxq_ref, wxo_ref, wdn_ref, wpa_ref, wpr_ref, wc_ref):
        for j in range(N_DEV):
            wo_ref[j * 128:(j + 1) * 128, :] = g2_ref[j, L2_OUT:L2_XQ, :]
            wxq_ref[j * 128:(j + 1) * 128, :] = g2_ref[j, L2_XQ:L2_XO, :]
            wxo_ref[j * 128:(j + 1) * 128, :] = g2_ref[j, L2_XO:L2_DN, :]
            wdn_ref[j * RS_DN:(j + 1) * RS_DN, :] = g2_ref[j, L2_DN:L2_ROWS, :]
            wpr_ref[:, j * 128:(j + 1) * 128] = g3_ref[j, L3_PRET:L3_ROWS, :]
            wc_ref[:, j * NS_UP:(j + 1) * NS_UP] = g4_ref[j, 0:3, 0:NS_UP]
            for h in range(MLA_H):
                wpa_ref[h * HP:h * HP + MLA_V, j * 128:(j + 1) * 128] = g3_ref[j, h * MLA_V:(h + 1) * MLA_V, :]
        for h in range(MLA_H):
            wpa_ref[h * HP + MLA_V:(h + 1) * HP, :] = jnp.zeros((HP - MLA_V, D), BF16)

    return pl.pallas_call(
        body,
        name="assemble_l234",
        in_specs=[VM] * 3,
        out_specs=[VM] * 7,
        out_shape=[_sds((D, D), BF16), _sds((D, D), BF16), _sds((D, D), BF16), _sds((D_FF, D), BF16),
                   _sds((MLA_H * HP, D), BF16), _sds((RET_H * RET_D, D), BF16), _sds((3, 2 * D_FF), F32)],
        compiler_params=_cp0(),
    )(g2, g3, g4)


def slots_l1(dwup_a, dwup_b, dwxkv, *, tile):
    per_half = D_FF // NS_UP

    def body(ua_ref, ub_ref, x_ref, s_ref):
        for j in range(N_DEV):
            src = ua_ref if j < per_half else ub_ref
            c0 = (j % per_half) * NS_UP
            s_ref[j, :, 0:NS_UP] = src[:, c0:c0 + NS_UP].astype(BF16)
            s_ref[j, :, NS_UP:L1_W] = x_ref[:, j * NS_XKV:(j + 1) * NS_XKV].astype(BF16)

    return pl.pallas_call(
        body,
        name="slots_l1",
        grid=(D // tile,),
        in_specs=[pl.BlockSpec((tile, D_FF), lambda i: (i, 0)), pl.BlockSpec((tile, D_FF), lambda i: (i, 0)),
                  pl.BlockSpec((tile, 2 * D), lambda i: (i, 0))],
        out_specs=pl.BlockSpec((N_DEV, tile, L1_W), lambda i: (0, i, 0)),
        out_shape=_sds((N_DEV, D, L1_W), BF16),
        compiler_params=_cp(1),
    )(dwup_a, dwup_b, dwxkv)


def slots_l23(dwo, dwxq, dwxo, dwdn, dwpa, dwpr):
    def body(o_ref, xq_ref, xo_ref, dn_ref, pa_ref, pr_ref, l2_ref, l3_ref):
        l2_ref[L2_OUT:L2_XQ, :] = o_ref[...].astype(BF16)
        l2_ref[L2_XQ:L2_XO, :] = xq_ref[...].astype(BF16)
        l2_ref[L2_XO:L2_DN, :] = xo_ref[...].astype(BF16)
        l2_ref[L2_DN:L2_ROWS, :] = dn_ref[...].astype(BF16)
        for h in range(MLA_H):
            l3_ref[h * MLA_V:(h + 1) * MLA_V, :] = pa_ref[h * HP:h * HP + MLA_V, :].astype(BF16)
        l3_ref[L3_PRET:L3_ROWS, :] = pr_ref[...].astype(BF16)

    rows128 = pl.BlockSpec((128, D), lambda j: (j, 0))

    def cols(r):
        return pl.BlockSpec((r, 128), lambda j: (0, j))

    return pl.pallas_call(
        body,
        name="slots_l23",
        grid=(N_DEV,),
        in_specs=[rows128, rows128, rows128, pl.BlockSpec((RS_DN, D), lambda j: (j, 0)),
                  cols(MLA_H * HP), cols(RET_H * RET_D)],
        out_specs=[pl.BlockSpec((None, L2_ROWS, D), lambda j: (j, 0, 0)),
                   pl.BlockSpec((None, L3_ROWS, HP), lambda j: (j, 0, 0))],
        out_shape=[_sds((N_DEV, L2_ROWS, D), BF16), _sds((N_DEV, L3_ROWS, HP), BF16)],
        compiler_params=_cp(1),
    )(dwo, dwxq, dwxo, dwdn, dwpa, dwpr)


def slots_early(dwz, dwk, dwv, dwq, *, tile):
    def body(dz_ref, k_ref, v_ref, q_ref, s1_ref, s2_ref):
        for os_, ln_, ds_ in WZ_RUNS:
            for j, off, d, ln in _pieces(os_, ln_, ds_, NS_IN):
                s1_ref[j, :, off:off + ln] = dz_ref[:, d:d + ln].astype(BF16)

        @pl.when(pl.program_id(0) == 0)
        def _():
            for j in range(N_DEV):
                s2_ref[j, 0:E2_UQ, 0:MLA_NOPE] = k_ref[:, j * HP:j * HP + MLA_NOPE].astype(BF16)
                s2_ref[j, 0:E2_UQ, MLA_NOPE:HP] = v_ref[:, j * HP:j * HP + MLA_V].astype(BF16)
                s2_ref[j, E2_UQ:E2_ROWS, 0:NS_UQ] = q_ref[:, j * HP:j * HP + NS_UQ].astype(BF16)
                s2_ref[j, E2_UQ:E2_ROWS, NS_UQ:HP] = jnp.zeros((E2_ROWS - E2_UQ, HP - NS_UQ), BF16)

    def whole(r):
        return pl.BlockSpec((r, MLA_H * HP), lambda i: (0, 0))

    return pl.pallas_call(
        body,
        name="slots_early",
        grid=(D // tile,),
        in_specs=[pl.BlockSpec((tile, ZW), lambda i: (i, 0)), whole(MLA_KVR), whole(MLA_KVR), whole(MLA_QR)],
        out_specs=[pl.BlockSpec((N_DEV, tile, NS_IN), lambda i: (0, i, 0)),
                   pl.BlockSpec((N_DEV, E2_ROWS, HP), lambda i: (0, 0, 0))],
        out_shape=[_sds((N_DEV, D, NS_IN), BF16), _sds((N_DEV, E2_ROWS, HP), BF16)],
        compiler_params=_cp(1),
    )(dwz, dwk, dwv, dwq)


SMALL = (("g_mix", 1024, 0), ("b_gate", 2048, 1), ("g_q_lat", 256, 3), ("g_kv_lat", 128, 4), ("g_ret", 512, 5),
         ("g_cross", 1024, 6), ("g_mem", 1024, 7), ("g_ffn", 1024, 8), ("b_conv", 5632, 9), ("g_final", 1024, 15))
SMALL_DIRECT = tuple(s for s in SMALL if s[0] != "b_conv")
S_ROWS = 16


def _flat_pieces(n, row0):
    return [(row0 + k // D, k, min(D, n - k)) for k in range(0, n, D)]


def exchange_small(cs_a, cs_b, smalls):
    ns = len(smalls)

    def body(*refs):
        ca_ref, cb_ref = refs[:2]
        small_refs = refs[2:2 + ns]
        rd_ref, rs_ref, dsl_ref, own_ref = refs[2 + ns:6 + ns]
        sems = refs[6 + ns:]
        own_ref[...] = jnp.zeros(own_ref.shape, F32)
        for (name, n, row0), g_ref in zip(SMALL_DIRECT, small_refs):
            for r, c0, ln in _flat_pieces(n, row0):
                own_ref[r:r + 1, 0:ln] = g_ref[:, c0:c0 + ln]
        row0 = dict((s[0], s[2]) for s in SMALL)["b_conv"]
        for half, c_ref in enumerate((ca_ref, cb_ref)):
            k = half * D_FF
            end = k + D_FF
            while k < end:
                r, lane = row0 + k // D, k % D
                ln = min(D - lane, end - k)
                own_ref[r:r + 1, lane:lane + ln] = c_ref[3:4, k - half * D_FF:k - half * D_FF + ln]
                k += ln
        dsl_ref[...] = jnp.zeros(dsl_ref.shape, F32)
        per_half = D_FF // NS_UP
        for j in range(N_DEV):
            c_ref = ca_ref if j < per_half else cb_ref
            c0 = (j % per_half) * NS_UP
            dsl_ref[j, 0:3, 0:NS_UP] = c_ref[0:3, c0:c0 + NS_UP]
        comm = ([(dsl_ref, True), (own_ref, False)], [rd_ref, rs_ref]) + tuple(sems)
        _exchange_start(*comm)
        _exchange_wait(*comm)

    n1 = N_DEV - 1
    return pl.pallas_call(
        body,
        name="exchange_small",
        in_specs=[VM] * (2 + ns),
        out_specs=[VM, VM, VM, VM],
        out_shape=[_sds((n1,) + L4_SHAPE, F32), _sds((n1, S_ROWS, D), F32), _sds((N_DEV,) + L4_SHAPE, F32),
                   _sds((S_ROWS, D), F32)],
        scratch_shapes=_exchange_scratch(2),
        compiler_params=_cp0(),
    )(cs_a, cs_b, *smalls)


def _adamw(w, g, m, v):
    m = ADAM_B1 * m + (1.0 - ADAM_B1) * g
    v = ADAM_B2 * v + (1.0 - ADAM_B2) * (g * g)
    m_hat = m / (1.0 - ADAM_B1 ** ADAM_STEP)
    v_hat = v / (1.0 - ADAM_B2 ** ADAM_STEP)
    delta = -ADAM_LR * (m_hat / (jnp.sqrt(v_hat) + ADAM_EPS) + ADAM_WD * w)
    return delta, m, v


def _apply(g, refs, outs):
    d, mn, vn = _adamw(refs[0][...], g, refs[1][...], refs[2][...])
    outs[0][...] = g
    outs[1][...] = d
    outs[2][...] = mn
    outs[3][...] = vn


def adam_cols(own, recv, wmv, spans, *, name, tile):
    R, W = own.shape
    nw = len(spans)

    def body(*refs):
        own_ref, recv_ref = refs[:2]
        ins, outs = refs[2:2 + 3 * nw], refs[2 + 3 * nw:]
        g = own_ref[...].astype(F32)
        for k in range(N_DEV - 1):
            g = g + recv_ref[k].astype(F32)
        for t, (lo, hi) in enumerate(spans):
            _apply(g[:, lo:hi], ins[3 * t:3 * t + 3], outs[4 * t:4 * t + 4])

    def blk(w):
        return pl.BlockSpec((tile, w), lambda i: (i, 0))

    widths = [hi - lo for lo, hi in spans]
    return pl.pallas_call(
        body,
        name=name,
        grid=(R // tile,),
        in_specs=[blk(W), pl.BlockSpec((N_DEV - 1, tile, W), lambda i: (0, i, 0))] + [blk(w) for w in widths for _ in range(3)],
        out_specs=[blk(w) for w in widths for _ in range(4)],
        out_shape=[_sds((R, w), F32) for w in widths for _ in range(4)],
        compiler_params=_cp(1),
    )(own, recv, *wmv)


def adam_rows(own, recv, wmv, spans, *, name):
    nw = len(spans)

    def body(*refs):
        own_ref, recv_ref = refs[:2]
        ins, outs = refs[2:2 + 3 * nw], refs[2 + 3 * nw:]
        for t, (lo, hi, w) in enumerate(spans):
            g = own_ref[lo:hi, :].astype(F32)
            for k in range(N_DEV - 1):
                g = g + recv_ref[k, lo:hi, :].astype(F32)
            _apply(g[:, 0:w], ins[3 * t:3 * t + 3], outs[4 * t:4 * t + 4])

    return pl.pallas_call(
        body,
        name=name,
        in_specs=[VM] * (2 + 3 * nw),
        out_specs=[VM] * (4 * nw),
        out_shape=[_sds((hi - lo, w), F32) for lo, hi, w in spans for _ in range(4)],
        compiler_params=_cp0(),
    )(own, recv, *wmv)


def adam_small(own_s, recv_s, dslots, recv_d, wmv_small, wmv_conv):
    ns = len(SMALL)

    def body(*refs):
        own_ref, rs_ref, dsl_ref, rd_ref = refs[:4]
        ins = refs[4:4 + 3 * ns + 3]
        outs = refs[4 + 3 * ns + 3:4 + 3 * ns + 3 + 4 * ns + 4]
        all_sc = refs[-1]
        me = _my_id()
        all_sc[0] = own_ref[...]
        for k in range(N_DEV - 1):
            all_sc[k + 1] = rs_ref[k]
        g = all_sc[jnp.bitwise_xor(me, 0)]
        for s in range(1, N_DEV):
            g = g + all_sc[jnp.bitwise_xor(me, s)]
        all_sc[0] = g
        for t, (name, n, row0) in enumerate(SMALL):
            pieces = [all_sc[0, r:r + 1, 0:ln] for r, _, ln in _flat_pieces(n, row0)]
            gt = pieces[0] if len(pieces) == 1 else jnp.concatenate(pieces, axis=1)
            _apply(gt, ins[3 * t:3 * t + 3], outs[4 * t:4 * t + 4])
        gc = dsl_ref[me]
        for k in range(N_DEV - 1):
            gc = gc + rd_ref[k]
        _apply(gc[0:3, 0:NS_UP], ins[3 * ns:3 * ns + 3], outs[4 * ns:4 * ns + 4])

    out_shape = [_sds((1, n), F32) for _, n, _ in SMALL for _ in range(4)] + [_sds((3, NS_UP), F32)] * 4
    return pl.pallas_call(
        body,
        name="adam_small",
        in_specs=[VM] * (4 + 3 * ns + 3),
        out_specs=[VM] * len(out_shape),
        out_shape=out_shape,
        scratch_shapes=[pltpu.VMEM((N_DEV, S_ROWS, D), F32)],
    )(own_s, recv_s, dslots, recv_d, *wmv_small, *wmv_conv)


def kernel(x, mem, positions, g_mix, w_in, b_gate, g_q_lat, w_uq, g_kv_lat, w_ukv, w_proj_mla, g_ret, w_proj_ret, w_out, g_cross, g_mem, w_xq, w_xkv, w_xo, g_ffn, w_up, w_conv, b_conv, w_down, g_final, loss_target, m_g_mix, m_w_in, m_b_gate, m_g_q_lat, m_w_uq, m_g_kv_lat, m_w_ukv, m_w_proj_mla, m_g_ret, m_w_proj_ret, m_w_out, m_g_cross, m_g_mem, m_w_xq, m_w_xkv, m_w_xo, m_g_ffn, m_w_up, m_w_conv, m_b_conv, m_w_down, m_g_final, v_g_mix, v_w_in, v_b_gate, v_g_q_lat, v_w_uq, v_g_kv_lat, v_w_ukv, v_w_proj_mla, v_g_ret, v_w_proj_ret, v_w_out, v_g_cross, v_g_mem, v_w_xq, v_w_xkv, v_w_xo, v_g_ffn, v_w_up, v_w_conv, v_b_conv, v_w_down, v_g_final):
    args = dict(locals())
    T = x.shape[1]
    M = mem.shape[1]
    tile = min(256, T)
    tq = min(512, T)
    rb = min(1024, T)

    xs = x[0]
    tgt = loss_target[0]
    mems = mem[0]

    def shard(name, prefix=""):
        a = args[prefix + name]
        return a.reshape(a.shape[-2:]) if a.ndim >= 2 else a.reshape(1, -1)

    e1, e2 = pack_early(shard("w_in"), shard("w_ukv"), shard("w_uq"))
    ge1, ge2 = all_gather_groups([e1, e2], name="gather_early")
    wz, wk, wv, wq = assemble_early(ge1, ge2, tile=tile)
    late_parts = pack_late(shard("w_up"), shard("w_xkv"), shard("w_out"), shard("w_xq"), shard("w_xo"), shard("w_down"),
                           shard("w_proj_mla"), shard("w_proj_ret"), shard("w_conv"))

    pos_f = jnp.broadcast_to(positions[0].astype(F32)[:, None], (T, 128))
    cm, s1, s2, cr, sr = rot_tables(pos_f, tile=tile)
    rconsts = _ret_consts()

    u = rowwise(lambda xv, g: _rms(xv, g), [(xs, None)], [g_mix], [(D, BF16)], [], tile=tile, name="norm_mix")[0]
    z = matmul(u, wz, name="mm_z", tn=1536)

    def lat_fwd(zl, cmv, s1v, s2v, gq, gkv, wqv, wkv_, wvv):
        cq = _rms(zl[:, 0:256], gq).astype(BF16)
        ckv = _rms(zl[:, 256:384], gkv).astype(BF16)
        qv = _rot_mla(_dot(cq, wqv), cmv, s1v, s2v)
        kr = _rot_mla(zl[:, 384:512], cmv, s1v, s2v)
        kn = _dot(ckv, wkv_)
        kv_ = jnp.concatenate([kn[:, h * HP:(h + 1) * HP] + kr for h in range(MLA_H)], axis=1)
        vv = _dot(ckv, wvv)
        lane = lax.broadcasted_iota(jnp.int32, vv.shape, 1)
        vv = jnp.where((lane & (HP - 1)) == ONE_LANE, 1.0, vv)
        return qv, kv_, vv

    q_a, k_a, v_a = rowwise(
        lat_fwd, [(z, (512, 8)), (cm, None), (s1, None), (s2, None)], [g_q_lat, g_kv_lat, wq, wk, wv],
        [(MLA_H * HP, BF16)] * 3, [], tile=tile, name="lat_fwd")

    def retprep_fwd(zr, crv, srv):
        rqv = _rot_ret(zr[:, 0:512], crv, srv)
        rkv = _rot_ret(zr[:, 512:1024], crv, srv) * (RET_D ** -0.5)
        return rqv, rkv, zr[:, 1024:1536]

    rq, rk, rv = rowwise(retprep_fwd, [(z, (2048, 0)), (cr, None), (sr, None)], [], [(512, BF16)] * 3, [],
                         tile=tile, name="retprep_fwd")

    o_a, lse, gl1, gl2, gl3, gl4 = mla_fwd(q_a, k_a, v_a, list(late_parts), tq=tq)
    wup, wxkv = assemble_l1(gl1, tile=tile)
    wo, wxq, wxo, wdn, wpa, wpr, wcv = assemble_l234(gl2, gl3, gl4)
    ret, rstate = ret_fwd(rq, rk, rv, rconsts, rb=rb)

    def gn_parts(r):
        outs = []
        for h in range(RET_H):
            rh = r[:, h * RET_D:(h + 1) * RET_D]
            mu = jnp.mean(rh, axis=-1, keepdims=True)
            dlt = rh - mu
            rstd = lax.rsqrt(jnp.mean(dlt * dlt, axis=-1, keepdims=True) + EPS)
            outs.append((dlt * rstd, rstd))
        return outs

    def mix_fwd(ov, rv_, rg, gt, wpav, wprv, gr, bg):
        ya = _dot(ov, wpav)
        xh = jnp.concatenate([p[0] for p in gn_parts(rv_)], axis=1)
        t = rg * _sigmoid(rg) * (xh * gr)
        yr = _dot(t.astype(BF16), wprv)
        ga_ = _sigmoid(gt[:, :D] + bg[:, :D])
        gr_ = _sigmoid(gt[:, D:] + bg[:, D:])
        return ga_ * ya + gr_ * yr

    mix = rowwise(mix_fwd, [(o_a, None), (ret, None), (z, (512, 3)), (z, (2048, 1))], [wpa, wpr, g_ret, b_gate],
                  [(D, BF16)], [], tile=tile, name="mix_fwd")[0]
    h1 = matmul(mix, wo, res=xs, name="mm_out")
    n2 = rowwise(lambda hv, g: _rms(hv, g), [(h1, None)], [g_cross], [(D, BF16)], [], tile=tile, name="norm_cross")[0]
    xq = matmul(n2, wxq, out_dtype=BF16, name="mm_xq")
    mn = rowwise(lambda mv_, g: _rms(mv_, g), [(mems, None)], [g_mem], [(D, BF16)], [], tile=min(tile, M), name="norm_mem")[0]
    mkv = matmul(mn, wxkv, out_dtype=BF16, name="mm_mkv")

    x_scale = X_HD ** -0.5

    def xattn_fwd(xqv, mkvv):
        outs = []
        for h in range(X_H):
            sl = slice(h * X_HD, (h + 1) * X_HD)
            s = _dot_nt(xqv[:, sl], mkvv[:, sl]) * x_scale
            s = s - jnp.max(s, axis=-1, keepdims=True)
            e = jnp.exp(s)
            p = e / jnp.sum(e, axis=-1, keepdims=True)
            outs.append(_dot(p.astype(BF16), mkvv[:, D + h * X_HD:D + (h + 1) * X_HD]))
        return jnp.concatenate(outs, axis=1)

    xo = rowwise(xattn_fwd, [(xq, None)], [mkv], [(D, BF16)], [], tile=tile, name="xattn_fwd")[0]
    h2 = matmul(xo, wxo, res=h1, name="mm_xo")
    n3 = rowwise(lambda hv, g: _rms(hv, g), [(h2, None)], [g_ffn], [(D, BF16)], [], tile=tile, name="norm_ffn")[0]
    up_pre = matmul(n3, wup, out_dtype=BF16, name="mm_up", tn=1408)
    cw = D_FF // 2
    act = conv_act_fwd(up_pre, wcv, b_conv, tile=tile, cw=cw)
    h3 = matmul(act, wdn, res=h2, name="mm_down")

    def loss_fn(hv, tv, g):
        y = _rms(hv, g)
        err = y - tv
        part = 0.5 * jnp.sum(jnp.sum(err * err, axis=-1, keepdims=True) / D, axis=0, keepdims=True)
        dx, dg = _rms_bwd(hv, g, err / D)
        return dx, dg, jnp.broadcast_to(part, (8, 128))

    g_fin2 = g_final.reshape(1, D)
    dh3, dg_final, loss_acc = rowwise(loss_fn, [(h3, None), (tgt, None)], [g_fin2], [(D, F32)],
                                      [((1, D), F32), ((8, 128), F32)], tile=tile, name="loss_bwd")
    loss = lax.psum(loss_acc[0, 0], ("x", "y", "c"))

    dact = matmul(dh3, wdn, tb=True, out_dtype=BF16, name="mm_dact", tn=1408)
    dw_down = matmul_tn(act, dh3, name="mm_dw_down", tm=1408)
    dup_a, dup_b, cs_a, cs_b = conv_act_bwd(up_pre, dact, wcv, b_conv, tile=tile, cw=cw)
    dn3 = matmul2_tb(dup_a, dup_b, wup, name="mm_dn3", tm=1024, tn=512)
    dw_up_a = matmul_tn(n3, dup_a, name="mm_dw_up_a", tn=1408)
    dw_up_b = matmul_tn(n3, dup_b, name="mm_dw_up_b", tn=1408)

    def norm_bwd(hv, dyv, drv, g):
        dx, dg = _rms_bwd(hv, g, dyv)
        return dx + drv, dg

    dh2, dg_ffn = rowwise(norm_bwd, [(h2, None), (dn3, None), (dh3, None)], [g_ffn], [(D, F32)], [((1, D), F32)],
                          tile=tile, name="norm_ffn_bwd")
    dxo = matmul(dh2, wxo, tb=True, out_dtype=BF16, name="mm_dxo")
    dw_xo = matmul_tn(xo, dh2, name="mm_dw_xo")

    def xattn_bwd(xqv, dxov, mkvv):
        dxq, dmk, dmv = [], [], []
        for h in range(X_H):
            sl = slice(h * X_HD, (h + 1) * X_HD)
            slv = slice(D + h * X_HD, D + (h + 1) * X_HD)
            s = _dot_nt(xqv[:, sl], mkvv[:, sl]) * x_scale
            s = s - jnp.max(s, axis=-1, keepdims=True)
            e = jnp.exp(s)
            p = e / jnp.sum(e, axis=-1, keepdims=True)
            dp = _dot_nt(dxov[:, sl], mkvv[:, slv])
            ds = (p * (dp - jnp.sum(dp * p, axis=-1, keepdims=True)) * x_scale).astype(BF16)
            dxq.append(_dot(ds, mkvv[:, sl]))
            dmk.append(_dot_tn(ds, xqv[:, sl]))
            dmv.append(_dot_tn(p.astype(BF16), dxov[:, sl]))
        return jnp.concatenate(dxq, axis=1), jnp.concatenate(dmk + dmv, axis=1)

    dxq, dmkv = rowwise(xattn_bwd, [(xq, None), (dxo, None)], [mkv], [(D, BF16)], [((M, 2 * D), F32)],
                        tile=tile, name="xattn_bwd")
    dn2 = matmul(dxq, wxq, tb=True, name="mm_dn2")
    dw_xq = matmul_tn(n2, dxq, name="mm_dw_xq")
    dh1, dg_cross = rowwise(norm_bwd, [(h1, None), (dn2, None), (dh2, None)], [g_cross], [(D, F32)], [((1, D), F32)],
                            tile=tile, name="norm_cross_bwd")
    dw_xkv = matmul_tn(mn, dmkv, name="mm_dw_xkv", tk=M)
    dmn = matmul(dmkv, wxkv, tb=True, name="mm_dmn", tm=M)
    dg_mem = rowwise(lambda mv_, dyv, g: _rms_bwd(mv_, g, dyv)[1], [(mems, None), (dmn, None)], [g_mem], [],
                     [((1, D), F32)], tile=min(tile, M), name="norm_mem_bwd")[0]

    dmix = matmul(dh1, wo, tb=True, out_dtype=BF16, name="mm_dmix")
    dw_out = matmul_tn(mix, dh1, name="mm_dw_out")

    def mix_bwd(ov, rv_, rg, gt, dmv_, wpav, wprv, gr, bg):
        dm_ = dmv_.astype(F32)
        ya = _dot(ov, wpav)
        parts = gn_parts(rv_)
        xh = jnp.concatenate([p[0] for p in parts], axis=1)
        yn = xh * gr
        sg = _sigmoid(rg)
        sl_ = rg * sg
        t = (sl_ * yn).astype(BF16)
        yr = _dot(t, wprv)
        ga_ = _sigmoid(gt[:, :D] + bg[:, :D])
        gr_ = _sigmoid(gt[:, D:] + bg[:, D:])
        dgates = jnp.concatenate([dm_ * ya * ga_ * (1.0 - ga_), dm_ * yr * gr_ * (1.0 - gr_)], axis=1)
        dya = (dm_ * ga_).astype(BF16)
        dyr = (dm_ * gr_).astype(BF16)
        do_ = _dot_nt(dya, wpav)
        dwpa_ = _dot_tn(ov, dya)
        dt = _dot_nt(dyr, wprv)
        dwpr_ = _dot_tn(t, dyr)
        drg_ = dt * yn * (sg * (1.0 + rg * (1.0 - sg)))
        dyn = dt * sl_
        dgr = jnp.sum(dyn * xh, axis=0, keepdims=True)
        dxh = dyn * gr
        drets = []
        for h in range(RET_H):
            sl = slice(h * RET_D, (h + 1) * RET_D)
            xhh, rstd = parts[h]
            dxhh = dxh[:, sl]
            drets.append(rstd * (dxhh - jnp.mean(dxhh, axis=-1, keepdims=True)
                                 - xhh * jnp.mean(dxhh * xhh, axis=-1, keepdims=True)))
        dret_ = jnp.concatenate(drets, axis=1)
        dbg = jnp.sum(dgates, axis=0, keepdims=True)
        return do_, dret_, drg_, dgates, dwpa_, dwpr_, dgr, dbg

    do_a, dret, drg, dgates, dwpa, dw_proj_ret, dg_ret, db_gate = rowwise(
        mix_bwd, [(o_a, None), (ret, None), (z, (512, 3)), (z, (2048, 1)), (dmix, None)], [wpa, wpr, g_ret, b_gate],
        [(MLA_H * HP, BF16), (512, F32), (512, BF16), (2 * D, BF16)],
        [((MLA_H * HP, D), F32), ((512, D), F32), ((1, 512), F32), ((1, 2 * D), F32)], tile=tile, name="mix_bwd")

    sl1 = slots_l1(dw_up_a, dw_up_b, dw_xkv, tile=tile)
    sl2, sl3 = slots_l23(dw_out, dw_xq, dw_xo, dw_down, dwpa, dw_proj_ret)
    lse_row, delta_row = mla_prep(o_a, do_a, lse, tq=tq)
    dq_a, dk_a, dv_a, rl1, rl2, rl3 = mla_bwd(q_a, k_a, v_a, do_a, lse_row, delta_row, [sl1, sl2, sl3], tq=tq)
    drq_r, drk_r, drv = ret_bwd(rq, rk, rv, rstate, dret, rconsts, rb=rb)

    def lat_bwd(zl, cmv, s1v, s2v, dqv, dkv_, dvv, gq, gkv, wqv, wkv_, wvv):
        cqf, ckvf = zl[:, 0:256], zl[:, 256:384]
        cq = _rms(cqf, gq).astype(BF16)
        ckv = _rms(ckvf, gkv).astype(BF16)
        dq_pre = _rot_mla(dqv.astype(F32), cmv, -s1v, -s2v).astype(BF16)
        dkf = dkv_.astype(F32)
        dkr = dkf[:, 0:HP]
        for h in range(1, MLA_H):
            dkr = dkr + dkf[:, h * HP:(h + 1) * HP]
        lane = lax.broadcasted_iota(jnp.int32, dkr.shape, 1)
        dzk = _rot_mla(jnp.where((lane >= 64) & (lane < 96), dkr, 0.0), cmv, -s1v, -s2v)
        dkb = dkv_.astype(BF16)
        dvb = dvv.astype(BF16)
        dcq_n = _dot_nt(dq_pre, wqv)
        dckv_n = _dot_nt(dkb, wkv_) + _dot_nt(dvb, wvv)
        dwq_ = _dot_tn(cq, dq_pre)
        dwk_ = _dot_tn(ckv, dkb)
        dwv_ = _dot_tn(ckv, dvb)
        dcq, dgq = _rms_bwd(cqf, gq, dcq_n)
        dckv, dgkv = _rms_bwd(ckvf, gkv, dckv_n)
        return jnp.concatenate([dcq, dckv, dzk], axis=1), dwq_, dwk_, dwv_, dgq, dgkv

    dz_lat, dwq, dwk, dwv, dg_q_lat, dg_kv_lat = rowwise(
        lat_bwd, [(z, (512, 8)), (cm, None), (s1, None), (s2, None), (dq_a, None), (dk_a, None), (dv_a, None)],
        [g_q_lat, g_kv_lat, wq, wk, wv], [(512, BF16)],
        [((MLA_QR, MLA_H * HP), F32), ((MLA_KVR, MLA_H * HP), F32), ((MLA_KVR, MLA_H * HP), F32),
         ((1, MLA_QR), F32), ((1, MLA_KVR), F32)], tile=tile, name="lat_bwd")

    def dz_assemble(dq_, dk_, dv_, drg_, dgt, dzl, crv, srv):
        a = _rot_ret(dq_, crv, -srv)
        b = _rot_ret(dk_, crv, -srv) * (RET_D ** -0.5)
        return jnp.concatenate([a, b, dv_, drg_.astype(F32), dgt.astype(F32), dzl.astype(F32)], axis=1)

    dz = rowwise(dz_assemble, [(drq_r, None), (drk_r, None), (drv, None), (drg, None), (dgates, None), (dz_lat, None),
                               (cr, None), (sr, None)], [], [(ZW, BF16)], [], tile=tile, name="dz_assemble")[0]
    dwz = matmul_tn(u, dz, name="mm_dw_z", tn=1536)
    se1, se2 = slots_early(dwz, dwk, dwv, dwq, tile=tile)
    du, re1, re2 = matmul(dz, wz, tb=True, name="mm_du", tm=1024, tn=512, exchange=[se1, se2])

    grad_x, dg_mix = rowwise(norm_bwd, [(xs, None), (du, None), (dh1, None)], [g_mix], [(D, F32)], [((1, D), F32)],
                             tile=tile, name="norm_mix_bwd")

    small_grads = {"g_mix": dg_mix, "b_gate": db_gate, "g_q_lat": dg_q_lat, "g_kv_lat": dg_kv_lat, "g_ret": dg_ret,
                   "g_cross": dg_cross, "g_mem": dg_mem, "g_ffn": dg_ffn, "g_final": dg_final}
    rd, rs, dslots, own_s = exchange_small(cs_a, cs_b, [small_grads[n] for n, _, _ in SMALL_DIRECT])

    me = _my_id()

    def own(slots):
        return lax.dynamic_index_in_dim(slots, me, axis=0, keepdims=False)

    def wmv(names):
        return [shard(n, p) for n in names for p in ("", "m_", "v_")]

    names_s = tuple(n for n, _, _ in SMALL)
    groups = (
        (("w_in",), adam_cols(own(se1), re1, wmv(("w_in",)), ((0, NS_IN),), name="adam_e1", tile=128)),
        (("w_ukv", "w_uq"), adam_rows(own(se2), re2, wmv(("w_ukv", "w_uq")),
                                      ((0, E2_UQ, HP), (E2_UQ, E2_ROWS, NS_UQ)), name="adam_e2")),
        (("w_up", "w_xkv"), adam_cols(own(sl1), rl1, wmv(("w_up", "w_xkv")), ((0, NS_UP), (NS_UP, L1_W)),
                                      name="adam_l1", tile=128)),
        (("w_out", "w_xq", "w_xo", "w_down"),
         adam_rows(own(sl2), rl2, wmv(("w_out", "w_xq", "w_xo", "w_down")),
                   ((L2_OUT, L2_XQ, D), (L2_XQ, L2_XO, D), (L2_XO, L2_DN, D), (L2_DN, L2_ROWS, D)), name="adam_l2")),
        (("w_proj_mla", "w_proj_ret"), adam_rows(own(sl3), rl3, wmv(("w_proj_mla", "w_proj_ret")),
                                                 ((0, L3_PRET, HP), (L3_PRET, L3_ROWS, HP)), name="adam_l3")),
        (names_s + ("w_conv",), adam_small(own_s, rs, dslots, rd, wmv(names_s), wmv(("w_conv",)))),
    )
    res = {}
    for names, outs_ in groups:
        for t, n in enumerate(names):
            res[n] = outs_[4 * t:4 * t + 4]

    order = ["g_mix", "w_in", "b_gate", "g_q_lat", "w_uq", "g_kv_lat", "w_ukv", "w_proj_mla", "g_ret", "w_proj_ret",
             "w_out", "g_cross", "g_mem", "w_xq", "w_xkv", "w_xo", "g_ffn", "w_up", "w_conv", "b_conv", "w_down",
             "g_final"]
    outs = [loss, grad_x[None]]
    for kind in range(4):
        outs += [res[n][kind].reshape(args[n].shape) for n in order]
    return tuple(outs)
```

```python
import functools
import math

import jax
import jax.numpy as jnp
import numpy as np
from jax import lax
from jax.experimental import pallas as pl
from jax.experimental.pallas import tpu as pltpu

F32 = jnp.float32
BF16 = jnp.bfloat16

D = 1024
MLA_H, MLA_NOPE, MLA_ROPE, MLA_V = 8, 64, 32, 64
MLA_QR, MLA_KVR = 256, 128
RET_H, RET_D, RET_C = 4, 128, 128
X_H, X_HD = 4, 256
D_FF = 2816
THETA = 10000.0
EPS = 1e-6
HP = 128
ZW = 4608
N_DEV = 8

ADAM_LR, ADAM_B1, ADAM_B2, ADAM_EPS, ADAM_WD, ADAM_STEP = 0.001, 0.9, 0.999, 1e-08, 0.01, 10

VMEM_LIMIT = 56 * 1024 * 1024
MESH = pl.DeviceIdType.MESH
VM = pl.BlockSpec(memory_space=pltpu.VMEM)
ANY = pl.BlockSpec(memory_space=pl.ANY)


def _cp(n_axes):
    return pltpu.CompilerParams(dimension_semantics=("arbitrary",) * n_axes, vmem_limit_bytes=VMEM_LIMIT)


def _cp0():
    return pltpu.CompilerParams(vmem_limit_bytes=VMEM_LIMIT)


def _pick(n, cap, mult=128):
    best = None
    for t in range(mult, min(n, cap) + 1, mult):
        if n % t == 0:
            best = t
    return best if best is not None else n


def _dot(a, b):
    return jnp.dot(a, b, preferred_element_type=F32)


def _dot_nt(a, b):
    return lax.dot_general(a, b, (((1,), (1,)), ((), ())), preferred_element_type=F32)


def _dot_tn(a, b):
    return lax.dot_general(a, b, (((0,), (0,)), ((), ())), preferred_element_type=F32)


def _sds(shape, dtype):
    return jax.ShapeDtypeStruct(shape, dtype)


def matmul(a, b, *, name, tb=False, res=None, out_dtype=F32, tm=1024, tn=1024, exchange=()):
    M, K = a.shape
    N = b.shape[0] if tb else b.shape[1]
    tm = _pick(M, tm, 8)
    tn = _pick(N, tn)
    gi, gj = M // tm, N // tn
    ns = len(exchange)
    n_in = 2 if res is None else 3

    def body(*refs):
        a_ref, b_ref = refs[:2]
        o_ref = refs[n_in + ns]
        if ns:
            comm = ([(r, True) for r in refs[n_in:n_in + ns]], refs[n_in + ns + 1:n_in + 2 * ns + 1]) + tuple(
                refs[n_in + 2 * ns + 1:])
            i, j = pl.program_id(0), pl.program_id(1)

            @pl.when((i == 0) & (j == 0))
            def _():
                _exchange_start(*comm)

        av = a_ref[...].astype(BF16)
        bv = b_ref[...].astype(BF16)
        acc = _dot_nt(av, bv) if tb else _dot(av, bv)
        if res is not None:
            acc = acc + refs[2][...].astype(F32)
        o_ref[...] = acc.astype(o_ref.dtype)
        if ns:

            @pl.when((i == gi - 1) & (j == gj - 1))
            def _():
                _exchange_wait(*comm)

    in_specs = [
        pl.BlockSpec((tm, K), lambda i, j: (i, 0)),
        pl.BlockSpec((tn, K), lambda i, j: (j, 0)) if tb else pl.BlockSpec((K, tn), lambda i, j: (0, j)),
    ]
    args = [a, b]
    if res is not None:
        in_specs.append(pl.BlockSpec((tm, tn), lambda i, j: (i, j)))
        args.append(res)
    out_spec = pl.BlockSpec((tm, tn), lambda i, j: (i, j))
    out_shape = _sds((M, N), out_dtype)
    return pl.pallas_call(
        body,
        name=name,
        grid=(gi, gj),
        in_specs=in_specs + [ANY] * ns,
        out_specs=[out_spec] + [ANY] * ns if ns else out_spec,
        out_shape=[out_shape] + [_sds((N_DEV - 1,) + s.shape[1:], s.dtype) for s in exchange] if ns else out_shape,
        scratch_shapes=_exchange_scratch(ns) if ns else [],
        compiler_params=_cp(2),
    )(*args, *exchange)


def matmul2_tb(a1, a2, b, *, name, tm=512, tn=512):
    M, K1 = a1.shape
    N = b.shape[0]
    tm = _pick(M, tm, 8)
    tn = _pick(N, tn)

    def body(a1_ref, a2_ref, b1_ref, b2_ref, o_ref):
        o_ref[...] = (_dot_nt(a1_ref[...].astype(BF16), b1_ref[...].astype(BF16))
                      + _dot_nt(a2_ref[...].astype(BF16), b2_ref[...].astype(BF16)))

    return pl.pallas_call(
        body,
        name=name,
        grid=(M // tm, N // tn),
        in_specs=[pl.BlockSpec((tm, K1), lambda i, j: (i, 0)), pl.BlockSpec((tm, K1), lambda i, j: (i, 0)),
                  pl.BlockSpec((tn, K1), lambda i, j: (j, 0)), pl.BlockSpec((tn, K1), lambda i, j: (j, 1))],
        out_specs=pl.BlockSpec((tm, tn), lambda i, j: (i, j)),
        out_shape=_sds((M, N), F32),
        compiler_params=_cp(2),
    )(a1, a2, b, b)


def matmul_tn(a, b, *, name, tm=1024, tn=1024, tk=1024):
    R, M = a.shape
    N = b.shape[1]
    tm = _pick(M, tm)
    tn = _pick(N, tn)
    tk = _pick(R, tk, 16)
    nk = R // tk

    def body(a_ref, b_ref, o_ref, acc_ref):
        k = pl.program_id(2)

        @pl.when(k == 0)
        def _():
            acc_ref[...] = jnp.zeros_like(acc_ref)

        acc_ref[...] += _dot_tn(a_ref[...].astype(BF16), b_ref[...].astype(BF16))

        @pl.when(k == nk - 1)
        def _():
            o_ref[...] = acc_ref[...]

    return pl.pallas_call(
        body,
        name=name,
        grid=(M // tm, N // tn, nk),
        in_specs=[pl.BlockSpec((tk, tm), lambda i, j, k: (k, i)), pl.BlockSpec((tk, tn), lambda i, j, k: (k, j))],
        out_specs=pl.BlockSpec((tm, tn), lambda i, j, k: (i, j)),
        out_shape=_sds((M, N), F32),
        scratch_shapes=[pltpu.VMEM((tm, tn), F32)],
        compiler_params=_cp(3),
    )(a, b)


def rowwise(fn, rows, consts, out_rows, out_accs, *, tile, name, gather=()):
    T = rows[0][0].shape[0]
    nt = T // tile
    n_r, n_c, n_o, n_a, n_g = len(rows), len(consts), len(out_rows), len(out_accs), len(gather)
    n_in = n_r + n_c + n_g

    def body(*refs):
        if n_g:
            comm = (refs[n_r + n_c:n_in], refs[n_in + n_o + n_a:n_in + n_o + n_a + n_g]) + tuple(
                refs[n_in + n_o + n_a + n_g:])

            @pl.when(pl.program_id(0) == 0)
            def _():
                _gather_start(*comm)

        ins = [r[...] for r in refs[: n_r + n_c]]
        outs = fn(*ins)
        if not isinstance(outs, (tuple, list)):
            outs = (outs,)
        o_refs = refs[n_in:n_in + n_o]
        a_refs = refs[n_in + n_o:n_in + n_o + n_a]
        for o_ref, o in zip(o_refs, outs[:n_o]):
            o_ref[...] = o.astype(o_ref.dtype)
        if n_a:
            first = pl.program_id(0) == 0

            @pl.when(first)
            def _():
                for a_ref, o in zip(a_refs, outs[n_o:]):
                    a_ref[...] = o.astype(a_ref.dtype)

            @pl.when(jnp.logical_not(first))
            def _():
                for a_ref, o in zip(a_refs, outs[n_o:]):
                    a_ref[...] += o.astype(a_ref.dtype)
        if n_g:

            @pl.when(pl.program_id(0) == nt - 1)
            def _():
                _gather_wait(*comm)

    in_specs = []
    args = []
    for arr, win in rows:
        if win is None:
            in_specs.append(pl.BlockSpec((tile, arr.shape[1]), lambda i: (i, 0)))
        else:
            w, cb = win
            in_specs.append(pl.BlockSpec((tile, w), functools.partial(lambda i, cb: (i, cb), cb=cb)))
        args.append(arr)
    for c in consts:
        in_specs.append(pl.BlockSpec(c.shape, functools.partial(lambda i, nd: (0,) * nd, nd=c.ndim)))
        args.append(c)
    out_specs = [pl.BlockSpec((tile, w), lambda i: (i, 0)) for w, _ in out_rows]
    out_shape = [_sds((T, w), dt) for w, dt in out_rows]
    for shp, dt in out_accs:
        out_specs.append(pl.BlockSpec(shp, functools.partial(lambda i, nd: (0,) * nd, nd=len(shp))))
        out_shape.append(_sds(shp, dt))
    return pl.pallas_call(
        body,
        name=name,
        grid=(nt,),
        in_specs=in_specs + [ANY] * n_g,
        out_specs=out_specs + [ANY] * n_g,
        out_shape=out_shape + [_sds((N_DEV,) + p.shape, p.dtype) for p in gather],
        scratch_shapes=_gather_scratch(n_g) if n_g else [],
        compiler_params=_cp(1),
    )(*args, *gather)


def _rms(x, g):
    r = lax.rsqrt(jnp.mean(x * x, axis=-1, keepdims=True) + EPS)
    return x * r * g


def _rms_bwd(x, g, dy):
    r = lax.rsqrt(jnp.mean(x * x, axis=-1, keepdims=True) + EPS)
    xh = x * r
    dg = jnp.sum(dy * xh, axis=0, keepdims=True)
    dxh = dy * g
    dx = r * (dxh - xh * jnp.mean(dxh * xh, axis=-1, keepdims=True))
    return dx, dg


def _sigmoid(x):
    return 0.5 * jnp.tanh(0.5 * x) + 0.5


def _rot_mla(x, c, s1, s2):
    n = x.shape[1] // HP
    outs = []
    for h in range(n):
        xh = x[:, h * HP : (h + 1) * HP]
        outs.append(xh * c + pltpu.roll(xh, HP - 16, 1) * s1 + pltpu.roll(xh, 16, 1) * s2)
    return outs[0] if n == 1 else jnp.concatenate(outs, axis=1)


def _rot_ret(x, c, s):
    n = x.shape[1] // RET_D
    outs = []
    for h in range(n):
        xh = x[:, h * RET_D : (h + 1) * RET_D]
        outs.append(xh * c + pltpu.roll(xh, RET_D // 2, 1) * s)
    return outs[0] if n == 1 else jnp.concatenate(outs, axis=1)


def _rot_inv():
    lane_np = np.arange(128)
    inv_m = (jnp.asarray(THETA, F32) ** (-jnp.asarray(lane_np & 15, F32) / 16.0)).reshape(1, 128)
    inv_r = (jnp.asarray(THETA, F32) ** (-jnp.asarray(lane_np & 63, F32) / 64.0)).reshape(1, 128)
    return inv_m, inv_r


def _rot_tables(p, im, ir):
    lane = lax.broadcasted_iota(jnp.int32, p.shape, 1)
    ang = p * im
    cm = jnp.where((lane >= 64) & (lane < 96), jnp.cos(ang), 1.0)
    sn = jnp.sin(ang)
    s1 = jnp.where((lane >= 64) & (lane < 80), -sn, 0.0)
    s2 = jnp.where((lane >= 80) & (lane < 96), sn, 0.0)
    angr = p * ir
    snr = jnp.sin(angr)
    return cm, s1, s2, jnp.cos(angr), jnp.where(lane < 64, -snr, snr)


def _peer(m):
    x, y, c = lax.axis_index("x"), lax.axis_index("y"), lax.axis_index("c")
    mx, my, mc = (m >> 2) & 1, (m >> 1) & 1, m & 1
    px = 1 - x if mx else x
    py = 1 - y if my else y
    pc = 1 - c if mc else c
    return (px, py, pc), 4 * px + 2 * py + pc


def _my_id():
    return 4 * lax.axis_index("x") + 2 * lax.axis_index("y") + lax.axis_index("c")


def _gather_copies(srcs, outs, send_sems, recv_sems, local_sems, arriving=False):
    me = _my_id()
    copies = []
    if not arriving:
        for g, (src, out) in enumerate(zip(srcs, outs)):
            copies.append((pltpu.make_async_copy(src, out.at[me], local_sems.at[g]), False))
    for m in range(1, N_DEV):
        peer, plin = _peer(m)
        for g, (src, out) in enumerate(zip(srcs, outs)):
            copies.append((pltpu.make_async_remote_copy(
                src_ref=src, dst_ref=out.at[plin if arriving else me], send_sem=send_sems.at[g, m - 1],
                recv_sem=recv_sems.at[g, m - 1], device_id=peer, device_id_type=MESH), True))
    return copies


def _gather_start(*a):
    for cp, _ in _gather_copies(*a):
        cp.start()


def _gather_wait(*a):
    for cp, _ in _gather_copies(*a, arriving=True):
        cp.wait_recv()
    for cp, remote in _gather_copies(*a):
        if remote:
            cp.wait_send()
        else:
            cp.wait()


def _gather_scratch(n):
    return [pltpu.SemaphoreType.DMA((n, N_DEV - 1)), pltpu.SemaphoreType.DMA((n, N_DEV - 1)), pltpu.SemaphoreType.DMA((n,))]


def _exchange_copies(srcs, dsts, send_sems, recv_sems):
    copies = []
    for m in range(1, N_DEV):
        peer, plin = _peer(m)
        for g, ((src, per_peer), dst) in enumerate(zip(srcs, dsts)):
            copies.append(pltpu.make_async_remote_copy(
                src_ref=src.at[plin] if per_peer else src, dst_ref=dst.at[m - 1], send_sem=send_sems.at[g, m - 1],
                recv_sem=recv_sems.at[g, m - 1], device_id=peer, device_id_type=MESH))
    return copies


def _exchange_start(*a):
    for cp in _exchange_copies(*a):
        cp.start()


def _exchange_wait(*a):
    copies = _exchange_copies(*a)
    for cp in copies:
        cp.wait_recv()
    for cp in copies:
        cp.wait_send()


def _exchange_scratch(n):
    return [pltpu.SemaphoreType.DMA((n, N_DEV - 1)), pltpu.SemaphoreType.DMA((n, N_DEV - 1))]


MLA_SCALE = (MLA_NOPE + MLA_ROPE) ** -0.5
MLA_C2 = MLA_SCALE * math.log2(math.e)
ONE_LANE = MLA_V
NEG = -1e30


def _tri_mask(n, lower_rows_ge_cols=True):
    r = lax.broadcasted_iota(jnp.int32, (n, n), 0)
    c = lax.broadcasted_iota(jnp.int32, (n, n), 1)
    return r >= c if lower_rows_ge_cols else c >= r


def mla_fwd(q, k, v, gather, *, tq):
    T = q.shape[0]
    nq = T // tq
    rep = tq // HP
    ng = len(gather)

    def body(*refs):
        q_ref, k_ref, v_ref = refs[:3]
        srcs = refs[3:3 + ng]
        o_ref, lse_ref = refs[3 + ng:5 + ng]
        outs = refs[5 + ng:5 + 2 * ng]
        m_sc, acc_sc = refs[5 + 2 * ng:7 + 2 * ng]
        comm = (srcs, outs) + tuple(refs[7 + 2 * ng:])
        h, i = pl.program_id(0), pl.program_id(1)

        @pl.when((h == 0) & (i == 0))
        def _():
            _gather_start(*comm)

        qv = q_ref[...]
        m_sc[...] = jnp.full(m_sc.shape, NEG, F32)
        acc_sc[...] = jnp.zeros(acc_sc.shape, F32)

        def block(j, masked):
            off = pl.multiple_of(j * tq, tq)
            kb = k_ref[pl.ds(off, tq), :]
            vb = v_ref[pl.ds(off, tq), :]
            s = _dot_nt(qv, kb) * MLA_C2
            if masked:
                s = jnp.where(_tri_mask(tq), s, NEG)
            m_prev = m_sc[...]
            m_next = jnp.maximum(m_prev, jnp.max(s, axis=-1, keepdims=True))
            p = jnp.exp2(s - jnp.tile(m_next, (1, rep)))
            alpha = jnp.exp2(m_prev - m_next)
            acc_sc[...] = alpha * acc_sc[...] + _dot(p.astype(BF16), vb)
            m_sc[...] = m_next

        def loop_body(j, carry):
            block(j, False)
            return carry

        lax.fori_loop(0, i, loop_body, 0)
        block(i, True)
        acc = acc_sc[...]
        l = acc[:, ONE_LANE:ONE_LANE + 1]
        o_ref[...] = (acc / l).astype(o_ref.dtype)
        lse_ref[...] = m_sc[...] + jnp.log(l) * math.log2(math.e)

        @pl.when((h == MLA_H - 1) & (i == nq - 1))
        def _():
            _gather_wait(*comm)

    blk = pl.BlockSpec((tq, HP), lambda h, i: (i, h))
    full = pl.BlockSpec((T, HP), lambda h, i: (0, h))
    return pl.pallas_call(
        body,
        name="mla_fwd",
        grid=(MLA_H, nq),
        in_specs=[blk, full, full] + [ANY] * ng,
        out_specs=[blk, blk] + [ANY] * ng,
        out_shape=[_sds((T, MLA_H * HP), BF16), _sds((T, MLA_H * HP), F32)]
        + [_sds((N_DEV,) + p.shape, p.dtype) for p in gather],
        scratch_shapes=[pltpu.VMEM((tq, HP), F32), pltpu.VMEM((tq, HP), F32)] + _gather_scratch(ng),
        compiler_params=_cp(2),
    )(q, k, v, *gather)


def mla_prep(o, do, lse, *, tq):
    T = o.shape[0]
    nq = T // tq

    def body(o_ref, do_ref, lse_ref, lse_row_ref, delta_row_ref):
        for h in range(MLA_H):
            sl = slice(h * HP, (h + 1) * HP)
            d = jnp.sum(o_ref[:, sl].astype(F32) * do_ref[:, sl].astype(F32), axis=-1, keepdims=True)
            lse_row_ref[h] = lse_ref[:, sl].T[0:1, :]
            delta_row_ref[h] = jnp.broadcast_to(d, (tq, HP)).T[0:1, :]

    blk = pl.BlockSpec((tq, MLA_H * HP), lambda i: (i, 0))
    row = pl.BlockSpec((MLA_H, None, 1, tq), lambda i: (0, i, 0, 0))
    return pl.pallas_call(
        body,
        name="mla_prep",
        grid=(nq,),
        in_specs=[blk, blk, blk],
        out_specs=[row, row],
        out_shape=[_sds((MLA_H, nq, 1, tq), F32), _sds((MLA_H, nq, 1, tq), F32)],
        compiler_params=_cp(1),
    )(o, do, lse)


def mla_bwd(q, k, v, do, lse_row, delta_row, slots, *, tq):
    T = q.shape[0]
    nq = T // tq
    ns = len(slots)

    def body(*refs):
        q_ref, k_ref, v_ref, do_ref, lse_ref, delta_ref = refs[:6]
        srcs = [(r, True) for r in refs[6:6 + ns]]
        dq_ref, dk_ref, dv_ref = refs[6 + ns:9 + ns]
        dsts = refs[9 + ns:9 + 2 * ns]
        dk_sc, dv_sc = refs[9 + 2 * ns:11 + 2 * ns]
        comm = (srcs, dsts) + tuple(refs[11 + 2 * ns:])
        h, j = pl.program_id(0), pl.program_id(1)

        @pl.when((h == 0) & (j == 0))
        def _():
            _exchange_start(*comm)

        kv = k_ref[...]
        vv = v_ref[...]
        dk_sc[...] = jnp.zeros(dk_sc.shape, F32)
        dv_sc[...] = jnp.zeros(dv_sc.shape, F32)

        @pl.when(j == 0)
        def _():
            dq_ref[...] = jnp.zeros(dq_ref.shape, F32)

        def block(i, masked):
            off = pl.multiple_of(i * tq, tq)
            qb = q_ref[pl.ds(off, tq), :]
            dob = do_ref[pl.ds(off, tq), :]
            st = _dot_nt(kv, qb) * MLA_C2
            if masked:
                st = jnp.where(_tri_mask(tq, False), st, NEG)
            pt = jnp.exp2(st - lse_ref[i])
            dv_sc[...] += _dot(pt.astype(BF16), dob)
            dpt = _dot_nt(vv, dob)
            dst = (pt * (dpt - delta_ref[i]) * MLA_SCALE).astype(BF16)
            dk_sc[...] += _dot(dst, qb)
            dq_ref[pl.ds(off, tq), :] += _dot_tn(dst, kv)

        block(j, True)

        def loop_body(i, carry):
            block(i, False)
            return carry

        lax.fori_loop(j + 1, nq, loop_body, 0)
        dk_ref[...] = dk_sc[...].astype(dk_ref.dtype)
        dv_ref[...] = dv_sc[...].astype(dv_ref.dtype)

        @pl.when((h == MLA_H - 1) & (j == nq - 1))
        def _():
            _exchange_wait(*comm)

    blk = pl.BlockSpec((tq, HP), lambda h, j: (j, h))
    full = pl.BlockSpec((T, HP), lambda h, j: (0, h))
    rows = pl.BlockSpec((None, nq, 1, tq), lambda h, j: (h, 0, 0, 0))
    return pl.pallas_call(
        body,
        name="mla_bwd",
        grid=(MLA_H, nq),
        in_specs=[full, blk, blk, full, rows, rows] + [ANY] * ns,
        out_specs=[full, blk, blk] + [ANY] * ns,
        out_shape=[_sds((T, MLA_H * HP), F32), _sds((T, MLA_H * HP), BF16), _sds((T, MLA_H * HP), BF16)]
        + [_sds((N_DEV - 1,) + s.shape[1:], s.dtype) for s in slots],
        scratch_shapes=[pltpu.VMEM((tq, HP), F32), pltpu.VMEM((tq, HP), F32)] + _exchange_scratch(ns),
        compiler_params=_cp(2),
    )(q, k, v, do, lse_row, delta_row, *slots)


def _ret_consts():
    h = jnp.arange(RET_H, dtype=F32)
    log_g = jnp.log1p(-jnp.exp2(-5.0 - h))
    idx = jnp.arange(RET_C, dtype=F32)
    rel = idx[:, None] - idx[None, :]
    dmask = jnp.where(rel >= 0, jnp.exp(log_g[:, None, None] * jnp.maximum(rel, 0.0)), 0.0)
    zeta = jnp.exp(log_g[:, None] * (RET_C - 1.0 - idx)[None, :])
    xi = jnp.exp(log_g[:, None] * (idx + 1.0)[None, :])
    decay = jnp.exp(log_g * RET_C)
    zb = jnp.broadcast_to(zeta[:, :, None], (RET_H, RET_C, RET_D))
    xb = jnp.broadcast_to(xi[:, :, None], (RET_H, RET_C, RET_D))
    db = jnp.broadcast_to(decay[:, None, None], (RET_H, RET_C, RET_D))
    return dmask.astype(F32), zb.astype(F32), xb.astype(F32), db.astype(F32)


def ret_fwd(rq, rk, rv, consts, *, rb):
    T = rq.shape[0]
    nb = T // rb
    ncb = rb // RET_C

    def body(q_ref, k_ref, v_ref, dm_ref, z_ref, x_ref, dc_ref, o_ref, st_ref, r_sc):
        @pl.when(pl.program_id(1) == 0)
        def _():
            r_sc[...] = jnp.zeros(r_sc.shape, F32)

        dm, zt, xi, dc = dm_ref[...], z_ref[...], x_ref[...], dc_ref[...]
        for c in range(ncb):
            sl = slice(c * RET_C, (c + 1) * RET_C)
            q, k, v = q_ref[sl, :], k_ref[sl, :], v_ref[sl, :]
            r = r_sc[...]
            rbf = r.astype(BF16)
            st_ref[sl, :] = rbf
            s = _dot_nt(q, k) * dm
            inner = _dot(s.astype(BF16), v)
            cross = _dot((q.astype(F32) * xi).astype(BF16), rbf)
            o_ref[sl, :] = inner + cross
            kz = (k.astype(F32) * zt).T.astype(BF16)
            r_sc[...] = r * dc + _dot(kz, v)

    blk = pl.BlockSpec((rb, RET_D), lambda h, b: (b, h))
    cst = pl.BlockSpec((None, RET_C, RET_D), lambda h, b: (h, 0, 0))
    return pl.pallas_call(
        body,
        name="ret_fwd",
        grid=(RET_H, nb),
        in_specs=[blk, blk, blk, cst, cst, cst, cst],
        out_specs=[blk, blk],
        out_shape=[_sds((T, RET_H * RET_D), F32), _sds((T, RET_H * RET_D), BF16)],
        scratch_shapes=[pltpu.VMEM((RET_D, RET_D), F32)],
        compiler_params=_cp(2),
    )(rq, rk, rv, *consts)


def ret_bwd(rq, rk, rv, st, dret, consts, *, rb):
    T = rq.shape[0]
    nb = T // rb
    ncb = rb // RET_C

    def body(q_ref, k_ref, v_ref, st_ref, do_ref, dm_ref, z_ref, x_ref, dc_ref, dq_ref, dk_ref, dv_ref, g_sc):
        @pl.when(pl.program_id(1) == 0)
        def _():
            g_sc[...] = jnp.zeros(g_sc.shape, F32)

        dm, zt, xi, dc = dm_ref[...], z_ref[...], x_ref[...], dc_ref[...]
        for c in reversed(range(ncb)):
            sl = slice(c * RET_C, (c + 1) * RET_C)
            q, k, v, rp = q_ref[sl, :], k_ref[sl, :], v_ref[sl, :], st_ref[sl, :]
            dob = do_ref[sl, :].astype(BF16)
            qf, kf = q.astype(F32), k.astype(F32)
            gn = g_sc[...]
            gnb = gn.astype(BF16)
            s = _dot_nt(q, k) * dm
            ds = _dot_nt(dob, v) * dm
            dq = _dot(ds.astype(BF16), k) + _dot_nt(dob, rp) * xi
            dk = _dot(ds.T.astype(BF16), q) + _dot_nt(v, gnb) * zt
            dv = _dot(s.T.astype(BF16), dob) + _dot((kf * zt).astype(BF16), gnb)
            dq_ref[sl, :] = dq.astype(dq_ref.dtype)
            dk_ref[sl, :] = dk.astype(dk_ref.dtype)
            dv_ref[sl, :] = dv.astype(dv_ref.dtype)
            g_sc[...] = _dot((qf * xi).T.astype(BF16), dob) + dc * gn

    blk = pl.BlockSpec((rb, RET_D), lambda h, b: (nb - 1 - b, h))
    cst = pl.BlockSpec((None, RET_C, RET_D), lambda h, b: (h, 0, 0))
    return pl.pallas_call(
        body,
        name="ret_bwd",
        grid=(RET_H, nb),
        in_specs=[blk, blk, blk, blk, blk, cst, cst, cst, cst],
        out_specs=[blk, blk, blk],
        out_shape=[_sds((T, RET_H * RET_D), F32)] * 3,
        scratch_shapes=[pltpu.VMEM((RET_D, RET_D), F32)],
        compiler_params=_cp(2),
    )(rq, rk, rv, st, dret, *consts)


HALO = 16


def conv_act_fwd(up_pre, w_conv, b_conv, *, tile, cw):
    T = up_pre.shape[0]
    nt = T // tile
    ncol = D_FF // cw
    hb = tile // HALO

    def body(pa_ref, a_ref, pb_ref, b_ref, wa_ref, wb_ref, ba_ref, bb_ref, o_ref):
        i = pl.program_id(1)
        keep = (i > 0).astype(F32)

        def conv(prev_ref, cur_ref, w_ref, bias_ref):
            ext = jnp.concatenate([prev_ref[...].astype(F32) * keep, cur_ref[...].astype(F32)], axis=0)
            w = w_ref[...]
            y = ext * w[2:3, :] + pltpu.roll(ext, 1, 0) * w[1:2, :] + pltpu.roll(ext, 2, 0) * w[0:1, :] + bias_ref[...]
            return y[HALO:, :]

        a = conv(pa_ref, a_ref, wa_ref, ba_ref)
        b = conv(pb_ref, b_ref, wb_ref, bb_ref)
        o_ref[...] = (a * _sigmoid(a) * b).astype(o_ref.dtype)

    prev_a = pl.BlockSpec((HALO, cw), lambda j, i: (jnp.maximum(i * hb - 1, 0), j))
    cur_a = pl.BlockSpec((tile, cw), lambda j, i: (i, j))
    prev_b = pl.BlockSpec((HALO, cw), lambda j, i: (jnp.maximum(i * hb - 1, 0), j + ncol))
    cur_b = pl.BlockSpec((tile, cw), lambda j, i: (i, j + ncol))
    w_a = pl.BlockSpec((3, cw), lambda j, i: (0, j))
    w_b = pl.BlockSpec((3, cw), lambda j, i: (0, j + ncol))
    bias_a = pl.BlockSpec((1, cw), lambda j, i: (0, j))
    bias_b = pl.BlockSpec((1, cw), lambda j, i: (0, j + ncol))
    return pl.pallas_call(
        body,
        name="conv_act_fwd",
        grid=(ncol, nt),
        in_specs=[prev_a, cur_a, prev_b, cur_b, w_a, w_b, bias_a, bias_b],
        out_specs=pl.BlockSpec((tile, cw), lambda j, i: (i, j)),
        out_shape=_sds((T, D_FF), BF16),
        compiler_params=_cp(2),
    )(up_pre, up_pre, up_pre, up_pre, w_conv, w_conv, b_conv, b_conv)


def conv_act_bwd(up_pre, dact, w_conv, b_conv, *, tile, cw):
    T = up_pre.shape[0]
    nt = T // tile
    ncol = D_FF // cw
    hb = tile // HALO
    ext_rows = tile + 2 * HALO

    def body(pa_ref, a_ref, na_ref, pb_ref, b_ref, nb_ref, d_ref, nd_ref, wa_ref, wb_ref, ba_ref, bb_ref,
             dxa_ref, dxb_ref, sa_ref, sb_ref):
        i = pl.program_id(1)
        keep_p = (i > 0).astype(F32)
        keep_n = (i < nt - 1).astype(F32)

        def ext_of(prev_ref, cur_ref, next_ref):
            return jnp.concatenate(
                [prev_ref[...].astype(F32) * keep_p, cur_ref[...].astype(F32), next_ref[...].astype(F32) * keep_n], axis=0)

        def conv(ext, w, bias):
            return ext * w[2:3, :] + pltpu.roll(ext, 1, 0) * w[1:2, :] + pltpu.roll(ext, 2, 0) * w[0:1, :] + bias

        xa = ext_of(pa_ref, a_ref, na_ref)
        xb = ext_of(pb_ref, b_ref, nb_ref)
        wa, wb = wa_ref[...], wb_ref[...]
        a = conv(xa, wa, ba_ref[...])
        b = conv(xb, wb, bb_ref[...])
        dy = jnp.concatenate(
            [jnp.zeros((HALO, cw), F32), d_ref[...].astype(F32), nd_ref[...].astype(F32) * keep_n], axis=0)
        sg = _sigmoid(a)
        da = dy * b * (sg * (1.0 + a * (1.0 - sg)))
        db = dy * (a * sg)

        def back(dup, ext, w, dx_ref, s_ref):
            dx = dup * w[2:3, :] + pltpu.roll(dup, ext_rows - 1, 0) * w[1:2, :] + pltpu.roll(dup, ext_rows - 2, 0) * w[0:1, :]
            dx_ref[...] = dx[HALO:HALO + tile, :].astype(dx_ref.dtype)
            dc = dup[HALO:HALO + tile, :]
            r2 = jnp.sum(dc * ext[HALO:HALO + tile, :], axis=0, keepdims=True)
            r1 = jnp.sum(dc * pltpu.roll(ext, 1, 0)[HALO:HALO + tile, :], axis=0, keepdims=True)
            r0 = jnp.sum(dc * pltpu.roll(ext, 2, 0)[HALO:HALO + tile, :], axis=0, keepdims=True)
            rb = jnp.sum(dc, axis=0, keepdims=True)
            row = lax.broadcasted_iota(jnp.int32, (8, cw), 0)
            upd = (jnp.where(row == 0, r0, 0.0) + jnp.where(row == 1, r1, 0.0) + jnp.where(row == 2, r2, 0.0)
                   + jnp.where(row == 3, rb, 0.0))

            @pl.when(i == 0)
            def _():
                s_ref[...] = upd

            @pl.when(i > 0)
            def _():
                s_ref[...] += upd

        back(da, xa, wa, dxa_ref, sa_ref)
        back(db, xb, wb, dxb_ref, sb_ref)

    def prev_of(shift):
        return pl.BlockSpec((HALO, cw), lambda j, i: (jnp.maximum(i * hb - 1, 0), j + shift))

    def next_of(shift):
        return pl.BlockSpec((HALO, cw), lambda j, i: (jnp.minimum((i + 1) * hb, nt * hb - 1), j + shift))

    def cur_of(shift):
        return pl.BlockSpec((tile, cw), lambda j, i: (i, j + shift))

    def row_of(rows, shift):
        return pl.BlockSpec((rows, cw), lambda j, i: (0, j + shift))

    return pl.pallas_call(
        body,
        name="conv_act_bwd",
        grid=(ncol, nt),
        in_specs=[prev_of(0), cur_of(0), next_of(0), prev_of(ncol), cur_of(ncol), next_of(ncol), cur_of(0), next_of(0),
                  row_of(3, 0), row_of(3, ncol), row_of(1, 0), row_of(1, ncol)],
        out_specs=[cur_of(0), cur_of(0), row_of(8, 0), row_of(8, 0)],
        out_shape=[_sds((T, D_FF), BF16), _sds((T, D_FF), BF16), _sds((8, D_FF), F32), _sds((8, D_FF), F32)],
        compiler_params=_cp(2),
    )(up_pre, up_pre, up_pre, up_pre, up_pre, up_pre, dact, dact, w_conv, w_conv, b_conv, b_conv)


NS_IN, NS_UP, NS_XKV, NS_UQ = 564, 704, 256, 96
E2_ROWS, E2_UQ = 384, 128
L1_W = NS_UP + NS_XKV
L2_ROWS = 736
L2_OUT, L2_XQ, L2_XO, L2_DN = 0, 128, 256, 384
RS_DN = 352
L3_ROWS, L3_PRET = 1024, 512
L4_SHAPE = (8, 768)
WZ_RUNS = ((416, 4096, 0), (0, 384, 4096), (384, 32, 4544))
WZ_ZERO = ((4480, 4544), (4576, 4608))


def _pieces(orig_start, length, dst_start, ns):
    out, c, d, end = [], orig_start, dst_start, orig_start + length
    while c < end:
        j, off = c // ns, c % ns
        ln = min(ns - off, end - c)
        out.append((j, off, d, ln))
        c += ln
        d += ln
    return out


def pack_early(w_in, w_ukv, w_uq):
    def body(in_ref, ukv_ref, uq_ref, e1_ref, e2_ref):
        e1_ref[...] = in_ref[...].astype(BF16)
        e2_ref[0:E2_UQ, :] = ukv_ref[...].astype(BF16)
        e2_ref[E2_UQ:E2_ROWS, 0:NS_UQ] = uq_ref[...].astype(BF16)
        e2_ref[E2_UQ:E2_ROWS, NS_UQ:HP] = jnp.zeros((E2_ROWS - E2_UQ, HP - NS_UQ), BF16)

    return pl.pallas_call(
        body,
        name="pack_early",
        in_specs=[VM] * 3,
        out_specs=[VM] * 2,
        out_shape=[_sds((D, NS_IN), BF16), _sds((E2_ROWS, HP), BF16)],
        compiler_params=_cp0(),
    )(w_in, w_ukv, w_uq)


def pack_late(w_up, w_xkv, w_out, w_xq, w_xo, w_down, w_pmla, w_pret, w_conv):
    def body(up_ref, xkv_ref, o_ref, xq_ref, xo_ref, dn_ref, pm_ref, pr_ref, cv_ref, l1_ref, l2_ref, l3_ref, l4_ref):
        l1_ref[:, 0:NS_UP] = up_ref[...].astype(BF16)
        l1_ref[:, NS_UP:L1_W] = xkv_ref[...].astype(BF16)
        l2_ref[L2_OUT:L2_XQ, :] = o_ref[...].astype(BF16)
        l2_ref[L2_XQ:L2_XO, :] = xq_ref[...].astype(BF16)
        l2_ref[L2_XO:L2_DN, :] = xo_ref[...].astype(BF16)
        l2_ref[L2_DN:L2_ROWS, :] = dn_ref[...].astype(BF16)
        l3_ref[0:L3_PRET, :] = pm_ref[...].astype(BF16)
        l3_ref[L3_PRET:L3_ROWS, :] = pr_ref[...].astype(BF16)
        l4_ref[...] = jnp.zeros(L4_SHAPE, F32)
        l4_ref[0:3, 0:NS_UP] = cv_ref[...]

    return pl.pallas_call(
        body,
        name="pack_late",
        in_specs=[VM] * 9,
        out_specs=[VM] * 4,
        out_shape=[_sds((D, L1_W), BF16), _sds((L2_ROWS, D), BF16), _sds((L3_ROWS, HP), BF16), _sds(L4_SHAPE, F32)],
        compiler_params=_cp0(),
    )(w_up, w_xkv, w_out, w_xq, w_xo, w_down, w_pmla, w_pret, w_conv)


def assemble_early(g1, g2, *, tile):
    def body(g1_ref, g2_ref, wz_ref, wk_ref, wv_ref, wq_ref):
        for lo, hi in WZ_ZERO:
            wz_ref[:, lo:hi] = jnp.zeros((tile, hi - lo), BF16)
        for os_, ln_, ds_ in WZ_RUNS:
            for j, off, d, ln in _pieces(os_, ln_, ds_, NS_IN):
                wz_ref[:, d:d + ln] = g1_ref[j, :, off:off + ln]

        @pl.when(pl.program_id(0) == 0)
        def _():
            half = jnp.zeros((MLA_KVR, HP - MLA_NOPE), BF16)
            for j in range(N_DEV):
                wk_ref[:, j * HP:j * HP + MLA_NOPE] = g2_ref[j, 0:E2_UQ, 0:MLA_NOPE]
                wk_ref[:, j * HP + MLA_NOPE:(j + 1) * HP] = half
                wv_ref[:, j * HP:j * HP + MLA_V] = g2_ref[j, 0:E2_UQ, MLA_NOPE:HP]
                wv_ref[:, j * HP + MLA_V:(j + 1) * HP] = half
                wq_ref[:, j * HP:j * HP + NS_UQ] = g2_ref[j, E2_UQ:E2_ROWS, 0:NS_UQ]
                wq_ref[:, j * HP + NS_UQ:(j + 1) * HP] = jnp.zeros((MLA_QR, HP - NS_UQ), BF16)

    def whole(r):
        return pl.BlockSpec((r, MLA_H * HP), lambda i: (0, 0))

    return pl.pallas_call(
        body,
        name="assemble_early",
        grid=(D // tile,),
        in_specs=[pl.BlockSpec((N_DEV, tile, NS_IN), lambda i: (0, i, 0)),
                  pl.BlockSpec((N_DEV, E2_ROWS, HP), lambda i: (0, 0, 0))],
        out_specs=[pl.BlockSpec((tile, ZW), lambda i: (i, 0)), whole(MLA_KVR), whole(MLA_KVR), whole(MLA_QR)],
        out_shape=[_sds((D, ZW), BF16), _sds((MLA_KVR, MLA_H * HP), BF16), _sds((MLA_KVR, MLA_H * HP), BF16),
                   _sds((MLA_QR, MLA_H * HP), BF16)],
        compiler_params=_cp(1),
    )(g1, g2)


def assemble_l1(g1, *, tile):
    def body(g_ref, wup_ref, wxkv_ref):
        for j in range(N_DEV):
            wup_ref[:, j * NS_UP:(j + 1) * NS_UP] = g_ref[j, :, 0:NS_UP]
            wxkv_ref[:, j * NS_XKV:(j + 1) * NS_XKV] = g_ref[j, :, NS_UP:L1_W]

    return pl.pallas_call(
        body,
        name="assemble_l1",
        grid=(D // tile,),
        in_specs=[pl.BlockSpec((N_DEV, tile, L1_W), lambda i: (0, i, 0))],
        out_specs=[pl.BlockSpec((tile, 2 * D_FF), lambda i: (i, 0)), pl.BlockSpec((tile, 2 * D), lambda i: (i, 0))],
        out_shape=[_sds((D, 2 * D_FF), BF16), _sds((D, 2 * D), BF16)],
        compiler_params=_cp(1),
    )(g1)


def assemble_l234(g2, g3, g4):
    def body(g2_ref, g3_ref, g4_ref, wo_ref, wxq_ref, wxo_ref, wdn_ref, wpa_ref, wpr_ref, wc_ref):
        for j in range(N_DEV):
            wo_ref[j * 128:(j + 1) * 128, :] = g2_ref[j, L2_OUT:L2_XQ, :]
            wxq_ref[j * 128:(j + 1) * 128, :] = g2_ref[j, L2_XQ:L2_XO, :]
            wxo_ref[j * 128:(j + 1) * 128, :] = g2_ref[j, L2_XO:L2_DN, :]
            wdn_ref[j * RS_DN:(j + 1) * RS_DN, :] = g2_ref[j, L2_DN:L2_ROWS, :]
            wpr_ref[:, j * 128:(j + 1) * 128] = g3_ref[j, L3_PRET:L3_ROWS, :]
            wc_ref[:, j * NS_UP:(j + 1) * NS_UP] = g4_ref[j, 0:3, 0:NS_UP]
            for h in range(MLA_H):
                wpa_ref[h * HP:h * HP + MLA_V, j * 128:(j + 1) * 128] = g3_ref[j, h * MLA_V:(h + 1) * MLA_V, :]
        for h in range(MLA_H):
            wpa_ref[h * HP + MLA_V:(h + 1) * HP, :] = jnp.zeros((HP - MLA_V, D), BF16)

    return pl.pallas_call(
        body,
        name="assemble_l234",
        in_specs=[VM] * 3,
        out_specs=[VM] * 7,
        out_shape=[_sds((D, D), BF16), _sds((D, D), BF16), _sds((D, D), BF16), _sds((D_FF, D), BF16),
                   _sds((MLA_H * HP, D), BF16), _sds((RET_H * RET_D, D), BF16), _sds((3, 2 * D_FF), F32)],
        compiler_params=_cp0(),
    )(g2, g3, g4)


def slots_l1(dwup_a, dwup_b, dwxkv, *, tile):
    per_half = D_FF // NS_UP

    def body(ua_ref, ub_ref, x_ref, s_ref):
        for j in range(N_DEV):
            src = ua_ref if j < per_half else ub_ref
            c0 = (j % per_half) * NS_UP
            s_ref[j, :, 0:NS_UP] = src[:, c0:c0 + NS_UP].astype(BF16)
            s_ref[j, :, NS_UP:L1_W] = x_ref[:, j * NS_XKV:(j + 1) * NS_XKV].astype(BF16)

    return pl.pallas_call(
        body,
        name="slots_l1",
        grid=(D // tile,),
        in_specs=[pl.BlockSpec((tile, D_FF), lambda i: (i, 0)), pl.BlockSpec((tile, D_FF), lambda i: (i, 0)),
                  pl.BlockSpec((tile, 2 * D), lambda i: (i, 0))],
        out_specs=pl.BlockSpec((N_DEV, tile, L1_W), lambda i: (0, i, 0)),
        out_shape=_sds((N_DEV, D, L1_W), BF16),
        compiler_params=_cp(1),
    )(dwup_a, dwup_b, dwxkv)


def slots_l23(dwo, dwxq, dwxo, dwdn, dwpa, dwpr):
    def body(o_ref, xq_ref, xo_ref, dn_ref, pa_ref, pr_ref, l2_ref, l3_ref):
        l2_ref[L2_OUT:L2_XQ, :] = o_ref[...].astype(BF16)
        l2_ref[L2_XQ:L2_XO, :] = xq_ref[...].astype(BF16)
        l2_ref[L2_XO:L2_DN, :] = xo_ref[...].astype(BF16)
        l2_ref[L2_DN:L2_ROWS, :] = dn_ref[...].astype(BF16)
        for h in range(MLA_H):
            l3_ref[h * MLA_V:(h + 1) * MLA_V, :] = pa_ref[h * HP:h * HP + MLA_V, :].astype(BF16)
        l3_ref[L3_PRET:L3_ROWS, :] = pr_ref[...].astype(BF16)

    rows128 = pl.BlockSpec((128, D), lambda j: (j, 0))

    def cols(r):
        return pl.BlockSpec((r, 128), lambda j: (0, j))

    return pl.pallas_call(
        body,
        name="slots_l23",
        grid=(N_DEV,),
        in_specs=[rows128, rows128, rows128, pl.BlockSpec((RS_DN, D), lambda j: (j, 0)),
                  cols(MLA_H * HP), cols(RET_H * RET_D)],
        out_specs=[pl.BlockSpec((None, L2_ROWS, D), lambda j: (j, 0, 0)),
                   pl.BlockSpec((None, L3_ROWS, HP), lambda j: (j, 0, 0))],
        out_shape=[_sds((N_DEV, L2_ROWS, D), BF16), _sds((N_DEV, L3_ROWS, HP), BF16)],
        compiler_params=_cp(1),
    )(dwo, dwxq, dwxo, dwdn, dwpa, dwpr)


def slots_early(dwz, dwk, dwv, dwq, *, tile):
    def body(dz_ref, k_ref, v_ref, q_ref, s1_ref, s2_ref):
        for os_, ln_, ds_ in WZ_RUNS:
            for j, off, d, ln in _pieces(os_, ln_, ds_, NS_IN):
                s1_ref[j, :, off:off + ln] = dz_ref[:, d:d + ln].astype(BF16)

        @pl.when(pl.program_id(0) == 0)
        def _():
            for j in range(N_DEV):
                s2_ref[j, 0:E2_UQ, 0:MLA_NOPE] = k_ref[:, j * HP:j * HP + MLA_NOPE].astype(BF16)
                s2_ref[j, 0:E2_UQ, MLA_NOPE:HP] = v_ref[:, j * HP:j * HP + MLA_V].astype(BF16)
                s2_ref[j, E2_UQ:E2_ROWS, 0:NS_UQ] = q_ref[:, j * HP:j * HP + NS_UQ].astype(BF16)
                s2_ref[j, E2_UQ:E2_ROWS, NS_UQ:HP] = jnp.zeros((E2_ROWS - E2_UQ, HP - NS_UQ), BF16)

    def whole(r):
        return pl.BlockSpec((r, MLA_H * HP), lambda i: (0, 0))

    return pl.pallas_call(
        body,
        name="slots_early",
        grid=(D // tile,),
        in_specs=[pl.BlockSpec((tile, ZW), lambda i: (i, 0)), whole(MLA_KVR), whole(MLA_KVR), whole(MLA_QR)],
        out_specs=[pl.BlockSpec((N_DEV, tile, NS_IN), lambda i: (0, i, 0)),
                   pl.BlockSpec((N_DEV, E2_ROWS, HP), lambda i: (0, 0, 0))],
        out_shape=[_sds((N_DEV, D, NS_IN), BF16), _sds((N_DEV, E2_ROWS, HP), BF16)],
        compiler_params=_cp(1),
    )(dwz, dwk, dwv, dwq)


SMALL = (("g_mix", 1024, 0), ("b_gate", 2048, 1), ("g_q_lat", 256, 3), ("g_kv_lat", 128, 4), ("g_ret", 512, 5),
         ("g_cross", 1024, 6), ("g_mem", 1024, 7), ("g_ffn", 1024, 8), ("b_conv", 5632, 9), ("g_final", 1024, 15))
SMALL_DIRECT = tuple(s for s in SMALL if s[0] != "b_conv")
S_ROWS = 16


def _flat_pieces(n, row0):
    return [(row0 + k // D, k, min(D, n - k)) for k in range(0, n, D)]


def exchange_small(cs_a, cs_b, smalls):
    ns = len(smalls)

    def body(*refs):
        ca_ref, cb_ref = refs[:2]
        small_refs = refs[2:2 + ns]
        rd_ref, rs_ref, dsl_ref, own_ref = refs[2 + ns:6 + ns]
        sems = refs[6 + ns:]
        own_ref[...] = jnp.zeros(own_ref.shape, F32)
        for (name, n, row0), g_ref in zip(SMALL_DIRECT, small_refs):
            for r, c0, ln in _flat_pieces(n, row0):
                own_ref[r:r + 1, 0:ln] = g_ref[:, c0:c0 + ln]
        row0 = dict((s[0], s[2]) for s in SMALL)["b_conv"]
        for half, c_ref in enumerate((ca_ref, cb_ref)):
            k = half * D_FF
            end = k + D_FF
            while k < end:
                r, lane = row0 + k // D, k % D
                ln = min(D - lane, end - k)
                own_ref[r:r + 1, lane:lane + ln] = c_ref[3:4, k - half * D_FF:k - half * D_FF + ln]
                k += ln
        dsl_ref[...] = jnp.zeros(dsl_ref.shape, F32)
        per_half = D_FF // NS_UP
        for j in range(N_DEV):
            c_ref = ca_ref if j < per_half else cb_ref
            c0 = (j % per_half) * NS_UP
            dsl_ref[j, 0:3, 0:NS_UP] = c_ref[0:3, c0:c0 + NS_UP]
        comm = ([(dsl_ref, True), (own_ref, False)], [rd_ref, rs_ref]) + tuple(sems)
        _exchange_start(*comm)
        _exchange_wait(*comm)

    n1 = N_DEV - 1
    return pl.pallas_call(
        body,
        name="exchange_small",
        in_specs=[VM] * (2 + ns),
        out_specs=[VM, VM, VM, VM],
        out_shape=[_sds((n1,) + L4_SHAPE, F32), _sds((n1, S_ROWS, D), F32), _sds((N_DEV,) + L4_SHAPE, F32),
                   _sds((S_ROWS, D), F32)],
        scratch_shapes=_exchange_scratch(2),
        compiler_params=_cp0(),
    )(cs_a, cs_b, *smalls)


def _adamw(w, g, m, v):
    m = ADAM_B1 * m + (1.0 - ADAM_B1) * g
    v = ADAM_B2 * v + (1.0 - ADAM_B2) * (g * g)
    m_hat = m / (1.0 - ADAM_B1 ** ADAM_STEP)
    v_hat = v / (1.0 - ADAM_B2 ** ADAM_STEP)
    delta = -ADAM_LR * (m_hat / (jnp.sqrt(v_hat) + ADAM_EPS) + ADAM_WD * w)
    return delta, m, v


def _apply(g, refs, outs):
    d, mn, vn = _adamw(refs[0][...], g, refs[1][...], refs[2][...])
    outs[0][...] = g
    outs[1][...] = d
    outs[2][...] = mn
    outs[3][...] = vn


def adam_cols(own, recv, wmv, spans, *, name, tile):
    R, W = own.shape
    nw = len(spans)

    def body(*refs):
        own_ref, recv_ref = refs[:2]
        ins, outs = refs[2:2 + 3 * nw], refs[2 + 3 * nw:]
        g = own_ref[...].astype(F32)
        for k in range(N_DEV - 1):
            g = g + recv_ref[k].astype(F32)
        for t, (lo, hi) in enumerate(spans):
            _apply(g[:, lo:hi], ins[3 * t:3 * t + 3], outs[4 * t:4 * t + 4])

    def blk(w):
        return pl.BlockSpec((tile, w), lambda i: (i, 0))

    widths = [hi - lo for lo, hi in spans]
    return pl.pallas_call(
        body,
        name=name,
        grid=(R // tile,),
        in_specs=[blk(W), pl.BlockSpec((N_DEV - 1, tile, W), lambda i: (0, i, 0))] + [blk(w) for w in widths for _ in range(3)],
        out_specs=[blk(w) for w in widths for _ in range(4)],
        out_shape=[_sds((R, w), F32) for w in widths for _ in range(4)],
        compiler_params=_cp(1),
    )(own, recv, *wmv)


def adam_rows(own, recv, wmv, spans, *, name):
    nw = len(spans)

    def body(*refs):
        own_ref, recv_ref = refs[:2]
        ins, outs = refs[2:2 + 3 * nw], refs[2 + 3 * nw:]
        for t, (lo, hi, w) in enumerate(spans):
            g = own_ref[lo:hi, :].astype(F32)
            for k in range(N_DEV - 1):
                g = g + recv_ref[k, lo:hi, :].astype(F32)
            _apply(g[:, 0:w], ins[3 * t:3 * t + 3], outs[4 * t:4 * t + 4])

    return pl.pallas_call(
        body,
        name=name,
        in_specs=[VM] * (2 + 3 * nw),
        out_specs=[VM] * (4 * nw),
        out_shape=[_sds((hi - lo, w), F32) for lo, hi, w in spans for _ in range(4)],
        compiler_params=_cp0(),
    )(own, recv, *wmv)


def adam_small(own_s, recv_s, dslots, recv_d, wmv_small, wmv_conv):
    ns = len(SMALL)

    def body(*refs):
        own_ref, rs_ref, dsl_ref, rd_ref = refs[:4]
        ins = refs[4:4 + 3 * ns + 3]
        outs = refs[4 + 3 * ns + 3:4 + 3 * ns + 3 + 4 * ns + 4]
        all_sc = refs[-1]
        me = _my_id()
        all_sc[0] = own_ref[...]
        for k in range(N_DEV - 1):
            all_sc[k + 1] = rs_ref[k]
        g = all_sc[jnp.bitwise_xor(me, 0)]
        for s in range(1, N_DEV):
            g = g + all_sc[jnp.bitwise_xor(me, s)]
        all_sc[0] = g
        for t, (name, n, row0) in enumerate(SMALL):
            pieces = [all_sc[0, r:r + 1, 0:ln] for r, _, ln in _flat_pieces(n, row0)]
            gt = pieces[0] if len(pieces) == 1 else jnp.concatenate(pieces, axis=1)
            _apply(gt, ins[3 * t:3 * t + 3], outs[4 * t:4 * t + 4])
        gc = dsl_ref[me]
        for k in range(N_DEV - 1):
            gc = gc + rd_ref[k]
        _apply(gc[0:3, 0:NS_UP], ins[3 * ns:3 * ns + 3], outs[4 * ns:4 * ns + 4])

    out_shape = [_sds((1, n), F32) for _, n, _ in SMALL for _ in range(4)] + [_sds((3, NS_UP), F32)] * 4
    return pl.pallas_call(
        body,
        name="adam_small",
        in_specs=[VM] * (4 + 3 * ns + 3),
        out_specs=[VM] * len(out_shape),
        out_shape=out_shape,
        scratch_shapes=[pltpu.VMEM((N_DEV, S_ROWS, D), F32)],
    )(own_s, recv_s, dslots, recv_d, *wmv_small, *wmv_conv)


def kernel(x, mem, positions, g_mix, w_in, b_gate, g_q_lat, w_uq, g_kv_lat, w_ukv, w_proj_mla, g_ret, w_proj_ret, w_out, g_cross, g_mem, w_xq, w_xkv, w_xo, g_ffn, w_up, w_conv, b_conv, w_down, g_final, loss_target, m_g_mix, m_w_in, m_b_gate, m_g_q_lat, m_w_uq, m_g_kv_lat, m_w_ukv, m_w_proj_mla, m_g_ret, m_w_proj_ret, m_w_out, m_g_cross, m_g_mem, m_w_xq, m_w_xkv, m_w_xo, m_g_ffn, m_w_up, m_w_conv, m_b_conv, m_w_down, m_g_final, v_g_mix, v_w_in, v_b_gate, v_g_q_lat, v_w_uq, v_g_kv_lat, v_w_ukv, v_w_proj_mla, v_g_ret, v_w_proj_ret, v_w_out, v_g_cross, v_g_mem, v_w_xq, v_w_xkv, v_w_xo, v_g_ffn, v_w_up, v_w_conv, v_b_conv, v_w_down, v_g_final):
    args = dict(locals())
    T = x.shape[1]
    M = mem.shape[1]
    tile = min(256, T)
    tile2 = min(512, T)
    tile4 = min(1024, T)
    tq = min(512, T)
    rb = min(1024, T)

    xs = x[0]
    tgt = loss_target[0]
    mems = mem[0]

    def shard(name, prefix=""):
        a = args[prefix + name]
        return a.reshape(a.shape[-2:]) if a.ndim >= 2 else a.reshape(1, -1)

    e1, e2 = pack_early(shard("w_in"), shard("w_ukv"), shard("w_uq"))
    late_parts = pack_late(shard("w_up"), shard("w_xkv"), shard("w_out"), shard("w_xq"), shard("w_xo"), shard("w_down"),
                           shard("w_proj_mla"), shard("w_proj_ret"), shard("w_conv"))

    pos_f = jnp.broadcast_to(positions[0].astype(F32)[:, None], (T, 128))
    inv_m, inv_r = _rot_inv()
    u, cm, s1, s2, cr, sr, ge1, ge2 = rowwise(
        lambda xv, p, im, ir, g: (_rms(xv, g),) + _rot_tables(p, im, ir), [(xs, None), (pos_f, None)],
        [inv_m, inv_r, g_mix], [(D, BF16)] + [(128, F32)] * 5, [], tile=tile2, name="norm_mix_tables", gather=[e1, e2])
    wz, wk, wv, wq = assemble_early(ge1, ge2, tile=tile)
    rconsts = _ret_consts()

    z = matmul(u, wz, name="mm_z", tn=1536)

    def lat_fwd(zl, cmv, s1v, s2v, gq, gkv, wqv, wkv_, wvv):
        cq = _rms(zl[:, 0:256], gq).astype(BF16)
        ckv = _rms(zl[:, 256:384], gkv).astype(BF16)
        qv = _rot_mla(_dot(cq, wqv), cmv, s1v, s2v)
        kr = _rot_mla(zl[:, 384:512], cmv, s1v, s2v)
        kn = _dot(ckv, wkv_)
        kv_ = jnp.concatenate([kn[:, h * HP:(h + 1) * HP] + kr for h in range(MLA_H)], axis=1)
        vv = _dot(ckv, wvv)
        lane = lax.broadcasted_iota(jnp.int32, vv.shape, 1)
        vv = jnp.where((lane & (HP - 1)) == ONE_LANE, 1.0, vv)
        return qv, kv_, vv

    q_a, k_a, v_a = rowwise(
        lat_fwd, [(z, (512, 8)), (cm, None), (s1, None), (s2, None)], [g_q_lat, g_kv_lat, wq, wk, wv],
        [(MLA_H * HP, BF16)] * 3, [], tile=tile2, name="lat_fwd")

    def retprep_fwd(zr, crv, srv):
        rqv = _rot_ret(zr[:, 0:512], crv, srv)
        rkv = _rot_ret(zr[:, 512:1024], crv, srv) * (RET_D ** -0.5)
        return rqv, rkv, zr[:, 1024:1536]

    rq, rk, rv = rowwise(retprep_fwd, [(z, (2048, 0)), (cr, None), (sr, None)], [], [(512, BF16)] * 3, [],
                         tile=tile2, name="retprep_fwd")

    o_a, lse, gl1, gl2, gl3, gl4 = mla_fwd(q_a, k_a, v_a, list(late_parts), tq=tq)
    wup, wxkv = assemble_l1(gl1, tile=tile)
    wo, wxq, wxo, wdn, wpa, wpr, wcv = assemble_l234(gl2, gl3, gl4)
    ret, rstate = ret_fwd(rq, rk, rv, rconsts, rb=rb)

    def gn_parts(r):
        outs = []
        for h in range(RET_H):
            rh = r[:, h * RET_D:(h + 1) * RET_D]
            mu = jnp.mean(rh, axis=-1, keepdims=True)
            dlt = rh - mu
            rstd = lax.rsqrt(jnp.mean(dlt * dlt, axis=-1, keepdims=True) + EPS)
            outs.append((dlt * rstd, rstd))
        return outs

    def mix_fwd(ov, rv_, rg, gt, wpav, wprv, gr, bg):
        ya = _dot(ov, wpav)
        xh = jnp.concatenate([p[0] for p in gn_parts(rv_)], axis=1)
        t = rg * _sigmoid(rg) * (xh * gr)
        yr = _dot(t.astype(BF16), wprv)
        ga_ = _sigmoid(gt[:, :D] + bg[:, :D])
        gr_ = _sigmoid(gt[:, D:] + bg[:, D:])
        return ga_ * ya + gr_ * yr

    mix = rowwise(mix_fwd, [(o_a, None), (ret, None), (z, (512, 3)), (z, (2048, 1))], [wpa, wpr, g_ret, b_gate],
                  [(D, BF16)], [], tile=tile, name="mix_fwd")[0]
    def proj_norm(av, rv_, wv_, g):
        hv = rv_ + _dot(av, wv_)
        return hv, _rms(hv, g)

    h1, n2 = rowwise(proj_norm, [(mix, None), (xs, None)], [wo, g_cross], [(D, F32), (D, BF16)], [], tile=tile4,
                     name="mm_out_norm")
    xq = matmul(n2, wxq, out_dtype=BF16, name="mm_xq")
    mn = rowwise(lambda mv_, g: _rms(mv_, g), [(mems, None)], [g_mem], [(D, BF16)], [], tile=min(tile, M), name="norm_mem")[0]
    mkv = matmul(mn, wxkv, out_dtype=BF16, name="mm_mkv")

    x_scale = X_HD ** -0.5

    def xattn_fwd(xqv, mkvv):
        outs = []
        for h in range(X_H):
            sl = slice(h * X_HD, (h + 1) * X_HD)
            s = _dot_nt(xqv[:, sl], mkvv[:, sl]) * x_scale
            s = s - jnp.max(s, axis=-1, keepdims=True)
            e = jnp.exp(s)
            p = e / jnp.sum(e, axis=-1, keepdims=True)
            outs.append(_dot(p.astype(BF16), mkvv[:, D + h * X_HD:D + (h + 1) * X_HD]))
        return jnp.concatenate(outs, axis=1)

    xo = rowwise(xattn_fwd, [(xq, None)], [mkv], [(D, BF16)], [], tile=tile, name="xattn_fwd")[0]
    h2, n3 = rowwise(proj_norm, [(xo, None), (h1, None)], [wxo, g_ffn], [(D, F32), (D, BF16)], [], tile=tile4,
                     name="mm_xo_norm")
    up_pre = matmul(n3, wup, out_dtype=BF16, name="mm_up", tn=1408)
    cw = D_FF // 2
    act = conv_act_fwd(up_pre, wcv, b_conv, tile=tile, cw=cw)

    def down_loss(av, hv2, tv, wv_, g):
        hv = hv2 + _dot(av, wv_)
        y = _rms(hv, g)
        err = y - tv
        part = 0.5 * jnp.sum(jnp.sum(err * err, axis=-1, keepdims=True) / D, axis=0, keepdims=True)
        dx, dg = _rms_bwd(hv, g, err / D)
        return dx, dg, jnp.broadcast_to(part, (8, 128))

    g_fin2 = g_final.reshape(1, D)
    dh3, dg_final, loss_acc = rowwise(down_loss, [(act, None), (h2, None), (tgt, None)], [wdn, g_fin2], [(D, F32)],
                                      [((1, D), F32), ((8, 128), F32)], tile=tile2, name="mm_down_loss")
    loss = lax.psum(loss_acc[0, 0], ("x", "y", "c"))

    dact = matmul(dh3, wdn, tb=True, out_dtype=BF16, name="mm_dact", tn=1408)
    dw_down = matmul_tn(act, dh3, name="mm_dw_down", tm=1408)
    dup_a, dup_b, cs_a, cs_b = conv_act_bwd(up_pre, dact, wcv, b_conv, tile=tile, cw=cw)
    dn3 = matmul2_tb(dup_a, dup_b, wup, name="mm_dn3", tm=1024, tn=512)
    dw_up_a = matmul_tn(n3, dup_a, name="mm_dw_up_a", tn=1408)
    dw_up_b = matmul_tn(n3, dup_b, name="mm_dw_up_b", tn=1408)

    def norm_bwd(hv, dyv, drv, g):
        dx, dg = _rms_bwd(hv, g, dyv)
        return dx + drv, dg

    dh2, dg_ffn = rowwise(norm_bwd, [(h2, None), (dn3, None), (dh3, None)], [g_ffn], [(D, F32)], [((1, D), F32)],
                          tile=tile2, name="norm_ffn_bwd")
    dxo = matmul(dh2, wxo, tb=True, out_dtype=BF16, name="mm_dxo")
    dw_xo = matmul_tn(xo, dh2, name="mm_dw_xo")

    def xattn_bwd(xqv, dxov, mkvv):
        dxq, dmk, dmv = [], [], []
        for h in range(X_H):
            sl = slice(h * X_HD, (h + 1) * X_HD)
            slv = slice(D + h * X_HD, D + (h + 1) * X_HD)
            s = _dot_nt(xqv[:, sl], mkvv[:, sl]) * x_scale
            s = s - jnp.max(s, axis=-1, keepdims=True)
            e = jnp.exp(s)
            p = e / jnp.sum(e, axis=-1, keepdims=True)
            dp = _dot_nt(dxov[:, sl], mkvv[:, slv])
            ds = (p * (dp - jnp.sum(dp * p, axis=-1, keepdims=True)) * x_scale).astype(BF16)
            dxq.append(_dot(ds, mkvv[:, sl]))
            dmk.append(_dot_tn(ds, xqv[:, sl]))
            dmv.append(_dot_tn(p.astype(BF16), dxov[:, sl]))
        return jnp.concatenate(dxq, axis=1), jnp.concatenate(dmk + dmv, axis=1)

    dxq, dmkv = rowwise(xattn_bwd, [(xq, None), (dxo, None)], [mkv], [(D, BF16)], [((M, 2 * D), F32)],
                        tile=tile, name="xattn_bwd")
    dw_xq = matmul_tn(n2, dxq, name="mm_dw_xq")

    def proj_norm_bwd(dyv, hv, drv, wv_, g):
        dx, dg = _rms_bwd(hv, g, _dot_nt(dyv, wv_))
        return dx + drv, dg

    dh1, dg_cross = rowwise(proj_norm_bwd, [(dxq, None), (h1, None), (dh2, None)], [wxq, g_cross], [(D, F32)],
                            [((1, D), F32)], tile=tile4, name="mm_dn2_norm_bwd")
    dw_xkv = matmul_tn(mn, dmkv, name="mm_dw_xkv", tk=M)
    dmn = matmul(dmkv, wxkv, tb=True, name="mm_dmn", tm=M)
    dg_mem = rowwise(lambda mv_, dyv, g: _rms_bwd(mv_, g, dyv)[1], [(mems, None), (dmn, None)], [g_mem], [],
                     [((1, D), F32)], tile=min(tile, M), name="norm_mem_bwd")[0]

    dmix = matmul(dh1, wo, tb=True, out_dtype=BF16, name="mm_dmix")
    dw_out = matmul_tn(mix, dh1, name="mm_dw_out")

    def mix_bwd(ov, rv_, rg, gt, dmv_, wpav, wprv, gr, bg):
        dm_ = dmv_.astype(F32)
        ya = _dot(ov, wpav)
        parts = gn_parts(rv_)
        xh = jnp.concatenate([p[0] for p in parts], axis=1)
        yn = xh * gr
        sg = _sigmoid(rg)
        sl_ = rg * sg
        t = (sl_ * yn).astype(BF16)
        yr = _dot(t, wprv)
        ga_ = _sigmoid(gt[:, :D] + bg[:, :D])
        gr_ = _sigmoid(gt[:, D:] + bg[:, D:])
        dgates = jnp.concatenate([dm_ * ya * ga_ * (1.0 - ga_), dm_ * yr * gr_ * (1.0 - gr_)], axis=1)
        dya = (dm_ * ga_).astype(BF16)
        dyr = (dm_ * gr_).astype(BF16)
        do_ = _dot_nt(dya, wpav)
        dwpa_ = _dot_tn(ov, dya)
        dt = _dot_nt(dyr, wprv)
        dwpr_ = _dot_tn(t, dyr)
        drg_ = dt * yn * (sg * (1.0 + rg * (1.0 - sg)))
        dyn = dt * sl_
        dgr = jnp.sum(dyn * xh, axis=0, keepdims=True)
        dxh = dyn * gr
        drets = []
        for h in range(RET_H):
            sl = slice(h * RET_D, (h + 1) * RET_D)
            xhh, rstd = parts[h]
            dxhh = dxh[:, sl]
            drets.append(rstd * (dxhh - jnp.mean(dxhh, axis=-1, keepdims=True)
                                 - xhh * jnp.mean(dxhh * xhh, axis=-1, keepdims=True)))
        dret_ = jnp.concatenate(drets, axis=1)
        dbg = jnp.sum(dgates, axis=0, keepdims=True)
        return do_, dret_, drg_, dgates, dwpa_, dwpr_, dgr, dbg

    do_a, dret, drg, dgates, dwpa, dw_proj_ret, dg_ret, db_gate = rowwise(
        mix_bwd, [(o_a, None), (ret, None), (z, (512, 3)), (z, (2048, 1)), (dmix, None)], [wpa, wpr, g_ret, b_gate],
        [(MLA_H * HP, BF16), (512, F32), (512, BF16), (2 * D, BF16)],
        [((MLA_H * HP, D), F32), ((512, D), F32), ((1, 512), F32), ((1, 2 * D), F32)], tile=tile, name="mix_bwd")

    sl1 = slots_l1(dw_up_a, dw_up_b, dw_xkv, tile=tile)
    sl2, sl3 = slots_l23(dw_out, dw_xq, dw_xo, dw_down, dwpa, dw_proj_ret)
    lse_row, delta_row = mla_prep(o_a, do_a, lse, tq=tq)
    dq_a, dk_a, dv_a, rl1, rl2, rl3 = mla_bwd(q_a, k_a, v_a, do_a, lse_row, delta_row, [sl1, sl2, sl3], tq=tq)
    drq_r, drk_r, drv = ret_bwd(rq, rk, rv, rstate, dret, rconsts, rb=rb)

    def lat_bwd(zl, cmv, s1v, s2v, dqv, dkv_, dvv, gq, gkv, wqv, wkv_, wvv):
        cqf, ckvf = zl[:, 0:256], zl[:, 256:384]
        cq = _rms(cqf, gq).astype(BF16)
        ckv = _rms(ckvf, gkv).astype(BF16)
        dq_pre = _rot_mla(dqv.astype(F32), cmv, -s1v, -s2v).astype(BF16)
        dkf = dkv_.astype(F32)
        dkr = dkf[:, 0:HP]
        for h in range(1, MLA_H):
            dkr = dkr + dkf[:, h * HP:(h + 1) * HP]
        lane = lax.broadcasted_iota(jnp.int32, dkr.shape, 1)
        dzk = _rot_mla(jnp.where((lane >= 64) & (lane < 96), dkr, 0.0), cmv, -s1v, -s2v)
        dkb = dkv_.astype(BF16)
        dvb = dvv.astype(BF16)
        dcq_n = _dot_nt(dq_pre, wqv)
        dckv_n = _dot_nt(dkb, wkv_) + _dot_nt(dvb, wvv)
        dwq_ = _dot_tn(cq, dq_pre)
        dwk_ = _dot_tn(ckv, dkb)
        dwv_ = _dot_tn(ckv, dvb)
        dcq, dgq = _rms_bwd(cqf, gq, dcq_n)
        dckv, dgkv = _rms_bwd(ckvf, gkv, dckv_n)
        return jnp.concatenate([dcq, dckv, dzk], axis=1), dwq_, dwk_, dwv_, dgq, dgkv

    dz_lat, dwq, dwk, dwv, dg_q_lat, dg_kv_lat = rowwise(
        lat_bwd, [(z, (512, 8)), (cm, None), (s1, None), (s2, None), (dq_a, None), (dk_a, None), (dv_a, None)],
        [g_q_lat, g_kv_lat, wq, wk, wv], [(512, BF16)],
        [((MLA_QR, MLA_H * HP), F32), ((MLA_KVR, MLA_H * HP), F32), ((MLA_KVR, MLA_H * HP), F32),
         ((1, MLA_QR), F32), ((1, MLA_KVR), F32)], tile=tile2, name="lat_bwd")

    def dz_assemble(dq_, dk_, dv_, drg_, dgt, dzl, crv, srv):
        a = _rot_ret(dq_, crv, -srv)
        b = _rot_ret(dk_, crv, -srv) * (RET_D ** -0.5)
        return jnp.concatenate([a, b, dv_, drg_.astype(F32), dgt.astype(F32), dzl.astype(F32)], axis=1)

    dz = rowwise(dz_assemble, [(drq_r, None), (drk_r, None), (drv, None), (drg, None), (dgates, None), (dz_lat, None),
                               (cr, None), (sr, None)], [], [(ZW, BF16)], [], tile=tile2, name="dz_assemble")[0]
    dwz = matmul_tn(u, dz, name="mm_dw_z", tn=1536)
    se1, se2 = slots_early(dwz, dwk, dwv, dwq, tile=tile)
    du, re1, re2 = matmul(dz, wz, tb=True, name="mm_du", tm=1024, tn=512, exchange=[se1, se2])

    grad_x, dg_mix = rowwise(norm_bwd, [(xs, None), (du, None), (dh1, None)], [g_mix], [(D, F32)], [((1, D), F32)],
                             tile=tile2, name="norm_mix_bwd")

    small_grads = {"g_mix": dg_mix, "b_gate": db_gate, "g_q_lat": dg_q_lat, "g_kv_lat": dg_kv_lat, "g_ret": dg_ret,
                   "g_cross": dg_cross, "g_mem": dg_mem, "g_ffn": dg_ffn, "g_final": dg_final}
    rd, rs, dslots, own_s = exchange_small(cs_a, cs_b, [small_grads[n] for n, _, _ in SMALL_DIRECT])

    me = _my_id()

    def own(slots):
        return lax.dynamic_index_in_dim(slots, me, axis=0, keepdims=False)

    def wmv(names):
        return [shard(n, p) for n in names for p in ("", "m_", "v_")]

    names_s = tuple(n for n, _, _ in SMALL)
    groups = (
        (("w_in",), adam_cols(own(se1), re1, wmv(("w_in",)), ((0, NS_IN),), name="adam_e1", tile=128)),
        (("w_ukv", "w_uq"), adam_rows(own(se2), re2, wmv(("w_ukv", "w_uq")),
                                      ((0, E2_UQ, HP), (E2_UQ, E2_ROWS, NS_UQ)), name="adam_e2")),
        (("w_up", "w_xkv"), adam_cols(own(sl1), rl1, wmv(("w_up", "w_xkv")), ((0, NS_UP), (NS_UP, L1_W)),
                                      name="adam_l1", tile=128)),
        (("w_out", "w_xq", "w_xo", "w_down"),
         adam_rows(own(sl2), rl2, wmv(("w_out", "w_xq", "w_xo", "w_down")),
                   ((L2_OUT, L2_XQ, D), (L2_XQ, L2_XO, D), (L2_XO, L2_DN, D), (L2_DN, L2_ROWS, D)), name="adam_l2")),
        (("w_proj_mla", "w_proj_ret"), adam_rows(own(sl3), rl3, wmv(("w_proj_mla", "w_proj_ret")),
                                                 ((0, L3_PRET, HP), (L3_PRET, L3_ROWS, HP)), name="adam_l3")),
        (names_s + ("w_conv",), adam_small(own_s, rs, dslots, rd, wmv(names_s), wmv(("w_conv",)))),
    )
    res = {}
    for names, outs_ in groups:
        for t, n in enumerate(names):
            res[n] = outs_[4 * t:4 * t + 4]

    order = ["g_mix", "w_in", "b_gate", "g_q_lat", "w_uq", "g_kv_lat", "w_ukv", "w_proj_mla", "g_ret", "w_proj_ret",
             "w_out", "g_cross", "g_mem", "w_xq", "w_xkv", "w_xo", "g_ffn", "w_up", "w_conv", "b_conv", "w_down",
             "g_final"]
    outs = [loss, grad_x[None]]
    for kind in range(4):
        outs += [res[n][kind].reshape(args[n].shape) for n in order]
    return tuple(outs)
```

```python
import functools
import math

import jax
import jax.numpy as jnp
import numpy as np
from jax import lax
from jax.experimental import pallas as pl
from jax.experimental.pallas import tpu as pltpu

F32 = jnp.float32
BF16 = jnp.bfloat16

D = 1024
MLA_H, MLA_NOPE, MLA_ROPE, MLA_V = 8, 64, 32, 64
MLA_QR, MLA_KVR = 256, 128
RET_H, RET_D, RET_C = 4, 128, 128
X_H, X_HD = 4, 256
D_FF = 2816
THETA = 10000.0
EPS = 1e-6
HP = 128
ZW = 4608
N_DEV = 8

ADAM_LR, ADAM_B1, ADAM_B2, ADAM_EPS, ADAM_WD, ADAM_STEP = 0.001, 0.9, 0.999, 1e-08, 0.01, 10

VMEM_LIMIT = 56 * 1024 * 1024
MESH = pl.DeviceIdType.MESH
VM = pl.BlockSpec(memory_space=pltpu.VMEM)
ANY = pl.BlockSpec(memory_space=pl.ANY)


def _cp(n_axes):
    return pltpu.CompilerParams(dimension_semantics=("arbitrary",) * n_axes, vmem_limit_bytes=VMEM_LIMIT)


def _cp0():
    return pltpu.CompilerParams(vmem_limit_bytes=VMEM_LIMIT)


def _pick(n, cap, mult=128):
    best = None
    for t in range(mult, min(n, cap) + 1, mult):
        if n % t == 0:
            best = t
    return best if best is not None else n


def _dot(a, b):
    return jnp.dot(a, b, preferred_element_type=F32)


def _dot_nt(a, b):
    return lax.dot_general(a, b, (((1,), (1,)), ((), ())), preferred_element_type=F32)


def _dot_tn(a, b):
    return lax.dot_general(a, b, (((0,), (0,)), ((), ())), preferred_element_type=F32)


def _sds(shape, dtype):
    return jax.ShapeDtypeStruct(shape, dtype)


def matmul(a, b, *, name, tb=False, res=None, out_dtype=F32, tm=1024, tn=1024, exchange=()):
    M, K = a.shape
    N = b.shape[0] if tb else b.shape[1]
    tm = _pick(M, tm, 8)
    tn = _pick(N, tn)
    gi, gj = M // tm, N // tn
    ns = len(exchange)
    n_in = 2 if res is None else 3

    def body(*refs):
        a_ref, b_ref = refs[:2]
        o_ref = refs[n_in + ns]
        if ns:
            comm = ([(r, True) for r in refs[n_in:n_in + ns]], refs[n_in + ns + 1:n_in + 2 * ns + 1]) + tuple(
                refs[n_in + 2 * ns + 1:])
            i, j = pl.program_id(0), pl.program_id(1)

            @pl.when((i == 0) & (j == 0))
            def _():
                _exchange_start(*comm)

        av = a_ref[...].astype(BF16)
        bv = b_ref[...].astype(BF16)
        acc = _dot_nt(av, bv) if tb else _dot(av, bv)
        if res is not None:
            acc = acc + refs[2][...].astype(F32)
        o_ref[...] = acc.astype(o_ref.dtype)
        if ns:

            @pl.when((i == gi - 1) & (j == gj - 1))
            def _():
                _exchange_wait(*comm)

    in_specs = [
        pl.BlockSpec((tm, K), lambda i, j: (i, 0)),
        pl.BlockSpec((tn, K), lambda i, j: (j, 0)) if tb else pl.BlockSpec((K, tn), lambda i, j: (0, j)),
    ]
    args = [a, b]
    if res is not None:
        in_specs.append(pl.BlockSpec((tm, tn), lambda i, j: (i, j)))
        args.append(res)
    out_spec = pl.BlockSpec((tm, tn), lambda i, j: (i, j))
    out_shape = _sds((M, N), out_dtype)
    return pl.pallas_call(
        body,
        name=name,
        grid=(gi, gj),
        in_specs=in_specs + [ANY] * ns,
        out_specs=[out_spec] + [ANY] * ns if ns else out_spec,
        out_shape=[out_shape] + [_sds((N_DEV - 1,) + s.shape[1:], s.dtype) for s in exchange] if ns else out_shape,
        scratch_shapes=_exchange_scratch(ns) if ns else [],
        compiler_params=_cp(2),
    )(*args, *exchange)


def matmul2_tb(a1, a2, b, *, name, tm=512, tn=512):
    M, K1 = a1.shape
    N = b.shape[0]
    tm = _pick(M, tm, 8)
    tn = _pick(N, tn)

    def body(a1_ref, a2_ref, b1_ref, b2_ref, o_ref):
        o_ref[...] = (_dot_nt(a1_ref[...].astype(BF16), b1_ref[...].astype(BF16))
                      + _dot_nt(a2_ref[...].astype(BF16), b2_ref[...].astype(BF16)))

    return pl.pallas_call(
        body,
        name=name,
        grid=(M // tm, N // tn),
        in_specs=[pl.BlockSpec((tm, K1), lambda i, j: (i, 0)), pl.BlockSpec((tm, K1), lambda i, j: (i, 0)),
                  pl.BlockSpec((tn, K1), lambda i, j: (j, 0)), pl.BlockSpec((tn, K1), lambda i, j: (j, 1))],
        out_specs=pl.BlockSpec((tm, tn), lambda i, j: (i, j)),
        out_shape=_sds((M, N), F32),
        compiler_params=_cp(2),
    )(a1, a2, b, b)


def matmul_tn(a, b, *, name, tm=1024, tn=1024, tk=1024):
    R, M = a.shape
    N = b.shape[1]
    tm = _pick(M, tm)
    tn = _pick(N, tn)
    tk = _pick(R, tk, 16)
    nk = R // tk

    def body(a_ref, b_ref, o_ref, acc_ref):
        k = pl.program_id(2)

        @pl.when(k == 0)
        def _():
            acc_ref[...] = jnp.zeros_like(acc_ref)

        acc_ref[...] += _dot_tn(a_ref[...].astype(BF16), b_ref[...].astype(BF16))

        @pl.when(k == nk - 1)
        def _():
            o_ref[...] = acc_ref[...]

    return pl.pallas_call(
        body,
        name=name,
        grid=(M // tm, N // tn, nk),
        in_specs=[pl.BlockSpec((tk, tm), lambda i, j, k: (k, i)), pl.BlockSpec((tk, tn), lambda i, j, k: (k, j))],
        out_specs=pl.BlockSpec((tm, tn), lambda i, j, k: (i, j)),
        out_shape=_sds((M, N), F32),
        scratch_shapes=[pltpu.VMEM((tm, tn), F32)],
        compiler_params=_cp(3),
    )(a, b)


def rowwise(fn, rows, consts, out_rows, out_accs, *, tile, name, gather=()):
    T = rows[0][0].shape[0]
    nt = T // tile
    n_r, n_c, n_o, n_a, n_g = len(rows), len(consts), len(out_rows), len(out_accs), len(gather)
    n_in = n_r + n_c + n_g

    def body(*refs):
        if n_g:
            comm = (refs[n_r + n_c:n_in], refs[n_in + n_o + n_a:n_in + n_o + n_a + n_g]) + tuple(
                refs[n_in + n_o + n_a + n_g:])

            @pl.when(pl.program_id(0) == 0)
            def _():
                _gather_start(*comm)

        ins = [r[...] for r in refs[: n_r + n_c]]
        outs = fn(*ins)
        if not isinstance(outs, (tuple, list)):
            outs = (outs,)
        o_refs = refs[n_in:n_in + n_o]
        a_refs = refs[n_in + n_o:n_in + n_o + n_a]
        for o_ref, o in zip(o_refs, outs[:n_o]):
            o_ref[...] = o.astype(o_ref.dtype)
        if n_a:
            first = pl.program_id(0) == 0

            @pl.when(first)
            def _():
                for a_ref, o in zip(a_refs, outs[n_o:]):
                    a_ref[...] = o.astype(a_ref.dtype)

            @pl.when(jnp.logical_not(first))
            def _():
                for a_ref, o in zip(a_refs, outs[n_o:]):
                    a_ref[...] += o.astype(a_ref.dtype)
        if n_g:

            @pl.when(pl.program_id(0) == nt - 1)
            def _():
                _gather_wait(*comm)

    in_specs = []
    args = []
    for arr, win in rows:
        if win is None:
            in_specs.append(pl.BlockSpec((tile, arr.shape[1]), lambda i: (i, 0)))
        else:
            w, cb = win
            in_specs.append(pl.BlockSpec((tile, w), functools.partial(lambda i, cb: (i, cb), cb=cb)))
        args.append(arr)
    for c in consts:
        in_specs.append(pl.BlockSpec(c.shape, functools.partial(lambda i, nd: (0,) * nd, nd=c.ndim)))
        args.append(c)
    out_specs = [pl.BlockSpec((tile, w), lambda i: (i, 0)) for w, _ in out_rows]
    out_shape = [_sds((T, w), dt) for w, dt in out_rows]
    for shp, dt in out_accs:
        out_specs.append(pl.BlockSpec(shp, functools.partial(lambda i, nd: (0,) * nd, nd=len(shp))))
        out_shape.append(_sds(shp, dt))
    return pl.pallas_call(
        body,
        name=name,
        grid=(nt,),
        in_specs=in_specs + [ANY] * n_g,
        out_specs=out_specs + [ANY] * n_g,
        out_shape=out_shape + [_sds((N_DEV,) + p.shape, p.dtype) for p in gather],
        scratch_shapes=_gather_scratch(n_g) if n_g else [],
        compiler_params=_cp(1),
    )(*args, *gather)


def _rms(x, g):
    r = lax.rsqrt(jnp.mean(x * x, axis=-1, keepdims=True) + EPS)
    return x * r * g


def _rms_bwd(x, g, dy):
    r = lax.rsqrt(jnp.mean(x * x, axis=-1, keepdims=True) + EPS)
    xh = x * r
    dg = jnp.sum(dy * xh, axis=0, keepdims=True)
    dxh = dy * g
    dx = r * (dxh - xh * jnp.mean(dxh * xh, axis=-1, keepdims=True))
    return dx, dg


def _sigmoid(x):
    return 0.5 * jnp.tanh(0.5 * x) + 0.5


def _rot_mla(x, c, s1, s2):
    n = x.shape[1] // HP
    outs = []
    for h in range(n):
        xh = x[:, h * HP : (h + 1) * HP]
        outs.append(xh * c + pltpu.roll(xh, HP - 16, 1) * s1 + pltpu.roll(xh, 16, 1) * s2)
    return outs[0] if n == 1 else jnp.concatenate(outs, axis=1)


def _rot_ret(x, c, s):
    n = x.shape[1] // RET_D
    outs = []
    for h in range(n):
        xh = x[:, h * RET_D : (h + 1) * RET_D]
        outs.append(xh * c + pltpu.roll(xh, RET_D // 2, 1) * s)
    return outs[0] if n == 1 else jnp.concatenate(outs, axis=1)


def _rot_inv():
    lane_np = np.arange(128)
    inv_m = (jnp.asarray(THETA, F32) ** (-jnp.asarray(lane_np & 15, F32) / 16.0)).reshape(1, 128)
    inv_r = (jnp.asarray(THETA, F32) ** (-jnp.asarray(lane_np & 63, F32) / 64.0)).reshape(1, 128)
    return inv_m, inv_r


def _rot_tables(p, im, ir):
    lane = lax.broadcasted_iota(jnp.int32, p.shape, 1)
    ang = p * im
    cm = jnp.where((lane >= 64) & (lane < 96), jnp.cos(ang), 1.0)
    sn = jnp.sin(ang)
    s1 = jnp.where((lane >= 64) & (lane < 80), -sn, 0.0)
    s2 = jnp.where((lane >= 80) & (lane < 96), sn, 0.0)
    angr = p * ir
    snr = jnp.sin(angr)
    return cm, s1, s2, jnp.cos(angr), jnp.where(lane < 64, -snr, snr)


def _peer(m):
    x, y, c = lax.axis_index("x"), lax.axis_index("y"), lax.axis_index("c")
    mx, my, mc = (m >> 2) & 1, (m >> 1) & 1, m & 1
    px = 1 - x if mx else x
    py = 1 - y if my else y
    pc = 1 - c if mc else c
    return (px, py, pc), 4 * px + 2 * py + pc


def _my_id():
    return 4 * lax.axis_index("x") + 2 * lax.axis_index("y") + lax.axis_index("c")


def _gather_copies(srcs, outs, send_sems, recv_sems, local_sems, arriving=False):
    me = _my_id()
    copies = []
    if not arriving:
        for g, (src, out) in enumerate(zip(srcs, outs)):
            copies.append((pltpu.make_async_copy(src, out.at[me], local_sems.at[g]), False))
    for m in range(1, N_DEV):
        peer, plin = _peer(m)
        for g, (src, out) in enumerate(zip(srcs, outs)):
            copies.append((pltpu.make_async_remote_copy(
                src_ref=src, dst_ref=out.at[plin if arriving else me], send_sem=send_sems.at[g, m - 1],
                recv_sem=recv_sems.at[g, m - 1], device_id=peer, device_id_type=MESH), True))
    return copies


def _gather_start(*a):
    for cp, _ in _gather_copies(*a):
        cp.start()


def _gather_wait(*a):
    for cp, _ in _gather_copies(*a, arriving=True):
        cp.wait_recv()
    for cp, remote in _gather_copies(*a):
        if remote:
            cp.wait_send()
        else:
            cp.wait()


def _gather_scratch(n):
    return [pltpu.SemaphoreType.DMA((n, N_DEV - 1)), pltpu.SemaphoreType.DMA((n, N_DEV - 1)), pltpu.SemaphoreType.DMA((n,))]


def _exchange_copies(srcs, dsts, send_sems, recv_sems):
    copies = []
    for m in range(1, N_DEV):
        peer, plin = _peer(m)
        for g, ((src, per_peer), dst) in enumerate(zip(srcs, dsts)):
            copies.append(pltpu.make_async_remote_copy(
                src_ref=src.at[plin] if per_peer else src, dst_ref=dst.at[m - 1], send_sem=send_sems.at[g, m - 1],
                recv_sem=recv_sems.at[g, m - 1], device_id=peer, device_id_type=MESH))
    return copies


def _exchange_start(*a):
    for cp in _exchange_copies(*a):
        cp.start()


def _exchange_wait(*a):
    copies = _exchange_copies(*a)
    for cp in copies:
        cp.wait_recv()
    for cp in copies:
        cp.wait_send()


def _exchange_scratch(n):
    return [pltpu.SemaphoreType.DMA((n, N_DEV - 1)), pltpu.SemaphoreType.DMA((n, N_DEV - 1))]


MLA_SCALE = (MLA_NOPE + MLA_ROPE) ** -0.5
MLA_C2 = MLA_SCALE * math.log2(math.e)
ONE_LANE = MLA_V
NEG = -1e30
HPS = 4
HPB = 2


def _tri_mask(n, lower_rows_ge_cols=True):
    r = lax.broadcasted_iota(jnp.int32, (n, n), 0)
    c = lax.broadcasted_iota(jnp.int32, (n, n), 1)
    return r >= c if lower_rows_ge_cols else c >= r


def mla_fwd(q, k, v, gather, *, tq):
    T = q.shape[0]
    nq = T // tq
    rep = tq // HP
    ng = len(gather)

    def body(*refs):
        q_ref, k_ref, v_ref = refs[:3]
        srcs = refs[3:3 + ng]
        o_ref, lse_ref = refs[3 + ng:5 + ng]
        outs = refs[5 + ng:5 + 2 * ng]
        m_sc, acc_sc = refs[5 + 2 * ng:7 + 2 * ng]
        comm = (srcs, outs) + tuple(refs[7 + 2 * ng:])
        h, i = pl.program_id(0), pl.program_id(1)

        @pl.when((h == 0) & (i == 0))
        def _():
            _gather_start(*comm)

        m_sc[...] = jnp.full(m_sc.shape, NEG, F32)
        acc_sc[...] = jnp.zeros(acc_sc.shape, F32)
        heads = [slice(t * HP, (t + 1) * HP) for t in range(HPS)]

        def block(j, masked):
            off = pl.multiple_of(j * tq, tq)
            ss = [_dot_nt(q_ref[:, hd], k_ref[pl.ds(off, tq), hd]) * MLA_C2 for hd in heads]
            for hd, s in zip(heads, ss):
                if masked:
                    s = jnp.where(_tri_mask(tq), s, NEG)
                m_prev = m_sc[:, hd]
                m_next = jnp.maximum(m_prev, jnp.max(s, axis=-1, keepdims=True))
                p = jnp.exp2(s - jnp.tile(m_next, (1, rep)))
                alpha = jnp.exp2(m_prev - m_next)
                acc_sc[:, hd] = alpha * acc_sc[:, hd] + _dot(p.astype(BF16), v_ref[pl.ds(off, tq), hd])
                m_sc[:, hd] = m_next

        def loop_body(j, carry):
            block(j, False)
            return carry

        lax.fori_loop(0, i, loop_body, 0)
        block(i, True)
        for hd in heads:
            acc = acc_sc[:, hd]
            l = acc[:, ONE_LANE:ONE_LANE + 1]
            o_ref[:, hd] = (acc / l).astype(o_ref.dtype)
            lse_ref[:, hd] = m_sc[:, hd] + jnp.log(l) * math.log2(math.e)

        @pl.when((h == MLA_H // HPS - 1) & (i == nq - 1))
        def _():
            _gather_wait(*comm)

    blk = pl.BlockSpec((tq, HPS * HP), lambda h, i: (i, h))
    full = pl.BlockSpec((T, HPS * HP), lambda h, i: (0, h))
    return pl.pallas_call(
        body,
        name="mla_fwd",
        grid=(MLA_H // HPS, nq),
        in_specs=[blk, full, full] + [ANY] * ng,
        out_specs=[blk, blk] + [ANY] * ng,
        out_shape=[_sds((T, MLA_H * HP), BF16), _sds((T, MLA_H * HP), F32)]
        + [_sds((N_DEV,) + p.shape, p.dtype) for p in gather],
        scratch_shapes=[pltpu.VMEM((tq, HPS * HP), F32), pltpu.VMEM((tq, HPS * HP), F32)] + _gather_scratch(ng),
        compiler_params=_cp(2),
    )(q, k, v, *gather)


def mla_prep(o, do, lse, *, tq):
    T = o.shape[0]
    nq = T // tq

    def body(o_ref, do_ref, lse_ref, lse_row_ref, delta_row_ref):
        for h in range(MLA_H):
            sl = slice(h * HP, (h + 1) * HP)
            d = jnp.sum(o_ref[:, sl].astype(F32) * do_ref[:, sl].astype(F32), axis=-1, keepdims=True)
            lse_row_ref[h] = lse_ref[:, sl].T[0:1, :]
            delta_row_ref[h] = jnp.broadcast_to(d, (tq, HP)).T[0:1, :]

    blk = pl.BlockSpec((tq, MLA_H * HP), lambda i: (i, 0))
    row = pl.BlockSpec((MLA_H, None, 1, tq), lambda i: (0, i, 0, 0))
    return pl.pallas_call(
        body,
        name="mla_prep",
        grid=(nq,),
        in_specs=[blk, blk, blk],
        out_specs=[row, row],
        out_shape=[_sds((MLA_H, nq, 1, tq), F32), _sds((MLA_H, nq, 1, tq), F32)],
        compiler_params=_cp(1),
    )(o, do, lse)


def mla_bwd(q, k, v, do, lse_row, delta_row, slots, *, tq):
    T = q.shape[0]
    nq = T // tq
    ns = len(slots)

    def body(*refs):
        q_ref, k_ref, v_ref, do_ref, lse_ref, delta_ref = refs[:6]
        srcs = [(r, True) for r in refs[6:6 + ns]]
        dq_ref, dk_ref, dv_ref = refs[6 + ns:9 + ns]
        dsts = refs[9 + ns:9 + 2 * ns]
        dk_sc, dv_sc = refs[9 + 2 * ns:11 + 2 * ns]
        comm = (srcs, dsts) + tuple(refs[11 + 2 * ns:])
        h, j = pl.program_id(0), pl.program_id(1)

        @pl.when((h == 0) & (j == 0))
        def _():
            _exchange_start(*comm)

        dk_sc[...] = jnp.zeros(dk_sc.shape, F32)
        dv_sc[...] = jnp.zeros(dv_sc.shape, F32)
        heads = [slice(t * HP, (t + 1) * HP) for t in range(HPB)]

        @pl.when(j == 0)
        def _():
            dq_ref[...] = jnp.zeros(dq_ref.shape, F32)

        def block(i, masked):
            off = pl.multiple_of(i * tq, tq)
            sts = [_dot_nt(k_ref[:, hd], q_ref[pl.ds(off, tq), hd]) * MLA_C2 for hd in heads]
            dpts = [_dot_nt(v_ref[:, hd], do_ref[pl.ds(off, tq), hd]) for hd in heads]
            for t, hd in enumerate(heads):
                st = sts[t]
                if masked:
                    st = jnp.where(_tri_mask(tq, False), st, NEG)
                pt = jnp.exp2(st - lse_ref[t, i])
                dv_sc[:, hd] += _dot(pt.astype(BF16), do_ref[pl.ds(off, tq), hd])
                dst = (pt * (dpts[t] - delta_ref[t, i]) * MLA_SCALE).astype(BF16)
                dk_sc[:, hd] += _dot(dst, q_ref[pl.ds(off, tq), hd])
                dq_ref[pl.ds(off, tq), hd] += _dot_tn(dst, k_ref[:, hd])

        block(j, True)

        def loop_body(i, carry):
            block(i, False)
            return carry

        lax.fori_loop(j + 1, nq, loop_body, 0)
        dk_ref[...] = dk_sc[...].astype(dk_ref.dtype)
        dv_ref[...] = dv_sc[...].astype(dv_ref.dtype)

        @pl.when((h == MLA_H // HPB - 1) & (j == nq - 1))
        def _():
            _exchange_wait(*comm)

    blk = pl.BlockSpec((tq, HPB * HP), lambda h, j: (j, h))
    full = pl.BlockSpec((T, HPB * HP), lambda h, j: (0, h))
    rows = pl.BlockSpec((HPB, nq, 1, tq), lambda h, j: (h, 0, 0, 0))
    return pl.pallas_call(
        body,
        name="mla_bwd",
        grid=(MLA_H // HPB, nq),
        in_specs=[full, blk, blk, full, rows, rows] + [ANY] * ns,
        out_specs=[full, blk, blk] + [ANY] * ns,
        out_shape=[_sds((T, MLA_H * HP), F32), _sds((T, MLA_H * HP), BF16), _sds((T, MLA_H * HP), BF16)]
        + [_sds((N_DEV - 1,) + s.shape[1:], s.dtype) for s in slots],
        scratch_shapes=[pltpu.VMEM((tq, HPB * HP), F32), pltpu.VMEM((tq, HPB * HP), F32)] + _exchange_scratch(ns),
        compiler_params=_cp(2),
    )(q, k, v, do, lse_row, delta_row, *slots)


def _ret_consts():
    h = jnp.arange(RET_H, dtype=F32)
    log_g = jnp.log1p(-jnp.exp2(-5.0 - h))
    idx = jnp.arange(RET_C, dtype=F32)
    rel = idx[:, None] - idx[None, :]
    dmask = jnp.where(rel >= 0, jnp.exp(log_g[:, None, None] * jnp.maximum(rel, 0.0)), 0.0)
    zeta = jnp.exp(log_g[:, None] * (RET_C - 1.0 - idx)[None, :])
    xi = jnp.exp(log_g[:, None] * (idx + 1.0)[None, :])
    decay = jnp.exp(log_g * RET_C)
    zb = jnp.broadcast_to(zeta[:, :, None], (RET_H, RET_C, RET_D))
    xb = jnp.broadcast_to(xi[:, :, None], (RET_H, RET_C, RET_D))
    db = jnp.broadcast_to(decay[:, None, None], (RET_H, RET_C, RET_D))
    return dmask.astype(F32), zb.astype(F32), xb.astype(F32), db.astype(F32)


def ret_fwd(rq, rk, rv, consts, *, rb):
    T = rq.shape[0]
    nb = T // rb
    ncb = rb // RET_C

    def body(q_ref, k_ref, v_ref, dm_ref, z_ref, x_ref, dc_ref, o_ref, st_ref, r_sc):
        @pl.when(pl.program_id(1) == 0)
        def _():
            r_sc[...] = jnp.zeros(r_sc.shape, F32)

        dm, zt, xi, dc = dm_ref[...], z_ref[...], x_ref[...], dc_ref[...]
        for c in range(ncb):
            sl = slice(c * RET_C, (c + 1) * RET_C)
            q, k, v = q_ref[sl, :], k_ref[sl, :], v_ref[sl, :]
            r = r_sc[...]
            rbf = r.astype(BF16)
            st_ref[sl, :] = rbf
            s = _dot_nt(q, k) * dm
            inner = _dot(s.astype(BF16), v)
            cross = _dot((q.astype(F32) * xi).astype(BF16), rbf)
            o_ref[sl, :] = inner + cross
            kz = (k.astype(F32) * zt).T.astype(BF16)
            r_sc[...] = r * dc + _dot(kz, v)

    blk = pl.BlockSpec((rb, RET_D), lambda h, b: (b, h))
    cst = pl.BlockSpec((None, RET_C, RET_D), lambda h, b: (h, 0, 0))
    return pl.pallas_call(
        body,
        name="ret_fwd",
        grid=(RET_H, nb),
        in_specs=[blk, blk, blk, cst, cst, cst, cst],
        out_specs=[blk, blk],
        out_shape=[_sds((T, RET_H * RET_D), F32), _sds((T, RET_H * RET_D), BF16)],
        scratch_shapes=[pltpu.VMEM((RET_D, RET_D), F32)],
        compiler_params=_cp(2),
    )(rq, rk, rv, *consts)


def ret_bwd(rq, rk, rv, st, dret, consts, *, rb):
    T = rq.shape[0]
    nb = T // rb
    ncb = rb // RET_C

    def body(q_ref, k_ref, v_ref, st_ref, do_ref, dm_ref, z_ref, x_ref, dc_ref, dq_ref, dk_ref, dv_ref, g_sc):
        @pl.when(pl.program_id(1) == 0)
        def _():
            g_sc[...] = jnp.zeros(g_sc.shape, F32)

        dm, zt, xi, dc = dm_ref[...], z_ref[...], x_ref[...], dc_ref[...]
        for c in reversed(range(ncb)):
            sl = slice(c * RET_C, (c + 1) * RET_C)
            q, k, v, rp = q_ref[sl, :], k_ref[sl, :], v_ref[sl, :], st_ref[sl, :]
            dob = do_ref[sl, :].astype(BF16)
            qf, kf = q.astype(F32), k.astype(F32)
            gn = g_sc[...]
            gnb = gn.astype(BF16)
            s = _dot_nt(q, k) * dm
            ds = _dot_nt(dob, v) * dm
            dq = _dot(ds.astype(BF16), k) + _dot_nt(dob, rp) * xi
            dk = _dot(ds.T.astype(BF16), q) + _dot_nt(v, gnb) * zt
            dv = _dot(s.T.astype(BF16), dob) + _dot((kf * zt).astype(BF16), gnb)
            dq_ref[sl, :] = dq.astype(dq_ref.dtype)
            dk_ref[sl, :] = dk.astype(dk_ref.dtype)
            dv_ref[sl, :] = dv.astype(dv_ref.dtype)
            g_sc[...] = _dot((qf * xi).T.astype(BF16), dob) + dc * gn

    blk = pl.BlockSpec((rb, RET_D), lambda h, b: (nb - 1 - b, h))
    cst = pl.BlockSpec((None, RET_C, RET_D), lambda h, b: (h, 0, 0))
    return pl.pallas_call(
        body,
        name="ret_bwd",
        grid=(RET_H, nb),
        in_specs=[blk, blk, blk, blk, blk, cst, cst, cst, cst],
        out_specs=[blk, blk, blk],
        out_shape=[_sds((T, RET_H * RET_D), F32)] * 3,
        scratch_shapes=[pltpu.VMEM((RET_D, RET_D), F32)],
        compiler_params=_cp(2),
    )(rq, rk, rv, st, dret, *consts)


HALO = 16


def conv_act_fwd(up_pre, w_conv, b_conv, *, tile, cw):
    T = up_pre.shape[0]
    nt = T // tile
    ncol = D_FF // cw
    hb = tile // HALO

    def body(pa_ref, a_ref, pb_ref, b_ref, wa_ref, wb_ref, ba_ref, bb_ref, o_ref):
        i = pl.program_id(1)
        keep = (i > 0).astype(F32)

        def conv(prev_ref, cur_ref, w_ref, bias_ref):
            ext = jnp.concatenate([prev_ref[...].astype(F32) * keep, cur_ref[...].astype(F32)], axis=0)
            w = w_ref[...]
            y = ext * w[2:3, :] + pltpu.roll(ext, 1, 0) * w[1:2, :] + pltpu.roll(ext, 2, 0) * w[0:1, :] + bias_ref[...]
            return y[HALO:, :]

        a = conv(pa_ref, a_ref, wa_ref, ba_ref)
        b = conv(pb_ref, b_ref, wb_ref, bb_ref)
        o_ref[...] = (a * _sigmoid(a) * b).astype(o_ref.dtype)

    prev_a = pl.BlockSpec((HALO, cw), lambda j, i: (jnp.maximum(i * hb - 1, 0), j))
    cur_a = pl.BlockSpec((tile, cw), lambda j, i: (i, j))
    prev_b = pl.BlockSpec((HALO, cw), lambda j, i: (jnp.maximum(i * hb - 1, 0), j + ncol))
    cur_b = pl.BlockSpec((tile, cw), lambda j, i: (i, j + ncol))
    w_a = pl.BlockSpec((3, cw), lambda j, i: (0, j))
    w_b = pl.BlockSpec((3, cw), lambda j, i: (0, j + ncol))
    bias_a = pl.BlockSpec((1, cw), lambda j, i: (0, j))
    bias_b = pl.BlockSpec((1, cw), lambda j, i: (0, j + ncol))
    return pl.pallas_call(
        body,
        name="conv_act_fwd",
        grid=(ncol, nt),
        in_specs=[prev_a, cur_a, prev_b, cur_b, w_a, w_b, bias_a, bias_b],
        out_specs=pl.BlockSpec((tile, cw), lambda j, i: (i, j)),
        out_shape=_sds((T, D_FF), BF16),
        compiler_params=_cp(2),
    )(up_pre, up_pre, up_pre, up_pre, w_conv, w_conv, b_conv, b_conv)


def conv_act_bwd(up_pre, dact, w_conv, b_conv, *, tile, cw):
    T = up_pre.shape[0]
    nt = T // tile
    ncol = D_FF // cw
    hb = tile // HALO
    ext_rows = tile + 2 * HALO

    def body(pa_ref, a_ref, na_ref, pb_ref, b_ref, nb_ref, d_ref, nd_ref, wa_ref, wb_ref, ba_ref, bb_ref,
             dxa_ref, dxb_ref, sa_ref, sb_ref):
        i = pl.program_id(1)
        keep_p = (i > 0).astype(F32)
        keep_n = (i < nt - 1).astype(F32)

        def ext_of(prev_ref, cur_ref, next_ref):
            return jnp.concatenate(
                [prev_ref[...].astype(F32) * keep_p, cur_ref[...].astype(F32), next_ref[...].astype(F32) * keep_n], axis=0)

        def conv(ext, w, bias):
            return ext * w[2:3, :] + pltpu.roll(ext, 1, 0) * w[1:2, :] + pltpu.roll(ext, 2, 0) * w[0:1, :] + bias

        xa = ext_of(pa_ref, a_ref, na_ref)
        xb = ext_of(pb_ref, b_ref, nb_ref)
        wa, wb = wa_ref[...], wb_ref[...]
        a = conv(xa, wa, ba_ref[...])
        b = conv(xb, wb, bb_ref[...])
        dy = jnp.concatenate(
            [jnp.zeros((HALO, cw), F32), d_ref[...].astype(F32), nd_ref[...].astype(F32) * keep_n], axis=0)
        sg = _sigmoid(a)
        da = dy * b * (sg * (1.0 + a * (1.0 - sg)))
        db = dy * (a * sg)

        def back(dup, ext, w, dx_ref, s_ref):
            dx = dup * w[2:3, :] + pltpu.roll(dup, ext_rows - 1, 0) * w[1:2, :] + pltpu.roll(dup, ext_rows - 2, 0) * w[0:1, :]
            dx_ref[...] = dx[HALO:HALO + tile, :].astype(dx_ref.dtype)
            dc = dup[HALO:HALO + tile, :]
            r2 = jnp.sum(dc * ext[HALO:HALO + tile, :], axis=0, keepdims=True)
            r1 = jnp.sum(dc * pltpu.roll(ext, 1, 0)[HALO:HALO + tile, :], axis=0, keepdims=True)
            r0 = jnp.sum(dc * pltpu.roll(ext, 2, 0)[HALO:HALO + tile, :], axis=0, keepdims=True)
            rb = jnp.sum(dc, axis=0, keepdims=True)
            row = lax.broadcasted_iota(jnp.int32, (8, cw), 0)
            upd = (jnp.where(row == 0, r0, 0.0) + jnp.where(row == 1, r1, 0.0) + jnp.where(row == 2, r2, 0.0)
                   + jnp.where(row == 3, rb, 0.0))

            @pl.when(i == 0)
            def _():
                s_ref[...] = upd

            @pl.when(i > 0)
            def _():
                s_ref[...] += upd

        back(da, xa, wa, dxa_ref, sa_ref)
        back(db, xb, wb, dxb_ref, sb_ref)

    def prev_of(shift):
        return pl.BlockSpec((HALO, cw), lambda j, i: (jnp.maximum(i * hb - 1, 0), j + shift))

    def next_of(shift):
        return pl.BlockSpec((HALO, cw), lambda j, i: (jnp.minimum((i + 1) * hb, nt * hb - 1), j + shift))

    def cur_of(shift):
        return pl.BlockSpec((tile, cw), lambda j, i: (i, j + shift))

    def row_of(rows, shift):
        return pl.BlockSpec((rows, cw), lambda j, i: (0, j + shift))

    return pl.pallas_call(
        body,
        name="conv_act_bwd",
        grid=(ncol, nt),
        in_specs=[prev_of(0), cur_of(0), next_of(0), prev_of(ncol), cur_of(ncol), next_of(ncol), cur_of(0), next_of(0),
                  row_of(3, 0), row_of(3, ncol), row_of(1, 0), row_of(1, ncol)],
        out_specs=[cur_of(0), cur_of(0), row_of(8, 0), row_of(8, 0)],
        out_shape=[_sds((T, D_FF), BF16), _sds((T, D_FF), BF16), _sds((8, D_FF), F32), _sds((8, D_FF), F32)],
        compiler_params=_cp(2),
    )(up_pre, up_pre, up_pre, up_pre, up_pre, up_pre, dact, dact, w_conv, w_conv, b_conv, b_conv)


NS_IN, NS_UP, NS_XKV, NS_UQ = 564, 704, 256, 96
E2_ROWS, E2_UQ = 384, 128
L1_W = NS_UP + NS_XKV
L2_ROWS = 736
L2_OUT, L2_XQ, L2_XO, L2_DN = 0, 128, 256, 384
RS_DN = 352
L3_ROWS, L3_PRET = 1024, 512
L4_SHAPE = (8, 768)
WZ_RUNS = ((416, 4096, 0), (0, 384, 4096), (384, 32, 4544))
WZ_ZERO = ((4480, 4544), (4576, 4608))


def _pieces(orig_start, length, dst_start, ns):
    out, c, d, end = [], orig_start, dst_start, orig_start + length
    while c < end:
        j, off = c // ns, c % ns
        ln = min(ns - off, end - c)
        out.append((j, off, d, ln))
        c += ln
        d += ln
    return out


def pack_early(w_in, w_ukv, w_uq):
    def body(in_ref, ukv_ref, uq_ref, e1_ref, e2_ref):
        e1_ref[...] = in_ref[...].astype(BF16)
        e2_ref[0:E2_UQ, :] = ukv_ref[...].astype(BF16)
        e2_ref[E2_UQ:E2_ROWS, 0:NS_UQ] = uq_ref[...].astype(BF16)
        e2_ref[E2_UQ:E2_ROWS, NS_UQ:HP] = jnp.zeros((E2_ROWS - E2_UQ, HP - NS_UQ), BF16)

    return pl.pallas_call(
        body,
        name="pack_early",
        in_specs=[VM] * 3,
        out_specs=[VM] * 2,
        out_shape=[_sds((D, NS_IN), BF16), _sds((E2_ROWS, HP), BF16)],
        compiler_params=_cp0(),
    )(w_in, w_ukv, w_uq)


def pack_late(w_up, w_xkv, w_out, w_xq, w_xo, w_down, w_pmla, w_pret, w_conv):
    def body(up_ref, xkv_ref, o_ref, xq_ref, xo_ref, dn_ref, pm_ref, pr_ref, cv_ref, l1_ref, l2_ref, l3_ref, l4_ref):
        l1_ref[:, 0:NS_UP] = up_ref[...].astype(BF16)
        l1_ref[:, NS_UP:L1_W] = xkv_ref[...].astype(BF16)
        l2_ref[L2_OUT:L2_XQ, :] = o_ref[...].astype(BF16)
        l2_ref[L2_XQ:L2_XO, :] = xq_ref[...].astype(BF16)
        l2_ref[L2_XO:L2_DN, :] = xo_ref[...].astype(BF16)
        l2_ref[L2_DN:L2_ROWS, :] = dn_ref[...].astype(BF16)
        l3_ref[0:L3_PRET, :] = pm_ref[...].astype(BF16)
        l3_ref[L3_PRET:L3_ROWS, :] = pr_ref[...].astype(BF16)
        l4_ref[...] = jnp.zeros(L4_SHAPE, F32)
        l4_ref[0:3, 0:NS_UP] = cv_ref[...]

    return pl.pallas_call(
        body,
        name="pack_late",
        in_specs=[VM] * 9,
        out_specs=[VM] * 4,
        out_shape=[_sds((D, L1_W), BF16), _sds((L2_ROWS, D), BF16), _sds((L3_ROWS, HP), BF16), _sds(L4_SHAPE, F32)],
        compiler_params=_cp0(),
    )(w_up, w_xkv, w_out, w_xq, w_xo, w_down, w_pmla, w_pret, w_conv)


def assemble_early(g1, g2, *, tile):
    def body(g1_ref, g2_ref, wz_ref, wk_ref, wv_ref, wq_ref):
        for lo, hi in WZ_ZERO:
            wz_ref[:, lo:hi] = jnp.zeros((tile, hi - lo), BF16)
        for os_, ln_, ds_ in WZ_RUNS:
            for j, off, d, ln in _pieces(os_, ln_, ds_, NS_IN):
                wz_ref[:, d:d + ln] = g1_ref[j, :, off:off + ln]

        @pl.when(pl.program_id(0) == 0)
        def _():
            half = jnp.zeros((MLA_KVR, HP - MLA_NOPE), BF16)
            for j in range(N_DEV):
                wk_ref[:, j * HP:j * HP + MLA_NOPE] = g2_ref[j, 0:E2_UQ, 0:MLA_NOPE]
                wk_ref[:, j * HP + MLA_NOPE:(j + 1) * HP] = half
                wv_ref[:, j * HP:j * HP + MLA_V] = g2_ref[j, 0:E2_UQ, MLA_NOPE:HP]
                wv_ref[:, j * HP + MLA_V:(j + 1) * HP] = half
                wq_ref[:, j * HP:j * HP + NS_UQ] = g2_ref[j, E2_UQ:E2_ROWS, 0:NS_UQ]
                wq_ref[:, j * HP + NS_UQ:(j + 1) * HP] = jnp.zeros((MLA_QR, HP - NS_UQ), BF16)

    def whole(r):
        return pl.BlockSpec((r, MLA_H * HP), lambda i: (0, 0))

    return pl.pallas_call(
        body,
        name="assemble_early",
        grid=(D // tile,),
        in_specs=[pl.BlockSpec((N_DEV, tile, NS_IN), lambda i: (0, i, 0)),
                  pl.BlockSpec((N_DEV, E2_ROWS, HP), lambda i: (0, 0, 0))],
        out_specs=[pl.BlockSpec((tile, ZW), lambda i: (i, 0)), whole(MLA_KVR), whole(MLA_KVR), whole(MLA_QR)],
        out_shape=[_sds((D, ZW), BF16), _sds((MLA_KVR, MLA_H * HP), BF16), _sds((MLA_KVR, MLA_H * HP), BF16),
                   _sds((MLA_QR, MLA_H * HP), BF16)],
        compiler_params=_cp(1),
    )(g1, g2)


def assemble_l1(g1, *, tile):
    def body(g_ref, wup_ref, wxkv_ref):
        for j in range(N_DEV):
            wup_ref[:, j * NS_UP:(j + 1) * NS_UP] = g_ref[j, :, 0:NS_UP]
            wxkv_ref[:, j * NS_XKV:(j + 1) * NS_XKV] = g_ref[j, :, NS_UP:L1_W]

    return pl.pallas_call(
        body,
        name="assemble_l1",
        grid=(D // tile,),
        in_specs=[pl.BlockSpec((N_DEV, tile, L1_W), lambda i: (0, i, 0))],
        out_specs=[pl.BlockSpec((tile, 2 * D_FF), lambda i: (i, 0)), pl.BlockSpec((tile, 2 * D), lambda i: (i, 0))],
        out_shape=[_sds((D, 2 * D_FF), BF16), _sds((D, 2 * D), BF16)],
        compiler_params=_cp(1),
    )(g1)


def assemble_l234(g2, g3, g4):
    def body(g2_ref, g3_ref, g4_ref, wo_ref, wxq_ref, wxo_ref, wdn_ref, wpa_ref, wpr_ref, wc_ref):
        for j in range(N_DEV):
            wo_ref[j * 128:(j + 1) * 128, :] = g2_ref[j, L2_OUT:L2_XQ, :]
            wxq_ref[j * 128:(j + 1) * 128, :] = g2_ref[j, L2_XQ:L2_XO, :]
            wxo_ref[j * 128:(j + 1) * 128, :] = g2_ref[j, L2_XO:L2_DN, :]
            wdn_ref[j * RS_DN:(j + 1) * RS_DN, :] = g2_ref[j, L2_DN:L2_ROWS, :]
            wpr_ref[:, j * 128:(j + 1) * 128] = g3_ref[j, L3_PRET:L3_ROWS, :]
            wc_ref[:, j * NS_UP:(j + 1) * NS_UP] = g4_ref[j, 0:3, 0:NS_UP]
            for h in range(MLA_H):
                wpa_ref[h * HP:h * HP + MLA_V, j * 128:(j + 1) * 128] = g3_ref[j, h * MLA_V:(h + 1) * MLA_V, :]
        for h in range(MLA_H):
            wpa_ref[h * HP + MLA_V:(h + 1) * HP, :] = jnp.zeros((HP - MLA_V, D), BF16)

    return pl.pallas_call(
        body,
        name="assemble_l234",
        in_specs=[VM] * 3,
        out_specs=[VM] * 7,
        out_shape=[_sds((D, D), BF16), _sds((D, D), BF16), _sds((D, D), BF16), _sds((D_FF, D), BF16),
                   _sds((MLA_H * HP, D), BF16), _sds((RET_H * RET_D, D), BF16), _sds((3, 2 * D_FF), F32)],
        compiler_params=_cp0(),
    )(g2, g3, g4)


def slots_l1(dwup_a, dwup_b, dwxkv, *, tile):
    per_half = D_FF // NS_UP

    def body(ua_ref, ub_ref, x_ref, s_ref):
        for j in range(N_DEV):
            src = ua_ref if j < per_half else ub_ref
            c0 = (j % per_half) * NS_UP
            s_ref[j, :, 0:NS_UP] = src[:, c0:c0 + NS_UP].astype(BF16)
            s_ref[j, :, NS_UP:L1_W] = x_ref[:, j * NS_XKV:(j + 1) * NS_XKV].astype(BF16)

    return pl.pallas_call(
        body,
        name="slots_l1",
        grid=(D // tile,),
        in_specs=[pl.BlockSpec((tile, D_FF), lambda i: (i, 0)), pl.BlockSpec((tile, D_FF), lambda i: (i, 0)),
                  pl.BlockSpec((tile, 2 * D), lambda i: (i, 0))],
        out_specs=pl.BlockSpec((N_DEV, tile, L1_W), lambda i: (0, i, 0)),
        out_shape=_sds((N_DEV, D, L1_W), BF16),
        compiler_params=_cp(1),
    )(dwup_a, dwup_b, dwxkv)


def slots_l23(dwo, dwxq, dwxo, dwdn, dwpa, dwpr):
    def body(o_ref, xq_ref, xo_ref, dn_ref, pa_ref, pr_ref, l2_ref, l3_ref):
        l2_ref[L2_OUT:L2_XQ, :] = o_ref[...].astype(BF16)
        l2_ref[L2_XQ:L2_XO, :] = xq_ref[...].astype(BF16)
        l2_ref[L2_XO:L2_DN, :] = xo_ref[...].astype(BF16)
        l2_ref[L2_DN:L2_ROWS, :] = dn_ref[...].astype(BF16)
        for h in range(MLA_H):
            l3_ref[h * MLA_V:(h + 1) * MLA_V, :] = pa_ref[h * HP:h * HP + MLA_V, :].astype(BF16)
        l3_ref[L3_PRET:L3_ROWS, :] = pr_ref[...].astype(BF16)

    rows128 = pl.BlockSpec((128, D), lambda j: (j, 0))

    def cols(r):
        return pl.BlockSpec((r, 128), lambda j: (0, j))

    return pl.pallas_call(
        body,
        name="slots_l23",
        grid=(N_DEV,),
        in_specs=[rows128, rows128, rows128, pl.BlockSpec((RS_DN, D), lambda j: (j, 0)),
                  cols(MLA_H * HP), cols(RET_H * RET_D)],
        out_specs=[pl.BlockSpec((None, L2_ROWS, D), lambda j: (j, 0, 0)),
                   pl.BlockSpec((None, L3_ROWS, HP), lambda j: (j, 0, 0))],
        out_shape=[_sds((N_DEV, L2_ROWS, D), BF16), _sds((N_DEV, L3_ROWS, HP), BF16)],
        compiler_params=_cp(1),
    )(dwo, dwxq, dwxo, dwdn, dwpa, dwpr)


def slots_early(dwz, dwk, dwv, dwq, *, tile):
    def body(dz_ref, k_ref, v_ref, q_ref, s1_ref, s2_ref):
        for os_, ln_, ds_ in WZ_RUNS:
            for j, off, d, ln in _pieces(os_, ln_, ds_, NS_IN):
                s1_ref[j, :, off:off + ln] = dz_ref[:, d:d + ln].astype(BF16)

        @pl.when(pl.program_id(0) == 0)
        def _():
            for j in range(N_DEV):
                s2_ref[j, 0:E2_UQ, 0:MLA_NOPE] = k_ref[:, j * HP:j * HP + MLA_NOPE].astype(BF16)
                s2_ref[j, 0:E2_UQ, MLA_NOPE:HP] = v_ref[:, j * HP:j * HP + MLA_V].astype(BF16)
                s2_ref[j, E2_UQ:E2_ROWS, 0:NS_UQ] = q_ref[:, j * HP:j * HP + NS_UQ].astype(BF16)
                s2_ref[j, E2_UQ:E2_ROWS, NS_UQ:HP] = jnp.zeros((E2_ROWS - E2_UQ, HP - NS_UQ), BF16)

    def whole(r):
        return pl.BlockSpec((r, MLA_H * HP), lambda i: (0, 0))

    return pl.pallas_call(
        body,
        name="slots_early",
        grid=(D // tile,),
        in_specs=[pl.BlockSpec((tile, ZW), lambda i: (i, 0)), whole(MLA_KVR), whole(MLA_KVR), whole(MLA_QR)],
        out_specs=[pl.BlockSpec((N_DEV, tile, NS_IN), lambda i: (0, i, 0)),
                   pl.BlockSpec((N_DEV, E2_ROWS, HP), lambda i: (0, 0, 0))],
        out_shape=[_sds((N_DEV, D, NS_IN), BF16), _sds((N_DEV, E2_ROWS, HP), BF16)],
        compiler_params=_cp(1),
    )(dwz, dwk, dwv, dwq)


SMALL = (("g_mix", 1024, 0), ("b_gate", 2048, 1), ("g_q_lat", 256, 3), ("g_kv_lat", 128, 4), ("g_ret", 512, 5),
         ("g_cross", 1024, 6), ("g_mem", 1024, 7), ("g_ffn", 1024, 8), ("b_conv", 5632, 9), ("g_final", 1024, 15))
SMALL_DIRECT = tuple(s for s in SMALL if s[0] != "b_conv")
S_ROWS = 16


def _flat_pieces(n, row0):
    return [(row0 + k // D, k, min(D, n - k)) for k in range(0, n, D)]


def exchange_small(cs_a, cs_b, smalls):
    ns = len(smalls)

    def body(*refs):
        ca_ref, cb_ref = refs[:2]
        small_refs = refs[2:2 + ns]
        rd_ref, rs_ref, dsl_ref, own_ref = refs[2 + ns:6 + ns]
        sems = refs[6 + ns:]
        own_ref[...] = jnp.zeros(own_ref.shape, F32)
        for (name, n, row0), g_ref in zip(SMALL_DIRECT, small_refs):
            for r, c0, ln in _flat_pieces(n, row0):
                own_ref[r:r + 1, 0:ln] = g_ref[:, c0:c0 + ln]
        row0 = dict((s[0], s[2]) for s in SMALL)["b_conv"]
        for half, c_ref in enumerate((ca_ref, cb_ref)):
            k = half * D_FF
            end = k + D_FF
            while k < end:
                r, lane = row0 + k // D, k % D
                ln = min(D - lane, end - k)
                own_ref[r:r + 1, lane:lane + ln] = c_ref[3:4, k - half * D_FF:k - half * D_FF + ln]
                k += ln
        dsl_ref[...] = jnp.zeros(dsl_ref.shape, F32)
        per_half = D_FF // NS_UP
        for j in range(N_DEV):
            c_ref = ca_ref if j < per_half else cb_ref
            c0 = (j % per_half) * NS_UP
            dsl_ref[j, 0:3, 0:NS_UP] = c_ref[0:3, c0:c0 + NS_UP]
        comm = ([(dsl_ref, True), (own_ref, False)], [rd_ref, rs_ref]) + tuple(sems)
        _exchange_start(*comm)
        _exchange_wait(*comm)

    n1 = N_DEV - 1
    return pl.pallas_call(
        body,
        name="exchange_small",
        in_specs=[VM] * (2 + ns),
        out_specs=[VM, VM, VM, VM],
        out_shape=[_sds((n1,) + L4_SHAPE, F32), _sds((n1, S_ROWS, D), F32), _sds((N_DEV,) + L4_SHAPE, F32),
                   _sds((S_ROWS, D), F32)],
        scratch_shapes=_exchange_scratch(2),
        compiler_params=_cp0(),
    )(cs_a, cs_b, *smalls)


def _adamw(w, g, m, v):
    m = ADAM_B1 * m + (1.0 - ADAM_B1) * g
    v = ADAM_B2 * v + (1.0 - ADAM_B2) * (g * g)
    m_hat = m / (1.0 - ADAM_B1 ** ADAM_STEP)
    v_hat = v / (1.0 - ADAM_B2 ** ADAM_STEP)
    delta = -ADAM_LR * (m_hat / (jnp.sqrt(v_hat) + ADAM_EPS) + ADAM_WD * w)
    return delta, m, v


def _apply(g, refs, outs):
    d, mn, vn = _adamw(refs[0][...], g, refs[1][...], refs[2][...])
    outs[0][...] = g
    outs[1][...] = d
    outs[2][...] = mn
    outs[3][...] = vn


def adam_cols(own, recv, wmv, spans, *, name, tile):
    R, W = own.shape
    nw = len(spans)

    def body(*refs):
        own_ref, recv_ref = refs[:2]
        ins, outs = refs[2:2 + 3 * nw], refs[2 + 3 * nw:]
        g = own_ref[...].astype(F32)
        for k in range(N_DEV - 1):
            g = g + recv_ref[k].astype(F32)
        for t, (lo, hi) in enumerate(spans):
            _apply(g[:, lo:hi], ins[3 * t:3 * t + 3], outs[4 * t:4 * t + 4])

    def blk(w):
        return pl.BlockSpec((tile, w), lambda i: (i, 0))

    widths = [hi - lo for lo, hi in spans]
    return pl.pallas_call(
        body,
        name=name,
        grid=(R // tile,),
        in_specs=[blk(W), pl.BlockSpec((N_DEV - 1, tile, W), lambda i: (0, i, 0))] + [blk(w) for w in widths for _ in range(3)],
        out_specs=[blk(w) for w in widths for _ in range(4)],
        out_shape=[_sds((R, w), F32) for w in widths for _ in range(4)],
        compiler_params=_cp(1),
    )(own, recv, *wmv)


def adam_rows(own, recv, wmv, spans, *, name):
    nw = len(spans)

    def body(*refs):
        own_ref, recv_ref = refs[:2]
        ins, outs = refs[2:2 + 3 * nw], refs[2 + 3 * nw:]
        for t, (lo, hi, w) in enumerate(spans):
            g = own_ref[lo:hi, :].astype(F32)
            for k in range(N_DEV - 1):
                g = g + recv_ref[k, lo:hi, :].astype(F32)
            _apply(g[:, 0:w], ins[3 * t:3 * t + 3], outs[4 * t:4 * t + 4])

    return pl.pallas_call(
        body,
        name=name,
        in_specs=[VM] * (2 + 3 * nw),
        out_specs=[VM] * (4 * nw),
        out_shape=[_sds((hi - lo, w), F32) for lo, hi, w in spans for _ in range(4)],
        compiler_params=_cp0(),
    )(own, recv, *wmv)


def adam_small(own_s, recv_s, dslots, recv_d, wmv_small, wmv_conv):
    ns = len(SMALL)

    def body(*refs):
        own_ref, rs_ref, dsl_ref, rd_ref = refs[:4]
        ins = refs[4:4 + 3 * ns + 3]
        outs = refs[4 + 3 * ns + 3:4 + 3 * ns + 3 + 4 * ns + 4]
        all_sc = refs[-1]
        me = _my_id()
        all_sc[0] = own_ref[...]
        for k in range(N_DEV - 1):
            all_sc[k + 1] = rs_ref[k]
        g = all_sc[jnp.bitwise_xor(me, 0)]
        for s in range(1, N_DEV):
            g = g + all_sc[jnp.bitwise_xor(me, s)]
        all_sc[0] = g
        for t, (name, n, row0) in enumerate(SMALL):
            pieces = [all_sc[0, r:r + 1, 0:ln] for r, _, ln in _flat_pieces(n, row0)]
            gt = pieces[0] if len(pieces) == 1 else jnp.concatenate(pieces, axis=1)
            _apply(gt, ins[3 * t:3 * t + 3], outs[4 * t:4 * t + 4])
        gc = dsl_ref[me]
        for k in range(N_DEV - 1):
            gc = gc + rd_ref[k]
        _apply(gc[0:3, 0:NS_UP], ins[3 * ns:3 * ns + 3], outs[4 * ns:4 * ns + 4])

    out_shape = [_sds((1, n), F32) for _, n, _ in SMALL for _ in range(4)] + [_sds((3, NS_UP), F32)] * 4
    return pl.pallas_call(
        body,
        name="adam_small",
        in_specs=[VM] * (4 + 3 * ns + 3),
        out_specs=[VM] * len(out_shape),
        out_shape=out_shape,
        scratch_shapes=[pltpu.VMEM((N_DEV, S_ROWS, D), F32)],
    )(own_s, recv_s, dslots, recv_d, *wmv_small, *wmv_conv)


def kernel(x, mem, positions, g_mix, w_in, b_gate, g_q_lat, w_uq, g_kv_lat, w_ukv, w_proj_mla, g_ret, w_proj_ret, w_out, g_cross, g_mem, w_xq, w_xkv, w_xo, g_ffn, w_up, w_conv, b_conv, w_down, g_final, loss_target, m_g_mix, m_w_in, m_b_gate, m_g_q_lat, m_w_uq, m_g_kv_lat, m_w_ukv, m_w_proj_mla, m_g_ret, m_w_proj_ret, m_w_out, m_g_cross, m_g_mem, m_w_xq, m_w_xkv, m_w_xo, m_g_ffn, m_w_up, m_w_conv, m_b_conv, m_w_down, m_g_final, v_g_mix, v_w_in, v_b_gate, v_g_q_lat, v_w_uq, v_g_kv_lat, v_w_ukv, v_w_proj_mla, v_g_ret, v_w_proj_ret, v_w_out, v_g_cross, v_g_mem, v_w_xq, v_w_xkv, v_w_xo, v_g_ffn, v_w_up, v_w_conv, v_b_conv, v_w_down, v_g_final):
    args = dict(locals())
    T = x.shape[1]
    M = mem.shape[1]
    tile = min(256, T)
    tile2 = min(512, T)
    tile4 = min(1024, T)
    tq = min(512, T)
    rb = min(1024, T)

    xs = x[0]
    tgt = loss_target[0]
    mems = mem[0]

    def shard(name, prefix=""):
        a = args[prefix + name]
        return a.reshape(a.shape[-2:]) if a.ndim >= 2 else a.reshape(1, -1)

    e1, e2 = pack_early(shard("w_in"), shard("w_ukv"), shard("w_uq"))
    late_parts = pack_late(shard("w_up"), shard("w_xkv"), shard("w_out"), shard("w_xq"), shard("w_xo"), shard("w_down"),
                           shard("w_proj_mla"), shard("w_proj_ret"), shard("w_conv"))

    pos_f = jnp.broadcast_to(positions[0].astype(F32)[:, None], (T, 128))
    inv_m, inv_r = _rot_inv()
    u, cm, s1, s2, cr, sr, ge1, ge2 = rowwise(
        lambda xv, p, im, ir, g: (_rms(xv, g),) + _rot_tables(p, im, ir), [(xs, None), (pos_f, None)],
        [inv_m, inv_r, g_mix], [(D, BF16)] + [(128, F32)] * 5, [], tile=tile2, name="norm_mix_tables", gather=[e1, e2])
    wz, wk, wv, wq = assemble_early(ge1, ge2, tile=tile)
    rconsts = _ret_consts()

    z = matmul(u, wz, name="mm_z", tn=1536)

    def lat_fwd(zl, cmv, s1v, s2v, gq, gkv, wqv, wkv_, wvv):
        cq = _rms(zl[:, 0:256], gq).astype(BF16)
        ckv = _rms(zl[:, 256:384], gkv).astype(BF16)
        qv = _rot_mla(_dot(cq, wqv), cmv, s1v, s2v)
        kr = _rot_mla(zl[:, 384:512], cmv, s1v, s2v)
        kn = _dot(ckv, wkv_)
        kv_ = jnp.concatenate([kn[:, h * HP:(h + 1) * HP] + kr for h in range(MLA_H)], axis=1)
        vv = _dot(ckv, wvv)
        lane = lax.broadcasted_iota(jnp.int32, vv.shape, 1)
        vv = jnp.where((lane & (HP - 1)) == ONE_LANE, 1.0, vv)
        return qv, kv_, vv

    q_a, k_a, v_a = rowwise(
        lat_fwd, [(z, (512, 8)), (cm, None), (s1, None), (s2, None)], [g_q_lat, g_kv_lat, wq, wk, wv],
        [(MLA_H * HP, BF16)] * 3, [], tile=tile2, name="lat_fwd")

    def retprep_fwd(zr, crv, srv):
        rqv = _rot_ret(zr[:, 0:512], crv, srv)
        rkv = _rot_ret(zr[:, 512:1024], crv, srv) * (RET_D ** -0.5)
        return rqv, rkv, zr[:, 1024:1536]

    rq, rk, rv = rowwise(retprep_fwd, [(z, (2048, 0)), (cr, None), (sr, None)], [], [(512, BF16)] * 3, [],
                         tile=tile2, name="retprep_fwd")

    o_a, lse, gl1, gl2, gl3, gl4 = mla_fwd(q_a, k_a, v_a, list(late_parts), tq=tq)
    wup, wxkv = assemble_l1(gl1, tile=tile)
    wo, wxq, wxo, wdn, wpa, wpr, wcv = assemble_l234(gl2, gl3, gl4)
    ret, rstate = ret_fwd(rq, rk, rv, rconsts, rb=rb)

    def gn_parts(r):
        outs = []
        for h in range(RET_H):
            rh = r[:, h * RET_D:(h + 1) * RET_D]
            mu = jnp.mean(rh, axis=-1, keepdims=True)
            dlt = rh - mu
            rstd = lax.rsqrt(jnp.mean(dlt * dlt, axis=-1, keepdims=True) + EPS)
            outs.append((dlt * rstd, rstd))
        return outs

    def mix_fwd(ov, rv_, rg, gt, wpav, wprv, gr, bg):
        ya = _dot(ov, wpav)
        xh = jnp.concatenate([p[0] for p in gn_parts(rv_)], axis=1)
        t = rg * _sigmoid(rg) * (xh * gr)
        yr = _dot(t.astype(BF16), wprv)
        ga_ = _sigmoid(gt[:, :D] + bg[:, :D])
        gr_ = _sigmoid(gt[:, D:] + bg[:, D:])
        return ga_ * ya + gr_ * yr

    mix = rowwise(mix_fwd, [(o_a, None), (ret, None), (z, (512, 3)), (z, (2048, 1))], [wpa, wpr, g_ret, b_gate],
                  [(D, BF16)], [], tile=tile, name="mix_fwd")[0]
    def proj_norm(av, rv_, wv_, g):
        hv = rv_ + _dot(av, wv_)
        return hv, _rms(hv, g)

    h1, n2 = rowwise(proj_norm, [(mix, None), (xs, None)], [wo, g_cross], [(D, F32), (D, BF16)], [], tile=tile4,
                     name="mm_out_norm")
    xq = matmul(n2, wxq, out_dtype=BF16, name="mm_xq")
    mn = rowwise(lambda mv_, g: _rms(mv_, g), [(mems, None)], [g_mem], [(D, BF16)], [], tile=min(tile, M), name="norm_mem")[0]
    mkv = matmul(mn, wxkv, out_dtype=BF16, name="mm_mkv")

    x_scale = X_HD ** -0.5

    def xattn_fwd(xqv, mkvv):
        outs = []
        for h in range(X_H):
            sl = slice(h * X_HD, (h + 1) * X_HD)
            s = _dot_nt(xqv[:, sl], mkvv[:, sl]) * x_scale
            s = s - jnp.max(s, axis=-1, keepdims=True)
            e = jnp.exp(s)
            p = e / jnp.sum(e, axis=-1, keepdims=True)
            outs.append(_dot(p.astype(BF16), mkvv[:, D + h * X_HD:D + (h + 1) * X_HD]))
        return jnp.concatenate(outs, axis=1)

    xo = rowwise(xattn_fwd, [(xq, None)], [mkv], [(D, BF16)], [], tile=tile, name="xattn_fwd")[0]
    h2, n3 = rowwise(proj_norm, [(xo, None), (h1, None)], [wxo, g_ffn], [(D, F32), (D, BF16)], [], tile=tile4,
                     name="mm_xo_norm")
    up_pre = matmul(n3, wup, out_dtype=BF16, name="mm_up", tn=1408)
    cw = D_FF // 2
    act = conv_act_fwd(up_pre, wcv, b_conv, tile=tile, cw=cw)

    def down_loss(av, hv2, tv, wv_, g):
        hv = hv2 + _dot(av, wv_)
        y = _rms(hv, g)
        err = y - tv
        part = 0.5 * jnp.sum(jnp.sum(err * err, axis=-1, keepdims=True) / D, axis=0, keepdims=True)
        dx, dg = _rms_bwd(hv, g, err / D)
        return dx, dg, jnp.broadcast_to(part, (8, 128))

    g_fin2 = g_final.reshape(1, D)
    dh3, dg_final, loss_acc = rowwise(down_loss, [(act, None), (h2, None), (tgt, None)], [wdn, g_fin2], [(D, F32)],
                                      [((1, D), F32), ((8, 128), F32)], tile=tile2, name="mm_down_loss")
    loss = lax.psum(loss_acc[0, 0], ("x", "y", "c"))

    dact = matmul(dh3, wdn, tb=True, out_dtype=BF16, name="mm_dact", tn=1408)
    dw_down = matmul_tn(act, dh3, name="mm_dw_down", tm=1408)
    dup_a, dup_b, cs_a, cs_b = conv_act_bwd(up_pre, dact, wcv, b_conv, tile=tile, cw=cw)
    dn3 = matmul2_tb(dup_a, dup_b, wup, name="mm_dn3", tm=1024, tn=512)
    dw_up_a = matmul_tn(n3, dup_a, name="mm_dw_up_a", tn=1408)
    dw_up_b = matmul_tn(n3, dup_b, name="mm_dw_up_b", tn=1408)

    def norm_bwd(hv, dyv, drv, g):
        dx, dg = _rms_bwd(hv, g, dyv)
        return dx + drv, dg

    dh2, dg_ffn = rowwise(norm_bwd, [(h2, None), (dn3, None), (dh3, None)], [g_ffn], [(D, F32)], [((1, D), F32)],
                          tile=tile2, name="norm_ffn_bwd")
    dxo = matmul(dh2, wxo, tb=True, out_dtype=BF16, name="mm_dxo")
    dw_xo = matmul_tn(xo, dh2, name="mm_dw_xo")

    def xattn_bwd(xqv, dxov, mkvv):
        dxq, dmk, dmv = [], [], []
        for h in range(X_H):
            sl = slice(h * X_HD, (h + 1) * X_HD)
            slv = slice(D + h * X_HD, D + (h + 1) * X_HD)
            s = _dot_nt(xqv[:, sl], mkvv[:, sl]) * x_scale
            s = s - jnp.max(s, axis=-1, keepdims=True)
            e = jnp.exp(s)
            p = e / jnp.sum(e, axis=-1, keepdims=True)
            dp = _dot_nt(dxov[:, sl], mkvv[:, slv])
            ds = (p * (dp - jnp.sum(dp * p, axis=-1, keepdims=True)) * x_scale).astype(BF16)
            dxq.append(_dot(ds, mkvv[:, sl]))
            dmk.append(_dot_tn(ds, xqv[:, sl]))
            dmv.append(_dot_tn(p.astype(BF16), dxov[:, sl]))
        return jnp.concatenate(dxq, axis=1), jnp.concatenate(dmk + dmv, axis=1)

    dxq, dmkv = rowwise(xattn_bwd, [(xq, None), (dxo, None)], [mkv], [(D, BF16)], [((M, 2 * D), F32)],
                        tile=tile, name="xattn_bwd")
    dw_xq = matmul_tn(n2, dxq, name="mm_dw_xq")

    def proj_norm_bwd(dyv, hv, drv, wv_, g):
        dx, dg = _rms_bwd(hv, g, _dot_nt(dyv, wv_))
        return dx + drv, dg

    dh1, dg_cross = rowwise(proj_norm_bwd, [(dxq, None), (h1, None), (dh2, None)], [wxq, g_cross], [(D, F32)],
                            [((1, D), F32)], tile=tile4, name="mm_dn2_norm_bwd")
    dw_xkv = matmul_tn(mn, dmkv, name="mm_dw_xkv", tk=M)
    dmn = matmul(dmkv, wxkv, tb=True, name="mm_dmn", tm=M)
    dg_mem = rowwise(lambda mv_, dyv, g: _rms_bwd(mv_, g, dyv)[1], [(mems, None), (dmn, None)], [g_mem], [],
                     [((1, D), F32)], tile=min(tile, M), name="norm_mem_bwd")[0]

    dmix = matmul(dh1, wo, tb=True, out_dtype=BF16, name="mm_dmix")
    dw_out = matmul_tn(mix, dh1, name="mm_dw_out")

    def mix_bwd(ov, rv_, rg, gt, dmv_, wpav, wprv, gr, bg):
        dm_ = dmv_.astype(F32)
        ya = _dot(ov, wpav)
        parts = gn_parts(rv_)
        xh = jnp.concatenate([p[0] for p in parts], axis=1)
        yn = xh * gr
        sg = _sigmoid(rg)
        sl_ = rg * sg
        t = (sl_ * yn).astype(BF16)
        yr = _dot(t, wprv)
        ga_ = _sigmoid(gt[:, :D] + bg[:, :D])
        gr_ = _sigmoid(gt[:, D:] + bg[:, D:])
        dgates = jnp.concatenate([dm_ * ya * ga_ * (1.0 - ga_), dm_ * yr * gr_ * (1.0 - gr_)], axis=1)
        dya = (dm_ * ga_).astype(BF16)
        dyr = (dm_ * gr_).astype(BF16)
        do_ = _dot_nt(dya, wpav)
        dwpa_ = _dot_tn(ov, dya)
        dt = _dot_nt(dyr, wprv)
        dwpr_ = _dot_tn(t, dyr)
        drg_ = dt * yn * (sg * (1.0 + rg * (1.0 - sg)))
        dyn = dt * sl_
        dgr = jnp.sum(dyn * xh, axis=0, keepdims=True)
        dxh = dyn * gr
        drets = []
        for h in range(RET_H):
            sl = slice(h * RET_D, (h + 1) * RET_D)
            xhh, rstd = parts[h]
            dxhh = dxh[:, sl]
            drets.append(rstd * (dxhh - jnp.mean(dxhh, axis=-1, keepdims=True)
                                 - xhh * jnp.mean(dxhh * xhh, axis=-1, keepdims=True)))
        dret_ = jnp.concatenate(drets, axis=1)
        dbg = jnp.sum(dgates, axis=0, keepdims=True)
        return do_, dret_, drg_, dgates, dwpa_, dwpr_, dgr, dbg

    do_a, dret, drg, dgates, dwpa, dw_proj_ret, dg_ret, db_gate = rowwise(
        mix_bwd, [(o_a, None), (ret, None), (z, (512, 3)), (z, (2048, 1)), (dmix, None)], [wpa, wpr, g_ret, b_gate],
        [(MLA_H * HP, BF16), (512, F32), (512, BF16), (2 * D, BF16)],
        [((MLA_H * HP, D), F32), ((512, D), F32), ((1, 512), F32), ((1, 2 * D), F32)], tile=tile, name="mix_bwd")

    sl1 = slots_l1(dw_up_a, dw_up_b, dw_xkv, tile=tile)
    sl2, sl3 = slots_l23(dw_out, dw_xq, dw_xo, dw_down, dwpa, dw_proj_ret)
    lse_row, delta_row = mla_prep(o_a, do_a, lse, tq=tq)
    dq_a, dk_a, dv_a, rl1, rl2, rl3 = mla_bwd(q_a, k_a, v_a, do_a, lse_row, delta_row, [sl1, sl2, sl3], tq=tq)
    drq_r, drk_r, drv = ret_bwd(rq, rk, rv, rstate, dret, rconsts, rb=rb)

    def lat_bwd(zl, cmv, s1v, s2v, dqv, dkv_, dvv, gq, gkv, wqv, wkv_, wvv):
        cqf, ckvf = zl[:, 0:256], zl[:, 256:384]
        cq = _rms(cqf, gq).astype(BF16)
        ckv = _rms(ckvf, gkv).astype(BF16)
        dq_pre = _rot_mla(dqv.astype(F32), cmv, -s1v, -s2v).astype(BF16)
        dkf = dkv_.astype(F32)
        dkr = dkf[:, 0:HP]
        for h in range(1, MLA_H):
            dkr = dkr + dkf[:, h * HP:(h + 1) * HP]
        lane = lax.broadcasted_iota(jnp.int32, dkr.shape, 1)
        dzk = _rot_mla(jnp.where((lane >= 64) & (lane < 96), dkr, 0.0), cmv, -s1v, -s2v)
        dkb = dkv_.astype(BF16)
        dvb = dvv.astype(BF16)
        dcq_n = _dot_nt(dq_pre, wqv)
        dckv_n = _dot_nt(dkb, wkv_) + _dot_nt(dvb, wvv)
        dwq_ = _dot_tn(cq, dq_pre)
        dwk_ = _dot_tn(ckv, dkb)
        dwv_ = _dot_tn(ckv, dvb)
        dcq, dgq = _rms_bwd(cqf, gq, dcq_n)
        dckv, dgkv = _rms_bwd(ckvf, gkv, dckv_n)
        return jnp.concatenate([dcq, dckv, dzk], axis=1), dwq_, dwk_, dwv_, dgq, dgkv

    dz_lat, dwq, dwk, dwv, dg_q_lat, dg_kv_lat = rowwise(
        lat_bwd, [(z, (512, 8)), (cm, None), (s1, None), (s2, None), (dq_a, None), (dk_a, None), (dv_a, None)],
        [g_q_lat, g_kv_lat, wq, wk, wv], [(512, BF16)],
        [((MLA_QR, MLA_H * HP), F32), ((MLA_KVR, MLA_H * HP), F32), ((MLA_KVR, MLA_H * HP), F32),
         ((1, MLA_QR), F32), ((1, MLA_KVR), F32)], tile=tile2, name="lat_bwd")

    def dz_assemble(dq_, dk_, dv_, drg_, dgt, dzl, crv, srv):
        a = _rot_ret(dq_, crv, -srv)
        b = _rot_ret(dk_, crv, -srv) * (RET_D ** -0.5)
        return jnp.concatenate([a, b, dv_, drg_.astype(F32), dgt.astype(F32), dzl.astype(F32)], axis=1)

    dz = rowwise(dz_assemble, [(drq_r, None), (drk_r, None), (drv, None), (drg, None), (dgates, None), (dz_lat, None),
                               (cr, None), (sr, None)], [], [(ZW, BF16)], [], tile=tile2, name="dz_assemble")[0]
    dwz = matmul_tn(u, dz, name="mm_dw_z", tn=1536)
    se1, se2 = slots_early(dwz, dwk, dwv, dwq, tile=tile)
    du, re1, re2 = matmul(dz, wz, tb=True, name="mm_du", tm=1024, tn=512, exchange=[se1, se2])

    grad_x, dg_mix = rowwise(norm_bwd, [(xs, None), (du, None), (dh1, None)], [g_mix], [(D, F32)], [((1, D), F32)],
                             tile=tile2, name="norm_mix_bwd")

    small_grads = {"g_mix": dg_mix, "b_gate": db_gate, "g_q_lat": dg_q_lat, "g_kv_lat": dg_kv_lat, "g_ret": dg_ret,
                   "g_cross": dg_cross, "g_mem": dg_mem, "g_ffn": dg_ffn, "g_final": dg_final}
    rd, rs, dslots, own_s = exchange_small(cs_a, cs_b, [small_grads[n] for n, _, _ in SMALL_DIRECT])

    me = _my_id()

    def own(slots):
        return lax.dynamic_index_in_dim(slots, me, axis=0, keepdims=False)

    def wmv(names):
        return [shard(n, p) for n in names for p in ("", "m_", "v_")]

    names_s = tuple(n for n, _, _ in SMALL)
    groups = (
        (("w_in",), adam_cols(own(se1), re1, wmv(("w_in",)), ((0, NS_IN),), name="adam_e1", tile=128)),
        (("w_ukv", "w_uq"), adam_rows(own(se2), re2, wmv(("w_ukv", "w_uq")),
                                      ((0, E2_UQ, HP), (E2_UQ, E2_ROWS, NS_UQ)), name="adam_e2")),
        (("w_up", "w_xkv"), adam_cols(own(sl1), rl1, wmv(("w_up", "w_xkv")), ((0, NS_UP), (NS_UP, L1_W)),
                                      name="adam_l1", tile=128)),
        (("w_out", "w_xq", "w_xo", "w_down"),
         adam_rows(own(sl2), rl2, wmv(("w_out", "w_xq", "w_xo", "w_down")),
                   ((L2_OUT, L2_XQ, D), (L2_XQ, L2_XO, D), (L2_XO, L2_DN, D), (L2_DN, L2_ROWS, D)), name="adam_l2")),
        (("w_proj_mla", "w_proj_ret"), adam_rows(own(sl3), rl3, wmv(("w_proj_mla", "w_proj_ret")),
                                                 ((0, L3_PRET, HP), (L3_PRET, L3_ROWS, HP)), name="adam_l3")),
        (names_s + ("w_conv",), adam_small(own_s, rs, dslots, rd, wmv(names_s), wmv(("w_conv",)))),
    )
    res = {}
    for names, outs_ in groups:
        for t, n in enumerate(names):
            res[n] = outs_[4 * t:4 * t + 4]

    order = ["g_mix", "w_in", "b_gate", "g_q_lat", "w_uq", "g_kv_lat", "w_ukv", "w_proj_mla", "g_ret", "w_proj_ret",
             "w_out", "g_cross", "g_mem", "w_xq", "w_xkv", "w_xo", "g_ffn", "w_up", "w_conv", "b_conv", "w_down",
             "g_final"]
    outs = [loss, grad_x[None]]
    for kind in range(4):
        outs += [res[n][kind].reshape(args[n].shape) for n in order]
    return tuple(outs)
```

```python
import functools
import math

import jax
import jax.numpy as jnp
import numpy as np
from jax import lax
from jax.experimental import pallas as pl
from jax.experimental.pallas import tpu as pltpu

F32 = jnp.float32
BF16 = jnp.bfloat16

D = 1024
MLA_H, MLA_NOPE, MLA_ROPE, MLA_V = 8, 64, 32, 64
MLA_QR, MLA_KVR = 256, 128
RET_H, RET_D, RET_C = 4, 128, 128
X_H, X_HD = 4, 256
D_FF = 2816
THETA = 10000.0
EPS = 1e-6
HP = 128
ZW = 4608
N_DEV = 8

ADAM_LR, ADAM_B1, ADAM_B2, ADAM_EPS, ADAM_WD, ADAM_STEP = 0.001, 0.9, 0.999, 1e-08, 0.01, 10

VMEM_LIMIT = 56 * 1024 * 1024
MESH = pl.DeviceIdType.MESH
VM = pl.BlockSpec(memory_space=pltpu.VMEM)
ANY = pl.BlockSpec(memory_space=pl.ANY)


def _cp(n_axes):
    return pltpu.CompilerParams(dimension_semantics=("arbitrary",) * n_axes, vmem_limit_bytes=VMEM_LIMIT)


def _cp0():
    return pltpu.CompilerParams(vmem_limit_bytes=VMEM_LIMIT)


def _pick(n, cap, mult=128):
    best = None
    for t in range(mult, min(n, cap) + 1, mult):
        if n % t == 0:
            best = t
    return best if best is not None else n


def _dot(a, b):
    return jnp.dot(a, b, preferred_element_type=F32)


def _dot_nt(a, b):
    return lax.dot_general(a, b, (((1,), (1,)), ((), ())), preferred_element_type=F32)


def _dot_tn(a, b):
    return lax.dot_general(a, b, (((0,), (0,)), ((), ())), preferred_element_type=F32)


def _sds(shape, dtype):
    return jax.ShapeDtypeStruct(shape, dtype)


def matmul(a, b, *, name, tb=False, res=None, out_dtype=F32, tm=1024, tn=1024, exchange=()):
    M, K = a.shape
    N = b.shape[0] if tb else b.shape[1]
    tm = _pick(M, tm, 8)
    tn = _pick(N, tn)
    gi, gj = M // tm, N // tn
    ns = len(exchange)
    n_in = 2 if res is None else 3

    def body(*refs):
        a_ref, b_ref = refs[:2]
        o_ref = refs[n_in + ns]
        if ns:
            comm = ([(r, True) for r in refs[n_in:n_in + ns]], refs[n_in + ns + 1:n_in + 2 * ns + 1]) + tuple(
                refs[n_in + 2 * ns + 1:])
            i, j = pl.program_id(0), pl.program_id(1)

            @pl.when((i == 0) & (j == 0))
            def _():
                _exchange_start(*comm)

        av = a_ref[...].astype(BF16)
        bv = b_ref[...].astype(BF16)
        acc = _dot_nt(av, bv) if tb else _dot(av, bv)
        if res is not None:
            acc = acc + refs[2][...].astype(F32)
        o_ref[...] = acc.astype(o_ref.dtype)
        if ns:

            @pl.when((i == gi - 1) & (j == gj - 1))
            def _():
                _exchange_wait(*comm)

    in_specs = [
        pl.BlockSpec((tm, K), lambda i, j: (i, 0)),
        pl.BlockSpec((tn, K), lambda i, j: (j, 0)) if tb else pl.BlockSpec((K, tn), lambda i, j: (0, j)),
    ]
    args = [a, b]
    if res is not None:
        in_specs.append(pl.BlockSpec((tm, tn), lambda i, j: (i, j)))
        args.append(res)
    out_spec = pl.BlockSpec((tm, tn), lambda i, j: (i, j))
    out_shape = _sds((M, N), out_dtype)
    return pl.pallas_call(
        body,
        name=name,
        grid=(gi, gj),
        in_specs=in_specs + [ANY] * ns,
        out_specs=[out_spec] + [ANY] * ns if ns else out_spec,
        out_shape=[out_shape] + [_sds((N_DEV - 1,) + s.shape[1:], s.dtype) for s in exchange] if ns else out_shape,
        scratch_shapes=_exchange_scratch(ns) if ns else [],
        compiler_params=_cp(2),
    )(*args, *exchange)


def matmul2_tb(a1, a2, b, *, name, tm=512, tn=512):
    M, K1 = a1.shape
    N = b.shape[0]
    tm = _pick(M, tm, 8)
    tn = _pick(N, tn)

    def body(a1_ref, a2_ref, b1_ref, b2_ref, o_ref):
        o_ref[...] = (_dot_nt(a1_ref[...].astype(BF16), b1_ref[...].astype(BF16))
                      + _dot_nt(a2_ref[...].astype(BF16), b2_ref[...].astype(BF16)))

    return pl.pallas_call(
        body,
        name=name,
        grid=(M // tm, N // tn),
        in_specs=[pl.BlockSpec((tm, K1), lambda i, j: (i, 0)), pl.BlockSpec((tm, K1), lambda i, j: (i, 0)),
                  pl.BlockSpec((tn, K1), lambda i, j: (j, 0)), pl.BlockSpec((tn, K1), lambda i, j: (j, 1))],
        out_specs=pl.BlockSpec((tm, tn), lambda i, j: (i, j)),
        out_shape=_sds((M, N), F32),
        compiler_params=_cp(2),
    )(a1, a2, b, b)


def matmul_tn(a, b, *, name, tm=1024, tn=1024, tk=1024):
    R, M = a.shape
    N = b.shape[1]
    tm = _pick(M, tm)
    tn = _pick(N, tn)
    tk = _pick(R, tk, 16)
    nk = R // tk

    def body(a_ref, b_ref, o_ref, acc_ref):
        k = pl.program_id(2)

        @pl.when(k == 0)
        def _():
            acc_ref[...] = jnp.zeros_like(acc_ref)

        acc_ref[...] += _dot_tn(a_ref[...].astype(BF16), b_ref[...].astype(BF16))

        @pl.when(k == nk - 1)
        def _():
            o_ref[...] = acc_ref[...]

    return pl.pallas_call(
        body,
        name=name,
        grid=(M // tm, N // tn, nk),
        in_specs=[pl.BlockSpec((tk, tm), lambda i, j, k: (k, i)), pl.BlockSpec((tk, tn), lambda i, j, k: (k, j))],
        out_specs=pl.BlockSpec((tm, tn), lambda i, j, k: (i, j)),
        out_shape=_sds((M, N), F32),
        scratch_shapes=[pltpu.VMEM((tm, tn), F32)],
        compiler_params=_cp(3),
    )(a, b)


def rowwise(fn, rows, consts, out_rows, out_accs, *, tile, name, gather=()):
    T = rows[0][0].shape[0]
    nt = T // tile
    n_r, n_c, n_o, n_a, n_g = len(rows), len(consts), len(out_rows), len(out_accs), len(gather)
    n_in = n_r + n_c + n_g

    def body(*refs):
        if n_g:
            comm = (refs[n_r + n_c:n_in], refs[n_in + n_o + n_a:n_in + n_o + n_a + n_g]) + tuple(
                refs[n_in + n_o + n_a + n_g:])

            @pl.when(pl.program_id(0) == 0)
            def _():
                _gather2_start(*comm)

        ins = [r[...] for r in refs[: n_r + n_c]]
        outs = fn(*ins)
        if not isinstance(outs, (tuple, list)):
            outs = (outs,)
        o_refs = refs[n_in:n_in + n_o]
        a_refs = refs[n_in + n_o:n_in + n_o + n_a]
        for o_ref, o in zip(o_refs, outs[:n_o]):
            o_ref[...] = o.astype(o_ref.dtype)
        if n_a:
            first = pl.program_id(0) == 0

            @pl.when(first)
            def _():
                for a_ref, o in zip(a_refs, outs[n_o:]):
                    a_ref[...] = o.astype(a_ref.dtype)

            @pl.when(jnp.logical_not(first))
            def _():
                for a_ref, o in zip(a_refs, outs[n_o:]):
                    a_ref[...] += o.astype(a_ref.dtype)
        if n_g:

            @pl.when(pl.program_id(0) == nt - 1)
            def _():
                _gather2_finish(*comm)

    in_specs = []
    args = []
    for arr, win in rows:
        if win is None:
            in_specs.append(pl.BlockSpec((tile, arr.shape[1]), lambda i: (i, 0)))
        else:
            w, cb = win
            in_specs.append(pl.BlockSpec((tile, w), functools.partial(lambda i, cb: (i, cb), cb=cb)))
        args.append(arr)
    for c in consts:
        in_specs.append(pl.BlockSpec(c.shape, functools.partial(lambda i, nd: (0,) * nd, nd=c.ndim)))
        args.append(c)
    out_specs = [pl.BlockSpec((tile, w), lambda i: (i, 0)) for w, _ in out_rows]
    out_shape = [_sds((T, w), dt) for w, dt in out_rows]
    for shp, dt in out_accs:
        out_specs.append(pl.BlockSpec(shp, functools.partial(lambda i, nd: (0,) * nd, nd=len(shp))))
        out_shape.append(_sds(shp, dt))
    return pl.pallas_call(
        body,
        name=name,
        grid=(nt,),
        in_specs=in_specs + [ANY] * n_g,
        out_specs=out_specs + [ANY] * n_g,
        out_shape=out_shape + [_sds((N_DEV,) + p.shape, p.dtype) for p in gather],
        scratch_shapes=_gather_scratch(n_g) if n_g else [],
        compiler_params=_cp(1),
    )(*args, *gather)


def _rms(x, g):
    r = lax.rsqrt(jnp.mean(x * x, axis=-1, keepdims=True) + EPS)
    return x * r * g


def _rms_bwd(x, g, dy):
    r = lax.rsqrt(jnp.mean(x * x, axis=-1, keepdims=True) + EPS)
    xh = x * r
    dg = jnp.sum(dy * xh, axis=0, keepdims=True)
    dxh = dy * g
    dx = r * (dxh - xh * jnp.mean(dxh * xh, axis=-1, keepdims=True))
    return dx, dg


def _sigmoid(x):
    return 0.5 * jnp.tanh(0.5 * x) + 0.5


def _rot_mla(x, c, s1, s2):
    n = x.shape[1] // HP
    outs = []
    for h in range(n):
        xh = x[:, h * HP : (h + 1) * HP]
        outs.append(xh * c + pltpu.roll(xh, HP - 16, 1) * s1 + pltpu.roll(xh, 16, 1) * s2)
    return outs[0] if n == 1 else jnp.concatenate(outs, axis=1)


def _rot_ret(x, c, s):
    n = x.shape[1] // RET_D
    outs = []
    for h in range(n):
        xh = x[:, h * RET_D : (h + 1) * RET_D]
        outs.append(xh * c + pltpu.roll(xh, RET_D // 2, 1) * s)
    return outs[0] if n == 1 else jnp.concatenate(outs, axis=1)


def _rot_inv():
    lane_np = np.arange(128)
    inv_m = (jnp.asarray(THETA, F32) ** (-jnp.asarray(lane_np & 15, F32) / 16.0)).reshape(1, 128)
    inv_r = (jnp.asarray(THETA, F32) ** (-jnp.asarray(lane_np & 63, F32) / 64.0)).reshape(1, 128)
    return inv_m, inv_r


def _rot_tables(p, im, ir):
    lane = lax.broadcasted_iota(jnp.int32, p.shape, 1)
    ang = p * im
    cm = jnp.where((lane >= 64) & (lane < 96), jnp.cos(ang), 1.0)
    sn = jnp.sin(ang)
    s1 = jnp.where((lane >= 64) & (lane < 80), -sn, 0.0)
    s2 = jnp.where((lane >= 80) & (lane < 96), sn, 0.0)
    angr = p * ir
    snr = jnp.sin(angr)
    return cm, s1, s2, jnp.cos(angr), jnp.where(lane < 64, -snr, snr)


def _peer(m):
    x, y, c = lax.axis_index("x"), lax.axis_index("y"), lax.axis_index("c")
    mx, my, mc = (m >> 2) & 1, (m >> 1) & 1, m & 1
    px = 1 - x if mx else x
    py = 1 - y if my else y
    pc = 1 - c if mc else c
    return (px, py, pc), 4 * px + 2 * py + pc


def _my_id():
    return 4 * lax.axis_index("x") + 2 * lax.axis_index("y") + lax.axis_index("c")


def _gather_copies(srcs, outs, send_sems, recv_sems, local_sems, arriving=False):
    me = _my_id()
    copies = []
    if not arriving:
        for g, (src, out) in enumerate(zip(srcs, outs)):
            copies.append((pltpu.make_async_copy(src, out.at[me], local_sems.at[g]), False))
    for m in range(1, N_DEV):
        peer, plin = _peer(m)
        for g, (src, out) in enumerate(zip(srcs, outs)):
            copies.append((pltpu.make_async_remote_copy(
                src_ref=src, dst_ref=out.at[plin if arriving else me], send_sem=send_sems.at[g, m - 1],
                recv_sem=recv_sems.at[g, m - 1], device_id=peer, device_id_type=MESH), True))
    return copies


def _gather_start(*a):
    for cp, _ in _gather_copies(*a):
        cp.start()


def _gather_wait(*a):
    for cp, _ in _gather_copies(*a, arriving=True):
        cp.wait_recv()
    for cp, remote in _gather_copies(*a):
        if remote:
            cp.wait_send()
        else:
            cp.wait()


CHIP_RELATIONS = (4, 2, 6)


def _gather2_copy(src, out, block, to, send_sems, recv_sems, g, k):
    return pltpu.make_async_remote_copy(src_ref=src, dst_ref=out.at[block], send_sem=send_sems.at[g, k],
                                        recv_sem=recv_sems.at[g, k], device_id=to, device_id_type=MESH)


def _gather2_start(srcs, outs, send_sems, recv_sems, local_sems):
    me = _my_id()
    sib, _ = _peer(1)
    for g, (src, out) in enumerate(zip(srcs, outs)):
        pltpu.make_async_copy(src, out.at[me], local_sems.at[g]).start()
        _gather2_copy(src, out, me, sib, send_sems, recv_sems, g, 0).start()
        for t, m in enumerate(CHIP_RELATIONS):
            _gather2_copy(src, out, me, _peer(m)[0], send_sems, recv_sems, g, 1 + t).start()


def _gather2_finish(srcs, outs, send_sems, recv_sems, local_sems):
    me = _my_id()
    sib, sib_lin = _peer(1)
    for t, m in enumerate(CHIP_RELATIONS):
        peer, plin = _peer(m)
        for g, (src, out) in enumerate(zip(srcs, outs)):
            _gather2_copy(src, out, plin, peer, send_sems, recv_sems, g, 1 + t).wait_recv()
            _gather2_copy(out.at[plin], out, plin, sib, send_sems, recv_sems, g, 4 + t).start()
    for g, (src, out) in enumerate(zip(srcs, outs)):
        _gather2_copy(src, out, sib_lin, sib, send_sems, recv_sems, g, 0).wait_recv()
        for t, m in enumerate(CHIP_RELATIONS):
            _gather2_copy(src, out, _peer(m | 1)[1], sib, send_sems, recv_sems, g, 4 + t).wait_recv()
    for g, (src, out) in enumerate(zip(srcs, outs)):
        _gather2_copy(src, out, me, sib, send_sems, recv_sems, g, 0).wait_send()
        for t, m in enumerate(CHIP_RELATIONS):
            peer, plin = _peer(m)
            _gather2_copy(src, out, me, peer, send_sems, recv_sems, g, 1 + t).wait_send()
            _gather2_copy(out.at[plin], out, plin, sib, send_sems, recv_sems, g, 4 + t).wait_send()
        pltpu.make_async_copy(src, out.at[me], local_sems.at[g]).wait()


def _gather_scratch(n):
    return [pltpu.SemaphoreType.DMA((n, N_DEV - 1)), pltpu.SemaphoreType.DMA((n, N_DEV - 1)), pltpu.SemaphoreType.DMA((n,))]


def _exchange_copies(srcs, dsts, send_sems, recv_sems):
    copies = []
    for m in range(1, N_DEV):
        peer, plin = _peer(m)
        for g, ((src, per_peer), dst) in enumerate(zip(srcs, dsts)):
            copies.append(pltpu.make_async_remote_copy(
                src_ref=src.at[plin] if per_peer else src, dst_ref=dst.at[m - 1], send_sem=send_sems.at[g, m - 1],
                recv_sem=recv_sems.at[g, m - 1], device_id=peer, device_id_type=MESH))
    return copies


def _exchange_start(*a):
    for cp in _exchange_copies(*a):
        cp.start()


def _exchange_wait(*a):
    copies = _exchange_copies(*a)
    for cp in copies:
        cp.wait_recv()
    for cp in copies:
        cp.wait_send()


def _exchange_scratch(n):
    return [pltpu.SemaphoreType.DMA((n, N_DEV - 1)), pltpu.SemaphoreType.DMA((n, N_DEV - 1))]


MLA_SCALE = (MLA_NOPE + MLA_ROPE) ** -0.5
MLA_C2 = MLA_SCALE * math.log2(math.e)
ONE_LANE = MLA_V
NEG = -1e30
HPS = 4
HPB = 2


def _tri_mask(n, lower_rows_ge_cols=True):
    r = lax.broadcasted_iota(jnp.int32, (n, n), 0)
    c = lax.broadcasted_iota(jnp.int32, (n, n), 1)
    return r >= c if lower_rows_ge_cols else c >= r


def mla_fwd(q, k, v, gather, *, tq):
    T = q.shape[0]
    nq = T // tq
    rep = tq // HP
    ng = len(gather)

    def body(*refs):
        q_ref, k_ref, v_ref = refs[:3]
        srcs = refs[3:3 + ng]
        o_ref, lse_ref = refs[3 + ng:5 + ng]
        outs = refs[5 + ng:5 + 2 * ng]
        m_sc, acc_sc = refs[5 + 2 * ng:7 + 2 * ng]
        comm = (srcs, outs) + tuple(refs[7 + 2 * ng:])
        h, i = pl.program_id(0), pl.program_id(1)

        @pl.when((h == 0) & (i == 0))
        def _():
            _gather_start(*comm)

        m_sc[...] = jnp.full(m_sc.shape, NEG, F32)
        acc_sc[...] = jnp.zeros(acc_sc.shape, F32)
        heads = [slice(t * HP, (t + 1) * HP) for t in range(HPS)]

        def block(j, masked):
            off = pl.multiple_of(j * tq, tq)
            ss = [_dot_nt(q_ref[:, hd], k_ref[pl.ds(off, tq), hd]) * MLA_C2 for hd in heads]
            for hd, s in zip(heads, ss):
                if masked:
                    s = jnp.where(_tri_mask(tq), s, NEG)
                m_prev = m_sc[:, hd]
                m_next = jnp.maximum(m_prev, jnp.max(s, axis=-1, keepdims=True))
                p = jnp.exp2(s - jnp.tile(m_next, (1, rep)))
                alpha = jnp.exp2(m_prev - m_next)
                acc_sc[:, hd] = alpha * acc_sc[:, hd] + _dot(p.astype(BF16), v_ref[pl.ds(off, tq), hd])
                m_sc[:, hd] = m_next

        def loop_body(j, carry):
            block(j, False)
            return carry

        lax.fori_loop(0, i, loop_body, 0)
        block(i, True)
        for hd in heads:
            acc = acc_sc[:, hd]
            l = acc[:, ONE_LANE:ONE_LANE + 1]
            o_ref[:, hd] = (acc / l).astype(o_ref.dtype)
            lse_ref[:, hd] = m_sc[:, hd] + jnp.log(l) * math.log2(math.e)

        @pl.when((h == MLA_H // HPS - 1) & (i == nq - 1))
        def _():
            _gather_wait(*comm)

    blk = pl.BlockSpec((tq, HPS * HP), lambda h, i: (i, h))
    full = pl.BlockSpec((T, HPS * HP), lambda h, i: (0, h))
    return pl.pallas_call(
        body,
        name="mla_fwd",
        grid=(MLA_H // HPS, nq),
        in_specs=[blk, full, full] + [ANY] * ng,
        out_specs=[blk, blk] + [ANY] * ng,
        out_shape=[_sds((T, MLA_H * HP), BF16), _sds((T, MLA_H * HP), F32)]
        + [_sds((N_DEV,) + p.shape, p.dtype) for p in gather],
        scratch_shapes=[pltpu.VMEM((tq, HPS * HP), F32), pltpu.VMEM((tq, HPS * HP), F32)] + _gather_scratch(ng),
        compiler_params=_cp(2),
    )(q, k, v, *gather)


def mla_prep(o, do, lse, *, tq):
    T = o.shape[0]
    nq = T // tq

    def body(o_ref, do_ref, lse_ref, lse_row_ref, delta_row_ref):
        for h in range(MLA_H):
            sl = slice(h * HP, (h + 1) * HP)
            d = jnp.sum(o_ref[:, sl].astype(F32) * do_ref[:, sl].astype(F32), axis=-1, keepdims=True)
            lse_row_ref[h] = lse_ref[:, sl].T[0:1, :]
            delta_row_ref[h] = jnp.broadcast_to(d, (tq, HP)).T[0:1, :]

    blk = pl.BlockSpec((tq, MLA_H * HP), lambda i: (i, 0))
    row = pl.BlockSpec((MLA_H, None, 1, tq), lambda i: (0, i, 0, 0))
    return pl.pallas_call(
        body,
        name="mla_prep",
        grid=(nq,),
        in_specs=[blk, blk, blk],
        out_specs=[row, row],
        out_shape=[_sds((MLA_H, nq, 1, tq), F32), _sds((MLA_H, nq, 1, tq), F32)],
        compiler_params=_cp(1),
    )(o, do, lse)


def mla_bwd(q, k, v, do, lse_row, delta_row, slots, *, tq):
    T = q.shape[0]
    nq = T // tq
    ns = len(slots)

    def body(*refs):
        q_ref, k_ref, v_ref, do_ref, lse_ref, delta_ref = refs[:6]
        srcs = [(r, True) for r in refs[6:6 + ns]]
        dq_ref, dk_ref, dv_ref = refs[6 + ns:9 + ns]
        dsts = refs[9 + ns:9 + 2 * ns]
        dk_sc, dv_sc = refs[9 + 2 * ns:11 + 2 * ns]
        comm = (srcs, dsts) + tuple(refs[11 + 2 * ns:])
        h, j = pl.program_id(0), pl.program_id(1)

        @pl.when((h == 0) & (j == 0))
        def _():
            _exchange_start(*comm)

        dk_sc[...] = jnp.zeros(dk_sc.shape, F32)
        dv_sc[...] = jnp.zeros(dv_sc.shape, F32)
        heads = [slice(t * HP, (t + 1) * HP) for t in range(HPB)]

        @pl.when(j == 0)
        def _():
            dq_ref[...] = jnp.zeros(dq_ref.shape, F32)

        def block(i, masked):
            off = pl.multiple_of(i * tq, tq)
            sts = [_dot_nt(k_ref[:, hd], q_ref[pl.ds(off, tq), hd]) * MLA_C2 for hd in heads]
            dpts = [_dot_nt(v_ref[:, hd], do_ref[pl.ds(off, tq), hd]) for hd in heads]
            for t, hd in enumerate(heads):
                st = sts[t]
                if masked:
                    st = jnp.where(_tri_mask(tq, False), st, NEG)
                pt = jnp.exp2(st - lse_ref[t, i])
                dv_sc[:, hd] += _dot(pt.astype(BF16), do_ref[pl.ds(off, tq), hd])
                dst = (pt * (dpts[t] - delta_ref[t, i]) * MLA_SCALE).astype(BF16)
                dk_sc[:, hd] += _dot(dst, q_ref[pl.ds(off, tq), hd])
                dq_ref[pl.ds(off, tq), hd] += _dot_tn(dst, k_ref[:, hd])

        block(j, True)

        def loop_body(i, carry):
            block(i, False)
            return carry

        lax.fori_loop(j + 1, nq, loop_body, 0)
        dk_ref[...] = dk_sc[...].astype(dk_ref.dtype)
        dv_ref[...] = dv_sc[...].astype(dv_ref.dtype)

        @pl.when((h == MLA_H // HPB - 1) & (j == nq - 1))
        def _():
            _exchange_wait(*comm)

    blk = pl.BlockSpec((tq, HPB * HP), lambda h, j: (j, h))
    full = pl.BlockSpec((T, HPB * HP), lambda h, j: (0, h))
    rows = pl.BlockSpec((HPB, nq, 1, tq), lambda h, j: (h, 0, 0, 0))
    return pl.pallas_call(
        body,
        name="mla_bwd",
        grid=(MLA_H // HPB, nq),
        in_specs=[full, blk, blk, full, rows, rows] + [ANY] * ns,
        out_specs=[full, blk, blk] + [ANY] * ns,
        out_shape=[_sds((T, MLA_H * HP), F32), _sds((T, MLA_H * HP), BF16), _sds((T, MLA_H * HP), BF16)]
        + [_sds((N_DEV - 1,) + s.shape[1:], s.dtype) for s in slots],
        scratch_shapes=[pltpu.VMEM((tq, HPB * HP), F32), pltpu.VMEM((tq, HPB * HP), F32)] + _exchange_scratch(ns),
        compiler_params=_cp(2),
    )(q, k, v, do, lse_row, delta_row, *slots)


def _ret_consts():
    h = jnp.arange(RET_H, dtype=F32)
    log_g = jnp.log1p(-jnp.exp2(-5.0 - h))
    idx = jnp.arange(RET_C, dtype=F32)
    rel = idx[:, None] - idx[None, :]
    dmask = jnp.where(rel >= 0, jnp.exp(log_g[:, None, None] * jnp.maximum(rel, 0.0)), 0.0)
    zeta = jnp.exp(log_g[:, None] * (RET_C - 1.0 - idx)[None, :])
    xi = jnp.exp(log_g[:, None] * (idx + 1.0)[None, :])
    decay = jnp.exp(log_g * RET_C)
    zb = jnp.broadcast_to(zeta[:, :, None], (RET_H, RET_C, RET_D))
    xb = jnp.broadcast_to(xi[:, :, None], (RET_H, RET_C, RET_D))
    db = jnp.broadcast_to(decay[:, None, None], (RET_H, RET_C, RET_D))
    return dmask.astype(F32), zb.astype(F32), xb.astype(F32), db.astype(F32)


def ret_fwd(rq, rk, rv, consts, *, rb):
    T = rq.shape[0]
    nb = T // rb
    ncb = rb // RET_C

    def body(q_ref, k_ref, v_ref, dm_ref, z_ref, x_ref, dc_ref, o_ref, st_ref, r_sc):
        @pl.when(pl.program_id(1) == 0)
        def _():
            r_sc[...] = jnp.zeros(r_sc.shape, F32)

        dm, zt, xi, dc = dm_ref[...], z_ref[...], x_ref[...], dc_ref[...]
        for c in range(ncb):
            sl = slice(c * RET_C, (c + 1) * RET_C)
            q, k, v = q_ref[sl, :], k_ref[sl, :], v_ref[sl, :]
            r = r_sc[...]
            rbf = r.astype(BF16)
            st_ref[sl, :] = rbf
            s = _dot_nt(q, k) * dm
            inner = _dot(s.astype(BF16), v)
            cross = _dot((q.astype(F32) * xi).astype(BF16), rbf)
            o_ref[sl, :] = inner + cross
            kz = (k.astype(F32) * zt).T.astype(BF16)
            r_sc[...] = r * dc + _dot(kz, v)

    blk = pl.BlockSpec((rb, RET_D), lambda h, b: (b, h))
    cst = pl.BlockSpec((None, RET_C, RET_D), lambda h, b: (h, 0, 0))
    return pl.pallas_call(
        body,
        name="ret_fwd",
        grid=(RET_H, nb),
        in_specs=[blk, blk, blk, cst, cst, cst, cst],
        out_specs=[blk, blk],
        out_shape=[_sds((T, RET_H * RET_D), F32), _sds((T, RET_H * RET_D), BF16)],
        scratch_shapes=[pltpu.VMEM((RET_D, RET_D), F32)],
        compiler_params=_cp(2),
    )(rq, rk, rv, *consts)


def ret_bwd(rq, rk, rv, st, dret, consts, *, rb):
    T = rq.shape[0]
    nb = T // rb
    ncb = rb // RET_C

    def body(q_ref, k_ref, v_ref, st_ref, do_ref, dm_ref, z_ref, x_ref, dc_ref, dq_ref, dk_ref, dv_ref, g_sc):
        @pl.when(pl.program_id(1) == 0)
        def _():
            g_sc[...] = jnp.zeros(g_sc.shape, F32)

        dm, zt, xi, dc = dm_ref[...], z_ref[...], x_ref[...], dc_ref[...]
        for c in reversed(range(ncb)):
            sl = slice(c * RET_C, (c + 1) * RET_C)
            q, k, v, rp = q_ref[sl, :], k_ref[sl, :], v_ref[sl, :], st_ref[sl, :]
            dob = do_ref[sl, :].astype(BF16)
            qf, kf = q.astype(F32), k.astype(F32)
            gn = g_sc[...]
            gnb = gn.astype(BF16)
            s = _dot_nt(q, k) * dm
            ds = _dot_nt(dob, v) * dm
            dq = _dot(ds.astype(BF16), k) + _dot_nt(dob, rp) * xi
            dk = _dot(ds.T.astype(BF16), q) + _dot_nt(v, gnb) * zt
            dv = _dot(s.T.astype(BF16), dob) + _dot((kf * zt).astype(BF16), gnb)
            dq_ref[sl, :] = dq.astype(dq_ref.dtype)
            dk_ref[sl, :] = dk.astype(dk_ref.dtype)
            dv_ref[sl, :] = dv.astype(dv_ref.dtype)
            g_sc[...] = _dot((qf * xi).T.astype(BF16), dob) + dc * gn

    blk = pl.BlockSpec((rb, RET_D), lambda h, b: (nb - 1 - b, h))
    cst = pl.BlockSpec((None, RET_C, RET_D), lambda h, b: (h, 0, 0))
    return pl.pallas_call(
        body,
        name="ret_bwd",
        grid=(RET_H, nb),
        in_specs=[blk, blk, blk, blk, blk, cst, cst, cst, cst],
        out_specs=[blk, blk, blk],
        out_shape=[_sds((T, RET_H * RET_D), F32)] * 3,
        scratch_shapes=[pltpu.VMEM((RET_D, RET_D), F32)],
        compiler_params=_cp(2),
    )(rq, rk, rv, st, dret, *consts)


HALO = 16


def conv_act_fwd(up_pre, w_conv, b_conv, *, tile, cw):
    T = up_pre.shape[0]
    nt = T // tile
    ncol = D_FF // cw
    hb = tile // HALO

    def body(pa_ref, a_ref, pb_ref, b_ref, wa_ref, wb_ref, ba_ref, bb_ref, o_ref):
        i = pl.program_id(1)
        keep = (i > 0).astype(F32)

        def conv(prev_ref, cur_ref, w_ref, bias_ref):
            ext = jnp.concatenate([prev_ref[...].astype(F32) * keep, cur_ref[...].astype(F32)], axis=0)
            w = w_ref[...]
            y = ext * w[2:3, :] + pltpu.roll(ext, 1, 0) * w[1:2, :] + pltpu.roll(ext, 2, 0) * w[0:1, :] + bias_ref[...]
            return y[HALO:, :]

        a = conv(pa_ref, a_ref, wa_ref, ba_ref)
        b = conv(pb_ref, b_ref, wb_ref, bb_ref)
        o_ref[...] = (a * _sigmoid(a) * b).astype(o_ref.dtype)

    prev_a = pl.BlockSpec((HALO, cw), lambda j, i: (jnp.maximum(i * hb - 1, 0), j))
    cur_a = pl.BlockSpec((tile, cw), lambda j, i: (i, j))
    prev_b = pl.BlockSpec((HALO, cw), lambda j, i: (jnp.maximum(i * hb - 1, 0), j + ncol))
    cur_b = pl.BlockSpec((tile, cw), lambda j, i: (i, j + ncol))
    w_a = pl.BlockSpec((3, cw), lambda j, i: (0, j))
    w_b = pl.BlockSpec((3, cw), lambda j, i: (0, j + ncol))
    bias_a = pl.BlockSpec((1, cw), lambda j, i: (0, j))
    bias_b = pl.BlockSpec((1, cw), lambda j, i: (0, j + ncol))
    return pl.pallas_call(
        body,
        name="conv_act_fwd",
        grid=(ncol, nt),
        in_specs=[prev_a, cur_a, prev_b, cur_b, w_a, w_b, bias_a, bias_b],
        out_specs=pl.BlockSpec((tile, cw), lambda j, i: (i, j)),
        out_shape=_sds((T, D_FF), BF16),
        compiler_params=_cp(2),
    )(up_pre, up_pre, up_pre, up_pre, w_conv, w_conv, b_conv, b_conv)


def conv_act_bwd(up_pre, dact, w_conv, b_conv, *, tile, cw):
    T = up_pre.shape[0]
    nt = T // tile
    ncol = D_FF // cw
    hb = tile // HALO
    ext_rows = tile + 2 * HALO

    def body(pa_ref, a_ref, na_ref, pb_ref, b_ref, nb_ref, d_ref, nd_ref, wa_ref, wb_ref, ba_ref, bb_ref,
             dxa_ref, dxb_ref, sa_ref, sb_ref):
        i = pl.program_id(1)
        keep_p = (i > 0).astype(F32)
        keep_n = (i < nt - 1).astype(F32)

        def ext_of(prev_ref, cur_ref, next_ref):
            return jnp.concatenate(
                [prev_ref[...].astype(F32) * keep_p, cur_ref[...].astype(F32), next_ref[...].astype(F32) * keep_n], axis=0)

        def taps(ext):
            return ext, pltpu.roll(ext, 1, 0), pltpu.roll(ext, 2, 0)

        def conv(tp, w, bias):
            return tp[0] * w[2:3, :] + tp[1] * w[1:2, :] + tp[2] * w[0:1, :] + bias

        xa = taps(ext_of(pa_ref, a_ref, na_ref))
        xb = taps(ext_of(pb_ref, b_ref, nb_ref))
        wa, wb = wa_ref[...], wb_ref[...]
        a = conv(xa, wa, ba_ref[...])
        b = conv(xb, wb, bb_ref[...])
        dy = jnp.concatenate(
            [jnp.zeros((HALO, cw), F32), d_ref[...].astype(F32), nd_ref[...].astype(F32) * keep_n], axis=0)
        sg = _sigmoid(a)
        da = dy * b * (sg * (1.0 + a * (1.0 - sg)))
        db = dy * (a * sg)

        def back(dup, tp, w, dx_ref, s_ref):
            dx = dup * w[2:3, :] + pltpu.roll(dup, ext_rows - 1, 0) * w[1:2, :] + pltpu.roll(dup, ext_rows - 2, 0) * w[0:1, :]
            dx_ref[...] = dx[HALO:HALO + tile, :].astype(dx_ref.dtype)
            dc = dup[HALO:HALO + tile, :]
            r2 = jnp.sum(dc * tp[0][HALO:HALO + tile, :], axis=0, keepdims=True)
            r1 = jnp.sum(dc * tp[1][HALO:HALO + tile, :], axis=0, keepdims=True)
            r0 = jnp.sum(dc * tp[2][HALO:HALO + tile, :], axis=0, keepdims=True)
            rb = jnp.sum(dc, axis=0, keepdims=True)
            row = lax.broadcasted_iota(jnp.int32, (8, cw), 0)
            upd = (jnp.where(row == 0, r0, 0.0) + jnp.where(row == 1, r1, 0.0) + jnp.where(row == 2, r2, 0.0)
                   + jnp.where(row == 3, rb, 0.0))

            @pl.when(i == 0)
            def _():
                s_ref[...] = upd

            @pl.when(i > 0)
            def _():
                s_ref[...] += upd

        back(da, xa, wa, dxa_ref, sa_ref)
        back(db, xb, wb, dxb_ref, sb_ref)

    def prev_of(shift):
        return pl.BlockSpec((HALO, cw), lambda j, i: (jnp.maximum(i * hb - 1, 0), j + shift))

    def next_of(shift):
        return pl.BlockSpec((HALO, cw), lambda j, i: (jnp.minimum((i + 1) * hb, nt * hb - 1), j + shift))

    def cur_of(shift):
        return pl.BlockSpec((tile, cw), lambda j, i: (i, j + shift))

    def row_of(rows, shift):
        return pl.BlockSpec((rows, cw), lambda j, i: (0, j + shift))

    return pl.pallas_call(
        body,
        name="conv_act_bwd",
        grid=(ncol, nt),
        in_specs=[prev_of(0), cur_of(0), next_of(0), prev_of(ncol), cur_of(ncol), next_of(ncol), cur_of(0), next_of(0),
                  row_of(3, 0), row_of(3, ncol), row_of(1, 0), row_of(1, ncol)],
        out_specs=[cur_of(0), cur_of(0), row_of(8, 0), row_of(8, 0)],
        out_shape=[_sds((T, D_FF), BF16), _sds((T, D_FF), BF16), _sds((8, D_FF), F32), _sds((8, D_FF), F32)],
        compiler_params=_cp(2),
    )(up_pre, up_pre, up_pre, up_pre, up_pre, up_pre, dact, dact, w_conv, w_conv, b_conv, b_conv)


NS_IN, NS_UP, NS_XKV, NS_UQ = 564, 704, 256, 96
E2_ROWS, E2_UQ = 384, 128
L1_W = NS_UP + NS_XKV
L2_ROWS = 736
L2_OUT, L2_XQ, L2_XO, L2_DN = 0, 128, 256, 384
RS_DN = 352
L3_ROWS, L3_PRET = 1024, 512
L4_SHAPE = (8, 768)
WZ_RUNS = ((416, 4096, 0), (0, 384, 4096), (384, 32, 4544))
WZ_ZERO = ((4480, 4544), (4576, 4608))


def _pieces(orig_start, length, dst_start, ns):
    out, c, d, end = [], orig_start, dst_start, orig_start + length
    while c < end:
        j, off = c // ns, c % ns
        ln = min(ns - off, end - c)
        out.append((j, off, d, ln))
        c += ln
        d += ln
    return out


def pack_early(w_in, w_ukv, w_uq):
    def body(in_ref, ukv_ref, uq_ref, e1_ref, e2_ref):
        e1_ref[...] = in_ref[...].astype(BF16)
        e2_ref[0:E2_UQ, :] = ukv_ref[...].astype(BF16)
        e2_ref[E2_UQ:E2_ROWS, 0:NS_UQ] = uq_ref[...].astype(BF16)
        e2_ref[E2_UQ:E2_ROWS, NS_UQ:HP] = jnp.zeros((E2_ROWS - E2_UQ, HP - NS_UQ), BF16)

    return pl.pallas_call(
        body,
        name="pack_early",
        in_specs=[VM] * 3,
        out_specs=[VM] * 2,
        out_shape=[_sds((D, NS_IN), BF16), _sds((E2_ROWS, HP), BF16)],
        compiler_params=_cp0(),
    )(w_in, w_ukv, w_uq)


def pack_late(w_up, w_xkv, w_out, w_xq, w_xo, w_down, w_pmla, w_pret, w_conv):
    def body(up_ref, xkv_ref, o_ref, xq_ref, xo_ref, dn_ref, pm_ref, pr_ref, cv_ref, l1_ref, l2_ref, l3_ref, l4_ref):
        l1_ref[:, 0:NS_UP] = up_ref[...].astype(BF16)
        l1_ref[:, NS_UP:L1_W] = xkv_ref[...].astype(BF16)
        l2_ref[L2_OUT:L2_XQ, :] = o_ref[...].astype(BF16)
        l2_ref[L2_XQ:L2_XO, :] = xq_ref[...].astype(BF16)
        l2_ref[L2_XO:L2_DN, :] = xo_ref[...].astype(BF16)
        l2_ref[L2_DN:L2_ROWS, :] = dn_ref[...].astype(BF16)
        l3_ref[0:L3_PRET, :] = pm_ref[...].astype(BF16)
        l3_ref[L3_PRET:L3_ROWS, :] = pr_ref[...].astype(BF16)
        l4_ref[...] = jnp.zeros(L4_SHAPE, F32)
        l4_ref[0:3, 0:NS_UP] = cv_ref[...]

    return pl.pallas_call(
        body,
        name="pack_late",
        in_specs=[VM] * 9,
        out_specs=[VM] * 4,
        out_shape=[_sds((D, L1_W), BF16), _sds((L2_ROWS, D), BF16), _sds((L3_ROWS, HP), BF16), _sds(L4_SHAPE, F32)],
        compiler_params=_cp0(),
    )(w_up, w_xkv, w_out, w_xq, w_xo, w_down, w_pmla, w_pret, w_conv)


def assemble_early(g1, g2, *, tile):
    def body(g1_ref, g2_ref, wz_ref, wk_ref, wv_ref, wq_ref):
        for lo, hi in WZ_ZERO:
            wz_ref[:, lo:hi] = jnp.zeros((tile, hi - lo), BF16)
        for os_, ln_, ds_ in WZ_RUNS:
            for j, off, d, ln in _pieces(os_, ln_, ds_, NS_IN):
                wz_ref[:, d:d + ln] = g1_ref[j, :, off:off + ln]

        @pl.when(pl.program_id(0) == 0)
        def _():
            half = jnp.zeros((MLA_KVR, HP - MLA_NOPE), BF16)
            for j in range(N_DEV):
                wk_ref[:, j * HP:j * HP + MLA_NOPE] = g2_ref[j, 0:E2_UQ, 0:MLA_NOPE]
                wk_ref[:, j * HP + MLA_NOPE:(j + 1) * HP] = half
                wv_ref[:, j * HP:j * HP + MLA_V] = g2_ref[j, 0:E2_UQ, MLA_NOPE:HP]
                wv_ref[:, j * HP + MLA_V:(j + 1) * HP] = half
                wq_ref[:, j * HP:j * HP + NS_UQ] = g2_ref[j, E2_UQ:E2_ROWS, 0:NS_UQ]
                wq_ref[:, j * HP + NS_UQ:(j + 1) * HP] = jnp.zeros((MLA_QR, HP - NS_UQ), BF16)

    def whole(r):
        return pl.BlockSpec((r, MLA_H * HP), lambda i: (0, 0))

    return pl.pallas_call(
        body,
        name="assemble_early",
        grid=(D // tile,),
        in_specs=[pl.BlockSpec((N_DEV, tile, NS_IN), lambda i: (0, i, 0)),
                  pl.BlockSpec((N_DEV, E2_ROWS, HP), lambda i: (0, 0, 0))],
        out_specs=[pl.BlockSpec((tile, ZW), lambda i: (i, 0)), whole(MLA_KVR), whole(MLA_KVR), whole(MLA_QR)],
        out_shape=[_sds((D, ZW), BF16), _sds((MLA_KVR, MLA_H * HP), BF16), _sds((MLA_KVR, MLA_H * HP), BF16),
                   _sds((MLA_QR, MLA_H * HP), BF16)],
        compiler_params=_cp(1),
    )(g1, g2)


def assemble_l1(g1, *, tile):
    def body(g_ref, wup_ref, wxkv_ref):
        for j in range(N_DEV):
            wup_ref[:, j * NS_UP:(j + 1) * NS_UP] = g_ref[j, :, 0:NS_UP]
            wxkv_ref[:, j * NS_XKV:(j + 1) * NS_XKV] = g_ref[j, :, NS_UP:L1_W]

    return pl.pallas_call(
        body,
        name="assemble_l1",
        grid=(D // tile,),
        in_specs=[pl.BlockSpec((N_DEV, tile, L1_W), lambda i: (0, i, 0))],
        out_specs=[pl.BlockSpec((tile, 2 * D_FF), lambda i: (i, 0)), pl.BlockSpec((tile, 2 * D), lambda i: (i, 0))],
        out_shape=[_sds((D, 2 * D_FF), BF16), _sds((D, 2 * D), BF16)],
        compiler_params=_cp(1),
    )(g1)


def assemble_l234(g2, g3, g4):
    def body(g2_ref, g3_ref, g4_ref, wo_ref, wxq_ref, wxo_ref, wdn_ref, wpa_ref, wpr_ref, wc_ref):
        for j in range(N_DEV):
            wo_ref[j * 128:(j + 1) * 128, :] = g2_ref[j, L2_OUT:L2_XQ, :]
            wxq_ref[j * 128:(j + 1) * 128, :] = g2_ref[j, L2_XQ:L2_XO, :]
            wxo_ref[j * 128:(j + 1) * 128, :] = g2_ref[j, L2_XO:L2_DN, :]
            wdn_ref[j * RS_DN:(j + 1) * RS_DN, :] = g2_ref[j, L2_DN:L2_ROWS, :]
            wpr_ref[:, j * 128:(j + 1) * 128] = g3_ref[j, L3_PRET:L3_ROWS, :]
            wc_ref[:, j * NS_UP:(j + 1) * NS_UP] = g4_ref[j, 0:3, 0:NS_UP]
            for h in range(MLA_H):
                wpa_ref[h * HP:h * HP + MLA_V, j * 128:(j + 1) * 128] = g3_ref[j, h * MLA_V:(h + 1) * MLA_V, :]
        for h in range(MLA_H):
            wpa_ref[h * HP + MLA_V:(h + 1) * HP, :] = jnp.zeros((HP - MLA_V, D), BF16)

    return pl.pallas_call(
        body,
        name="assemble_l234",
        in_specs=[VM] * 3,
        out_specs=[VM] * 7,
        out_shape=[_sds((D, D), BF16), _sds((D, D), BF16), _sds((D, D), BF16), _sds((D_FF, D), BF16),
                   _sds((MLA_H * HP, D), BF16), _sds((RET_H * RET_D, D), BF16), _sds((3, 2 * D_FF), F32)],
        compiler_params=_cp0(),
    )(g2, g3, g4)


def slots_l1(dwup_a, dwup_b, dwxkv, *, tile):
    per_half = D_FF // NS_UP

    def body(ua_ref, ub_ref, x_ref, s_ref):
        for j in range(N_DEV):
            src = ua_ref if j < per_half else ub_ref
            c0 = (j % per_half) * NS_UP
            s_ref[j, :, 0:NS_UP] = src[:, c0:c0 + NS_UP].astype(BF16)
            s_ref[j, :, NS_UP:L1_W] = x_ref[:, j * NS_XKV:(j + 1) * NS_XKV].astype(BF16)

    return pl.pallas_call(
        body,
        name="slots_l1",
        grid=(D // tile,),
        in_specs=[pl.BlockSpec((tile, D_FF), lambda i: (i, 0)), pl.BlockSpec((tile, D_FF), lambda i: (i, 0)),
                  pl.BlockSpec((tile, 2 * D), lambda i: (i, 0))],
        out_specs=pl.BlockSpec((N_DEV, tile, L1_W), lambda i: (0, i, 0)),
        out_shape=_sds((N_DEV, D, L1_W), BF16),
        compiler_params=_cp(1),
    )(dwup_a, dwup_b, dwxkv)


def slots_l23(dwo, dwxq, dwxo, dwdn, dwpa, dwpr):
    def body(o_ref, xq_ref, xo_ref, dn_ref, pa_ref, pr_ref, l2_ref, l3_ref):
        l2_ref[L2_OUT:L2_XQ, :] = o_ref[...].astype(BF16)
        l2_ref[L2_XQ:L2_XO, :] = xq_ref[...].astype(BF16)
        l2_ref[L2_XO:L2_DN, :] = xo_ref[...].astype(BF16)
        l2_ref[L2_DN:L2_ROWS, :] = dn_ref[...].astype(BF16)
        for h in range(MLA_H):
            l3_ref[h * MLA_V:(h + 1) * MLA_V, :] = pa_ref[h * HP:h * HP + MLA_V, :].astype(BF16)
        l3_ref[L3_PRET:L3_ROWS, :] = pr_ref[...].astype(BF16)

    rows128 = pl.BlockSpec((128, D), lambda j: (j, 0))

    def cols(r):
        return pl.BlockSpec((r, 128), lambda j: (0, j))

    return pl.pallas_call(
        body,
        name="slots_l23",
        grid=(N_DEV,),
        in_specs=[rows128, rows128, rows128, pl.BlockSpec((RS_DN, D), lambda j: (j, 0)),
                  cols(MLA_H * HP), cols(RET_H * RET_D)],
        out_specs=[pl.BlockSpec((None, L2_ROWS, D), lambda j: (j, 0, 0)),
                   pl.BlockSpec((None, L3_ROWS, HP), lambda j: (j, 0, 0))],
        out_shape=[_sds((N_DEV, L2_ROWS, D), BF16), _sds((N_DEV, L3_ROWS, HP), BF16)],
        compiler_params=_cp(1),
    )(dwo, dwxq, dwxo, dwdn, dwpa, dwpr)


def slots_early(dwz, dwk, dwv, dwq, *, tile):
    def body(dz_ref, k_ref, v_ref, q_ref, s1_ref, s2_ref):
        for os_, ln_, ds_ in WZ_RUNS:
            for j, off, d, ln in _pieces(os_, ln_, ds_, NS_IN):
                s1_ref[j, :, off:off + ln] = dz_ref[:, d:d + ln].astype(BF16)

        @pl.when(pl.program_id(0) == 0)
        def _():
            for j in range(N_DEV):
                s2_ref[j, 0:E2_UQ, 0:MLA_NOPE] = k_ref[:, j * HP:j * HP + MLA_NOPE].astype(BF16)
                s2_ref[j, 0:E2_UQ, MLA_NOPE:HP] = v_ref[:, j * HP:j * HP + MLA_V].astype(BF16)
                s2_ref[j, E2_UQ:E2_ROWS, 0:NS_UQ] = q_ref[:, j * HP:j * HP + NS_UQ].astype(BF16)
                s2_ref[j, E2_UQ:E2_ROWS, NS_UQ:HP] = jnp.zeros((E2_ROWS - E2_UQ, HP - NS_UQ), BF16)

    def whole(r):
        return pl.BlockSpec((r, MLA_H * HP), lambda i: (0, 0))

    return pl.pallas_call(
        body,
        name="slots_early",
        grid=(D // tile,),
        in_specs=[pl.BlockSpec((tile, ZW), lambda i: (i, 0)), whole(MLA_KVR), whole(MLA_KVR), whole(MLA_QR)],
        out_specs=[pl.BlockSpec((N_DEV, tile, NS_IN), lambda i: (0, i, 0)),
                   pl.BlockSpec((N_DEV, E2_ROWS, HP), lambda i: (0, 0, 0))],
        out_shape=[_sds((N_DEV, D, NS_IN), BF16), _sds((N_DEV, E2_ROWS, HP), BF16)],
        compiler_params=_cp(1),
    )(dwz, dwk, dwv, dwq)


SMALL = (("g_mix", 1024, 0), ("b_gate", 2048, 1), ("g_q_lat", 256, 3), ("g_kv_lat", 128, 4), ("g_ret", 512, 5),
         ("g_cross", 1024, 6), ("g_mem", 1024, 7), ("g_ffn", 1024, 8), ("b_conv", 5632, 9), ("g_final", 1024, 15))
SMALL_DIRECT = tuple(s for s in SMALL if s[0] != "b_conv")
S_ROWS = 16


def _flat_pieces(n, row0):
    return [(row0 + k // D, k, min(D, n - k)) for k in range(0, n, D)]


def exchange_small(cs_a, cs_b, smalls):
    ns = len(smalls)

    def body(*refs):
        ca_ref, cb_ref = refs[:2]
        small_refs = refs[2:2 + ns]
        rd_ref, rs_ref, dsl_ref, own_ref = refs[2 + ns:6 + ns]
        sems = refs[6 + ns:]
        own_ref[...] = jnp.zeros(own_ref.shape, F32)
        for (name, n, row0), g_ref in zip(SMALL_DIRECT, small_refs):
            for r, c0, ln in _flat_pieces(n, row0):
                own_ref[r:r + 1, 0:ln] = g_ref[:, c0:c0 + ln]
        row0 = dict((s[0], s[2]) for s in SMALL)["b_conv"]
        for half, c_ref in enumerate((ca_ref, cb_ref)):
            k = half * D_FF
            end = k + D_FF
            while k < end:
                r, lane = row0 + k // D, k % D
                ln = min(D - lane, end - k)
                own_ref[r:r + 1, lane:lane + ln] = c_ref[3:4, k - half * D_FF:k - half * D_FF + ln]
                k += ln
        dsl_ref[...] = jnp.zeros(dsl_ref.shape, F32)
        per_half = D_FF // NS_UP
        for j in range(N_DEV):
            c_ref = ca_ref if j < per_half else cb_ref
            c0 = (j % per_half) * NS_UP
            dsl_ref[j, 0:3, 0:NS_UP] = c_ref[0:3, c0:c0 + NS_UP]
        comm = ([(dsl_ref, True), (own_ref, False)], [rd_ref, rs_ref]) + tuple(sems)
        _exchange_start(*comm)
        _exchange_wait(*comm)

    n1 = N_DEV - 1
    return pl.pallas_call(
        body,
        name="exchange_small",
        in_specs=[VM] * (2 + ns),
        out_specs=[VM, VM, VM, VM],
        out_shape=[_sds((n1,) + L4_SHAPE, F32), _sds((n1, S_ROWS, D), F32), _sds((N_DEV,) + L4_SHAPE, F32),
                   _sds((S_ROWS, D), F32)],
        scratch_shapes=_exchange_scratch(2),
        compiler_params=_cp0(),
    )(cs_a, cs_b, *smalls)


def _adamw(w, g, m, v):
    m = ADAM_B1 * m + (1.0 - ADAM_B1) * g
    v = ADAM_B2 * v + (1.0 - ADAM_B2) * (g * g)
    m_hat = m / (1.0 - ADAM_B1 ** ADAM_STEP)
    v_hat = v / (1.0 - ADAM_B2 ** ADAM_STEP)
    delta = -ADAM_LR * (m_hat / (jnp.sqrt(v_hat) + ADAM_EPS) + ADAM_WD * w)
    return delta, m, v


def _apply(g, refs, outs):
    d, mn, vn = _adamw(refs[0][...], g, refs[1][...], refs[2][...])
    outs[0][...] = g
    outs[1][...] = d
    outs[2][...] = mn
    outs[3][...] = vn


def adam_cols(own, recv, wmv, spans, *, name, tile):
    R, W = own.shape
    nw = len(spans)

    def body(*refs):
        own_ref, recv_ref = refs[:2]
        ins, outs = refs[2:2 + 3 * nw], refs[2 + 3 * nw:]
        g = own_ref[...].astype(F32)
        for k in range(N_DEV - 1):
            g = g + recv_ref[k].astype(F32)
        for t, (lo, hi) in enumerate(spans):
            _apply(g[:, lo:hi], ins[3 * t:3 * t + 3], outs[4 * t:4 * t + 4])

    def blk(w):
        return pl.BlockSpec((tile, w), lambda i: (i, 0))

    widths = [hi - lo for lo, hi in spans]
    return pl.pallas_call(
        body,
        name=name,
        grid=(R // tile,),
        in_specs=[blk(W), pl.BlockSpec((N_DEV - 1, tile, W), lambda i: (0, i, 0))] + [blk(w) for w in widths for _ in range(3)],
        out_specs=[blk(w) for w in widths for _ in range(4)],
        out_shape=[_sds((R, w), F32) for w in widths for _ in range(4)],
        compiler_params=_cp(1),
    )(own, recv, *wmv)


def adam_rows(own, recv, wmv, spans, *, name):
    nw = len(spans)

    def body(*refs):
        own_ref, recv_ref = refs[:2]
        ins, outs = refs[2:2 + 3 * nw], refs[2 + 3 * nw:]
        for t, (lo, hi, w) in enumerate(spans):
            g = own_ref[lo:hi, :].astype(F32)
            for k in range(N_DEV - 1):
                g = g + recv_ref[k, lo:hi, :].astype(F32)
            _apply(g[:, 0:w], ins[3 * t:3 * t + 3], outs[4 * t:4 * t + 4])

    return pl.pallas_call(
        body,
        name=name,
        in_specs=[VM] * (2 + 3 * nw),
        out_specs=[VM] * (4 * nw),
        out_shape=[_sds((hi - lo, w), F32) for lo, hi, w in spans for _ in range(4)],
        compiler_params=_cp0(),
    )(own, recv, *wmv)


def adam_small(own_s, recv_s, dslots, recv_d, wmv_small, wmv_conv):
    ns = len(SMALL)

    def body(*refs):
        own_ref, rs_ref, dsl_ref, rd_ref = refs[:4]
        ins = refs[4:4 + 3 * ns + 3]
        outs = refs[4 + 3 * ns + 3:4 + 3 * ns + 3 + 4 * ns + 4]
        all_sc = refs[-1]
        me = _my_id()
        all_sc[0] = own_ref[...]
        for k in range(N_DEV - 1):
            all_sc[k + 1] = rs_ref[k]
        g = all_sc[jnp.bitwise_xor(me, 0)]
        for s in range(1, N_DEV):
            g = g + all_sc[jnp.bitwise_xor(me, s)]
        all_sc[0] = g
        for t, (name, n, row0) in enumerate(SMALL):
            pieces = [all_sc[0, r:r + 1, 0:ln] for r, _, ln in _flat_pieces(n, row0)]
            gt = pieces[0] if len(pieces) == 1 else jnp.concatenate(pieces, axis=1)
            _apply(gt, ins[3 * t:3 * t + 3], outs[4 * t:4 * t + 4])
        gc = dsl_ref[me]
        for k in range(N_DEV - 1):
            gc = gc + rd_ref[k]
        _apply(gc[0:3, 0:NS_UP], ins[3 * ns:3 * ns + 3], outs[4 * ns:4 * ns + 4])

    out_shape = [_sds((1, n), F32) for _, n, _ in SMALL for _ in range(4)] + [_sds((3, NS_UP), F32)] * 4
    return pl.pallas_call(
        body,
        name="adam_small",
        in_specs=[VM] * (4 + 3 * ns + 3),
        out_specs=[VM] * len(out_shape),
        out_shape=out_shape,
        scratch_shapes=[pltpu.VMEM((N_DEV, S_ROWS, D), F32)],
    )(own_s, recv_s, dslots, recv_d, *wmv_small, *wmv_conv)


def kernel(x, mem, positions, g_mix, w_in, b_gate, g_q_lat, w_uq, g_kv_lat, w_ukv, w_proj_mla, g_ret, w_proj_ret, w_out, g_cross, g_mem, w_xq, w_xkv, w_xo, g_ffn, w_up, w_conv, b_conv, w_down, g_final, loss_target, m_g_mix, m_w_in, m_b_gate, m_g_q_lat, m_w_uq, m_g_kv_lat, m_w_ukv, m_w_proj_mla, m_g_ret, m_w_proj_ret, m_w_out, m_g_cross, m_g_mem, m_w_xq, m_w_xkv, m_w_xo, m_g_ffn, m_w_up, m_w_conv, m_b_conv, m_w_down, m_g_final, v_g_mix, v_w_in, v_b_gate, v_g_q_lat, v_w_uq, v_g_kv_lat, v_w_ukv, v_w_proj_mla, v_g_ret, v_w_proj_ret, v_w_out, v_g_cross, v_g_mem, v_w_xq, v_w_xkv, v_w_xo, v_g_ffn, v_w_up, v_w_conv, v_b_conv, v_w_down, v_g_final):
    args = dict(locals())
    T = x.shape[1]
    M = mem.shape[1]
    tile = min(256, T)
    tile2 = min(512, T)
    tile4 = min(1024, T)
    tq = min(512, T)
    rb = min(1024, T)

    xs = x[0]
    tgt = loss_target[0]
    mems = mem[0]

    def shard(name, prefix=""):
        a = args[prefix + name]
        return a.reshape(a.shape[-2:]) if a.ndim >= 2 else a.reshape(1, -1)

    e1, e2 = pack_early(shard("w_in"), shard("w_ukv"), shard("w_uq"))
    late_parts = pack_late(shard("w_up"), shard("w_xkv"), shard("w_out"), shard("w_xq"), shard("w_xo"), shard("w_down"),
                           shard("w_proj_mla"), shard("w_proj_ret"), shard("w_conv"))

    pos_f = jnp.broadcast_to(positions[0].astype(F32)[:, None], (T, 128))
    inv_m, inv_r = _rot_inv()
    u, cm, s1, s2, cr, sr, ge1, ge2 = rowwise(
        lambda xv, p, im, ir, g: (_rms(xv, g),) + _rot_tables(p, im, ir), [(xs, None), (pos_f, None)],
        [inv_m, inv_r, g_mix], [(D, BF16)] + [(128, F32)] * 5, [], tile=tile2, name="norm_mix_tables", gather=[e1, e2])
    wz, wk, wv, wq = assemble_early(ge1, ge2, tile=tile)
    rconsts = _ret_consts()

    z = matmul(u, wz, name="mm_z", tn=1536)

    def mixers_in(zl, zr, cmv, s1v, s2v, crv, srv, gq, gkv, wqv, wkv_, wvv):
        cq = _rms(zl[:, 0:256], gq).astype(BF16)
        ckv = _rms(zl[:, 256:384], gkv).astype(BF16)
        qv = _rot_mla(_dot(cq, wqv), cmv, s1v, s2v)
        kr = _rot_mla(zl[:, 384:512], cmv, s1v, s2v)
        kn = _dot(ckv, wkv_)
        kv_ = jnp.concatenate([kn[:, h * HP:(h + 1) * HP] + kr for h in range(MLA_H)], axis=1)
        vv = _dot(ckv, wvv)
        lane = lax.broadcasted_iota(jnp.int32, vv.shape, 1)
        vv = jnp.where((lane & (HP - 1)) == ONE_LANE, 1.0, vv)
        rqv = _rot_ret(zr[:, 0:512], crv, srv)
        rkv = _rot_ret(zr[:, 512:1024], crv, srv) * (RET_D ** -0.5)
        return qv, kv_, vv, rqv, rkv, zr[:, 1024:1536]

    q_a, k_a, v_a, rq, rk, rv = rowwise(
        mixers_in, [(z, (512, 8)), (z, (2048, 0)), (cm, None), (s1, None), (s2, None), (cr, None), (sr, None)],
        [g_q_lat, g_kv_lat, wq, wk, wv], [(MLA_H * HP, BF16)] * 3 + [(512, BF16)] * 3, [], tile=tile2, name="mixers_in")

    o_a, lse, gl1, gl2, gl3, gl4 = mla_fwd(q_a, k_a, v_a, list(late_parts), tq=tq)
    wup, wxkv = assemble_l1(gl1, tile=tile)
    wo, wxq, wxo, wdn, wpa, wpr, wcv = assemble_l234(gl2, gl3, gl4)
    ret, rstate = ret_fwd(rq, rk, rv, rconsts, rb=rb)

    def gn_parts(r):
        outs = []
        for h in range(RET_H):
            rh = r[:, h * RET_D:(h + 1) * RET_D]
            mu = jnp.mean(rh, axis=-1, keepdims=True)
            dlt = rh - mu
            rstd = lax.rsqrt(jnp.mean(dlt * dlt, axis=-1, keepdims=True) + EPS)
            outs.append((dlt * rstd, rstd))
        return outs

    def mix_fwd(ov, rv_, rg, gt, wpav, wprv, gr, bg):
        ya = _dot(ov, wpav)
        xh = jnp.concatenate([p[0] for p in gn_parts(rv_)], axis=1)
        t = rg * _sigmoid(rg) * (xh * gr)
        yr = _dot(t.astype(BF16), wprv)
        ga_ = _sigmoid(gt[:, :D] + bg[:, :D])
        gr_ = _sigmoid(gt[:, D:] + bg[:, D:])
        return ga_ * ya + gr_ * yr

    mix = rowwise(mix_fwd, [(o_a, None), (ret, None), (z, (512, 3)), (z, (2048, 1))], [wpa, wpr, g_ret, b_gate],
                  [(D, BF16)], [], tile=tile, name="mix_fwd")[0]
    def proj_norm(av, rv_, wv_, g):
        hv = rv_ + _dot(av, wv_)
        return hv, _rms(hv, g)

    h1, n2 = rowwise(proj_norm, [(mix, None), (xs, None)], [wo, g_cross], [(D, F32), (D, BF16)], [], tile=tile4,
                     name="mm_out_norm")
    xq = matmul(n2, wxq, out_dtype=BF16, name="mm_xq")
    mn = rowwise(lambda mv_, g: _rms(mv_, g), [(mems, None)], [g_mem], [(D, BF16)], [], tile=min(tile, M), name="norm_mem")[0]
    mkv = matmul(mn, wxkv, out_dtype=BF16, name="mm_mkv")

    x_scale = X_HD ** -0.5

    def xattn_fwd(xqv, mkvv):
        outs = []
        for h in range(X_H):
            sl = slice(h * X_HD, (h + 1) * X_HD)
            s = _dot_nt(xqv[:, sl], mkvv[:, sl]) * x_scale
            s = s - jnp.max(s, axis=-1, keepdims=True)
            e = jnp.exp(s)
            p = e / jnp.sum(e, axis=-1, keepdims=True)
            outs.append(_dot(p.astype(BF16), mkvv[:, D + h * X_HD:D + (h + 1) * X_HD]))
        return jnp.concatenate(outs, axis=1)

    xo = rowwise(xattn_fwd, [(xq, None)], [mkv], [(D, BF16)], [], tile=tile, name="xattn_fwd")[0]
    h2, n3 = rowwise(proj_norm, [(xo, None), (h1, None)], [wxo, g_ffn], [(D, F32), (D, BF16)], [], tile=tile4,
                     name="mm_xo_norm")
    up_pre = matmul(n3, wup, out_dtype=BF16, name="mm_up", tn=1408)
    cw = D_FF // 2
    act = conv_act_fwd(up_pre, wcv, b_conv, tile=tile, cw=cw)

    def down_loss(av, hv2, tv, wv_, g):
        hv = hv2 + _dot(av, wv_)
        y = _rms(hv, g)
        err = y - tv
        part = 0.5 * jnp.sum(jnp.sum(err * err, axis=-1, keepdims=True) / D, axis=0, keepdims=True)
        dx, dg = _rms_bwd(hv, g, err / D)
        return dx, dg, jnp.broadcast_to(part, (8, 128))

    g_fin2 = g_final.reshape(1, D)
    dh3, dg_final, loss_acc = rowwise(down_loss, [(act, None), (h2, None), (tgt, None)], [wdn, g_fin2], [(D, F32)],
                                      [((1, D), F32), ((8, 128), F32)], tile=tile2, name="mm_down_loss")
    loss = lax.psum(loss_acc[0, 0], ("x", "y", "c"))

    dact = matmul(dh3, wdn, tb=True, out_dtype=BF16, name="mm_dact", tn=1408)
    dw_down = matmul_tn(act, dh3, name="mm_dw_down", tm=1408)
    dup_a, dup_b, cs_a, cs_b = conv_act_bwd(up_pre, dact, wcv, b_conv, tile=tile, cw=cw)
    dn3 = matmul2_tb(dup_a, dup_b, wup, name="mm_dn3", tm=1024, tn=512)
    dw_up_a = matmul_tn(n3, dup_a, name="mm_dw_up_a", tn=1408)
    dw_up_b = matmul_tn(n3, dup_b, name="mm_dw_up_b", tn=1408)

    def norm_bwd(hv, dyv, drv, g):
        dx, dg = _rms_bwd(hv, g, dyv)
        return dx + drv, dg

    dh2, dg_ffn = rowwise(norm_bwd, [(h2, None), (dn3, None), (dh3, None)], [g_ffn], [(D, F32)], [((1, D), F32)],
                          tile=tile2, name="norm_ffn_bwd")
    dxo = matmul(dh2, wxo, tb=True, out_dtype=BF16, name="mm_dxo")
    dw_xo = matmul_tn(xo, dh2, name="mm_dw_xo")

    def xattn_bwd(xqv, dxov, mkvv):
        dxq, dmk, dmv = [], [], []
        for h in range(X_H):
            sl = slice(h * X_HD, (h + 1) * X_HD)
            slv = slice(D + h * X_HD, D + (h + 1) * X_HD)
            s = _dot_nt(xqv[:, sl], mkvv[:, sl]) * x_scale
            s = s - jnp.max(s, axis=-1, keepdims=True)
            e = jnp.exp(s)
            p = e / jnp.sum(e, axis=-1, keepdims=True)
            dp = _dot_nt(dxov[:, sl], mkvv[:, slv])
            ds = (p * (dp - jnp.sum(dp * p, axis=-1, keepdims=True)) * x_scale).astype(BF16)
            dxq.append(_dot(ds, mkvv[:, sl]))
            dmk.append(_dot_tn(ds, xqv[:, sl]))
            dmv.append(_dot_tn(p.astype(BF16), dxov[:, sl]))
        return jnp.concatenate(dxq, axis=1), jnp.concatenate(dmk + dmv, axis=1)

    dxq, dmkv = rowwise(xattn_bwd, [(xq, None), (dxo, None)], [mkv], [(D, BF16)], [((M, 2 * D), F32)],
                        tile=tile, name="xattn_bwd")
    dw_xq = matmul_tn(n2, dxq, name="mm_dw_xq")

    def proj_norm_bwd(dyv, hv, drv, wv_, g):
        dx, dg = _rms_bwd(hv, g, _dot_nt(dyv, wv_))
        return dx + drv, dg

    dh1, dg_cross = rowwise(proj_norm_bwd, [(dxq, None), (h1, None), (dh2, None)], [wxq, g_cross], [(D, F32)],
                            [((1, D), F32)], tile=tile4, name="mm_dn2_norm_bwd")
    dw_xkv = matmul_tn(mn, dmkv, name="mm_dw_xkv", tk=M)
    dmn = matmul(dmkv, wxkv, tb=True, name="mm_dmn", tm=M)
    dg_mem = rowwise(lambda mv_, dyv, g: _rms_bwd(mv_, g, dyv)[1], [(mems, None), (dmn, None)], [g_mem], [],
                     [((1, D), F32)], tile=min(tile, M), name="norm_mem_bwd")[0]

    dmix = matmul(dh1, wo, tb=True, out_dtype=BF16, name="mm_dmix")
    dw_out = matmul_tn(mix, dh1, name="mm_dw_out")

    def mix_bwd(ov, rv_, rg, gt, dmv_, wpav, wprv, gr, bg):
        dm_ = dmv_.astype(F32)
        ya = _dot(ov, wpav)
        parts = gn_parts(rv_)
        xh = jnp.concatenate([p[0] for p in parts], axis=1)
        yn = xh * gr
        sg = _sigmoid(rg)
        sl_ = rg * sg
        t = (sl_ * yn).astype(BF16)
        yr = _dot(t, wprv)
        ga_ = _sigmoid(gt[:, :D] + bg[:, :D])
        gr_ = _sigmoid(gt[:, D:] + bg[:, D:])
        dgates = jnp.concatenate([dm_ * ya * ga_ * (1.0 - ga_), dm_ * yr * gr_ * (1.0 - gr_)], axis=1)
        dya = (dm_ * ga_).astype(BF16)
        dyr = (dm_ * gr_).astype(BF16)
        do_ = _dot_nt(dya, wpav)
        dwpa_ = _dot_tn(ov, dya)
        dt = _dot_nt(dyr, wprv)
        dwpr_ = _dot_tn(t, dyr)
        drg_ = dt * yn * (sg * (1.0 + rg * (1.0 - sg)))
        dyn = dt * sl_
        dgr = jnp.sum(dyn * xh, axis=0, keepdims=True)
        dxh = dyn * gr
        drets = []
        for h in range(RET_H):
            sl = slice(h * RET_D, (h + 1) * RET_D)
            xhh, rstd = parts[h]
            dxhh = dxh[:, sl]
            drets.append(rstd * (dxhh - jnp.mean(dxhh, axis=-1, keepdims=True)
                                 - xhh * jnp.mean(dxhh * xhh, axis=-1, keepdims=True)))
        dret_ = jnp.concatenate(drets, axis=1)
        dbg = jnp.sum(dgates, axis=0, keepdims=True)
        return do_, dret_, drg_, dgates, dwpa_, dwpr_, dgr, dbg

    do_a, dret, drg, dgates, dwpa, dw_proj_ret, dg_ret, db_gate = rowwise(
        mix_bwd, [(o_a, None), (ret, None), (z, (512, 3)), (z, (2048, 1)), (dmix, None)], [wpa, wpr, g_ret, b_gate],
        [(MLA_H * HP, BF16), (512, F32), (512, BF16), (2 * D, BF16)],
        [((MLA_H * HP, D), F32), ((512, D), F32), ((1, 512), F32), ((1, 2 * D), F32)], tile=tile, name="mix_bwd")

    sl1 = slots_l1(dw_up_a, dw_up_b, dw_xkv, tile=tile)
    sl2, sl3 = slots_l23(dw_out, dw_xq, dw_xo, dw_down, dwpa, dw_proj_ret)
    lse_row, delta_row = mla_prep(o_a, do_a, lse, tq=tq)
    dq_a, dk_a, dv_a, rl1, rl2, rl3 = mla_bwd(q_a, k_a, v_a, do_a, lse_row, delta_row, [sl1, sl2, sl3], tq=tq)
    drq_r, drk_r, drv = ret_bwd(rq, rk, rv, rstate, dret, rconsts, rb=rb)

    def mixers_in_bwd(zl, cmv, s1v, s2v, dqv, dkv_, dvv, drq_, drk_, drv_, drg_, dgt, crv, srv, gq, gkv, wqv, wkv_, wvv):
        cqf, ckvf = zl[:, 0:256], zl[:, 256:384]
        cq = _rms(cqf, gq).astype(BF16)
        ckv = _rms(ckvf, gkv).astype(BF16)
        dq_pre = _rot_mla(dqv.astype(F32), cmv, -s1v, -s2v).astype(BF16)
        dkf = dkv_.astype(F32)
        dkr = dkf[:, 0:HP]
        for h in range(1, MLA_H):
            dkr = dkr + dkf[:, h * HP:(h + 1) * HP]
        lane = lax.broadcasted_iota(jnp.int32, dkr.shape, 1)
        dzk = _rot_mla(jnp.where((lane >= 64) & (lane < 96), dkr, 0.0), cmv, -s1v, -s2v)
        dkb = dkv_.astype(BF16)
        dvb = dvv.astype(BF16)
        dcq_n = _dot_nt(dq_pre, wqv)
        dckv_n = _dot_nt(dkb, wkv_) + _dot_nt(dvb, wvv)
        dwq_ = _dot_tn(cq, dq_pre)
        dwk_ = _dot_tn(ckv, dkb)
        dwv_ = _dot_tn(ckv, dvb)
        dcq, dgq = _rms_bwd(cqf, gq, dcq_n)
        dckv, dgkv = _rms_bwd(ckvf, gkv, dckv_n)
        a = _rot_ret(drq_, crv, -srv)
        b = _rot_ret(drk_, crv, -srv) * (RET_D ** -0.5)
        dz_ = jnp.concatenate([a.astype(BF16), b.astype(BF16), drv_.astype(BF16), drg_, dgt,
                               dcq.astype(BF16), dckv.astype(BF16), dzk.astype(BF16)], axis=1)
        return dz_, dwq_, dwk_, dwv_, dgq, dgkv

    dz, dwq, dwk, dwv, dg_q_lat, dg_kv_lat = rowwise(
        mixers_in_bwd, [(z, (512, 8)), (cm, None), (s1, None), (s2, None), (dq_a, None), (dk_a, None), (dv_a, None),
                        (drq_r, None), (drk_r, None), (drv, None), (drg, None), (dgates, None), (cr, None), (sr, None)],
        [g_q_lat, g_kv_lat, wq, wk, wv], [(ZW, BF16)],
        [((MLA_QR, MLA_H * HP), F32), ((MLA_KVR, MLA_H * HP), F32), ((MLA_KVR, MLA_H * HP), F32),
         ((1, MLA_QR), F32), ((1, MLA_KVR), F32)], tile=tile, name="mixers_in_bwd")
    dwz = matmul_tn(u, dz, name="mm_dw_z", tn=1536)
    se1, se2 = slots_early(dwz, dwk, dwv, dwq, tile=tile)
    du, re1, re2 = matmul(dz, wz, tb=True, name="mm_du", tm=1024, tn=512, exchange=[se1, se2])

    grad_x, dg_mix = rowwise(norm_bwd, [(xs, None), (du, None), (dh1, None)], [g_mix], [(D, F32)], [((1, D), F32)],
                             tile=tile2, name="norm_mix_bwd")

    small_grads = {"g_mix": dg_mix, "b_gate": db_gate, "g_q_lat": dg_q_lat, "g_kv_lat": dg_kv_lat, "g_ret": dg_ret,
                   "g_cross": dg_cross, "g_mem": dg_mem, "g_ffn": dg_ffn, "g_final": dg_final}
    rd, rs, dslots, own_s = exchange_small(cs_a, cs_b, [small_grads[n] for n, _, _ in SMALL_DIRECT])

    me = _my_id()

    def own(slots):
        return lax.dynamic_index_in_dim(slots, me, axis=0, keepdims=False)

    def wmv(names):
        return [shard(n, p) for n in names for p in ("", "m_", "v_")]

    names_s = tuple(n for n, _, _ in SMALL)
    groups = (
        (("w_in",), adam_cols(own(se1), re1, wmv(("w_in",)), ((0, NS_IN),), name="adam_e1", tile=128)),
        (("w_ukv", "w_uq"), adam_rows(own(se2), re2, wmv(("w_ukv", "w_uq")),
                                      ((0, E2_UQ, HP), (E2_UQ, E2_ROWS, NS_UQ)), name="adam_e2")),
        (("w_up", "w_xkv"), adam_cols(own(sl1), rl1, wmv(("w_up", "w_xkv")), ((0, NS_UP), (NS_UP, L1_W)),
                                      name="adam_l1", tile=128)),
        (("w_out", "w_xq", "w_xo", "w_down"),
         adam_rows(own(sl2), rl2, wmv(("w_out", "w_xq", "w_xo", "w_down")),
                   ((L2_OUT, L2_XQ, D), (L2_XQ, L2_XO, D), (L2_XO, L2_DN, D), (L2_DN, L2_ROWS, D)), name="adam_l2")),
        (("w_proj_mla", "w_proj_ret"), adam_rows(own(sl3), rl3, wmv(("w_proj_mla", "w_proj_ret")),
                                                 ((0, L3_PRET, HP), (L3_PRET, L3_ROWS, HP)), name="adam_l3")),
        (names_s + ("w_conv",), adam_small(own_s, rs, dslots, rd, wmv(names_s), wmv(("w_conv",)))),
    )
    res = {}
    for names, outs_ in groups:
        for t, n in enumerate(names):
            res[n] = outs_[4 * t:4 * t + 4]

    order = ["g_mix", "w_in", "b_gate", "g_q_lat", "w_uq", "g_kv_lat", "w_ukv", "w_proj_mla", "g_ret", "w_proj_ret",
             "w_out", "g_cross", "g_mem", "w_xq", "w_xkv", "w_xo", "g_ffn", "w_up", "w_conv", "b_conv", "w_down",
             "g_final"]
    outs = [loss, grad_x[None]]
    for kind in range(4):
        outs += [res[n][kind].reshape(args[n].shape) for n in order]
    return tuple(outs)
```

```python
import functools
import math

import jax
import jax.numpy as jnp
import numpy as np
from jax import lax
from jax.experimental import pallas as pl
from jax.experimental.pallas import tpu as pltpu

F32 = jnp.float32
BF16 = jnp.bfloat16

D = 1024
MLA_H, MLA_NOPE, MLA_ROPE, MLA_V = 8, 64, 32, 64
MLA_QR, MLA_KVR = 256, 128
RET_H, RET_D, RET_C = 4, 128, 128
X_H, X_HD = 4, 256
D_FF = 2816
THETA = 10000.0
EPS = 1e-6
HP = 128
ZW = 4608
N_DEV = 8

ADAM_LR, ADAM_B1, ADAM_B2, ADAM_EPS, ADAM_WD, ADAM_STEP = 0.001, 0.9, 0.999, 1e-08, 0.01, 10

VMEM_LIMIT = 56 * 1024 * 1024
MESH = pl.DeviceIdType.MESH
VM = pl.BlockSpec(memory_space=pltpu.VMEM)
ANY = pl.BlockSpec(memory_space=pl.ANY)


def _cp(n_axes):
    return pltpu.CompilerParams(dimension_semantics=("arbitrary",) * n_axes, vmem_limit_bytes=VMEM_LIMIT)


def _cp0():
    return pltpu.CompilerParams(vmem_limit_bytes=VMEM_LIMIT)


def _pick(n, cap, mult=128):
    best = None
    for t in range(mult, min(n, cap) + 1, mult):
        if n % t == 0:
            best = t
    return best if best is not None else n


def _dot(a, b):
    return jnp.dot(a, b, preferred_element_type=F32)


def _dot_nt(a, b):
    return lax.dot_general(a, b, (((1,), (1,)), ((), ())), preferred_element_type=F32)


def _dot_tn(a, b):
    return lax.dot_general(a, b, (((0,), (0,)), ((), ())), preferred_element_type=F32)


def _sds(shape, dtype):
    return jax.ShapeDtypeStruct(shape, dtype)


def matmul(a, b, *, name, tb=False, res=None, out_dtype=F32, tm=1024, tn=1024, exchange=()):
    M, K = a.shape
    N = b.shape[0] if tb else b.shape[1]
    tm = _pick(M, tm, 8)
    tn = _pick(N, tn)
    gi, gj = M // tm, N // tn
    ns = len(exchange)
    n_in = 2 if res is None else 3

    def body(*refs):
        a_ref, b_ref = refs[:2]
        o_ref = refs[n_in + ns]
        if ns:
            comm = ([(r, True) for r in refs[n_in:n_in + ns]], refs[n_in + ns + 1:n_in + 2 * ns + 1]) + tuple(
                refs[n_in + 2 * ns + 1:])
            i, j = pl.program_id(0), pl.program_id(1)

            @pl.when((i == 0) & (j == 0))
            def _():
                _exchange_start(*comm)

        av = a_ref[...].astype(BF16)
        bv = b_ref[...].astype(BF16)
        acc = _dot_nt(av, bv) if tb else _dot(av, bv)
        if res is not None:
            acc = acc + refs[2][...].astype(F32)
        o_ref[...] = acc.astype(o_ref.dtype)
        if ns:

            @pl.when((i == gi - 1) & (j == gj - 1))
            def _():
                _exchange_wait(*comm)

    in_specs = [
        pl.BlockSpec((tm, K), lambda i, j: (i, 0)),
        pl.BlockSpec((tn, K), lambda i, j: (j, 0)) if tb else pl.BlockSpec((K, tn), lambda i, j: (0, j)),
    ]
    args = [a, b]
    if res is not None:
        in_specs.append(pl.BlockSpec((tm, tn), lambda i, j: (i, j)))
        args.append(res)
    out_spec = pl.BlockSpec((tm, tn), lambda i, j: (i, j))
    out_shape = _sds((M, N), out_dtype)
    return pl.pallas_call(
        body,
        name=name,
        grid=(gi, gj),
        in_specs=in_specs + [ANY] * ns,
        out_specs=[out_spec] + [ANY] * ns if ns else out_spec,
        out_shape=[out_shape] + [_sds((N_DEV - 1,) + s.shape[1:], s.dtype) for s in exchange] if ns else out_shape,
        scratch_shapes=_exchange_scratch(ns) if ns else [],
        compiler_params=_cp(2),
    )(*args, *exchange)


def matmul2_tb(a1, a2, b, *, name, tm=512, tn=512):
    M, K1 = a1.shape
    N = b.shape[0]
    tm = _pick(M, tm, 8)
    tn = _pick(N, tn)

    def body(a1_ref, a2_ref, b1_ref, b2_ref, o_ref):
        o_ref[...] = (_dot_nt(a1_ref[...].astype(BF16), b1_ref[...].astype(BF16))
                      + _dot_nt(a2_ref[...].astype(BF16), b2_ref[...].astype(BF16)))

    return pl.pallas_call(
        body,
        name=name,
        grid=(M // tm, N // tn),
        in_specs=[pl.BlockSpec((tm, K1), lambda i, j: (i, 0)), pl.BlockSpec((tm, K1), lambda i, j: (i, 0)),
                  pl.BlockSpec((tn, K1), lambda i, j: (j, 0)), pl.BlockSpec((tn, K1), lambda i, j: (j, 1))],
        out_specs=pl.BlockSpec((tm, tn), lambda i, j: (i, j)),
        out_shape=_sds((M, N), F32),
        compiler_params=_cp(2),
    )(a1, a2, b, b)


def matmul_tn(a, b, *, name, tm=1024, tn=1024, tk=1024):
    R, M = a.shape
    N = b.shape[1]
    tm = _pick(M, tm)
    tn = _pick(N, tn)
    tk = _pick(R, tk, 16)
    nk = R // tk

    def body(a_ref, b_ref, o_ref, acc_ref):
        k = pl.program_id(2)

        @pl.when(k == 0)
        def _():
            acc_ref[...] = jnp.zeros_like(acc_ref)

        acc_ref[...] += _dot_tn(a_ref[...].astype(BF16), b_ref[...].astype(BF16))

        @pl.when(k == nk - 1)
        def _():
            o_ref[...] = acc_ref[...]

    return pl.pallas_call(
        body,
        name=name,
        grid=(M // tm, N // tn, nk),
        in_specs=[pl.BlockSpec((tk, tm), lambda i, j, k: (k, i)), pl.BlockSpec((tk, tn), lambda i, j, k: (k, j))],
        out_specs=pl.BlockSpec((tm, tn), lambda i, j, k: (i, j)),
        out_shape=_sds((M, N), F32),
        scratch_shapes=[pltpu.VMEM((tm, tn), F32)],
        compiler_params=_cp(3),
    )(a, b)


def rowwise(fn, rows, consts, out_rows, out_accs, *, tile, name, gather=()):
    T = rows[0][0].shape[0]
    nt = T // tile
    n_r, n_c, n_o, n_a, n_g = len(rows), len(consts), len(out_rows), len(out_accs), len(gather)
    n_in = n_r + n_c + n_g

    def body(*refs):
        if n_g:
            comm = (refs[n_r + n_c:n_in], refs[n_in + n_o + n_a:n_in + n_o + n_a + n_g]) + tuple(
                refs[n_in + n_o + n_a + n_g:])

            @pl.when(pl.program_id(0) == 0)
            def _():
                _gather2_start(*comm)

        ins = [r[...] for r in refs[: n_r + n_c]]
        outs = fn(*ins)
        if not isinstance(outs, (tuple, list)):
            outs = (outs,)
        o_refs = refs[n_in:n_in + n_o]
        a_refs = refs[n_in + n_o:n_in + n_o + n_a]
        for o_ref, o in zip(o_refs, outs[:n_o]):
            o_ref[...] = o.astype(o_ref.dtype)
        if n_a:
            first = pl.program_id(0) == 0

            @pl.when(first)
            def _():
                for a_ref, o in zip(a_refs, outs[n_o:]):
                    a_ref[...] = o.astype(a_ref.dtype)

            @pl.when(jnp.logical_not(first))
            def _():
                for a_ref, o in zip(a_refs, outs[n_o:]):
                    a_ref[...] += o.astype(a_ref.dtype)
        if n_g:

            @pl.when(pl.program_id(0) == nt - 1)
            def _():
                _gather2_finish(*comm)

    in_specs = []
    args = []
    for arr, win in rows:
        if win is None:
            in_specs.append(pl.BlockSpec((tile, arr.shape[1]), lambda i: (i, 0)))
        else:
            w, cb = win
            in_specs.append(pl.BlockSpec((tile, w), functools.partial(lambda i, cb: (i, cb), cb=cb)))
        args.append(arr)
    for c in consts:
        in_specs.append(pl.BlockSpec(c.shape, functools.partial(lambda i, nd: (0,) * nd, nd=c.ndim)))
        args.append(c)
    out_specs = [pl.BlockSpec((tile, w), lambda i: (i, 0)) for w, _ in out_rows]
    out_shape = [_sds((T, w), dt) for w, dt in out_rows]
    for shp, dt in out_accs:
        out_specs.append(pl.BlockSpec(shp, functools.partial(lambda i, nd: (0,) * nd, nd=len(shp))))
        out_shape.append(_sds(shp, dt))
    return pl.pallas_call(
        body,
        name=name,
        grid=(nt,),
        in_specs=in_specs + [ANY] * n_g,
        out_specs=out_specs + [ANY] * n_g,
        out_shape=out_shape + [_sds((N_DEV,) + p.shape, p.dtype) for p in gather],
        scratch_shapes=_gather_scratch(n_g) if n_g else [],
        compiler_params=_cp(1),
    )(*args, *gather)


def _rms(x, g):
    r = lax.rsqrt(jnp.mean(x * x, axis=-1, keepdims=True) + EPS)
    return x * r * g


def _rms_bwd(x, g, dy):
    r = lax.rsqrt(jnp.mean(x * x, axis=-1, keepdims=True) + EPS)
    xh = x * r
    dg = jnp.sum(dy * xh, axis=0, keepdims=True)
    dxh = dy * g
    dx = r * (dxh - xh * jnp.mean(dxh * xh, axis=-1, keepdims=True))
    return dx, dg


def _sigmoid(x):
    return 0.5 * jnp.tanh(0.5 * x) + 0.5


def _rot_mla(x, c, s1, s2):
    n = x.shape[1] // HP
    outs = []
    for h in range(n):
        xh = x[:, h * HP : (h + 1) * HP]
        outs.append(xh * c + pltpu.roll(xh, HP - 16, 1) * s1 + pltpu.roll(xh, 16, 1) * s2)
    return outs[0] if n == 1 else jnp.concatenate(outs, axis=1)


def _rot_ret(x, c, s):
    n = x.shape[1] // RET_D
    outs = []
    for h in range(n):
        xh = x[:, h * RET_D : (h + 1) * RET_D]
        outs.append(xh * c + pltpu.roll(xh, RET_D // 2, 1) * s)
    return outs[0] if n == 1 else jnp.concatenate(outs, axis=1)


def _rot_inv():
    lane_np = np.arange(128)
    inv_m = (jnp.asarray(THETA, F32) ** (-jnp.asarray(lane_np & 15, F32) / 16.0)).reshape(1, 128)
    inv_r = (jnp.asarray(THETA, F32) ** (-jnp.asarray(lane_np & 63, F32) / 64.0)).reshape(1, 128)
    return inv_m, inv_r


def _rot_tables(p, im, ir):
    lane = lax.broadcasted_iota(jnp.int32, p.shape, 1)
    ang = p * im
    cm = jnp.where((lane >= 64) & (lane < 96), jnp.cos(ang), 1.0)
    sn = jnp.sin(ang)
    s1 = jnp.where((lane >= 64) & (lane < 80), -sn, 0.0)
    s2 = jnp.where((lane >= 80) & (lane < 96), sn, 0.0)
    angr = p * ir
    snr = jnp.sin(angr)
    return cm, s1, s2, jnp.cos(angr), jnp.where(lane < 64, -snr, snr)


def _peer(m):
    x, y, c = lax.axis_index("x"), lax.axis_index("y"), lax.axis_index("c")
    mx, my, mc = (m >> 2) & 1, (m >> 1) & 1, m & 1
    px = 1 - x if mx else x
    py = 1 - y if my else y
    pc = 1 - c if mc else c
    return (px, py, pc), 4 * px + 2 * py + pc


def _my_id():
    return 4 * lax.axis_index("x") + 2 * lax.axis_index("y") + lax.axis_index("c")


def _gather_copies(srcs, outs, send_sems, recv_sems, local_sems, arriving=False):
    me = _my_id()
    copies = []
    if not arriving:
        for g, (src, out) in enumerate(zip(srcs, outs)):
            copies.append((pltpu.make_async_copy(src, out.at[me], local_sems.at[g]), False))
    for m in range(1, N_DEV):
        peer, plin = _peer(m)
        for g, (src, out) in enumerate(zip(srcs, outs)):
            copies.append((pltpu.make_async_remote_copy(
                src_ref=src, dst_ref=out.at[plin if arriving else me], send_sem=send_sems.at[g, m - 1],
                recv_sem=recv_sems.at[g, m - 1], device_id=peer, device_id_type=MESH), True))
    return copies


def _gather_start(*a):
    for cp, _ in _gather_copies(*a):
        cp.start()


def _gather_wait(*a):
    for cp, _ in _gather_copies(*a, arriving=True):
        cp.wait_recv()
    for cp, remote in _gather_copies(*a):
        if remote:
            cp.wait_send()
        else:
            cp.wait()


CHIP_RELATIONS = (4, 2, 6)


def _gather2_copy(src, out, block, to, send_sems, recv_sems, g, k):
    return pltpu.make_async_remote_copy(src_ref=src, dst_ref=out.at[block], send_sem=send_sems.at[g, k],
                                        recv_sem=recv_sems.at[g, k], device_id=to, device_id_type=MESH)


def _gather2_start(srcs, outs, send_sems, recv_sems, local_sems):
    me = _my_id()
    sib, _ = _peer(1)
    for g, (src, out) in enumerate(zip(srcs, outs)):
        pltpu.make_async_copy(src, out.at[me], local_sems.at[g]).start()
        _gather2_copy(src, out, me, sib, send_sems, recv_sems, g, 0).start()
        for t, m in enumerate(CHIP_RELATIONS):
            _gather2_copy(src, out, me, _peer(m)[0], send_sems, recv_sems, g, 1 + t).start()


def _gather2_finish(srcs, outs, send_sems, recv_sems, local_sems):
    me = _my_id()
    sib, sib_lin = _peer(1)
    for t, m in enumerate(CHIP_RELATIONS):
        peer, plin = _peer(m)
        for g, (src, out) in enumerate(zip(srcs, outs)):
            _gather2_copy(src, out, plin, peer, send_sems, recv_sems, g, 1 + t).wait_recv()
            _gather2_copy(out.at[plin], out, plin, sib, send_sems, recv_sems, g, 4 + t).start()
    for g, (src, out) in enumerate(zip(srcs, outs)):
        _gather2_copy(src, out, sib_lin, sib, send_sems, recv_sems, g, 0).wait_recv()
        for t, m in enumerate(CHIP_RELATIONS):
            _gather2_copy(src, out, _peer(m | 1)[1], sib, send_sems, recv_sems, g, 4 + t).wait_recv()
    for g, (src, out) in enumerate(zip(srcs, outs)):
        _gather2_copy(src, out, me, sib, send_sems, recv_sems, g, 0).wait_send()
        for t, m in enumerate(CHIP_RELATIONS):
            peer, plin = _peer(m)
            _gather2_copy(src, out, me, peer, send_sems, recv_sems, g, 1 + t).wait_send()
            _gather2_copy(out.at[plin], out, plin, sib, send_sems, recv_sems, g, 4 + t).wait_send()
        pltpu.make_async_copy(src, out.at[me], local_sems.at[g]).wait()


def _gather_scratch(n):
    return [pltpu.SemaphoreType.DMA((n, N_DEV - 1)), pltpu.SemaphoreType.DMA((n, N_DEV - 1)), pltpu.SemaphoreType.DMA((n,))]


def _exchange_copies(srcs, dsts, send_sems, recv_sems):
    copies = []
    for m in range(1, N_DEV):
        peer, plin = _peer(m)
        for g, ((src, per_peer), dst) in enumerate(zip(srcs, dsts)):
            copies.append(pltpu.make_async_remote_copy(
                src_ref=src.at[plin] if per_peer else src, dst_ref=dst.at[m - 1], send_sem=send_sems.at[g, m - 1],
                recv_sem=recv_sems.at[g, m - 1], device_id=peer, device_id_type=MESH))
    return copies


def _exchange_start(*a):
    for cp in _exchange_copies(*a):
        cp.start()


def _exchange_wait(*a):
    copies = _exchange_copies(*a)
    for cp in copies:
        cp.wait_recv()
    for cp in copies:
        cp.wait_send()


def _exchange_scratch(n):
    return [pltpu.SemaphoreType.DMA((n, N_DEV - 1)), pltpu.SemaphoreType.DMA((n, N_DEV - 1))]


MLA_SCALE = (MLA_NOPE + MLA_ROPE) ** -0.5
MLA_C2 = MLA_SCALE * math.log2(math.e)
ONE_LANE = MLA_V
NEG = -1e30
HPS = 4
HPB = 4


def _tri_mask(n, lower_rows_ge_cols=True):
    r = lax.broadcasted_iota(jnp.int32, (n, n), 0)
    c = lax.broadcasted_iota(jnp.int32, (n, n), 1)
    return r >= c if lower_rows_ge_cols else c >= r


def mla_fwd(q, k, v, gather, *, tq):
    T = q.shape[0]
    nq = T // tq
    rep = tq // HP
    ng = len(gather)

    def body(*refs):
        q_ref, k_ref, v_ref = refs[:3]
        srcs = refs[3:3 + ng]
        o_ref, lse_ref = refs[3 + ng:5 + ng]
        outs = refs[5 + ng:5 + 2 * ng]
        m_sc, acc_sc = refs[5 + 2 * ng:7 + 2 * ng]
        comm = (srcs, outs) + tuple(refs[7 + 2 * ng:])
        h, i = pl.program_id(0), pl.program_id(1)

        @pl.when((h == 0) & (i == 0))
        def _():
            _gather_start(*comm)

        m_sc[...] = jnp.full(m_sc.shape, NEG, F32)
        acc_sc[...] = jnp.zeros(acc_sc.shape, F32)
        heads = [slice(t * HP, (t + 1) * HP) for t in range(HPS)]

        def block(j, masked):
            off = pl.multiple_of(j * tq, tq)
            ss = [_dot_nt(q_ref[:, hd], k_ref[pl.ds(off, tq), hd]) * MLA_C2 for hd in heads]
            for hd, s in zip(heads, ss):
                if masked:
                    s = jnp.where(_tri_mask(tq), s, NEG)
                m_prev = m_sc[:, hd]
                m_next = jnp.maximum(m_prev, jnp.max(s, axis=-1, keepdims=True))
                p = jnp.exp2(s - jnp.tile(m_next, (1, rep)))
                alpha = jnp.exp2(m_prev - m_next)
                acc_sc[:, hd] = alpha * acc_sc[:, hd] + _dot(p.astype(BF16), v_ref[pl.ds(off, tq), hd])
                m_sc[:, hd] = m_next

        def loop_body(j, carry):
            block(j, False)
            return carry

        lax.fori_loop(0, i, loop_body, 0)
        block(i, True)
        for hd in heads:
            acc = acc_sc[:, hd]
            l = acc[:, ONE_LANE:ONE_LANE + 1]
            o_ref[:, hd] = (acc / l).astype(o_ref.dtype)
            lse_ref[:, hd] = m_sc[:, hd] + jnp.log(l) * math.log2(math.e)

        @pl.when((h == MLA_H // HPS - 1) & (i == nq - 1))
        def _():
            _gather_wait(*comm)

    blk = pl.BlockSpec((tq, HPS * HP), lambda h, i: (i, h))
    full = pl.BlockSpec((T, HPS * HP), lambda h, i: (0, h))
    return pl.pallas_call(
        body,
        name="mla_fwd",
        grid=(MLA_H // HPS, nq),
        in_specs=[blk, full, full] + [ANY] * ng,
        out_specs=[blk, blk] + [ANY] * ng,
        out_shape=[_sds((T, MLA_H * HP), BF16), _sds((T, MLA_H * HP), F32)]
        + [_sds((N_DEV,) + p.shape, p.dtype) for p in gather],
        scratch_shapes=[pltpu.VMEM((tq, HPS * HP), F32), pltpu.VMEM((tq, HPS * HP), F32)] + _gather_scratch(ng),
        compiler_params=_cp(2),
    )(q, k, v, *gather)


def mla_prep(o, do, lse, *, tq):
    T = o.shape[0]
    nq = T // tq

    def body(o_ref, do_ref, lse_ref, lse_row_ref, delta_row_ref):
        for h in range(MLA_H):
            sl = slice(h * HP, (h + 1) * HP)
            d = jnp.sum(o_ref[:, sl].astype(F32) * do_ref[:, sl].astype(F32), axis=-1, keepdims=True)
            lse_row_ref[h] = lse_ref[:, sl].T[0:1, :]
            delta_row_ref[h] = jnp.broadcast_to(d, (tq, HP)).T[0:1, :]

    blk = pl.BlockSpec((tq, MLA_H * HP), lambda i: (i, 0))
    row = pl.BlockSpec((MLA_H, None, 1, tq), lambda i: (0, i, 0, 0))
    return pl.pallas_call(
        body,
        name="mla_prep",
        grid=(nq,),
        in_specs=[blk, blk, blk],
        out_specs=[row, row],
        out_shape=[_sds((MLA_H, nq, 1, tq), F32), _sds((MLA_H, nq, 1, tq), F32)],
        compiler_params=_cp(1),
    )(o, do, lse)


def mla_bwd(q, k, v, do, lse_row, delta_row, slots, *, tq):
    T = q.shape[0]
    nq = T // tq
    ns = len(slots)

    def body(*refs):
        q_ref, k_ref, v_ref, do_ref, lse_ref, delta_ref = refs[:6]
        srcs = [(r, True) for r in refs[6:6 + ns]]
        dq_ref, dk_ref, dv_ref = refs[6 + ns:9 + ns]
        dsts = refs[9 + ns:9 + 2 * ns]
        dk_sc, dv_sc = refs[9 + 2 * ns:11 + 2 * ns]
        comm = (srcs, dsts) + tuple(refs[11 + 2 * ns:])
        h, j = pl.program_id(0), pl.program_id(1)

        @pl.when((h == 0) & (j == 0))
        def _():
            _exchange_start(*comm)

        dk_sc[...] = jnp.zeros(dk_sc.shape, F32)
        dv_sc[...] = jnp.zeros(dv_sc.shape, F32)
        heads = [slice(t * HP, (t + 1) * HP) for t in range(HPB)]

        @pl.when(j == 0)
        def _():
            dq_ref[...] = jnp.zeros(dq_ref.shape, F32)

        def block(i, masked):
            off = pl.multiple_of(i * tq, tq)
            sts = [_dot_nt(k_ref[:, hd], q_ref[pl.ds(off, tq), hd]) * MLA_C2 for hd in heads]
            dpts = [_dot_nt(v_ref[:, hd], do_ref[pl.ds(off, tq), hd]) for hd in heads]
            for t, hd in enumerate(heads):
                st = sts[t]
                if masked:
                    st = jnp.where(_tri_mask(tq, False), st, NEG)
                pt = jnp.exp2(st - lse_ref[t, i])
                dv_sc[:, hd] += _dot(pt.astype(BF16), do_ref[pl.ds(off, tq), hd])
                dst = (pt * (dpts[t] - delta_ref[t, i]) * MLA_SCALE).astype(BF16)
                dk_sc[:, hd] += _dot(dst, q_ref[pl.ds(off, tq), hd])
                dq_ref[pl.ds(off, tq), hd] += _dot_tn(dst, k_ref[:, hd])

        block(j, True)

        def loop_body(i, carry):
            block(i, False)
            return carry

        lax.fori_loop(j + 1, nq, loop_body, 0)
        dk_ref[...] = dk_sc[...].astype(dk_ref.dtype)
        dv_ref[...] = dv_sc[...].astype(dv_ref.dtype)

        @pl.when((h == MLA_H // HPB - 1) & (j == nq - 1))
        def _():
            _exchange_wait(*comm)

    blk = pl.BlockSpec((tq, HPB * HP), lambda h, j: (j, h))
    full = pl.BlockSpec((T, HPB * HP), lambda h, j: (0, h), pipeline_mode=pl.Buffered(1))
    rows = pl.BlockSpec((HPB, nq, 1, tq), lambda h, j: (h, 0, 0, 0))
    return pl.pallas_call(
        body,
        name="mla_bwd",
        grid=(MLA_H // HPB, nq),
        in_specs=[full, blk, blk, full, rows, rows] + [ANY] * ns,
        out_specs=[full, blk, blk] + [ANY] * ns,
        out_shape=[_sds((T, MLA_H * HP), F32), _sds((T, MLA_H * HP), BF16), _sds((T, MLA_H * HP), BF16)]
        + [_sds((N_DEV - 1,) + s.shape[1:], s.dtype) for s in slots],
        scratch_shapes=[pltpu.VMEM((tq, HPB * HP), F32), pltpu.VMEM((tq, HPB * HP), F32)] + _exchange_scratch(ns),
        compiler_params=_cp(2),
    )(q, k, v, do, lse_row, delta_row, *slots)


def _ret_consts():
    h = jnp.arange(RET_H, dtype=F32)
    log_g = jnp.log1p(-jnp.exp2(-5.0 - h))
    idx = jnp.arange(RET_C, dtype=F32)
    rel = idx[:, None] - idx[None, :]
    dmask = jnp.where(rel >= 0, jnp.exp(log_g[:, None, None] * jnp.maximum(rel, 0.0)), 0.0)
    zeta = jnp.exp(log_g[:, None] * (RET_C - 1.0 - idx)[None, :])
    xi = jnp.exp(log_g[:, None] * (idx + 1.0)[None, :])
    decay = jnp.exp(log_g * RET_C)
    zb = jnp.broadcast_to(zeta[:, :, None], (RET_H, RET_C, RET_D))
    xb = jnp.broadcast_to(xi[:, :, None], (RET_H, RET_C, RET_D))
    db = jnp.broadcast_to(decay[:, None, None], (RET_H, RET_C, RET_D))
    return dmask.astype(F32), zb.astype(F32), xb.astype(F32), db.astype(F32)


def ret_fwd(rq, rk, rv, consts, *, rb):
    T = rq.shape[0]
    nb = T // rb
    ncb = rb // RET_C

    def body(q_ref, k_ref, v_ref, dm_ref, z_ref, x_ref, dc_ref, o_ref, st_ref, r_sc):
        @pl.when(pl.program_id(0) == 0)
        def _():
            r_sc[...] = jnp.zeros(r_sc.shape, F32)

        for c in range(ncb):
            sl = slice(c * RET_C, (c + 1) * RET_C)
            for h in range(RET_H):
                hd = slice(h * RET_D, (h + 1) * RET_D)
                q, k, v = q_ref[sl, hd], k_ref[sl, hd], v_ref[sl, hd]
                r = r_sc[h]
                rbf = r.astype(BF16)
                st_ref[sl, hd] = rbf
                s = _dot_nt(q, k) * dm_ref[h]
                inner = _dot(s.astype(BF16), v)
                cross = _dot((q.astype(F32) * x_ref[h]).astype(BF16), rbf)
                o_ref[sl, hd] = inner + cross
                kz = (k.astype(F32) * z_ref[h]).T.astype(BF16)
                r_sc[h] = r * dc_ref[h] + _dot(kz, v)

    blk = pl.BlockSpec((rb, RET_H * RET_D), lambda b: (b, 0))
    cst = pl.BlockSpec((RET_H, RET_C, RET_D), lambda b: (0, 0, 0))
    return pl.pallas_call(
        body,
        name="ret_fwd",
        grid=(nb,),
        in_specs=[blk, blk, blk, cst, cst, cst, cst],
        out_specs=[blk, blk],
        out_shape=[_sds((T, RET_H * RET_D), F32), _sds((T, RET_H * RET_D), BF16)],
        scratch_shapes=[pltpu.VMEM((RET_H, RET_D, RET_D), F32)],
        compiler_params=_cp(1),
    )(rq, rk, rv, *consts)


def ret_bwd(rq, rk, rv, st, dret, consts, *, rb):
    T = rq.shape[0]
    nb = T // rb
    ncb = rb // RET_C

    def body(q_ref, k_ref, v_ref, st_ref, do_ref, dm_ref, z_ref, x_ref, dc_ref, dq_ref, dk_ref, dv_ref, g_sc):
        @pl.when(pl.program_id(0) == 0)
        def _():
            g_sc[...] = jnp.zeros(g_sc.shape, F32)

        for c in reversed(range(ncb)):
            sl = slice(c * RET_C, (c + 1) * RET_C)
            for h in range(RET_H):
                hd = slice(h * RET_D, (h + 1) * RET_D)
                dm, zt, xi = dm_ref[h], z_ref[h], x_ref[h]
                q, k, v, rp = q_ref[sl, hd], k_ref[sl, hd], v_ref[sl, hd], st_ref[sl, hd]
                dob = do_ref[sl, hd].astype(BF16)
                qf, kf = q.astype(F32), k.astype(F32)
                gn = g_sc[h]
                gnb = gn.astype(BF16)
                s = _dot_nt(q, k) * dm
                ds = _dot_nt(dob, v) * dm
                dq = _dot(ds.astype(BF16), k) + _dot_nt(dob, rp) * xi
                dk = _dot(ds.T.astype(BF16), q) + _dot_nt(v, gnb) * zt
                dv = _dot(s.T.astype(BF16), dob) + _dot((kf * zt).astype(BF16), gnb)
                dq_ref[sl, hd] = dq.astype(dq_ref.dtype)
                dk_ref[sl, hd] = dk.astype(dk_ref.dtype)
                dv_ref[sl, hd] = dv.astype(dv_ref.dtype)
                g_sc[h] = _dot((qf * xi).T.astype(BF16), dob) + dc_ref[h] * gn

    blk = pl.BlockSpec((rb, RET_H * RET_D), lambda b: (nb - 1 - b, 0))
    cst = pl.BlockSpec((RET_H, RET_C, RET_D), lambda b: (0, 0, 0))
    return pl.pallas_call(
        body,
        name="ret_bwd",
        grid=(nb,),
        in_specs=[blk, blk, blk, blk, blk, cst, cst, cst, cst],
        out_specs=[blk, blk, blk],
        out_shape=[_sds((T, RET_H * RET_D), F32)] * 3,
        scratch_shapes=[pltpu.VMEM((RET_H, RET_D, RET_D), F32)],
        compiler_params=_cp(1),
    )(rq, rk, rv, st, dret, *consts)


HALO = 16


def conv_act_fwd(up_pre, w_conv, b_conv, *, tile, cw):
    T = up_pre.shape[0]
    nt = T // tile
    ncol = D_FF // cw
    hb = tile // HALO

    def body(pa_ref, a_ref, pb_ref, b_ref, wa_ref, wb_ref, ba_ref, bb_ref, o_ref):
        i = pl.program_id(1)
        keep = (i > 0).astype(F32)

        def conv(prev_ref, cur_ref, w_ref, bias_ref):
            ext = jnp.concatenate([prev_ref[...].astype(F32) * keep, cur_ref[...].astype(F32)], axis=0)
            w = w_ref[...]
            y = ext * w[2:3, :] + pltpu.roll(ext, 1, 0) * w[1:2, :] + pltpu.roll(ext, 2, 0) * w[0:1, :] + bias_ref[...]
            return y[HALO:, :]

        a = conv(pa_ref, a_ref, wa_ref, ba_ref)
        b = conv(pb_ref, b_ref, wb_ref, bb_ref)
        o_ref[...] = (a * _sigmoid(a) * b).astype(o_ref.dtype)

    prev_a = pl.BlockSpec((HALO, cw), lambda j, i: (jnp.maximum(i * hb - 1, 0), j))
    cur_a = pl.BlockSpec((tile, cw), lambda j, i: (i, j))
    prev_b = pl.BlockSpec((HALO, cw), lambda j, i: (jnp.maximum(i * hb - 1, 0), j + ncol))
    cur_b = pl.BlockSpec((tile, cw), lambda j, i: (i, j + ncol))
    w_a = pl.BlockSpec((3, cw), lambda j, i: (0, j))
    w_b = pl.BlockSpec((3, cw), lambda j, i: (0, j + ncol))
    bias_a = pl.BlockSpec((1, cw), lambda j, i: (0, j))
    bias_b = pl.BlockSpec((1, cw), lambda j, i: (0, j + ncol))
    return pl.pallas_call(
        body,
        name="conv_act_fwd",
        grid=(ncol, nt),
        in_specs=[prev_a, cur_a, prev_b, cur_b, w_a, w_b, bias_a, bias_b],
        out_specs=pl.BlockSpec((tile, cw), lambda j, i: (i, j)),
        out_shape=_sds((T, D_FF), BF16),
        compiler_params=_cp(2),
    )(up_pre, up_pre, up_pre, up_pre, w_conv, w_conv, b_conv, b_conv)


def conv_act_bwd(up_pre, dact, w_conv, b_conv, *, tile, cw):
    T = up_pre.shape[0]
    nt = T // tile
    ncol = D_FF // cw
    hb = tile // HALO
    ext_rows = tile + 2 * HALO

    def body(pa_ref, a_ref, na_ref, pb_ref, b_ref, nb_ref, d_ref, nd_ref, wa_ref, wb_ref, ba_ref, bb_ref,
             dxa_ref, dxb_ref, sa_ref, sb_ref):
        i = pl.program_id(1)
        keep_p = (i > 0).astype(F32)
        keep_n = (i < nt - 1).astype(F32)

        def ext_of(prev_ref, cur_ref, next_ref):
            return jnp.concatenate(
                [prev_ref[...].astype(F32) * keep_p, cur_ref[...].astype(F32), next_ref[...].astype(F32) * keep_n], axis=0)

        def taps(ext):
            return ext, pltpu.roll(ext, 1, 0), pltpu.roll(ext, 2, 0)

        def conv(tp, w, bias):
            return tp[0] * w[2:3, :] + tp[1] * w[1:2, :] + tp[2] * w[0:1, :] + bias

        xa = taps(ext_of(pa_ref, a_ref, na_ref))
        xb = taps(ext_of(pb_ref, b_ref, nb_ref))
        wa, wb = wa_ref[...], wb_ref[...]
        a = conv(xa, wa, ba_ref[...])
        b = conv(xb, wb, bb_ref[...])
        dy = jnp.concatenate(
            [jnp.zeros((HALO, cw), F32), d_ref[...].astype(F32), nd_ref[...].astype(F32) * keep_n], axis=0)
        sg = _sigmoid(a)
        da = dy * b * (sg * (1.0 + a * (1.0 - sg)))
        db = dy * (a * sg)

        def back(dup, tp, w, dx_ref, s_ref):
            dx = dup * w[2:3, :] + pltpu.roll(dup, ext_rows - 1, 0) * w[1:2, :] + pltpu.roll(dup, ext_rows - 2, 0) * w[0:1, :]
            dx_ref[...] = dx[HALO:HALO + tile, :].astype(dx_ref.dtype)
            dc = dup[HALO:HALO + tile, :]
            r2 = jnp.sum(dc * tp[0][HALO:HALO + tile, :], axis=0, keepdims=True)
            r1 = jnp.sum(dc * tp[1][HALO:HALO + tile, :], axis=0, keepdims=True)
            r0 = jnp.sum(dc * tp[2][HALO:HALO + tile, :], axis=0, keepdims=True)
            rb = jnp.sum(dc, axis=0, keepdims=True)
            row = lax.broadcasted_iota(jnp.int32, (8, cw), 0)
            upd = (jnp.where(row == 0, r0, 0.0) + jnp.where(row == 1, r1, 0.0) + jnp.where(row == 2, r2, 0.0)
                   + jnp.where(row == 3, rb, 0.0))

            @pl.when(i == 0)
            def _():
                s_ref[...] = upd

            @pl.when(i > 0)
            def _():
                s_ref[...] += upd

        back(da, xa, wa, dxa_ref, sa_ref)
        back(db, xb, wb, dxb_ref, sb_ref)

    def prev_of(shift):
        return pl.BlockSpec((HALO, cw), lambda j, i: (jnp.maximum(i * hb - 1, 0), j + shift))

    def next_of(shift):
        return pl.BlockSpec((HALO, cw), lambda j, i: (jnp.minimum((i + 1) * hb, nt * hb - 1), j + shift))

    def cur_of(shift):
        return pl.BlockSpec((tile, cw), lambda j, i: (i, j + shift))

    def row_of(rows, shift):
        return pl.BlockSpec((rows, cw), lambda j, i: (0, j + shift))

    return pl.pallas_call(
        body,
        name="conv_act_bwd",
        grid=(ncol, nt),
        in_specs=[prev_of(0), cur_of(0), next_of(0), prev_of(ncol), cur_of(ncol), next_of(ncol), cur_of(0), next_of(0),
                  row_of(3, 0), row_of(3, ncol), row_of(1, 0), row_of(1, ncol)],
        out_specs=[cur_of(0), cur_of(0), row_of(8, 0), row_of(8, 0)],
        out_shape=[_sds((T, D_FF), BF16), _sds((T, D_FF), BF16), _sds((8, D_FF), F32), _sds((8, D_FF), F32)],
        compiler_params=_cp(2),
    )(up_pre, up_pre, up_pre, up_pre, up_pre, up_pre, dact, dact, w_conv, w_conv, b_conv, b_conv)


NS_IN, NS_UP, NS_XKV, NS_UQ = 564, 704, 256, 96
E2_ROWS, E2_UQ = 384, 128
L1_W = NS_UP + NS_XKV
L2_ROWS = 736
L2_OUT, L2_XQ, L2_XO, L2_DN = 0, 128, 256, 384
RS_DN = 352
L3_ROWS, L3_PRET = 1024, 512
L4_SHAPE = (8, 768)
WZ_RUNS = ((416, 4096, 0), (0, 384, 4096), (384, 32, 4544))
WZ_ZERO = ((4480, 4544), (4576, 4608))


def _pieces(orig_start, length, dst_start, ns):
    out, c, d, end = [], orig_start, dst_start, orig_start + length
    while c < end:
        j, off = c // ns, c % ns
        ln = min(ns - off, end - c)
        out.append((j, off, d, ln))
        c += ln
        d += ln
    return out


def pack_early(w_in, w_ukv, w_uq):
    def body(in_ref, ukv_ref, uq_ref, e1_ref, e2_ref):
        e1_ref[...] = in_ref[...].astype(BF16)
        e2_ref[0:E2_UQ, :] = ukv_ref[...].astype(BF16)
        e2_ref[E2_UQ:E2_ROWS, 0:NS_UQ] = uq_ref[...].astype(BF16)
        e2_ref[E2_UQ:E2_ROWS, NS_UQ:HP] = jnp.zeros((E2_ROWS - E2_UQ, HP - NS_UQ), BF16)

    return pl.pallas_call(
        body,
        name="pack_early",
        in_specs=[VM] * 3,
        out_specs=[VM] * 2,
        out_shape=[_sds((D, NS_IN), BF16), _sds((E2_ROWS, HP), BF16)],
        compiler_params=_cp0(),
    )(w_in, w_ukv, w_uq)


def pack_late(w_up, w_xkv, w_out, w_xq, w_xo, w_down, w_pmla, w_pret, w_conv):
    def body(up_ref, xkv_ref, o_ref, xq_ref, xo_ref, dn_ref, pm_ref, pr_ref, cv_ref, l1_ref, l2_ref, l3_ref, l4_ref):
        l1_ref[:, 0:NS_UP] = up_ref[...].astype(BF16)
        l1_ref[:, NS_UP:L1_W] = xkv_ref[...].astype(BF16)
        l2_ref[L2_OUT:L2_XQ, :] = o_ref[...].astype(BF16)
        l2_ref[L2_XQ:L2_XO, :] = xq_ref[...].astype(BF16)
        l2_ref[L2_XO:L2_DN, :] = xo_ref[...].astype(BF16)
        l2_ref[L2_DN:L2_ROWS, :] = dn_ref[...].astype(BF16)
        l3_ref[0:L3_PRET, :] = pm_ref[...].astype(BF16)
        l3_ref[L3_PRET:L3_ROWS, :] = pr_ref[...].astype(BF16)
        l4_ref[...] = jnp.zeros(L4_SHAPE, F32)
        l4_ref[0:3, 0:NS_UP] = cv_ref[...]

    return pl.pallas_call(
        body,
        name="pack_late",
        in_specs=[VM] * 9,
        out_specs=[VM] * 4,
        out_shape=[_sds((D, L1_W), BF16), _sds((L2_ROWS, D), BF16), _sds((L3_ROWS, HP), BF16), _sds(L4_SHAPE, F32)],
        compiler_params=_cp0(),
    )(w_up, w_xkv, w_out, w_xq, w_xo, w_down, w_pmla, w_pret, w_conv)


def assemble_early(g1, g2, *, tile):
    def body(g1_ref, g2_ref, wz_ref, wk_ref, wv_ref, wq_ref):
        for lo, hi in WZ_ZERO:
            wz_ref[:, lo:hi] = jnp.zeros((tile, hi - lo), BF16)
        for os_, ln_, ds_ in WZ_RUNS:
            for j, off, d, ln in _pieces(os_, ln_, ds_, NS_IN):
                wz_ref[:, d:d + ln] = g1_ref[j, :, off:off + ln]

        @pl.when(pl.program_id(0) == 0)
        def _():
            half = jnp.zeros((MLA_KVR, HP - MLA_NOPE), BF16)
            for j in range(N_DEV):
                wk_ref[:, j * HP:j * HP + MLA_NOPE] = g2_ref[j, 0:E2_UQ, 0:MLA_NOPE]
                wk_ref[:, j * HP + MLA_NOPE:(j + 1) * HP] = half
                wv_ref[:, j * HP:j * HP + MLA_V] = g2_ref[j, 0:E2_UQ, MLA_NOPE:HP]
                wv_ref[:, j * HP + MLA_V:(j + 1) * HP] = half
                wq_ref[:, j * HP:j * HP + NS_UQ] = g2_ref[j, E2_UQ:E2_ROWS, 0:NS_UQ]
                wq_ref[:, j * HP + NS_UQ:(j + 1) * HP] = jnp.zeros((MLA_QR, HP - NS_UQ), BF16)

    def whole(r):
        return pl.BlockSpec((r, MLA_H * HP), lambda i: (0, 0))

    return pl.pallas_call(
        body,
        name="assemble_early",
        grid=(D // tile,),
        in_specs=[pl.BlockSpec((N_DEV, tile, NS_IN), lambda i: (0, i, 0)),
                  pl.BlockSpec((N_DEV, E2_ROWS, HP), lambda i: (0, 0, 0))],
        out_specs=[pl.BlockSpec((tile, ZW), lambda i: (i, 0)), whole(MLA_KVR), whole(MLA_KVR), whole(MLA_QR)],
        out_shape=[_sds((D, ZW), BF16), _sds((MLA_KVR, MLA_H * HP), BF16), _sds((MLA_KVR, MLA_H * HP), BF16),
                   _sds((MLA_QR, MLA_H * HP), BF16)],
        compiler_params=_cp(1),
    )(g1, g2)


def assemble_l1(g1, *, tile):
    def body(g_ref, wup_ref, wxkv_ref):
        for j in range(N_DEV):
            wup_ref[:, j * NS_UP:(j + 1) * NS_UP] = g_ref[j, :, 0:NS_UP]
            wxkv_ref[:, j * NS_XKV:(j + 1) * NS_XKV] = g_ref[j, :, NS_UP:L1_W]

    return pl.pallas_call(
        body,
        name="assemble_l1",
        grid=(D // tile,),
        in_specs=[pl.BlockSpec((N_DEV, tile, L1_W), lambda i: (0, i, 0))],
        out_specs=[pl.BlockSpec((tile, 2 * D_FF), lambda i: (i, 0)), pl.BlockSpec((tile, 2 * D), lambda i: (i, 0))],
        out_shape=[_sds((D, 2 * D_FF), BF16), _sds((D, 2 * D), BF16)],
        compiler_params=_cp(1),
    )(g1)


def assemble_l234(g2, g3, g4):
    def body(g2_ref, g3_ref, g4_ref, wo_ref, wxq_ref, wxo_ref, wdn_ref, wpa_ref, wpr_ref, wc_ref):
        for j in range(N_DEV):
            wo_ref[j * 128:(j + 1) * 128, :] = g2_ref[j, L2_OUT:L2_XQ, :]
            wxq_ref[j * 128:(j + 1) * 128, :] = g2_ref[j, L2_XQ:L2_XO, :]
            wxo_ref[j * 128:(j + 1) * 128, :] = g2_ref[j, L2_XO:L2_DN, :]
            wdn_ref[j * RS_DN:(j + 1) * RS_DN, :] = g2_ref[j, L2_DN:L2_ROWS, :]
            wpr_ref[:, j * 128:(j + 1) * 128] = g3_ref[j, L3_PRET:L3_ROWS, :]
            wc_ref[:, j * NS_UP:(j + 1) * NS_UP] = g4_ref[j, 0:3, 0:NS_UP]
            for h in range(MLA_H):
                wpa_ref[h * HP:h * HP + MLA_V, j * 128:(j + 1) * 128] = g3_ref[j, h * MLA_V:(h + 1) * MLA_V, :]
        for h in range(MLA_H):
            wpa_ref[h * HP + MLA_V:(h + 1) * HP, :] = jnp.zeros((HP - MLA_V, D), BF16)

    return pl.pallas_call(
        body,
        name="assemble_l234",
        in_specs=[VM] * 3,
        out_specs=[VM] * 7,
        out_shape=[_sds((D, D), BF16), _sds((D, D), BF16), _sds((D, D), BF16), _sds((D_FF, D), BF16),
                   _sds((MLA_H * HP, D), BF16), _sds((RET_H * RET_D, D), BF16), _sds((3, 2 * D_FF), F32)],
        compiler_params=_cp0(),
    )(g2, g3, g4)


def slots_l1(dwup_a, dwup_b, dwxkv, *, tile):
    per_half = D_FF // NS_UP

    def body(ua_ref, ub_ref, x_ref, s_ref):
        for j in range(N_DEV):
            src = ua_ref if j < per_half else ub_ref
            c0 = (j % per_half) * NS_UP
            s_ref[j, :, 0:NS_UP] = src[:, c0:c0 + NS_UP].astype(BF16)
            s_ref[j, :, NS_UP:L1_W] = x_ref[:, j * NS_XKV:(j + 1) * NS_XKV].astype(BF16)

    return pl.pallas_call(
        body,
        name="slots_l1",
        grid=(D // tile,),
        in_specs=[pl.BlockSpec((tile, D_FF), lambda i: (i, 0)), pl.BlockSpec((tile, D_FF), lambda i: (i, 0)),
                  pl.BlockSpec((tile, 2 * D), lambda i: (i, 0))],
        out_specs=pl.BlockSpec((N_DEV, tile, L1_W), lambda i: (0, i, 0)),
        out_shape=_sds((N_DEV, D, L1_W), BF16),
        compiler_params=_cp(1),
    )(dwup_a, dwup_b, dwxkv)


def slots_l23(dwo, dwxq, dwxo, dwdn, dwpa, dwpr):
    def body(o_ref, xq_ref, xo_ref, dn_ref, pa_ref, pr_ref, l2_ref, l3_ref):
        l2_ref[L2_OUT:L2_XQ, :] = o_ref[...].astype(BF16)
        l2_ref[L2_XQ:L2_XO, :] = xq_ref[...].astype(BF16)
        l2_ref[L2_XO:L2_DN, :] = xo_ref[...].astype(BF16)
        l2_ref[L2_DN:L2_ROWS, :] = dn_ref[...].astype(BF16)
        for h in range(MLA_H):
            l3_ref[h * MLA_V:(h + 1) * MLA_V, :] = pa_ref[h * HP:h * HP + MLA_V, :].astype(BF16)
        l3_ref[L3_PRET:L3_ROWS, :] = pr_ref[...].astype(BF16)

    rows128 = pl.BlockSpec((128, D), lambda j: (j, 0))

    def cols(r):
        return pl.BlockSpec((r, 128), lambda j: (0, j))

    return pl.pallas_call(
        body,
        name="slots_l23",
        grid=(N_DEV,),
        in_specs=[rows128, rows128, rows128, pl.BlockSpec((RS_DN, D), lambda j: (j, 0)),
                  cols(MLA_H * HP), cols(RET_H * RET_D)],
        out_specs=[pl.BlockSpec((None, L2_ROWS, D), lambda j: (j, 0, 0)),
                   pl.BlockSpec((None, L3_ROWS, HP), lambda j: (j, 0, 0))],
        out_shape=[_sds((N_DEV, L2_ROWS, D), BF16), _sds((N_DEV, L3_ROWS, HP), BF16)],
        compiler_params=_cp(1),
    )(dwo, dwxq, dwxo, dwdn, dwpa, dwpr)


def slots_early(dwz, dwk, dwv, dwq, *, tile):
    def body(dz_ref, k_ref, v_ref, q_ref, s1_ref, s2_ref):
        for os_, ln_, ds_ in WZ_RUNS:
            for j, off, d, ln in _pieces(os_, ln_, ds_, NS_IN):
                s1_ref[j, :, off:off + ln] = dz_ref[:, d:d + ln].astype(BF16)

        @pl.when(pl.program_id(0) == 0)
        def _():
            for j in range(N_DEV):
                s2_ref[j, 0:E2_UQ, 0:MLA_NOPE] = k_ref[:, j * HP:j * HP + MLA_NOPE].astype(BF16)
                s2_ref[j, 0:E2_UQ, MLA_NOPE:HP] = v_ref[:, j * HP:j * HP + MLA_V].astype(BF16)
                s2_ref[j, E2_UQ:E2_ROWS, 0:NS_UQ] = q_ref[:, j * HP:j * HP + NS_UQ].astype(BF16)
                s2_ref[j, E2_UQ:E2_ROWS, NS_UQ:HP] = jnp.zeros((E2_ROWS - E2_UQ, HP - NS_UQ), BF16)

    def whole(r):
        return pl.BlockSpec((r, MLA_H * HP), lambda i: (0, 0))

    return pl.pallas_call(
        body,
        name="slots_early",
        grid=(D // tile,),
        in_specs=[pl.BlockSpec((tile, ZW), lambda i: (i, 0)), whole(MLA_KVR), whole(MLA_KVR), whole(MLA_QR)],
        out_specs=[pl.BlockSpec((N_DEV, tile, NS_IN), lambda i: (0, i, 0)),
                   pl.BlockSpec((N_DEV, E2_ROWS, HP), lambda i: (0, 0, 0))],
        out_shape=[_sds((N_DEV, D, NS_IN), BF16), _sds((N_DEV, E2_ROWS, HP), BF16)],
        compiler_params=_cp(1),
    )(dwz, dwk, dwv, dwq)


SMALL = (("g_mix", 1024, 0), ("b_gate", 2048, 1), ("g_q_lat", 256, 3), ("g_kv_lat", 128, 4), ("g_ret", 512, 5),
         ("g_cross", 1024, 6), ("g_mem", 1024, 7), ("g_ffn", 1024, 8), ("b_conv", 5632, 9), ("g_final", 1024, 15))
SMALL_DIRECT = tuple(s for s in SMALL if s[0] != "b_conv")
S_ROWS = 16


def _flat_pieces(n, row0):
    return [(row0 + k // D, k, min(D, n - k)) for k in range(0, n, D)]


def exchange_small(cs_a, cs_b, smalls):
    ns = len(smalls)

    def body(*refs):
        ca_ref, cb_ref = refs[:2]
        small_refs = refs[2:2 + ns]
        rd_ref, rs_ref, dsl_ref, own_ref = refs[2 + ns:6 + ns]
        sems = refs[6 + ns:]
        own_ref[...] = jnp.zeros(own_ref.shape, F32)
        for (name, n, row0), g_ref in zip(SMALL_DIRECT, small_refs):
            for r, c0, ln in _flat_pieces(n, row0):
                own_ref[r:r + 1, 0:ln] = g_ref[:, c0:c0 + ln]
        row0 = dict((s[0], s[2]) for s in SMALL)["b_conv"]
        for half, c_ref in enumerate((ca_ref, cb_ref)):
            k = half * D_FF
            end = k + D_FF
            while k < end:
                r, lane = row0 + k // D, k % D
                ln = min(D - lane, end - k)
                own_ref[r:r + 1, lane:lane + ln] = c_ref[3:4, k - half * D_FF:k - half * D_FF + ln]
                k += ln
        dsl_ref[...] = jnp.zeros(dsl_ref.shape, F32)
        per_half = D_FF // NS_UP
        for j in range(N_DEV):
            c_ref = ca_ref if j < per_half else cb_ref
            c0 = (j % per_half) * NS_UP
            dsl_ref[j, 0:3, 0:NS_UP] = c_ref[0:3, c0:c0 + NS_UP]
        comm = ([(dsl_ref, True), (own_ref, False)], [rd_ref, rs_ref]) + tuple(sems)
        _exchange_start(*comm)
        _exchange_wait(*comm)

    n1 = N_DEV - 1
    return pl.pallas_call(
        body,
        name="exchange_small",
        in_specs=[VM] * (2 + ns),
        out_specs=[VM, VM, VM, VM],
        out_shape=[_sds((n1,) + L4_SHAPE, F32), _sds((n1, S_ROWS, D), F32), _sds((N_DEV,) + L4_SHAPE, F32),
                   _sds((S_ROWS, D), F32)],
        scratch_shapes=_exchange_scratch(2),
        compiler_params=_cp0(),
    )(cs_a, cs_b, *smalls)


def _adamw(w, g, m, v):
    m = ADAM_B1 * m + (1.0 - ADAM_B1) * g
    v = ADAM_B2 * v + (1.0 - ADAM_B2) * (g * g)
    m_hat = m / (1.0 - ADAM_B1 ** ADAM_STEP)
    v_hat = v / (1.0 - ADAM_B2 ** ADAM_STEP)
    delta = -ADAM_LR * (m_hat / (jnp.sqrt(v_hat) + ADAM_EPS) + ADAM_WD * w)
    return delta, m, v


def _apply(g, refs, outs):
    d, mn, vn = _adamw(refs[0][...], g, refs[1][...], refs[2][...])
    outs[0][...] = g
    outs[1][...] = d
    outs[2][...] = mn
    outs[3][...] = vn


def adam_cols(own, recv, wmv, spans, *, name, tile):
    R, W = own.shape
    nw = len(spans)

    def body(*refs):
        own_ref, recv_ref = refs[:2]
        ins, outs = refs[2:2 + 3 * nw], refs[2 + 3 * nw:]
        g = own_ref[...].astype(F32)
        for k in range(N_DEV - 1):
            g = g + recv_ref[k].astype(F32)
        for t, (lo, hi) in enumerate(spans):
            _apply(g[:, lo:hi], ins[3 * t:3 * t + 3], outs[4 * t:4 * t + 4])

    def blk(w):
        return pl.BlockSpec((tile, w), lambda i: (i, 0))

    widths = [hi - lo for lo, hi in spans]
    return pl.pallas_call(
        body,
        name=name,
        grid=(R // tile,),
        in_specs=[blk(W), pl.BlockSpec((N_DEV - 1, tile, W), lambda i: (0, i, 0))] + [blk(w) for w in widths for _ in range(3)],
        out_specs=[blk(w) for w in widths for _ in range(4)],
        out_shape=[_sds((R, w), F32) for w in widths for _ in range(4)],
        compiler_params=_cp(1),
    )(own, recv, *wmv)


def adam_rows(own, recv, wmv, spans, *, name):
    nw = len(spans)

    def body(*refs):
        own_ref, recv_ref = refs[:2]
        ins, outs = refs[2:2 + 3 * nw], refs[2 + 3 * nw:]
        for t, (lo, hi, w) in enumerate(spans):
            g = own_ref[lo:hi, :].astype(F32)
            for k in range(N_DEV - 1):
                g = g + recv_ref[k, lo:hi, :].astype(F32)
            _apply(g[:, 0:w], ins[3 * t:3 * t + 3], outs[4 * t:4 * t + 4])

    return pl.pallas_call(
        body,
        name=name,
        in_specs=[VM] * (2 + 3 * nw),
        out_specs=[VM] * (4 * nw),
        out_shape=[_sds((hi - lo, w), F32) for lo, hi, w in spans for _ in range(4)],
        compiler_params=_cp0(),
    )(own, recv, *wmv)


def adam_small(own_s, recv_s, dslots, recv_d, wmv_small, wmv_conv):
    ns = len(SMALL)

    def body(*refs):
        own_ref, rs_ref, dsl_ref, rd_ref = refs[:4]
        ins = refs[4:4 + 3 * ns + 3]
        outs = refs[4 + 3 * ns + 3:4 + 3 * ns + 3 + 4 * ns + 4]
        all_sc = refs[-1]
        me = _my_id()
        all_sc[0] = own_ref[...]
        for k in range(N_DEV - 1):
            all_sc[k + 1] = rs_ref[k]
        g = all_sc[jnp.bitwise_xor(me, 0)]
        for s in range(1, N_DEV):
            g = g + all_sc[jnp.bitwise_xor(me, s)]
        all_sc[0] = g
        for t, (name, n, row0) in enumerate(SMALL):
            pieces = [all_sc[0, r:r + 1, 0:ln] for r, _, ln in _flat_pieces(n, row0)]
            gt = pieces[0] if len(pieces) == 1 else jnp.concatenate(pieces, axis=1)
            _apply(gt, ins[3 * t:3 * t + 3], outs[4 * t:4 * t + 4])
        gc = dsl_ref[me]
        for k in range(N_DEV - 1):
            gc = gc + rd_ref[k]
        _apply(gc[0:3, 0:NS_UP], ins[3 * ns:3 * ns + 3], outs[4 * ns:4 * ns + 4])

    out_shape = [_sds((1, n), F32) for _, n, _ in SMALL for _ in range(4)] + [_sds((3, NS_UP), F32)] * 4
    return pl.pallas_call(
        body,
        name="adam_small",
        in_specs=[VM] * (4 + 3 * ns + 3),
        out_specs=[VM] * len(out_shape),
        out_shape=out_shape,
        scratch_shapes=[pltpu.VMEM((N_DEV, S_ROWS, D), F32)],
    )(own_s, recv_s, dslots, recv_d, *wmv_small, *wmv_conv)


def kernel(x, mem, positions, g_mix, w_in, b_gate, g_q_lat, w_uq, g_kv_lat, w_ukv, w_proj_mla, g_ret, w_proj_ret, w_out, g_cross, g_mem, w_xq, w_xkv, w_xo, g_ffn, w_up, w_conv, b_conv, w_down, g_final, loss_target, m_g_mix, m_w_in, m_b_gate, m_g_q_lat, m_w_uq, m_g_kv_lat, m_w_ukv, m_w_proj_mla, m_g_ret, m_w_proj_ret, m_w_out, m_g_cross, m_g_mem, m_w_xq, m_w_xkv, m_w_xo, m_g_ffn, m_w_up, m_w_conv, m_b_conv, m_w_down, m_g_final, v_g_mix, v_w_in, v_b_gate, v_g_q_lat, v_w_uq, v_g_kv_lat, v_w_ukv, v_w_proj_mla, v_g_ret, v_w_proj_ret, v_w_out, v_g_cross, v_g_mem, v_w_xq, v_w_xkv, v_w_xo, v_g_ffn, v_w_up, v_w_conv, v_b_conv, v_w_down, v_g_final):
    args = dict(locals())
    T = x.shape[1]
    M = mem.shape[1]
    tile = min(256, T)
    tile2 = min(512, T)
    tile4 = min(1024, T)
    tq = min(512, T)
    rb = min(1024, T)

    xs = x[0]
    tgt = loss_target[0]
    mems = mem[0]

    def shard(name, prefix=""):
        a = args[prefix + name]
        return a.reshape(a.shape[-2:]) if a.ndim >= 2 else a.reshape(1, -1)

    e1, e2 = pack_early(shard("w_in"), shard("w_ukv"), shard("w_uq"))
    late_parts = pack_late(shard("w_up"), shard("w_xkv"), shard("w_out"), shard("w_xq"), shard("w_xo"), shard("w_down"),
                           shard("w_proj_mla"), shard("w_proj_ret"), shard("w_conv"))

    pos_f = jnp.broadcast_to(positions[0].astype(F32)[:, None], (T, 128))
    inv_m, inv_r = _rot_inv()
    u, cm, s1, s2, cr, sr, ge1, ge2 = rowwise(
        lambda xv, p, im, ir, g: (_rms(xv, g),) + _rot_tables(p, im, ir), [(xs, None), (pos_f, None)],
        [inv_m, inv_r, g_mix], [(D, BF16)] + [(128, F32)] * 5, [], tile=tile2, name="norm_mix_tables", gather=[e1, e2])
    wz, wk, wv, wq = assemble_early(ge1, ge2, tile=tile)
    rconsts = _ret_consts()

    z = matmul(u, wz, name="mm_z", tn=1536)

    def mixers_in(zl, zr, cmv, s1v, s2v, crv, srv, gq, gkv, wqv, wkv_, wvv):
        cq = _rms(zl[:, 0:256], gq).astype(BF16)
        ckv = _rms(zl[:, 256:384], gkv).astype(BF16)
        qv = _rot_mla(_dot(cq, wqv), cmv, s1v, s2v)
        kr = _rot_mla(zl[:, 384:512], cmv, s1v, s2v)
        kn = _dot(ckv, wkv_)
        kv_ = jnp.concatenate([kn[:, h * HP:(h + 1) * HP] + kr for h in range(MLA_H)], axis=1)
        vv = _dot(ckv, wvv)
        lane = lax.broadcasted_iota(jnp.int32, vv.shape, 1)
        vv = jnp.where((lane & (HP - 1)) == ONE_LANE, 1.0, vv)
        rqv = _rot_ret(zr[:, 0:512], crv, srv)
        rkv = _rot_ret(zr[:, 512:1024], crv, srv) * (RET_D ** -0.5)
        return qv, kv_, vv, rqv, rkv, zr[:, 1024:1536]

    q_a, k_a, v_a, rq, rk, rv = rowwise(
        mixers_in, [(z, (512, 8)), (z, (2048, 0)), (cm, None), (s1, None), (s2, None), (cr, None), (sr, None)],
        [g_q_lat, g_kv_lat, wq, wk, wv], [(MLA_H * HP, BF16)] * 3 + [(512, BF16)] * 3, [], tile=tile2, name="mixers_in")

    o_a, lse, gl1, gl2, gl3, gl4 = mla_fwd(q_a, k_a, v_a, list(late_parts), tq=tq)
    wup, wxkv = assemble_l1(gl1, tile=tile)
    wo, wxq, wxo, wdn, wpa, wpr, wcv = assemble_l234(gl2, gl3, gl4)
    ret, rstate = ret_fwd(rq, rk, rv, rconsts, rb=rb)

    def gn_parts(r):
        outs = []
        for h in range(RET_H):
            rh = r[:, h * RET_D:(h + 1) * RET_D]
            mu = jnp.mean(rh, axis=-1, keepdims=True)
            dlt = rh - mu
            rstd = lax.rsqrt(jnp.mean(dlt * dlt, axis=-1, keepdims=True) + EPS)
            outs.append((dlt * rstd, rstd))
        return outs

    def mix_fwd(ov, rv_, rg, gt, wpav, wprv, gr, bg):
        ya = _dot(ov, wpav)
        xh = jnp.concatenate([p[0] for p in gn_parts(rv_)], axis=1)
        t = rg * _sigmoid(rg) * (xh * gr)
        yr = _dot(t.astype(BF16), wprv)
        ga_ = _sigmoid(gt[:, :D] + bg[:, :D])
        gr_ = _sigmoid(gt[:, D:] + bg[:, D:])
        return ga_ * ya + gr_ * yr

    mix = rowwise(mix_fwd, [(o_a, None), (ret, None), (z, (512, 3)), (z, (2048, 1))], [wpa, wpr, g_ret, b_gate],
                  [(D, BF16)], [], tile=tile, name="mix_fwd")[0]
    def proj_norm(av, rv_, wv_, g):
        hv = rv_ + _dot(av, wv_)
        return hv, _rms(hv, g)

    h1, n2 = rowwise(proj_norm, [(mix, None), (xs, None)], [wo, g_cross], [(D, F32), (D, BF16)], [], tile=tile4,
                     name="mm_out_norm")
    xq = matmul(n2, wxq, out_dtype=BF16, name="mm_xq")
    mn = rowwise(lambda mv_, g: _rms(mv_, g), [(mems, None)], [g_mem], [(D, BF16)], [], tile=min(tile, M), name="norm_mem")[0]
    mkv = matmul(mn, wxkv, out_dtype=BF16, name="mm_mkv")

    x_scale = X_HD ** -0.5

    def xattn_fwd(xqv, mkvv):
        outs = []
        for h in range(X_H):
            sl = slice(h * X_HD, (h + 1) * X_HD)
            s = _dot_nt(xqv[:, sl], mkvv[:, sl]) * x_scale
            s = s - jnp.max(s, axis=-1, keepdims=True)
            e = jnp.exp(s)
            p = e / jnp.sum(e, axis=-1, keepdims=True)
            outs.append(_dot(p.astype(BF16), mkvv[:, D + h * X_HD:D + (h + 1) * X_HD]))
        return jnp.concatenate(outs, axis=1)

    xo = rowwise(xattn_fwd, [(xq, None)], [mkv], [(D, BF16)], [], tile=tile, name="xattn_fwd")[0]
    h2, n3 = rowwise(proj_norm, [(xo, None), (h1, None)], [wxo, g_ffn], [(D, F32), (D, BF16)], [], tile=tile4,
                     name="mm_xo_norm")
    up_pre = matmul(n3, wup, out_dtype=BF16, name="mm_up", tn=1408)
    cw = D_FF // 2
    act = conv_act_fwd(up_pre, wcv, b_conv, tile=tile, cw=cw)

    def down_loss(av, hv2, tv, wv_, g):
        hv = hv2 + _dot(av, wv_)
        y = _rms(hv, g)
        err = y - tv
        part = 0.5 * jnp.sum(jnp.sum(err * err, axis=-1, keepdims=True) / D, axis=0, keepdims=True)
        dx, dg = _rms_bwd(hv, g, err / D)
        return dx, dg, jnp.broadcast_to(part, (8, 128))

    g_fin2 = g_final.reshape(1, D)
    dh3, dg_final, loss_acc = rowwise(down_loss, [(act, None), (h2, None), (tgt, None)], [wdn, g_fin2], [(D, F32)],
                                      [((1, D), F32), ((8, 128), F32)], tile=tile2, name="mm_down_loss")
    loss = lax.psum(loss_acc[0, 0], ("x", "y", "c"))

    dact = matmul(dh3, wdn, tb=True, out_dtype=BF16, name="mm_dact", tn=1408)
    dw_down = matmul_tn(act, dh3, name="mm_dw_down", tm=1408)
    dup_a, dup_b, cs_a, cs_b = conv_act_bwd(up_pre, dact, wcv, b_conv, tile=tile, cw=cw)
    dn3 = matmul2_tb(dup_a, dup_b, wup, name="mm_dn3", tm=1024, tn=512)
    dw_up_a = matmul_tn(n3, dup_a, name="mm_dw_up_a", tn=1408)
    dw_up_b = matmul_tn(n3, dup_b, name="mm_dw_up_b", tn=1408)

    def norm_bwd(hv, dyv, drv, g):
        dx, dg = _rms_bwd(hv, g, dyv)
        return dx + drv, dg

    dh2, dg_ffn = rowwise(norm_bwd, [(h2, None), (dn3, None), (dh3, None)], [g_ffn], [(D, F32)], [((1, D), F32)],
                          tile=tile2, name="norm_ffn_bwd")
    dxo = matmul(dh2, wxo, tb=True, out_dtype=BF16, name="mm_dxo")
    dw_xo = matmul_tn(xo, dh2, name="mm_dw_xo")

    def xattn_bwd(xqv, dxov, mkvv):
        dxq, dmk, dmv = [], [], []
        for h in range(X_H):
            sl = slice(h * X_HD, (h + 1) * X_HD)
            slv = slice(D + h * X_HD, D + (h + 1) * X_HD)
            s = _dot_nt(xqv[:, sl], mkvv[:, sl]) * x_scale
            s = s - jnp.max(s, axis=-1, keepdims=True)
            e = jnp.exp(s)
            p = e / jnp.sum(e, axis=-1, keepdims=True)
            dp = _dot_nt(dxov[:, sl], mkvv[:, slv])
            ds = (p * (dp - jnp.sum(dp * p, axis=-1, keepdims=True)) * x_scale).astype(BF16)
            dxq.append(_dot(ds, mkvv[:, sl]))
            dmk.append(_dot_tn(ds, xqv[:, sl]))
            dmv.append(_dot_tn(p.astype(BF16), dxov[:, sl]))
        return jnp.concatenate(dxq, axis=1), jnp.concatenate(dmk + dmv, axis=1)

    dxq, dmkv = rowwise(xattn_bwd, [(xq, None), (dxo, None)], [mkv], [(D, BF16)], [((M, 2 * D), F32)],
                        tile=tile, name="xattn_bwd")
    dw_xq = matmul_tn(n2, dxq, name="mm_dw_xq")

    def proj_norm_bwd(dyv, hv, drv, wv_, g):
        dx, dg = _rms_bwd(hv, g, _dot_nt(dyv, wv_))
        return dx + drv, dg

    dh1, dg_cross = rowwise(proj_norm_bwd, [(dxq, None), (h1, None), (dh2, None)], [wxq, g_cross], [(D, F32)],
                            [((1, D), F32)], tile=tile4, name="mm_dn2_norm_bwd")
    dw_xkv = matmul_tn(mn, dmkv, name="mm_dw_xkv", tk=M)
    dmn = matmul(dmkv, wxkv, tb=True, name="mm_dmn", tm=M)
    dg_mem = rowwise(lambda mv_, dyv, g: _rms_bwd(mv_, g, dyv)[1], [(mems, None), (dmn, None)], [g_mem], [],
                     [((1, D), F32)], tile=min(tile, M), name="norm_mem_bwd")[0]

    dmix = matmul(dh1, wo, tb=True, out_dtype=BF16, name="mm_dmix")
    dw_out = matmul_tn(mix, dh1, name="mm_dw_out")

    def mix_bwd(ov, rv_, rg, gt, dmv_, wpav, wprv, gr, bg):
        dm_ = dmv_.astype(F32)
        ya = _dot(ov, wpav)
        parts = gn_parts(rv_)
        xh = jnp.concatenate([p[0] for p in parts], axis=1)
        yn = xh * gr
        sg = _sigmoid(rg)
        sl_ = rg * sg
        t = (sl_ * yn).astype(BF16)
        yr = _dot(t, wprv)
        ga_ = _sigmoid(gt[:, :D] + bg[:, :D])
        gr_ = _sigmoid(gt[:, D:] + bg[:, D:])
        dgates = jnp.concatenate([dm_ * ya * ga_ * (1.0 - ga_), dm_ * yr * gr_ * (1.0 - gr_)], axis=1)
        dya = (dm_ * ga_).astype(BF16)
        dyr = (dm_ * gr_).astype(BF16)
        do_ = _dot_nt(dya, wpav)
        dwpa_ = _dot_tn(ov, dya)
        dt = _dot_nt(dyr, wprv)
        dwpr_ = _dot_tn(t, dyr)
        drg_ = dt * yn * (sg * (1.0 + rg * (1.0 - sg)))
        dyn = dt * sl_
        dgr = jnp.sum(dyn * xh, axis=0, keepdims=True)
        dxh = dyn * gr
        drets = []
        for h in range(RET_H):
            sl = slice(h * RET_D, (h + 1) * RET_D)
            xhh, rstd = parts[h]
            dxhh = dxh[:, sl]
            drets.append(rstd * (dxhh - jnp.mean(dxhh, axis=-1, keepdims=True)
                                 - xhh * jnp.mean(dxhh * xhh, axis=-1, keepdims=True)))
        dret_ = jnp.concatenate(drets, axis=1)
        dbg = jnp.sum(dgates, axis=0, keepdims=True)
        return do_, dret_, drg_, dgates, dwpa_, dwpr_, dgr, dbg

    do_a, dret, drg, dgates, dwpa, dw_proj_ret, dg_ret, db_gate = rowwise(
        mix_bwd, [(o_a, None), (ret, None), (z, (512, 3)), (z, (2048, 1)), (dmix, None)], [wpa, wpr, g_ret, b_gate],
        [(MLA_H * HP, BF16), (512, F32), (512, BF16), (2 * D, BF16)],
        [((MLA_H * HP, D), F32), ((512, D), F32), ((1, 512), F32), ((1, 2 * D), F32)], tile=tile, name="mix_bwd")

    sl1 = slots_l1(dw_up_a, dw_up_b, dw_xkv, tile=tile)
    sl2, sl3 = slots_l23(dw_out, dw_xq, dw_xo, dw_down, dwpa, dw_proj_ret)
    lse_row, delta_row = mla_prep(o_a, do_a, lse, tq=tq)
    dq_a, dk_a, dv_a, rl1, rl2, rl3 = mla_bwd(q_a, k_a, v_a, do_a, lse_row, delta_row, [sl1, sl2, sl3], tq=tq)
    drq_r, drk_r, drv = ret_bwd(rq, rk, rv, rstate, dret, rconsts, rb=rb)

    def mixers_in_bwd(zl, cmv, s1v, s2v, dqv, dkv_, dvv, drq_, drk_, drv_, drg_, dgt, crv, srv, gq, gkv, wqv, wkv_, wvv):
        cqf, ckvf = zl[:, 0:256], zl[:, 256:384]
        cq = _rms(cqf, gq).astype(BF16)
        ckv = _rms(ckvf, gkv).astype(BF16)
        dq_pre = _rot_mla(dqv.astype(F32), cmv, -s1v, -s2v).astype(BF16)
        dkf = dkv_.astype(F32)
        dkr = dkf[:, 0:HP]
        for h in range(1, MLA_H):
            dkr = dkr + dkf[:, h * HP:(h + 1) * HP]
        lane = lax.broadcasted_iota(jnp.int32, dkr.shape, 1)
        dzk = _rot_mla(jnp.where((lane >= 64) & (lane < 96), dkr, 0.0), cmv, -s1v, -s2v)
        dkb = dkv_.astype(BF16)
        dvb = dvv.astype(BF16)
        dcq_n = _dot_nt(dq_pre, wqv)
        dckv_n = _dot_nt(dkb, wkv_) + _dot_nt(dvb, wvv)
        dwq_ = _dot_tn(cq, dq_pre)
        dwk_ = _dot_tn(ckv, dkb)
        dwv_ = _dot_tn(ckv, dvb)
        dcq, dgq = _rms_bwd(cqf, gq, dcq_n)
        dckv, dgkv = _rms_bwd(ckvf, gkv, dckv_n)
        a = _rot_ret(drq_, crv, -srv)
        b = _rot_ret(drk_, crv, -srv) * (RET_D ** -0.5)
        dz_ = jnp.concatenate([a.astype(BF16), b.astype(BF16), drv_.astype(BF16), drg_, dgt,
                               dcq.astype(BF16), dckv.astype(BF16), dzk.astype(BF16)], axis=1)
        return dz_, dwq_, dwk_, dwv_, dgq, dgkv

    dz, dwq, dwk, dwv, dg_q_lat, dg_kv_lat = rowwise(
        mixers_in_bwd, [(z, (512, 8)), (cm, None), (s1, None), (s2, None), (dq_a, None), (dk_a, None), (dv_a, None),
                        (drq_r, None), (drk_r, None), (drv, None), (drg, None), (dgates, None), (cr, None), (sr, None)],
        [g_q_lat, g_kv_lat, wq, wk, wv], [(ZW, BF16)],
        [((MLA_QR, MLA_H * HP), F32), ((MLA_KVR, MLA_H * HP), F32), ((MLA_KVR, MLA_H * HP), F32),
         ((1, MLA_QR), F32), ((1, MLA_KVR), F32)], tile=tile, name="mixers_in_bwd")
    dwz = matmul_tn(u, dz, name="mm_dw_z", tn=1536)
    se1, se2 = slots_early(dwz, dwk, dwv, dwq, tile=tile)
    du, re1, re2 = matmul(dz, wz, tb=True, name="mm_du", tm=1024, tn=512, exchange=[se1, se2])

    grad_x, dg_mix = rowwise(norm_bwd, [(xs, None), (du, None), (dh1, None)], [g_mix], [(D, F32)], [((1, D), F32)],
                             tile=tile2, name="norm_mix_bwd")

    small_grads = {"g_mix": dg_mix, "b_gate": db_gate, "g_q_lat": dg_q_lat, "g_kv_lat": dg_kv_lat, "g_ret": dg_ret,
                   "g_cross": dg_cross, "g_mem": dg_mem, "g_ffn": dg_ffn, "g_final": dg_final}
    rd, rs, dslots, own_s = exchange_small(cs_a, cs_b, [small_grads[n] for n, _, _ in SMALL_DIRECT])

    me = _my_id()

    def own(slots):
        return lax.dynamic_index_in_dim(slots, me, axis=0, keepdims=False)

    def wmv(names):
        return [shard(n, p) for n in names for p in ("", "m_", "v_")]

    names_s = tuple(n for n, _, _ in SMALL)
    groups = (
        (("w_in",), adam_cols(own(se1), re1, wmv(("w_in",)), ((0, NS_IN),), name="adam_e1", tile=128)),
        (("w_ukv", "w_uq"), adam_rows(own(se2), re2, wmv(("w_ukv", "w_uq")),
                                      ((0, E2_UQ, HP), (E2_UQ, E2_ROWS, NS_UQ)), name="adam_e2")),
        (("w_up", "w_xkv"), adam_cols(own(sl1), rl1, wmv(("w_up", "w_xkv")), ((0, NS_UP), (NS_UP, L1_W)),
                                      name="adam_l1", tile=128)),
        (("w_out", "w_xq", "w_xo", "w_down"),
         adam_rows(own(sl2), rl2, wmv(("w_out", "w_xq", "w_xo", "w_down")),
                   ((L2_OUT, L2_XQ, D), (L2_XQ, L2_XO, D), (L2_XO, L2_DN, D), (L2_DN, L2_ROWS, D)), name="adam_l2")),
        (("w_proj_mla", "w_proj_ret"), adam_rows(own(sl3), rl3, wmv(("w_proj_mla", "w_proj_ret")),
                                                 ((0, L3_PRET, HP), (L3_PRET, L3_ROWS, HP)), name="adam_l3")),
        (names_s + ("w_conv",), adam_small(own_s, rs, dslots, rd, wmv(names_s), wmv(("w_conv",)))),
    )
    res = {}
    for names, outs_ in groups:
        for t, n in enumerate(names):
            res[n] = outs_[4 * t:4 * t + 4]

    order = ["g_mix", "w_in", "b_gate", "g_q_lat", "w_uq", "g_kv_lat", "w_ukv", "w_proj_mla", "g_ret", "w_proj_ret",
             "w_out", "g_cross", "g_mem", "w_xq", "w_xkv", "w_xo", "g_ffn", "w_up", "w_conv", "b_conv", "w_down",
             "g_final"]
    outs = [loss, grad_x[None]]
    for kind in range(4):
        outs += [res[n][kind].reshape(args[n].shape) for n in order]
    return tuple(outs)
```

```python
import functools
import math

import jax
import jax.numpy as jnp
import numpy as np
from jax import lax
from jax.experimental import pallas as pl
from jax.experimental.pallas import tpu as pltpu

F32 = jnp.float32
BF16 = jnp.bfloat16

D = 1024
MLA_H, MLA_NOPE, MLA_ROPE, MLA_V = 8, 64, 32, 64
MLA_QR, MLA_KVR = 256, 128
RET_H, RET_D, RET_C = 4, 128, 128
X_H, X_HD = 4, 256
D_FF = 2816
THETA = 10000.0
EPS = 1e-6
HP = 128
ZW = 4608
N_DEV = 8

ADAM_LR, ADAM_B1, ADAM_B2, ADAM_EPS, ADAM_WD, ADAM_STEP = 0.001, 0.9, 0.999, 1e-08, 0.01, 10

VMEM_LIMIT = 56 * 1024 * 1024
BIG_CONST_BYTES = 4 * 1024 * 1024
MESH = pl.DeviceIdType.MESH
VM = pl.BlockSpec(memory_space=pltpu.VMEM)
ANY = pl.BlockSpec(memory_space=pl.ANY)


def _cp(n_axes):
    return pltpu.CompilerParams(dimension_semantics=("arbitrary",) * n_axes, vmem_limit_bytes=VMEM_LIMIT)


def _cp0():
    return pltpu.CompilerParams(vmem_limit_bytes=VMEM_LIMIT)


def _pick(n, cap, mult=128):
    best = None
    for t in range(mult, min(n, cap) + 1, mult):
        if n % t == 0:
            best = t
    return best if best is not None else n


def _dot(a, b):
    return jnp.dot(a, b, preferred_element_type=F32)


def _dot_nt(a, b):
    return lax.dot_general(a, b, (((1,), (1,)), ((), ())), preferred_element_type=F32)


def _dot_tn(a, b):
    return lax.dot_general(a, b, (((0,), (0,)), ((), ())), preferred_element_type=F32)


def _sds(shape, dtype):
    return jax.ShapeDtypeStruct(shape, dtype)


def matmul(a, b, *, name, tb=False, out_dtype=F32, tm=1024, tn=1024):
    M, K = a.shape
    N = b.shape[0] if tb else b.shape[1]
    tm = _pick(M, tm, 8)
    tn = _pick(N, tn)

    def body(a_ref, b_ref, o_ref):
        av = a_ref[...].astype(BF16)
        bv = b_ref[...].astype(BF16)
        acc = _dot_nt(av, bv) if tb else _dot(av, bv)
        o_ref[...] = acc.astype(o_ref.dtype)

    return pl.pallas_call(
        body,
        name=name,
        grid=(M // tm, N // tn),
        in_specs=[pl.BlockSpec((tm, K), lambda i, j: (i, 0)),
                  pl.BlockSpec((tn, K), lambda i, j: (j, 0)) if tb else pl.BlockSpec((K, tn), lambda i, j: (0, j))],
        out_specs=pl.BlockSpec((tm, tn), lambda i, j: (i, j)),
        out_shape=_sds((M, N), out_dtype),
        compiler_params=_cp(2),
    )(a, b)


def matmul_tn(a, b, *, name, tm=1024, tn=1024, tk=1024):
    R, M = a.shape
    N = b.shape[1]
    tm = _pick(M, tm)
    tn = _pick(N, tn)
    tk = _pick(R, tk, 16)
    nk = R // tk

    def body(a_ref, b_ref, o_ref, acc_ref):
        k = pl.program_id(2)

        @pl.when(k == 0)
        def _():
            acc_ref[...] = jnp.zeros_like(acc_ref)

        acc_ref[...] += _dot_tn(a_ref[...].astype(BF16), b_ref[...].astype(BF16))

        @pl.when(k == nk - 1)
        def _():
            o_ref[...] = acc_ref[...]

    return pl.pallas_call(
        body,
        name=name,
        grid=(M // tm, N // tn, nk),
        in_specs=[pl.BlockSpec((tk, tm), lambda i, j, k: (k, i)), pl.BlockSpec((tk, tn), lambda i, j, k: (k, j))],
        out_specs=pl.BlockSpec((tm, tn), lambda i, j, k: (i, j)),
        out_shape=_sds((M, N), F32),
        scratch_shapes=[pltpu.VMEM((tm, tn), F32)],
        compiler_params=_cp(3),
    )(a, b)


def rowwise(fn, rows, consts, out_rows, out_accs, *, tile, name, gather=(), exchange=()):
    T = rows[0][0].shape[0]
    nt = T // tile
    travel = tuple(gather) + tuple(exchange)
    n_r, n_c, n_o, n_a, n_g = len(rows), len(consts), len(out_rows), len(out_accs), len(travel)
    n_in = n_r + n_c + n_g

    def body(*refs):
        if n_g:
            srcs, dsts = refs[n_r + n_c:n_in], refs[n_in + n_o + n_a:n_in + n_o + n_a + n_g]
            sems = tuple(refs[n_in + n_o + n_a + n_g:])
            if gather:
                comm = (srcs, dsts) + sems
                comm_start, comm_finish = _gather2_start, _gather2_finish
            else:
                comm = ([(r, True) for r in srcs], dsts) + sems
                comm_start, comm_finish = _exchange_start, _exchange_wait

            @pl.when(pl.program_id(0) == 0)
            def _():
                comm_start(*comm)

        ins = [r[...] for r in refs[: n_r + n_c]]
        outs = fn(*ins)
        if not isinstance(outs, (tuple, list)):
            outs = (outs,)
        o_refs = refs[n_in:n_in + n_o]
        a_refs = refs[n_in + n_o:n_in + n_o + n_a]
        for o_ref, o in zip(o_refs, outs[:n_o]):
            o_ref[...] = o.astype(o_ref.dtype)
        if n_a:
            first = pl.program_id(0) == 0

            @pl.when(first)
            def _():
                for a_ref, o in zip(a_refs, outs[n_o:]):
                    a_ref[...] = o.astype(a_ref.dtype)

            @pl.when(jnp.logical_not(first))
            def _():
                for a_ref, o in zip(a_refs, outs[n_o:]):
                    a_ref[...] += o.astype(a_ref.dtype)
        if n_g:

            @pl.when(pl.program_id(0) == nt - 1)
            def _():
                comm_finish(*comm)

    in_specs = []
    args = []
    for arr, win in rows:
        if win is None:
            in_specs.append(pl.BlockSpec((tile, arr.shape[1]), lambda i: (i, 0)))
        else:
            w, cb = win
            in_specs.append(pl.BlockSpec((tile, w), functools.partial(lambda i, cb: (i, cb), cb=cb)))
        args.append(arr)
    for c in consts:
        index_map = functools.partial(lambda i, nd: (0,) * nd, nd=c.ndim)
        if c.size * c.dtype.itemsize >= BIG_CONST_BYTES:
            in_specs.append(pl.BlockSpec(c.shape, index_map, pipeline_mode=pl.Buffered(1)))
        else:
            in_specs.append(pl.BlockSpec(c.shape, index_map))
        args.append(c)
    out_specs = [pl.BlockSpec((tile, w), lambda i: (i, 0)) for w, _ in out_rows]
    out_shape = [_sds((T, w), dt) for w, dt in out_rows]
    for shp, dt in out_accs:
        out_specs.append(pl.BlockSpec(shp, functools.partial(lambda i, nd: (0,) * nd, nd=len(shp))))
        out_shape.append(_sds(shp, dt))
    return pl.pallas_call(
        body,
        name=name,
        grid=(nt,),
        in_specs=in_specs + [ANY] * n_g,
        out_specs=out_specs + [ANY] * n_g,
        out_shape=out_shape + [_sds((N_DEV,) + p.shape, p.dtype) for p in gather]
        + [_sds((N_DEV - 1,) + s.shape[1:], s.dtype) for s in exchange],
        scratch_shapes=(_gather_scratch(n_g) if gather else _exchange_scratch(n_g)) if n_g else [],
        compiler_params=_cp(1),
    )(*args, *travel)


def _rms(x, g):
    r = lax.rsqrt(jnp.mean(x * x, axis=-1, keepdims=True) + EPS)
    return x * r * g


def _rms_bwd(x, g, dy):
    r = lax.rsqrt(jnp.mean(x * x, axis=-1, keepdims=True) + EPS)
    xh = x * r
    dg = jnp.sum(dy * xh, axis=0, keepdims=True)
    dxh = dy * g
    dx = r * (dxh - xh * jnp.mean(dxh * xh, axis=-1, keepdims=True))
    return dx, dg


def _sigmoid(x):
    return 0.5 * jnp.tanh(0.5 * x) + 0.5


def _rot_mla(x, c, s1, s2):
    n = x.shape[1] // HP
    outs = []
    for h in range(n):
        xh = x[:, h * HP : (h + 1) * HP]
        outs.append(xh * c + pltpu.roll(xh, HP - 16, 1) * s1 + pltpu.roll(xh, 16, 1) * s2)
    return outs[0] if n == 1 else jnp.concatenate(outs, axis=1)


def _rot_ret(x, c, s):
    n = x.shape[1] // RET_D
    outs = []
    for h in range(n):
        xh = x[:, h * RET_D : (h + 1) * RET_D]
        outs.append(xh * c + pltpu.roll(xh, RET_D // 2, 1) * s)
    return outs[0] if n == 1 else jnp.concatenate(outs, axis=1)


def _rot_inv():
    lane_np = np.arange(128)
    inv_m = (jnp.asarray(THETA, F32) ** (-jnp.asarray(lane_np & 15, F32) / 16.0)).reshape(1, 128)
    inv_r = (jnp.asarray(THETA, F32) ** (-jnp.asarray(lane_np & 63, F32) / 64.0)).reshape(1, 128)
    return inv_m, inv_r


def _rot_tables(p, im, ir):
    lane = lax.broadcasted_iota(jnp.int32, p.shape, 1)
    ang = p * im
    cm = jnp.where((lane >= 64) & (lane < 96), jnp.cos(ang), 1.0)
    sn = jnp.sin(ang)
    s1 = jnp.where((lane >= 64) & (lane < 80), -sn, 0.0)
    s2 = jnp.where((lane >= 80) & (lane < 96), sn, 0.0)
    angr = p * ir
    snr = jnp.sin(angr)
    return cm, s1, s2, jnp.cos(angr), jnp.where(lane < 64, -snr, snr)


def _peer(m):
    x, y, c = lax.axis_index("x"), lax.axis_index("y"), lax.axis_index("c")
    mx, my, mc = (m >> 2) & 1, (m >> 1) & 1, m & 1
    px = 1 - x if mx else x
    py = 1 - y if my else y
    pc = 1 - c if mc else c
    return (px, py, pc), 4 * px + 2 * py + pc


def _my_id():
    return 4 * lax.axis_index("x") + 2 * lax.axis_index("y") + lax.axis_index("c")


def _gather_copies(srcs, outs, send_sems, recv_sems, local_sems, arriving=False):
    me = _my_id()
    copies = []
    if not arriving:
        for g, (src, out) in enumerate(zip(srcs, outs)):
            copies.append((pltpu.make_async_copy(src, out.at[me], local_sems.at[g]), False))
    for m in range(1, N_DEV):
        peer, plin = _peer(m)
        for g, (src, out) in enumerate(zip(srcs, outs)):
            copies.append((pltpu.make_async_remote_copy(
                src_ref=src, dst_ref=out.at[plin if arriving else me], send_sem=send_sems.at[g, m - 1],
                recv_sem=recv_sems.at[g, m - 1], device_id=peer, device_id_type=MESH), True))
    return copies


def _gather_start(*a):
    for cp, _ in _gather_copies(*a):
        cp.start()


def _gather_wait(*a):
    for cp, _ in _gather_copies(*a, arriving=True):
        cp.wait_recv()
    for cp, remote in _gather_copies(*a):
        if remote:
            cp.wait_send()
        else:
            cp.wait()


CHIP_RELATIONS = (4, 2, 6)


def _gather2_copy(src, out, block, to, send_sems, recv_sems, g, k):
    return pltpu.make_async_remote_copy(src_ref=src, dst_ref=out.at[block], send_sem=send_sems.at[g, k],
                                        recv_sem=recv_sems.at[g, k], device_id=to, device_id_type=MESH)


def _gather2_start(srcs, outs, send_sems, recv_sems, local_sems):
    me = _my_id()
    sib, _ = _peer(1)
    for g, (src, out) in enumerate(zip(srcs, outs)):
        pltpu.make_async_copy(src, out.at[me], local_sems.at[g]).start()
        _gather2_copy(src, out, me, sib, send_sems, recv_sems, g, 0).start()
        for t, m in enumerate(CHIP_RELATIONS):
            _gather2_copy(src, out, me, _peer(m)[0], send_sems, recv_sems, g, 1 + t).start()


def _gather2_finish(srcs, outs, send_sems, recv_sems, local_sems):
    me = _my_id()
    sib, sib_lin = _peer(1)
    for t, m in enumerate(CHIP_RELATIONS):
        peer, plin = _peer(m)
        for g, (src, out) in enumerate(zip(srcs, outs)):
            _gather2_copy(src, out, plin, peer, send_sems, recv_sems, g, 1 + t).wait_recv()
            _gather2_copy(out.at[plin], out, plin, sib, send_sems, recv_sems, g, 4 + t).start()
    for g, (src, out) in enumerate(zip(srcs, outs)):
        _gather2_copy(src, out, sib_lin, sib, send_sems, recv_sems, g, 0).wait_recv()
        for t, m in enumerate(CHIP_RELATIONS):
            _gather2_copy(src, out, _peer(m | 1)[1], sib, send_sems, recv_sems, g, 4 + t).wait_recv()
    for g, (src, out) in enumerate(zip(srcs, outs)):
        _gather2_copy(src, out, me, sib, send_sems, recv_sems, g, 0).wait_send()
        for t, m in enumerate(CHIP_RELATIONS):
            peer, plin = _peer(m)
            _gather2_copy(src, out, me, peer, send_sems, recv_sems, g, 1 + t).wait_send()
            _gather2_copy(out.at[plin], out, plin, sib, send_sems, recv_sems, g, 4 + t).wait_send()
        pltpu.make_async_copy(src, out.at[me], local_sems.at[g]).wait()


def _gather_scratch(n):
    return [pltpu.SemaphoreType.DMA((n, N_DEV - 1)), pltpu.SemaphoreType.DMA((n, N_DEV - 1)), pltpu.SemaphoreType.DMA((n,))]


def _exchange_copies(srcs, dsts, send_sems, recv_sems):
    copies = []
    for m in range(1, N_DEV):
        peer, plin = _peer(m)
        for g, ((src, per_peer), dst) in enumerate(zip(srcs, dsts)):
            copies.append(pltpu.make_async_remote_copy(
                src_ref=src.at[plin] if per_peer else src, dst_ref=dst.at[m - 1], send_sem=send_sems.at[g, m - 1],
                recv_sem=recv_sems.at[g, m - 1], device_id=peer, device_id_type=MESH))
    return copies


def _exchange_start(*a):
    for cp in _exchange_copies(*a):
        cp.start()


def _exchange_wait(*a):
    copies = _exchange_copies(*a)
    for cp in copies:
        cp.wait_recv()
    for cp in copies:
        cp.wait_send()


def _exchange_scratch(n):
    return [pltpu.SemaphoreType.DMA((n, N_DEV - 1)), pltpu.SemaphoreType.DMA((n, N_DEV - 1))]


MLA_SCALE = (MLA_NOPE + MLA_ROPE) ** -0.5
MLA_C2 = MLA_SCALE * math.log2(math.e)
ONE_LANE = MLA_V
NEG = -1e30
HPS = 4
HPB = 4


def _tri_mask(n, lower_rows_ge_cols=True):
    r = lax.broadcasted_iota(jnp.int32, (n, n), 0)
    c = lax.broadcasted_iota(jnp.int32, (n, n), 1)
    return r >= c if lower_rows_ge_cols else c >= r


def mla_fwd(q, k, v, gather, *, tq):
    T = q.shape[0]
    nq = T // tq
    rep = tq // HP
    ng = len(gather)

    def body(*refs):
        q_ref, k_ref, v_ref = refs[:3]
        srcs = refs[3:3 + ng]
        o_ref, lse_ref = refs[3 + ng:5 + ng]
        outs = refs[5 + ng:5 + 2 * ng]
        m_sc, acc_sc = refs[5 + 2 * ng:7 + 2 * ng]
        comm = (srcs, outs) + tuple(refs[7 + 2 * ng:])
        h, i = pl.program_id(0), pl.program_id(1)

        @pl.when((h == 0) & (i == 0))
        def _():
            _gather_start(*comm)

        m_sc[...] = jnp.full(m_sc.shape, NEG, F32)
        acc_sc[...] = jnp.zeros(acc_sc.shape, F32)
        heads = [slice(t * HP, (t + 1) * HP) for t in range(HPS)]

        def block(j, masked):
            off = pl.multiple_of(j * tq, tq)
            ss = [_dot_nt(q_ref[:, hd], k_ref[pl.ds(off, tq), hd]) * MLA_C2 for hd in heads]
            for hd, s in zip(heads, ss):
                if masked:
                    s = jnp.where(_tri_mask(tq), s, NEG)
                m_prev = m_sc[:, hd]
                m_next = jnp.maximum(m_prev, jnp.max(s, axis=-1, keepdims=True))
                p = jnp.exp2(s - jnp.tile(m_next, (1, rep)))
                alpha = jnp.exp2(m_prev - m_next)
                acc_sc[:, hd] = alpha * acc_sc[:, hd] + _dot(p.astype(BF16), v_ref[pl.ds(off, tq), hd])
                m_sc[:, hd] = m_next

        def loop_body(j, carry):
            block(j, False)
            return carry

        lax.fori_loop(0, i, loop_body, 0)
        block(i, True)
        for hd in heads:
            acc = acc_sc[:, hd]
            l = acc[:, ONE_LANE:ONE_LANE + 1]
            o_ref[:, hd] = (acc / l).astype(o_ref.dtype)
            lse_ref[:, hd] = m_sc[:, hd] + jnp.log(l) * math.log2(math.e)

        @pl.when((h == MLA_H // HPS - 1) & (i == nq - 1))
        def _():
            _gather_wait(*comm)

    blk = pl.BlockSpec((tq, HPS * HP), lambda h, i: (i, h))
    full = pl.BlockSpec((T, HPS * HP), lambda h, i: (0, h))
    return pl.pallas_call(
        body,
        name="mla_fwd",
        grid=(MLA_H // HPS, nq),
        in_specs=[blk, full, full] + [ANY] * ng,
        out_specs=[blk, blk] + [ANY] * ng,
        out_shape=[_sds((T, MLA_H * HP), BF16), _sds((T, MLA_H * HP), F32)]
        + [_sds((N_DEV,) + p.shape, p.dtype) for p in gather],
        scratch_shapes=[pltpu.VMEM((tq, HPS * HP), F32), pltpu.VMEM((tq, HPS * HP), F32)] + _gather_scratch(ng),
        compiler_params=_cp(2),
    )(q, k, v, *gather)


def mla_prep(o, do, lse, *, tq):
    T = o.shape[0]
    nq = T // tq

    def body(o_ref, do_ref, lse_ref, lse_row_ref, delta_row_ref):
        for h in range(MLA_H):
            sl = slice(h * HP, (h + 1) * HP)
            d = jnp.sum(o_ref[:, sl].astype(F32) * do_ref[:, sl].astype(F32), axis=-1, keepdims=True)
            lse_row_ref[h] = lse_ref[:, sl].T[0:1, :]
            delta_row_ref[h] = jnp.broadcast_to(d, (tq, HP)).T[0:1, :]

    blk = pl.BlockSpec((tq, MLA_H * HP), lambda i: (i, 0))
    row = pl.BlockSpec((MLA_H, None, 1, tq), lambda i: (0, i, 0, 0))
    return pl.pallas_call(
        body,
        name="mla_prep",
        grid=(nq,),
        in_specs=[blk, blk, blk],
        out_specs=[row, row],
        out_shape=[_sds((MLA_H, nq, 1, tq), F32), _sds((MLA_H, nq, 1, tq), F32)],
        compiler_params=_cp(1),
    )(o, do, lse)


def mla_bwd(q, k, v, do, lse_row, delta_row, slots, *, tq):
    T = q.shape[0]
    nq = T // tq
    ns = len(slots)

    def body(*refs):
        q_ref, k_ref, v_ref, do_ref, lse_ref, delta_ref = refs[:6]
        srcs = [(r, True) for r in refs[6:6 + ns]]
        dq_ref, dk_ref, dv_ref = refs[6 + ns:9 + ns]
        dsts = refs[9 + ns:9 + 2 * ns]
        dk_sc, dv_sc = refs[9 + 2 * ns:11 + 2 * ns]
        comm = (srcs, dsts) + tuple(refs[11 + 2 * ns:])
        h, j = pl.program_id(0), pl.program_id(1)

        @pl.when((h == 0) & (j == 0))
        def _():
            _exchange_start(*comm)

        dk_sc[...] = jnp.zeros(dk_sc.shape, F32)
        dv_sc[...] = jnp.zeros(dv_sc.shape, F32)
        heads = [slice(t * HP, (t + 1) * HP) for t in range(HPB)]

        @pl.when(j == 0)
        def _():
            dq_ref[...] = jnp.zeros(dq_ref.shape, F32)

        def block(i, masked):
            off = pl.multiple_of(i * tq, tq)
            sts = [_dot_nt(k_ref[:, hd], q_ref[pl.ds(off, tq), hd]) * MLA_C2 for hd in heads]
            dpts = [_dot_nt(v_ref[:, hd], do_ref[pl.ds(off, tq), hd]) for hd in heads]
            for t, hd in enumerate(heads):
                st = sts[t]
                if masked:
                    st = jnp.where(_tri_mask(tq, False), st, NEG)
                pt = jnp.exp2(st - lse_ref[t, i])
                dv_sc[:, hd] += _dot(pt.astype(BF16), do_ref[pl.ds(off, tq), hd])
                dst = (pt * (dpts[t] - delta_ref[t, i]) * MLA_SCALE).astype(BF16)
                dk_sc[:, hd] += _dot(dst, q_ref[pl.ds(off, tq), hd])
                dq_ref[pl.ds(off, tq), hd] += _dot_tn(dst, k_ref[:, hd])

        block(j, True)

        def loop_body(i, carry):
            block(i, False)
            return carry

        lax.fori_loop(j + 1, nq, loop_body, 0)
        dk_ref[...] = dk_sc[...].astype(dk_ref.dtype)
        dv_ref[...] = dv_sc[...].astype(dv_ref.dtype)

        @pl.when((h == MLA_H // HPB - 1) & (j == nq - 1))
        def _():
            _exchange_wait(*comm)

    blk = pl.BlockSpec((tq, HPB * HP), lambda h, j: (j, h))
    full = pl.BlockSpec((T, HPB * HP), lambda h, j: (0, h), pipeline_mode=pl.Buffered(1))
    rows = pl.BlockSpec((HPB, nq, 1, tq), lambda h, j: (h, 0, 0, 0))
    return pl.pallas_call(
        body,
        name="mla_bwd",
        grid=(MLA_H // HPB, nq),
        in_specs=[full, blk, blk, full, rows, rows] + [ANY] * ns,
        out_specs=[full, blk, blk] + [ANY] * ns,
        out_shape=[_sds((T, MLA_H * HP), F32), _sds((T, MLA_H * HP), BF16), _sds((T, MLA_H * HP), BF16)]
        + [_sds((N_DEV - 1,) + s.shape[1:], s.dtype) for s in slots],
        scratch_shapes=[pltpu.VMEM((tq, HPB * HP), F32), pltpu.VMEM((tq, HPB * HP), F32)] + _exchange_scratch(ns),
        compiler_params=_cp(2),
    )(q, k, v, do, lse_row, delta_row, *slots)


def _ret_consts():
    h = jnp.arange(RET_H, dtype=F32)
    log_g = jnp.log1p(-jnp.exp2(-5.0 - h))
    idx = jnp.arange(RET_C, dtype=F32)
    rel = idx[:, None] - idx[None, :]
    dmask = jnp.where(rel >= 0, jnp.exp(log_g[:, None, None] * jnp.maximum(rel, 0.0)), 0.0)
    zeta = jnp.exp(log_g[:, None] * (RET_C - 1.0 - idx)[None, :])
    xi = jnp.exp(log_g[:, None] * (idx + 1.0)[None, :])
    decay = jnp.exp(log_g * RET_C)
    zb = jnp.broadcast_to(zeta[:, :, None], (RET_H, RET_C, RET_D))
    xb = jnp.broadcast_to(xi[:, :, None], (RET_H, RET_C, RET_D))
    db = jnp.broadcast_to(decay[:, None, None], (RET_H, RET_C, RET_D))
    return dmask.astype(F32), zb.astype(F32), xb.astype(F32), db.astype(F32)


def ret_fwd(rq, rk, rv, consts, *, rb):
    T = rq.shape[0]
    nb = T // rb
    ncb = rb // RET_C

    def body(q_ref, k_ref, v_ref, dm_ref, z_ref, x_ref, dc_ref, o_ref, st_ref, r_sc):
        @pl.when(pl.program_id(0) == 0)
        def _():
            r_sc[...] = jnp.zeros(r_sc.shape, F32)

        for c in range(ncb):
            sl = slice(c * RET_C, (c + 1) * RET_C)
            for h in range(RET_H):
                hd = slice(h * RET_D, (h + 1) * RET_D)
                q, k, v = q_ref[sl, hd], k_ref[sl, hd], v_ref[sl, hd]
                r = r_sc[h]
                rbf = r.astype(BF16)
                st_ref[sl, hd] = rbf
                s = _dot_nt(q, k) * dm_ref[h]
                inner = _dot(s.astype(BF16), v)
                cross = _dot((q.astype(F32) * x_ref[h]).astype(BF16), rbf)
                o_ref[sl, hd] = inner + cross
                kz = (k.astype(F32) * z_ref[h]).T.astype(BF16)
                r_sc[h] = r * dc_ref[h] + _dot(kz, v)

    blk = pl.BlockSpec((rb, RET_H * RET_D), lambda b: (b, 0))
    cst = pl.BlockSpec((RET_H, RET_C, RET_D), lambda b: (0, 0, 0))
    return pl.pallas_call(
        body,
        name="ret_fwd",
        grid=(nb,),
        in_specs=[blk, blk, blk, cst, cst, cst, cst],
        out_specs=[blk, blk],
        out_shape=[_sds((T, RET_H * RET_D), F32), _sds((T, RET_H * RET_D), BF16)],
        scratch_shapes=[pltpu.VMEM((RET_H, RET_D, RET_D), F32)],
        compiler_params=_cp(1),
    )(rq, rk, rv, *consts)


def ret_bwd(rq, rk, rv, st, dret, consts, *, rb):
    T = rq.shape[0]
    nb = T // rb
    ncb = rb // RET_C

    def body(q_ref, k_ref, v_ref, st_ref, do_ref, dm_ref, z_ref, x_ref, dc_ref, dq_ref, dk_ref, dv_ref, g_sc):
        @pl.when(pl.program_id(0) == 0)
        def _():
            g_sc[...] = jnp.zeros(g_sc.shape, F32)

        for c in reversed(range(ncb)):
            sl = slice(c * RET_C, (c + 1) * RET_C)
            for h in range(RET_H):
                hd = slice(h * RET_D, (h + 1) * RET_D)
                dm, zt, xi = dm_ref[h], z_ref[h], x_ref[h]
                q, k, v, rp = q_ref[sl, hd], k_ref[sl, hd], v_ref[sl, hd], st_ref[sl, hd]
                dob = do_ref[sl, hd].astype(BF16)
                qf, kf = q.astype(F32), k.astype(F32)
                gn = g_sc[h]
                gnb = gn.astype(BF16)
                s = _dot_nt(q, k) * dm
                ds = _dot_nt(dob, v) * dm
                dq = _dot(ds.astype(BF16), k) + _dot_nt(dob, rp) * xi
                dk = _dot(ds.T.astype(BF16), q) + _dot_nt(v, gnb) * zt
                dv = _dot(s.T.astype(BF16), dob) + _dot((kf * zt).astype(BF16), gnb)
                dq_ref[sl, hd] = dq.astype(dq_ref.dtype)
                dk_ref[sl, hd] = dk.astype(dk_ref.dtype)
                dv_ref[sl, hd] = dv.astype(dv_ref.dtype)
                g_sc[h] = _dot((qf * xi).T.astype(BF16), dob) + dc_ref[h] * gn

    blk = pl.BlockSpec((rb, RET_H * RET_D), lambda b: (nb - 1 - b, 0))
    cst = pl.BlockSpec((RET_H, RET_C, RET_D), lambda b: (0, 0, 0))
    return pl.pallas_call(
        body,
        name="ret_bwd",
        grid=(nb,),
        in_specs=[blk, blk, blk, blk, blk, cst, cst, cst, cst],
        out_specs=[blk, blk, blk],
        out_shape=[_sds((T, RET_H * RET_D), F32)] * 3,
        scratch_shapes=[pltpu.VMEM((RET_H, RET_D, RET_D), F32)],
        compiler_params=_cp(1),
    )(rq, rk, rv, st, dret, *consts)


HALO = 16


def conv_act_fwd(up_pre, w_conv, b_conv, *, tile, cw):
    T = up_pre.shape[0]
    nt = T // tile
    ncol = D_FF // cw
    hb = tile // HALO

    def body(pa_ref, a_ref, pb_ref, b_ref, wa_ref, wb_ref, ba_ref, bb_ref, o_ref):
        i = pl.program_id(1)
        keep = (i > 0).astype(F32)

        def conv(prev_ref, cur_ref, w_ref, bias_ref):
            ext = jnp.concatenate([prev_ref[...].astype(F32) * keep, cur_ref[...].astype(F32)], axis=0)
            w = w_ref[...]
            y = ext * w[2:3, :] + pltpu.roll(ext, 1, 0) * w[1:2, :] + pltpu.roll(ext, 2, 0) * w[0:1, :] + bias_ref[...]
            return y[HALO:, :]

        a = conv(pa_ref, a_ref, wa_ref, ba_ref)
        b = conv(pb_ref, b_ref, wb_ref, bb_ref)
        o_ref[...] = (a * _sigmoid(a) * b).astype(o_ref.dtype)

    prev_a = pl.BlockSpec((HALO, cw), lambda j, i: (jnp.maximum(i * hb - 1, 0), j))
    cur_a = pl.BlockSpec((tile, cw), lambda j, i: (i, j))
    prev_b = pl.BlockSpec((HALO, cw), lambda j, i: (jnp.maximum(i * hb - 1, 0), j + ncol))
    cur_b = pl.BlockSpec((tile, cw), lambda j, i: (i, j + ncol))
    w_a = pl.BlockSpec((3, cw), lambda j, i: (0, j))
    w_b = pl.BlockSpec((3, cw), lambda j, i: (0, j + ncol))
    bias_a = pl.BlockSpec((1, cw), lambda j, i: (0, j))
    bias_b = pl.BlockSpec((1, cw), lambda j, i: (0, j + ncol))
    return pl.pallas_call(
        body,
        name="conv_act_fwd",
        grid=(ncol, nt),
        in_specs=[prev_a, cur_a, prev_b, cur_b, w_a, w_b, bias_a, bias_b],
        out_specs=pl.BlockSpec((tile, cw), lambda j, i: (i, j)),
        out_shape=_sds((T, D_FF), BF16),
        compiler_params=_cp(2),
    )(up_pre, up_pre, up_pre, up_pre, w_conv, w_conv, b_conv, b_conv)


def conv_act_bwd(up_pre, dact, w_conv, b_conv, *, tile, cw):
    T = up_pre.shape[0]
    nt = T // tile
    ncol = D_FF // cw
    hb = tile // HALO
    ext_rows = tile + 2 * HALO

    def body(pa_ref, a_ref, na_ref, pb_ref, b_ref, nb_ref, d_ref, nd_ref, wa_ref, wb_ref, ba_ref, bb_ref,
             dxa_ref, dxb_ref, sa_ref, sb_ref):
        i = pl.program_id(1)
        keep_p = (i > 0).astype(F32)
        keep_n = (i < nt - 1).astype(F32)

        def ext_of(prev_ref, cur_ref, next_ref):
            return jnp.concatenate(
                [prev_ref[...].astype(F32) * keep_p, cur_ref[...].astype(F32), next_ref[...].astype(F32) * keep_n], axis=0)

        def taps(ext):
            return ext, pltpu.roll(ext, 1, 0), pltpu.roll(ext, 2, 0)

        def conv(tp, w, bias):
            return tp[0] * w[2:3, :] + tp[1] * w[1:2, :] + tp[2] * w[0:1, :] + bias

        xa = taps(ext_of(pa_ref, a_ref, na_ref))
        xb = taps(ext_of(pb_ref, b_ref, nb_ref))
        wa, wb = wa_ref[...], wb_ref[...]
        a = conv(xa, wa, ba_ref[...])
        b = conv(xb, wb, bb_ref[...])
        dy = jnp.concatenate(
            [jnp.zeros((HALO, cw), F32), d_ref[...].astype(F32), nd_ref[...].astype(F32) * keep_n], axis=0)
        sg = _sigmoid(a)
        da = dy * b * (sg * (1.0 + a * (1.0 - sg)))
        db = dy * (a * sg)

        def back(dup, tp, w, dx_ref, s_ref):
            dx = dup * w[2:3, :] + pltpu.roll(dup, ext_rows - 1, 0) * w[1:2, :] + pltpu.roll(dup, ext_rows - 2, 0) * w[0:1, :]
            dx_ref[...] = dx[HALO:HALO + tile, :].astype(dx_ref.dtype)
            dc = dup[HALO:HALO + tile, :]
            r2 = jnp.sum(dc * tp[0][HALO:HALO + tile, :], axis=0, keepdims=True)
            r1 = jnp.sum(dc * tp[1][HALO:HALO + tile, :], axis=0, keepdims=True)
            r0 = jnp.sum(dc * tp[2][HALO:HALO + tile, :], axis=0, keepdims=True)
            rb = jnp.sum(dc, axis=0, keepdims=True)
            row = lax.broadcasted_iota(jnp.int32, (8, cw), 0)
            upd = (jnp.where(row == 0, r0, 0.0) + jnp.where(row == 1, r1, 0.0) + jnp.where(row == 2, r2, 0.0)
                   + jnp.where(row == 3, rb, 0.0))

            @pl.when(i == 0)
            def _():
                s_ref[...] = upd

            @pl.when(i > 0)
            def _():
                s_ref[...] += upd

        back(da, xa, wa, dxa_ref, sa_ref)
        back(db, xb, wb, dxb_ref, sb_ref)

    def prev_of(shift):
        return pl.BlockSpec((HALO, cw), lambda j, i: (jnp.maximum(i * hb - 1, 0), j + shift))

    def next_of(shift):
        return pl.BlockSpec((HALO, cw), lambda j, i: (jnp.minimum((i + 1) * hb, nt * hb - 1), j + shift))

    def cur_of(shift):
        return pl.BlockSpec((tile, cw), lambda j, i: (i, j + shift))

    def row_of(rows, shift):
        return pl.BlockSpec((rows, cw), lambda j, i: (0, j + shift))

    return pl.pallas_call(
        body,
        name="conv_act_bwd",
        grid=(ncol, nt),
        in_specs=[prev_of(0), cur_of(0), next_of(0), prev_of(ncol), cur_of(ncol), next_of(ncol), cur_of(0), next_of(0),
                  row_of(3, 0), row_of(3, ncol), row_of(1, 0), row_of(1, ncol)],
        out_specs=[cur_of(0), cur_of(0), row_of(8, 0), row_of(8, 0)],
        out_shape=[_sds((T, D_FF), BF16), _sds((T, D_FF), BF16), _sds((8, D_FF), F32), _sds((8, D_FF), F32)],
        compiler_params=_cp(2),
    )(up_pre, up_pre, up_pre, up_pre, up_pre, up_pre, dact, dact, w_conv, w_conv, b_conv, b_conv)


NS_IN, NS_UP, NS_XKV, NS_UQ = 564, 704, 256, 96
E2_ROWS, E2_UQ = 384, 128
L1_W = NS_UP + NS_XKV
L2_ROWS = 736
L2_OUT, L2_XQ, L2_XO, L2_DN = 0, 128, 256, 384
RS_DN = 352
L3_ROWS, L3_PRET = 1024, 512
L4_SHAPE = (8, 768)
WZ_RUNS = ((416, 4096, 0), (0, 384, 4096), (384, 32, 4544))
WZ_ZERO = ((4480, 4544), (4576, 4608))


def _pieces(orig_start, length, dst_start, ns):
    out, c, d, end = [], orig_start, dst_start, orig_start + length
    while c < end:
        j, off = c // ns, c % ns
        ln = min(ns - off, end - c)
        out.append((j, off, d, ln))
        c += ln
        d += ln
    return out


def pack_early(w_in, w_ukv, w_uq):
    def body(in_ref, ukv_ref, uq_ref, e1_ref, e2_ref):
        e1_ref[...] = in_ref[...].astype(BF16)
        e2_ref[0:E2_UQ, :] = ukv_ref[...].astype(BF16)
        e2_ref[E2_UQ:E2_ROWS, 0:NS_UQ] = uq_ref[...].astype(BF16)
        e2_ref[E2_UQ:E2_ROWS, NS_UQ:HP] = jnp.zeros((E2_ROWS - E2_UQ, HP - NS_UQ), BF16)

    return pl.pallas_call(
        body,
        name="pack_early",
        in_specs=[VM] * 3,
        out_specs=[VM] * 2,
        out_shape=[_sds((D, NS_IN), BF16), _sds((E2_ROWS, HP), BF16)],
        compiler_params=_cp0(),
    )(w_in, w_ukv, w_uq)


def pack_late(w_up, w_xkv, w_out, w_xq, w_xo, w_down, w_pmla, w_pret, w_conv):
    def body(up_ref, xkv_ref, o_ref, xq_ref, xo_ref, dn_ref, pm_ref, pr_ref, cv_ref, l1_ref, l2_ref, l3_ref, l4_ref):
        l1_ref[:, 0:NS_UP] = up_ref[...].astype(BF16)
        l1_ref[:, NS_UP:L1_W] = xkv_ref[...].astype(BF16)
        l2_ref[L2_OUT:L2_XQ, :] = o_ref[...].astype(BF16)
        l2_ref[L2_XQ:L2_XO, :] = xq_ref[...].astype(BF16)
        l2_ref[L2_XO:L2_DN, :] = xo_ref[...].astype(BF16)
        l2_ref[L2_DN:L2_ROWS, :] = dn_ref[...].astype(BF16)
        l3_ref[0:L3_PRET, :] = pm_ref[...].astype(BF16)
        l3_ref[L3_PRET:L3_ROWS, :] = pr_ref[...].astype(BF16)
        l4_ref[...] = jnp.zeros(L4_SHAPE, F32)
        l4_ref[0:3, 0:NS_UP] = cv_ref[...]

    return pl.pallas_call(
        body,
        name="pack_late",
        in_specs=[VM] * 9,
        out_specs=[VM] * 4,
        out_shape=[_sds((D, L1_W), BF16), _sds((L2_ROWS, D), BF16), _sds((L3_ROWS, HP), BF16), _sds(L4_SHAPE, F32)],
        compiler_params=_cp0(),
    )(w_up, w_xkv, w_out, w_xq, w_xo, w_down, w_pmla, w_pret, w_conv)


def assemble_early(g1, g2, *, tile):
    def body(g1_ref, g2_ref, wz_ref, wk_ref, wv_ref, wq_ref):
        for lo, hi in WZ_ZERO:
            wz_ref[:, lo:hi] = jnp.zeros((tile, hi - lo), BF16)
        for os_, ln_, ds_ in WZ_RUNS:
            for j, off, d, ln in _pieces(os_, ln_, ds_, NS_IN):
                wz_ref[:, d:d + ln] = g1_ref[j, :, off:off + ln]

        @pl.when(pl.program_id(0) == 0)
        def _():
            half = jnp.zeros((MLA_KVR, HP - MLA_NOPE), BF16)
            for j in range(N_DEV):
                wk_ref[:, j * HP:j * HP + MLA_NOPE] = g2_ref[j, 0:E2_UQ, 0:MLA_NOPE]
                wk_ref[:, j * HP + MLA_NOPE:(j + 1) * HP] = half
                wv_ref[:, j * HP:j * HP + MLA_V] = g2_ref[j, 0:E2_UQ, MLA_NOPE:HP]
                wv_ref[:, j * HP + MLA_V:(j + 1) * HP] = half
                wq_ref[:, j * HP:j * HP + NS_UQ] = g2_ref[j, E2_UQ:E2_ROWS, 0:NS_UQ]
                wq_ref[:, j * HP + NS_UQ:(j + 1) * HP] = jnp.zeros((MLA_QR, HP - NS_UQ), BF16)

    def whole(r):
        return pl.BlockSpec((r, MLA_H * HP), lambda i: (0, 0))

    return pl.pallas_call(
        body,
        name="assemble_early",
        grid=(D // tile,),
        in_specs=[pl.BlockSpec((N_DEV, tile, NS_IN), lambda i: (0, i, 0)),
                  pl.BlockSpec((N_DEV, E2_ROWS, HP), lambda i: (0, 0, 0))],
        out_specs=[pl.BlockSpec((tile, ZW), lambda i: (i, 0)), whole(MLA_KVR), whole(MLA_KVR), whole(MLA_QR)],
        out_shape=[_sds((D, ZW), BF16), _sds((MLA_KVR, MLA_H * HP), BF16), _sds((MLA_KVR, MLA_H * HP), BF16),
                   _sds((MLA_QR, MLA_H * HP), BF16)],
        compiler_params=_cp(1),
    )(g1, g2)


def assemble_l1(g1, *, tile):
    def body(g_ref, wup_ref, wxkv_ref):
        for j in range(N_DEV):
            wup_ref[:, j * NS_UP:(j + 1) * NS_UP] = g_ref[j, :, 0:NS_UP]
            wxkv_ref[:, j * NS_XKV:(j + 1) * NS_XKV] = g_ref[j, :, NS_UP:L1_W]

    return pl.pallas_call(
        body,
        name="assemble_l1",
        grid=(D // tile,),
        in_specs=[pl.BlockSpec((N_DEV, tile, L1_W), lambda i: (0, i, 0))],
        out_specs=[pl.BlockSpec((tile, 2 * D_FF), lambda i: (i, 0)), pl.BlockSpec((tile, 2 * D), lambda i: (i, 0))],
        out_shape=[_sds((D, 2 * D_FF), BF16), _sds((D, 2 * D), BF16)],
        compiler_params=_cp(1),
    )(g1)


def assemble_l234(g2, g3, g4):
    def body(g2_ref, g3_ref, g4_ref, wo_ref, wxq_ref, wxo_ref, wdn_ref, wpa_ref, wpr_ref, wc_ref):
        for j in range(N_DEV):
            wo_ref[j * 128:(j + 1) * 128, :] = g2_ref[j, L2_OUT:L2_XQ, :]
            wxq_ref[j * 128:(j + 1) * 128, :] = g2_ref[j, L2_XQ:L2_XO, :]
            wxo_ref[j * 128:(j + 1) * 128, :] = g2_ref[j, L2_XO:L2_DN, :]
            wdn_ref[j * RS_DN:(j + 1) * RS_DN, :] = g2_ref[j, L2_DN:L2_ROWS, :]
            wpr_ref[:, j * 128:(j + 1) * 128] = g3_ref[j, L3_PRET:L3_ROWS, :]
            wc_ref[:, j * NS_UP:(j + 1) * NS_UP] = g4_ref[j, 0:3, 0:NS_UP]
            for h in range(MLA_H):
                wpa_ref[h * HP:h * HP + MLA_V, j * 128:(j + 1) * 128] = g3_ref[j, h * MLA_V:(h + 1) * MLA_V, :]
        for h in range(MLA_H):
            wpa_ref[h * HP + MLA_V:(h + 1) * HP, :] = jnp.zeros((HP - MLA_V, D), BF16)

    return pl.pallas_call(
        body,
        name="assemble_l234",
        in_specs=[VM] * 3,
        out_specs=[VM] * 7,
        out_shape=[_sds((D, D), BF16), _sds((D, D), BF16), _sds((D, D), BF16), _sds((D_FF, D), BF16),
                   _sds((MLA_H * HP, D), BF16), _sds((RET_H * RET_D, D), BF16), _sds((3, 2 * D_FF), F32)],
        compiler_params=_cp0(),
    )(g2, g3, g4)


def slots_l1(dwup_a, dwup_b, dwxkv, *, tile):
    per_half = D_FF // NS_UP

    def body(ua_ref, ub_ref, x_ref, s_ref):
        for j in range(N_DEV):
            src = ua_ref if j < per_half else ub_ref
            c0 = (j % per_half) * NS_UP
            s_ref[j, :, 0:NS_UP] = src[:, c0:c0 + NS_UP].astype(BF16)
            s_ref[j, :, NS_UP:L1_W] = x_ref[:, j * NS_XKV:(j + 1) * NS_XKV].astype(BF16)

    return pl.pallas_call(
        body,
        name="slots_l1",
        grid=(D // tile,),
        in_specs=[pl.BlockSpec((tile, D_FF), lambda i: (i, 0)), pl.BlockSpec((tile, D_FF), lambda i: (i, 0)),
                  pl.BlockSpec((tile, 2 * D), lambda i: (i, 0))],
        out_specs=pl.BlockSpec((N_DEV, tile, L1_W), lambda i: (0, i, 0)),
        out_shape=_sds((N_DEV, D, L1_W), BF16),
        compiler_params=_cp(1),
    )(dwup_a, dwup_b, dwxkv)


def slots_l23(dwo, dwxq, dwxo, dwdn, dwpa, dwpr):
    def body(o_ref, xq_ref, xo_ref, dn_ref, pa_ref, pr_ref, l2_ref, l3_ref):
        l2_ref[L2_OUT:L2_XQ, :] = o_ref[...].astype(BF16)
        l2_ref[L2_XQ:L2_XO, :] = xq_ref[...].astype(BF16)
        l2_ref[L2_XO:L2_DN, :] = xo_ref[...].astype(BF16)
        l2_ref[L2_DN:L2_ROWS, :] = dn_ref[...].astype(BF16)
        for h in range(MLA_H):
            l3_ref[h * MLA_V:(h + 1) * MLA_V, :] = pa_ref[h * HP:h * HP + MLA_V, :].astype(BF16)
        l3_ref[L3_PRET:L3_ROWS, :] = pr_ref[...].astype(BF16)

    rows128 = pl.BlockSpec((128, D), lambda j: (j, 0))

    def cols(r):
        return pl.BlockSpec((r, 128), lambda j: (0, j))

    return pl.pallas_call(
        body,
        name="slots_l23",
        grid=(N_DEV,),
        in_specs=[rows128, rows128, rows128, pl.BlockSpec((RS_DN, D), lambda j: (j, 0)),
                  cols(MLA_H * HP), cols(RET_H * RET_D)],
        out_specs=[pl.BlockSpec((None, L2_ROWS, D), lambda j: (j, 0, 0)),
                   pl.BlockSpec((None, L3_ROWS, HP), lambda j: (j, 0, 0))],
        out_shape=[_sds((N_DEV, L2_ROWS, D), BF16), _sds((N_DEV, L3_ROWS, HP), BF16)],
        compiler_params=_cp(1),
    )(dwo, dwxq, dwxo, dwdn, dwpa, dwpr)


def slots_early(dwz, dwk, dwv, dwq, *, tile):
    def body(dz_ref, k_ref, v_ref, q_ref, s1_ref, s2_ref):
        for os_, ln_, ds_ in WZ_RUNS:
            for j, off, d, ln in _pieces(os_, ln_, ds_, NS_IN):
                s1_ref[j, :, off:off + ln] = dz_ref[:, d:d + ln].astype(BF16)

        @pl.when(pl.program_id(0) == 0)
        def _():
            for j in range(N_DEV):
                s2_ref[j, 0:E2_UQ, 0:MLA_NOPE] = k_ref[:, j * HP:j * HP + MLA_NOPE].astype(BF16)
                s2_ref[j, 0:E2_UQ, MLA_NOPE:HP] = v_ref[:, j * HP:j * HP + MLA_V].astype(BF16)
                s2_ref[j, E2_UQ:E2_ROWS, 0:NS_UQ] = q_ref[:, j * HP:j * HP + NS_UQ].astype(BF16)
                s2_ref[j, E2_UQ:E2_ROWS, NS_UQ:HP] = jnp.zeros((E2_ROWS - E2_UQ, HP - NS_UQ), BF16)

    def whole(r):
        return pl.BlockSpec((r, MLA_H * HP), lambda i: (0, 0))

    return pl.pallas_call(
        body,
        name="slots_early",
        grid=(D // tile,),
        in_specs=[pl.BlockSpec((tile, ZW), lambda i: (i, 0)), whole(MLA_KVR), whole(MLA_KVR), whole(MLA_QR)],
        out_specs=[pl.BlockSpec((N_DEV, tile, NS_IN), lambda i: (0, i, 0)),
                   pl.BlockSpec((N_DEV, E2_ROWS, HP), lambda i: (0, 0, 0))],
        out_shape=[_sds((N_DEV, D, NS_IN), BF16), _sds((N_DEV, E2_ROWS, HP), BF16)],
        compiler_params=_cp(1),
    )(dwz, dwk, dwv, dwq)


SMALL = (("g_mix", 1024, 0), ("b_gate", 2048, 1), ("g_q_lat", 256, 3), ("g_kv_lat", 128, 4), ("g_ret", 512, 5),
         ("g_cross", 1024, 6), ("g_mem", 1024, 7), ("g_ffn", 1024, 8), ("b_conv", 5632, 9), ("g_final", 1024, 15))
SMALL_DIRECT = tuple(s for s in SMALL if s[0] != "b_conv")
S_ROWS = 16


def _flat_pieces(n, row0):
    return [(row0 + k // D, k, min(D, n - k)) for k in range(0, n, D)]


def exchange_small(cs_a, cs_b, smalls):
    ns = len(smalls)

    def body(*refs):
        ca_ref, cb_ref = refs[:2]
        small_refs = refs[2:2 + ns]
        rd_ref, rs_ref, dsl_ref, own_ref = refs[2 + ns:6 + ns]
        sems = refs[6 + ns:]
        own_ref[...] = jnp.zeros(own_ref.shape, F32)
        for (name, n, row0), g_ref in zip(SMALL_DIRECT, small_refs):
            for r, c0, ln in _flat_pieces(n, row0):
                own_ref[r:r + 1, 0:ln] = g_ref[:, c0:c0 + ln]
        row0 = dict((s[0], s[2]) for s in SMALL)["b_conv"]
        for half, c_ref in enumerate((ca_ref, cb_ref)):
            k = half * D_FF
            end = k + D_FF
            while k < end:
                r, lane = row0 + k // D, k % D
                ln = min(D - lane, end - k)
                own_ref[r:r + 1, lane:lane + ln] = c_ref[3:4, k - half * D_FF:k - half * D_FF + ln]
                k += ln
        dsl_ref[...] = jnp.zeros(dsl_ref.shape, F32)
        per_half = D_FF // NS_UP
        for j in range(N_DEV):
            c_ref = ca_ref if j < per_half else cb_ref
            c0 = (j % per_half) * NS_UP
            dsl_ref[j, 0:3, 0:NS_UP] = c_ref[0:3, c0:c0 + NS_UP]
        comm = ([(dsl_ref, True), (own_ref, False)], [rd_ref, rs_ref]) + tuple(sems)
        _exchange_start(*comm)
        _exchange_wait(*comm)

    n1 = N_DEV - 1
    return pl.pallas_call(
        body,
        name="exchange_small",
        in_specs=[VM] * (2 + ns),
        out_specs=[VM, VM, VM, VM],
        out_shape=[_sds((n1,) + L4_SHAPE, F32), _sds((n1, S_ROWS, D), F32), _sds((N_DEV,) + L4_SHAPE, F32),
                   _sds((S_ROWS, D), F32)],
        scratch_shapes=_exchange_scratch(2),
        compiler_params=_cp0(),
    )(cs_a, cs_b, *smalls)


def _adamw(w, g, m, v):
    m = ADAM_B1 * m + (1.0 - ADAM_B1) * g
    v = ADAM_B2 * v + (1.0 - ADAM_B2) * (g * g)
    m_hat = m / (1.0 - ADAM_B1 ** ADAM_STEP)
    v_hat = v / (1.0 - ADAM_B2 ** ADAM_STEP)
    delta = -ADAM_LR * (m_hat / (jnp.sqrt(v_hat) + ADAM_EPS) + ADAM_WD * w)
    return delta, m, v


def _apply(g, refs, outs):
    d, mn, vn = _adamw(refs[0][...], g, refs[1][...], refs[2][...])
    outs[0][...] = g
    outs[1][...] = d
    outs[2][...] = mn
    outs[3][...] = vn


def adam_cols(own, recv, wmv, spans, *, name, tile):
    R, W = own.shape
    nw = len(spans)

    def body(*refs):
        own_ref, recv_ref = refs[:2]
        ins, outs = refs[2:2 + 3 * nw], refs[2 + 3 * nw:]
        g = own_ref[...].astype(F32)
        for k in range(N_DEV - 1):
            g = g + recv_ref[k].astype(F32)
        for t, (lo, hi) in enumerate(spans):
            _apply(g[:, lo:hi], ins[3 * t:3 * t + 3], outs[4 * t:4 * t + 4])

    def blk(w):
        return pl.BlockSpec((tile, w), lambda i: (i, 0))

    widths = [hi - lo for lo, hi in spans]
    return pl.pallas_call(
        body,
        name=name,
        grid=(R // tile,),
        in_specs=[blk(W), pl.BlockSpec((N_DEV - 1, tile, W), lambda i: (0, i, 0))] + [blk(w) for w in widths for _ in range(3)],
        out_specs=[blk(w) for w in widths for _ in range(4)],
        out_shape=[_sds((R, w), F32) for w in widths for _ in range(4)],
        compiler_params=_cp(1),
    )(own, recv, *wmv)


def adam_rows(own, recv, wmv, spans, *, name):
    nw = len(spans)

    def body(*refs):
        own_ref, recv_ref = refs[:2]
        ins, outs = refs[2:2 + 3 * nw], refs[2 + 3 * nw:]
        for t, (lo, hi, w) in enumerate(spans):
            g = own_ref[lo:hi, :].astype(F32)
            for k in range(N_DEV - 1):
                g = g + recv_ref[k, lo:hi, :].astype(F32)
            _apply(g[:, 0:w], ins[3 * t:3 * t + 3], outs[4 * t:4 * t + 4])

    return pl.pallas_call(
        body,
        name=name,
        in_specs=[VM] * (2 + 3 * nw),
        out_specs=[VM] * (4 * nw),
        out_shape=[_sds((hi - lo, w), F32) for lo, hi, w in spans for _ in range(4)],
        compiler_params=_cp0(),
    )(own, recv, *wmv)


def adam_small(own_s, recv_s, dslots, recv_d, wmv_small, wmv_conv):
    ns = len(SMALL)

    def body(*refs):
        own_ref, rs_ref, dsl_ref, rd_ref = refs[:4]
        ins = refs[4:4 + 3 * ns + 3]
        outs = refs[4 + 3 * ns + 3:4 + 3 * ns + 3 + 4 * ns + 4]
        all_sc = refs[-1]
        me = _my_id()
        all_sc[0] = own_ref[...]
        for k in range(N_DEV - 1):
            all_sc[k + 1] = rs_ref[k]
        g = all_sc[jnp.bitwise_xor(me, 0)]
        for s in range(1, N_DEV):
            g = g + all_sc[jnp.bitwise_xor(me, s)]
        all_sc[0] = g
        for t, (name, n, row0) in enumerate(SMALL):
            pieces = [all_sc[0, r:r + 1, 0:ln] for r, _, ln in _flat_pieces(n, row0)]
            gt = pieces[0] if len(pieces) == 1 else jnp.concatenate(pieces, axis=1)
            _apply(gt, ins[3 * t:3 * t + 3], outs[4 * t:4 * t + 4])
        gc = dsl_ref[me]
        for k in range(N_DEV - 1):
            gc = gc + rd_ref[k]
        _apply(gc[0:3, 0:NS_UP], ins[3 * ns:3 * ns + 3], outs[4 * ns:4 * ns + 4])

    out_shape = [_sds((1, n), F32) for _, n, _ in SMALL for _ in range(4)] + [_sds((3, NS_UP), F32)] * 4
    return pl.pallas_call(
        body,
        name="adam_small",
        in_specs=[VM] * (4 + 3 * ns + 3),
        out_specs=[VM] * len(out_shape),
        out_shape=out_shape,
        scratch_shapes=[pltpu.VMEM((N_DEV, S_ROWS, D), F32)],
    )(own_s, recv_s, dslots, recv_d, *wmv_small, *wmv_conv)


def kernel(x, mem, positions, g_mix, w_in, b_gate, g_q_lat, w_uq, g_kv_lat, w_ukv, w_proj_mla, g_ret, w_proj_ret, w_out, g_cross, g_mem, w_xq, w_xkv, w_xo, g_ffn, w_up, w_conv, b_conv, w_down, g_final, loss_target, m_g_mix, m_w_in, m_b_gate, m_g_q_lat, m_w_uq, m_g_kv_lat, m_w_ukv, m_w_proj_mla, m_g_ret, m_w_proj_ret, m_w_out, m_g_cross, m_g_mem, m_w_xq, m_w_xkv, m_w_xo, m_g_ffn, m_w_up, m_w_conv, m_b_conv, m_w_down, m_g_final, v_g_mix, v_w_in, v_b_gate, v_g_q_lat, v_w_uq, v_g_kv_lat, v_w_ukv, v_w_proj_mla, v_g_ret, v_w_proj_ret, v_w_out, v_g_cross, v_g_mem, v_w_xq, v_w_xkv, v_w_xo, v_g_ffn, v_w_up, v_w_conv, v_b_conv, v_w_down, v_g_final):
    args = dict(locals())
    T = x.shape[1]
    M = mem.shape[1]
    tile = min(256, T)
    tile2 = min(512, T)
    tile4 = min(1024, T)
    tq = min(512, T)
    rb = min(1024, T)

    xs = x[0]
    tgt = loss_target[0]
    mems = mem[0]

    def shard(name, prefix=""):
        a = args[prefix + name]
        return a.reshape(a.shape[-2:]) if a.ndim >= 2 else a.reshape(1, -1)

    e1, e2 = pack_early(shard("w_in"), shard("w_ukv"), shard("w_uq"))
    late_parts = pack_late(shard("w_up"), shard("w_xkv"), shard("w_out"), shard("w_xq"), shard("w_xo"), shard("w_down"),
                           shard("w_proj_mla"), shard("w_proj_ret"), shard("w_conv"))

    pos_f = jnp.broadcast_to(positions[0].astype(F32)[:, None], (T, 128))
    inv_m, inv_r = _rot_inv()
    u, cm, s1, s2, cr, sr, ge1, ge2 = rowwise(
        lambda xv, p, im, ir, g: (_rms(xv, g),) + _rot_tables(p, im, ir), [(xs, None), (pos_f, None)],
        [inv_m, inv_r, g_mix], [(D, BF16)] + [(128, F32)] * 5, [], tile=tile2, name="norm_mix_tables", gather=[e1, e2])
    wz, wk, wv, wq = assemble_early(ge1, ge2, tile=tile)
    rconsts = _ret_consts()

    z = matmul(u, wz, name="mm_z", tm=2048, tn=1536)

    def mixers_in(zl, zr, cmv, s1v, s2v, crv, srv, gq, gkv, wqv, wkv_, wvv):
        cq = _rms(zl[:, 0:256], gq).astype(BF16)
        ckv = _rms(zl[:, 256:384], gkv).astype(BF16)
        qv = _rot_mla(_dot(cq, wqv), cmv, s1v, s2v)
        kr = _rot_mla(zl[:, 384:512], cmv, s1v, s2v)
        kn = _dot(ckv, wkv_)
        kv_ = jnp.concatenate([kn[:, h * HP:(h + 1) * HP] + kr for h in range(MLA_H)], axis=1)
        vv = _dot(ckv, wvv)
        lane = lax.broadcasted_iota(jnp.int32, vv.shape, 1)
        vv = jnp.where((lane & (HP - 1)) == ONE_LANE, 1.0, vv)
        rqv = _rot_ret(zr[:, 0:512], crv, srv)
        rkv = _rot_ret(zr[:, 512:1024], crv, srv) * (RET_D ** -0.5)
        return qv, kv_, vv, rqv, rkv, zr[:, 1024:1536]

    q_a, k_a, v_a, rq, rk, rv = rowwise(
        mixers_in, [(z, (512, 8)), (z, (2048, 0)), (cm, None), (s1, None), (s2, None), (cr, None), (sr, None)],
        [g_q_lat, g_kv_lat, wq, wk, wv], [(MLA_H * HP, BF16)] * 3 + [(512, BF16)] * 3, [], tile=tile2, name="mixers_in")

    o_a, lse, gl1, gl2, gl3, gl4 = mla_fwd(q_a, k_a, v_a, list(late_parts), tq=tq)
    wup, wxkv = assemble_l1(gl1, tile=tile)
    wo, wxq, wxo, wdn, wpa, wpr, wcv = assemble_l234(gl2, gl3, gl4)
    ret, rstate = ret_fwd(rq, rk, rv, rconsts, rb=rb)

    def gn_parts(r):
        outs = []
        for h in range(RET_H):
            rh = r[:, h * RET_D:(h + 1) * RET_D]
            mu = jnp.mean(rh, axis=-1, keepdims=True)
            dlt = rh - mu
            rstd = lax.rsqrt(jnp.mean(dlt * dlt, axis=-1, keepdims=True) + EPS)
            outs.append((dlt * rstd, rstd))
        return outs

    def mix_fwd(ov, rv_, rg, gt, wpav, wprv, gr, bg):
        ya = _dot(ov, wpav)
        xh = jnp.concatenate([p[0] for p in gn_parts(rv_)], axis=1)
        t = rg * _sigmoid(rg) * (xh * gr)
        yr = _dot(t.astype(BF16), wprv)
        ga_ = _sigmoid(gt[:, :D] + bg[:, :D])
        gr_ = _sigmoid(gt[:, D:] + bg[:, D:])
        return ga_ * ya + gr_ * yr

    mix = rowwise(mix_fwd, [(o_a, None), (ret, None), (z, (512, 3)), (z, (2048, 1))], [wpa, wpr, g_ret, b_gate],
                  [(D, BF16)], [], tile=tile, name="mix_fwd")[0]
    def proj_norm(av, rv_, wv_, g):
        hv = rv_ + _dot(av, wv_)
        return hv, _rms(hv, g)

    h1, n2 = rowwise(proj_norm, [(mix, None), (xs, None)], [wo, g_cross], [(D, F32), (D, BF16)], [], tile=tile4,
                     name="mm_out_norm")
    xq = matmul(n2, wxq, out_dtype=BF16, name="mm_xq")
    mn = rowwise(lambda mv_, g: _rms(mv_, g), [(mems, None)], [g_mem], [(D, BF16)], [], tile=min(tile, M), name="norm_mem")[0]
    mkv = matmul(mn, wxkv, out_dtype=BF16, name="mm_mkv")

    x_scale = X_HD ** -0.5

    def xattn_fwd(xqv, mkvv):
        outs = []
        for h in range(X_H):
            sl = slice(h * X_HD, (h + 1) * X_HD)
            s = _dot_nt(xqv[:, sl], mkvv[:, sl]) * x_scale
            s = s - jnp.max(s, axis=-1, keepdims=True)
            e = jnp.exp(s)
            p = e / jnp.sum(e, axis=-1, keepdims=True)
            outs.append(_dot(p.astype(BF16), mkvv[:, D + h * X_HD:D + (h + 1) * X_HD]))
        return jnp.concatenate(outs, axis=1)

    xo = rowwise(xattn_fwd, [(xq, None)], [mkv], [(D, BF16)], [], tile=tile, name="xattn_fwd")[0]
    h2, n3 = rowwise(proj_norm, [(xo, None), (h1, None)], [wxo, g_ffn], [(D, F32), (D, BF16)], [], tile=tile4,
                     name="mm_xo_norm")
    up_pre = matmul(n3, wup, out_dtype=BF16, name="mm_up", tm=2048, tn=1408)
    cw = D_FF // 2
    act = conv_act_fwd(up_pre, wcv, b_conv, tile=tile, cw=cw)

    def down_loss(av, hv2, tv, wv_, g):
        hv = hv2 + _dot(av, wv_)
        y = _rms(hv, g)
        err = y - tv
        part = 0.5 * jnp.sum(jnp.sum(err * err, axis=-1, keepdims=True) / D, axis=0, keepdims=True)
        dx, dg = _rms_bwd(hv, g, err / D)
        return dx, dg, jnp.broadcast_to(part, (8, 128))

    g_fin2 = g_final.reshape(1, D)
    dh3, dg_final, loss_acc = rowwise(down_loss, [(act, None), (h2, None), (tgt, None)], [wdn, g_fin2], [(D, F32)],
                                      [((1, D), F32), ((8, 128), F32)], tile=tile2, name="mm_down_loss")
    loss = lax.psum(loss_acc[0, 0], ("x", "y", "c"))

    dact = matmul(dh3, wdn, tb=True, out_dtype=BF16, name="mm_dact", tm=2048, tn=1408)
    dw_down = matmul_tn(act, dh3, name="mm_dw_down", tm=1408)
    dup_a, dup_b, cs_a, cs_b = conv_act_bwd(up_pre, dact, wcv, b_conv, tile=tile, cw=cw)
    dw_up_a = matmul_tn(n3, dup_a, name="mm_dw_up_a", tn=1408)
    dw_up_b = matmul_tn(n3, dup_b, name="mm_dw_up_b", tn=1408)

    def ffn_in_bwd(da_, db_, hv, drv, wv_, g):
        dn = _dot_nt(da_, wv_[:, :D_FF]) + _dot_nt(db_, wv_[:, D_FF:])
        dx, dg = _rms_bwd(hv, g, dn)
        return dx + drv, dg

    dh2, dg_ffn = rowwise(ffn_in_bwd, [(dup_a, None), (dup_b, None), (h2, None), (dh3, None)], [wup, g_ffn],
                          [(D, F32)], [((1, D), F32)], tile=tile2, name="mm_dn3_norm_bwd")
    dxo = matmul(dh2, wxo, tb=True, out_dtype=BF16, name="mm_dxo")
    dw_xo = matmul_tn(xo, dh2, name="mm_dw_xo")

    def xattn_bwd(xqv, dxov, mkvv):
        dxq, dmk, dmv = [], [], []
        for h in range(X_H):
            sl = slice(h * X_HD, (h + 1) * X_HD)
            slv = slice(D + h * X_HD, D + (h + 1) * X_HD)
            s = _dot_nt(xqv[:, sl], mkvv[:, sl]) * x_scale
            s = s - jnp.max(s, axis=-1, keepdims=True)
            e = jnp.exp(s)
            p = e / jnp.sum(e, axis=-1, keepdims=True)
            dp = _dot_nt(dxov[:, sl], mkvv[:, slv])
            ds = (p * (dp - jnp.sum(dp * p, axis=-1, keepdims=True)) * x_scale).astype(BF16)
            dxq.append(_dot(ds, mkvv[:, sl]))
            dmk.append(_dot_tn(ds, xqv[:, sl]))
            dmv.append(_dot_tn(p.astype(BF16), dxov[:, sl]))
        return jnp.concatenate(dxq, axis=1), jnp.concatenate(dmk + dmv, axis=1)

    dxq, dmkv = rowwise(xattn_bwd, [(xq, None), (dxo, None)], [mkv], [(D, BF16)], [((M, 2 * D), F32)],
                        tile=tile, name="xattn_bwd")
    dw_xq = matmul_tn(n2, dxq, name="mm_dw_xq")

    def proj_norm_bwd(dyv, hv, drv, wv_, g):
        dx, dg = _rms_bwd(hv, g, _dot_nt(dyv, wv_))
        return dx + drv, dg

    dh1, dg_cross = rowwise(proj_norm_bwd, [(dxq, None), (h1, None), (dh2, None)], [wxq, g_cross], [(D, F32)],
                            [((1, D), F32)], tile=tile4, name="mm_dn2_norm_bwd")
    dw_xkv = matmul_tn(mn, dmkv, name="mm_dw_xkv", tk=M)
    dmn = matmul(dmkv, wxkv, tb=True, name="mm_dmn", tm=M)
    dg_mem = rowwise(lambda mv_, dyv, g: _rms_bwd(mv_, g, dyv)[1], [(mems, None), (dmn, None)], [g_mem], [],
                     [((1, D), F32)], tile=min(tile, M), name="norm_mem_bwd")[0]

    dmix = matmul(dh1, wo, tb=True, out_dtype=BF16, name="mm_dmix")
    dw_out = matmul_tn(mix, dh1, name="mm_dw_out")

    def mix_bwd(ov, rv_, rg, gt, dmv_, wpav, wprv, gr, bg):
        dm_ = dmv_.astype(F32)
        ya = _dot(ov, wpav)
        parts = gn_parts(rv_)
        xh = jnp.concatenate([p[0] for p in parts], axis=1)
        yn = xh * gr
        sg = _sigmoid(rg)
        sl_ = rg * sg
        t = (sl_ * yn).astype(BF16)
        yr = _dot(t, wprv)
        ga_ = _sigmoid(gt[:, :D] + bg[:, :D])
        gr_ = _sigmoid(gt[:, D:] + bg[:, D:])
        dgates = jnp.concatenate([dm_ * ya * ga_ * (1.0 - ga_), dm_ * yr * gr_ * (1.0 - gr_)], axis=1)
        dya = (dm_ * ga_).astype(BF16)
        dyr = (dm_ * gr_).astype(BF16)
        do_ = _dot_nt(dya, wpav)
        dwpa_ = _dot_tn(ov, dya)
        dt = _dot_nt(dyr, wprv)
        dwpr_ = _dot_tn(t, dyr)
        drg_ = dt * yn * (sg * (1.0 + rg * (1.0 - sg)))
        dyn = dt * sl_
        dgr = jnp.sum(dyn * xh, axis=0, keepdims=True)
        dxh = dyn * gr
        drets = []
        for h in range(RET_H):
            sl = slice(h * RET_D, (h + 1) * RET_D)
            xhh, rstd = parts[h]
            dxhh = dxh[:, sl]
            drets.append(rstd * (dxhh - jnp.mean(dxhh, axis=-1, keepdims=True)
                                 - xhh * jnp.mean(dxhh * xhh, axis=-1, keepdims=True)))
        dret_ = jnp.concatenate(drets, axis=1)
        dbg = jnp.sum(dgates, axis=0, keepdims=True)
        return do_, dret_, drg_, dgates, dwpa_, dwpr_, dgr, dbg

    do_a, dret, drg, dgates, dwpa, dw_proj_ret, dg_ret, db_gate = rowwise(
        mix_bwd, [(o_a, None), (ret, None), (z, (512, 3)), (z, (2048, 1)), (dmix, None)], [wpa, wpr, g_ret, b_gate],
        [(MLA_H * HP, BF16), (512, F32), (512, BF16), (2 * D, BF16)],
        [((MLA_H * HP, D), F32), ((512, D), F32), ((1, 512), F32), ((1, 2 * D), F32)], tile=tile, name="mix_bwd")

    sl1 = slots_l1(dw_up_a, dw_up_b, dw_xkv, tile=tile)
    sl2, sl3 = slots_l23(dw_out, dw_xq, dw_xo, dw_down, dwpa, dw_proj_ret)
    lse_row, delta_row = mla_prep(o_a, do_a, lse, tq=tq)
    dq_a, dk_a, dv_a, rl1, rl2, rl3 = mla_bwd(q_a, k_a, v_a, do_a, lse_row, delta_row, [sl1, sl2, sl3], tq=tq)
    drq_r, drk_r, drv = ret_bwd(rq, rk, rv, rstate, dret, rconsts, rb=rb)

    def mixers_in_bwd(zl, cmv, s1v, s2v, dqv, dkv_, dvv, drq_, drk_, drv_, drg_, dgt, crv, srv, gq, gkv, wqv, wkv_, wvv):
        cqf, ckvf = zl[:, 0:256], zl[:, 256:384]
        cq = _rms(cqf, gq).astype(BF16)
        ckv = _rms(ckvf, gkv).astype(BF16)
        dq_pre = _rot_mla(dqv.astype(F32), cmv, -s1v, -s2v).astype(BF16)
        dkf = dkv_.astype(F32)
        dkr = dkf[:, 0:HP]
        for h in range(1, MLA_H):
            dkr = dkr + dkf[:, h * HP:(h + 1) * HP]
        lane = lax.broadcasted_iota(jnp.int32, dkr.shape, 1)
        dzk = _rot_mla(jnp.where((lane >= 64) & (lane < 96), dkr, 0.0), cmv, -s1v, -s2v)
        dkb = dkv_.astype(BF16)
        dvb = dvv.astype(BF16)
        dcq_n = _dot_nt(dq_pre, wqv)
        dckv_n = _dot_nt(dkb, wkv_) + _dot_nt(dvb, wvv)
        dwq_ = _dot_tn(cq, dq_pre)
        dwk_ = _dot_tn(ckv, dkb)
        dwv_ = _dot_tn(ckv, dvb)
        dcq, dgq = _rms_bwd(cqf, gq, dcq_n)
        dckv, dgkv = _rms_bwd(ckvf, gkv, dckv_n)
        a = _rot_ret(drq_, crv, -srv)
        b = _rot_ret(drk_, crv, -srv) * (RET_D ** -0.5)
        dz_ = jnp.concatenate([a.astype(BF16), b.astype(BF16), drv_.astype(BF16), drg_, dgt,
                               dcq.astype(BF16), dckv.astype(BF16), dzk.astype(BF16)], axis=1)
        return dz_, dwq_, dwk_, dwv_, dgq, dgkv

    dz, dwq, dwk, dwv, dg_q_lat, dg_kv_lat = rowwise(
        mixers_in_bwd, [(z, (512, 8)), (cm, None), (s1, None), (s2, None), (dq_a, None), (dk_a, None), (dv_a, None),
                        (drq_r, None), (drk_r, None), (drv, None), (drg, None), (dgates, None), (cr, None), (sr, None)],
        [g_q_lat, g_kv_lat, wq, wk, wv], [(ZW, BF16)],
        [((MLA_QR, MLA_H * HP), F32), ((MLA_KVR, MLA_H * HP), F32), ((MLA_KVR, MLA_H * HP), F32),
         ((1, MLA_QR), F32), ((1, MLA_KVR), F32)], tile=tile, name="mixers_in_bwd")
    dwz = matmul_tn(u, dz, name="mm_dw_z", tn=1536)
    se1, se2 = slots_early(dwz, dwk, dwv, dwq, tile=tile)
    grad_x, dg_mix, re1, re2 = rowwise(proj_norm_bwd, [(dz, None), (xs, None), (dh1, None)], [wz, g_mix], [(D, F32)],
                                       [((1, D), F32)], tile=tile2, name="mm_du_norm_bwd", exchange=[se1, se2])

    small_grads = {"g_mix": dg_mix, "b_gate": db_gate, "g_q_lat": dg_q_lat, "g_kv_lat": dg_kv_lat, "g_ret": dg_ret,
                   "g_cross": dg_cross, "g_mem": dg_mem, "g_ffn": dg_ffn, "g_final": dg_final}
    rd, rs, dslots, own_s = exchange_small(cs_a, cs_b, [small_grads[n] for n, _, _ in SMALL_DIRECT])

    me = _my_id()

    def own(slots):
        return lax.dynamic_index_in_dim(slots, me, axis=0, keepdims=False)

    def wmv(names):
        return [shard(n, p) for n in names for p in ("", "m_", "v_")]

    names_s = tuple(n for n, _, _ in SMALL)
    groups = (
        (("w_in",), adam_cols(own(se1), re1, wmv(("w_in",)), ((0, NS_IN),), name="adam_e1", tile=128)),
        (("w_ukv", "w_uq"), adam_rows(own(se2), re2, wmv(("w_ukv", "w_uq")),
                                      ((0, E2_UQ, HP), (E2_UQ, E2_ROWS, NS_UQ)), name="adam_e2")),
        (("w_up", "w_xkv"), adam_cols(own(sl1), rl1, wmv(("w_up", "w_xkv")), ((0, NS_UP), (NS_UP, L1_W)),
                                      name="adam_l1", tile=128)),
        (("w_out", "w_xq", "w_xo", "w_down"),
         adam_rows(own(sl2), rl2, wmv(("w_out", "w_xq", "w_xo", "w_down")),
                   ((L2_OUT, L2_XQ, D), (L2_XQ, L2_XO, D), (L2_XO, L2_DN, D), (L2_DN, L2_ROWS, D)), name="adam_l2")),
        (("w_proj_mla", "w_proj_ret"), adam_rows(own(sl3), rl3, wmv(("w_proj_mla", "w_proj_ret")),
                                                 ((0, L3_PRET, HP), (L3_PRET, L3_ROWS, HP)), name="adam_l3")),
        (names_s + ("w_conv",), adam_small(own_s, rs, dslots, rd, wmv(names_s), wmv(("w_conv",)))),
    )
    res = {}
    for names, outs_ in groups:
        for t, n in enumerate(names):
            res[n] = outs_[4 * t:4 * t + 4]

    order = ["g_mix", "w_in", "b_gate", "g_q_lat", "w_uq", "g_kv_lat", "w_ukv", "w_proj_mla", "g_ret", "w_proj_ret",
             "w_out", "g_cross", "g_mem", "w_xq", "w_xkv", "w_xo", "g_ffn", "w_up", "w_conv", "b_conv", "w_down",
             "g_final"]
    outs = [loss, grad_x[None]]
    for kind in range(4):
        outs += [res[n][kind].reshape(args[n].shape) for n in order]
    return tuple(outs)
```

```python
import functools
import math

import jax
import jax.numpy as jnp
import numpy as np
from jax import lax
from jax.experimental import pallas as pl
from jax.experimental.pallas import tpu as pltpu

F32 = jnp.float32
BF16 = jnp.bfloat16

D = 1024
MLA_H, MLA_NOPE, MLA_ROPE, MLA_V = 8, 64, 32, 64
MLA_QR, MLA_KVR = 256, 128
RET_H, RET_D, RET_C = 4, 128, 128
X_H, X_HD = 4, 256
D_FF = 2816
THETA = 10000.0
EPS = 1e-6
HP = 128
ZW = 4608
N_DEV = 8

ADAM_LR, ADAM_B1, ADAM_B2, ADAM_EPS, ADAM_WD, ADAM_STEP = 0.001, 0.9, 0.999, 1e-08, 0.01, 10

VMEM_LIMIT = 56 * 1024 * 1024
BIG_CONST_BYTES = 4 * 1024 * 1024
MESH = pl.DeviceIdType.MESH
VM = pl.BlockSpec(memory_space=pltpu.VMEM)
ANY = pl.BlockSpec(memory_space=pl.ANY)


def _cp(n_axes):
    return pltpu.CompilerParams(dimension_semantics=("arbitrary",) * n_axes, vmem_limit_bytes=VMEM_LIMIT)


def _cp0():
    return pltpu.CompilerParams(vmem_limit_bytes=VMEM_LIMIT)


def _pick(n, cap, mult=128):
    best = None
    for t in range(mult, min(n, cap) + 1, mult):
        if n % t == 0:
            best = t
    return best if best is not None else n


def _dot(a, b):
    return jnp.dot(a, b, preferred_element_type=F32)


def _dot_nt(a, b):
    return lax.dot_general(a, b, (((1,), (1,)), ((), ())), preferred_element_type=F32)


def _dot_tn(a, b):
    return lax.dot_general(a, b, (((0,), (0,)), ((), ())), preferred_element_type=F32)


def _sds(shape, dtype):
    return jax.ShapeDtypeStruct(shape, dtype)


def matmul(a, b, *, name, tb=False, out_dtype=F32, tm=1024, tn=1024):
    M, K = a.shape
    N = b.shape[0] if tb else b.shape[1]
    tm = _pick(M, tm, 8)
    tn = _pick(N, tn)

    def body(a_ref, b_ref, o_ref):
        av = a_ref[...].astype(BF16)
        bv = b_ref[...].astype(BF16)
        acc = _dot_nt(av, bv) if tb else _dot(av, bv)
        o_ref[...] = acc.astype(o_ref.dtype)

    return pl.pallas_call(
        body,
        name=name,
        grid=(M // tm, N // tn),
        in_specs=[pl.BlockSpec((tm, K), lambda i, j: (i, 0)),
                  pl.BlockSpec((tn, K), lambda i, j: (j, 0)) if tb else pl.BlockSpec((K, tn), lambda i, j: (0, j))],
        out_specs=pl.BlockSpec((tm, tn), lambda i, j: (i, j)),
        out_shape=_sds((M, N), out_dtype),
        compiler_params=_cp(2),
    )(a, b)


def matmul_tn(a, b, *, name, tm=1024, tn=1024, tk=1024):
    R, M = a.shape
    N = b.shape[1]
    tm = _pick(M, tm)
    tn = _pick(N, tn)
    tk = _pick(R, tk, 16)
    nk = R // tk

    def body(a_ref, b_ref, o_ref, acc_ref):
        k = pl.program_id(2)

        @pl.when(k == 0)
        def _():
            acc_ref[...] = jnp.zeros_like(acc_ref)

        acc_ref[...] += _dot_tn(a_ref[...].astype(BF16), b_ref[...].astype(BF16))

        @pl.when(k == nk - 1)
        def _():
            o_ref[...] = acc_ref[...]

    return pl.pallas_call(
        body,
        name=name,
        grid=(M // tm, N // tn, nk),
        in_specs=[pl.BlockSpec((tk, tm), lambda i, j, k: (k, i)), pl.BlockSpec((tk, tn), lambda i, j, k: (k, j))],
        out_specs=pl.BlockSpec((tm, tn), lambda i, j, k: (i, j)),
        out_shape=_sds((M, N), F32),
        scratch_shapes=[pltpu.VMEM((tm, tn), F32)],
        compiler_params=_cp(3),
    )(a, b)


def rowwise(fn, rows, consts, out_rows, out_accs, *, tile, name, gather=(), exchange=()):
    T = rows[0][0].shape[0]
    nt = T // tile
    travel = tuple(gather) + tuple(exchange)
    n_r, n_c, n_o, n_a, n_g = len(rows), len(consts), len(out_rows), len(out_accs), len(travel)
    n_in = n_r + n_c + n_g

    def body(*refs):
        if n_g:
            srcs, dsts = refs[n_r + n_c:n_in], refs[n_in + n_o + n_a:n_in + n_o + n_a + n_g]
            sems = tuple(refs[n_in + n_o + n_a + n_g:])
            if gather:
                comm = (srcs, dsts) + sems
                comm_start, comm_finish = _gather2_start, _gather2_finish
            else:
                comm = ([(r, True) for r in srcs], dsts) + sems
                comm_start, comm_finish = _exchange_start, _exchange_wait

            @pl.when(pl.program_id(0) == 0)
            def _():
                comm_start(*comm)

        ins = [r[...] for r in refs[: n_r + n_c]]
        outs = fn(*ins)
        if not isinstance(outs, (tuple, list)):
            outs = (outs,)
        o_refs = refs[n_in:n_in + n_o]
        a_refs = refs[n_in + n_o:n_in + n_o + n_a]
        for o_ref, o in zip(o_refs, outs[:n_o]):
            o_ref[...] = o.astype(o_ref.dtype)
        if n_a:
            first = pl.program_id(0) == 0

            @pl.when(first)
            def _():
                for a_ref, o in zip(a_refs, outs[n_o:]):
                    a_ref[...] = o.astype(a_ref.dtype)

            @pl.when(jnp.logical_not(first))
            def _():
                for a_ref, o in zip(a_refs, outs[n_o:]):
                    a_ref[...] += o.astype(a_ref.dtype)
        if n_g:

            @pl.when(pl.program_id(0) == nt - 1)
            def _():
                comm_finish(*comm)

    in_specs = []
    args = []
    for arr, win in rows:
        if win is None:
            in_specs.append(pl.BlockSpec((tile, arr.shape[1]), lambda i: (i, 0)))
        else:
            w, cb = win
            in_specs.append(pl.BlockSpec((tile, w), functools.partial(lambda i, cb: (i, cb), cb=cb)))
        args.append(arr)
    for c in consts:
        index_map = functools.partial(lambda i, nd: (0,) * nd, nd=c.ndim)
        if c.size * c.dtype.itemsize >= BIG_CONST_BYTES:
            in_specs.append(pl.BlockSpec(c.shape, index_map, pipeline_mode=pl.Buffered(1)))
        else:
            in_specs.append(pl.BlockSpec(c.shape, index_map))
        args.append(c)
    out_specs = [pl.BlockSpec((tile, w), lambda i: (i, 0)) for w, _ in out_rows]
    out_shape = [_sds((T, w), dt) for w, dt in out_rows]
    for shp, dt in out_accs:
        out_specs.append(pl.BlockSpec(shp, functools.partial(lambda i, nd: (0,) * nd, nd=len(shp))))
        out_shape.append(_sds(shp, dt))
    return pl.pallas_call(
        body,
        name=name,
        grid=(nt,),
        in_specs=in_specs + [ANY] * n_g,
        out_specs=out_specs + [ANY] * n_g,
        out_shape=out_shape + [_sds((N_DEV,) + p.shape, p.dtype) for p in gather]
        + [_sds((N_DEV - 1,) + s.shape[1:], s.dtype) for s in exchange],
        scratch_shapes=(_gather_scratch(n_g) if gather else _exchange_scratch(n_g)) if n_g else [],
        compiler_params=_cp(1),
    )(*args, *travel)


def _rms(x, g):
    r = lax.rsqrt(jnp.mean(x * x, axis=-1, keepdims=True) + EPS)
    return x * r * g


def _rms_bwd(x, g, dy):
    r = lax.rsqrt(jnp.mean(x * x, axis=-1, keepdims=True) + EPS)
    xh = x * r
    dg = jnp.sum(dy * xh, axis=0, keepdims=True)
    dxh = dy * g
    dx = r * (dxh - xh * jnp.mean(dxh * xh, axis=-1, keepdims=True))
    return dx, dg


def _sigmoid(x):
    return 0.5 * jnp.tanh(0.5 * x) + 0.5


def _rot_mla(x, c, s1, s2):
    n = x.shape[1] // HP
    outs = []
    for h in range(n):
        xh = x[:, h * HP : (h + 1) * HP]
        outs.append(xh * c + pltpu.roll(xh, HP - 16, 1) * s1 + pltpu.roll(xh, 16, 1) * s2)
    return outs[0] if n == 1 else jnp.concatenate(outs, axis=1)


def _rot_ret(x, c, s):
    n = x.shape[1] // RET_D
    outs = []
    for h in range(n):
        xh = x[:, h * RET_D : (h + 1) * RET_D]
        outs.append(xh * c + pltpu.roll(xh, RET_D // 2, 1) * s)
    return outs[0] if n == 1 else jnp.concatenate(outs, axis=1)


def _rot_inv():
    lane_np = np.arange(128)
    inv_m = (jnp.asarray(THETA, F32) ** (-jnp.asarray(lane_np & 15, F32) / 16.0)).reshape(1, 128)
    inv_r = (jnp.asarray(THETA, F32) ** (-jnp.asarray(lane_np & 63, F32) / 64.0)).reshape(1, 128)
    return inv_m, inv_r


def _rot_tables(p, im, ir):
    lane = lax.broadcasted_iota(jnp.int32, p.shape, 1)
    ang = p * im
    cm = jnp.where((lane >= 64) & (lane < 96), jnp.cos(ang), 1.0)
    sn = jnp.sin(ang)
    s1 = jnp.where((lane >= 64) & (lane < 80), -sn, 0.0)
    s2 = jnp.where((lane >= 80) & (lane < 96), sn, 0.0)
    angr = p * ir
    snr = jnp.sin(angr)
    return cm, s1, s2, jnp.cos(angr), jnp.where(lane < 64, -snr, snr)


def _peer(m):
    x, y, c = lax.axis_index("x"), lax.axis_index("y"), lax.axis_index("c")
    mx, my, mc = (m >> 2) & 1, (m >> 1) & 1, m & 1
    px = 1 - x if mx else x
    py = 1 - y if my else y
    pc = 1 - c if mc else c
    return (px, py, pc), 4 * px + 2 * py + pc


def _my_id():
    return 4 * lax.axis_index("x") + 2 * lax.axis_index("y") + lax.axis_index("c")


def _gather_copies(srcs, outs, send_sems, recv_sems, local_sems, arriving=False):
    me = _my_id()
    copies = []
    if not arriving:
        for g, (src, out) in enumerate(zip(srcs, outs)):
            copies.append((pltpu.make_async_copy(src, out.at[me], local_sems.at[g]), False))
    for m in range(1, N_DEV):
        peer, plin = _peer(m)
        for g, (src, out) in enumerate(zip(srcs, outs)):
            copies.append((pltpu.make_async_remote_copy(
                src_ref=src, dst_ref=out.at[plin if arriving else me], send_sem=send_sems.at[g, m - 1],
                recv_sem=recv_sems.at[g, m - 1], device_id=peer, device_id_type=MESH), True))
    return copies


def _gather_start(*a):
    for cp, _ in _gather_copies(*a):
        cp.start()


def _gather_wait(*a):
    for cp, _ in _gather_copies(*a, arriving=True):
        cp.wait_recv()
    for cp, remote in _gather_copies(*a):
        if remote:
            cp.wait_send()
        else:
            cp.wait()


CHIP_RELATIONS = (4, 2, 6)


def _gather2_copy(src, out, block, to, send_sems, recv_sems, g, k):
    return pltpu.make_async_remote_copy(src_ref=src, dst_ref=out.at[block], send_sem=send_sems.at[g, k],
                                        recv_sem=recv_sems.at[g, k], device_id=to, device_id_type=MESH)


def _gather2_start(srcs, outs, send_sems, recv_sems, local_sems):
    me = _my_id()
    sib, _ = _peer(1)
    for g, (src, out) in enumerate(zip(srcs, outs)):
        pltpu.make_async_copy(src, out.at[me], local_sems.at[g]).start()
        _gather2_copy(src, out, me, sib, send_sems, recv_sems, g, 0).start()
        for t, m in enumerate(CHIP_RELATIONS):
            _gather2_copy(src, out, me, _peer(m)[0], send_sems, recv_sems, g, 1 + t).start()


def _gather2_finish(srcs, outs, send_sems, recv_sems, local_sems):
    me = _my_id()
    sib, sib_lin = _peer(1)
    for t, m in enumerate(CHIP_RELATIONS):
        peer, plin = _peer(m)
        for g, (src, out) in enumerate(zip(srcs, outs)):
            _gather2_copy(src, out, plin, peer, send_sems, recv_sems, g, 1 + t).wait_recv()
            _gather2_copy(out.at[plin], out, plin, sib, send_sems, recv_sems, g, 4 + t).start()
    for g, (src, out) in enumerate(zip(srcs, outs)):
        _gather2_copy(src, out, sib_lin, sib, send_sems, recv_sems, g, 0).wait_recv()
        for t, m in enumerate(CHIP_RELATIONS):
            _gather2_copy(src, out, _peer(m | 1)[1], sib, send_sems, recv_sems, g, 4 + t).wait_recv()
    for g, (src, out) in enumerate(zip(srcs, outs)):
        _gather2_copy(src, out, me, sib, send_sems, recv_sems, g, 0).wait_send()
        for t, m in enumerate(CHIP_RELATIONS):
            peer, plin = _peer(m)
            _gather2_copy(src, out, me, peer, send_sems, recv_sems, g, 1 + t).wait_send()
            _gather2_copy(out.at[plin], out, plin, sib, send_sems, recv_sems, g, 4 + t).wait_send()
        pltpu.make_async_copy(src, out.at[me], local_sems.at[g]).wait()


def _gather_scratch(n):
    return [pltpu.SemaphoreType.DMA((n, N_DEV - 1)), pltpu.SemaphoreType.DMA((n, N_DEV - 1)), pltpu.SemaphoreType.DMA((n,))]


def _exchange_copies(srcs, dsts, send_sems, recv_sems):
    copies = []
    for m in range(1, N_DEV):
        peer, plin = _peer(m)
        for g, ((src, per_peer), dst) in enumerate(zip(srcs, dsts)):
            copies.append(pltpu.make_async_remote_copy(
                src_ref=src.at[plin] if per_peer else src, dst_ref=dst.at[m - 1], send_sem=send_sems.at[g, m - 1],
                recv_sem=recv_sems.at[g, m - 1], device_id=peer, device_id_type=MESH))
    return copies


def _exchange_start(*a):
    for cp in _exchange_copies(*a):
        cp.start()


def _exchange_wait(*a):
    copies = _exchange_copies(*a)
    for cp in copies:
        cp.wait_recv()
    for cp in copies:
        cp.wait_send()


def _exchange_scratch(n):
    return [pltpu.SemaphoreType.DMA((n, N_DEV - 1)), pltpu.SemaphoreType.DMA((n, N_DEV - 1))]


MLA_SCALE = (MLA_NOPE + MLA_ROPE) ** -0.5
MLA_C2 = MLA_SCALE * math.log2(math.e)
ONE_LANE = MLA_V
NEG = -1e30
HPS = 4
HPB = 4


def _tri_mask(n, lower_rows_ge_cols=True):
    r = lax.broadcasted_iota(jnp.int32, (n, n), 0)
    c = lax.broadcasted_iota(jnp.int32, (n, n), 1)
    return r >= c if lower_rows_ge_cols else c >= r


def mla_fwd(q, k, v, gather, *, tq):
    T = q.shape[0]
    nq = T // tq
    rep = tq // HP
    ng = len(gather)

    def body(*refs):
        q_ref, k_ref, v_ref = refs[:3]
        srcs = refs[3:3 + ng]
        o_ref, lse_ref = refs[3 + ng:5 + ng]
        outs = refs[5 + ng:5 + 2 * ng]
        m_sc, acc_sc = refs[5 + 2 * ng:7 + 2 * ng]
        comm = (srcs, outs) + tuple(refs[7 + 2 * ng:])
        h, i = pl.program_id(0), pl.program_id(1)

        @pl.when((h == 0) & (i == 0))
        def _():
            _gather_start(*comm)

        m_sc[...] = jnp.full(m_sc.shape, NEG, F32)
        acc_sc[...] = jnp.zeros(acc_sc.shape, F32)
        heads = [slice(t * HP, (t + 1) * HP) for t in range(HPS)]

        def block(j, masked):
            off = pl.multiple_of(j * tq, tq)
            ss = [_dot_nt(q_ref[:, hd], k_ref[pl.ds(off, tq), hd]) * MLA_C2 for hd in heads]
            for hd, s in zip(heads, ss):
                if masked:
                    s = jnp.where(_tri_mask(tq), s, NEG)
                m_prev = m_sc[:, hd]
                m_next = jnp.maximum(m_prev, jnp.max(s, axis=-1, keepdims=True))
                p = jnp.exp2(s - jnp.tile(m_next, (1, rep)))
                alpha = jnp.exp2(m_prev - m_next)
                acc_sc[:, hd] = alpha * acc_sc[:, hd] + _dot(p.astype(BF16), v_ref[pl.ds(off, tq), hd])
                m_sc[:, hd] = m_next

        def loop_body(j, carry):
            block(j, False)
            return carry

        lax.fori_loop(0, i, loop_body, 0)
        block(i, True)
        for hd in heads:
            acc = acc_sc[:, hd]
            l = acc[:, ONE_LANE:ONE_LANE + 1]
            o_ref[:, hd] = (acc / l).astype(o_ref.dtype)
            lse_ref[hd.start // HP] = (m_sc[:, hd] + jnp.log(l) * math.log2(math.e)).T[0:1, :]

        @pl.when((h == MLA_H // HPS - 1) & (i == nq - 1))
        def _():
            _gather_wait(*comm)

    blk = pl.BlockSpec((tq, HPS * HP), lambda h, i: (i, h))
    full = pl.BlockSpec((T, HPS * HP), lambda h, i: (0, h))
    return pl.pallas_call(
        body,
        name="mla_fwd",
        grid=(MLA_H // HPS, nq),
        in_specs=[blk, full, full] + [ANY] * ng,
        out_specs=[blk, pl.BlockSpec((HPS, None, 1, tq), lambda h, i: (h, i, 0, 0))] + [ANY] * ng,
        out_shape=[_sds((T, MLA_H * HP), BF16), _sds((MLA_H, nq, 1, tq), F32)]
        + [_sds((N_DEV,) + p.shape, p.dtype) for p in gather],
        scratch_shapes=[pltpu.VMEM((tq, HPS * HP), F32), pltpu.VMEM((tq, HPS * HP), F32)] + _gather_scratch(ng),
        compiler_params=_cp(2),
    )(q, k, v, *gather)


def mla_prep(delta, *, tq):
    T = delta.shape[0]
    nq = T // tq

    def body(d_ref, row_ref):
        dt = d_ref[...].T
        for h in range(MLA_H):
            row_ref[h] = dt[h:h + 1, :]

    return pl.pallas_call(
        body,
        name="mla_prep",
        grid=(nq,),
        in_specs=[pl.BlockSpec((tq, HP), lambda i: (i, 0))],
        out_specs=pl.BlockSpec((MLA_H, None, 1, tq), lambda i: (0, i, 0, 0)),
        out_shape=_sds((MLA_H, nq, 1, tq), F32),
        compiler_params=_cp(1),
    )(delta)


def mla_bwd(q, k, v, do, lse_row, delta_row, slots, *, tq):
    T = q.shape[0]
    nq = T // tq
    ns = len(slots)

    def body(*refs):
        q_ref, k_ref, v_ref, do_ref, lse_ref, delta_ref = refs[:6]
        srcs = [(r, True) for r in refs[6:6 + ns]]
        dq_ref, dk_ref, dv_ref = refs[6 + ns:9 + ns]
        dsts = refs[9 + ns:9 + 2 * ns]
        dk_sc, dv_sc = refs[9 + 2 * ns:11 + 2 * ns]
        comm = (srcs, dsts) + tuple(refs[11 + 2 * ns:])
        h, j = pl.program_id(0), pl.program_id(1)

        @pl.when((h == 0) & (j == 0))
        def _():
            _exchange_start(*comm)

        dk_sc[...] = jnp.zeros(dk_sc.shape, F32)
        dv_sc[...] = jnp.zeros(dv_sc.shape, F32)
        heads = [slice(t * HP, (t + 1) * HP) for t in range(HPB)]

        @pl.when(j == 0)
        def _():
            dq_ref[...] = jnp.zeros(dq_ref.shape, F32)

        def block(i, masked):
            off = pl.multiple_of(i * tq, tq)
            sts = [_dot_nt(k_ref[:, hd], q_ref[pl.ds(off, tq), hd]) * MLA_C2 for hd in heads]
            dpts = [_dot_nt(v_ref[:, hd], do_ref[pl.ds(off, tq), hd]) for hd in heads]
            for t, hd in enumerate(heads):
                st = sts[t]
                if masked:
                    st = jnp.where(_tri_mask(tq, False), st, NEG)
                pt = jnp.exp2(st - lse_ref[t, i])
                dv_sc[:, hd] += _dot(pt.astype(BF16), do_ref[pl.ds(off, tq), hd])
                dst = (pt * (dpts[t] - delta_ref[t, i]) * MLA_SCALE).astype(BF16)
                dk_sc[:, hd] += _dot(dst, q_ref[pl.ds(off, tq), hd])
                dq_ref[pl.ds(off, tq), hd] += _dot_tn(dst, k_ref[:, hd])

        block(j, True)

        def loop_body(i, carry):
            block(i, False)
            return carry

        lax.fori_loop(j + 1, nq, loop_body, 0)
        dk_ref[...] = dk_sc[...].astype(dk_ref.dtype)
        dv_ref[...] = dv_sc[...].astype(dv_ref.dtype)

        @pl.when((h == MLA_H // HPB - 1) & (j == nq - 1))
        def _():
            _exchange_wait(*comm)

    blk = pl.BlockSpec((tq, HPB * HP), lambda h, j: (j, h))
    full = pl.BlockSpec((T, HPB * HP), lambda h, j: (0, h), pipeline_mode=pl.Buffered(1))
    rows = pl.BlockSpec((HPB, nq, 1, tq), lambda h, j: (h, 0, 0, 0))
    return pl.pallas_call(
        body,
        name="mla_bwd",
        grid=(MLA_H // HPB, nq),
        in_specs=[full, blk, blk, full, rows, rows] + [ANY] * ns,
        out_specs=[full, blk, blk] + [ANY] * ns,
        out_shape=[_sds((T, MLA_H * HP), F32), _sds((T, MLA_H * HP), BF16), _sds((T, MLA_H * HP), BF16)]
        + [_sds((N_DEV - 1,) + s.shape[1:], s.dtype) for s in slots],
        scratch_shapes=[pltpu.VMEM((tq, HPB * HP), F32), pltpu.VMEM((tq, HPB * HP), F32)] + _exchange_scratch(ns),
        compiler_params=_cp(2),
    )(q, k, v, do, lse_row, delta_row, *slots)


def _ret_consts():
    h = jnp.arange(RET_H, dtype=F32)
    log_g = jnp.log1p(-jnp.exp2(-5.0 - h))
    idx = jnp.arange(RET_C, dtype=F32)
    rel = idx[:, None] - idx[None, :]
    dmask = jnp.where(rel >= 0, jnp.exp(log_g[:, None, None] * jnp.maximum(rel, 0.0)), 0.0)
    zeta = jnp.exp(log_g[:, None] * (RET_C - 1.0 - idx)[None, :])
    xi = jnp.exp(log_g[:, None] * (idx + 1.0)[None, :])
    decay = jnp.exp(log_g * RET_C)
    zb = jnp.broadcast_to(zeta[:, :, None], (RET_H, RET_C, RET_D))
    xb = jnp.broadcast_to(xi[:, :, None], (RET_H, RET_C, RET_D))
    db = jnp.broadcast_to(decay[:, None, None], (RET_H, RET_C, RET_D))
    return dmask.astype(F32), zb.astype(F32), xb.astype(F32), db.astype(F32)


def ret_fwd(rq, rk, rv, consts, *, rb):
    T = rq.shape[0]
    nb = T // rb
    ncb = rb // RET_C

    def body(q_ref, k_ref, v_ref, dm_ref, z_ref, x_ref, dc_ref, o_ref, st_ref, r_sc):
        @pl.when(pl.program_id(0) == 0)
        def _():
            r_sc[...] = jnp.zeros(r_sc.shape, F32)

        for c in range(ncb):
            sl = slice(c * RET_C, (c + 1) * RET_C)
            for h in range(RET_H):
                hd = slice(h * RET_D, (h + 1) * RET_D)
                q, k, v = q_ref[sl, hd], k_ref[sl, hd], v_ref[sl, hd]
                r = r_sc[h]
                rbf = r.astype(BF16)
                st_ref[sl, hd] = rbf
                s = _dot_nt(q, k) * dm_ref[h]
                inner = _dot(s.astype(BF16), v)
                cross = _dot((q.astype(F32) * x_ref[h]).astype(BF16), rbf)
                o_ref[sl, hd] = inner + cross
                kz = (k.astype(F32) * z_ref[h]).T.astype(BF16)
                r_sc[h] = r * dc_ref[h] + _dot(kz, v)

    blk = pl.BlockSpec((rb, RET_H * RET_D), lambda b: (b, 0))
    cst = pl.BlockSpec((RET_H, RET_C, RET_D), lambda b: (0, 0, 0))
    return pl.pallas_call(
        body,
        name="ret_fwd",
        grid=(nb,),
        in_specs=[blk, blk, blk, cst, cst, cst, cst],
        out_specs=[blk, blk],
        out_shape=[_sds((T, RET_H * RET_D), F32), _sds((T, RET_H * RET_D), BF16)],
        scratch_shapes=[pltpu.VMEM((RET_H, RET_D, RET_D), F32)],
        compiler_params=_cp(1),
    )(rq, rk, rv, *consts)


def ret_bwd(rq, rk, rv, st, dret, consts, *, rb):
    T = rq.shape[0]
    nb = T // rb
    ncb = rb // RET_C

    def body(q_ref, k_ref, v_ref, st_ref, do_ref, dm_ref, z_ref, x_ref, dc_ref, dq_ref, dk_ref, dv_ref, g_sc):
        @pl.when(pl.program_id(0) == 0)
        def _():
            g_sc[...] = jnp.zeros(g_sc.shape, F32)

        for c in reversed(range(ncb)):
            sl = slice(c * RET_C, (c + 1) * RET_C)
            for h in range(RET_H):
                hd = slice(h * RET_D, (h + 1) * RET_D)
                dm, zt, xi = dm_ref[h], z_ref[h], x_ref[h]
                q, k, v, rp = q_ref[sl, hd], k_ref[sl, hd], v_ref[sl, hd], st_ref[sl, hd]
                dob = do_ref[sl, hd].astype(BF16)
                qf, kf = q.astype(F32), k.astype(F32)
                gn = g_sc[h]
                gnb = gn.astype(BF16)
                s = _dot_nt(q, k) * dm
                ds = _dot_nt(dob, v) * dm
                dq = _dot(ds.astype(BF16), k) + _dot_nt(dob, rp) * xi
                dk = _dot(ds.T.astype(BF16), q) + _dot_nt(v, gnb) * zt
                dv = _dot(s.T.astype(BF16), dob) + _dot((kf * zt).astype(BF16), gnb)
                dq_ref[sl, hd] = dq.astype(dq_ref.dtype)
                dk_ref[sl, hd] = dk.astype(dk_ref.dtype)
                dv_ref[sl, hd] = dv.astype(dv_ref.dtype)
                g_sc[h] = _dot((qf * xi).T.astype(BF16), dob) + dc_ref[h] * gn

    blk = pl.BlockSpec((rb, RET_H * RET_D), lambda b: (nb - 1 - b, 0))
    cst = pl.BlockSpec((RET_H, RET_C, RET_D), lambda b: (0, 0, 0))
    return pl.pallas_call(
        body,
        name="ret_bwd",
        grid=(nb,),
        in_specs=[blk, blk, blk, blk, blk, cst, cst, cst, cst],
        out_specs=[blk, blk, blk],
        out_shape=[_sds((T, RET_H * RET_D), F32), _sds((T, RET_H * RET_D), F32), _sds((T, RET_H * RET_D), BF16)],
        scratch_shapes=[pltpu.VMEM((RET_H, RET_D, RET_D), F32)],
        compiler_params=_cp(1),
    )(rq, rk, rv, st, dret, *consts)


HALO = 16


def conv_act_fwd(up_pre, w_conv, b_conv, *, tile, cw):
    T = up_pre.shape[0]
    nt = T // tile
    ncol = D_FF // cw
    hb = tile // HALO

    def body(pa_ref, a_ref, pb_ref, b_ref, wa_ref, wb_ref, ba_ref, bb_ref, o_ref):
        i = pl.program_id(1)
        keep = (i > 0).astype(F32)

        def conv(prev_ref, cur_ref, w_ref, bias_ref):
            ext = jnp.concatenate([prev_ref[...].astype(F32) * keep, cur_ref[...].astype(F32)], axis=0)
            w = w_ref[...]
            y = ext * w[2:3, :] + pltpu.roll(ext, 1, 0) * w[1:2, :] + pltpu.roll(ext, 2, 0) * w[0:1, :] + bias_ref[...]
            return y[HALO:, :]

        a = conv(pa_ref, a_ref, wa_ref, ba_ref)
        b = conv(pb_ref, b_ref, wb_ref, bb_ref)
        o_ref[...] = (a * _sigmoid(a) * b).astype(o_ref.dtype)

    prev_a = pl.BlockSpec((HALO, cw), lambda j, i: (jnp.maximum(i * hb - 1, 0), j))
    cur_a = pl.BlockSpec((tile, cw), lambda j, i: (i, j))
    prev_b = pl.BlockSpec((HALO, cw), lambda j, i: (jnp.maximum(i * hb - 1, 0), j + ncol))
    cur_b = pl.BlockSpec((tile, cw), lambda j, i: (i, j + ncol))
    w_a = pl.BlockSpec((3, cw), lambda j, i: (0, j))
    w_b = pl.BlockSpec((3, cw), lambda j, i: (0, j + ncol))
    bias_a = pl.BlockSpec((1, cw), lambda j, i: (0, j))
    bias_b = pl.BlockSpec((1, cw), lambda j, i: (0, j + ncol))
    return pl.pallas_call(
        body,
        name="conv_act_fwd",
        grid=(ncol, nt),
        in_specs=[prev_a, cur_a, prev_b, cur_b, w_a, w_b, bias_a, bias_b],
        out_specs=pl.BlockSpec((tile, cw), lambda j, i: (i, j)),
        out_shape=_sds((T, D_FF), BF16),
        compiler_params=_cp(2),
    )(up_pre, up_pre, up_pre, up_pre, w_conv, w_conv, b_conv, b_conv)


def conv_act_bwd(up_pre, dact, w_conv, b_conv, *, tile, cw):
    T = up_pre.shape[0]
    nt = T // tile
    ncol = D_FF // cw
    hb = tile // HALO
    ext_rows = tile + 2 * HALO

    def body(pa_ref, a_ref, na_ref, pb_ref, b_ref, nb_ref, d_ref, nd_ref, wa_ref, wb_ref, ba_ref, bb_ref,
             dxa_ref, dxb_ref, sa_ref, sb_ref):
        i = pl.program_id(1)
        keep_p = (i > 0).astype(F32)
        keep_n = (i < nt - 1).astype(F32)

        def ext_of(prev_ref, cur_ref, next_ref):
            return jnp.concatenate(
                [prev_ref[...].astype(F32) * keep_p, cur_ref[...].astype(F32), next_ref[...].astype(F32) * keep_n], axis=0)

        def taps(ext):
            return ext, pltpu.roll(ext, 1, 0), pltpu.roll(ext, 2, 0)

        def conv(tp, w, bias):
            return tp[0] * w[2:3, :] + tp[1] * w[1:2, :] + tp[2] * w[0:1, :] + bias

        xa = taps(ext_of(pa_ref, a_ref, na_ref))
        xb = taps(ext_of(pb_ref, b_ref, nb_ref))
        wa, wb = wa_ref[...], wb_ref[...]
        a = conv(xa, wa, ba_ref[...])
        b = conv(xb, wb, bb_ref[...])
        dy = jnp.concatenate(
            [jnp.zeros((HALO, cw), F32), d_ref[...].astype(F32), nd_ref[...].astype(F32) * keep_n], axis=0)
        sg = _sigmoid(a)
        da = dy * b * (sg * (1.0 + a * (1.0 - sg)))
        db = dy * (a * sg)

        def back(dup, tp, w, dx_ref, s_ref):
            dx = dup * w[2:3, :] + pltpu.roll(dup, ext_rows - 1, 0) * w[1:2, :] + pltpu.roll(dup, ext_rows - 2, 0) * w[0:1, :]
            dx_ref[...] = dx[HALO:HALO + tile, :].astype(dx_ref.dtype)
            dc = dup[HALO:HALO + tile, :]
            r2 = jnp.sum(dc * tp[0][HALO:HALO + tile, :], axis=0, keepdims=True)
            r1 = jnp.sum(dc * tp[1][HALO:HALO + tile, :], axis=0, keepdims=True)
            r0 = jnp.sum(dc * tp[2][HALO:HALO + tile, :], axis=0, keepdims=True)
            rb = jnp.sum(dc, axis=0, keepdims=True)
            row = lax.broadcasted_iota(jnp.int32, (8, cw), 0)
            upd = (jnp.where(row == 0, r0, 0.0) + jnp.where(row == 1, r1, 0.0) + jnp.where(row == 2, r2, 0.0)
                   + jnp.where(row == 3, rb, 0.0))

            @pl.when(i == 0)
            def _():
                s_ref[...] = upd

            @pl.when(i > 0)
            def _():
                s_ref[...] += upd

        back(da, xa, wa, dxa_ref, sa_ref)
        back(db, xb, wb, dxb_ref, sb_ref)

    def prev_of(shift):
        return pl.BlockSpec((HALO, cw), lambda j, i: (jnp.maximum(i * hb - 1, 0), j + shift))

    def next_of(shift):
        return pl.BlockSpec((HALO, cw), lambda j, i: (jnp.minimum((i + 1) * hb, nt * hb - 1), j + shift))

    def cur_of(shift):
        return pl.BlockSpec((tile, cw), lambda j, i: (i, j + shift))

    def row_of(rows, shift):
        return pl.BlockSpec((rows, cw), lambda j, i: (0, j + shift))

    return pl.pallas_call(
        body,
        name="conv_act_bwd",
        grid=(ncol, nt),
        in_specs=[prev_of(0), cur_of(0), next_of(0), prev_of(ncol), cur_of(ncol), next_of(ncol), cur_of(0), next_of(0),
                  row_of(3, 0), row_of(3, ncol), row_of(1, 0), row_of(1, ncol)],
        out_specs=[cur_of(0), cur_of(0), row_of(8, 0), row_of(8, 0)],
        out_shape=[_sds((T, D_FF), BF16), _sds((T, D_FF), BF16), _sds((8, D_FF), F32), _sds((8, D_FF), F32)],
        compiler_params=_cp(2),
    )(up_pre, up_pre, up_pre, up_pre, up_pre, up_pre, dact, dact, w_conv, w_conv, b_conv, b_conv)


NS_IN, NS_UP, NS_XKV, NS_UQ = 564, 704, 256, 96
E2_ROWS, E2_UQ = 384, 128
L1_W = NS_UP + NS_XKV
L2_ROWS = 736
L2_OUT, L2_XQ, L2_XO, L2_DN = 0, 128, 256, 384
RS_DN = 352
L3_ROWS, L3_PRET = 1024, 512
L4_SHAPE = (8, 768)
WZ_RUNS = ((416, 4096, 0), (0, 384, 4096), (384, 32, 4544))
WZ_ZERO = ((4480, 4544), (4576, 4608))


def _pieces(orig_start, length, dst_start, ns):
    out, c, d, end = [], orig_start, dst_start, orig_start + length
    while c < end:
        j, off = c // ns, c % ns
        ln = min(ns - off, end - c)
        out.append((j, off, d, ln))
        c += ln
        d += ln
    return out


def pack_early(w_in, w_ukv, w_uq):
    def body(in_ref, ukv_ref, uq_ref, e1_ref, e2_ref):
        e1_ref[...] = in_ref[...].astype(BF16)
        e2_ref[0:E2_UQ, :] = ukv_ref[...].astype(BF16)
        e2_ref[E2_UQ:E2_ROWS, 0:NS_UQ] = uq_ref[...].astype(BF16)
        e2_ref[E2_UQ:E2_ROWS, NS_UQ:HP] = jnp.zeros((E2_ROWS - E2_UQ, HP - NS_UQ), BF16)

    return pl.pallas_call(
        body,
        name="pack_early",
        in_specs=[VM] * 3,
        out_specs=[VM] * 2,
        out_shape=[_sds((D, NS_IN), BF16), _sds((E2_ROWS, HP), BF16)],
        compiler_params=_cp0(),
    )(w_in, w_ukv, w_uq)


def pack_late(w_up, w_xkv, w_out, w_xq, w_xo, w_down, w_pmla, w_pret, w_conv):
    def body(up_ref, xkv_ref, o_ref, xq_ref, xo_ref, dn_ref, pm_ref, pr_ref, cv_ref, l1_ref, l2_ref, l3_ref, l4_ref):
        l1_ref[:, 0:NS_UP] = up_ref[...].astype(BF16)
        l1_ref[:, NS_UP:L1_W] = xkv_ref[...].astype(BF16)
        l2_ref[L2_OUT:L2_XQ, :] = o_ref[...].astype(BF16)
        l2_ref[L2_XQ:L2_XO, :] = xq_ref[...].astype(BF16)
        l2_ref[L2_XO:L2_DN, :] = xo_ref[...].astype(BF16)
        l2_ref[L2_DN:L2_ROWS, :] = dn_ref[...].astype(BF16)
        l3_ref[0:L3_PRET, :] = pm_ref[...].astype(BF16)
        l3_ref[L3_PRET:L3_ROWS, :] = pr_ref[...].astype(BF16)
        l4_ref[...] = jnp.zeros(L4_SHAPE, F32)
        l4_ref[0:3, 0:NS_UP] = cv_ref[...]

    return pl.pallas_call(
        body,
        name="pack_late",
        in_specs=[VM] * 9,
        out_specs=[VM] * 4,
        out_shape=[_sds((D, L1_W), BF16), _sds((L2_ROWS, D), BF16), _sds((L3_ROWS, HP), BF16), _sds(L4_SHAPE, F32)],
        compiler_params=_cp0(),
    )(w_up, w_xkv, w_out, w_xq, w_xo, w_down, w_pmla, w_pret, w_conv)


def assemble_early(g1, g2, *, tile):
    def body(g1_ref, g2_ref, wz_ref, wk_ref, wv_ref, wq_ref):
        for lo, hi in WZ_ZERO:
            wz_ref[:, lo:hi] = jnp.zeros((tile, hi - lo), BF16)
        for os_, ln_, ds_ in WZ_RUNS:
            for j, off, d, ln in _pieces(os_, ln_, ds_, NS_IN):
                wz_ref[:, d:d + ln] = g1_ref[j, :, off:off + ln]

        @pl.when(pl.program_id(0) == 0)
        def _():
            half = jnp.zeros((MLA_KVR, HP - MLA_NOPE), BF16)
            for j in range(N_DEV):
                wk_ref[:, j * HP:j * HP + MLA_NOPE] = g2_ref[j, 0:E2_UQ, 0:MLA_NOPE]
                wk_ref[:, j * HP + MLA_NOPE:(j + 1) * HP] = half
                wv_ref[:, j * HP:j * HP + MLA_V] = g2_ref[j, 0:E2_UQ, MLA_NOPE:HP]
                wv_ref[:, j * HP + MLA_V:(j + 1) * HP] = half
                wq_ref[:, j * HP:j * HP + NS_UQ] = g2_ref[j, E2_UQ:E2_ROWS, 0:NS_UQ]
                wq_ref[:, j * HP + NS_UQ:(j + 1) * HP] = jnp.zeros((MLA_QR, HP - NS_UQ), BF16)

    def whole(r):
        return pl.BlockSpec((r, MLA_H * HP), lambda i: (0, 0))

    return pl.pallas_call(
        body,
        name="assemble_early",
        grid=(D // tile,),
        in_specs=[pl.BlockSpec((N_DEV, tile, NS_IN), lambda i: (0, i, 0)),
                  pl.BlockSpec((N_DEV, E2_ROWS, HP), lambda i: (0, 0, 0))],
        out_specs=[pl.BlockSpec((tile, ZW), lambda i: (i, 0)), whole(MLA_KVR), whole(MLA_KVR), whole(MLA_QR)],
        out_shape=[_sds((D, ZW), BF16), _sds((MLA_KVR, MLA_H * HP), BF16), _sds((MLA_KVR, MLA_H * HP), BF16),
                   _sds((MLA_QR, MLA_H * HP), BF16)],
        compiler_params=_cp(1),
    )(g1, g2)


def assemble_l1(g1, *, tile):
    def body(g_ref, wup_ref, wxkv_ref):
        for j in range(N_DEV):
            wup_ref[:, j * NS_UP:(j + 1) * NS_UP] = g_ref[j, :, 0:NS_UP]
            wxkv_ref[:, j * NS_XKV:(j + 1) * NS_XKV] = g_ref[j, :, NS_UP:L1_W]

    return pl.pallas_call(
        body,
        name="assemble_l1",
        grid=(D // tile,),
        in_specs=[pl.BlockSpec((N_DEV, tile, L1_W), lambda i: (0, i, 0))],
        out_specs=[pl.BlockSpec((tile, 2 * D_FF), lambda i: (i, 0)), pl.BlockSpec((tile, 2 * D), lambda i: (i, 0))],
        out_shape=[_sds((D, 2 * D_FF), BF16), _sds((D, 2 * D), BF16)],
        compiler_params=_cp(1),
    )(g1)


def assemble_l234(g2, g3, g4):
    def body(g2_ref, g3_ref, g4_ref, wo_ref, wxq_ref, wxo_ref, wdn_ref, wpa_ref, wpr_ref, wc_ref):
        for j in range(N_DEV):
            wo_ref[j * 128:(j + 1) * 128, :] = g2_ref[j, L2_OUT:L2_XQ, :]
            wxq_ref[j * 128:(j + 1) * 128, :] = g2_ref[j, L2_XQ:L2_XO, :]
            wxo_ref[j * 128:(j + 1) * 128, :] = g2_ref[j, L2_XO:L2_DN, :]
            wdn_ref[j * RS_DN:(j + 1) * RS_DN, :] = g2_ref[j, L2_DN:L2_ROWS, :]
            wpr_ref[:, j * 128:(j + 1) * 128] = g3_ref[j, L3_PRET:L3_ROWS, :]
            wc_ref[:, j * NS_UP:(j + 1) * NS_UP] = g4_ref[j, 0:3, 0:NS_UP]
            for h in range(MLA_H):
                wpa_ref[h * HP:h * HP + MLA_V, j * 128:(j + 1) * 128] = g3_ref[j, h * MLA_V:(h + 1) * MLA_V, :]
        for h in range(MLA_H):
            wpa_ref[h * HP + MLA_V:(h + 1) * HP, :] = jnp.zeros((HP - MLA_V, D), BF16)

    return pl.pallas_call(
        body,
        name="assemble_l234",
        in_specs=[VM] * 3,
        out_specs=[VM] * 7,
        out_shape=[_sds((D, D), BF16), _sds((D, D), BF16), _sds((D, D), BF16), _sds((D_FF, D), BF16),
                   _sds((MLA_H * HP, D), BF16), _sds((RET_H * RET_D, D), BF16), _sds((3, 2 * D_FF), F32)],
        compiler_params=_cp0(),
    )(g2, g3, g4)


def slots_l1(dwup_a, dwup_b, dwxkv, *, tile):
    per_half = D_FF // NS_UP

    def body(ua_ref, ub_ref, x_ref, s_ref):
        for j in range(N_DEV):
            src = ua_ref if j < per_half else ub_ref
            c0 = (j % per_half) * NS_UP
            s_ref[j, :, 0:NS_UP] = src[:, c0:c0 + NS_UP].astype(BF16)
            s_ref[j, :, NS_UP:L1_W] = x_ref[:, j * NS_XKV:(j + 1) * NS_XKV].astype(BF16)

    return pl.pallas_call(
        body,
        name="slots_l1",
        grid=(D // tile,),
        in_specs=[pl.BlockSpec((tile, D_FF), lambda i: (i, 0)), pl.BlockSpec((tile, D_FF), lambda i: (i, 0)),
                  pl.BlockSpec((tile, 2 * D), lambda i: (i, 0))],
        out_specs=pl.BlockSpec((N_DEV, tile, L1_W), lambda i: (0, i, 0)),
        out_shape=_sds((N_DEV, D, L1_W), BF16),
        compiler_params=_cp(1),
    )(dwup_a, dwup_b, dwxkv)


def slots_l23(dwo, dwxq, dwxo, dwdn, dwpa, dwpr):
    def body(o_ref, xq_ref, xo_ref, dn_ref, pa_ref, pr_ref, l2_ref, l3_ref):
        l2_ref[L2_OUT:L2_XQ, :] = o_ref[...].astype(BF16)
        l2_ref[L2_XQ:L2_XO, :] = xq_ref[...].astype(BF16)
        l2_ref[L2_XO:L2_DN, :] = xo_ref[...].astype(BF16)
        l2_ref[L2_DN:L2_ROWS, :] = dn_ref[...].astype(BF16)
        for h in range(MLA_H):
            l3_ref[h * MLA_V:(h + 1) * MLA_V, :] = pa_ref[h * HP:h * HP + MLA_V, :].astype(BF16)
        l3_ref[L3_PRET:L3_ROWS, :] = pr_ref[...].astype(BF16)

    rows128 = pl.BlockSpec((128, D), lambda j: (j, 0))

    def cols(r):
        return pl.BlockSpec((r, 128), lambda j: (0, j))

    return pl.pallas_call(
        body,
        name="slots_l23",
        grid=(N_DEV,),
        in_specs=[rows128, rows128, rows128, pl.BlockSpec((RS_DN, D), lambda j: (j, 0)),
                  cols(MLA_H * HP), cols(RET_H * RET_D)],
        out_specs=[pl.BlockSpec((None, L2_ROWS, D), lambda j: (j, 0, 0)),
                   pl.BlockSpec((None, L3_ROWS, HP), lambda j: (j, 0, 0))],
        out_shape=[_sds((N_DEV, L2_ROWS, D), BF16), _sds((N_DEV, L3_ROWS, HP), BF16)],
        compiler_params=_cp(1),
    )(dwo, dwxq, dwxo, dwdn, dwpa, dwpr)


def slots_early(dwz, dwk, dwv, dwq, *, tile):
    def body(dz_ref, k_ref, v_ref, q_ref, s1_ref, s2_ref):
        for os_, ln_, ds_ in WZ_RUNS:
            for j, off, d, ln in _pieces(os_, ln_, ds_, NS_IN):
                s1_ref[j, :, off:off + ln] = dz_ref[:, d:d + ln].astype(BF16)

        @pl.when(pl.program_id(0) == 0)
        def _():
            for j in range(N_DEV):
                s2_ref[j, 0:E2_UQ, 0:MLA_NOPE] = k_ref[:, j * HP:j * HP + MLA_NOPE].astype(BF16)
                s2_ref[j, 0:E2_UQ, MLA_NOPE:HP] = v_ref[:, j * HP:j * HP + MLA_V].astype(BF16)
                s2_ref[j, E2_UQ:E2_ROWS, 0:NS_UQ] = q_ref[:, j * HP:j * HP + NS_UQ].astype(BF16)
                s2_ref[j, E2_UQ:E2_ROWS, NS_UQ:HP] = jnp.zeros((E2_ROWS - E2_UQ, HP - NS_UQ), BF16)

    def whole(r):
        return pl.BlockSpec((r, MLA_H * HP), lambda i: (0, 0))

    return pl.pallas_call(
        body,
        name="slots_early",
        grid=(D // tile,),
        in_specs=[pl.BlockSpec((tile, ZW), lambda i: (i, 0)), whole(MLA_KVR), whole(MLA_KVR), whole(MLA_QR)],
        out_specs=[pl.BlockSpec((N_DEV, tile, NS_IN), lambda i: (0, i, 0)),
                   pl.BlockSpec((N_DEV, E2_ROWS, HP), lambda i: (0, 0, 0))],
        out_shape=[_sds((N_DEV, D, NS_IN), BF16), _sds((N_DEV, E2_ROWS, HP), BF16)],
        compiler_params=_cp(1),
    )(dwz, dwk, dwv, dwq)


SMALL = (("g_mix", 1024, 0), ("b_gate", 2048, 1), ("g_q_lat", 256, 3), ("g_kv_lat", 128, 4), ("g_ret", 512, 5),
         ("g_cross", 1024, 6), ("g_mem", 1024, 7), ("g_ffn", 1024, 8), ("b_conv", 5632, 9), ("g_final", 1024, 15))
SMALL_DIRECT = tuple(s for s in SMALL if s[0] != "b_conv")
S_ROWS = 16


def _flat_pieces(n, row0):
    return [(row0 + k // D, k, min(D, n - k)) for k in range(0, n, D)]


def exchange_small(cs_a, cs_b, smalls):
    ns = len(smalls)

    def body(*refs):
        ca_ref, cb_ref = refs[:2]
        small_refs = refs[2:2 + ns]
        rd_ref, rs_ref, dsl_ref, own_ref = refs[2 + ns:6 + ns]
        sems = refs[6 + ns:]
        own_ref[...] = jnp.zeros(own_ref.shape, F32)
        for (name, n, row0), g_ref in zip(SMALL_DIRECT, small_refs):
            for r, c0, ln in _flat_pieces(n, row0):
                own_ref[r:r + 1, 0:ln] = g_ref[:, c0:c0 + ln]
        row0 = dict((s[0], s[2]) for s in SMALL)["b_conv"]
        for half, c_ref in enumerate((ca_ref, cb_ref)):
            k = half * D_FF
            end = k + D_FF
            while k < end:
                r, lane = row0 + k // D, k % D
                ln = min(D - lane, end - k)
                own_ref[r:r + 1, lane:lane + ln] = c_ref[3:4, k - half * D_FF:k - half * D_FF + ln]
                k += ln
        dsl_ref[...] = jnp.zeros(dsl_ref.shape, F32)
        per_half = D_FF // NS_UP
        for j in range(N_DEV):
            c_ref = ca_ref if j < per_half else cb_ref
            c0 = (j % per_half) * NS_UP
            dsl_ref[j, 0:3, 0:NS_UP] = c_ref[0:3, c0:c0 + NS_UP]
        comm = ([(dsl_ref, True), (own_ref, False)], [rd_ref, rs_ref]) + tuple(sems)
        _exchange_start(*comm)
        _exchange_wait(*comm)

    n1 = N_DEV - 1
    return pl.pallas_call(
        body,
        name="exchange_small",
        in_specs=[VM] * (2 + ns),
        out_specs=[VM, VM, VM, VM],
        out_shape=[_sds((n1,) + L4_SHAPE, F32), _sds((n1, S_ROWS, D), F32), _sds((N_DEV,) + L4_SHAPE, F32),
                   _sds((S_ROWS, D), F32)],
        scratch_shapes=_exchange_scratch(2),
        compiler_params=_cp0(),
    )(cs_a, cs_b, *smalls)


def _adamw(w, g, m, v):
    m = ADAM_B1 * m + (1.0 - ADAM_B1) * g
    v = ADAM_B2 * v + (1.0 - ADAM_B2) * (g * g)
    m_hat = m / (1.0 - ADAM_B1 ** ADAM_STEP)
    v_hat = v / (1.0 - ADAM_B2 ** ADAM_STEP)
    delta = -ADAM_LR * (m_hat / (jnp.sqrt(v_hat) + ADAM_EPS) + ADAM_WD * w)
    return delta, m, v


def _apply(g, refs, outs):
    d, mn, vn = _adamw(refs[0][...], g, refs[1][...], refs[2][...])
    outs[0][...] = g
    outs[1][...] = d
    outs[2][...] = mn
    outs[3][...] = vn


def adam_cols(own, recv, wmv, spans, *, name, tile):
    R, W = own.shape
    nw = len(spans)

    def body(*refs):
        own_ref, recv_ref = refs[:2]
        ins, outs = refs[2:2 + 3 * nw], refs[2 + 3 * nw:]
        g = own_ref[...].astype(F32)
        for k in range(N_DEV - 1):
            g = g + recv_ref[k].astype(F32)
        for t, (lo, hi) in enumerate(spans):
            _apply(g[:, lo:hi], ins[3 * t:3 * t + 3], outs[4 * t:4 * t + 4])

    def blk(w):
        return pl.BlockSpec((tile, w), lambda i: (i, 0))

    widths = [hi - lo for lo, hi in spans]
    return pl.pallas_call(
        body,
        name=name,
        grid=(R // tile,),
        in_specs=[blk(W), pl.BlockSpec((N_DEV - 1, tile, W), lambda i: (0, i, 0))] + [blk(w) for w in widths for _ in range(3)],
        out_specs=[blk(w) for w in widths for _ in range(4)],
        out_shape=[_sds((R, w), F32) for w in widths for _ in range(4)],
        compiler_params=_cp(1),
    )(own, recv, *wmv)


def adam_rows(own, recv, wmv, spans, *, name):
    nw = len(spans)

    def body(*refs):
        own_ref, recv_ref = refs[:2]
        ins, outs = refs[2:2 + 3 * nw], refs[2 + 3 * nw:]
        for t, (lo, hi, w) in enumerate(spans):
            g = own_ref[lo:hi, :].astype(F32)
            for k in range(N_DEV - 1):
                g = g + recv_ref[k, lo:hi, :].astype(F32)
            _apply(g[:, 0:w], ins[3 * t:3 * t + 3], outs[4 * t:4 * t + 4])

    return pl.pallas_call(
        body,
        name=name,
        in_specs=[VM] * (2 + 3 * nw),
        out_specs=[VM] * (4 * nw),
        out_shape=[_sds((hi - lo, w), F32) for lo, hi, w in spans for _ in range(4)],
        compiler_params=_cp0(),
    )(own, recv, *wmv)


def adam_small(own_s, recv_s, dslots, recv_d, wmv_small, wmv_conv):
    ns = len(SMALL)

    def body(*refs):
        own_ref, rs_ref, dsl_ref, rd_ref = refs[:4]
        ins = refs[4:4 + 3 * ns + 3]
        outs = refs[4 + 3 * ns + 3:4 + 3 * ns + 3 + 4 * ns + 4]
        all_sc = refs[-1]
        me = _my_id()
        all_sc[0] = own_ref[...]
        for k in range(N_DEV - 1):
            all_sc[k + 1] = rs_ref[k]
        g = all_sc[jnp.bitwise_xor(me, 0)]
        for s in range(1, N_DEV):
            g = g + all_sc[jnp.bitwise_xor(me, s)]
        all_sc[0] = g
        for t, (name, n, row0) in enumerate(SMALL):
            pieces = [all_sc[0, r:r + 1, 0:ln] for r, _, ln in _flat_pieces(n, row0)]
            gt = pieces[0] if len(pieces) == 1 else jnp.concatenate(pieces, axis=1)
            _apply(gt, ins[3 * t:3 * t + 3], outs[4 * t:4 * t + 4])
        gc = dsl_ref[me]
        for k in range(N_DEV - 1):
            gc = gc + rd_ref[k]
        _apply(gc[0:3, 0:NS_UP], ins[3 * ns:3 * ns + 3], outs[4 * ns:4 * ns + 4])

    out_shape = [_sds((1, n), F32) for _, n, _ in SMALL for _ in range(4)] + [_sds((3, NS_UP), F32)] * 4
    return pl.pallas_call(
        body,
        name="adam_small",
        in_specs=[VM] * (4 + 3 * ns + 3),
        out_specs=[VM] * len(out_shape),
        out_shape=out_shape,
        scratch_shapes=[pltpu.VMEM((N_DEV, S_ROWS, D), F32)],
    )(own_s, recv_s, dslots, recv_d, *wmv_small, *wmv_conv)


def kernel(x, mem, positions, g_mix, w_in, b_gate, g_q_lat, w_uq, g_kv_lat, w_ukv, w_proj_mla, g_ret, w_proj_ret, w_out, g_cross, g_mem, w_xq, w_xkv, w_xo, g_ffn, w_up, w_conv, b_conv, w_down, g_final, loss_target, m_g_mix, m_w_in, m_b_gate, m_g_q_lat, m_w_uq, m_g_kv_lat, m_w_ukv, m_w_proj_mla, m_g_ret, m_w_proj_ret, m_w_out, m_g_cross, m_g_mem, m_w_xq, m_w_xkv, m_w_xo, m_g_ffn, m_w_up, m_w_conv, m_b_conv, m_w_down, m_g_final, v_g_mix, v_w_in, v_b_gate, v_g_q_lat, v_w_uq, v_g_kv_lat, v_w_ukv, v_w_proj_mla, v_g_ret, v_w_proj_ret, v_w_out, v_g_cross, v_g_mem, v_w_xq, v_w_xkv, v_w_xo, v_g_ffn, v_w_up, v_w_conv, v_b_conv, v_w_down, v_g_final):
    args = dict(locals())
    T = x.shape[1]
    M = mem.shape[1]
    tile = min(256, T)
    tile2 = min(512, T)
    tile4 = min(1024, T)
    tq = min(512, T)
    rb = min(1024, T)

    xs = x[0]
    tgt = loss_target[0]
    mems = mem[0]

    def shard(name, prefix=""):
        a = args[prefix + name]
        return a.reshape(a.shape[-2:]) if a.ndim >= 2 else a.reshape(1, -1)

    e1, e2 = pack_early(shard("w_in"), shard("w_ukv"), shard("w_uq"))
    late_parts = pack_late(shard("w_up"), shard("w_xkv"), shard("w_out"), shard("w_xq"), shard("w_xo"), shard("w_down"),
                           shard("w_proj_mla"), shard("w_proj_ret"), shard("w_conv"))

    pos_f = jnp.broadcast_to(positions[0].astype(F32)[:, None], (T, 128))
    inv_m, inv_r = _rot_inv()
    u, cm, s1, s2, cr, sr, ge1, ge2 = rowwise(
        lambda xv, p, im, ir, g: (_rms(xv, g),) + _rot_tables(p, im, ir), [(xs, None), (pos_f, None)],
        [inv_m, inv_r, g_mix], [(D, BF16)] + [(128, F32)] * 5, [], tile=tile2, name="norm_mix_tables", gather=[e1, e2])
    wz, wk, wv, wq = assemble_early(ge1, ge2, tile=tile)
    rconsts = _ret_consts()

    z = matmul(u, wz, name="mm_z", tm=2048, tn=1536)

    def mixers_in(zl, zr, cmv, s1v, s2v, crv, srv, gq, gkv, wqv, wkv_, wvv):
        cq = _rms(zl[:, 0:256], gq).astype(BF16)
        ckv = _rms(zl[:, 256:384], gkv).astype(BF16)
        qv = _rot_mla(_dot(cq, wqv), cmv, s1v, s2v)
        kr = _rot_mla(zl[:, 384:512], cmv, s1v, s2v)
        kn = _dot(ckv, wkv_)
        kv_ = jnp.concatenate([kn[:, h * HP:(h + 1) * HP] + kr for h in range(MLA_H)], axis=1)
        vv = _dot(ckv, wvv)
        lane = lax.broadcasted_iota(jnp.int32, vv.shape, 1)
        vv = jnp.where((lane & (HP - 1)) == ONE_LANE, 1.0, vv)
        rqv = _rot_ret(zr[:, 0:512], crv, srv)
        rkv = _rot_ret(zr[:, 512:1024], crv, srv) * (RET_D ** -0.5)
        return qv, kv_, vv, rqv, rkv, zr[:, 1024:1536]

    q_a, k_a, v_a, rq, rk, rv = rowwise(
        mixers_in, [(z, (512, 8)), (z, (2048, 0)), (cm, None), (s1, None), (s2, None), (cr, None), (sr, None)],
        [g_q_lat, g_kv_lat, wq, wk, wv], [(MLA_H * HP, BF16)] * 3 + [(512, BF16)] * 3, [], tile=tile2, name="mixers_in")

    o_a, lse_row, gl1, gl2, gl3, gl4 = mla_fwd(q_a, k_a, v_a, list(late_parts), tq=tq)
    wup, wxkv = assemble_l1(gl1, tile=tile)
    wo, wxq, wxo, wdn, wpa, wpr, wcv = assemble_l234(gl2, gl3, gl4)
    ret, rstate = ret_fwd(rq, rk, rv, rconsts, rb=rb)

    def gn_parts(r):
        outs = []
        for h in range(RET_H):
            rh = r[:, h * RET_D:(h + 1) * RET_D]
            mu = jnp.mean(rh, axis=-1, keepdims=True)
            dlt = rh - mu
            rstd = lax.rsqrt(jnp.mean(dlt * dlt, axis=-1, keepdims=True) + EPS)
            outs.append((dlt * rstd, rstd))
        return outs

    def mix_fwd(ov, rv_, rg, gt, wpav, wprv, gr, bg):
        ya = _dot(ov, wpav)
        xh = jnp.concatenate([p[0] for p in gn_parts(rv_)], axis=1)
        t = rg * _sigmoid(rg) * (xh * gr)
        yr = _dot(t.astype(BF16), wprv)
        ga_ = _sigmoid(gt[:, :D] + bg[:, :D])
        gr_ = _sigmoid(gt[:, D:] + bg[:, D:])
        return ga_ * ya + gr_ * yr

    mix = rowwise(mix_fwd, [(o_a, None), (ret, None), (z, (512, 3)), (z, (2048, 1))], [wpa, wpr, g_ret, b_gate],
                  [(D, BF16)], [], tile=tile, name="mix_fwd")[0]
    def proj_norm(av, rv_, wv_, g):
        hv = rv_ + _dot(av, wv_)
        return hv, _rms(hv, g)

    h1, n2 = rowwise(proj_norm, [(mix, None), (xs, None)], [wo, g_cross], [(D, F32), (D, BF16)], [], tile=tile4,
                     name="mm_out_norm")
    xq = matmul(n2, wxq, out_dtype=BF16, name="mm_xq")
    mn = rowwise(lambda mv_, g: _rms(mv_, g), [(mems, None)], [g_mem], [(D, BF16)], [], tile=min(tile, M), name="norm_mem")[0]
    mkv = matmul(mn, wxkv, out_dtype=BF16, name="mm_mkv")

    x_scale = X_HD ** -0.5

    def xattn_fwd(xqv, mkvv):
        outs = []
        for h in range(X_H):
            sl = slice(h * X_HD, (h + 1) * X_HD)
            s = _dot_nt(xqv[:, sl], mkvv[:, sl]) * x_scale
            s = s - jnp.max(s, axis=-1, keepdims=True)
            e = jnp.exp(s)
            p = e / jnp.sum(e, axis=-1, keepdims=True)
            outs.append(_dot(p.astype(BF16), mkvv[:, D + h * X_HD:D + (h + 1) * X_HD]))
        return jnp.concatenate(outs, axis=1)

    xo = rowwise(xattn_fwd, [(xq, None)], [mkv], [(D, BF16)], [], tile=tile, name="xattn_fwd")[0]
    h2, n3 = rowwise(proj_norm, [(xo, None), (h1, None)], [wxo, g_ffn], [(D, F32), (D, BF16)], [], tile=tile4,
                     name="mm_xo_norm")
    up_pre = matmul(n3, wup, out_dtype=BF16, name="mm_up", tm=2048, tn=1408)
    cw = D_FF // 2
    act = conv_act_fwd(up_pre, wcv, b_conv, tile=tile, cw=cw)

    def down_loss(av, hv2, tv, wv_, g):
        hv = hv2 + _dot(av, wv_)
        y = _rms(hv, g)
        err = y - tv
        part = 0.5 * jnp.sum(jnp.sum(err * err, axis=-1, keepdims=True) / D, axis=0, keepdims=True)
        dx, dg = _rms_bwd(hv, g, err / D)
        return dx, dg, jnp.broadcast_to(part, (8, 128))

    g_fin2 = g_final.reshape(1, D)
    dh3, dg_final, loss_acc = rowwise(down_loss, [(act, None), (h2, None), (tgt, None)], [wdn, g_fin2], [(D, F32)],
                                      [((1, D), F32), ((8, 128), F32)], tile=tile2, name="mm_down_loss")
    loss = lax.psum(loss_acc[0, 0], ("x", "y", "c"))

    dact = matmul(dh3, wdn, tb=True, out_dtype=BF16, name="mm_dact", tm=2048, tn=1408)
    dw_down = matmul_tn(act, dh3, name="mm_dw_down", tm=1408)
    dup_a, dup_b, cs_a, cs_b = conv_act_bwd(up_pre, dact, wcv, b_conv, tile=tile, cw=cw)
    dw_up_a = matmul_tn(n3, dup_a, name="mm_dw_up_a", tn=1408)
    dw_up_b = matmul_tn(n3, dup_b, name="mm_dw_up_b", tn=1408)

    def ffn_in_bwd(da_, db_, hv, drv, wv_, g):
        dn = _dot_nt(da_, wv_[:, :D_FF]) + _dot_nt(db_, wv_[:, D_FF:])
        dx, dg = _rms_bwd(hv, g, dn)
        return dx + drv, dg

    dh2, dg_ffn = rowwise(ffn_in_bwd, [(dup_a, None), (dup_b, None), (h2, None), (dh3, None)], [wup, g_ffn],
                          [(D, F32)], [((1, D), F32)], tile=tile2, name="mm_dn3_norm_bwd")
    dxo = matmul(dh2, wxo, tb=True, out_dtype=BF16, name="mm_dxo")
    dw_xo = matmul_tn(xo, dh2, name="mm_dw_xo")

    def xattn_bwd(xqv, dxov, mkvv):
        dxq, dmk, dmv = [], [], []
        for h in range(X_H):
            sl = slice(h * X_HD, (h + 1) * X_HD)
            slv = slice(D + h * X_HD, D + (h + 1) * X_HD)
            s = _dot_nt(xqv[:, sl], mkvv[:, sl]) * x_scale
            s = s - jnp.max(s, axis=-1, keepdims=True)
            e = jnp.exp(s)
            p = e / jnp.sum(e, axis=-1, keepdims=True)
            dp = _dot_nt(dxov[:, sl], mkvv[:, slv])
            ds = (p * (dp - jnp.sum(dp * p, axis=-1, keepdims=True)) * x_scale).astype(BF16)
            dxq.append(_dot(ds, mkvv[:, sl]))
            dmk.append(_dot_tn(ds, xqv[:, sl]))
            dmv.append(_dot_tn(p.astype(BF16), dxov[:, sl]))
        return jnp.concatenate(dxq, axis=1), jnp.concatenate(dmk + dmv, axis=1)

    dxq, dmkv = rowwise(xattn_bwd, [(xq, None), (dxo, None)], [mkv], [(D, BF16)], [((M, 2 * D), F32)],
                        tile=tile, name="xattn_bwd")
    dw_xq = matmul_tn(n2, dxq, name="mm_dw_xq")

    def proj_norm_bwd(dyv, hv, drv, wv_, g):
        dx, dg = _rms_bwd(hv, g, _dot_nt(dyv, wv_))
        return dx + drv, dg

    dh1, dg_cross = rowwise(proj_norm_bwd, [(dxq, None), (h1, None), (dh2, None)], [wxq, g_cross], [(D, F32)],
                            [((1, D), F32)], tile=tile4, name="mm_dn2_norm_bwd")
    dw_xkv = matmul_tn(mn, dmkv, name="mm_dw_xkv", tk=M)
    dmn = matmul(dmkv, wxkv, tb=True, name="mm_dmn", tm=M)
    dg_mem = rowwise(lambda mv_, dyv, g: _rms_bwd(mv_, g, dyv)[1], [(mems, None), (dmn, None)], [g_mem], [],
                     [((1, D), F32)], tile=min(tile, M), name="norm_mem_bwd")[0]

    dmix = matmul(dh1, wo, tb=True, out_dtype=BF16, name="mm_dmix")
    dw_out = matmul_tn(mix, dh1, name="mm_dw_out")

    def mix_bwd(ov, rv_, rg, gt, dmv_, wpav, wprv, gr, bg):
        dm_ = dmv_.astype(F32)
        ya = _dot(ov, wpav)
        parts = gn_parts(rv_)
        xh = jnp.concatenate([p[0] for p in parts], axis=1)
        yn = xh * gr
        sg = _sigmoid(rg)
        sl_ = rg * sg
        t = (sl_ * yn).astype(BF16)
        yr = _dot(t, wprv)
        ga_ = _sigmoid(gt[:, :D] + bg[:, :D])
        gr_ = _sigmoid(gt[:, D:] + bg[:, D:])
        dgates = jnp.concatenate([dm_ * ya * ga_ * (1.0 - ga_), dm_ * yr * gr_ * (1.0 - gr_)], axis=1)
        dya = (dm_ * ga_).astype(BF16)
        dyr = (dm_ * gr_).astype(BF16)
        do_ = _dot_nt(dya, wpav)
        dwpa_ = _dot_tn(ov, dya)
        dt = _dot_nt(dyr, wprv)
        dwpr_ = _dot_tn(t, dyr)
        drg_ = dt * yn * (sg * (1.0 + rg * (1.0 - sg)))
        dyn = dt * sl_
        dgr = jnp.sum(dyn * xh, axis=0, keepdims=True)
        dxh = dyn * gr
        drets = []
        for h in range(RET_H):
            sl = slice(h * RET_D, (h + 1) * RET_D)
            xhh, rstd = parts[h]
            dxhh = dxh[:, sl]
            drets.append(rstd * (dxhh - jnp.mean(dxhh, axis=-1, keepdims=True)
                                 - xhh * jnp.mean(dxhh * xhh, axis=-1, keepdims=True)))
        dret_ = jnp.concatenate(drets, axis=1)
        dbg = jnp.sum(dgates, axis=0, keepdims=True)
        prod = ov.astype(F32) * do_.astype(BF16).astype(F32)
        lane = lax.broadcasted_iota(jnp.int32, (prod.shape[0], HP), 1)
        dlt = jnp.zeros((prod.shape[0], HP), F32)
        for h in range(MLA_H):
            dlt = jnp.where(lane == h, jnp.sum(prod[:, h * HP:(h + 1) * HP], axis=-1, keepdims=True), dlt)
        return do_, dret_, drg_, dgates, dlt, dwpa_, dwpr_, dgr, dbg

    do_a, dret, drg, dgates, delta, dwpa, dw_proj_ret, dg_ret, db_gate = rowwise(
        mix_bwd, [(o_a, None), (ret, None), (z, (512, 3)), (z, (2048, 1)), (dmix, None)], [wpa, wpr, g_ret, b_gate],
        [(MLA_H * HP, BF16), (512, F32), (512, BF16), (2 * D, BF16), (HP, F32)],
        [((MLA_H * HP, D), F32), ((512, D), F32), ((1, 512), F32), ((1, 2 * D), F32)], tile=tile, name="mix_bwd")

    sl1 = slots_l1(dw_up_a, dw_up_b, dw_xkv, tile=tile)
    sl2, sl3 = slots_l23(dw_out, dw_xq, dw_xo, dw_down, dwpa, dw_proj_ret)
    delta_row = mla_prep(delta, tq=tq)
    dq_a, dk_a, dv_a, rl1, rl2, rl3 = mla_bwd(q_a, k_a, v_a, do_a, lse_row, delta_row, [sl1, sl2, sl3], tq=tq)
    drq_r, drk_r, drv = ret_bwd(rq, rk, rv, rstate, dret, rconsts, rb=rb)

    def mixers_in_bwd(zl, cmv, s1v, s2v, dqv, dkv_, dvv, drq_, drk_, drv_, drg_, dgt, crv, srv, gq, gkv, wqv, wkv_, wvv):
        cqf, ckvf = zl[:, 0:256], zl[:, 256:384]
        cq = _rms(cqf, gq).astype(BF16)
        ckv = _rms(ckvf, gkv).astype(BF16)
        dq_pre = _rot_mla(dqv.astype(F32), cmv, -s1v, -s2v).astype(BF16)
        dkf = dkv_.astype(F32)
        dkr = dkf[:, 0:HP]
        for h in range(1, MLA_H):
            dkr = dkr + dkf[:, h * HP:(h + 1) * HP]
        lane = lax.broadcasted_iota(jnp.int32, dkr.shape, 1)
        dzk = _rot_mla(jnp.where((lane >= 64) & (lane < 96), dkr, 0.0), cmv, -s1v, -s2v)
        dkb = dkv_.astype(BF16)
        dvb = dvv.astype(BF16)
        dcq_n = _dot_nt(dq_pre, wqv)
        dckv_n = _dot_nt(dkb, wkv_) + _dot_nt(dvb, wvv)
        dwq_ = _dot_tn(cq, dq_pre)
        dwk_ = _dot_tn(ckv, dkb)
        dwv_ = _dot_tn(ckv, dvb)
        dcq, dgq = _rms_bwd(cqf, gq, dcq_n)
        dckv, dgkv = _rms_bwd(ckvf, gkv, dckv_n)
        a = _rot_ret(drq_, crv, -srv)
        b = _rot_ret(drk_, crv, -srv) * (RET_D ** -0.5)
        dz_ = jnp.concatenate([a.astype(BF16), b.astype(BF16), drv_, drg_, dgt,
                               dcq.astype(BF16), dckv.astype(BF16), dzk.astype(BF16)], axis=1)
        return dz_, dwq_, dwk_, dwv_, dgq, dgkv

    dz, dwq, dwk, dwv, dg_q_lat, dg_kv_lat = rowwise(
        mixers_in_bwd, [(z, (512, 8)), (cm, None), (s1, None), (s2, None), (dq_a, None), (dk_a, None), (dv_a, None),
                        (drq_r, None), (drk_r, None), (drv, None), (drg, None), (dgates, None), (cr, None), (sr, None)],
        [g_q_lat, g_kv_lat, wq, wk, wv], [(ZW, BF16)],
        [((MLA_QR, MLA_H * HP), F32), ((MLA_KVR, MLA_H * HP), F32), ((MLA_KVR, MLA_H * HP), F32),
         ((1, MLA_QR), F32), ((1, MLA_KVR), F32)], tile=tile, name="mixers_in_bwd")
    dwz = matmul_tn(u, dz, name="mm_dw_z", tn=1536)
    se1, se2 = slots_early(dwz, dwk, dwv, dwq, tile=tile)
    grad_x, dg_mix, re1, re2 = rowwise(proj_norm_bwd, [(dz, None), (xs, None), (dh1, None)], [wz, g_mix], [(D, F32)],
                                       [((1, D), F32)], tile=tile2, name="mm_du_norm_bwd", exchange=[se1, se2])

    small_grads = {"g_mix": dg_mix, "b_gate": db_gate, "g_q_lat": dg_q_lat, "g_kv_lat": dg_kv_lat, "g_ret": dg_ret,
                   "g_cross": dg_cross, "g_mem": dg_mem, "g_ffn": dg_ffn, "g_final": dg_final}
    rd, rs, dslots, own_s = exchange_small(cs_a, cs_b, [small_grads[n] for n, _, _ in SMALL_DIRECT])

    me = _my_id()

    def own(slots):
        return lax.dynamic_index_in_dim(slots, me, axis=0, keepdims=False)

    def wmv(names):
        return [shard(n, p) for n in names for p in ("", "m_", "v_")]

    names_s = tuple(n for n, _, _ in SMALL)
    groups = (
        (("w_in",), adam_cols(own(se1), re1, wmv(("w_in",)), ((0, NS_IN),), name="adam_e1", tile=128)),
        (("w_ukv", "w_uq"), adam_rows(own(se2), re2, wmv(("w_ukv", "w_uq")),
                                      ((0, E2_UQ, HP), (E2_UQ, E2_ROWS, NS_UQ)), name="adam_e2")),
        (("w_up", "w_xkv"), adam_cols(own(sl1), rl1, wmv(("w_up", "w_xkv")), ((0, NS_UP), (NS_UP, L1_W)),
                                      name="adam_l1", tile=128)),
        (("w_out", "w_xq", "w_xo", "w_down"),
         adam_rows(own(sl2), rl2, wmv(("w_out", "w_xq", "w_xo", "w_down")),
                   ((L2_OUT, L2_XQ, D), (L2_XQ, L2_XO, D), (L2_XO, L2_DN, D), (L2_DN, L2_ROWS, D)), name="adam_l2")),
        (("w_proj_mla", "w_proj_ret"), adam_rows(own(sl3), rl3, wmv(("w_proj_mla", "w_proj_ret")),
                                                 ((0, L3_PRET, HP), (L3_PRET, L3_ROWS, HP)), name="adam_l3")),
        (names_s + ("w_conv",), adam_small(own_s, rs, dslots, rd, wmv(names_s), wmv(("w_conv",)))),
    )
    res = {}
    for names, outs_ in groups:
        for t, n in enumerate(names):
            res[n] = outs_[4 * t:4 * t + 4]

    order = ["g_mix", "w_in", "b_gate", "g_q_lat", "w_uq", "g_kv_lat", "w_ukv", "w_proj_mla", "g_ret", "w_proj_ret",
             "w_out", "g_cross", "g_mem", "w_xq", "w_xkv", "w_xo", "g_ffn", "w_up", "w_conv", "b_conv", "w_down",
             "g_final"]
    outs = [loss, grad_x[None]]
    for kind in range(4):
        outs += [res[n][kind].reshape(args[n].shape) for n in order]
    return tuple(outs)
```

```python
import functools
import math

import jax
import jax.numpy as jnp
import numpy as np
from jax import lax
from jax.experimental import pallas as pl
from jax.experimental.pallas import tpu as pltpu

F32 = jnp.float32
BF16 = jnp.bfloat16

D = 1024
MLA_H, MLA_NOPE, MLA_ROPE, MLA_V = 8, 64, 32, 64
MLA_QR, MLA_KVR = 256, 128
RET_H, RET_D, RET_C = 4, 128, 128
X_H, X_HD = 4, 256
D_FF = 2816
THETA = 10000.0
EPS = 1e-6
HP = 128
ZW = 4608
N_DEV = 8

ADAM_LR, ADAM_B1, ADAM_B2, ADAM_EPS, ADAM_WD, ADAM_STEP = 0.001, 0.9, 0.999, 1e-08, 0.01, 10

VMEM_LIMIT = 56 * 1024 * 1024
BIG_CONST_BYTES = 4 * 1024 * 1024
MESH = pl.DeviceIdType.MESH
VM = pl.BlockSpec(memory_space=pltpu.VMEM)
ANY = pl.BlockSpec(memory_space=pl.ANY)


def _cp(n_axes):
    return pltpu.CompilerParams(dimension_semantics=("arbitrary",) * n_axes, vmem_limit_bytes=VMEM_LIMIT)


def _cp0():
    return pltpu.CompilerParams(vmem_limit_bytes=VMEM_LIMIT)


def _pick(n, cap, mult=128):
    best = None
    for t in range(mult, min(n, cap) + 1, mult):
        if n % t == 0:
            best = t
    return best if best is not None else n


def _dot(a, b):
    return jnp.dot(a, b, preferred_element_type=F32)


def _dot_nt(a, b):
    return lax.dot_general(a, b, (((1,), (1,)), ((), ())), preferred_element_type=F32)


def _dot_tn(a, b):
    return lax.dot_general(a, b, (((0,), (0,)), ((), ())), preferred_element_type=F32)


def _sds(shape, dtype):
    return jax.ShapeDtypeStruct(shape, dtype)


def matmul(a, b, *, name, tb=False, out_dtype=F32, tm=1024, tn=1024):
    M, K = a.shape
    N = b.shape[0] if tb else b.shape[1]
    tm = _pick(M, tm, 8)
    tn = _pick(N, tn)

    def body(a_ref, b_ref, o_ref):
        av = a_ref[...].astype(BF16)
        bv = b_ref[...].astype(BF16)
        acc = _dot_nt(av, bv) if tb else _dot(av, bv)
        o_ref[...] = acc.astype(o_ref.dtype)

    return pl.pallas_call(
        body,
        name=name,
        grid=(M // tm, N // tn),
        in_specs=[pl.BlockSpec((tm, K), lambda i, j: (i, 0)),
                  pl.BlockSpec((tn, K), lambda i, j: (j, 0)) if tb else pl.BlockSpec((K, tn), lambda i, j: (0, j))],
        out_specs=pl.BlockSpec((tm, tn), lambda i, j: (i, j)),
        out_shape=_sds((M, N), out_dtype),
        compiler_params=_cp(2),
    )(a, b)


def matmul_tn(a, b, *, name, tm=1024, tn=1024, tk=2048):
    R, M = a.shape
    N = b.shape[1]
    tm = _pick(M, tm)
    tn = _pick(N, tn)
    tk = _pick(R, tk, 16)
    nk = R // tk

    def body(a_ref, b_ref, o_ref, acc_ref):
        k = pl.program_id(2)

        @pl.when(k == 0)
        def _():
            acc_ref[...] = jnp.zeros_like(acc_ref)

        acc_ref[...] += _dot_tn(a_ref[...].astype(BF16), b_ref[...].astype(BF16))

        @pl.when(k == nk - 1)
        def _():
            o_ref[...] = acc_ref[...]

    return pl.pallas_call(
        body,
        name=name,
        grid=(M // tm, N // tn, nk),
        in_specs=[pl.BlockSpec((tk, tm), lambda i, j, k: (k, i)), pl.BlockSpec((tk, tn), lambda i, j, k: (k, j))],
        out_specs=pl.BlockSpec((tm, tn), lambda i, j, k: (i, j)),
        out_shape=_sds((M, N), F32),
        scratch_shapes=[pltpu.VMEM((tm, tn), F32)],
        compiler_params=_cp(3),
    )(a, b)


def rowwise(fn, rows, consts, out_rows, out_accs, *, tile, name, gather=(), exchange=()):
    T = rows[0][0].shape[0]
    nt = T // tile
    travel = tuple(gather) + tuple(exchange)
    n_r, n_c, n_o, n_a, n_g = len(rows), len(consts), len(out_rows), len(out_accs), len(travel)
    n_in = n_r + n_c + n_g

    def body(*refs):
        if n_g:
            srcs, dsts = refs[n_r + n_c:n_in], refs[n_in + n_o + n_a:n_in + n_o + n_a + n_g]
            sems = tuple(refs[n_in + n_o + n_a + n_g:])
            if gather:
                comm = (srcs, dsts) + sems
                comm_start, comm_finish = _gather2_start, _gather2_finish
            else:
                comm = ([(r, True) for r in srcs], dsts) + sems
                comm_start, comm_finish = _exchange_start, _exchange_wait

            @pl.when(pl.program_id(0) == 0)
            def _():
                comm_start(*comm)

        ins = [r[...] for r in refs[: n_r + n_c]]
        outs = fn(*ins)
        if not isinstance(outs, (tuple, list)):
            outs = (outs,)
        o_refs = refs[n_in:n_in + n_o]
        a_refs = refs[n_in + n_o:n_in + n_o + n_a]
        for o_ref, o in zip(o_refs, outs[:n_o]):
            o_ref[...] = o.astype(o_ref.dtype)
        if n_a:
            first = pl.program_id(0) == 0

            @pl.when(first)
            def _():
                for a_ref, o in zip(a_refs, outs[n_o:]):
                    a_ref[...] = o.astype(a_ref.dtype)

            @pl.when(jnp.logical_not(first))
            def _():
                for a_ref, o in zip(a_refs, outs[n_o:]):
                    a_ref[...] += o.astype(a_ref.dtype)
        if n_g:

            @pl.when(pl.program_id(0) == nt - 1)
            def _():
                comm_finish(*comm)

    in_specs = []
    args = []
    for arr, win in rows:
        if win is None:
            in_specs.append(pl.BlockSpec((tile, arr.shape[1]), lambda i: (i, 0)))
        else:
            w, cb = win
            in_specs.append(pl.BlockSpec((tile, w), functools.partial(lambda i, cb: (i, cb), cb=cb)))
        args.append(arr)
    for c in consts:
        index_map = functools.partial(lambda i, nd: (0,) * nd, nd=c.ndim)
        if c.size * c.dtype.itemsize >= BIG_CONST_BYTES:
            in_specs.append(pl.BlockSpec(c.shape, index_map, pipeline_mode=pl.Buffered(1)))
        else:
            in_specs.append(pl.BlockSpec(c.shape, index_map))
        args.append(c)
    out_specs = [pl.BlockSpec((tile, w), lambda i: (i, 0)) for w, _ in out_rows]
    out_shape = [_sds((T, w), dt) for w, dt in out_rows]
    for shp, dt in out_accs:
        out_specs.append(pl.BlockSpec(shp, functools.partial(lambda i, nd: (0,) * nd, nd=len(shp))))
        out_shape.append(_sds(shp, dt))
    return pl.pallas_call(
        body,
        name=name,
        grid=(nt,),
        in_specs=in_specs + [ANY] * n_g,
        out_specs=out_specs + [ANY] * n_g,
        out_shape=out_shape + [_sds((N_DEV,) + p.shape, p.dtype) for p in gather]
        + [_sds((N_DEV - 1,) + s.shape[1:], s.dtype) for s in exchange],
        scratch_shapes=(_gather_scratch(n_g) if gather else _exchange_scratch(n_g)) if n_g else [],
        compiler_params=_cp(1),
    )(*args, *travel)


def _rms(x, g):
    r = lax.rsqrt(jnp.mean(x * x, axis=-1, keepdims=True) + EPS)
    return x * r * g


def _rms_bwd(x, g, dy):
    r = lax.rsqrt(jnp.mean(x * x, axis=-1, keepdims=True) + EPS)
    xh = x * r
    dg = jnp.sum(dy * xh, axis=0, keepdims=True)
    dxh = dy * g
    dx = r * (dxh - xh * jnp.mean(dxh * xh, axis=-1, keepdims=True))
    return dx, dg


def _sigmoid(x):
    return 0.5 * jnp.tanh(0.5 * x) + 0.5


def _rot_mla(x, c, s1, s2):
    n = x.shape[1] // HP
    outs = []
    for h in range(n):
        xh = x[:, h * HP : (h + 1) * HP]
        outs.append(xh * c + pltpu.roll(xh, HP - 16, 1) * s1 + pltpu.roll(xh, 16, 1) * s2)
    return outs[0] if n == 1 else jnp.concatenate(outs, axis=1)


def _rot_ret(x, c, s):
    n = x.shape[1] // RET_D
    outs = []
    for h in range(n):
        xh = x[:, h * RET_D : (h + 1) * RET_D]
        outs.append(xh * c + pltpu.roll(xh, RET_D // 2, 1) * s)
    return outs[0] if n == 1 else jnp.concatenate(outs, axis=1)


def _rot_inv():
    lane_np = np.arange(128)
    inv_m = (jnp.asarray(THETA, F32) ** (-jnp.asarray(lane_np & 15, F32) / 16.0)).reshape(1, 128)
    inv_r = (jnp.asarray(THETA, F32) ** (-jnp.asarray(lane_np & 63, F32) / 64.0)).reshape(1, 128)
    return inv_m, inv_r


def _rot_tables(p, im, ir):
    lane = lax.broadcasted_iota(jnp.int32, p.shape, 1)
    ang = p * im
    cm = jnp.where((lane >= 64) & (lane < 96), jnp.cos(ang), 1.0)
    sn = jnp.sin(ang)
    s1 = jnp.where((lane >= 64) & (lane < 80), -sn, 0.0)
    s2 = jnp.where((lane >= 80) & (lane < 96), sn, 0.0)
    angr = p * ir
    snr = jnp.sin(angr)
    return cm, s1, s2, jnp.cos(angr), jnp.where(lane < 64, -snr, snr)


def _peer(m):
    x, y, c = lax.axis_index("x"), lax.axis_index("y"), lax.axis_index("c")
    mx, my, mc = (m >> 2) & 1, (m >> 1) & 1, m & 1
    px = 1 - x if mx else x
    py = 1 - y if my else y
    pc = 1 - c if mc else c
    return (px, py, pc), 4 * px + 2 * py + pc


def _my_id():
    return 4 * lax.axis_index("x") + 2 * lax.axis_index("y") + lax.axis_index("c")


def _gather_copies(srcs, outs, send_sems, recv_sems, local_sems, arriving=False):
    me = _my_id()
    copies = []
    if not arriving:
        for g, (src, out) in enumerate(zip(srcs, outs)):
            copies.append((pltpu.make_async_copy(src, out.at[me], local_sems.at[g]), False))
    for m in range(1, N_DEV):
        peer, plin = _peer(m)
        for g, (src, out) in enumerate(zip(srcs, outs)):
            copies.append((pltpu.make_async_remote_copy(
                src_ref=src, dst_ref=out.at[plin if arriving else me], send_sem=send_sems.at[g, m - 1],
                recv_sem=recv_sems.at[g, m - 1], device_id=peer, device_id_type=MESH), True))
    return copies


def _gather_start(*a):
    for cp, _ in _gather_copies(*a):
        cp.start()


def _gather_wait(*a):
    for cp, _ in _gather_copies(*a, arriving=True):
        cp.wait_recv()
    for cp, remote in _gather_copies(*a):
        if remote:
            cp.wait_send()
        else:
            cp.wait()


CHIP_RELATIONS = (4, 2, 6)


def _gather2_copy(src, out, block, to, send_sems, recv_sems, g, k):
    return pltpu.make_async_remote_copy(src_ref=src, dst_ref=out.at[block], send_sem=send_sems.at[g, k],
                                        recv_sem=recv_sems.at[g, k], device_id=to, device_id_type=MESH)


def _gather2_start(srcs, outs, send_sems, recv_sems, local_sems):
    me = _my_id()
    sib, _ = _peer(1)
    for g, (src, out) in enumerate(zip(srcs, outs)):
        pltpu.make_async_copy(src, out.at[me], local_sems.at[g]).start()
        _gather2_copy(src, out, me, sib, send_sems, recv_sems, g, 0).start()
        for t, m in enumerate(CHIP_RELATIONS):
            _gather2_copy(src, out, me, _peer(m)[0], send_sems, recv_sems, g, 1 + t).start()


def _gather2_finish(srcs, outs, send_sems, recv_sems, local_sems):
    me = _my_id()
    sib, sib_lin = _peer(1)
    for t, m in enumerate(CHIP_RELATIONS):
        peer, plin = _peer(m)
        for g, (src, out) in enumerate(zip(srcs, outs)):
            _gather2_copy(src, out, plin, peer, send_sems, recv_sems, g, 1 + t).wait_recv()
            _gather2_copy(out.at[plin], out, plin, sib, send_sems, recv_sems, g, 4 + t).start()
    for g, (src, out) in enumerate(zip(srcs, outs)):
        _gather2_copy(src, out, sib_lin, sib, send_sems, recv_sems, g, 0).wait_recv()
        for t, m in enumerate(CHIP_RELATIONS):
            _gather2_copy(src, out, _peer(m | 1)[1], sib, send_sems, recv_sems, g, 4 + t).wait_recv()
    for g, (src, out) in enumerate(zip(srcs, outs)):
        _gather2_copy(src, out, me, sib, send_sems, recv_sems, g, 0).wait_send()
        for t, m in enumerate(CHIP_RELATIONS):
            peer, plin = _peer(m)
            _gather2_copy(src, out, me, peer, send_sems, recv_sems, g, 1 + t).wait_send()
            _gather2_copy(out.at[plin], out, plin, sib, send_sems, recv_sems, g, 4 + t).wait_send()
        pltpu.make_async_copy(src, out.at[me], local_sems.at[g]).wait()


def _gather_scratch(n):
    return [pltpu.SemaphoreType.DMA((n, N_DEV - 1)), pltpu.SemaphoreType.DMA((n, N_DEV - 1)), pltpu.SemaphoreType.DMA((n,))]


def _exchange_copies(srcs, dsts, send_sems, recv_sems):
    copies = []
    for m in range(1, N_DEV):
        peer, plin = _peer(m)
        for g, ((src, per_peer), dst) in enumerate(zip(srcs, dsts)):
            copies.append(pltpu.make_async_remote_copy(
                src_ref=src.at[plin] if per_peer else src, dst_ref=dst.at[m - 1], send_sem=send_sems.at[g, m - 1],
                recv_sem=recv_sems.at[g, m - 1], device_id=peer, device_id_type=MESH))
    return copies


def _exchange_start(*a):
    for cp in _exchange_copies(*a):
        cp.start()


def _exchange_wait(*a):
    copies = _exchange_copies(*a)
    for cp in copies:
        cp.wait_recv()
    for cp in copies:
        cp.wait_send()


def _exchange_scratch(n):
    return [pltpu.SemaphoreType.DMA((n, N_DEV - 1)), pltpu.SemaphoreType.DMA((n, N_DEV - 1))]


MLA_SCALE = (MLA_NOPE + MLA_ROPE) ** -0.5
MLA_C2 = MLA_SCALE * math.log2(math.e)
ONE_LANE = MLA_V
NEG = -1e30
HPS = 4
HPB = 4


def _tri_mask(n, lower_rows_ge_cols=True):
    r = lax.broadcasted_iota(jnp.int32, (n, n), 0)
    c = lax.broadcasted_iota(jnp.int32, (n, n), 1)
    return r >= c if lower_rows_ge_cols else c >= r


def mla_fwd(q, k, v, gather, *, tq):
    T = q.shape[0]
    nq = T // tq
    rep = tq // HP
    ng = len(gather)

    def body(*refs):
        q_ref, k_ref, v_ref = refs[:3]
        srcs = refs[3:3 + ng]
        o_ref, lse_ref = refs[3 + ng:5 + ng]
        outs = refs[5 + ng:5 + 2 * ng]
        m_sc, acc_sc = refs[5 + 2 * ng:7 + 2 * ng]
        comm = (srcs, outs) + tuple(refs[7 + 2 * ng:])
        h, i = pl.program_id(0), pl.program_id(1)

        @pl.when((h == 0) & (i == 0))
        def _():
            _gather_start(*comm)

        m_sc[...] = jnp.full(m_sc.shape, NEG, F32)
        acc_sc[...] = jnp.zeros(acc_sc.shape, F32)
        heads = [slice(t * HP, (t + 1) * HP) for t in range(HPS)]

        def block(j, masked):
            off = pl.multiple_of(j * tq, tq)
            ss = [_dot_nt(q_ref[:, hd], k_ref[pl.ds(off, tq), hd]) * MLA_C2 for hd in heads]
            mask = _tri_mask(tq) if masked else None
            for hd, s in zip(heads, ss):
                if masked:
                    s = jnp.where(mask, s, NEG)
                m_prev = m_sc[:, hd]
                m_next = jnp.maximum(m_prev, jnp.max(s, axis=-1, keepdims=True))
                p = jnp.exp2(s - jnp.tile(m_next, (1, rep)))
                alpha = jnp.exp2(m_prev - m_next)
                acc_sc[:, hd] = alpha * acc_sc[:, hd] + _dot(p.astype(BF16), v_ref[pl.ds(off, tq), hd])
                m_sc[:, hd] = m_next

        def loop_body(j, carry):
            block(j, False)
            return carry

        lax.fori_loop(0, i, loop_body, 0)
        block(i, True)
        for hd in heads:
            acc = acc_sc[:, hd]
            l = acc[:, ONE_LANE:ONE_LANE + 1]
            o_ref[:, hd] = (acc / l).astype(o_ref.dtype)
            lse_ref[hd.start // HP] = (m_sc[:, hd] + jnp.log(l) * math.log2(math.e)).T[0:1, :]

        @pl.when((h == MLA_H // HPS - 1) & (i == nq - 1))
        def _():
            _gather_wait(*comm)

    blk = pl.BlockSpec((tq, HPS * HP), lambda h, i: (i, h))
    full = pl.BlockSpec((T, HPS * HP), lambda h, i: (0, h))
    return pl.pallas_call(
        body,
        name="mla_fwd",
        grid=(MLA_H // HPS, nq),
        in_specs=[blk, full, full] + [ANY] * ng,
        out_specs=[blk, pl.BlockSpec((HPS, None, 1, tq), lambda h, i: (h, i, 0, 0))] + [ANY] * ng,
        out_shape=[_sds((T, MLA_H * HP), BF16), _sds((MLA_H, nq, 1, tq), F32)]
        + [_sds((N_DEV,) + p.shape, p.dtype) for p in gather],
        scratch_shapes=[pltpu.VMEM((tq, HPS * HP), F32), pltpu.VMEM((tq, HPS * HP), F32)] + _gather_scratch(ng),
        compiler_params=_cp(2),
    )(q, k, v, *gather)


def mla_prep(delta, *, tq):
    T = delta.shape[0]
    nq = T // tq

    def body(d_ref, row_ref):
        dt = d_ref[...].T
        for h in range(MLA_H):
            row_ref[h] = dt[h:h + 1, :]

    return pl.pallas_call(
        body,
        name="mla_prep",
        grid=(nq,),
        in_specs=[pl.BlockSpec((tq, HP), lambda i: (i, 0))],
        out_specs=pl.BlockSpec((MLA_H, None, 1, tq), lambda i: (0, i, 0, 0)),
        out_shape=_sds((MLA_H, nq, 1, tq), F32),
        compiler_params=_cp(1),
    )(delta)


def mla_bwd(q, k, v, do, lse_row, delta_row, slots, *, tq):
    T = q.shape[0]
    nq = T // tq
    ns = len(slots)

    def body(*refs):
        q_ref, k_ref, v_ref, do_ref, lse_ref, delta_ref = refs[:6]
        srcs = [(r, True) for r in refs[6:6 + ns]]
        dq_ref, dk_ref, dv_ref = refs[6 + ns:9 + ns]
        dsts = refs[9 + ns:9 + 2 * ns]
        dk_sc, dv_sc = refs[9 + 2 * ns:11 + 2 * ns]
        comm = (srcs, dsts) + tuple(refs[11 + 2 * ns:])
        h, j = pl.program_id(0), pl.program_id(1)

        @pl.when((h == 0) & (j == 0))
        def _():
            _exchange_start(*comm)

        dk_sc[...] = jnp.zeros(dk_sc.shape, F32)
        dv_sc[...] = jnp.zeros(dv_sc.shape, F32)
        heads = [slice(t * HP, (t + 1) * HP) for t in range(HPB)]

        @pl.when(j == 0)
        def _():
            dq_ref[...] = jnp.zeros(dq_ref.shape, F32)

        def block(i, masked):
            off = pl.multiple_of(i * tq, tq)
            sts = [_dot_nt(k_ref[:, hd], q_ref[pl.ds(off, tq), hd]) * MLA_C2 for hd in heads]
            dpts = [_dot_nt(v_ref[:, hd], do_ref[pl.ds(off, tq), hd]) for hd in heads]
            mask = _tri_mask(tq, False) if masked else None
            for t, hd in enumerate(heads):
                st = sts[t]
                if masked:
                    st = jnp.where(mask, st, NEG)
                pt = jnp.exp2(st - lse_ref[t, i])
                dv_sc[:, hd] += _dot(pt.astype(BF16), do_ref[pl.ds(off, tq), hd])
                dst = (pt * (dpts[t] - delta_ref[t, i]) * MLA_SCALE).astype(BF16)
                dk_sc[:, hd] += _dot(dst, q_ref[pl.ds(off, tq), hd])
                dq_ref[pl.ds(off, tq), hd] += _dot_tn(dst, k_ref[:, hd])

        block(j, True)

        def loop_body(i, carry):
            block(i, False)
            return carry

        lax.fori_loop(j + 1, nq, loop_body, 0)
        dk_ref[...] = dk_sc[...].astype(dk_ref.dtype)
        dv_ref[...] = dv_sc[...].astype(dv_ref.dtype)

        @pl.when((h == MLA_H // HPB - 1) & (j == nq - 1))
        def _():
            _exchange_wait(*comm)

    blk = pl.BlockSpec((tq, HPB * HP), lambda h, j: (j, h))
    full = pl.BlockSpec((T, HPB * HP), lambda h, j: (0, h), pipeline_mode=pl.Buffered(1))
    rows = pl.BlockSpec((HPB, nq, 1, tq), lambda h, j: (h, 0, 0, 0))
    return pl.pallas_call(
        body,
        name="mla_bwd",
        grid=(MLA_H // HPB, nq),
        in_specs=[full, blk, blk, full, rows, rows] + [ANY] * ns,
        out_specs=[full, blk, blk] + [ANY] * ns,
        out_shape=[_sds((T, MLA_H * HP), F32), _sds((T, MLA_H * HP), BF16), _sds((T, MLA_H * HP), BF16)]
        + [_sds((N_DEV - 1,) + s.shape[1:], s.dtype) for s in slots],
        scratch_shapes=[pltpu.VMEM((tq, HPB * HP), F32), pltpu.VMEM((tq, HPB * HP), F32)] + _exchange_scratch(ns),
        compiler_params=_cp(2),
    )(q, k, v, do, lse_row, delta_row, *slots)


def _ret_consts():
    h = jnp.arange(RET_H, dtype=F32)
    log_g = jnp.log1p(-jnp.exp2(-5.0 - h))
    idx = jnp.arange(RET_C, dtype=F32)
    rel = idx[:, None] - idx[None, :]
    dmask = jnp.where(rel >= 0, jnp.exp(log_g[:, None, None] * jnp.maximum(rel, 0.0)), 0.0)
    zeta = jnp.exp(log_g[:, None] * (RET_C - 1.0 - idx)[None, :])
    xi = jnp.exp(log_g[:, None] * (idx + 1.0)[None, :])
    decay = jnp.exp(log_g * RET_C)
    zb = jnp.broadcast_to(zeta[:, :, None], (RET_H, RET_C, RET_D))
    xb = jnp.broadcast_to(xi[:, :, None], (RET_H, RET_C, RET_D))
    db = jnp.broadcast_to(decay[:, None, None], (RET_H, RET_C, RET_D))
    return dmask.astype(F32), zb.astype(F32), xb.astype(F32), db.astype(F32)


def ret_fwd(rq, rk, rv, consts, *, rb):
    T = rq.shape[0]
    nb = T // rb
    ncb = rb // RET_C

    def body(q_ref, k_ref, v_ref, dm_ref, z_ref, x_ref, dc_ref, o_ref, st_ref, r_sc):
        @pl.when(pl.program_id(0) == 0)
        def _():
            r_sc[...] = jnp.zeros(r_sc.shape, F32)

        for c in range(ncb):
            sl = slice(c * RET_C, (c + 1) * RET_C)
            for h in range(RET_H):
                hd = slice(h * RET_D, (h + 1) * RET_D)
                q, k, v = q_ref[sl, hd], k_ref[sl, hd], v_ref[sl, hd]
                r = r_sc[h]
                rbf = r.astype(BF16)
                st_ref[sl, hd] = rbf
                s = _dot_nt(q, k) * dm_ref[h]
                inner = _dot(s.astype(BF16), v)
                cross = _dot((q.astype(F32) * x_ref[h]).astype(BF16), rbf)
                o_ref[sl, hd] = inner + cross
                kz = (k.astype(F32) * z_ref[h]).T.astype(BF16)
                r_sc[h] = r * dc_ref[h] + _dot(kz, v)

    blk = pl.BlockSpec((rb, RET_H * RET_D), lambda b: (b, 0))
    cst = pl.BlockSpec((RET_H, RET_C, RET_D), lambda b: (0, 0, 0))
    return pl.pallas_call(
        body,
        name="ret_fwd",
        grid=(nb,),
        in_specs=[blk, blk, blk, cst, cst, cst, cst],
        out_specs=[blk, blk],
        out_shape=[_sds((T, RET_H * RET_D), F32), _sds((T, RET_H * RET_D), BF16)],
        scratch_shapes=[pltpu.VMEM((RET_H, RET_D, RET_D), F32)],
        compiler_params=_cp(1),
    )(rq, rk, rv, *consts)


def ret_bwd(rq, rk, rv, st, dret, consts, *, rb):
    T = rq.shape[0]
    nb = T // rb
    ncb = rb // RET_C

    def body(q_ref, k_ref, v_ref, st_ref, do_ref, dm_ref, z_ref, x_ref, dc_ref, dq_ref, dk_ref, dv_ref, g_sc):
        @pl.when(pl.program_id(0) == 0)
        def _():
            g_sc[...] = jnp.zeros(g_sc.shape, F32)

        for c in reversed(range(ncb)):
            sl = slice(c * RET_C, (c + 1) * RET_C)
            for h in range(RET_H):
                hd = slice(h * RET_D, (h + 1) * RET_D)
                dm, zt, xi = dm_ref[h], z_ref[h], x_ref[h]
                q, k, v, rp = q_ref[sl, hd], k_ref[sl, hd], v_ref[sl, hd], st_ref[sl, hd]
                dob = do_ref[sl, hd].astype(BF16)
                qf, kf = q.astype(F32), k.astype(F32)
                gn = g_sc[h]
                gnb = gn.astype(BF16)
                s = _dot_nt(q, k) * dm
                ds = _dot_nt(dob, v) * dm
                dq = _dot(ds.astype(BF16), k) + _dot_nt(dob, rp) * xi
                dk = _dot(ds.T.astype(BF16), q) + _dot_nt(v, gnb) * zt
                dv = _dot(s.T.astype(BF16), dob) + _dot((kf * zt).astype(BF16), gnb)
                dq_ref[sl, hd] = dq.astype(dq_ref.dtype)
                dk_ref[sl, hd] = dk.astype(dk_ref.dtype)
                dv_ref[sl, hd] = dv.astype(dv_ref.dtype)
                g_sc[h] = _dot((qf * xi).T.astype(BF16), dob) + dc_ref[h] * gn

    blk = pl.BlockSpec((rb, RET_H * RET_D), lambda b: (nb - 1 - b, 0))
    cst = pl.BlockSpec((RET_H, RET_C, RET_D), lambda b: (0, 0, 0))
    return pl.pallas_call(
        body,
        name="ret_bwd",
        grid=(nb,),
        in_specs=[blk, blk, blk, blk, blk, cst, cst, cst, cst],
        out_specs=[blk, blk, blk],
        out_shape=[_sds((T, RET_H * RET_D), F32), _sds((T, RET_H * RET_D), F32), _sds((T, RET_H * RET_D), BF16)],
        scratch_shapes=[pltpu.VMEM((RET_H, RET_D, RET_D), F32)],
        compiler_params=_cp(1),
    )(rq, rk, rv, st, dret, *consts)


HALO = 16


def conv_act_fwd(up_pre, w_conv, b_conv, *, tile, cw):
    T = up_pre.shape[0]
    nt = T // tile
    ncol = D_FF // cw
    hb = tile // HALO

    def body(pa_ref, a_ref, pb_ref, b_ref, wa_ref, wb_ref, ba_ref, bb_ref, o_ref):
        i = pl.program_id(1)
        keep = (i > 0).astype(F32)

        def conv(prev_ref, cur_ref, w_ref, bias_ref):
            ext = jnp.concatenate([prev_ref[...].astype(F32) * keep, cur_ref[...].astype(F32)], axis=0)
            w = w_ref[...]
            y = ext * w[2:3, :] + pltpu.roll(ext, 1, 0) * w[1:2, :] + pltpu.roll(ext, 2, 0) * w[0:1, :] + bias_ref[...]
            return y[HALO:, :]

        a = conv(pa_ref, a_ref, wa_ref, ba_ref)
        b = conv(pb_ref, b_ref, wb_ref, bb_ref)
        o_ref[...] = (a * _sigmoid(a) * b).astype(o_ref.dtype)

    prev_a = pl.BlockSpec((HALO, cw), lambda j, i: (jnp.maximum(i * hb - 1, 0), j))
    cur_a = pl.BlockSpec((tile, cw), lambda j, i: (i, j))
    prev_b = pl.BlockSpec((HALO, cw), lambda j, i: (jnp.maximum(i * hb - 1, 0), j + ncol))
    cur_b = pl.BlockSpec((tile, cw), lambda j, i: (i, j + ncol))
    w_a = pl.BlockSpec((3, cw), lambda j, i: (0, j))
    w_b = pl.BlockSpec((3, cw), lambda j, i: (0, j + ncol))
    bias_a = pl.BlockSpec((1, cw), lambda j, i: (0, j))
    bias_b = pl.BlockSpec((1, cw), lambda j, i: (0, j + ncol))
    return pl.pallas_call(
        body,
        name="conv_act_fwd",
        grid=(ncol, nt),
        in_specs=[prev_a, cur_a, prev_b, cur_b, w_a, w_b, bias_a, bias_b],
        out_specs=pl.BlockSpec((tile, cw), lambda j, i: (i, j)),
        out_shape=_sds((T, D_FF), BF16),
        compiler_params=_cp(2),
    )(up_pre, up_pre, up_pre, up_pre, w_conv, w_conv, b_conv, b_conv)


def conv_act_bwd(up_pre, dact, w_conv, b_conv, *, tile, cw):
    T = up_pre.shape[0]
    nt = T // tile
    ncol = D_FF // cw
    hb = tile // HALO
    ext_rows = tile + 2 * HALO

    def body(pa_ref, a_ref, na_ref, pb_ref, b_ref, nb_ref, d_ref, nd_ref, wa_ref, wb_ref, ba_ref, bb_ref,
             dxa_ref, dxb_ref, sa_ref, sb_ref):
        i = pl.program_id(1)
        keep_p = (i > 0).astype(F32)
        keep_n = (i < nt - 1).astype(F32)

        def ext_of(prev_ref, cur_ref, next_ref):
            return jnp.concatenate(
                [prev_ref[...].astype(F32) * keep_p, cur_ref[...].astype(F32), next_ref[...].astype(F32) * keep_n], axis=0)

        def taps(ext):
            return ext, pltpu.roll(ext, 1, 0), pltpu.roll(ext, 2, 0)

        def conv(tp, w, bias):
            return tp[0] * w[2:3, :] + tp[1] * w[1:2, :] + tp[2] * w[0:1, :] + bias

        xa = taps(ext_of(pa_ref, a_ref, na_ref))
        xb = taps(ext_of(pb_ref, b_ref, nb_ref))
        wa, wb = wa_ref[...], wb_ref[...]
        a = conv(xa, wa, ba_ref[...])
        b = conv(xb, wb, bb_ref[...])
        dy = jnp.concatenate(
            [jnp.zeros((HALO, cw), F32), d_ref[...].astype(F32), nd_ref[...].astype(F32) * keep_n], axis=0)
        sg = _sigmoid(a)
        da = dy * b * (sg * (1.0 + a * (1.0 - sg)))
        db = dy * (a * sg)

        def back(dup, tp, w, dx_ref, s_ref):
            dx = dup * w[2:3, :] + pltpu.roll(dup, ext_rows - 1, 0) * w[1:2, :] + pltpu.roll(dup, ext_rows - 2, 0) * w[0:1, :]
            dx_ref[...] = dx[HALO:HALO + tile, :].astype(dx_ref.dtype)
            dc = dup[HALO:HALO + tile, :]
            r2 = jnp.sum(dc * tp[0][HALO:HALO + tile, :], axis=0, keepdims=True)
            r1 = jnp.sum(dc * tp[1][HALO:HALO + tile, :], axis=0, keepdims=True)
            r0 = jnp.sum(dc * tp[2][HALO:HALO + tile, :], axis=0, keepdims=True)
            rb = jnp.sum(dc, axis=0, keepdims=True)
            row = lax.broadcasted_iota(jnp.int32, (8, cw), 0)
            upd = (jnp.where(row == 0, r0, 0.0) + jnp.where(row == 1, r1, 0.0) + jnp.where(row == 2, r2, 0.0)
                   + jnp.where(row == 3, rb, 0.0))

            @pl.when(i == 0)
            def _():
                s_ref[...] = upd

            @pl.when(i > 0)
            def _():
                s_ref[...] += upd

        back(da, xa, wa, dxa_ref, sa_ref)
        back(db, xb, wb, dxb_ref, sb_ref)

    def prev_of(shift):
        return pl.BlockSpec((HALO, cw), lambda j, i: (jnp.maximum(i * hb - 1, 0), j + shift))

    def next_of(shift):
        return pl.BlockSpec((HALO, cw), lambda j, i: (jnp.minimum((i + 1) * hb, nt * hb - 1), j + shift))

    def cur_of(shift):
        return pl.BlockSpec((tile, cw), lambda j, i: (i, j + shift))

    def row_of(rows, shift):
        return pl.BlockSpec((rows, cw), lambda j, i: (0, j + shift))

    return pl.pallas_call(
        body,
        name="conv_act_bwd",
        grid=(ncol, nt),
        in_specs=[prev_of(0), cur_of(0), next_of(0), prev_of(ncol), cur_of(ncol), next_of(ncol), cur_of(0), next_of(0),
                  row_of(3, 0), row_of(3, ncol), row_of(1, 0), row_of(1, ncol)],
        out_specs=[cur_of(0), cur_of(0), row_of(8, 0), row_of(8, 0)],
        out_shape=[_sds((T, D_FF), BF16), _sds((T, D_FF), BF16), _sds((8, D_FF), F32), _sds((8, D_FF), F32)],
        compiler_params=_cp(2),
    )(up_pre, up_pre, up_pre, up_pre, up_pre, up_pre, dact, dact, w_conv, w_conv, b_conv, b_conv)


NS_IN, NS_UP, NS_XKV, NS_UQ = 564, 704, 256, 96
E2_ROWS, E2_UQ = 384, 128
L1_W = NS_UP + NS_XKV
L2_ROWS = 736
L2_OUT, L2_XQ, L2_XO, L2_DN = 0, 128, 256, 384
RS_DN = 352
L3_ROWS, L3_PRET = 1024, 512
L4_SHAPE = (8, 768)
WZ_RUNS = ((416, 4096, 0), (0, 384, 4096), (384, 32, 4544))
WZ_ZERO = ((4480, 4544), (4576, 4608))


def _pieces(orig_start, length, dst_start, ns):
    out, c, d, end = [], orig_start, dst_start, orig_start + length
    while c < end:
        j, off = c // ns, c % ns
        ln = min(ns - off, end - c)
        out.append((j, off, d, ln))
        c += ln
        d += ln
    return out


def pack_early(w_in, w_ukv, w_uq):
    def body(in_ref, ukv_ref, uq_ref, e1_ref, e2_ref):
        e1_ref[...] = in_ref[...].astype(BF16)
        e2_ref[0:E2_UQ, :] = ukv_ref[...].astype(BF16)
        e2_ref[E2_UQ:E2_ROWS, 0:NS_UQ] = uq_ref[...].astype(BF16)
        e2_ref[E2_UQ:E2_ROWS, NS_UQ:HP] = jnp.zeros((E2_ROWS - E2_UQ, HP - NS_UQ), BF16)

    return pl.pallas_call(
        body,
        name="pack_early",
        in_specs=[VM] * 3,
        out_specs=[VM] * 2,
        out_shape=[_sds((D, NS_IN), BF16), _sds((E2_ROWS, HP), BF16)],
        compiler_params=_cp0(),
    )(w_in, w_ukv, w_uq)


def pack_late(w_up, w_xkv, w_out, w_xq, w_xo, w_down, w_pmla, w_pret, w_conv):
    def body(up_ref, xkv_ref, o_ref, xq_ref, xo_ref, dn_ref, pm_ref, pr_ref, cv_ref, l1_ref, l2_ref, l3_ref, l4_ref):
        l1_ref[:, 0:NS_UP] = up_ref[...].astype(BF16)
        l1_ref[:, NS_UP:L1_W] = xkv_ref[...].astype(BF16)
        l2_ref[L2_OUT:L2_XQ, :] = o_ref[...].astype(BF16)
        l2_ref[L2_XQ:L2_XO, :] = xq_ref[...].astype(BF16)
        l2_ref[L2_XO:L2_DN, :] = xo_ref[...].astype(BF16)
        l2_ref[L2_DN:L2_ROWS, :] = dn_ref[...].astype(BF16)
        l3_ref[0:L3_PRET, :] = pm_ref[...].astype(BF16)
        l3_ref[L3_PRET:L3_ROWS, :] = pr_ref[...].astype(BF16)
        l4_ref[...] = jnp.zeros(L4_SHAPE, F32)
        l4_ref[0:3, 0:NS_UP] = cv_ref[...]

    return pl.pallas_call(
        body,
        name="pack_late",
        in_specs=[VM] * 9,
        out_specs=[VM] * 4,
        out_shape=[_sds((D, L1_W), BF16), _sds((L2_ROWS, D), BF16), _sds((L3_ROWS, HP), BF16), _sds(L4_SHAPE, F32)],
        compiler_params=_cp0(),
    )(w_up, w_xkv, w_out, w_xq, w_xo, w_down, w_pmla, w_pret, w_conv)


def assemble_early(g1, g2, *, tile):
    def body(g1_ref, g2_ref, wz_ref, wk_ref, wv_ref, wq_ref):
        for lo, hi in WZ_ZERO:
            wz_ref[:, lo:hi] = jnp.zeros((tile, hi - lo), BF16)
        for os_, ln_, ds_ in WZ_RUNS:
            for j, off, d, ln in _pieces(os_, ln_, ds_, NS_IN):
                wz_ref[:, d:d + ln] = g1_ref[j, :, off:off + ln]

        @pl.when(pl.program_id(0) == 0)
        def _():
            half = jnp.zeros((MLA_KVR, HP - MLA_NOPE), BF16)
            for j in range(N_DEV):
                wk_ref[:, j * HP:j * HP + MLA_NOPE] = g2_ref[j, 0:E2_UQ, 0:MLA_NOPE]
                wk_ref[:, j * HP + MLA_NOPE:(j + 1) * HP] = half
                wv_ref[:, j * HP:j * HP + MLA_V] = g2_ref[j, 0:E2_UQ, MLA_NOPE:HP]
                wv_ref[:, j * HP + MLA_V:(j + 1) * HP] = half
                wq_ref[:, j * HP:j * HP + NS_UQ] = g2_ref[j, E2_UQ:E2_ROWS, 0:NS_UQ]
                wq_ref[:, j * HP + NS_UQ:(j + 1) * HP] = jnp.zeros((MLA_QR, HP - NS_UQ), BF16)

    def whole(r):
        return pl.BlockSpec((r, MLA_H * HP), lambda i: (0, 0))

    return pl.pallas_call(
        body,
        name="assemble_early",
        grid=(D // tile,),
        in_specs=[pl.BlockSpec((N_DEV, tile, NS_IN), lambda i: (0, i, 0)),
                  pl.BlockSpec((N_DEV, E2_ROWS, HP), lambda i: (0, 0, 0))],
        out_specs=[pl.BlockSpec((tile, ZW), lambda i: (i, 0)), whole(MLA_KVR), whole(MLA_KVR), whole(MLA_QR)],
        out_shape=[_sds((D, ZW), BF16), _sds((MLA_KVR, MLA_H * HP), BF16), _sds((MLA_KVR, MLA_H * HP), BF16),
                   _sds((MLA_QR, MLA_H * HP), BF16)],
        compiler_params=_cp(1),
    )(g1, g2)


def assemble_l1(g1, *, tile):
    def body(g_ref, wup_ref, wxkv_ref):
        for j in range(N_DEV):
            wup_ref[:, j * NS_UP:(j + 1) * NS_UP] = g_ref[j, :, 0:NS_UP]
            wxkv_ref[:, j * NS_XKV:(j + 1) * NS_XKV] = g_ref[j, :, NS_UP:L1_W]

    return pl.pallas_call(
        body,
        name="assemble_l1",
        grid=(D // tile,),
        in_specs=[pl.BlockSpec((N_DEV, tile, L1_W), lambda i: (0, i, 0))],
        out_specs=[pl.BlockSpec((tile, 2 * D_FF), lambda i: (i, 0)), pl.BlockSpec((tile, 2 * D), lambda i: (i, 0))],
        out_shape=[_sds((D, 2 * D_FF), BF16), _sds((D, 2 * D), BF16)],
        compiler_params=_cp(1),
    )(g1)


def assemble_l234(g2, g3, g4):
    def body(g2_ref, g3_ref, g4_ref, wo_ref, wxq_ref, wxo_ref, wdn_ref, wpa_ref, wpr_ref, wc_ref):
        for j in range(N_DEV):
            wo_ref[j * 128:(j + 1) * 128, :] = g2_ref[j, L2_OUT:L2_XQ, :]
            wxq_ref[j * 128:(j + 1) * 128, :] = g2_ref[j, L2_XQ:L2_XO, :]
            wxo_ref[j * 128:(j + 1) * 128, :] = g2_ref[j, L2_XO:L2_DN, :]
            wdn_ref[j * RS_DN:(j + 1) * RS_DN, :] = g2_ref[j, L2_DN:L2_ROWS, :]
            wpr_ref[:, j * 128:(j + 1) * 128] = g3_ref[j, L3_PRET:L3_ROWS, :]
            wc_ref[:, j * NS_UP:(j + 1) * NS_UP] = g4_ref[j, 0:3, 0:NS_UP]
            for h in range(MLA_H):
                wpa_ref[h * HP:h * HP + MLA_V, j * 128:(j + 1) * 128] = g3_ref[j, h * MLA_V:(h + 1) * MLA_V, :]
        for h in range(MLA_H):
            wpa_ref[h * HP + MLA_V:(h + 1) * HP, :] = jnp.zeros((HP - MLA_V, D), BF16)

    return pl.pallas_call(
        body,
        name="assemble_l234",
        in_specs=[VM] * 3,
        out_specs=[VM] * 7,
        out_shape=[_sds((D, D), BF16), _sds((D, D), BF16), _sds((D, D), BF16), _sds((D_FF, D), BF16),
                   _sds((MLA_H * HP, D), BF16), _sds((RET_H * RET_D, D), BF16), _sds((3, 2 * D_FF), F32)],
        compiler_params=_cp0(),
    )(g2, g3, g4)


def slots_l1(dwup_a, dwup_b, dwxkv, *, tile):
    per_half = D_FF // NS_UP

    def body(ua_ref, ub_ref, x_ref, s_ref):
        for j in range(N_DEV):
            src = ua_ref if j < per_half else ub_ref
            c0 = (j % per_half) * NS_UP
            s_ref[j, :, 0:NS_UP] = src[:, c0:c0 + NS_UP].astype(BF16)
            s_ref[j, :, NS_UP:L1_W] = x_ref[:, j * NS_XKV:(j + 1) * NS_XKV].astype(BF16)

    return pl.pallas_call(
        body,
        name="slots_l1",
        grid=(D // tile,),
        in_specs=[pl.BlockSpec((tile, D_FF), lambda i: (i, 0)), pl.BlockSpec((tile, D_FF), lambda i: (i, 0)),
                  pl.BlockSpec((tile, 2 * D), lambda i: (i, 0))],
        out_specs=pl.BlockSpec((N_DEV, tile, L1_W), lambda i: (0, i, 0)),
        out_shape=_sds((N_DEV, D, L1_W), BF16),
        compiler_params=_cp(1),
    )(dwup_a, dwup_b, dwxkv)


def slots_l23(dwo, dwxq, dwxo, dwdn, dwpa, dwpr):
    def body(o_ref, xq_ref, xo_ref, dn_ref, pa_ref, pr_ref, l2_ref, l3_ref):
        l2_ref[L2_OUT:L2_XQ, :] = o_ref[...].astype(BF16)
        l2_ref[L2_XQ:L2_XO, :] = xq_ref[...].astype(BF16)
        l2_ref[L2_XO:L2_DN, :] = xo_ref[...].astype(BF16)
        l2_ref[L2_DN:L2_ROWS, :] = dn_ref[...].astype(BF16)
        for h in range(MLA_H):
            l3_ref[h * MLA_V:(h + 1) * MLA_V, :] = pa_ref[h * HP:h * HP + MLA_V, :].astype(BF16)
        l3_ref[L3_PRET:L3_ROWS, :] = pr_ref[...].astype(BF16)

    rows128 = pl.BlockSpec((128, D), lambda j: (j, 0))

    def cols(r):
        return pl.BlockSpec((r, 128), lambda j: (0, j))

    return pl.pallas_call(
        body,
        name="slots_l23",
        grid=(N_DEV,),
        in_specs=[rows128, rows128, rows128, pl.BlockSpec((RS_DN, D), lambda j: (j, 0)),
                  cols(MLA_H * HP), cols(RET_H * RET_D)],
        out_specs=[pl.BlockSpec((None, L2_ROWS, D), lambda j: (j, 0, 0)),
                   pl.BlockSpec((None, L3_ROWS, HP), lambda j: (j, 0, 0))],
        out_shape=[_sds((N_DEV, L2_ROWS, D), BF16), _sds((N_DEV, L3_ROWS, HP), BF16)],
        compiler_params=_cp(1),
    )(dwo, dwxq, dwxo, dwdn, dwpa, dwpr)


def slots_early(dwz, dwk, dwv, dwq, *, tile):
    def body(dz_ref, k_ref, v_ref, q_ref, s1_ref, s2_ref):
        for os_, ln_, ds_ in WZ_RUNS:
            for j, off, d, ln in _pieces(os_, ln_, ds_, NS_IN):
                s1_ref[j, :, off:off + ln] = dz_ref[:, d:d + ln].astype(BF16)

        @pl.when(pl.program_id(0) == 0)
        def _():
            for j in range(N_DEV):
                s2_ref[j, 0:E2_UQ, 0:MLA_NOPE] = k_ref[:, j * HP:j * HP + MLA_NOPE].astype(BF16)
                s2_ref[j, 0:E2_UQ, MLA_NOPE:HP] = v_ref[:, j * HP:j * HP + MLA_V].astype(BF16)
                s2_ref[j, E2_UQ:E2_ROWS, 0:NS_UQ] = q_ref[:, j * HP:j * HP + NS_UQ].astype(BF16)
                s2_ref[j, E2_UQ:E2_ROWS, NS_UQ:HP] = jnp.zeros((E2_ROWS - E2_UQ, HP - NS_UQ), BF16)

    def whole(r):
        return pl.BlockSpec((r, MLA_H * HP), lambda i: (0, 0))

    return pl.pallas_call(
        body,
        name="slots_early",
        grid=(D // tile,),
        in_specs=[pl.BlockSpec((tile, ZW), lambda i: (i, 0)), whole(MLA_KVR), whole(MLA_KVR), whole(MLA_QR)],
        out_specs=[pl.BlockSpec((N_DEV, tile, NS_IN), lambda i: (0, i, 0)),
                   pl.BlockSpec((N_DEV, E2_ROWS, HP), lambda i: (0, 0, 0))],
        out_shape=[_sds((N_DEV, D, NS_IN), BF16), _sds((N_DEV, E2_ROWS, HP), BF16)],
        compiler_params=_cp(1),
    )(dwz, dwk, dwv, dwq)


SMALL = (("g_mix", 1024, 0), ("b_gate", 2048, 1), ("g_q_lat", 256, 3), ("g_kv_lat", 128, 4), ("g_ret", 512, 5),
         ("g_cross", 1024, 6), ("g_mem", 1024, 7), ("g_ffn", 1024, 8), ("b_conv", 5632, 9), ("g_final", 1024, 15))
SMALL_DIRECT = tuple(s for s in SMALL if s[0] != "b_conv")
S_ROWS = 16


def _flat_pieces(n, row0):
    return [(row0 + k // D, k, min(D, n - k)) for k in range(0, n, D)]


def exchange_small(cs_a, cs_b, smalls):
    ns = len(smalls)

    def body(*refs):
        ca_ref, cb_ref = refs[:2]
        small_refs = refs[2:2 + ns]
        rd_ref, rs_ref, dsl_ref, own_ref = refs[2 + ns:6 + ns]
        sems = refs[6 + ns:]
        own_ref[...] = jnp.zeros(own_ref.shape, F32)
        for (name, n, row0), g_ref in zip(SMALL_DIRECT, small_refs):
            for r, c0, ln in _flat_pieces(n, row0):
                own_ref[r:r + 1, 0:ln] = g_ref[:, c0:c0 + ln]
        row0 = dict((s[0], s[2]) for s in SMALL)["b_conv"]
        for half, c_ref in enumerate((ca_ref, cb_ref)):
            k = half * D_FF
            end = k + D_FF
            while k < end:
                r, lane = row0 + k // D, k % D
                ln = min(D - lane, end - k)
                own_ref[r:r + 1, lane:lane + ln] = c_ref[3:4, k - half * D_FF:k - half * D_FF + ln]
                k += ln
        dsl_ref[...] = jnp.zeros(dsl_ref.shape, F32)
        per_half = D_FF // NS_UP
        for j in range(N_DEV):
            c_ref = ca_ref if j < per_half else cb_ref
            c0 = (j % per_half) * NS_UP
            dsl_ref[j, 0:3, 0:NS_UP] = c_ref[0:3, c0:c0 + NS_UP]
        comm = ([(dsl_ref, True), (own_ref, False)], [rd_ref, rs_ref]) + tuple(sems)
        _exchange_start(*comm)
        _exchange_wait(*comm)

    n1 = N_DEV - 1
    return pl.pallas_call(
        body,
        name="exchange_small",
        in_specs=[VM] * (2 + ns),
        out_specs=[VM, VM, VM, VM],
        out_shape=[_sds((n1,) + L4_SHAPE, F32), _sds((n1, S_ROWS, D), F32), _sds((N_DEV,) + L4_SHAPE, F32),
                   _sds((S_ROWS, D), F32)],
        scratch_shapes=_exchange_scratch(2),
        compiler_params=_cp0(),
    )(cs_a, cs_b, *smalls)


def _adamw(w, g, m, v):
    m = ADAM_B1 * m + (1.0 - ADAM_B1) * g
    v = ADAM_B2 * v + (1.0 - ADAM_B2) * (g * g)
    m_hat = m / (1.0 - ADAM_B1 ** ADAM_STEP)
    v_hat = v / (1.0 - ADAM_B2 ** ADAM_STEP)
    delta = -ADAM_LR * (m_hat / (jnp.sqrt(v_hat) + ADAM_EPS) + ADAM_WD * w)
    return delta, m, v


def _apply(g, refs, outs):
    d, mn, vn = _adamw(refs[0][...], g, refs[1][...], refs[2][...])
    outs[0][...] = g
    outs[1][...] = d
    outs[2][...] = mn
    outs[3][...] = vn


def adam_cols(own, recv, wmv, spans, *, name, tile):
    R, W = own.shape
    nw = len(spans)

    def body(*refs):
        own_ref, recv_ref = refs[:2]
        ins, outs = refs[2:2 + 3 * nw], refs[2 + 3 * nw:]
        g = own_ref[...].astype(F32)
        for k in range(N_DEV - 1):
            g = g + recv_ref[k].astype(F32)
        for t, (lo, hi) in enumerate(spans):
            _apply(g[:, lo:hi], ins[3 * t:3 * t + 3], outs[4 * t:4 * t + 4])

    def blk(w):
        return pl.BlockSpec((tile, w), lambda i: (i, 0))

    widths = [hi - lo for lo, hi in spans]
    return pl.pallas_call(
        body,
        name=name,
        grid=(R // tile,),
        in_specs=[blk(W), pl.BlockSpec((N_DEV - 1, tile, W), lambda i: (0, i, 0))] + [blk(w) for w in widths for _ in range(3)],
        out_specs=[blk(w) for w in widths for _ in range(4)],
        out_shape=[_sds((R, w), F32) for w in widths for _ in range(4)],
        compiler_params=_cp(1),
    )(own, recv, *wmv)


def adam_rows(own, recv, wmv, spans, *, name):
    nw = len(spans)

    def body(*refs):
        own_ref, recv_ref = refs[:2]
        ins, outs = refs[2:2 + 3 * nw], refs[2 + 3 * nw:]
        for t, (lo, hi, w) in enumerate(spans):
            g = own_ref[lo:hi, :].astype(F32)
            for k in range(N_DEV - 1):
                g = g + recv_ref[k, lo:hi, :].astype(F32)
            _apply(g[:, 0:w], ins[3 * t:3 * t + 3], outs[4 * t:4 * t + 4])

    return pl.pallas_call(
        body,
        name=name,
        in_specs=[VM] * (2 + 3 * nw),
        out_specs=[VM] * (4 * nw),
        out_shape=[_sds((hi - lo, w), F32) for lo, hi, w in spans for _ in range(4)],
        compiler_params=_cp0(),
    )(own, recv, *wmv)


def adam_small(own_s, recv_s, dslots, recv_d, wmv_small, wmv_conv):
    ns = len(SMALL)

    def body(*refs):
        own_ref, rs_ref, dsl_ref, rd_ref = refs[:4]
        ins = refs[4:4 + 3 * ns + 3]
        outs = refs[4 + 3 * ns + 3:4 + 3 * ns + 3 + 4 * ns + 4]
        all_sc = refs[-1]
        me = _my_id()
        all_sc[0] = own_ref[...]
        for k in range(N_DEV - 1):
            all_sc[k + 1] = rs_ref[k]
        g = all_sc[jnp.bitwise_xor(me, 0)]
        for s in range(1, N_DEV):
            g = g + all_sc[jnp.bitwise_xor(me, s)]
        all_sc[0] = g
        for t, (name, n, row0) in enumerate(SMALL):
            pieces = [all_sc[0, r:r + 1, 0:ln] for r, _, ln in _flat_pieces(n, row0)]
            gt = pieces[0] if len(pieces) == 1 else jnp.concatenate(pieces, axis=1)
            _apply(gt, ins[3 * t:3 * t + 3], outs[4 * t:4 * t + 4])
        gc = dsl_ref[me]
        for k in range(N_DEV - 1):
            gc = gc + rd_ref[k]
        _apply(gc[0:3, 0:NS_UP], ins[3 * ns:3 * ns + 3], outs[4 * ns:4 * ns + 4])

    out_shape = [_sds((1, n), F32) for _, n, _ in SMALL for _ in range(4)] + [_sds((3, NS_UP), F32)] * 4
    return pl.pallas_call(
        body,
        name="adam_small",
        in_specs=[VM] * (4 + 3 * ns + 3),
        out_specs=[VM] * len(out_shape),
        out_shape=out_shape,
        scratch_shapes=[pltpu.VMEM((N_DEV, S_ROWS, D), F32)],
    )(own_s, recv_s, dslots, recv_d, *wmv_small, *wmv_conv)


def kernel(x, mem, positions, g_mix, w_in, b_gate, g_q_lat, w_uq, g_kv_lat, w_ukv, w_proj_mla, g_ret, w_proj_ret, w_out, g_cross, g_mem, w_xq, w_xkv, w_xo, g_ffn, w_up, w_conv, b_conv, w_down, g_final, loss_target, m_g_mix, m_w_in, m_b_gate, m_g_q_lat, m_w_uq, m_g_kv_lat, m_w_ukv, m_w_proj_mla, m_g_ret, m_w_proj_ret, m_w_out, m_g_cross, m_g_mem, m_w_xq, m_w_xkv, m_w_xo, m_g_ffn, m_w_up, m_w_conv, m_b_conv, m_w_down, m_g_final, v_g_mix, v_w_in, v_b_gate, v_g_q_lat, v_w_uq, v_g_kv_lat, v_w_ukv, v_w_proj_mla, v_g_ret, v_w_proj_ret, v_w_out, v_g_cross, v_g_mem, v_w_xq, v_w_xkv, v_w_xo, v_g_ffn, v_w_up, v_w_conv, v_b_conv, v_w_down, v_g_final):
    args = dict(locals())
    T = x.shape[1]
    M = mem.shape[1]
    tile = min(256, T)
    tile2 = min(512, T)
    tile4 = min(1024, T)
    tq = min(512, T)
    rb = min(1024, T)

    xs = x[0]
    tgt = loss_target[0]
    mems = mem[0]

    def shard(name, prefix=""):
        a = args[prefix + name]
        return a.reshape(a.shape[-2:]) if a.ndim >= 2 else a.reshape(1, -1)

    e1, e2 = pack_early(shard("w_in"), shard("w_ukv"), shard("w_uq"))
    late_parts = pack_late(shard("w_up"), shard("w_xkv"), shard("w_out"), shard("w_xq"), shard("w_xo"), shard("w_down"),
                           shard("w_proj_mla"), shard("w_proj_ret"), shard("w_conv"))

    pos_f = jnp.broadcast_to(positions[0].astype(F32)[:, None], (T, 128))
    inv_m, inv_r = _rot_inv()
    u, cm, s1, s2, cr, sr, ge1, ge2 = rowwise(
        lambda xv, p, im, ir, g: (_rms(xv, g),) + _rot_tables(p, im, ir), [(xs, None), (pos_f, None)],
        [inv_m, inv_r, g_mix], [(D, BF16)] + [(128, F32)] * 5, [], tile=tile2, name="norm_mix_tables", gather=[e1, e2])
    wz, wk, wv, wq = assemble_early(ge1, ge2, tile=tile)
    rconsts = _ret_consts()

    z = matmul(u, wz, name="mm_z", tm=2048, tn=1536)

    def mixers_in(zl, zr, cmv, s1v, s2v, crv, srv, gq, gkv, wqv, wkv_, wvv):
        cq = _rms(zl[:, 0:256], gq).astype(BF16)
        ckv = _rms(zl[:, 256:384], gkv).astype(BF16)
        qv = _rot_mla(_dot(cq, wqv), cmv, s1v, s2v)
        kr = _rot_mla(zl[:, 384:512], cmv, s1v, s2v)
        kn = _dot(ckv, wkv_)
        kv_ = jnp.concatenate([kn[:, h * HP:(h + 1) * HP] + kr for h in range(MLA_H)], axis=1)
        vv = _dot(ckv, wvv)
        lane = lax.broadcasted_iota(jnp.int32, vv.shape, 1)
        vv = jnp.where((lane & (HP - 1)) == ONE_LANE, 1.0, vv)
        rqv = _rot_ret(zr[:, 0:512], crv, srv)
        rkv = _rot_ret(zr[:, 512:1024], crv, srv) * (RET_D ** -0.5)
        return qv, kv_, vv, rqv, rkv, zr[:, 1024:1536]

    q_a, k_a, v_a, rq, rk, rv = rowwise(
        mixers_in, [(z, (512, 8)), (z, (2048, 0)), (cm, None), (s1, None), (s2, None), (cr, None), (sr, None)],
        [g_q_lat, g_kv_lat, wq, wk, wv], [(MLA_H * HP, BF16)] * 3 + [(512, BF16)] * 3, [], tile=tile2, name="mixers_in")

    o_a, lse_row, gl1, gl2, gl3, gl4 = mla_fwd(q_a, k_a, v_a, list(late_parts), tq=tq)
    wup, wxkv = assemble_l1(gl1, tile=tile)
    wo, wxq, wxo, wdn, wpa, wpr, wcv = assemble_l234(gl2, gl3, gl4)
    ret, rstate = ret_fwd(rq, rk, rv, rconsts, rb=rb)

    def gn_parts(r):
        outs = []
        for h in range(RET_H):
            rh = r[:, h * RET_D:(h + 1) * RET_D]
            mu = jnp.mean(rh, axis=-1, keepdims=True)
            dlt = rh - mu
            rstd = lax.rsqrt(jnp.mean(dlt * dlt, axis=-1, keepdims=True) + EPS)
            outs.append((dlt * rstd, rstd))
        return outs

    def mix_fwd(ov, rv_, rg, gt, wpav, wprv, gr, bg):
        ya = _dot(ov, wpav)
        xh = jnp.concatenate([p[0] for p in gn_parts(rv_)], axis=1)
        t = rg * _sigmoid(rg) * (xh * gr)
        yr = _dot(t.astype(BF16), wprv)
        ga_ = _sigmoid(gt[:, :D] + bg[:, :D])
        gr_ = _sigmoid(gt[:, D:] + bg[:, D:])
        return ga_ * ya + gr_ * yr

    mix = rowwise(mix_fwd, [(o_a, None), (ret, None), (z, (512, 3)), (z, (2048, 1))], [wpa, wpr, g_ret, b_gate],
                  [(D, BF16)], [], tile=tile2, name="mix_fwd")[0]
    def proj_norm(av, rv_, wv_, g):
        hv = rv_ + _dot(av, wv_)
        return hv, _rms(hv, g)

    h1, n2 = rowwise(proj_norm, [(mix, None), (xs, None)], [wo, g_cross], [(D, F32), (D, BF16)], [], tile=tile4,
                     name="mm_out_norm")
    xq = matmul(n2, wxq, out_dtype=BF16, name="mm_xq")
    mn = rowwise(lambda mv_, g: _rms(mv_, g), [(mems, None)], [g_mem], [(D, BF16)], [], tile=min(tile, M), name="norm_mem")[0]
    mkv = matmul(mn, wxkv, out_dtype=BF16, name="mm_mkv")

    x_scale = X_HD ** -0.5

    def xattn_fwd(xqv, mkvv):
        outs = []
        for h in range(X_H):
            sl = slice(h * X_HD, (h + 1) * X_HD)
            s = _dot_nt(xqv[:, sl], mkvv[:, sl]) * x_scale
            s = s - jnp.max(s, axis=-1, keepdims=True)
            e = jnp.exp(s)
            p = e / jnp.sum(e, axis=-1, keepdims=True)
            outs.append(_dot(p.astype(BF16), mkvv[:, D + h * X_HD:D + (h + 1) * X_HD]))
        return jnp.concatenate(outs, axis=1)

    xo = rowwise(xattn_fwd, [(xq, None)], [mkv], [(D, BF16)], [], tile=tile2, name="xattn_fwd")[0]
    h2, n3 = rowwise(proj_norm, [(xo, None), (h1, None)], [wxo, g_ffn], [(D, F32), (D, BF16)], [], tile=tile4,
                     name="mm_xo_norm")
    up_pre = matmul(n3, wup, out_dtype=BF16, name="mm_up", tm=2048, tn=1408)
    cw = D_FF // 2
    act = conv_act_fwd(up_pre, wcv, b_conv, tile=tile, cw=cw)

    def down_loss(av, hv2, tv, wv_, g):
        hv = hv2 + _dot(av, wv_)
        y = _rms(hv, g)
        err = y - tv
        part = 0.5 * jnp.sum(jnp.sum(err * err, axis=-1, keepdims=True) / D, axis=0, keepdims=True)
        dx, dg = _rms_bwd(hv, g, err / D)
        return dx, dg, jnp.broadcast_to(part, (8, 128))

    g_fin2 = g_final.reshape(1, D)
    dh3, dg_final, loss_acc = rowwise(down_loss, [(act, None), (h2, None), (tgt, None)], [wdn, g_fin2], [(D, F32)],
                                      [((1, D), F32), ((8, 128), F32)], tile=tile2, name="mm_down_loss")
    loss = lax.psum(loss_acc[0, 0], ("x", "y", "c"))

    dact = matmul(dh3, wdn, tb=True, out_dtype=BF16, name="mm_dact", tm=2048, tn=1408)
    dw_down = matmul_tn(act, dh3, name="mm_dw_down", tm=1408, tk=1024)
    dup_a, dup_b, cs_a, cs_b = conv_act_bwd(up_pre, dact, wcv, b_conv, tile=tile, cw=cw)
    dw_up_a = matmul_tn(n3, dup_a, name="mm_dw_up_a", tn=1408)
    dw_up_b = matmul_tn(n3, dup_b, name="mm_dw_up_b", tn=1408)

    def ffn_in_bwd(da_, db_, hv, drv, wv_, g):
        dn = _dot_nt(da_, wv_[:, :D_FF]) + _dot_nt(db_, wv_[:, D_FF:])
        dx, dg = _rms_bwd(hv, g, dn)
        return dx + drv, dg

    dh2, dg_ffn = rowwise(ffn_in_bwd, [(dup_a, None), (dup_b, None), (h2, None), (dh3, None)], [wup, g_ffn],
                          [(D, F32)], [((1, D), F32)], tile=tile2, name="mm_dn3_norm_bwd")
    dxo = matmul(dh2, wxo, tb=True, out_dtype=BF16, name="mm_dxo")
    dw_xo = matmul_tn(xo, dh2, name="mm_dw_xo")

    def xattn_bwd(xqv, dxov, mkvv):
        dxq, dmk, dmv = [], [], []
        for h in range(X_H):
            sl = slice(h * X_HD, (h + 1) * X_HD)
            slv = slice(D + h * X_HD, D + (h + 1) * X_HD)
            s = _dot_nt(xqv[:, sl], mkvv[:, sl]) * x_scale
            s = s - jnp.max(s, axis=-1, keepdims=True)
            e = jnp.exp(s)
            p = e / jnp.sum(e, axis=-1, keepdims=True)
            dp = _dot_nt(dxov[:, sl], mkvv[:, slv])
            ds = (p * (dp - jnp.sum(dp * p, axis=-1, keepdims=True)) * x_scale).astype(BF16)
            dxq.append(_dot(ds, mkvv[:, sl]))
            dmk.append(_dot_tn(ds, xqv[:, sl]))
            dmv.append(_dot_tn(p.astype(BF16), dxov[:, sl]))
        return jnp.concatenate(dxq, axis=1), jnp.concatenate(dmk + dmv, axis=1)

    dxq, dmkv = rowwise(xattn_bwd, [(xq, None), (dxo, None)], [mkv], [(D, BF16)], [((M, 2 * D), F32)],
                        tile=tile2, name="xattn_bwd")
    dw_xq = matmul_tn(n2, dxq, name="mm_dw_xq")

    def proj_norm_bwd(dyv, hv, drv, wv_, g):
        dx, dg = _rms_bwd(hv, g, _dot_nt(dyv, wv_))
        return dx + drv, dg

    dh1, dg_cross = rowwise(proj_norm_bwd, [(dxq, None), (h1, None), (dh2, None)], [wxq, g_cross], [(D, F32)],
                            [((1, D), F32)], tile=tile4, name="mm_dn2_norm_bwd")
    dw_xkv = matmul_tn(mn, dmkv, name="mm_dw_xkv", tk=M)
    dmn = matmul(dmkv, wxkv, tb=True, name="mm_dmn", tm=M)
    dg_mem = rowwise(lambda mv_, dyv, g: _rms_bwd(mv_, g, dyv)[1], [(mems, None), (dmn, None)], [g_mem], [],
                     [((1, D), F32)], tile=min(tile, M), name="norm_mem_bwd")[0]

    dmix = matmul(dh1, wo, tb=True, out_dtype=BF16, name="mm_dmix")
    dw_out = matmul_tn(mix, dh1, name="mm_dw_out")

    def mix_bwd(ov, rv_, rg, gt, dmv_, wpav, wprv, gr, bg):
        dm_ = dmv_.astype(F32)
        ya = _dot(ov, wpav)
        parts = gn_parts(rv_)
        xh = jnp.concatenate([p[0] for p in parts], axis=1)
        yn = xh * gr
        sg = _sigmoid(rg)
        sl_ = rg * sg
        t = (sl_ * yn).astype(BF16)
        yr = _dot(t, wprv)
        ga_ = _sigmoid(gt[:, :D] + bg[:, :D])
        gr_ = _sigmoid(gt[:, D:] + bg[:, D:])
        dgates = jnp.concatenate([dm_ * ya * ga_ * (1.0 - ga_), dm_ * yr * gr_ * (1.0 - gr_)], axis=1)
        dya = (dm_ * ga_).astype(BF16)
        dyr = (dm_ * gr_).astype(BF16)
        do_ = _dot_nt(dya, wpav)
        dwpa_ = _dot_tn(ov, dya)
        dt = _dot_nt(dyr, wprv)
        dwpr_ = _dot_tn(t, dyr)
        drg_ = dt * yn * (sg * (1.0 + rg * (1.0 - sg)))
        dyn = dt * sl_
        dgr = jnp.sum(dyn * xh, axis=0, keepdims=True)
        dxh = dyn * gr
        drets = []
        for h in range(RET_H):
            sl = slice(h * RET_D, (h + 1) * RET_D)
            xhh, rstd = parts[h]
            dxhh = dxh[:, sl]
            drets.append(rstd * (dxhh - jnp.mean(dxhh, axis=-1, keepdims=True)
                                 - xhh * jnp.mean(dxhh * xhh, axis=-1, keepdims=True)))
        dret_ = jnp.concatenate(drets, axis=1)
        dbg = jnp.sum(dgates, axis=0, keepdims=True)
        prod = ov.astype(F32) * do_.astype(BF16).astype(F32)
        lane = lax.broadcasted_iota(jnp.int32, (prod.shape[0], HP), 1)
        dlt = jnp.zeros((prod.shape[0], HP), F32)
        for h in range(MLA_H):
            dlt = jnp.where(lane == h, jnp.sum(prod[:, h * HP:(h + 1) * HP], axis=-1, keepdims=True), dlt)
        return do_, dret_, drg_, dgates, dlt, dwpa_, dwpr_, dgr, dbg

    do_a, dret, drg, dgates, delta, dwpa, dw_proj_ret, dg_ret, db_gate = rowwise(
        mix_bwd, [(o_a, None), (ret, None), (z, (512, 3)), (z, (2048, 1)), (dmix, None)], [wpa, wpr, g_ret, b_gate],
        [(MLA_H * HP, BF16), (512, F32), (512, BF16), (2 * D, BF16), (HP, F32)],
        [((MLA_H * HP, D), F32), ((512, D), F32), ((1, 512), F32), ((1, 2 * D), F32)], tile=tile, name="mix_bwd")

    sl1 = slots_l1(dw_up_a, dw_up_b, dw_xkv, tile=tile)
    sl2, sl3 = slots_l23(dw_out, dw_xq, dw_xo, dw_down, dwpa, dw_proj_ret)
    delta_row = mla_prep(delta, tq=tq)
    dq_a, dk_a, dv_a, rl1, rl2, rl3 = mla_bwd(q_a, k_a, v_a, do_a, lse_row, delta_row, [sl1, sl2, sl3], tq=tq)
    drq_r, drk_r, drv = ret_bwd(rq, rk, rv, rstate, dret, rconsts, rb=rb)

    def mixers_in_bwd(zl, cmv, s1v, s2v, dqv, dkv_, dvv, drq_, drk_, drv_, drg_, dgt, crv, srv, gq, gkv, wqv, wkv_, wvv):
        cqf, ckvf = zl[:, 0:256], zl[:, 256:384]
        cq = _rms(cqf, gq).astype(BF16)
        ckv = _rms(ckvf, gkv).astype(BF16)
        dq_pre = _rot_mla(dqv.astype(F32), cmv, -s1v, -s2v).astype(BF16)
        dkf = dkv_.astype(F32)
        dkr = dkf[:, 0:HP]
        for h in range(1, MLA_H):
            dkr = dkr + dkf[:, h * HP:(h + 1) * HP]
        lane = lax.broadcasted_iota(jnp.int32, dkr.shape, 1)
        dzk = _rot_mla(jnp.where((lane >= 64) & (lane < 96), dkr, 0.0), cmv, -s1v, -s2v)
        dkb = dkv_.astype(BF16)
        dvb = dvv.astype(BF16)
        dcq_n = _dot_nt(dq_pre, wqv)
        dckv_n = _dot_nt(dkb, wkv_) + _dot_nt(dvb, wvv)
        dwq_ = _dot_tn(cq, dq_pre)
        dwk_ = _dot_tn(ckv, dkb)
        dwv_ = _dot_tn(ckv, dvb)
        dcq, dgq = _rms_bwd(cqf, gq, dcq_n)
        dckv, dgkv = _rms_bwd(ckvf, gkv, dckv_n)
        a = _rot_ret(drq_, crv, -srv)
        b = _rot_ret(drk_, crv, -srv) * (RET_D ** -0.5)
        dz_ = jnp.concatenate([a.astype(BF16), b.astype(BF16), drv_, drg_, dgt,
                               dcq.astype(BF16), dckv.astype(BF16), dzk.astype(BF16)], axis=1)
        return dz_, dwq_, dwk_, dwv_, dgq, dgkv

    dz, dwq, dwk, dwv, dg_q_lat, dg_kv_lat = rowwise(
        mixers_in_bwd, [(z, (512, 8)), (cm, None), (s1, None), (s2, None), (dq_a, None), (dk_a, None), (dv_a, None),
                        (drq_r, None), (drk_r, None), (drv, None), (drg, None), (dgates, None), (cr, None), (sr, None)],
        [g_q_lat, g_kv_lat, wq, wk, wv], [(ZW, BF16)],
        [((MLA_QR, MLA_H * HP), F32), ((MLA_KVR, MLA_H * HP), F32), ((MLA_KVR, MLA_H * HP), F32),
         ((1, MLA_QR), F32), ((1, MLA_KVR), F32)], tile=tile, name="mixers_in_bwd")
    dwz = matmul_tn(u, dz, name="mm_dw_z", tn=1536)
    se1, se2 = slots_early(dwz, dwk, dwv, dwq, tile=tile)
    grad_x, dg_mix, re1, re2 = rowwise(proj_norm_bwd, [(dz, None), (xs, None), (dh1, None)], [wz, g_mix], [(D, F32)],
                                       [((1, D), F32)], tile=tile2, name="mm_du_norm_bwd", exchange=[se1, se2])

    small_grads = {"g_mix": dg_mix, "b_gate": db_gate, "g_q_lat": dg_q_lat, "g_kv_lat": dg_kv_lat, "g_ret": dg_ret,
                   "g_cross": dg_cross, "g_mem": dg_mem, "g_ffn": dg_ffn, "g_final": dg_final}
    rd, rs, dslots, own_s = exchange_small(cs_a, cs_b, [small_grads[n] for n, _, _ in SMALL_DIRECT])

    me = _my_id()

    def own(slots):
        return lax.dynamic_index_in_dim(slots, me, axis=0, keepdims=False)

    def wmv(names):
        return [shard(n, p) for n in names for p in ("", "m_", "v_")]

    names_s = tuple(n for n, _, _ in SMALL)
    groups = (
        (("w_in",), adam_cols(own(se1), re1, wmv(("w_in",)), ((0, NS_IN),), name="adam_e1", tile=128)),
        (("w_ukv", "w_uq"), adam_rows(own(se2), re2, wmv(("w_ukv", "w_uq")),
                                      ((0, E2_UQ, HP), (E2_UQ, E2_ROWS, NS_UQ)), name="adam_e2")),
        (("w_up", "w_xkv"), adam_cols(own(sl1), rl1, wmv(("w_up", "w_xkv")), ((0, NS_UP), (NS_UP, L1_W)),
                                      name="adam_l1", tile=128)),
        (("w_out", "w_xq", "w_xo", "w_down"),
         adam_rows(own(sl2), rl2, wmv(("w_out", "w_xq", "w_xo", "w_down")),
                   ((L2_OUT, L2_XQ, D), (L2_XQ, L2_XO, D), (L2_XO, L2_DN, D), (L2_DN, L2_ROWS, D)), name="adam_l2")),
        (("w_proj_mla", "w_proj_ret"), adam_rows(own(sl3), rl3, wmv(("w_proj_mla", "w_proj_ret")),
                                                 ((0, L3_PRET, HP), (L3_PRET, L3_ROWS, HP)), name="adam_l3")),
        (names_s + ("w_conv",), adam_small(own_s, rs, dslots, rd, wmv(names_s), wmv(("w_conv",)))),
    )
    res = {}
    for names, outs_ in groups:
        for t, n in enumerate(names):
            res[n] = outs_[4 * t:4 * t + 4]

    order = ["g_mix", "w_in", "b_gate", "g_q_lat", "w_uq", "g_kv_lat", "w_ukv", "w_proj_mla", "g_ret", "w_proj_ret",
             "w_out", "g_cross", "g_mem", "w_xq", "w_xkv", "w_xo", "g_ffn", "w_up", "w_conv", "b_conv", "w_down",
             "g_final"]
    outs = [loss, grad_x[None]]
    for kind in range(4):
        outs += [res[n][kind].reshape(args[n].shape) for n in order]
    return tuple(outs)
```

```python
import functools
import math

import jax
import jax.numpy as jnp
import numpy as np
from jax import lax
from jax.experimental import pallas as pl
from jax.experimental.pallas import tpu as pltpu

F32 = jnp.float32
BF16 = jnp.bfloat16

D = 1024
MLA_H, MLA_NOPE, MLA_ROPE, MLA_V = 8, 64, 32, 64
MLA_QR, MLA_KVR = 256, 128
RET_H, RET_D, RET_C = 4, 128, 128
X_H, X_HD = 4, 256
D_FF = 2816
THETA = 10000.0
EPS = 1e-6
HP = 128
ZW = 4608
N_DEV = 8

ADAM_LR, ADAM_B1, ADAM_B2, ADAM_EPS, ADAM_WD, ADAM_STEP = 0.001, 0.9, 0.999, 1e-08, 0.01, 10

VMEM_LIMIT = 56 * 1024 * 1024
BIG_CONST_BYTES = 4 * 1024 * 1024
MESH = pl.DeviceIdType.MESH
VM = pl.BlockSpec(memory_space=pltpu.VMEM)
ANY = pl.BlockSpec(memory_space=pl.ANY)


def _cp(n_axes):
    return pltpu.CompilerParams(dimension_semantics=("arbitrary",) * n_axes, vmem_limit_bytes=VMEM_LIMIT)


def _cp0():
    return pltpu.CompilerParams(vmem_limit_bytes=VMEM_LIMIT)


def _pick(n, cap, mult=128):
    best = None
    for t in range(mult, min(n, cap) + 1, mult):
        if n % t == 0:
            best = t
    return best if best is not None else n


def _dot(a, b):
    return jnp.dot(a, b, preferred_element_type=F32)


def _dot_nt(a, b):
    return lax.dot_general(a, b, (((1,), (1,)), ((), ())), preferred_element_type=F32)


def _dot_tn(a, b):
    return lax.dot_general(a, b, (((0,), (0,)), ((), ())), preferred_element_type=F32)


def _sds(shape, dtype):
    return jax.ShapeDtypeStruct(shape, dtype)


def matmul(a, b, *, name, tb=False, out_dtype=F32, tm=1024, tn=1024):
    M, K = a.shape
    N = b.shape[0] if tb else b.shape[1]
    tm = _pick(M, tm, 8)
    tn = _pick(N, tn)

    def body(a_ref, b_ref, o_ref):
        av = a_ref[...].astype(BF16)
        bv = b_ref[...].astype(BF16)
        acc = _dot_nt(av, bv) if tb else _dot(av, bv)
        o_ref[...] = acc.astype(o_ref.dtype)

    return pl.pallas_call(
        body,
        name=name,
        grid=(M // tm, N // tn),
        in_specs=[pl.BlockSpec((tm, K), lambda i, j: (i, 0)),
                  pl.BlockSpec((tn, K), lambda i, j: (j, 0)) if tb else pl.BlockSpec((K, tn), lambda i, j: (0, j))],
        out_specs=pl.BlockSpec((tm, tn), lambda i, j: (i, j)),
        out_shape=_sds((M, N), out_dtype),
        compiler_params=_cp(2),
    )(a, b)


def matmul_tn(a, b, *, name, tm=1024, tn=1024, tk=2048):
    R, M = a.shape
    N = b.shape[1]
    tm = _pick(M, tm)
    tn = _pick(N, tn)
    tk = _pick(R, tk, 16)
    nk = R // tk

    def body(a_ref, b_ref, o_ref, acc_ref):
        k = pl.program_id(2)

        @pl.when(k == 0)
        def _():
            acc_ref[...] = jnp.zeros_like(acc_ref)

        acc_ref[...] += _dot_tn(a_ref[...].astype(BF16), b_ref[...].astype(BF16))

        @pl.when(k == nk - 1)
        def _():
            o_ref[...] = acc_ref[...]

    return pl.pallas_call(
        body,
        name=name,
        grid=(M // tm, N // tn, nk),
        in_specs=[pl.BlockSpec((tk, tm), lambda i, j, k: (k, i)), pl.BlockSpec((tk, tn), lambda i, j, k: (k, j))],
        out_specs=pl.BlockSpec((tm, tn), lambda i, j, k: (i, j)),
        out_shape=_sds((M, N), F32),
        scratch_shapes=[pltpu.VMEM((tm, tn), F32)],
        compiler_params=_cp(3),
    )(a, b)


def rowwise(fn, rows, consts, out_rows, out_accs, *, tile, name, gather=(), exchange=()):
    T = rows[0][0].shape[0]
    nt = T // tile
    travel = tuple(gather) + tuple(exchange)
    n_r, n_c, n_o, n_a, n_g = len(rows), len(consts), len(out_rows), len(out_accs), len(travel)
    n_in = n_r + n_c + n_g

    def body(*refs):
        if n_g:
            srcs, dsts = refs[n_r + n_c:n_in], refs[n_in + n_o + n_a:n_in + n_o + n_a + n_g]
            sems = tuple(refs[n_in + n_o + n_a + n_g:])
            if gather:
                comm = (srcs, dsts) + sems
                comm_start, comm_finish = _gather2_start, _gather2_finish
            else:
                comm = ([(r, True) for r in srcs], dsts) + sems
                comm_start, comm_finish = _exchange_start, _exchange_wait

            @pl.when(pl.program_id(0) == 0)
            def _():
                comm_start(*comm)

        ins = [r[...] for r in refs[: n_r + n_c]]
        outs = fn(*ins)
        if not isinstance(outs, (tuple, list)):
            outs = (outs,)
        o_refs = refs[n_in:n_in + n_o]
        a_refs = refs[n_in + n_o:n_in + n_o + n_a]
        for o_ref, o in zip(o_refs, outs[:n_o]):
            o_ref[...] = o.astype(o_ref.dtype)
        if n_a:
            first = pl.program_id(0) == 0

            @pl.when(first)
            def _():
                for a_ref, o in zip(a_refs, outs[n_o:]):
                    a_ref[...] = o.astype(a_ref.dtype)

            @pl.when(jnp.logical_not(first))
            def _():
                for a_ref, o in zip(a_refs, outs[n_o:]):
                    a_ref[...] += o.astype(a_ref.dtype)
        if n_g:

            @pl.when(pl.program_id(0) == nt - 1)
            def _():
                comm_finish(*comm)

    in_specs = []
    args = []
    for arr, win in rows:
        if win is None:
            in_specs.append(pl.BlockSpec((tile, arr.shape[1]), lambda i: (i, 0)))
        else:
            w, cb = win
            in_specs.append(pl.BlockSpec((tile, w), functools.partial(lambda i, cb: (i, cb), cb=cb)))
        args.append(arr)
    for c in consts:
        index_map = functools.partial(lambda i, nd: (0,) * nd, nd=c.ndim)
        if c.size * c.dtype.itemsize >= BIG_CONST_BYTES:
            in_specs.append(pl.BlockSpec(c.shape, index_map, pipeline_mode=pl.Buffered(1)))
        else:
            in_specs.append(pl.BlockSpec(c.shape, index_map))
        args.append(c)
    out_specs = [pl.BlockSpec((tile, w), lambda i: (i, 0)) for w, _ in out_rows]
    out_shape = [_sds((T, w), dt) for w, dt in out_rows]
    for shp, dt in out_accs:
        out_specs.append(pl.BlockSpec(shp, functools.partial(lambda i, nd: (0,) * nd, nd=len(shp))))
        out_shape.append(_sds(shp, dt))
    return pl.pallas_call(
        body,
        name=name,
        grid=(nt,),
        in_specs=in_specs + [ANY] * n_g,
        out_specs=out_specs + [ANY] * n_g,
        out_shape=out_shape + [_sds((N_DEV,) + p.shape, p.dtype) for p in gather]
        + [_sds((N_DEV - 1,) + s.shape[1:], s.dtype) for s in exchange],
        scratch_shapes=(_gather_scratch(n_g) if gather else _exchange_scratch(n_g)) if n_g else [],
        compiler_params=_cp(1),
    )(*args, *travel)


def _rms(x, g):
    r = lax.rsqrt(jnp.mean(x * x, axis=-1, keepdims=True) + EPS)
    return x * r * g


def _rms_bwd(x, g, dy):
    r = lax.rsqrt(jnp.mean(x * x, axis=-1, keepdims=True) + EPS)
    xh = x * r
    dg = jnp.sum(dy * xh, axis=0, keepdims=True)
    dxh = dy * g
    dx = r * (dxh - xh * jnp.mean(dxh * xh, axis=-1, keepdims=True))
    return dx, dg


def _sigmoid(x):
    return 0.5 * jnp.tanh(0.5 * x) + 0.5


def _rot_mla(x, c, s1, s2):
    n = x.shape[1] // HP
    outs = []
    for h in range(n):
        xh = x[:, h * HP : (h + 1) * HP]
        outs.append(xh * c + pltpu.roll(xh, HP - 16, 1) * s1 + pltpu.roll(xh, 16, 1) * s2)
    return outs[0] if n == 1 else jnp.concatenate(outs, axis=1)


def _rot_ret(x, c, s):
    n = x.shape[1] // RET_D
    outs = []
    for h in range(n):
        xh = x[:, h * RET_D : (h + 1) * RET_D]
        outs.append(xh * c + pltpu.roll(xh, RET_D // 2, 1) * s)
    return outs[0] if n == 1 else jnp.concatenate(outs, axis=1)


def _rot_inv():
    lane_np = np.arange(128)
    inv_m = (jnp.asarray(THETA, F32) ** (-jnp.asarray(lane_np & 15, F32) / 16.0)).reshape(1, 128)
    inv_r = (jnp.asarray(THETA, F32) ** (-jnp.asarray(lane_np & 63, F32) / 64.0)).reshape(1, 128)
    return inv_m, inv_r


def _rot_tables(p, im, ir):
    lane = lax.broadcasted_iota(jnp.int32, p.shape, 1)
    ang = p * im
    cm = jnp.where((lane >= 64) & (lane < 96), jnp.cos(ang), 1.0)
    sn = jnp.sin(ang)
    s1 = jnp.where((lane >= 64) & (lane < 80), -sn, 0.0)
    s2 = jnp.where((lane >= 80) & (lane < 96), sn, 0.0)
    angr = p * ir
    snr = jnp.sin(angr)
    return cm, s1, s2, jnp.cos(angr), jnp.where(lane < 64, -snr, snr)


def _peer(m):
    x, y, c = lax.axis_index("x"), lax.axis_index("y"), lax.axis_index("c")
    mx, my, mc = (m >> 2) & 1, (m >> 1) & 1, m & 1
    px = 1 - x if mx else x
    py = 1 - y if my else y
    pc = 1 - c if mc else c
    return (px, py, pc), 4 * px + 2 * py + pc


def _my_id():
    return 4 * lax.axis_index("x") + 2 * lax.axis_index("y") + lax.axis_index("c")


def _gather_copies(srcs, outs, send_sems, recv_sems, local_sems, arriving=False):
    me = _my_id()
    copies = []
    if not arriving:
        for g, (src, out) in enumerate(zip(srcs, outs)):
            copies.append((pltpu.make_async_copy(src, out.at[me], local_sems.at[g]), False))
    for m in range(1, N_DEV):
        peer, plin = _peer(m)
        for g, (src, out) in enumerate(zip(srcs, outs)):
            copies.append((pltpu.make_async_remote_copy(
                src_ref=src, dst_ref=out.at[plin if arriving else me], send_sem=send_sems.at[g, m - 1],
                recv_sem=recv_sems.at[g, m - 1], device_id=peer, device_id_type=MESH), True))
    return copies


def _gather_start(*a):
    for cp, _ in _gather_copies(*a):
        cp.start()


def _gather_wait(*a):
    for cp, _ in _gather_copies(*a, arriving=True):
        cp.wait_recv()
    for cp, remote in _gather_copies(*a):
        if remote:
            cp.wait_send()
        else:
            cp.wait()


CHIP_RELATIONS = (4, 2, 6)


def _gather2_copy(src, out, block, to, send_sems, recv_sems, g, k):
    return pltpu.make_async_remote_copy(src_ref=src, dst_ref=out.at[block], send_sem=send_sems.at[g, k],
                                        recv_sem=recv_sems.at[g, k], device_id=to, device_id_type=MESH)


def _gather2_start(srcs, outs, send_sems, recv_sems, local_sems):
    me = _my_id()
    sib, _ = _peer(1)
    for g, (src, out) in enumerate(zip(srcs, outs)):
        pltpu.make_async_copy(src, out.at[me], local_sems.at[g]).start()
        _gather2_copy(src, out, me, sib, send_sems, recv_sems, g, 0).start()
        for t, m in enumerate(CHIP_RELATIONS):
            _gather2_copy(src, out, me, _peer(m)[0], send_sems, recv_sems, g, 1 + t).start()


def _gather2_finish(srcs, outs, send_sems, recv_sems, local_sems):
    me = _my_id()
    sib, sib_lin = _peer(1)
    for t, m in enumerate(CHIP_RELATIONS):
        peer, plin = _peer(m)
        for g, (src, out) in enumerate(zip(srcs, outs)):
            _gather2_copy(src, out, plin, peer, send_sems, recv_sems, g, 1 + t).wait_recv()
            _gather2_copy(out.at[plin], out, plin, sib, send_sems, recv_sems, g, 4 + t).start()
    for g, (src, out) in enumerate(zip(srcs, outs)):
        _gather2_copy(src, out, sib_lin, sib, send_sems, recv_sems, g, 0).wait_recv()
        for t, m in enumerate(CHIP_RELATIONS):
            _gather2_copy(src, out, _peer(m | 1)[1], sib, send_sems, recv_sems, g, 4 + t).wait_recv()
    for g, (src, out) in enumerate(zip(srcs, outs)):
        _gather2_copy(src, out, me, sib, send_sems, recv_sems, g, 0).wait_send()
        for t, m in enumerate(CHIP_RELATIONS):
            peer, plin = _peer(m)
            _gather2_copy(src, out, me, peer, send_sems, recv_sems, g, 1 + t).wait_send()
            _gather2_copy(out.at[plin], out, plin, sib, send_sems, recv_sems, g, 4 + t).wait_send()
        pltpu.make_async_copy(src, out.at[me], local_sems.at[g]).wait()


def _gather_scratch(n):
    return [pltpu.SemaphoreType.DMA((n, N_DEV - 1)), pltpu.SemaphoreType.DMA((n, N_DEV - 1)), pltpu.SemaphoreType.DMA((n,))]


def _exchange_copies(srcs, dsts, send_sems, recv_sems):
    copies = []
    for m in range(1, N_DEV):
        peer, plin = _peer(m)
        for g, ((src, per_peer), dst) in enumerate(zip(srcs, dsts)):
            copies.append(pltpu.make_async_remote_copy(
                src_ref=src.at[plin] if per_peer else src, dst_ref=dst.at[m - 1], send_sem=send_sems.at[g, m - 1],
                recv_sem=recv_sems.at[g, m - 1], device_id=peer, device_id_type=MESH))
    return copies


def _exchange_start(*a):
    for cp in _exchange_copies(*a):
        cp.start()


def _exchange_wait(*a):
    copies = _exchange_copies(*a)
    for cp in copies:
        cp.wait_recv()
    for cp in copies:
        cp.wait_send()


def _exchange_scratch(n):
    return [pltpu.SemaphoreType.DMA((n, N_DEV - 1)), pltpu.SemaphoreType.DMA((n, N_DEV - 1))]


MLA_SCALE = (MLA_NOPE + MLA_ROPE) ** -0.5
MLA_C2 = MLA_SCALE * math.log2(math.e)
ONE_LANE = MLA_V
NEG = -1e30
HPS = 4
HPB = 4


def _tri_mask(n, lower_rows_ge_cols=True):
    r = lax.broadcasted_iota(jnp.int32, (n, n), 0)
    c = lax.broadcasted_iota(jnp.int32, (n, n), 1)
    return r >= c if lower_rows_ge_cols else c >= r


def mla_fwd(q, k, v, gather, *, tq):
    T = q.shape[0]
    nq = T // tq
    rep = tq // HP
    ng = len(gather)

    def body(*refs):
        q_ref, k_ref, v_ref = refs[:3]
        srcs = refs[3:3 + ng]
        o_ref, lse_ref = refs[3 + ng:5 + ng]
        outs = refs[5 + ng:5 + 2 * ng]
        m_sc, acc_sc = refs[5 + 2 * ng:7 + 2 * ng]
        comm = (srcs, outs) + tuple(refs[7 + 2 * ng:])
        h, i = pl.program_id(0), pl.program_id(1)

        @pl.when((h == 0) & (i == 0))
        def _():
            _gather_start(*comm)

        m_sc[...] = jnp.full(m_sc.shape, NEG, F32)
        acc_sc[...] = jnp.zeros(acc_sc.shape, F32)
        heads = [slice(t * HP, (t + 1) * HP) for t in range(HPS)]

        def block(j, masked):
            off = pl.multiple_of(j * tq, tq)
            ss = [_dot_nt(q_ref[:, hd], k_ref[pl.ds(off, tq), hd]) * MLA_C2 for hd in heads]
            mask = _tri_mask(tq) if masked else None
            for hd, s in zip(heads, ss):
                if masked:
                    s = jnp.where(mask, s, NEG)
                m_prev = m_sc[:, hd]
                m_next = jnp.maximum(m_prev, jnp.max(s, axis=-1, keepdims=True))
                p = jnp.exp2(s - jnp.tile(m_next, (1, rep)))
                alpha = jnp.exp2(m_prev - m_next)
                acc_sc[:, hd] = alpha * acc_sc[:, hd] + _dot(p.astype(BF16), v_ref[pl.ds(off, tq), hd])
                m_sc[:, hd] = m_next

        def loop_body(j, carry):
            block(j, False)
            return carry

        lax.fori_loop(0, i, loop_body, 0)
        block(i, True)
        for hd in heads:
            acc = acc_sc[:, hd]
            l = acc[:, ONE_LANE:ONE_LANE + 1]
            o_ref[:, hd] = (acc / l).astype(o_ref.dtype)
            lse_ref[hd.start // HP] = (m_sc[:, hd] + jnp.log(l) * math.log2(math.e)).T[0:1, :]

        @pl.when((h == MLA_H // HPS - 1) & (i == nq - 1))
        def _():
            _gather_wait(*comm)

    blk = pl.BlockSpec((tq, HPS * HP), lambda h, i: (i, h))
    full = pl.BlockSpec((T, HPS * HP), lambda h, i: (0, h))
    return pl.pallas_call(
        body,
        name="mla_fwd",
        grid=(MLA_H // HPS, nq),
        in_specs=[blk, full, full] + [ANY] * ng,
        out_specs=[blk, pl.BlockSpec((HPS, None, 1, tq), lambda h, i: (h, i, 0, 0))] + [ANY] * ng,
        out_shape=[_sds((T, MLA_H * HP), BF16), _sds((MLA_H, nq, 1, tq), F32)]
        + [_sds((N_DEV,) + p.shape, p.dtype) for p in gather],
        scratch_shapes=[pltpu.VMEM((tq, HPS * HP), F32), pltpu.VMEM((tq, HPS * HP), F32)] + _gather_scratch(ng),
        compiler_params=_cp(2),
    )(q, k, v, *gather)


def mla_prep(delta, *, tq):
    T = delta.shape[0]
    nq = T // tq

    def body(d_ref, row_ref):
        dt = d_ref[...].T
        for h in range(MLA_H):
            row_ref[h] = dt[h:h + 1, :]

    return pl.pallas_call(
        body,
        name="mla_prep",
        grid=(nq,),
        in_specs=[pl.BlockSpec((tq, HP), lambda i: (i, 0))],
        out_specs=pl.BlockSpec((MLA_H, None, 1, tq), lambda i: (0, i, 0, 0)),
        out_shape=_sds((MLA_H, nq, 1, tq), F32),
        compiler_params=_cp(1),
    )(delta)


def mla_bwd(q, k, v, do, lse_row, delta_row, slots, *, tq):
    T = q.shape[0]
    nq = T // tq
    ns = len(slots)

    def body(*refs):
        q_ref, k_ref, v_ref, do_ref, lse_ref, delta_ref = refs[:6]
        srcs = [(r, True) for r in refs[6:6 + ns]]
        dq_ref, dk_ref, dv_ref = refs[6 + ns:9 + ns]
        dsts = refs[9 + ns:9 + 2 * ns]
        dk_sc, dv_sc = refs[9 + 2 * ns:11 + 2 * ns]
        comm = (srcs, dsts) + tuple(refs[11 + 2 * ns:])
        h, j = pl.program_id(0), pl.program_id(1)

        @pl.when((h == 0) & (j == 0))
        def _():
            _exchange_start(*comm)

        dk_sc[...] = jnp.zeros(dk_sc.shape, F32)
        dv_sc[...] = jnp.zeros(dv_sc.shape, F32)
        heads = [slice(t * HP, (t + 1) * HP) for t in range(HPB)]

        @pl.when(j == 0)
        def _():
            dq_ref[...] = jnp.zeros(dq_ref.shape, F32)

        def block(i, masked):
            off = pl.multiple_of(i * tq, tq)
            sts = [_dot_nt(k_ref[:, hd], q_ref[pl.ds(off, tq), hd]) * MLA_C2 for hd in heads]
            dpts = [_dot_nt(v_ref[:, hd], do_ref[pl.ds(off, tq), hd]) for hd in heads]
            mask = _tri_mask(tq, False) if masked else None
            for t, hd in enumerate(heads):
                st = sts[t]
                if masked:
                    st = jnp.where(mask, st, NEG)
                pt = jnp.exp2(st - lse_ref[t, i])
                dv_sc[:, hd] += _dot(pt.astype(BF16), do_ref[pl.ds(off, tq), hd])
                dst = (pt * (dpts[t] - delta_ref[t, i]) * MLA_SCALE).astype(BF16)
                dk_sc[:, hd] += _dot(dst, q_ref[pl.ds(off, tq), hd])
                dq_ref[pl.ds(off, tq), hd] += _dot_tn(dst, k_ref[:, hd])

        block(j, True)

        def loop_body(i, carry):
            block(i, False)
            return carry

        lax.fori_loop(j + 1, nq, loop_body, 0)
        dk_ref[...] = dk_sc[...].astype(dk_ref.dtype)
        dv_ref[...] = dv_sc[...].astype(dv_ref.dtype)

        @pl.when((h == MLA_H // HPB - 1) & (j == nq - 1))
        def _():
            _exchange_wait(*comm)

    blk = pl.BlockSpec((tq, HPB * HP), lambda h, j: (j, h))
    full = pl.BlockSpec((T, HPB * HP), lambda h, j: (0, h), pipeline_mode=pl.Buffered(1))
    rows = pl.BlockSpec((HPB, nq, 1, tq), lambda h, j: (h, 0, 0, 0))
    return pl.pallas_call(
        body,
        name="mla_bwd",
        grid=(MLA_H // HPB, nq),
        in_specs=[full, blk, blk, full, rows, rows] + [ANY] * ns,
        out_specs=[full, blk, blk] + [ANY] * ns,
        out_shape=[_sds((T, MLA_H * HP), F32), _sds((T, MLA_H * HP), BF16), _sds((T, MLA_H * HP), BF16)]
        + [_sds((N_DEV - 1,) + s.shape[1:], s.dtype) for s in slots],
        scratch_shapes=[pltpu.VMEM((tq, HPB * HP), F32), pltpu.VMEM((tq, HPB * HP), F32)] + _exchange_scratch(ns),
        compiler_params=_cp(2),
    )(q, k, v, do, lse_row, delta_row, *slots)


def _ret_consts():
    h = jnp.arange(RET_H, dtype=F32)
    log_g = jnp.log1p(-jnp.exp2(-5.0 - h))
    idx = jnp.arange(RET_C, dtype=F32)
    rel = idx[:, None] - idx[None, :]
    dmask = jnp.where(rel >= 0, jnp.exp(log_g[:, None, None] * jnp.maximum(rel, 0.0)), 0.0)
    zeta = jnp.exp(log_g[:, None] * (RET_C - 1.0 - idx)[None, :])
    xi = jnp.exp(log_g[:, None] * (idx + 1.0)[None, :])
    decay = jnp.exp(log_g * RET_C)
    zb = jnp.broadcast_to(zeta[:, :, None], (RET_H, RET_C, RET_D))
    xb = jnp.broadcast_to(xi[:, :, None], (RET_H, RET_C, RET_D))
    db = jnp.broadcast_to(decay[:, None, None], (RET_H, RET_C, RET_D))
    return dmask.astype(F32), zb.astype(F32), xb.astype(F32), db.astype(F32)


def ret_fwd(rq, rk, rv, consts, *, rb):
    T = rq.shape[0]
    nb = T // rb
    ncb = rb // RET_C

    def body(q_ref, k_ref, v_ref, dm_ref, z_ref, x_ref, dc_ref, o_ref, st_ref, r_sc):
        @pl.when(pl.program_id(0) == 0)
        def _():
            r_sc[...] = jnp.zeros(r_sc.shape, F32)

        for c in range(ncb):
            sl = slice(c * RET_C, (c + 1) * RET_C)
            for h in range(RET_H):
                hd = slice(h * RET_D, (h + 1) * RET_D)
                q, k, v = q_ref[sl, hd], k_ref[sl, hd], v_ref[sl, hd]
                r = r_sc[h]
                rbf = r.astype(BF16)
                st_ref[sl, hd] = rbf
                s = _dot_nt(q, k) * dm_ref[h]
                inner = _dot(s.astype(BF16), v)
                cross = _dot((q.astype(F32) * x_ref[h]).astype(BF16), rbf)
                o_ref[sl, hd] = inner + cross
                kz = (k.astype(F32) * z_ref[h]).T.astype(BF16)
                r_sc[h] = r * dc_ref[h] + _dot(kz, v)

    blk = pl.BlockSpec((rb, RET_H * RET_D), lambda b: (b, 0))
    cst = pl.BlockSpec((RET_H, RET_C, RET_D), lambda b: (0, 0, 0))
    return pl.pallas_call(
        body,
        name="ret_fwd",
        grid=(nb,),
        in_specs=[blk, blk, blk, cst, cst, cst, cst],
        out_specs=[blk, blk],
        out_shape=[_sds((T, RET_H * RET_D), F32), _sds((T, RET_H * RET_D), BF16)],
        scratch_shapes=[pltpu.VMEM((RET_H, RET_D, RET_D), F32)],
        compiler_params=_cp(1),
    )(rq, rk, rv, *consts)


def ret_bwd(rq, rk, rv, st, dret, consts, *, rb):
    T = rq.shape[0]
    nb = T // rb
    ncb = rb // RET_C

    def body(q_ref, k_ref, v_ref, st_ref, do_ref, dm_ref, z_ref, x_ref, dc_ref, dq_ref, dk_ref, dv_ref, g_sc):
        @pl.when(pl.program_id(0) == 0)
        def _():
            g_sc[...] = jnp.zeros(g_sc.shape, F32)

        for c in reversed(range(ncb)):
            sl = slice(c * RET_C, (c + 1) * RET_C)
            for h in range(RET_H):
                hd = slice(h * RET_D, (h + 1) * RET_D)
                dm, zt, xi = dm_ref[h], z_ref[h], x_ref[h]
                q, k, v, rp = q_ref[sl, hd], k_ref[sl, hd], v_ref[sl, hd], st_ref[sl, hd]
                dob = do_ref[sl, hd].astype(BF16)
                qf, kf = q.astype(F32), k.astype(F32)
                gn = g_sc[h]
                gnb = gn.astype(BF16)
                s = _dot_nt(q, k) * dm
                ds = _dot_nt(dob, v) * dm
                dq = _dot(ds.astype(BF16), k) + _dot_nt(dob, rp) * xi
                dk = _dot(ds.T.astype(BF16), q) + _dot_nt(v, gnb) * zt
                dv = _dot(s.T.astype(BF16), dob) + _dot((kf * zt).astype(BF16), gnb)
                dq_ref[sl, hd] = dq.astype(dq_ref.dtype)
                dk_ref[sl, hd] = dk.astype(dk_ref.dtype)
                dv_ref[sl, hd] = dv.astype(dv_ref.dtype)
                g_sc[h] = _dot((qf * xi).T.astype(BF16), dob) + dc_ref[h] * gn

    blk = pl.BlockSpec((rb, RET_H * RET_D), lambda b: (nb - 1 - b, 0))
    cst = pl.BlockSpec((RET_H, RET_C, RET_D), lambda b: (0, 0, 0))
    return pl.pallas_call(
        body,
        name="ret_bwd",
        grid=(nb,),
        in_specs=[blk, blk, blk, blk, blk, cst, cst, cst, cst],
        out_specs=[blk, blk, blk],
        out_shape=[_sds((T, RET_H * RET_D), F32), _sds((T, RET_H * RET_D), F32), _sds((T, RET_H * RET_D), BF16)],
        scratch_shapes=[pltpu.VMEM((RET_H, RET_D, RET_D), F32)],
        compiler_params=_cp(1),
    )(rq, rk, rv, st, dret, *consts)


HALO = 16


def conv_act_fwd(up_pre, w_conv, b_conv, *, tile, cw):
    T = up_pre.shape[0]
    nt = T // tile
    ncol = D_FF // cw
    hb = tile // HALO

    def body(pa_ref, a_ref, pb_ref, b_ref, wa_ref, wb_ref, ba_ref, bb_ref, o_ref):
        i = pl.program_id(1)
        keep = (i > 0).astype(F32)

        def conv(prev_ref, cur_ref, w_ref, bias_ref):
            ext = jnp.concatenate([prev_ref[...].astype(F32) * keep, cur_ref[...].astype(F32)], axis=0)
            w = w_ref[...]
            y = ext * w[2:3, :] + pltpu.roll(ext, 1, 0) * w[1:2, :] + pltpu.roll(ext, 2, 0) * w[0:1, :] + bias_ref[...]
            return y[HALO:, :]

        a = conv(pa_ref, a_ref, wa_ref, ba_ref)
        b = conv(pb_ref, b_ref, wb_ref, bb_ref)
        o_ref[...] = (a * _sigmoid(a) * b).astype(o_ref.dtype)

    prev_a = pl.BlockSpec((HALO, cw), lambda j, i: (jnp.maximum(i * hb - 1, 0), j))
    cur_a = pl.BlockSpec((tile, cw), lambda j, i: (i, j))
    prev_b = pl.BlockSpec((HALO, cw), lambda j, i: (jnp.maximum(i * hb - 1, 0), j + ncol))
    cur_b = pl.BlockSpec((tile, cw), lambda j, i: (i, j + ncol))
    w_a = pl.BlockSpec((3, cw), lambda j, i: (0, j))
    w_b = pl.BlockSpec((3, cw), lambda j, i: (0, j + ncol))
    bias_a = pl.BlockSpec((1, cw), lambda j, i: (0, j))
    bias_b = pl.BlockSpec((1, cw), lambda j, i: (0, j + ncol))
    return pl.pallas_call(
        body,
        name="conv_act_fwd",
        grid=(ncol, nt),
        in_specs=[prev_a, cur_a, prev_b, cur_b, w_a, w_b, bias_a, bias_b],
        out_specs=pl.BlockSpec((tile, cw), lambda j, i: (i, j)),
        out_shape=_sds((T, D_FF), BF16),
        compiler_params=_cp(2),
    )(up_pre, up_pre, up_pre, up_pre, w_conv, w_conv, b_conv, b_conv)


def conv_act_bwd(up_pre, dact, w_conv, b_conv, *, tile, cw):
    T = up_pre.shape[0]
    nt = T // tile
    ncol = D_FF // cw
    hb = tile // HALO
    ext_rows = tile + 2 * HALO

    def body(pa_ref, a_ref, na_ref, pb_ref, b_ref, nb_ref, d_ref, nd_ref, wa_ref, wb_ref, ba_ref, bb_ref,
             dxa_ref, dxb_ref, sa_ref, sb_ref):
        i = pl.program_id(1)
        keep_p = (i > 0).astype(F32)
        keep_n = (i < nt - 1).astype(F32)

        def ext_of(prev_ref, cur_ref, next_ref):
            return jnp.concatenate(
                [prev_ref[...].astype(F32) * keep_p, cur_ref[...].astype(F32), next_ref[...].astype(F32) * keep_n], axis=0)

        def taps(ext):
            return ext, pltpu.roll(ext, 1, 0), pltpu.roll(ext, 2, 0)

        def conv(tp, w, bias):
            return tp[0] * w[2:3, :] + tp[1] * w[1:2, :] + tp[2] * w[0:1, :] + bias

        xa = taps(ext_of(pa_ref, a_ref, na_ref))
        xb = taps(ext_of(pb_ref, b_ref, nb_ref))
        wa, wb = wa_ref[...], wb_ref[...]
        a = conv(xa, wa, ba_ref[...])
        b = conv(xb, wb, bb_ref[...])
        dy = jnp.concatenate(
            [jnp.zeros((HALO, cw), F32), d_ref[...].astype(F32), nd_ref[...].astype(F32) * keep_n], axis=0)
        sg = _sigmoid(a)
        da = dy * b * (sg * (1.0 + a * (1.0 - sg)))
        db = dy * (a * sg)

        def back(dup, tp, w, dx_ref, s_ref):
            dx = dup * w[2:3, :] + pltpu.roll(dup, ext_rows - 1, 0) * w[1:2, :] + pltpu.roll(dup, ext_rows - 2, 0) * w[0:1, :]
            dx_ref[...] = dx[HALO:HALO + tile, :].astype(dx_ref.dtype)
            dc = dup[HALO:HALO + tile, :]
            r2 = jnp.sum(dc * tp[0][HALO:HALO + tile, :], axis=0, keepdims=True)
            r1 = jnp.sum(dc * tp[1][HALO:HALO + tile, :], axis=0, keepdims=True)
            r0 = jnp.sum(dc * tp[2][HALO:HALO + tile, :], axis=0, keepdims=True)
            rb = jnp.sum(dc, axis=0, keepdims=True)
            row = lax.broadcasted_iota(jnp.int32, (8, cw), 0)
            upd = (jnp.where(row == 0, r0, 0.0) + jnp.where(row == 1, r1, 0.0) + jnp.where(row == 2, r2, 0.0)
                   + jnp.where(row == 3, rb, 0.0))

            @pl.when(i == 0)
            def _():
                s_ref[...] = upd

            @pl.when(i > 0)
            def _():
                s_ref[...] += upd

        back(da, xa, wa, dxa_ref, sa_ref)
        back(db, xb, wb, dxb_ref, sb_ref)

    def prev_of(shift):
        return pl.BlockSpec((HALO, cw), lambda j, i: (jnp.maximum(i * hb - 1, 0), j + shift))

    def next_of(shift):
        return pl.BlockSpec((HALO, cw), lambda j, i: (jnp.minimum((i + 1) * hb, nt * hb - 1), j + shift))

    def cur_of(shift):
        return pl.BlockSpec((tile, cw), lambda j, i: (i, j + shift))

    def row_of(rows, shift):
        return pl.BlockSpec((rows, cw), lambda j, i: (0, j + shift))

    return pl.pallas_call(
        body,
        name="conv_act_bwd",
        grid=(ncol, nt),
        in_specs=[prev_of(0), cur_of(0), next_of(0), prev_of(ncol), cur_of(ncol), next_of(ncol), cur_of(0), next_of(0),
                  row_of(3, 0), row_of(3, ncol), row_of(1, 0), row_of(1, ncol)],
        out_specs=[cur_of(0), cur_of(0), row_of(8, 0), row_of(8, 0)],
        out_shape=[_sds((T, D_FF), BF16), _sds((T, D_FF), BF16), _sds((8, D_FF), F32), _sds((8, D_FF), F32)],
        compiler_params=_cp(2),
    )(up_pre, up_pre, up_pre, up_pre, up_pre, up_pre, dact, dact, w_conv, w_conv, b_conv, b_conv)


NS_IN, NS_UP, NS_XKV, NS_UQ = 564, 704, 256, 96
E2_ROWS, E2_UQ = 384, 128
L1_W = NS_UP + NS_XKV
L2_ROWS = 736
L2_OUT, L2_XQ, L2_XO, L2_DN = 0, 128, 256, 384
RS_DN = 352
L3_ROWS, L3_PRET = 1024, 512
L4_SHAPE = (8, 768)
WZ_RUNS = ((416, 4096, 0), (0, 384, 4096), (384, 32, 4544))
WZ_ZERO = ((4480, 4544), (4576, 4608))


def _pieces(orig_start, length, dst_start, ns):
    out, c, d, end = [], orig_start, dst_start, orig_start + length
    while c < end:
        j, off = c // ns, c % ns
        ln = min(ns - off, end - c)
        out.append((j, off, d, ln))
        c += ln
        d += ln
    return out


def pack_early(w_in, w_ukv, w_uq):
    def body(in_ref, ukv_ref, uq_ref, e1_ref, e2_ref):
        e1_ref[...] = in_ref[...].astype(BF16)
        e2_ref[0:E2_UQ, :] = ukv_ref[...].astype(BF16)
        e2_ref[E2_UQ:E2_ROWS, 0:NS_UQ] = uq_ref[...].astype(BF16)
        e2_ref[E2_UQ:E2_ROWS, NS_UQ:HP] = jnp.zeros((E2_ROWS - E2_UQ, HP - NS_UQ), BF16)

    return pl.pallas_call(
        body,
        name="pack_early",
        in_specs=[VM] * 3,
        out_specs=[VM] * 2,
        out_shape=[_sds((D, NS_IN), BF16), _sds((E2_ROWS, HP), BF16)],
        compiler_params=_cp0(),
    )(w_in, w_ukv, w_uq)


def pack_late(w_up, w_xkv, w_out, w_xq, w_xo, w_down, w_pmla, w_pret, w_conv):
    def body(up_ref, xkv_ref, o_ref, xq_ref, xo_ref, dn_ref, pm_ref, pr_ref, cv_ref, l1_ref, l2_ref, l3_ref, l4_ref):
        l1_ref[:, 0:NS_UP] = up_ref[...].astype(BF16)
        l1_ref[:, NS_UP:L1_W] = xkv_ref[...].astype(BF16)
        l2_ref[L2_OUT:L2_XQ, :] = o_ref[...].astype(BF16)
        l2_ref[L2_XQ:L2_XO, :] = xq_ref[...].astype(BF16)
        l2_ref[L2_XO:L2_DN, :] = xo_ref[...].astype(BF16)
        l2_ref[L2_DN:L2_ROWS, :] = dn_ref[...].astype(BF16)
        l3_ref[0:L3_PRET, :] = pm_ref[...].astype(BF16)
        l3_ref[L3_PRET:L3_ROWS, :] = pr_ref[...].astype(BF16)
        l4_ref[...] = jnp.zeros(L4_SHAPE, F32)
        l4_ref[0:3, 0:NS_UP] = cv_ref[...]

    return pl.pallas_call(
        body,
        name="pack_late",
        in_specs=[VM] * 9,
        out_specs=[VM] * 4,
        out_shape=[_sds((D, L1_W), BF16), _sds((L2_ROWS, D), BF16), _sds((L3_ROWS, HP), BF16), _sds(L4_SHAPE, F32)],
        compiler_params=_cp0(),
    )(w_up, w_xkv, w_out, w_xq, w_xo, w_down, w_pmla, w_pret, w_conv)


def assemble_early(g1, g2, *, tile):
    def body(g1_ref, g2_ref, wz_ref, wk_ref, wv_ref, wq_ref):
        for lo, hi in WZ_ZERO:
            wz_ref[:, lo:hi] = jnp.zeros((tile, hi - lo), BF16)
        for os_, ln_, ds_ in WZ_RUNS:
            for j, off, d, ln in _pieces(os_, ln_, ds_, NS_IN):
                wz_ref[:, d:d + ln] = g1_ref[j, :, off:off + ln]

        @pl.when(pl.program_id(0) == 0)
        def _():
            half = jnp.zeros((MLA_KVR, HP - MLA_NOPE), BF16)
            for j in range(N_DEV):
                wk_ref[:, j * HP:j * HP + MLA_NOPE] = g2_ref[j, 0:E2_UQ, 0:MLA_NOPE]
                wk_ref[:, j * HP + MLA_NOPE:(j + 1) * HP] = half
                wv_ref[:, j * HP:j * HP + MLA_V] = g2_ref[j, 0:E2_UQ, MLA_NOPE:HP]
                wv_ref[:, j * HP + MLA_V:(j + 1) * HP] = half
                wq_ref[:, j * HP:j * HP + NS_UQ] = g2_ref[j, E2_UQ:E2_ROWS, 0:NS_UQ]
                wq_ref[:, j * HP + NS_UQ:(j + 1) * HP] = jnp.zeros((MLA_QR, HP - NS_UQ), BF16)

    def whole(r):
        return pl.BlockSpec((r, MLA_H * HP), lambda i: (0, 0))

    return pl.pallas_call(
        body,
        name="assemble_early",
        grid=(D // tile,),
        in_specs=[pl.BlockSpec((N_DEV, tile, NS_IN), lambda i: (0, i, 0)),
                  pl.BlockSpec((N_DEV, E2_ROWS, HP), lambda i: (0, 0, 0))],
        out_specs=[pl.BlockSpec((tile, ZW), lambda i: (i, 0)), whole(MLA_KVR), whole(MLA_KVR), whole(MLA_QR)],
        out_shape=[_sds((D, ZW), BF16), _sds((MLA_KVR, MLA_H * HP), BF16), _sds((MLA_KVR, MLA_H * HP), BF16),
                   _sds((MLA_QR, MLA_H * HP), BF16)],
        compiler_params=_cp(1),
    )(g1, g2)


def assemble_l1(g1, *, tile):
    def body(g_ref, wup_ref, wxkv_ref):
        for j in range(N_DEV):
            wup_ref[:, j * NS_UP:(j + 1) * NS_UP] = g_ref[j, :, 0:NS_UP]
            wxkv_ref[:, j * NS_XKV:(j + 1) * NS_XKV] = g_ref[j, :, NS_UP:L1_W]

    return pl.pallas_call(
        body,
        name="assemble_l1",
        grid=(D // tile,),
        in_specs=[pl.BlockSpec((N_DEV, tile, L1_W), lambda i: (0, i, 0))],
        out_specs=[pl.BlockSpec((tile, 2 * D_FF), lambda i: (i, 0)), pl.BlockSpec((tile, 2 * D), lambda i: (i, 0))],
        out_shape=[_sds((D, 2 * D_FF), BF16), _sds((D, 2 * D), BF16)],
        compiler_params=_cp(1),
    )(g1)


def assemble_l234(g2, g3, g4):
    def body(g2_ref, g3_ref, g4_ref, wo_ref, wxq_ref, wxo_ref, wdn_ref, wpa_ref, wpr_ref, wc_ref):
        for j in range(N_DEV):
            wo_ref[j * 128:(j + 1) * 128, :] = g2_ref[j, L2_OUT:L2_XQ, :]
            wxq_ref[j * 128:(j + 1) * 128, :] = g2_ref[j, L2_XQ:L2_XO, :]
            wxo_ref[j * 128:(j + 1) * 128, :] = g2_ref[j, L2_XO:L2_DN, :]
            wdn_ref[j * RS_DN:(j + 1) * RS_DN, :] = g2_ref[j, L2_DN:L2_ROWS, :]
            wpr_ref[:, j * 128:(j + 1) * 128] = g3_ref[j, L3_PRET:L3_ROWS, :]
            wc_ref[:, j * NS_UP:(j + 1) * NS_UP] = g4_ref[j, 0:3, 0:NS_UP]
            for h in range(MLA_H):
                wpa_ref[h * HP:h * HP + MLA_V, j * 128:(j + 1) * 128] = g3_ref[j, h * MLA_V:(h + 1) * MLA_V, :]
        for h in range(MLA_H):
            wpa_ref[h * HP + MLA_V:(h + 1) * HP, :] = jnp.zeros((HP - MLA_V, D), BF16)

    return pl.pallas_call(
        body,
        name="assemble_l234",
        in_specs=[VM] * 3,
        out_specs=[VM] * 7,
        out_shape=[_sds((D, D), BF16), _sds((D, D), BF16), _sds((D, D), BF16), _sds((D_FF, D), BF16),
                   _sds((MLA_H * HP, D), BF16), _sds((RET_H * RET_D, D), BF16), _sds((3, 2 * D_FF), F32)],
        compiler_params=_cp0(),
    )(g2, g3, g4)


def slots_l1(dwup_a, dwup_b, dwxkv, *, tile):
    per_half = D_FF // NS_UP

    def body(ua_ref, ub_ref, x_ref, s_ref):
        for j in range(N_DEV):
            src = ua_ref if j < per_half else ub_ref
            c0 = (j % per_half) * NS_UP
            s_ref[j, :, 0:NS_UP] = src[:, c0:c0 + NS_UP].astype(BF16)
            s_ref[j, :, NS_UP:L1_W] = x_ref[:, j * NS_XKV:(j + 1) * NS_XKV].astype(BF16)

    return pl.pallas_call(
        body,
        name="slots_l1",
        grid=(D // tile,),
        in_specs=[pl.BlockSpec((tile, D_FF), lambda i: (i, 0)), pl.BlockSpec((tile, D_FF), lambda i: (i, 0)),
                  pl.BlockSpec((tile, 2 * D), lambda i: (i, 0))],
        out_specs=pl.BlockSpec((N_DEV, tile, L1_W), lambda i: (0, i, 0)),
        out_shape=_sds((N_DEV, D, L1_W), BF16),
        compiler_params=_cp(1),
    )(dwup_a, dwup_b, dwxkv)


def slots_l23(dwo, dwxq, dwxo, dwdn, dwpa, dwpr):
    def body(o_ref, xq_ref, xo_ref, dn_ref, pa_ref, pr_ref, l2_ref, l3_ref):
        l2_ref[L2_OUT:L2_XQ, :] = o_ref[...].astype(BF16)
        l2_ref[L2_XQ:L2_XO, :] = xq_ref[...].astype(BF16)
        l2_ref[L2_XO:L2_DN, :] = xo_ref[...].astype(BF16)
        l2_ref[L2_DN:L2_ROWS, :] = dn_ref[...].astype(BF16)
        for h in range(MLA_H):
            l3_ref[h * MLA_V:(h + 1) * MLA_V, :] = pa_ref[h * HP:h * HP + MLA_V, :].astype(BF16)
        l3_ref[L3_PRET:L3_ROWS, :] = pr_ref[...].astype(BF16)

    rows128 = pl.BlockSpec((128, D), lambda j: (j, 0))

    def cols(r):
        return pl.BlockSpec((r, 128), lambda j: (0, j))

    return pl.pallas_call(
        body,
        name="slots_l23",
        grid=(N_DEV,),
        in_specs=[rows128, rows128, rows128, pl.BlockSpec((RS_DN, D), lambda j: (j, 0)),
                  cols(MLA_H * HP), cols(RET_H * RET_D)],
        out_specs=[pl.BlockSpec((None, L2_ROWS, D), lambda j: (j, 0, 0)),
                   pl.BlockSpec((None, L3_ROWS, HP), lambda j: (j, 0, 0))],
        out_shape=[_sds((N_DEV, L2_ROWS, D), BF16), _sds((N_DEV, L3_ROWS, HP), BF16)],
        compiler_params=_cp(1),
    )(dwo, dwxq, dwxo, dwdn, dwpa, dwpr)


def slots_early(dwz, dwk, dwv, dwq, *, tile):
    def body(dz_ref, k_ref, v_ref, q_ref, s1_ref, s2_ref):
        for os_, ln_, ds_ in WZ_RUNS:
            for j, off, d, ln in _pieces(os_, ln_, ds_, NS_IN):
                s1_ref[j, :, off:off + ln] = dz_ref[:, d:d + ln].astype(BF16)

        @pl.when(pl.program_id(0) == 0)
        def _():
            for j in range(N_DEV):
                s2_ref[j, 0:E2_UQ, 0:MLA_NOPE] = k_ref[:, j * HP:j * HP + MLA_NOPE].astype(BF16)
                s2_ref[j, 0:E2_UQ, MLA_NOPE:HP] = v_ref[:, j * HP:j * HP + MLA_V].astype(BF16)
                s2_ref[j, E2_UQ:E2_ROWS, 0:NS_UQ] = q_ref[:, j * HP:j * HP + NS_UQ].astype(BF16)
                s2_ref[j, E2_UQ:E2_ROWS, NS_UQ:HP] = jnp.zeros((E2_ROWS - E2_UQ, HP - NS_UQ), BF16)

    def whole(r):
        return pl.BlockSpec((r, MLA_H * HP), lambda i: (0, 0))

    return pl.pallas_call(
        body,
        name="slots_early",
        grid=(D // tile,),
        in_specs=[pl.BlockSpec((tile, ZW), lambda i: (i, 0)), whole(MLA_KVR), whole(MLA_KVR), whole(MLA_QR)],
        out_specs=[pl.BlockSpec((N_DEV, tile, NS_IN), lambda i: (0, i, 0)),
                   pl.BlockSpec((N_DEV, E2_ROWS, HP), lambda i: (0, 0, 0))],
        out_shape=[_sds((N_DEV, D, NS_IN), BF16), _sds((N_DEV, E2_ROWS, HP), BF16)],
        compiler_params=_cp(1),
    )(dwz, dwk, dwv, dwq)


SMALL = (("g_mix", 1024, 0), ("b_gate", 2048, 1), ("g_q_lat", 256, 3), ("g_kv_lat", 128, 4), ("g_ret", 512, 5),
         ("g_cross", 1024, 6), ("g_mem", 1024, 7), ("g_ffn", 1024, 8), ("b_conv", 5632, 9), ("g_final", 1024, 15))
SMALL_DIRECT = tuple(s for s in SMALL if s[0] != "b_conv")
S_ROWS = 16


def _flat_pieces(n, row0):
    return [(row0 + k // D, k, min(D, n - k)) for k in range(0, n, D)]


def exchange_small(cs_a, cs_b, smalls):
    ns = len(smalls)

    def body(*refs):
        ca_ref, cb_ref = refs[:2]
        small_refs = refs[2:2 + ns]
        rd_ref, rs_ref, dsl_ref, own_ref = refs[2 + ns:6 + ns]
        sems = refs[6 + ns:]
        own_ref[...] = jnp.zeros(own_ref.shape, F32)
        for (name, n, row0), g_ref in zip(SMALL_DIRECT, small_refs):
            for r, c0, ln in _flat_pieces(n, row0):
                own_ref[r:r + 1, 0:ln] = g_ref[:, c0:c0 + ln]
        row0 = dict((s[0], s[2]) for s in SMALL)["b_conv"]
        for half, c_ref in enumerate((ca_ref, cb_ref)):
            k = half * D_FF
            end = k + D_FF
            while k < end:
                r, lane = row0 + k // D, k % D
                ln = min(D - lane, end - k)
                own_ref[r:r + 1, lane:lane + ln] = c_ref[3:4, k - half * D_FF:k - half * D_FF + ln]
                k += ln
        dsl_ref[...] = jnp.zeros(dsl_ref.shape, F32)
        per_half = D_FF // NS_UP
        for j in range(N_DEV):
            c_ref = ca_ref if j < per_half else cb_ref
            c0 = (j % per_half) * NS_UP
            dsl_ref[j, 0:3, 0:NS_UP] = c_ref[0:3, c0:c0 + NS_UP]
        comm = ([(dsl_ref, True), (own_ref, False)], [rd_ref, rs_ref]) + tuple(sems)
        _exchange_start(*comm)
        _exchange_wait(*comm)

    n1 = N_DEV - 1
    return pl.pallas_call(
        body,
        name="exchange_small",
        in_specs=[VM] * (2 + ns),
        out_specs=[VM, VM, VM, VM],
        out_shape=[_sds((n1,) + L4_SHAPE, F32), _sds((n1, S_ROWS, D), F32), _sds((N_DEV,) + L4_SHAPE, F32),
                   _sds((S_ROWS, D), F32)],
        scratch_shapes=_exchange_scratch(2),
        compiler_params=_cp0(),
    )(cs_a, cs_b, *smalls)


def _adamw(w, g, m, v):
    m = ADAM_B1 * m + (1.0 - ADAM_B1) * g
    v = ADAM_B2 * v + (1.0 - ADAM_B2) * (g * g)
    m_hat = m / (1.0 - ADAM_B1 ** ADAM_STEP)
    v_hat = v / (1.0 - ADAM_B2 ** ADAM_STEP)
    delta = -ADAM_LR * (m_hat / (jnp.sqrt(v_hat) + ADAM_EPS) + ADAM_WD * w)
    return delta, m, v


def _apply(g, refs, outs):
    d, mn, vn = _adamw(refs[0][...], g, refs[1][...], refs[2][...])
    outs[0][...] = g
    outs[1][...] = d
    outs[2][...] = mn
    outs[3][...] = vn


def adam_cols(own, recv, wmv, spans, *, name, tile):
    R, W = own.shape
    nw = len(spans)

    def body(*refs):
        own_ref, recv_ref = refs[:2]
        ins, outs = refs[2:2 + 3 * nw], refs[2 + 3 * nw:]
        g = own_ref[...].astype(F32)
        for k in range(N_DEV - 1):
            g = g + recv_ref[k].astype(F32)
        for t, (lo, hi) in enumerate(spans):
            _apply(g[:, lo:hi], ins[3 * t:3 * t + 3], outs[4 * t:4 * t + 4])

    def blk(w):
        return pl.BlockSpec((tile, w), lambda i: (i, 0))

    widths = [hi - lo for lo, hi in spans]
    return pl.pallas_call(
        body,
        name=name,
        grid=(R // tile,),
        in_specs=[blk(W), pl.BlockSpec((N_DEV - 1, tile, W), lambda i: (0, i, 0))] + [blk(w) for w in widths for _ in range(3)],
        out_specs=[blk(w) for w in widths for _ in range(4)],
        out_shape=[_sds((R, w), F32) for w in widths for _ in range(4)],
        compiler_params=_cp(1),
    )(own, recv, *wmv)


def adam_rows(own, recv, wmv, spans, *, name):
    nw = len(spans)

    def body(*refs):
        own_ref, recv_ref = refs[:2]
        ins, outs = refs[2:2 + 3 * nw], refs[2 + 3 * nw:]
        for t, (lo, hi, w) in enumerate(spans):
            g = own_ref[lo:hi, :].astype(F32)
            for k in range(N_DEV - 1):
                g = g + recv_ref[k, lo:hi, :].astype(F32)
            _apply(g[:, 0:w], ins[3 * t:3 * t + 3], outs[4 * t:4 * t + 4])

    return pl.pallas_call(
        body,
        name=name,
        in_specs=[VM] * (2 + 3 * nw),
        out_specs=[VM] * (4 * nw),
        out_shape=[_sds((hi - lo, w), F32) for lo, hi, w in spans for _ in range(4)],
        compiler_params=_cp0(),
    )(own, recv, *wmv)


def adam_small(own_s, recv_s, dslots, recv_d, wmv_small, wmv_conv):
    ns = len(SMALL)

    def body(*refs):
        own_ref, rs_ref, dsl_ref, rd_ref = refs[:4]
        ins = refs[4:4 + 3 * ns + 3]
        outs = refs[4 + 3 * ns + 3:4 + 3 * ns + 3 + 4 * ns + 4]
        all_sc = refs[-1]
        me = _my_id()
        all_sc[0] = own_ref[...]
        for k in range(N_DEV - 1):
            all_sc[k + 1] = rs_ref[k]
        g = all_sc[jnp.bitwise_xor(me, 0)]
        for s in range(1, N_DEV):
            g = g + all_sc[jnp.bitwise_xor(me, s)]
        all_sc[0] = g
        for t, (name, n, row0) in enumerate(SMALL):
            pieces = [all_sc[0, r:r + 1, 0:ln] for r, _, ln in _flat_pieces(n, row0)]
            gt = pieces[0] if len(pieces) == 1 else jnp.concatenate(pieces, axis=1)
            _apply(gt, ins[3 * t:3 * t + 3], outs[4 * t:4 * t + 4])
        gc = dsl_ref[me]
        for k in range(N_DEV - 1):
            gc = gc + rd_ref[k]
        _apply(gc[0:3, 0:NS_UP], ins[3 * ns:3 * ns + 3], outs[4 * ns:4 * ns + 4])

    out_shape = [_sds((1, n), F32) for _, n, _ in SMALL for _ in range(4)] + [_sds((3, NS_UP), F32)] * 4
    return pl.pallas_call(
        body,
        name="adam_small",
        in_specs=[VM] * (4 + 3 * ns + 3),
        out_specs=[VM] * len(out_shape),
        out_shape=out_shape,
        scratch_shapes=[pltpu.VMEM((N_DEV, S_ROWS, D), F32)],
    )(own_s, recv_s, dslots, recv_d, *wmv_small, *wmv_conv)


def kernel(x, mem, positions, g_mix, w_in, b_gate, g_q_lat, w_uq, g_kv_lat, w_ukv, w_proj_mla, g_ret, w_proj_ret, w_out, g_cross, g_mem, w_xq, w_xkv, w_xo, g_ffn, w_up, w_conv, b_conv, w_down, g_final, loss_target, m_g_mix, m_w_in, m_b_gate, m_g_q_lat, m_w_uq, m_g_kv_lat, m_w_ukv, m_w_proj_mla, m_g_ret, m_w_proj_ret, m_w_out, m_g_cross, m_g_mem, m_w_xq, m_w_xkv, m_w_xo, m_g_ffn, m_w_up, m_w_conv, m_b_conv, m_w_down, m_g_final, v_g_mix, v_w_in, v_b_gate, v_g_q_lat, v_w_uq, v_g_kv_lat, v_w_ukv, v_w_proj_mla, v_g_ret, v_w_proj_ret, v_w_out, v_g_cross, v_g_mem, v_w_xq, v_w_xkv, v_w_xo, v_g_ffn, v_w_up, v_w_conv, v_b_conv, v_w_down, v_g_final):
    args = dict(locals())
    T = x.shape[1]
    M = mem.shape[1]
    tile = min(256, T)
    tile2 = min(512, T)
    tile4 = min(1024, T)
    tq = min(512, T)
    rb = min(1024, T)

    xs = x[0]
    tgt = loss_target[0]
    mems = mem[0]

    def shard(name, prefix=""):
        a = args[prefix + name]
        return a.reshape(a.shape[-2:]) if a.ndim >= 2 else a.reshape(1, -1)

    e1, e2 = pack_early(shard("w_in"), shard("w_ukv"), shard("w_uq"))
    late_parts = pack_late(shard("w_up"), shard("w_xkv"), shard("w_out"), shard("w_xq"), shard("w_xo"), shard("w_down"),
                           shard("w_proj_mla"), shard("w_proj_ret"), shard("w_conv"))

    pos_f = jnp.broadcast_to(positions[0].astype(F32)[:, None], (T, 128))
    inv_m, inv_r = _rot_inv()
    u, cm, s1, s2, cr, sr, ge1, ge2 = rowwise(
        lambda xv, p, im, ir, g: (_rms(xv, g),) + _rot_tables(p, im, ir), [(xs, None), (pos_f, None)],
        [inv_m, inv_r, g_mix], [(D, BF16)] + [(128, F32)] * 5, [], tile=tile2, name="norm_mix_tables", gather=[e1, e2])
    wz, wk, wv, wq = assemble_early(ge1, ge2, tile=tile)
    rconsts = _ret_consts()

    z = matmul(u, wz, name="mm_z", tm=2048, tn=1536)

    def mixers_in(zl, zr, cmv, s1v, s2v, crv, srv, gq, gkv, wqv, wkv_, wvv):
        cq = _rms(zl[:, 0:256], gq).astype(BF16)
        ckv = _rms(zl[:, 256:384], gkv).astype(BF16)
        qv = _rot_mla(_dot(cq, wqv), cmv, s1v, s2v)
        kr = _rot_mla(zl[:, 384:512], cmv, s1v, s2v)
        kn = _dot(ckv, wkv_)
        kv_ = jnp.concatenate([kn[:, h * HP:(h + 1) * HP] + kr for h in range(MLA_H)], axis=1)
        vv = _dot(ckv, wvv)
        lane = lax.broadcasted_iota(jnp.int32, vv.shape, 1)
        vv = jnp.where((lane & (HP - 1)) == ONE_LANE, 1.0, vv)
        rqv = _rot_ret(zr[:, 0:512], crv, srv)
        rkv = _rot_ret(zr[:, 512:1024], crv, srv) * (RET_D ** -0.5)
        return qv, kv_, vv, rqv, rkv, zr[:, 1024:1536]

    q_a, k_a, v_a, rq, rk, rv = rowwise(
        mixers_in, [(z, (512, 8)), (z, (2048, 0)), (cm, None), (s1, None), (s2, None), (cr, None), (sr, None)],
        [g_q_lat, g_kv_lat, wq, wk, wv], [(MLA_H * HP, BF16)] * 3 + [(512, BF16)] * 3, [], tile=tile2, name="mixers_in")

    o_a, lse_row, gl1, gl2, gl3, gl4 = mla_fwd(q_a, k_a, v_a, list(late_parts), tq=tq)
    wup, wxkv = assemble_l1(gl1, tile=tile)
    wo, wxq, wxo, wdn, wpa, wpr, wcv = assemble_l234(gl2, gl3, gl4)
    ret, rstate = ret_fwd(rq, rk, rv, rconsts, rb=rb)

    def gn_parts(r):
        outs = []
        for h in range(RET_H):
            rh = r[:, h * RET_D:(h + 1) * RET_D]
            mu = jnp.mean(rh, axis=-1, keepdims=True)
            dlt = rh - mu
            rstd = lax.rsqrt(jnp.mean(dlt * dlt, axis=-1, keepdims=True) + EPS)
            outs.append((dlt * rstd, rstd))
        return outs

    def mix_fwd(ov, rv_, rg, gt, wpav, wprv, gr, bg):
        ya = _dot(ov, wpav)
        xh = jnp.concatenate([p[0] for p in gn_parts(rv_)], axis=1)
        t = rg * _sigmoid(rg) * (xh * gr)
        yr = _dot(t.astype(BF16), wprv)
        ga_ = _sigmoid(gt[:, :D] + bg[:, :D])
        gr_ = _sigmoid(gt[:, D:] + bg[:, D:])
        return ga_ * ya + gr_ * yr

    mix = rowwise(mix_fwd, [(o_a, None), (ret, None), (z, (512, 3)), (z, (2048, 1))], [wpa, wpr, g_ret, b_gate],
                  [(D, BF16)], [], tile=tile2, name="mix_fwd")[0]
    def proj_norm(av, rv_, wv_, g):
        hv = rv_ + _dot(av, wv_)
        return hv, _rms(hv, g)

    h1, n2 = rowwise(proj_norm, [(mix, None), (xs, None)], [wo, g_cross], [(D, F32), (D, BF16)], [], tile=tile4,
                     name="mm_out_norm")
    xq = matmul(n2, wxq, out_dtype=BF16, name="mm_xq")
    mn = rowwise(lambda mv_, g: _rms(mv_, g), [(mems, None)], [g_mem], [(D, BF16)], [], tile=min(tile, M), name="norm_mem")[0]
    mkv = matmul(mn, wxkv, out_dtype=BF16, name="mm_mkv")

    x_scale = X_HD ** -0.5

    def xattn_fwd(xqv, mkvv):
        outs = []
        for h in range(X_H):
            sl = slice(h * X_HD, (h + 1) * X_HD)
            s = _dot_nt(xqv[:, sl], mkvv[:, sl]) * x_scale
            s = s - jnp.max(s, axis=-1, keepdims=True)
            e = jnp.exp(s)
            p = e / jnp.sum(e, axis=-1, keepdims=True)
            outs.append(_dot(p.astype(BF16), mkvv[:, D + h * X_HD:D + (h + 1) * X_HD]))
        return jnp.concatenate(outs, axis=1)

    xo = rowwise(xattn_fwd, [(xq, None)], [mkv], [(D, BF16)], [], tile=tile4, name="xattn_fwd")[0]
    h2, n3 = rowwise(proj_norm, [(xo, None), (h1, None)], [wxo, g_ffn], [(D, F32), (D, BF16)], [], tile=tile4,
                     name="mm_xo_norm")
    up_pre = matmul(n3, wup, out_dtype=BF16, name="mm_up", tm=2048, tn=1408)
    cw = D_FF // 2
    act = conv_act_fwd(up_pre, wcv, b_conv, tile=tile, cw=cw)

    def down_loss(av, hv2, tv, wv_, g):
        hv = hv2 + _dot(av, wv_)
        y = _rms(hv, g)
        err = y - tv
        part = 0.5 * jnp.sum(jnp.sum(err * err, axis=-1, keepdims=True) / D, axis=0, keepdims=True)
        dx, dg = _rms_bwd(hv, g, err / D)
        return dx, dg, jnp.broadcast_to(part, (8, 128))

    g_fin2 = g_final.reshape(1, D)
    dh3, dg_final, loss_acc = rowwise(down_loss, [(act, None), (h2, None), (tgt, None)], [wdn, g_fin2], [(D, F32)],
                                      [((1, D), F32), ((8, 128), F32)], tile=tile2, name="mm_down_loss")
    loss = lax.psum(loss_acc[0, 0], ("x", "y", "c"))

    dact = matmul(dh3, wdn, tb=True, out_dtype=BF16, name="mm_dact", tm=2048, tn=1408)
    dw_down = matmul_tn(act, dh3, name="mm_dw_down", tm=1408, tk=1024)
    dup_a, dup_b, cs_a, cs_b = conv_act_bwd(up_pre, dact, wcv, b_conv, tile=tile, cw=cw)
    dw_up_a = matmul_tn(n3, dup_a, name="mm_dw_up_a", tn=1408)
    dw_up_b = matmul_tn(n3, dup_b, name="mm_dw_up_b", tn=1408)

    def ffn_in_bwd(da_, db_, hv, drv, wv_, g):
        dn = _dot_nt(da_, wv_[:, :D_FF]) + _dot_nt(db_, wv_[:, D_FF:])
        dx, dg = _rms_bwd(hv, g, dn)
        return dx + drv, dg

    dh2, dg_ffn = rowwise(ffn_in_bwd, [(dup_a, None), (dup_b, None), (h2, None), (dh3, None)], [wup, g_ffn],
                          [(D, F32)], [((1, D), F32)], tile=tile2, name="mm_dn3_norm_bwd")
    dxo = matmul(dh2, wxo, tb=True, out_dtype=BF16, name="mm_dxo")
    dw_xo = matmul_tn(xo, dh2, name="mm_dw_xo")

    def xattn_bwd(xqv, dxov, mkvv):
        dxq, dmk, dmv = [], [], []
        for h in range(X_H):
            sl = slice(h * X_HD, (h + 1) * X_HD)
            slv = slice(D + h * X_HD, D + (h + 1) * X_HD)
            s = _dot_nt(xqv[:, sl], mkvv[:, sl]) * x_scale
            s = s - jnp.max(s, axis=-1, keepdims=True)
            e = jnp.exp(s)
            p = e / jnp.sum(e, axis=-1, keepdims=True)
            dp = _dot_nt(dxov[:, sl], mkvv[:, slv])
            ds = (p * (dp - jnp.sum(dp * p, axis=-1, keepdims=True)) * x_scale).astype(BF16)
            dxq.append(_dot(ds, mkvv[:, sl]))
            dmk.append(_dot_tn(ds, xqv[:, sl]))
            dmv.append(_dot_tn(p.astype(BF16), dxov[:, sl]))
        return jnp.concatenate(dxq, axis=1), jnp.concatenate(dmk + dmv, axis=1)

    dxq, dmkv = rowwise(xattn_bwd, [(xq, None), (dxo, None)], [mkv], [(D, BF16)], [((M, 2 * D), F32)],
                        tile=tile4, name="xattn_bwd")
    dw_xq = matmul_tn(n2, dxq, name="mm_dw_xq")

    def proj_norm_bwd(dyv, hv, drv, wv_, g):
        dx, dg = _rms_bwd(hv, g, _dot_nt(dyv, wv_))
        return dx + drv, dg

    dh1, dg_cross = rowwise(proj_norm_bwd, [(dxq, None), (h1, None), (dh2, None)], [wxq, g_cross], [(D, F32)],
                            [((1, D), F32)], tile=tile4, name="mm_dn2_norm_bwd")
    dw_xkv = matmul_tn(mn, dmkv, name="mm_dw_xkv", tk=M)
    dmn = matmul(dmkv, wxkv, tb=True, name="mm_dmn", tm=M)
    dg_mem = rowwise(lambda mv_, dyv, g: _rms_bwd(mv_, g, dyv)[1], [(mems, None), (dmn, None)], [g_mem], [],
                     [((1, D), F32)], tile=min(tile, M), name="norm_mem_bwd")[0]

    dmix = matmul(dh1, wo, tb=True, out_dtype=BF16, name="mm_dmix")
    dw_out = matmul_tn(mix, dh1, name="mm_dw_out")

    def mix_bwd(ov, rv_, rg, gt, dmv_, wpav, wprv, gr, bg):
        dm_ = dmv_.astype(F32)
        ya = _dot(ov, wpav)
        parts = gn_parts(rv_)
        xh = jnp.concatenate([p[0] for p in parts], axis=1)
        yn = xh * gr
        sg = _sigmoid(rg)
        sl_ = rg * sg
        t = (sl_ * yn).astype(BF16)
        yr = _dot(t, wprv)
        ga_ = _sigmoid(gt[:, :D] + bg[:, :D])
        gr_ = _sigmoid(gt[:, D:] + bg[:, D:])
        dgates = jnp.concatenate([dm_ * ya * ga_ * (1.0 - ga_), dm_ * yr * gr_ * (1.0 - gr_)], axis=1)
        dya = (dm_ * ga_).astype(BF16)
        dyr = (dm_ * gr_).astype(BF16)
        do_ = _dot_nt(dya, wpav)
        dwpa_ = _dot_tn(ov, dya)
        dt = _dot_nt(dyr, wprv)
        dwpr_ = _dot_tn(t, dyr)
        drg_ = dt * yn * (sg * (1.0 + rg * (1.0 - sg)))
        dyn = dt * sl_
        dgr = jnp.sum(dyn * xh, axis=0, keepdims=True)
        dxh = dyn * gr
        drets = []
        for h in range(RET_H):
            sl = slice(h * RET_D, (h + 1) * RET_D)
            xhh, rstd = parts[h]
            dxhh = dxh[:, sl]
            drets.append(rstd * (dxhh - jnp.mean(dxhh, axis=-1, keepdims=True)
                                 - xhh * jnp.mean(dxhh * xhh, axis=-1, keepdims=True)))
        dret_ = jnp.concatenate(drets, axis=1)
        dbg = jnp.sum(dgates, axis=0, keepdims=True)
        prod = ov.astype(F32) * do_.astype(BF16).astype(F32)
        lane = lax.broadcasted_iota(jnp.int32, (prod.shape[0], HP), 1)
        dlt = jnp.zeros((prod.shape[0], HP), F32)
        for h in range(MLA_H):
            dlt = jnp.where(lane == h, jnp.sum(prod[:, h * HP:(h + 1) * HP], axis=-1, keepdims=True), dlt)
        return do_, dret_, drg_, dgates, dlt, dwpa_, dwpr_, dgr, dbg

    do_a, dret, drg, dgates, delta, dwpa, dw_proj_ret, dg_ret, db_gate = rowwise(
        mix_bwd, [(o_a, None), (ret, None), (z, (512, 3)), (z, (2048, 1)), (dmix, None)], [wpa, wpr, g_ret, b_gate],
        [(MLA_H * HP, BF16), (512, F32), (512, BF16), (2 * D, BF16), (HP, F32)],
        [((MLA_H * HP, D), F32), ((512, D), F32), ((1, 512), F32), ((1, 2 * D), F32)], tile=tile2, name="mix_bwd")

    sl1 = slots_l1(dw_up_a, dw_up_b, dw_xkv, tile=tile)
    sl2, sl3 = slots_l23(dw_out, dw_xq, dw_xo, dw_down, dwpa, dw_proj_ret)
    delta_row = mla_prep(delta, tq=tq)
    dq_a, dk_a, dv_a, rl1, rl2, rl3 = mla_bwd(q_a, k_a, v_a, do_a, lse_row, delta_row, [sl1, sl2, sl3], tq=tq)
    drq_r, drk_r, drv = ret_bwd(rq, rk, rv, rstate, dret, rconsts, rb=rb)

    def mixers_in_bwd(zl, cmv, s1v, s2v, dqv, dkv_, dvv, drq_, drk_, drv_, drg_, dgt, crv, srv, gq, gkv, wqv, wkv_, wvv):
        cqf, ckvf = zl[:, 0:256], zl[:, 256:384]
        cq = _rms(cqf, gq).astype(BF16)
        ckv = _rms(ckvf, gkv).astype(BF16)
        dq_pre = _rot_mla(dqv.astype(F32), cmv, -s1v, -s2v).astype(BF16)
        dkf = dkv_.astype(F32)
        dkr = dkf[:, 0:HP]
        for h in range(1, MLA_H):
            dkr = dkr + dkf[:, h * HP:(h + 1) * HP]
        lane = lax.broadcasted_iota(jnp.int32, dkr.shape, 1)
        dzk = _rot_mla(jnp.where((lane >= 64) & (lane < 96), dkr, 0.0), cmv, -s1v, -s2v)
        dkb = dkv_.astype(BF16)
        dvb = dvv.astype(BF16)
        dcq_n = _dot_nt(dq_pre, wqv)
        dckv_n = _dot_nt(dkb, wkv_) + _dot_nt(dvb, wvv)
        dwq_ = _dot_tn(cq, dq_pre)
        dwk_ = _dot_tn(ckv, dkb)
        dwv_ = _dot_tn(ckv, dvb)
        dcq, dgq = _rms_bwd(cqf, gq, dcq_n)
        dckv, dgkv = _rms_bwd(ckvf, gkv, dckv_n)
        a = _rot_ret(drq_, crv, -srv)
        b = _rot_ret(drk_, crv, -srv) * (RET_D ** -0.5)
        dz_ = jnp.concatenate([a.astype(BF16), b.astype(BF16), drv_, drg_, dgt,
                               dcq.astype(BF16), dckv.astype(BF16), dzk.astype(BF16)], axis=1)
        return dz_, dwq_, dwk_, dwv_, dgq, dgkv

    dz, dwq, dwk, dwv, dg_q_lat, dg_kv_lat = rowwise(
        mixers_in_bwd, [(z, (512, 8)), (cm, None), (s1, None), (s2, None), (dq_a, None), (dk_a, None), (dv_a, None),
                        (drq_r, None), (drk_r, None), (drv, None), (drg, None), (dgates, None), (cr, None), (sr, None)],
        [g_q_lat, g_kv_lat, wq, wk, wv], [(ZW, BF16)],
        [((MLA_QR, MLA_H * HP), F32), ((MLA_KVR, MLA_H * HP), F32), ((MLA_KVR, MLA_H * HP), F32),
         ((1, MLA_QR), F32), ((1, MLA_KVR), F32)], tile=tile2, name="mixers_in_bwd")
    dwz = matmul_tn(u, dz, name="mm_dw_z", tn=1536)
    se1, se2 = slots_early(dwz, dwk, dwv, dwq, tile=tile)
    grad_x, dg_mix, re1, re2 = rowwise(proj_norm_bwd, [(dz, None), (xs, None), (dh1, None)], [wz, g_mix], [(D, F32)],
                                       [((1, D), F32)], tile=tile2, name="mm_du_norm_bwd", exchange=[se1, se2])

    small_grads = {"g_mix": dg_mix, "b_gate": db_gate, "g_q_lat": dg_q_lat, "g_kv_lat": dg_kv_lat, "g_ret": dg_ret,
                   "g_cross": dg_cross, "g_mem": dg_mem, "g_ffn": dg_ffn, "g_final": dg_final}
    rd, rs, dslots, own_s = exchange_small(cs_a, cs_b, [small_grads[n] for n, _, _ in SMALL_DIRECT])

    me = _my_id()

    def own(slots):
        return lax.dynamic_index_in_dim(slots, me, axis=0, keepdims=False)

    def wmv(names):
        return [shard(n, p) for n in names for p in ("", "m_", "v_")]

    names_s = tuple(n for n, _, _ in SMALL)
    groups = (
        (("w_in",), adam_cols(own(se1), re1, wmv(("w_in",)), ((0, NS_IN),), name="adam_e1", tile=128)),
        (("w_ukv", "w_uq"), adam_rows(own(se2), re2, wmv(("w_ukv", "w_uq")),
                                      ((0, E2_UQ, HP), (E2_UQ, E2_ROWS, NS_UQ)), name="adam_e2")),
        (("w_up", "w_xkv"), adam_cols(own(sl1), rl1, wmv(("w_up", "w_xkv")), ((0, NS_UP), (NS_UP, L1_W)),
                                      name="adam_l1", tile=128)),
        (("w_out", "w_xq", "w_xo", "w_down"),
         adam_rows(own(sl2), rl2, wmv(("w_out", "w_xq", "w_xo", "w_down")),
                   ((L2_OUT, L2_XQ, D), (L2_XQ, L2_XO, D), (L2_XO, L2_DN, D), (L2_DN, L2_ROWS, D)), name="adam_l2")),
        (("w_proj_mla", "w_proj_ret"), adam_rows(own(sl3), rl3, wmv(("w_proj_mla", "w_proj_ret")),
                                                 ((0, L3_PRET, HP), (L3_PRET, L3_ROWS, HP)), name="adam_l3")),
        (names_s + ("w_conv",), adam_small(own_s, rs, dslots, rd, wmv(names_s), wmv(("w_conv",)))),
    )
    res = {}
    for names, outs_ in groups:
        for t, n in enumerate(names):
            res[n] = outs_[4 * t:4 * t + 4]

    order = ["g_mix", "w_in", "b_gate", "g_q_lat", "w_uq", "g_kv_lat", "w_ukv", "w_proj_mla", "g_ret", "w_proj_ret",
             "w_out", "g_cross", "g_mem", "w_xq", "w_xkv", "w_xo", "g_ffn", "w_up", "w_conv", "b_conv", "w_down",
             "g_final"]
    outs = [loss, grad_x[None]]
    for kind in range(4):
        outs += [res[n][kind].reshape(args[n].shape) for n in order]
    return tuple(outs)
```

```python
import functools
import math

import jax
import jax.numpy as jnp
import numpy as np
from jax import lax
from jax.experimental import pallas as pl
from jax.experimental.pallas import tpu as pltpu

F32 = jnp.float32
BF16 = jnp.bfloat16

D = 1024
MLA_H, MLA_NOPE, MLA_ROPE, MLA_V = 8, 64, 32, 64
MLA_QR, MLA_KVR = 256, 128
RET_H, RET_D, RET_C = 4, 128, 128
X_H, X_HD = 4, 256
D_FF = 2816
THETA = 10000.0
EPS = 1e-6
HP = 128
ZW = 4608
N_DEV = 8

ADAM_LR, ADAM_B1, ADAM_B2, ADAM_EPS, ADAM_WD, ADAM_STEP = 0.001, 0.9, 0.999, 1e-08, 0.01, 10

VMEM_LIMIT = 56 * 1024 * 1024
BIG_CONST_BYTES = 4 * 1024 * 1024
MESH = pl.DeviceIdType.MESH
VM = pl.BlockSpec(memory_space=pltpu.VMEM)
ANY = pl.BlockSpec(memory_space=pl.ANY)


def _cp(n_axes):
    return pltpu.CompilerParams(dimension_semantics=("arbitrary",) * n_axes, vmem_limit_bytes=VMEM_LIMIT)


def _cp0():
    return pltpu.CompilerParams(vmem_limit_bytes=VMEM_LIMIT)


def _pick(n, cap, mult=128):
    best = None
    for t in range(mult, min(n, cap) + 1, mult):
        if n % t == 0:
            best = t
    return best if best is not None else n


def _dot(a, b):
    return jnp.dot(a, b, preferred_element_type=F32)


def _dot_nt(a, b):
    return lax.dot_general(a, b, (((1,), (1,)), ((), ())), preferred_element_type=F32)


def _dot_tn(a, b):
    return lax.dot_general(a, b, (((0,), (0,)), ((), ())), preferred_element_type=F32)


def _sds(shape, dtype):
    return jax.ShapeDtypeStruct(shape, dtype)


def matmul(a, b, *, name, tb=False, out_dtype=F32, tm=1024, tn=1024):
    M, K = a.shape
    N = b.shape[0] if tb else b.shape[1]
    tm = _pick(M, tm, 8)
    tn = _pick(N, tn)

    def body(a_ref, b_ref, o_ref):
        av = a_ref[...].astype(BF16)
        bv = b_ref[...].astype(BF16)
        acc = _dot_nt(av, bv) if tb else _dot(av, bv)
        o_ref[...] = acc.astype(o_ref.dtype)

    return pl.pallas_call(
        body,
        name=name,
        grid=(M // tm, N // tn),
        in_specs=[pl.BlockSpec((tm, K), lambda i, j: (i, 0)),
                  pl.BlockSpec((tn, K), lambda i, j: (j, 0)) if tb else pl.BlockSpec((K, tn), lambda i, j: (0, j))],
        out_specs=pl.BlockSpec((tm, tn), lambda i, j: (i, j)),
        out_shape=_sds((M, N), out_dtype),
        compiler_params=_cp(2),
    )(a, b)


def matmul_tn(a, b, *, name, tm=1024, tn=1024, tk=2048):
    R, M = a.shape
    N = b.shape[1]
    tm = _pick(M, tm)
    tn = _pick(N, tn)
    tk = _pick(R, tk, 16)
    nk = R // tk

    def body(a_ref, b_ref, o_ref, acc_ref):
        k = pl.program_id(2)

        @pl.when(k == 0)
        def _():
            acc_ref[...] = jnp.zeros_like(acc_ref)

        acc_ref[...] += _dot_tn(a_ref[...].astype(BF16), b_ref[...].astype(BF16))

        @pl.when(k == nk - 1)
        def _():
            o_ref[...] = acc_ref[...]

    return pl.pallas_call(
        body,
        name=name,
        grid=(M // tm, N // tn, nk),
        in_specs=[pl.BlockSpec((tk, tm), lambda i, j, k: (k, i)), pl.BlockSpec((tk, tn), lambda i, j, k: (k, j))],
        out_specs=pl.BlockSpec((tm, tn), lambda i, j, k: (i, j)),
        out_shape=_sds((M, N), F32),
        scratch_shapes=[pltpu.VMEM((tm, tn), F32)],
        compiler_params=_cp(3),
    )(a, b)


def rowwise(fn, rows, consts, out_rows, out_accs, *, tile, name, gather=(), exchange=()):
    T = rows[0][0].shape[0]
    nt = T // tile
    travel = tuple(gather) + tuple(exchange)
    n_r, n_c, n_o, n_a, n_g = len(rows), len(consts), len(out_rows), len(out_accs), len(travel)
    n_in = n_r + n_c + n_g

    def body(*refs):
        if n_g:
            srcs, dsts = refs[n_r + n_c:n_in], refs[n_in + n_o + n_a:n_in + n_o + n_a + n_g]
            sems = tuple(refs[n_in + n_o + n_a + n_g:])
            if gather:
                comm = (srcs, dsts) + sems
                comm_start, comm_finish = _gather2_start, _gather2_finish
            else:
                comm = ([(r, True) for r in srcs], dsts) + sems
                comm_start, comm_finish = _exchange_start, _exchange_wait

            @pl.when(pl.program_id(0) == 0)
            def _():
                comm_start(*comm)

        ins = [r[...] for r in refs[: n_r + n_c]]
        outs = fn(*ins)
        if not isinstance(outs, (tuple, list)):
            outs = (outs,)
        o_refs = refs[n_in:n_in + n_o]
        a_refs = refs[n_in + n_o:n_in + n_o + n_a]
        for o_ref, o in zip(o_refs, outs[:n_o]):
            o_ref[...] = o.astype(o_ref.dtype)
        if n_a:
            first = pl.program_id(0) == 0

            @pl.when(first)
            def _():
                for a_ref, o in zip(a_refs, outs[n_o:]):
                    a_ref[...] = o.astype(a_ref.dtype)

            @pl.when(jnp.logical_not(first))
            def _():
                for a_ref, o in zip(a_refs, outs[n_o:]):
                    a_ref[...] += o.astype(a_ref.dtype)
        if n_g:

            @pl.when(pl.program_id(0) == nt - 1)
            def _():
                comm_finish(*comm)

    in_specs = []
    args = []
    for arr, win in rows:
        if win is None:
            in_specs.append(pl.BlockSpec((tile, arr.shape[1]), lambda i: (i, 0)))
        else:
            w, cb = win
            in_specs.append(pl.BlockSpec((tile, w), functools.partial(lambda i, cb: (i, cb), cb=cb)))
        args.append(arr)
    for c in consts:
        index_map = functools.partial(lambda i, nd: (0,) * nd, nd=c.ndim)
        if c.size * c.dtype.itemsize >= BIG_CONST_BYTES:
            in_specs.append(pl.BlockSpec(c.shape, index_map, pipeline_mode=pl.Buffered(1)))
        else:
            in_specs.append(pl.BlockSpec(c.shape, index_map))
        args.append(c)
    out_specs = [pl.BlockSpec((tile, w), lambda i: (i, 0)) for w, _ in out_rows]
    out_shape = [_sds((T, w), dt) for w, dt in out_rows]
    for shp, dt in out_accs:
        out_specs.append(pl.BlockSpec(shp, functools.partial(lambda i, nd: (0,) * nd, nd=len(shp))))
        out_shape.append(_sds(shp, dt))
    return pl.pallas_call(
        body,
        name=name,
        grid=(nt,),
        in_specs=in_specs + [ANY] * n_g,
        out_specs=out_specs + [ANY] * n_g,
        out_shape=out_shape + [_sds((N_DEV,) + p.shape, p.dtype) for p in gather]
        + [_sds((N_DEV - 1,) + s.shape[1:], s.dtype) for s in exchange],
        scratch_shapes=(_gather_scratch(n_g) if gather else _exchange_scratch(n_g)) if n_g else [],
        compiler_params=_cp(1),
    )(*args, *travel)


def _rms(x, g):
    r = lax.rsqrt(jnp.mean(x * x, axis=-1, keepdims=True) + EPS)
    return x * r * g


def _rms_bwd(x, g, dy):
    r = lax.rsqrt(jnp.mean(x * x, axis=-1, keepdims=True) + EPS)
    xh = x * r
    dg = jnp.sum(dy * xh, axis=0, keepdims=True)
    dxh = dy * g
    dx = r * (dxh - xh * jnp.mean(dxh * xh, axis=-1, keepdims=True))
    return dx, dg


def _sigmoid(x):
    return 0.5 * jnp.tanh(0.5 * x) + 0.5


def _rot_mla(x, c, s1, s2):
    n = x.shape[1] // HP
    outs = []
    for h in range(n):
        xh = x[:, h * HP : (h + 1) * HP]
        outs.append(xh * c + pltpu.roll(xh, HP - 16, 1) * s1 + pltpu.roll(xh, 16, 1) * s2)
    return outs[0] if n == 1 else jnp.concatenate(outs, axis=1)


def _rot_ret(x, c, s):
    n = x.shape[1] // RET_D
    outs = []
    for h in range(n):
        xh = x[:, h * RET_D : (h + 1) * RET_D]
        outs.append(xh * c + pltpu.roll(xh, RET_D // 2, 1) * s)
    return outs[0] if n == 1 else jnp.concatenate(outs, axis=1)


def _rot_inv():
    lane_np = np.arange(128)
    inv_m = (jnp.asarray(THETA, F32) ** (-jnp.asarray(lane_np & 15, F32) / 16.0)).reshape(1, 128)
    inv_r = (jnp.asarray(THETA, F32) ** (-jnp.asarray(lane_np & 63, F32) / 64.0)).reshape(1, 128)
    return inv_m, inv_r


def _rot_tables(p, im, ir):
    lane = lax.broadcasted_iota(jnp.int32, p.shape, 1)
    ang = p * im
    cm = jnp.where((lane >= 64) & (lane < 96), jnp.cos(ang), 1.0)
    sn = jnp.sin(ang)
    s1 = jnp.where((lane >= 64) & (lane < 80), -sn, 0.0)
    s2 = jnp.where((lane >= 80) & (lane < 96), sn, 0.0)
    angr = p * ir
    snr = jnp.sin(angr)
    return cm, s1, s2, jnp.cos(angr), jnp.where(lane < 64, -snr, snr)


def _peer(m):
    x, y, c = lax.axis_index("x"), lax.axis_index("y"), lax.axis_index("c")
    mx, my, mc = (m >> 2) & 1, (m >> 1) & 1, m & 1
    px = 1 - x if mx else x
    py = 1 - y if my else y
    pc = 1 - c if mc else c
    return (px, py, pc), 4 * px + 2 * py + pc


def _my_id():
    return 4 * lax.axis_index("x") + 2 * lax.axis_index("y") + lax.axis_index("c")


def _gather_copies(srcs, outs, send_sems, recv_sems, local_sems, arriving=False):
    me = _my_id()
    copies = []
    if not arriving:
        for g, (src, out) in enumerate(zip(srcs, outs)):
            copies.append((pltpu.make_async_copy(src, out.at[me], local_sems.at[g]), False))
    for m in range(1, N_DEV):
        peer, plin = _peer(m)
        for g, (src, out) in enumerate(zip(srcs, outs)):
            copies.append((pltpu.make_async_remote_copy(
                src_ref=src, dst_ref=out.at[plin if arriving else me], send_sem=send_sems.at[g, m - 1],
                recv_sem=recv_sems.at[g, m - 1], device_id=peer, device_id_type=MESH), True))
    return copies


def _gather_start(*a):
    for cp, _ in _gather_copies(*a):
        cp.start()


def _gather_wait(*a):
    for cp, _ in _gather_copies(*a, arriving=True):
        cp.wait_recv()
    for cp, remote in _gather_copies(*a):
        if remote:
            cp.wait_send()
        else:
            cp.wait()


CHIP_RELATIONS = (4, 2, 6)


def _gather2_copy(src, out, block, to, send_sems, recv_sems, g, k):
    return pltpu.make_async_remote_copy(src_ref=src, dst_ref=out.at[block], send_sem=send_sems.at[g, k],
                                        recv_sem=recv_sems.at[g, k], device_id=to, device_id_type=MESH)


def _gather2_start(srcs, outs, send_sems, recv_sems, local_sems):
    me = _my_id()
    sib, _ = _peer(1)
    for g, (src, out) in enumerate(zip(srcs, outs)):
        pltpu.make_async_copy(src, out.at[me], local_sems.at[g]).start()
        _gather2_copy(src, out, me, sib, send_sems, recv_sems, g, 0).start()
        for t, m in enumerate(CHIP_RELATIONS):
            _gather2_copy(src, out, me, _peer(m)[0], send_sems, recv_sems, g, 1 + t).start()


def _gather2_finish(srcs, outs, send_sems, recv_sems, local_sems):
    me = _my_id()
    sib, sib_lin = _peer(1)
    for t, m in enumerate(CHIP_RELATIONS):
        peer, plin = _peer(m)
        for g, (src, out) in enumerate(zip(srcs, outs)):
            _gather2_copy(src, out, plin, peer, send_sems, recv_sems, g, 1 + t).wait_recv()
            _gather2_copy(out.at[plin], out, plin, sib, send_sems, recv_sems, g, 4 + t).start()
    for g, (src, out) in enumerate(zip(srcs, outs)):
        _gather2_copy(src, out, sib_lin, sib, send_sems, recv_sems, g, 0).wait_recv()
        for t, m in enumerate(CHIP_RELATIONS):
            _gather2_copy(src, out, _peer(m | 1)[1], sib, send_sems, recv_sems, g, 4 + t).wait_recv()
    for g, (src, out) in enumerate(zip(srcs, outs)):
        _gather2_copy(src, out, me, sib, send_sems, recv_sems, g, 0).wait_send()
        for t, m in enumerate(CHIP_RELATIONS):
            peer, plin = _peer(m)
            _gather2_copy(src, out, me, peer, send_sems, recv_sems, g, 1 + t).wait_send()
            _gather2_copy(out.at[plin], out, plin, sib, send_sems, recv_sems, g, 4 + t).wait_send()
        pltpu.make_async_copy(src, out.at[me], local_sems.at[g]).wait()


def _gather_scratch(n):
    return [pltpu.SemaphoreType.DMA((n, N_DEV - 1)), pltpu.SemaphoreType.DMA((n, N_DEV - 1)), pltpu.SemaphoreType.DMA((n,))]


def _exchange_copies(srcs, dsts, send_sems, recv_sems):
    copies = []
    for m in range(1, N_DEV):
        peer, plin = _peer(m)
        for g, ((src, per_peer), dst) in enumerate(zip(srcs, dsts)):
            copies.append(pltpu.make_async_remote_copy(
                src_ref=src.at[plin] if per_peer else src, dst_ref=dst.at[m - 1], send_sem=send_sems.at[g, m - 1],
                recv_sem=recv_sems.at[g, m - 1], device_id=peer, device_id_type=MESH))
    return copies


def _exchange_start(*a):
    for cp in _exchange_copies(*a):
        cp.start()


def _exchange_wait(*a):
    copies = _exchange_copies(*a)
    for cp in copies:
        cp.wait_recv()
    for cp in copies:
        cp.wait_send()


def _exchange_scratch(n):
    return [pltpu.SemaphoreType.DMA((n, N_DEV - 1)), pltpu.SemaphoreType.DMA((n, N_DEV - 1))]


MLA_SCALE = (MLA_NOPE + MLA_ROPE) ** -0.5
MLA_C2 = MLA_SCALE * math.log2(math.e)
ONE_LANE = MLA_V
NEG = -1e30
HPS = 4
HPB = 4


def _tri_mask(n, lower_rows_ge_cols=True):
    r = lax.broadcasted_iota(jnp.int32, (n, n), 0)
    c = lax.broadcasted_iota(jnp.int32, (n, n), 1)
    return r >= c if lower_rows_ge_cols else c >= r


def mla_fwd(q, k, v, gather, *, tq):
    T = q.shape[0]
    nq = T // tq
    rep = tq // HP
    ng = len(gather)

    def body(*refs):
        q_ref, k_ref, v_ref = refs[:3]
        srcs = refs[3:3 + ng]
        o_ref, lse_ref = refs[3 + ng:5 + ng]
        outs = refs[5 + ng:5 + 2 * ng]
        m_sc, acc_sc = refs[5 + 2 * ng:7 + 2 * ng]
        comm = (srcs, outs) + tuple(refs[7 + 2 * ng:])
        h, i = pl.program_id(0), pl.program_id(1)

        @pl.when((h == 0) & (i == 0))
        def _():
            _gather_start(*comm)

        m_sc[...] = jnp.full(m_sc.shape, NEG, F32)
        acc_sc[...] = jnp.zeros(acc_sc.shape, F32)
        heads = [slice(t * HP, (t + 1) * HP) for t in range(HPS)]

        def block(j, masked):
            off = pl.multiple_of(j * tq, tq)
            ss = [_dot_nt(q_ref[:, hd], k_ref[pl.ds(off, tq), hd]) * MLA_C2 for hd in heads]
            mask = _tri_mask(tq) if masked else None
            for hd, s in zip(heads, ss):
                if masked:
                    s = jnp.where(mask, s, NEG)
                m_prev = m_sc[:, hd]
                m_next = jnp.maximum(m_prev, jnp.max(s, axis=-1, keepdims=True))
                p = jnp.exp2(s - jnp.tile(m_next, (1, rep)))
                alpha = jnp.exp2(m_prev - m_next)
                acc_sc[:, hd] = alpha * acc_sc[:, hd] + _dot(p.astype(BF16), v_ref[pl.ds(off, tq), hd])
                m_sc[:, hd] = m_next

        def loop_body(j, carry):
            block(j, False)
            return carry

        lax.fori_loop(0, i, loop_body, 0)
        block(i, True)
        lane = lax.broadcasted_iota(jnp.int32, (tq, HP), 1)
        lse_cols = jnp.zeros((tq, HP), F32)
        for t, hd in enumerate(heads):
            acc = acc_sc[:, hd]
            l = acc[:, ONE_LANE:ONE_LANE + 1]
            o_ref[:, hd] = (acc / l).astype(o_ref.dtype)
            lse_cols = jnp.where(lane == t, m_sc[:, hd] + jnp.log(l) * math.log2(math.e), lse_cols)
        lse_rows = lse_cols.T
        for t in range(HPS):
            lse_ref[t] = lse_rows[t:t + 1, :]

        @pl.when((h == MLA_H // HPS - 1) & (i == nq - 1))
        def _():
            _gather_wait(*comm)

    blk = pl.BlockSpec((tq, HPS * HP), lambda h, i: (i, h))
    full = pl.BlockSpec((T, HPS * HP), lambda h, i: (0, h))
    return pl.pallas_call(
        body,
        name="mla_fwd",
        grid=(MLA_H // HPS, nq),
        in_specs=[blk, full, full] + [ANY] * ng,
        out_specs=[blk, pl.BlockSpec((HPS, None, 1, tq), lambda h, i: (h, i, 0, 0))] + [ANY] * ng,
        out_shape=[_sds((T, MLA_H * HP), BF16), _sds((MLA_H, nq, 1, tq), F32)]
        + [_sds((N_DEV,) + p.shape, p.dtype) for p in gather],
        scratch_shapes=[pltpu.VMEM((tq, HPS * HP), F32), pltpu.VMEM((tq, HPS * HP), F32)] + _gather_scratch(ng),
        compiler_params=_cp(2),
    )(q, k, v, *gather)


def mla_prep(delta, *, tq):
    T = delta.shape[0]
    nq = T // tq

    def body(d_ref, row_ref):
        dt = d_ref[...].T
        for h in range(MLA_H):
            row_ref[h] = dt[h:h + 1, :]

    return pl.pallas_call(
        body,
        name="mla_prep",
        grid=(nq,),
        in_specs=[pl.BlockSpec((tq, HP), lambda i: (i, 0))],
        out_specs=pl.BlockSpec((MLA_H, None, 1, tq), lambda i: (0, i, 0, 0)),
        out_shape=_sds((MLA_H, nq, 1, tq), F32),
        compiler_params=_cp(1),
    )(delta)


def mla_bwd(q, k, v, do, lse_row, delta_row, slots, *, tq):
    T = q.shape[0]
    nq = T // tq
    ns = len(slots)

    def body(*refs):
        q_ref, k_ref, v_ref, do_ref, lse_ref, delta_ref = refs[:6]
        srcs = [(r, True) for r in refs[6:6 + ns]]
        dq_ref, dk_ref, dv_ref = refs[6 + ns:9 + ns]
        dsts = refs[9 + ns:9 + 2 * ns]
        dk_sc, dv_sc = refs[9 + 2 * ns:11 + 2 * ns]
        comm = (srcs, dsts) + tuple(refs[11 + 2 * ns:])
        h, j = pl.program_id(0), pl.program_id(1)

        @pl.when((h == 0) & (j == 0))
        def _():
            _exchange_start(*comm)

        dk_sc[...] = jnp.zeros(dk_sc.shape, F32)
        dv_sc[...] = jnp.zeros(dv_sc.shape, F32)
        heads = [slice(t * HP, (t + 1) * HP) for t in range(HPB)]

        @pl.when(j == 0)
        def _():
            dq_ref[...] = jnp.zeros(dq_ref.shape, F32)

        def block(i, masked):
            off = pl.multiple_of(i * tq, tq)
            sts = [_dot_nt(k_ref[:, hd], q_ref[pl.ds(off, tq), hd]) * MLA_C2 for hd in heads]
            dpts = [_dot_nt(v_ref[:, hd], do_ref[pl.ds(off, tq), hd]) for hd in heads]
            mask = _tri_mask(tq, False) if masked else None
            for t, hd in enumerate(heads):
                st = sts[t]
                if masked:
                    st = jnp.where(mask, st, NEG)
                pt = jnp.exp2(st - lse_ref[t, i])
                dv_sc[:, hd] += _dot(pt.astype(BF16), do_ref[pl.ds(off, tq), hd])
                dst = (pt * (dpts[t] - delta_ref[t, i]) * MLA_SCALE).astype(BF16)
                dk_sc[:, hd] += _dot(dst, q_ref[pl.ds(off, tq), hd])
                dq_ref[pl.ds(off, tq), hd] += _dot_tn(dst, k_ref[:, hd])

        block(j, True)

        def loop_body(i, carry):
            block(i, False)
            return carry

        lax.fori_loop(j + 1, nq, loop_body, 0)
        dk_ref[...] = dk_sc[...].astype(dk_ref.dtype)
        dv_ref[...] = dv_sc[...].astype(dv_ref.dtype)

        @pl.when((h == MLA_H // HPB - 1) & (j == nq - 1))
        def _():
            _exchange_wait(*comm)

    blk = pl.BlockSpec((tq, HPB * HP), lambda h, j: (j, h))
    full = pl.BlockSpec((T, HPB * HP), lambda h, j: (0, h), pipeline_mode=pl.Buffered(1))
    rows = pl.BlockSpec((HPB, nq, 1, tq), lambda h, j: (h, 0, 0, 0))
    return pl.pallas_call(
        body,
        name="mla_bwd",
        grid=(MLA_H // HPB, nq),
        in_specs=[full, blk, blk, full, rows, rows] + [ANY] * ns,
        out_specs=[full, blk, blk] + [ANY] * ns,
        out_shape=[_sds((T, MLA_H * HP), F32), _sds((T, MLA_H * HP), BF16), _sds((T, MLA_H * HP), BF16)]
        + [_sds((N_DEV - 1,) + s.shape[1:], s.dtype) for s in slots],
        scratch_shapes=[pltpu.VMEM((tq, HPB * HP), F32), pltpu.VMEM((tq, HPB * HP), F32)] + _exchange_scratch(ns),
        compiler_params=_cp(2),
    )(q, k, v, do, lse_row, delta_row, *slots)


def _ret_consts():
    h = jnp.arange(RET_H, dtype=F32)
    log_g = jnp.log1p(-jnp.exp2(-5.0 - h))
    idx = jnp.arange(RET_C, dtype=F32)
    rel = idx[:, None] - idx[None, :]
    dmask = jnp.where(rel >= 0, jnp.exp(log_g[:, None, None] * jnp.maximum(rel, 0.0)), 0.0)
    zeta = jnp.exp(log_g[:, None] * (RET_C - 1.0 - idx)[None, :])
    xi = jnp.exp(log_g[:, None] * (idx + 1.0)[None, :])
    decay = jnp.exp(log_g * RET_C)
    zb = jnp.broadcast_to(zeta[:, :, None], (RET_H, RET_C, RET_D))
    xb = jnp.broadcast_to(xi[:, :, None], (RET_H, RET_C, RET_D))
    db = jnp.broadcast_to(decay[:, None, None], (RET_H, RET_C, RET_D))
    return dmask.astype(F32), zb.astype(F32), xb.astype(F32), db.astype(F32)


def ret_fwd(rq, rk, rv, consts, *, rb):
    T = rq.shape[0]
    nb = T // rb
    ncb = rb // RET_C

    def body(q_ref, k_ref, v_ref, dm_ref, z_ref, x_ref, dc_ref, o_ref, st_ref, r_sc):
        @pl.when(pl.program_id(0) == 0)
        def _():
            r_sc[...] = jnp.zeros(r_sc.shape, F32)

        for c in range(ncb):
            sl = slice(c * RET_C, (c + 1) * RET_C)
            for h in range(RET_H):
                hd = slice(h * RET_D, (h + 1) * RET_D)
                q, k, v = q_ref[sl, hd], k_ref[sl, hd], v_ref[sl, hd]
                r = r_sc[h]
                rbf = r.astype(BF16)
                st_ref[sl, hd] = rbf
                s = _dot_nt(q, k) * dm_ref[h]
                inner = _dot(s.astype(BF16), v)
                cross = _dot((q.astype(F32) * x_ref[h]).astype(BF16), rbf)
                o_ref[sl, hd] = inner + cross
                kz = (k.astype(F32) * z_ref[h]).T.astype(BF16)
                r_sc[h] = r * dc_ref[h] + _dot(kz, v)

    blk = pl.BlockSpec((rb, RET_H * RET_D), lambda b: (b, 0))
    cst = pl.BlockSpec((RET_H, RET_C, RET_D), lambda b: (0, 0, 0))
    return pl.pallas_call(
        body,
        name="ret_fwd",
        grid=(nb,),
        in_specs=[blk, blk, blk, cst, cst, cst, cst],
        out_specs=[blk, blk],
        out_shape=[_sds((T, RET_H * RET_D), F32), _sds((T, RET_H * RET_D), BF16)],
        scratch_shapes=[pltpu.VMEM((RET_H, RET_D, RET_D), F32)],
        compiler_params=_cp(1),
    )(rq, rk, rv, *consts)


def ret_bwd(rq, rk, rv, st, dret, consts, *, rb):
    T = rq.shape[0]
    nb = T // rb
    ncb = rb // RET_C

    def body(q_ref, k_ref, v_ref, st_ref, do_ref, dm_ref, z_ref, x_ref, dc_ref, dq_ref, dk_ref, dv_ref, g_sc):
        @pl.when(pl.program_id(0) == 0)
        def _():
            g_sc[...] = jnp.zeros(g_sc.shape, F32)

        for c in reversed(range(ncb)):
            sl = slice(c * RET_C, (c + 1) * RET_C)
            for h in range(RET_H):
                hd = slice(h * RET_D, (h + 1) * RET_D)
                dm, zt, xi = dm_ref[h], z_ref[h], x_ref[h]
                q, k, v, rp = q_ref[sl, hd], k_ref[sl, hd], v_ref[sl, hd], st_ref[sl, hd]
                dob = do_ref[sl, hd].astype(BF16)
                qf, kf = q.astype(F32), k.astype(F32)
                gn = g_sc[h]
                gnb = gn.astype(BF16)
                s = _dot_nt(q, k) * dm
                ds = _dot_nt(dob, v) * dm
                dq = _dot(ds.astype(BF16), k) + _dot_nt(dob, rp) * xi
                dk = _dot(ds.T.astype(BF16), q) + _dot_nt(v, gnb) * zt
                dv = _dot(s.T.astype(BF16), dob) + _dot((kf * zt).astype(BF16), gnb)
                dq_ref[sl, hd] = dq.astype(dq_ref.dtype)
                dk_ref[sl, hd] = dk.astype(dk_ref.dtype)
                dv_ref[sl, hd] = dv.astype(dv_ref.dtype)
                g_sc[h] = _dot((qf * xi).T.astype(BF16), dob) + dc_ref[h] * gn

    blk = pl.BlockSpec((rb, RET_H * RET_D), lambda b: (nb - 1 - b, 0))
    cst = pl.BlockSpec((RET_H, RET_C, RET_D), lambda b: (0, 0, 0))
    return pl.pallas_call(
        body,
        name="ret_bwd",
        grid=(nb,),
        in_specs=[blk, blk, blk, blk, blk, cst, cst, cst, cst],
        out_specs=[blk, blk, blk],
        out_shape=[_sds((T, RET_H * RET_D), F32), _sds((T, RET_H * RET_D), F32), _sds((T, RET_H * RET_D), BF16)],
        scratch_shapes=[pltpu.VMEM((RET_H, RET_D, RET_D), F32)],
        compiler_params=_cp(1),
    )(rq, rk, rv, st, dret, *consts)


HALO = 16
EDGE = 8


def conv_act_fwd(up_pre, w_conv, b_conv, *, tile, cw):
    T = up_pre.shape[0]
    nt = T // tile
    ncol = D_FF // cw
    hb = tile // HALO

    def body(pa_ref, a_ref, pb_ref, b_ref, wa_ref, wb_ref, ba_ref, bb_ref, o_ref):
        i = pl.program_id(1)
        keep = (i > 0).astype(F32)

        def conv(prev_ref, cur_ref, w_ref, bias_ref):
            ext = jnp.concatenate([prev_ref[...].astype(F32)[HALO - EDGE:, :] * keep, cur_ref[...].astype(F32)], axis=0)
            w = w_ref[...]
            y = ext * w[2:3, :] + pltpu.roll(ext, 1, 0) * w[1:2, :] + pltpu.roll(ext, 2, 0) * w[0:1, :] + bias_ref[...]
            return y[EDGE:, :]

        a = conv(pa_ref, a_ref, wa_ref, ba_ref)
        b = conv(pb_ref, b_ref, wb_ref, bb_ref)
        o_ref[...] = (a * _sigmoid(a) * b).astype(o_ref.dtype)

    prev_a = pl.BlockSpec((HALO, cw), lambda j, i: (jnp.maximum(i * hb - 1, 0), j))
    cur_a = pl.BlockSpec((tile, cw), lambda j, i: (i, j))
    prev_b = pl.BlockSpec((HALO, cw), lambda j, i: (jnp.maximum(i * hb - 1, 0), j + ncol))
    cur_b = pl.BlockSpec((tile, cw), lambda j, i: (i, j + ncol))
    w_a = pl.BlockSpec((3, cw), lambda j, i: (0, j))
    w_b = pl.BlockSpec((3, cw), lambda j, i: (0, j + ncol))
    bias_a = pl.BlockSpec((1, cw), lambda j, i: (0, j))
    bias_b = pl.BlockSpec((1, cw), lambda j, i: (0, j + ncol))
    return pl.pallas_call(
        body,
        name="conv_act_fwd",
        grid=(ncol, nt),
        in_specs=[prev_a, cur_a, prev_b, cur_b, w_a, w_b, bias_a, bias_b],
        out_specs=pl.BlockSpec((tile, cw), lambda j, i: (i, j)),
        out_shape=_sds((T, D_FF), BF16),
        compiler_params=_cp(2),
    )(up_pre, up_pre, up_pre, up_pre, w_conv, w_conv, b_conv, b_conv)


def conv_act_bwd(up_pre, dact, w_conv, b_conv, *, tile, cw):
    T = up_pre.shape[0]
    nt = T // tile
    ncol = D_FF // cw
    hb = tile // HALO
    ext_rows = tile + 2 * EDGE

    def body(pa_ref, a_ref, na_ref, pb_ref, b_ref, nb_ref, d_ref, nd_ref, wa_ref, wb_ref, ba_ref, bb_ref,
             dxa_ref, dxb_ref, sa_ref, sb_ref):
        i = pl.program_id(1)
        keep_p = (i > 0).astype(F32)
        keep_n = (i < nt - 1).astype(F32)

        def ext_of(prev_ref, cur_ref, next_ref):
            return jnp.concatenate(
                [prev_ref[...].astype(F32)[HALO - EDGE:, :] * keep_p, cur_ref[...].astype(F32),
                 next_ref[...].astype(F32)[:EDGE, :] * keep_n], axis=0)

        def taps(ext):
            return ext, pltpu.roll(ext, 1, 0), pltpu.roll(ext, 2, 0)

        def conv(tp, w, bias):
            return tp[0] * w[2:3, :] + tp[1] * w[1:2, :] + tp[2] * w[0:1, :] + bias

        xa = taps(ext_of(pa_ref, a_ref, na_ref))
        xb = taps(ext_of(pb_ref, b_ref, nb_ref))
        wa, wb = wa_ref[...], wb_ref[...]
        a = conv(xa, wa, ba_ref[...])
        b = conv(xb, wb, bb_ref[...])
        dy = jnp.concatenate(
            [jnp.zeros((EDGE, cw), F32), d_ref[...].astype(F32), nd_ref[...].astype(F32)[:EDGE, :] * keep_n], axis=0)
        sg = _sigmoid(a)
        da = dy * b * (sg * (1.0 + a * (1.0 - sg)))
        db = dy * (a * sg)

        def back(dup, tp, w, dx_ref, s_ref):
            dx = dup * w[2:3, :] + pltpu.roll(dup, ext_rows - 1, 0) * w[1:2, :] + pltpu.roll(dup, ext_rows - 2, 0) * w[0:1, :]
            dx_ref[...] = dx[EDGE:EDGE + tile, :].astype(dx_ref.dtype)
            dc = dup[EDGE:EDGE + tile, :]
            r2 = jnp.sum(dc * tp[0][EDGE:EDGE + tile, :], axis=0, keepdims=True)
            r1 = jnp.sum(dc * tp[1][EDGE:EDGE + tile, :], axis=0, keepdims=True)
            r0 = jnp.sum(dc * tp[2][EDGE:EDGE + tile, :], axis=0, keepdims=True)
            rb = jnp.sum(dc, axis=0, keepdims=True)
            row = lax.broadcasted_iota(jnp.int32, (8, cw), 0)
            upd = (jnp.where(row == 0, r0, 0.0) + jnp.where(row == 1, r1, 0.0) + jnp.where(row == 2, r2, 0.0)
                   + jnp.where(row == 3, rb, 0.0))

            @pl.when(i == 0)
            def _():
                s_ref[...] = upd

            @pl.when(i > 0)
            def _():
                s_ref[...] += upd

        back(da, xa, wa, dxa_ref, sa_ref)
        back(db, xb, wb, dxb_ref, sb_ref)

    def prev_of(shift):
        return pl.BlockSpec((HALO, cw), lambda j, i: (jnp.maximum(i * hb - 1, 0), j + shift))

    def next_of(shift):
        return pl.BlockSpec((HALO, cw), lambda j, i: (jnp.minimum((i + 1) * hb, nt * hb - 1), j + shift))

    def cur_of(shift):
        return pl.BlockSpec((tile, cw), lambda j, i: (i, j + shift))

    def row_of(rows, shift):
        return pl.BlockSpec((rows, cw), lambda j, i: (0, j + shift))

    return pl.pallas_call(
        body,
        name="conv_act_bwd",
        grid=(ncol, nt),
        in_specs=[prev_of(0), cur_of(0), next_of(0), prev_of(ncol), cur_of(ncol), next_of(ncol), cur_of(0), next_of(0),
                  row_of(3, 0), row_of(3, ncol), row_of(1, 0), row_of(1, ncol)],
        out_specs=[cur_of(0), cur_of(0), row_of(8, 0), row_of(8, 0)],
        out_shape=[_sds((T, D_FF), BF16), _sds((T, D_FF), BF16), _sds((8, D_FF), F32), _sds((8, D_FF), F32)],
        compiler_params=_cp(2),
    )(up_pre, up_pre, up_pre, up_pre, up_pre, up_pre, dact, dact, w_conv, w_conv, b_conv, b_conv)


NS_IN, NS_UP, NS_XKV, NS_UQ = 564, 704, 256, 96
E2_ROWS, E2_UQ = 384, 128
L1_W = NS_UP + NS_XKV
L2_ROWS = 736
L2_OUT, L2_XQ, L2_XO, L2_DN = 0, 128, 256, 384
RS_DN = 352
L3_ROWS, L3_PRET = 1024, 512
L4_SHAPE = (8, 768)
WZ_RUNS = ((416, 4096, 0), (0, 384, 4096), (384, 32, 4544))
WZ_ZERO = ((4480, 4544), (4576, 4608))


def _pieces(orig_start, length, dst_start, ns):
    out, c, d, end = [], orig_start, dst_start, orig_start + length
    while c < end:
        j, off = c // ns, c % ns
        ln = min(ns - off, end - c)
        out.append((j, off, d, ln))
        c += ln
        d += ln
    return out


def pack_early(w_in, w_ukv, w_uq):
    def body(in_ref, ukv_ref, uq_ref, e1_ref, e2_ref):
        e1_ref[...] = in_ref[...].astype(BF16)
        e2_ref[0:E2_UQ, :] = ukv_ref[...].astype(BF16)
        e2_ref[E2_UQ:E2_ROWS, 0:NS_UQ] = uq_ref[...].astype(BF16)
        e2_ref[E2_UQ:E2_ROWS, NS_UQ:HP] = jnp.zeros((E2_ROWS - E2_UQ, HP - NS_UQ), BF16)

    return pl.pallas_call(
        body,
        name="pack_early",
        in_specs=[VM] * 3,
        out_specs=[VM] * 2,
        out_shape=[_sds((D, NS_IN), BF16), _sds((E2_ROWS, HP), BF16)],
        compiler_params=_cp0(),
    )(w_in, w_ukv, w_uq)


def pack_late(w_up, w_xkv, w_out, w_xq, w_xo, w_down, w_pmla, w_pret, w_conv):
    def body(up_ref, xkv_ref, o_ref, xq_ref, xo_ref, dn_ref, pm_ref, pr_ref, cv_ref, l1_ref, l2_ref, l3_ref, l4_ref):
        l1_ref[:, 0:NS_UP] = up_ref[...].astype(BF16)
        l1_ref[:, NS_UP:L1_W] = xkv_ref[...].astype(BF16)
        l2_ref[L2_OUT:L2_XQ, :] = o_ref[...].astype(BF16)
        l2_ref[L2_XQ:L2_XO, :] = xq_ref[...].astype(BF16)
        l2_ref[L2_XO:L2_DN, :] = xo_ref[...].astype(BF16)
        l2_ref[L2_DN:L2_ROWS, :] = dn_ref[...].astype(BF16)
        l3_ref[0:L3_PRET, :] = pm_ref[...].astype(BF16)
        l3_ref[L3_PRET:L3_ROWS, :] = pr_ref[...].astype(BF16)
        l4_ref[...] = jnp.zeros(L4_SHAPE, F32)
        l4_ref[0:3, 0:NS_UP] = cv_ref[...]

    return pl.pallas_call(
        body,
        name="pack_late",
        in_specs=[VM] * 9,
        out_specs=[VM] * 4,
        out_shape=[_sds((D, L1_W), BF16), _sds((L2_ROWS, D), BF16), _sds((L3_ROWS, HP), BF16), _sds(L4_SHAPE, F32)],
        compiler_params=_cp0(),
    )(w_up, w_xkv, w_out, w_xq, w_xo, w_down, w_pmla, w_pret, w_conv)


def assemble_early(g1, g2, *, tile):
    def body(g1_ref, g2_ref, wz_ref, wk_ref, wv_ref, wq_ref):
        for lo, hi in WZ_ZERO:
            wz_ref[:, lo:hi] = jnp.zeros((tile, hi - lo), BF16)
        for os_, ln_, ds_ in WZ_RUNS:
            for j, off, d, ln in _pieces(os_, ln_, ds_, NS_IN):
                wz_ref[:, d:d + ln] = g1_ref[j, :, off:off + ln]

        @pl.when(pl.program_id(0) == 0)
        def _():
            half = jnp.zeros((MLA_KVR, HP - MLA_NOPE), BF16)
            for j in range(N_DEV):
                wk_ref[:, j * HP:j * HP + MLA_NOPE] = g2_ref[j, 0:E2_UQ, 0:MLA_NOPE]
                wk_ref[:, j * HP + MLA_NOPE:(j + 1) * HP] = half
                wv_ref[:, j * HP:j * HP + MLA_V] = g2_ref[j, 0:E2_UQ, MLA_NOPE:HP]
                wv_ref[:, j * HP + MLA_V:(j + 1) * HP] = half
                wq_ref[:, j * HP:j * HP + NS_UQ] = g2_ref[j, E2_UQ:E2_ROWS, 0:NS_UQ]
                wq_ref[:, j * HP + NS_UQ:(j + 1) * HP] = jnp.zeros((MLA_QR, HP - NS_UQ), BF16)

    def whole(r):
        return pl.BlockSpec((r, MLA_H * HP), lambda i: (0, 0))

    return pl.pallas_call(
        body,
        name="assemble_early",
        grid=(D // tile,),
        in_specs=[pl.BlockSpec((N_DEV, tile, NS_IN), lambda i: (0, i, 0)),
                  pl.BlockSpec((N_DEV, E2_ROWS, HP), lambda i: (0, 0, 0))],
        out_specs=[pl.BlockSpec((tile, ZW), lambda i: (i, 0)), whole(MLA_KVR), whole(MLA_KVR), whole(MLA_QR)],
        out_shape=[_sds((D, ZW), BF16), _sds((MLA_KVR, MLA_H * HP), BF16), _sds((MLA_KVR, MLA_H * HP), BF16),
                   _sds((MLA_QR, MLA_H * HP), BF16)],
        compiler_params=_cp(1),
    )(g1, g2)


def assemble_l1(g1, *, tile):
    def body(g_ref, wup_ref, wxkv_ref):
        for j in range(N_DEV):
            wup_ref[:, j * NS_UP:(j + 1) * NS_UP] = g_ref[j, :, 0:NS_UP]
            wxkv_ref[:, j * NS_XKV:(j + 1) * NS_XKV] = g_ref[j, :, NS_UP:L1_W]

    return pl.pallas_call(
        body,
        name="assemble_l1",
        grid=(D // tile,),
        in_specs=[pl.BlockSpec((N_DEV, tile, L1_W), lambda i: (0, i, 0))],
        out_specs=[pl.BlockSpec((tile, 2 * D_FF), lambda i: (i, 0)), pl.BlockSpec((tile, 2 * D), lambda i: (i, 0))],
        out_shape=[_sds((D, 2 * D_FF), BF16), _sds((D, 2 * D), BF16)],
        compiler_params=_cp(1),
    )(g1)


def assemble_l234(g2, g3, g4):
    def body(g2_ref, g3_ref, g4_ref, wo_ref, wxq_ref, wxo_ref, wdn_ref, wpa_ref, wpr_ref, wc_ref):
        for j in range(N_DEV):
            wo_ref[j * 128:(j + 1) * 128, :] = g2_ref[j, L2_OUT:L2_XQ, :]
            wxq_ref[j * 128:(j + 1) * 128, :] = g2_ref[j, L2_XQ:L2_XO, :]
            wxo_ref[j * 128:(j + 1) * 128, :] = g2_ref[j, L2_XO:L2_DN, :]
            wdn_ref[j * RS_DN:(j + 1) * RS_DN, :] = g2_ref[j, L2_DN:L2_ROWS, :]
            wpr_ref[:, j * 128:(j + 1) * 128] = g3_ref[j, L3_PRET:L3_ROWS, :]
            wc_ref[:, j * NS_UP:(j + 1) * NS_UP] = g4_ref[j, 0:3, 0:NS_UP]
            for h in range(MLA_H):
                wpa_ref[h * HP:h * HP + MLA_V, j * 128:(j + 1) * 128] = g3_ref[j, h * MLA_V:(h + 1) * MLA_V, :]
        for h in range(MLA_H):
            wpa_ref[h * HP + MLA_V:(h + 1) * HP, :] = jnp.zeros((HP - MLA_V, D), BF16)

    return pl.pallas_call(
        body,
        name="assemble_l234",
        in_specs=[VM] * 3,
        out_specs=[VM] * 7,
        out_shape=[_sds((D, D), BF16), _sds((D, D), BF16), _sds((D, D), BF16), _sds((D_FF, D), BF16),
                   _sds((MLA_H * HP, D), BF16), _sds((RET_H * RET_D, D), BF16), _sds((3, 2 * D_FF), F32)],
        compiler_params=_cp0(),
    )(g2, g3, g4)


def slots_l1(dwup_a, dwup_b, dwxkv, *, tile):
    per_half = D_FF // NS_UP

    def body(ua_ref, ub_ref, x_ref, s_ref):
        for j in range(N_DEV):
            src = ua_ref if j < per_half else ub_ref
            c0 = (j % per_half) * NS_UP
            s_ref[j, :, 0:NS_UP] = src[:, c0:c0 + NS_UP].astype(BF16)
            s_ref[j, :, NS_UP:L1_W] = x_ref[:, j * NS_XKV:(j + 1) * NS_XKV].astype(BF16)

    return pl.pallas_call(
        body,
        name="slots_l1",
        grid=(D // tile,),
        in_specs=[pl.BlockSpec((tile, D_FF), lambda i: (i, 0)), pl.BlockSpec((tile, D_FF), lambda i: (i, 0)),
                  pl.BlockSpec((tile, 2 * D), lambda i: (i, 0))],
        out_specs=pl.BlockSpec((N_DEV, tile, L1_W), lambda i: (0, i, 0)),
        out_shape=_sds((N_DEV, D, L1_W), BF16),
        compiler_params=_cp(1),
    )(dwup_a, dwup_b, dwxkv)


def slots_l23(dwo, dwxq, dwxo, dwdn, dwpa, dwpr):
    def body(o_ref, xq_ref, xo_ref, dn_ref, pa_ref, pr_ref, l2_ref, l3_ref):
        l2_ref[L2_OUT:L2_XQ, :] = o_ref[...].astype(BF16)
        l2_ref[L2_XQ:L2_XO, :] = xq_ref[...].astype(BF16)
        l2_ref[L2_XO:L2_DN, :] = xo_ref[...].astype(BF16)
        l2_ref[L2_DN:L2_ROWS, :] = dn_ref[...].astype(BF16)
        for h in range(MLA_H):
            l3_ref[h * MLA_V:(h + 1) * MLA_V, :] = pa_ref[h * HP:h * HP + MLA_V, :].astype(BF16)
        l3_ref[L3_PRET:L3_ROWS, :] = pr_ref[...].astype(BF16)

    rows128 = pl.BlockSpec((128, D), lambda j: (j, 0))

    def cols(r):
        return pl.BlockSpec((r, 128), lambda j: (0, j))

    return pl.pallas_call(
        body,
        name="slots_l23",
        grid=(N_DEV,),
        in_specs=[rows128, rows128, rows128, pl.BlockSpec((RS_DN, D), lambda j: (j, 0)),
                  cols(MLA_H * HP), cols(RET_H * RET_D)],
        out_specs=[pl.BlockSpec((None, L2_ROWS, D), lambda j: (j, 0, 0)),
                   pl.BlockSpec((None, L3_ROWS, HP), lambda j: (j, 0, 0))],
        out_shape=[_sds((N_DEV, L2_ROWS, D), BF16), _sds((N_DEV, L3_ROWS, HP), BF16)],
        compiler_params=_cp(1),
    )(dwo, dwxq, dwxo, dwdn, dwpa, dwpr)


def slots_early(dwz, dwk, dwv, dwq, *, tile):
    def body(dz_ref, k_ref, v_ref, q_ref, s1_ref, s2_ref):
        for os_, ln_, ds_ in WZ_RUNS:
            for j, off, d, ln in _pieces(os_, ln_, ds_, NS_IN):
                s1_ref[j, :, off:off + ln] = dz_ref[:, d:d + ln].astype(BF16)

        @pl.when(pl.program_id(0) == 0)
        def _():
            for j in range(N_DEV):
                s2_ref[j, 0:E2_UQ, 0:MLA_NOPE] = k_ref[:, j * HP:j * HP + MLA_NOPE].astype(BF16)
                s2_ref[j, 0:E2_UQ, MLA_NOPE:HP] = v_ref[:, j * HP:j * HP + MLA_V].astype(BF16)
                s2_ref[j, E2_UQ:E2_ROWS, 0:NS_UQ] = q_ref[:, j * HP:j * HP + NS_UQ].astype(BF16)
                s2_ref[j, E2_UQ:E2_ROWS, NS_UQ:HP] = jnp.zeros((E2_ROWS - E2_UQ, HP - NS_UQ), BF16)

    def whole(r):
        return pl.BlockSpec((r, MLA_H * HP), lambda i: (0, 0))

    return pl.pallas_call(
        body,
        name="slots_early",
        grid=(D // tile,),
        in_specs=[pl.BlockSpec((tile, ZW), lambda i: (i, 0)), whole(MLA_KVR), whole(MLA_KVR), whole(MLA_QR)],
        out_specs=[pl.BlockSpec((N_DEV, tile, NS_IN), lambda i: (0, i, 0)),
                   pl.BlockSpec((N_DEV, E2_ROWS, HP), lambda i: (0, 0, 0))],
        out_shape=[_sds((N_DEV, D, NS_IN), BF16), _sds((N_DEV, E2_ROWS, HP), BF16)],
        compiler_params=_cp(1),
    )(dwz, dwk, dwv, dwq)


SMALL = (("g_mix", 1024, 0), ("b_gate", 2048, 1), ("g_q_lat", 256, 3), ("g_kv_lat", 128, 4), ("g_ret", 512, 5),
         ("g_cross", 1024, 6), ("g_mem", 1024, 7), ("g_ffn", 1024, 8), ("b_conv", 5632, 9), ("g_final", 1024, 15))
SMALL_DIRECT = tuple(s for s in SMALL if s[0] != "b_conv")
S_ROWS = 16


def _flat_pieces(n, row0):
    return [(row0 + k // D, k, min(D, n - k)) for k in range(0, n, D)]


def exchange_small(cs_a, cs_b, smalls):
    ns = len(smalls)

    def body(*refs):
        ca_ref, cb_ref = refs[:2]
        small_refs = refs[2:2 + ns]
        rd_ref, rs_ref, dsl_ref, own_ref = refs[2 + ns:6 + ns]
        sems = refs[6 + ns:]
        own_ref[...] = jnp.zeros(own_ref.shape, F32)
        for (name, n, row0), g_ref in zip(SMALL_DIRECT, small_refs):
            for r, c0, ln in _flat_pieces(n, row0):
                own_ref[r:r + 1, 0:ln] = g_ref[:, c0:c0 + ln]
        row0 = dict((s[0], s[2]) for s in SMALL)["b_conv"]
        for half, c_ref in enumerate((ca_ref, cb_ref)):
            k = half * D_FF
            end = k + D_FF
            while k < end:
                r, lane = row0 + k // D, k % D
                ln = min(D - lane, end - k)
                own_ref[r:r + 1, lane:lane + ln] = c_ref[3:4, k - half * D_FF:k - half * D_FF + ln]
                k += ln
        dsl_ref[...] = jnp.zeros(dsl_ref.shape, F32)
        per_half = D_FF // NS_UP
        for j in range(N_DEV):
            c_ref = ca_ref if j < per_half else cb_ref
            c0 = (j % per_half) * NS_UP
            dsl_ref[j, 0:3, 0:NS_UP] = c_ref[0:3, c0:c0 + NS_UP]
        comm = ([(dsl_ref, True), (own_ref, False)], [rd_ref, rs_ref]) + tuple(sems)
        _exchange_start(*comm)
        _exchange_wait(*comm)

    n1 = N_DEV - 1
    return pl.pallas_call(
        body,
        name="exchange_small",
        in_specs=[VM] * (2 + ns),
        out_specs=[VM, VM, VM, VM],
        out_shape=[_sds((n1,) + L4_SHAPE, F32), _sds((n1, S_ROWS, D), F32), _sds((N_DEV,) + L4_SHAPE, F32),
                   _sds((S_ROWS, D), F32)],
        scratch_shapes=_exchange_scratch(2),
        compiler_params=_cp0(),
    )(cs_a, cs_b, *smalls)


def _adamw(w, g, m, v):
    m = ADAM_B1 * m + (1.0 - ADAM_B1) * g
    v = ADAM_B2 * v + (1.0 - ADAM_B2) * (g * g)
    m_hat = m / (1.0 - ADAM_B1 ** ADAM_STEP)
    v_hat = v / (1.0 - ADAM_B2 ** ADAM_STEP)
    delta = -ADAM_LR * (m_hat / (jnp.sqrt(v_hat) + ADAM_EPS) + ADAM_WD * w)
    return delta, m, v


def _apply(g, refs, outs):
    d, mn, vn = _adamw(refs[0][...], g, refs[1][...], refs[2][...])
    outs[0][...] = g
    outs[1][...] = d
    outs[2][...] = mn
    outs[3][...] = vn


def adam_cols(own, recv, wmv, spans, *, name, tile):
    R, W = own.shape
    nw = len(spans)

    def body(*refs):
        own_ref, recv_ref = refs[:2]
        ins, outs = refs[2:2 + 3 * nw], refs[2 + 3 * nw:]
        g = own_ref[...].astype(F32)
        for k in range(N_DEV - 1):
            g = g + recv_ref[k].astype(F32)
        for t, (lo, hi) in enumerate(spans):
            _apply(g[:, lo:hi], ins[3 * t:3 * t + 3], outs[4 * t:4 * t + 4])

    def blk(w):
        return pl.BlockSpec((tile, w), lambda i: (i, 0))

    widths = [hi - lo for lo, hi in spans]
    return pl.pallas_call(
        body,
        name=name,
        grid=(R // tile,),
        in_specs=[blk(W), pl.BlockSpec((N_DEV - 1, tile, W), lambda i: (0, i, 0))] + [blk(w) for w in widths for _ in range(3)],
        out_specs=[blk(w) for w in widths for _ in range(4)],
        out_shape=[_sds((R, w), F32) for w in widths for _ in range(4)],
        compiler_params=_cp(1),
    )(own, recv, *wmv)


def adam_rows(own, recv, wmv, spans, *, name):
    nw = len(spans)

    def body(*refs):
        own_ref, recv_ref = refs[:2]
        ins, outs = refs[2:2 + 3 * nw], refs[2 + 3 * nw:]
        for t, (lo, hi, w) in enumerate(spans):
            g = own_ref[lo:hi, :].astype(F32)
            for k in range(N_DEV - 1):
                g = g + recv_ref[k, lo:hi, :].astype(F32)
            _apply(g[:, 0:w], ins[3 * t:3 * t + 3], outs[4 * t:4 * t + 4])

    return pl.pallas_call(
        body,
        name=name,
        in_specs=[VM] * (2 + 3 * nw),
        out_specs=[VM] * (4 * nw),
        out_shape=[_sds((hi - lo, w), F32) for lo, hi, w in spans for _ in range(4)],
        compiler_params=_cp0(),
    )(own, recv, *wmv)


def adam_small(own_s, recv_s, dslots, recv_d, wmv_small, wmv_conv):
    ns = len(SMALL)

    def body(*refs):
        own_ref, rs_ref, dsl_ref, rd_ref = refs[:4]
        ins = refs[4:4 + 3 * ns + 3]
        outs = refs[4 + 3 * ns + 3:4 + 3 * ns + 3 + 4 * ns + 4]
        all_sc = refs[-1]
        me = _my_id()
        all_sc[0] = own_ref[...]
        for k in range(N_DEV - 1):
            all_sc[k + 1] = rs_ref[k]
        g = all_sc[jnp.bitwise_xor(me, 0)]
        for s in range(1, N_DEV):
            g = g + all_sc[jnp.bitwise_xor(me, s)]
        all_sc[0] = g
        for t, (name, n, row0) in enumerate(SMALL):
            pieces = [all_sc[0, r:r + 1, 0:ln] for r, _, ln in _flat_pieces(n, row0)]
            gt = pieces[0] if len(pieces) == 1 else jnp.concatenate(pieces, axis=1)
            _apply(gt, ins[3 * t:3 * t + 3], outs[4 * t:4 * t + 4])
        gc = dsl_ref[me]
        for k in range(N_DEV - 1):
            gc = gc + rd_ref[k]
        _apply(gc[0:3, 0:NS_UP], ins[3 * ns:3 * ns + 3], outs[4 * ns:4 * ns + 4])

    out_shape = [_sds((1, n), F32) for _, n, _ in SMALL for _ in range(4)] + [_sds((3, NS_UP), F32)] * 4
    return pl.pallas_call(
        body,
        name="adam_small",
        in_specs=[VM] * (4 + 3 * ns + 3),
        out_specs=[VM] * len(out_shape),
        out_shape=out_shape,
        scratch_shapes=[pltpu.VMEM((N_DEV, S_ROWS, D), F32)],
    )(own_s, recv_s, dslots, recv_d, *wmv_small, *wmv_conv)


def kernel(x, mem, positions, g_mix, w_in, b_gate, g_q_lat, w_uq, g_kv_lat, w_ukv, w_proj_mla, g_ret, w_proj_ret, w_out, g_cross, g_mem, w_xq, w_xkv, w_xo, g_ffn, w_up, w_conv, b_conv, w_down, g_final, loss_target, m_g_mix, m_w_in, m_b_gate, m_g_q_lat, m_w_uq, m_g_kv_lat, m_w_ukv, m_w_proj_mla, m_g_ret, m_w_proj_ret, m_w_out, m_g_cross, m_g_mem, m_w_xq, m_w_xkv, m_w_xo, m_g_ffn, m_w_up, m_w_conv, m_b_conv, m_w_down, m_g_final, v_g_mix, v_w_in, v_b_gate, v_g_q_lat, v_w_uq, v_g_kv_lat, v_w_ukv, v_w_proj_mla, v_g_ret, v_w_proj_ret, v_w_out, v_g_cross, v_g_mem, v_w_xq, v_w_xkv, v_w_xo, v_g_ffn, v_w_up, v_w_conv, v_b_conv, v_w_down, v_g_final):
    args = dict(locals())
    T = x.shape[1]
    M = mem.shape[1]
    tile = min(256, T)
    tile2 = min(512, T)
    tile4 = min(1024, T)
    tq = min(512, T)
    rb = min(1024, T)

    xs = x[0]
    tgt = loss_target[0]
    mems = mem[0]

    def shard(name, prefix=""):
        a = args[prefix + name]
        return a.reshape(a.shape[-2:]) if a.ndim >= 2 else a.reshape(1, -1)

    e1, e2 = pack_early(shard("w_in"), shard("w_ukv"), shard("w_uq"))
    late_parts = pack_late(shard("w_up"), shard("w_xkv"), shard("w_out"), shard("w_xq"), shard("w_xo"), shard("w_down"),
                           shard("w_proj_mla"), shard("w_proj_ret"), shard("w_conv"))

    pos_f = jnp.broadcast_to(positions[0].astype(F32)[:, None], (T, 128))
    inv_m, inv_r = _rot_inv()
    u, cm, s1, s2, cr, sr, ge1, ge2 = rowwise(
        lambda xv, p, im, ir, g: (_rms(xv, g),) + _rot_tables(p, im, ir), [(xs, None), (pos_f, None)],
        [inv_m, inv_r, g_mix], [(D, BF16)] + [(128, F32)] * 5, [], tile=tile2, name="norm_mix_tables", gather=[e1, e2])
    wz, wk, wv, wq = assemble_early(ge1, ge2, tile=tile)
    rconsts = _ret_consts()

    z = matmul(u, wz, name="mm_z", tm=2048, tn=1536)

    def mixers_in(zl, zr, cmv, s1v, s2v, crv, srv, gq, gkv, wqv, wkv_, wvv):
        cq = _rms(zl[:, 0:256], gq).astype(BF16)
        ckv = _rms(zl[:, 256:384], gkv).astype(BF16)
        qv = _rot_mla(_dot(cq, wqv), cmv, s1v, s2v)
        kr = _rot_mla(zl[:, 384:512], cmv, s1v, s2v)
        kn = _dot(ckv, wkv_)
        kv_ = jnp.concatenate([kn[:, h * HP:(h + 1) * HP] + kr for h in range(MLA_H)], axis=1)
        vv = _dot(ckv, wvv)
        lane = lax.broadcasted_iota(jnp.int32, vv.shape, 1)
        vv = jnp.where((lane & (HP - 1)) == ONE_LANE, 1.0, vv)
        rqv = _rot_ret(zr[:, 0:512], crv, srv)
        rkv = _rot_ret(zr[:, 512:1024], crv, srv) * (RET_D ** -0.5)
        return qv, kv_, vv, rqv, rkv, zr[:, 1024:1536]

    q_a, k_a, v_a, rq, rk, rv = rowwise(
        mixers_in, [(z, (512, 8)), (z, (2048, 0)), (cm, None), (s1, None), (s2, None), (cr, None), (sr, None)],
        [g_q_lat, g_kv_lat, wq, wk, wv], [(MLA_H * HP, BF16)] * 3 + [(512, BF16)] * 3, [], tile=tile2, name="mixers_in")

    o_a, lse_row, gl1, gl2, gl3, gl4 = mla_fwd(q_a, k_a, v_a, list(late_parts), tq=tq)
    wup, wxkv = assemble_l1(gl1, tile=tile)
    wo, wxq, wxo, wdn, wpa, wpr, wcv = assemble_l234(gl2, gl3, gl4)
    ret, rstate = ret_fwd(rq, rk, rv, rconsts, rb=rb)

    def gn_parts(r):
        outs = []
        for h in range(RET_H):
            rh = r[:, h * RET_D:(h + 1) * RET_D]
            mu = jnp.mean(rh, axis=-1, keepdims=True)
            dlt = rh - mu
            rstd = lax.rsqrt(jnp.mean(dlt * dlt, axis=-1, keepdims=True) + EPS)
            outs.append((dlt * rstd, rstd))
        return outs

    def mix_fwd(ov, rv_, rg, gt, wpav, wprv, gr, bg):
        ya = _dot(ov, wpav)
        xh = jnp.concatenate([p[0] for p in gn_parts(rv_)], axis=1)
        t = rg * _sigmoid(rg) * (xh * gr)
        yr = _dot(t.astype(BF16), wprv)
        ga_ = _sigmoid(gt[:, :D] + bg[:, :D])
        gr_ = _sigmoid(gt[:, D:] + bg[:, D:])
        return ga_ * ya + gr_ * yr

    mix = rowwise(mix_fwd, [(o_a, None), (ret, None), (z, (512, 3)), (z, (2048, 1))], [wpa, wpr, g_ret, b_gate],
                  [(D, BF16)], [], tile=tile2, name="mix_fwd")[0]
    def proj_norm(av, rv_, wv_, g):
        hv = rv_ + _dot(av, wv_)
        return hv, _rms(hv, g)

    h1, n2 = rowwise(proj_norm, [(mix, None), (xs, None)], [wo, g_cross], [(D, F32), (D, BF16)], [], tile=tile4,
                     name="mm_out_norm")
    xq = matmul(n2, wxq, out_dtype=BF16, name="mm_xq")
    mn = rowwise(lambda mv_, g: _rms(mv_, g), [(mems, None)], [g_mem], [(D, BF16)], [], tile=min(tile, M), name="norm_mem")[0]
    mkv = matmul(mn, wxkv, out_dtype=BF16, name="mm_mkv")

    x_scale = X_HD ** -0.5

    def xattn_fwd(xqv, mkvv):
        outs = []
        for h in range(X_H):
            sl = slice(h * X_HD, (h + 1) * X_HD)
            s = _dot_nt(xqv[:, sl], mkvv[:, sl]) * x_scale
            s = s - jnp.max(s, axis=-1, keepdims=True)
            e = jnp.exp(s)
            p = e / jnp.sum(e, axis=-1, keepdims=True)
            outs.append(_dot(p.astype(BF16), mkvv[:, D + h * X_HD:D + (h + 1) * X_HD]))
        return jnp.concatenate(outs, axis=1)

    xo = rowwise(xattn_fwd, [(xq, None)], [mkv], [(D, BF16)], [], tile=tile4, name="xattn_fwd")[0]
    h2, n3 = rowwise(proj_norm, [(xo, None), (h1, None)], [wxo, g_ffn], [(D, F32), (D, BF16)], [], tile=tile4,
                     name="mm_xo_norm")
    up_pre = matmul(n3, wup, out_dtype=BF16, name="mm_up", tm=2048, tn=1408)
    cw = D_FF // 2
    act = conv_act_fwd(up_pre, wcv, b_conv, tile=tile, cw=cw)

    def down_loss(av, hv2, tv, wv_, g):
        hv = hv2 + _dot(av, wv_)
        y = _rms(hv, g)
        err = y - tv
        part = 0.5 * jnp.sum(jnp.sum(err * err, axis=-1, keepdims=True) / D, axis=0, keepdims=True)
        dx, dg = _rms_bwd(hv, g, err / D)
        return dx, dg, jnp.broadcast_to(part, (8, 128))

    g_fin2 = g_final.reshape(1, D)
    dh3, dg_final, loss_acc = rowwise(down_loss, [(act, None), (h2, None), (tgt, None)], [wdn, g_fin2], [(D, F32)],
                                      [((1, D), F32), ((8, 128), F32)], tile=tile2, name="mm_down_loss")
    loss = lax.psum(loss_acc[0, 0], ("x", "y", "c"))

    dact = matmul(dh3, wdn, tb=True, out_dtype=BF16, name="mm_dact", tm=2048, tn=1408)
    dw_down = matmul_tn(act, dh3, name="mm_dw_down", tm=1408, tk=1024)
    dup_a, dup_b, cs_a, cs_b = conv_act_bwd(up_pre, dact, wcv, b_conv, tile=tile, cw=cw)
    dw_up_a = matmul_tn(n3, dup_a, name="mm_dw_up_a", tn=1408)
    dw_up_b = matmul_tn(n3, dup_b, name="mm_dw_up_b", tn=1408)

    def ffn_in_bwd(da_, db_, hv, drv, wv_, g):
        dn = _dot_nt(da_, wv_[:, :D_FF]) + _dot_nt(db_, wv_[:, D_FF:])
        dx, dg = _rms_bwd(hv, g, dn)
        return dx + drv, dg

    dh2, dg_ffn = rowwise(ffn_in_bwd, [(dup_a, None), (dup_b, None), (h2, None), (dh3, None)], [wup, g_ffn],
                          [(D, F32)], [((1, D), F32)], tile=tile2, name="mm_dn3_norm_bwd")
    dxo = matmul(dh2, wxo, tb=True, out_dtype=BF16, name="mm_dxo")
    dw_xo = matmul_tn(xo, dh2, name="mm_dw_xo")

    def xattn_bwd(xqv, dxov, mkvv):
        dxq, dmk, dmv = [], [], []
        for h in range(X_H):
            sl = slice(h * X_HD, (h + 1) * X_HD)
            slv = slice(D + h * X_HD, D + (h + 1) * X_HD)
            s = _dot_nt(xqv[:, sl], mkvv[:, sl]) * x_scale
            s = s - jnp.max(s, axis=-1, keepdims=True)
            e = jnp.exp(s)
            p = e / jnp.sum(e, axis=-1, keepdims=True)
            dp = _dot_nt(dxov[:, sl], mkvv[:, slv])
            ds = (p * (dp - jnp.sum(dp * p, axis=-1, keepdims=True)) * x_scale).astype(BF16)
            dxq.append(_dot(ds, mkvv[:, sl]))
            dmk.append(_dot_tn(ds, xqv[:, sl]))
            dmv.append(_dot_tn(p.astype(BF16), dxov[:, sl]))
        return jnp.concatenate(dxq, axis=1), jnp.concatenate(dmk + dmv, axis=1)

    dxq, dmkv = rowwise(xattn_bwd, [(xq, None), (dxo, None)], [mkv], [(D, BF16)], [((M, 2 * D), F32)],
                        tile=tile4, name="xattn_bwd")
    dw_xq = matmul_tn(n2, dxq, name="mm_dw_xq")

    def proj_norm_bwd(dyv, hv, drv, wv_, g):
        dx, dg = _rms_bwd(hv, g, _dot_nt(dyv, wv_))
        return dx + drv, dg

    dh1, dg_cross = rowwise(proj_norm_bwd, [(dxq, None), (h1, None), (dh2, None)], [wxq, g_cross], [(D, F32)],
                            [((1, D), F32)], tile=tile4, name="mm_dn2_norm_bwd")
    dw_xkv = matmul_tn(mn, dmkv, name="mm_dw_xkv", tk=M)
    dmn = matmul(dmkv, wxkv, tb=True, name="mm_dmn", tm=M)
    dg_mem = rowwise(lambda mv_, dyv, g: _rms_bwd(mv_, g, dyv)[1], [(mems, None), (dmn, None)], [g_mem], [],
                     [((1, D), F32)], tile=min(tile, M), name="norm_mem_bwd")[0]

    dmix = matmul(dh1, wo, tb=True, out_dtype=BF16, name="mm_dmix")
    dw_out = matmul_tn(mix, dh1, name="mm_dw_out")

    def mix_bwd(ov, rv_, rg, gt, dmv_, wpav, wprv, gr, bg):
        dm_ = dmv_.astype(F32)
        ya = _dot(ov, wpav)
        parts = gn_parts(rv_)
        xh = jnp.concatenate([p[0] for p in parts], axis=1)
        yn = xh * gr
        sg = _sigmoid(rg)
        sl_ = rg * sg
        t = (sl_ * yn).astype(BF16)
        yr = _dot(t, wprv)
        ga_ = _sigmoid(gt[:, :D] + bg[:, :D])
        gr_ = _sigmoid(gt[:, D:] + bg[:, D:])
        dgates = jnp.concatenate([dm_ * ya * ga_ * (1.0 - ga_), dm_ * yr * gr_ * (1.0 - gr_)], axis=1)
        dya = (dm_ * ga_).astype(BF16)
        dyr = (dm_ * gr_).astype(BF16)
        do_ = _dot_nt(dya, wpav)
        dwpa_ = _dot_tn(ov, dya)
        dt = _dot_nt(dyr, wprv)
        dwpr_ = _dot_tn(t, dyr)
        drg_ = dt * yn * (sg * (1.0 + rg * (1.0 - sg)))
        dyn = dt * sl_
        dgr = jnp.sum(dyn * xh, axis=0, keepdims=True)
        dxh = dyn * gr
        drets = []
        for h in range(RET_H):
            sl = slice(h * RET_D, (h + 1) * RET_D)
            xhh, rstd = parts[h]
            dxhh = dxh[:, sl]
            drets.append(rstd * (dxhh - jnp.mean(dxhh, axis=-1, keepdims=True)
                                 - xhh * jnp.mean(dxhh * xhh, axis=-1, keepdims=True)))
        dret_ = jnp.concatenate(drets, axis=1)
        dbg = jnp.sum(dgates, axis=0, keepdims=True)
        prod = ov.astype(F32) * do_.astype(BF16).astype(F32)
        lane = lax.broadcasted_iota(jnp.int32, (prod.shape[0], HP), 1)
        dlt = jnp.zeros((prod.shape[0], HP), F32)
        for h in range(MLA_H):
            dlt = jnp.where(lane == h, jnp.sum(prod[:, h * HP:(h + 1) * HP], axis=-1, keepdims=True), dlt)
        return do_, dret_, drg_, dgates, dlt, dwpa_, dwpr_, dgr, dbg

    do_a, dret, drg, dgates, delta, dwpa, dw_proj_ret, dg_ret, db_gate = rowwise(
        mix_bwd, [(o_a, None), (ret, None), (z, (512, 3)), (z, (2048, 1)), (dmix, None)], [wpa, wpr, g_ret, b_gate],
        [(MLA_H * HP, BF16), (512, F32), (512, BF16), (2 * D, BF16), (HP, F32)],
        [((MLA_H * HP, D), F32), ((512, D), F32), ((1, 512), F32), ((1, 2 * D), F32)], tile=tile2, name="mix_bwd")

    sl1 = slots_l1(dw_up_a, dw_up_b, dw_xkv, tile=tile)
    sl2, sl3 = slots_l23(dw_out, dw_xq, dw_xo, dw_down, dwpa, dw_proj_ret)
    delta_row = mla_prep(delta, tq=tq)
    dq_a, dk_a, dv_a, rl1, rl2, rl3 = mla_bwd(q_a, k_a, v_a, do_a, lse_row, delta_row, [sl1, sl2, sl3], tq=tq)
    drq_r, drk_r, drv = ret_bwd(rq, rk, rv, rstate, dret, rconsts, rb=rb)

    def mixers_in_bwd(zl, cmv, s1v, s2v, dqv, dkv_, dvv, drq_, drk_, drv_, drg_, dgt, crv, srv, gq, gkv, wqv, wkv_, wvv):
        cqf, ckvf = zl[:, 0:256], zl[:, 256:384]
        cq = _rms(cqf, gq).astype(BF16)
        ckv = _rms(ckvf, gkv).astype(BF16)
        dq_pre = _rot_mla(dqv.astype(F32), cmv, -s1v, -s2v).astype(BF16)
        dkf = dkv_.astype(F32)
        dkr = dkf[:, 0:HP]
        for h in range(1, MLA_H):
            dkr = dkr + dkf[:, h * HP:(h + 1) * HP]
        lane = lax.broadcasted_iota(jnp.int32, dkr.shape, 1)
        dzk = _rot_mla(jnp.where((lane >= 64) & (lane < 96), dkr, 0.0), cmv, -s1v, -s2v)
        dkb = dkv_.astype(BF16)
        dvb = dvv.astype(BF16)
        dcq_n = _dot_nt(dq_pre, wqv)
        dckv_n = _dot_nt(dkb, wkv_) + _dot_nt(dvb, wvv)
        dwq_ = _dot_tn(cq, dq_pre)
        dwk_ = _dot_tn(ckv, dkb)
        dwv_ = _dot_tn(ckv, dvb)
        dcq, dgq = _rms_bwd(cqf, gq, dcq_n)
        dckv, dgkv = _rms_bwd(ckvf, gkv, dckv_n)
        a = _rot_ret(drq_, crv, -srv)
        b = _rot_ret(drk_, crv, -srv) * (RET_D ** -0.5)
        dz_ = jnp.concatenate([a.astype(BF16), b.astype(BF16), drv_, drg_, dgt,
                               dcq.astype(BF16), dckv.astype(BF16), dzk.astype(BF16)], axis=1)
        return dz_, dwq_, dwk_, dwv_, dgq, dgkv

    dz, dwq, dwk, dwv, dg_q_lat, dg_kv_lat = rowwise(
        mixers_in_bwd, [(z, (512, 8)), (cm, None), (s1, None), (s2, None), (dq_a, None), (dk_a, None), (dv_a, None),
                        (drq_r, None), (drk_r, None), (drv, None), (drg, None), (dgates, None), (cr, None), (sr, None)],
        [g_q_lat, g_kv_lat, wq, wk, wv], [(ZW, BF16)],
        [((MLA_QR, MLA_H * HP), F32), ((MLA_KVR, MLA_H * HP), F32), ((MLA_KVR, MLA_H * HP), F32),
         ((1, MLA_QR), F32), ((1, MLA_KVR), F32)], tile=tile2, name="mixers_in_bwd")
    dwz = matmul_tn(u, dz, name="mm_dw_z", tn=1536)
    se1, se2 = slots_early(dwz, dwk, dwv, dwq, tile=tile)
    grad_x, dg_mix, re1, re2 = rowwise(proj_norm_bwd, [(dz, None), (xs, None), (dh1, None)], [wz, g_mix], [(D, F32)],
                                       [((1, D), F32)], tile=tile2, name="mm_du_norm_bwd", exchange=[se1, se2])

    small_grads = {"g_mix": dg_mix, "b_gate": db_gate, "g_q_lat": dg_q_lat, "g_kv_lat": dg_kv_lat, "g_ret": dg_ret,
                   "g_cross": dg_cross, "g_mem": dg_mem, "g_ffn": dg_ffn, "g_final": dg_final}
    rd, rs, dslots, own_s = exchange_small(cs_a, cs_b, [small_grads[n] for n, _, _ in SMALL_DIRECT])

    me = _my_id()

    def own(slots):
        return lax.dynamic_index_in_dim(slots, me, axis=0, keepdims=False)

    def wmv(names):
        return [shard(n, p) for n in names for p in ("", "m_", "v_")]

    names_s = tuple(n for n, _, _ in SMALL)
    groups = (
        (("w_in",), adam_cols(own(se1), re1, wmv(("w_in",)), ((0, NS_IN),), name="adam_e1", tile=128)),
        (("w_ukv", "w_uq"), adam_rows(own(se2), re2, wmv(("w_ukv", "w_uq")),
                                      ((0, E2_UQ, HP), (E2_UQ, E2_ROWS, NS_UQ)), name="adam_e2")),
        (("w_up", "w_xkv"), adam_cols(own(sl1), rl1, wmv(("w_up", "w_xkv")), ((0, NS_UP), (NS_UP, L1_W)),
                                      name="adam_l1", tile=128)),
        (("w_out", "w_xq", "w_xo", "w_down"),
         adam_rows(own(sl2), rl2, wmv(("w_out", "w_xq", "w_xo", "w_down")),
                   ((L2_OUT, L2_XQ, D), (L2_XQ, L2_XO, D), (L2_XO, L2_DN, D), (L2_DN, L2_ROWS, D)), name="adam_l2")),
        (("w_proj_mla", "w_proj_ret"), adam_rows(own(sl3), rl3, wmv(("w_proj_mla", "w_proj_ret")),
                                                 ((0, L3_PRET, HP), (L3_PRET, L3_ROWS, HP)), name="adam_l3")),
        (names_s + ("w_conv",), adam_small(own_s, rs, dslots, rd, wmv(names_s), wmv(("w_conv",)))),
    )
    res = {}
    for names, outs_ in groups:
        for t, n in enumerate(names):
            res[n] = outs_[4 * t:4 * t + 4]

    order = ["g_mix", "w_in", "b_gate", "g_q_lat", "w_uq", "g_kv_lat", "w_ukv", "w_proj_mla", "g_ret", "w_proj_ret",
             "w_out", "g_cross", "g_mem", "w_xq", "w_xkv", "w_xo", "g_ffn", "w_up", "w_conv", "b_conv", "w_down",
             "g_final"]
    outs = [loss, grad_x[None]]
    for kind in range(4):
        outs += [res[n][kind].reshape(args[n].shape) for n in order]
    return tuple(outs)
```

```python
import functools
import math

import jax
import jax.numpy as jnp
import numpy as np
from jax import lax
from jax.experimental import pallas as pl
from jax.experimental.pallas import tpu as pltpu

F32 = jnp.float32
BF16 = jnp.bfloat16

D = 1024
MLA_H, MLA_NOPE, MLA_ROPE, MLA_V = 8, 64, 32, 64
MLA_QR, MLA_KVR = 256, 128
RET_H, RET_D, RET_C = 4, 128, 128
X_H, X_HD = 4, 256
D_FF = 2816
THETA = 10000.0
EPS = 1e-6
HP = 128
ZW = 4608
N_DEV = 8

ADAM_LR, ADAM_B1, ADAM_B2, ADAM_EPS, ADAM_WD, ADAM_STEP = 0.001, 0.9, 0.999, 1e-08, 0.01, 10

VMEM_LIMIT = 56 * 1024 * 1024
BIG_CONST_BYTES = 4 * 1024 * 1024
MESH = pl.DeviceIdType.MESH
VM = pl.BlockSpec(memory_space=pltpu.VMEM)
ANY = pl.BlockSpec(memory_space=pl.ANY)


def _cp(n_axes):
    return pltpu.CompilerParams(dimension_semantics=("arbitrary",) * n_axes, vmem_limit_bytes=VMEM_LIMIT)


def _cp0():
    return pltpu.CompilerParams(vmem_limit_bytes=VMEM_LIMIT)


def _pick(n, cap, mult=128):
    best = None
    for t in range(mult, min(n, cap) + 1, mult):
        if n % t == 0:
            best = t
    return best if best is not None else n


def _dot(a, b):
    return jnp.dot(a, b, preferred_element_type=F32)


def _dot_nt(a, b):
    return lax.dot_general(a, b, (((1,), (1,)), ((), ())), preferred_element_type=F32)


def _dot_tn(a, b):
    return lax.dot_general(a, b, (((0,), (0,)), ((), ())), preferred_element_type=F32)


def _sds(shape, dtype):
    return jax.ShapeDtypeStruct(shape, dtype)


def matmul(a, b, *, name, tb=False, out_dtype=F32, tm=1024, tn=1024):
    M, K = a.shape
    N = b.shape[0] if tb else b.shape[1]
    tm = _pick(M, tm, 8)
    tn = _pick(N, tn)

    def body(a_ref, b_ref, o_ref):
        av = a_ref[...].astype(BF16)
        bv = b_ref[...].astype(BF16)
        acc = _dot_nt(av, bv) if tb else _dot(av, bv)
        o_ref[...] = acc.astype(o_ref.dtype)

    return pl.pallas_call(
        body,
        name=name,
        grid=(M // tm, N // tn),
        in_specs=[pl.BlockSpec((tm, K), lambda i, j: (i, 0)),
                  pl.BlockSpec((tn, K), lambda i, j: (j, 0)) if tb else pl.BlockSpec((K, tn), lambda i, j: (0, j))],
        out_specs=pl.BlockSpec((tm, tn), lambda i, j: (i, j)),
        out_shape=_sds((M, N), out_dtype),
        compiler_params=_cp(2),
    )(a, b)


def matmul_tn(a, b, *, name, tm=1024, tn=1024, tk=2048):
    R, M = a.shape
    N = b.shape[1]
    tm = _pick(M, tm)
    tn = _pick(N, tn)
    tk = _pick(R, tk, 16)
    nk = R // tk

    def body(a_ref, b_ref, o_ref, acc_ref):
        k = pl.program_id(2)

        @pl.when(k == 0)
        def _():
            acc_ref[...] = jnp.zeros_like(acc_ref)

        acc_ref[...] += _dot_tn(a_ref[...].astype(BF16), b_ref[...].astype(BF16))

        @pl.when(k == nk - 1)
        def _():
            o_ref[...] = acc_ref[...]

    return pl.pallas_call(
        body,
        name=name,
        grid=(M // tm, N // tn, nk),
        in_specs=[pl.BlockSpec((tk, tm), lambda i, j, k: (k, i)), pl.BlockSpec((tk, tn), lambda i, j, k: (k, j))],
        out_specs=pl.BlockSpec((tm, tn), lambda i, j, k: (i, j)),
        out_shape=_sds((M, N), F32),
        scratch_shapes=[pltpu.VMEM((tm, tn), F32)],
        compiler_params=_cp(3),
    )(a, b)


def rowwise(fn, rows, consts, out_rows, out_accs, *, tile, name, gather=(), exchange=()):
    T = rows[0][0].shape[0]
    nt = T // tile
    travel = tuple(gather) + tuple(exchange)
    n_r, n_c, n_o, n_a, n_g = len(rows), len(consts), len(out_rows), len(out_accs), len(travel)
    n_in = n_r + n_c + n_g

    def body(*refs):
        if n_g:
            srcs, dsts = refs[n_r + n_c:n_in], refs[n_in + n_o + n_a:n_in + n_o + n_a + n_g]
            sems = tuple(refs[n_in + n_o + n_a + n_g:])
            if gather:
                comm = (srcs, dsts) + sems
                comm_start, comm_finish = _gather2_start, _gather2_finish
            else:
                comm = ([(r, True) for r in srcs], dsts) + sems
                comm_start, comm_finish = _exchange_start, _exchange_wait

            @pl.when(pl.program_id(0) == 0)
            def _():
                comm_start(*comm)

        ins = [r[...] for r in refs[: n_r + n_c]]
        outs = fn(*ins)
        if not isinstance(outs, (tuple, list)):
            outs = (outs,)
        o_refs = refs[n_in:n_in + n_o]
        a_refs = refs[n_in + n_o:n_in + n_o + n_a]
        for o_ref, o in zip(o_refs, outs[:n_o]):
            o_ref[...] = o.astype(o_ref.dtype)
        if n_a:
            first = pl.program_id(0) == 0

            @pl.when(first)
            def _():
                for a_ref, o in zip(a_refs, outs[n_o:]):
                    a_ref[...] = o.astype(a_ref.dtype)

            @pl.when(jnp.logical_not(first))
            def _():
                for a_ref, o in zip(a_refs, outs[n_o:]):
                    a_ref[...] += o.astype(a_ref.dtype)
        if n_g:

            @pl.when(pl.program_id(0) == nt - 1)
            def _():
                comm_finish(*comm)

    in_specs = []
    args = []
    for arr, win in rows:
        if win is None:
            in_specs.append(pl.BlockSpec((tile, arr.shape[1]), lambda i: (i, 0)))
        else:
            w, cb = win
            in_specs.append(pl.BlockSpec((tile, w), functools.partial(lambda i, cb: (i, cb), cb=cb)))
        args.append(arr)
    for c in consts:
        index_map = functools.partial(lambda i, nd: (0,) * nd, nd=c.ndim)
        if c.size * c.dtype.itemsize >= BIG_CONST_BYTES:
            in_specs.append(pl.BlockSpec(c.shape, index_map, pipeline_mode=pl.Buffered(1)))
        else:
            in_specs.append(pl.BlockSpec(c.shape, index_map))
        args.append(c)
    out_specs = [pl.BlockSpec((tile, w), lambda i: (i, 0)) for w, _ in out_rows]
    out_shape = [_sds((T, w), dt) for w, dt in out_rows]
    for shp, dt in out_accs:
        out_specs.append(pl.BlockSpec(shp, functools.partial(lambda i, nd: (0,) * nd, nd=len(shp))))
        out_shape.append(_sds(shp, dt))
    return pl.pallas_call(
        body,
        name=name,
        grid=(nt,),
        in_specs=in_specs + [ANY] * n_g,
        out_specs=out_specs + [ANY] * n_g,
        out_shape=out_shape + [_sds((N_DEV,) + p.shape, p.dtype) for p in gather]
        + [_sds((N_DEV - 1,) + s.shape[1:], s.dtype) for s in exchange],
        scratch_shapes=(_gather_scratch(n_g) if gather else _exchange_scratch(n_g)) if n_g else [],
        compiler_params=_cp(1),
    )(*args, *travel)


def _rms(x, g):
    r = lax.rsqrt(jnp.mean(x * x, axis=-1, keepdims=True) + EPS)
    return x * r * g


def _rms_bwd(x, g, dy):
    r = lax.rsqrt(jnp.mean(x * x, axis=-1, keepdims=True) + EPS)
    xh = x * r
    dg = jnp.sum(dy * xh, axis=0, keepdims=True)
    dxh = dy * g
    dx = r * (dxh - xh * jnp.mean(dxh * xh, axis=-1, keepdims=True))
    return dx, dg


def _sigmoid(x):
    return 0.5 * jnp.tanh(0.5 * x) + 0.5


def _rot_mla(x, c, s1, s2):
    n = x.shape[1] // HP
    outs = []
    for h in range(n):
        xh = x[:, h * HP : (h + 1) * HP]
        outs.append(xh * c + pltpu.roll(xh, HP - 16, 1) * s1 + pltpu.roll(xh, 16, 1) * s2)
    return outs[0] if n == 1 else jnp.concatenate(outs, axis=1)


def _rot_ret(x, c, s):
    n = x.shape[1] // RET_D
    outs = []
    for h in range(n):
        xh = x[:, h * RET_D : (h + 1) * RET_D]
        outs.append(xh * c + pltpu.roll(xh, RET_D // 2, 1) * s)
    return outs[0] if n == 1 else jnp.concatenate(outs, axis=1)


def _rot_inv():
    lane_np = np.arange(128)
    inv_m = (jnp.asarray(THETA, F32) ** (-jnp.asarray(lane_np & 15, F32) / 16.0)).reshape(1, 128)
    inv_r = (jnp.asarray(THETA, F32) ** (-jnp.asarray(lane_np & 63, F32) / 64.0)).reshape(1, 128)
    return inv_m, inv_r


def _rot_tables(p, im, ir):
    lane = lax.broadcasted_iota(jnp.int32, p.shape, 1)
    ang = p * im
    cm = jnp.where((lane >= 64) & (lane < 96), jnp.cos(ang), 1.0)
    sn = jnp.sin(ang)
    s1 = jnp.where((lane >= 64) & (lane < 80), -sn, 0.0)
    s2 = jnp.where((lane >= 80) & (lane < 96), sn, 0.0)
    angr = p * ir
    snr = jnp.sin(angr)
    return cm, s1, s2, jnp.cos(angr), jnp.where(lane < 64, -snr, snr)


def _peer(m):
    x, y, c = lax.axis_index("x"), lax.axis_index("y"), lax.axis_index("c")
    mx, my, mc = (m >> 2) & 1, (m >> 1) & 1, m & 1
    px = 1 - x if mx else x
    py = 1 - y if my else y
    pc = 1 - c if mc else c
    return (px, py, pc), 4 * px + 2 * py + pc


def _my_id():
    return 4 * lax.axis_index("x") + 2 * lax.axis_index("y") + lax.axis_index("c")


def _gather_copies(srcs, outs, send_sems, recv_sems, local_sems, arriving=False):
    me = _my_id()
    copies = []
    if not arriving:
        for g, (src, out) in enumerate(zip(srcs, outs)):
            copies.append((pltpu.make_async_copy(src, out.at[me], local_sems.at[g]), False))
    for m in range(1, N_DEV):
        peer, plin = _peer(m)
        for g, (src, out) in enumerate(zip(srcs, outs)):
            copies.append((pltpu.make_async_remote_copy(
                src_ref=src, dst_ref=out.at[plin if arriving else me], send_sem=send_sems.at[g, m - 1],
                recv_sem=recv_sems.at[g, m - 1], device_id=peer, device_id_type=MESH), True))
    return copies


def _gather_start(*a):
    for cp, _ in _gather_copies(*a):
        cp.start()


def _gather_wait(*a):
    for cp, _ in _gather_copies(*a, arriving=True):
        cp.wait_recv()
    for cp, remote in _gather_copies(*a):
        if remote:
            cp.wait_send()
        else:
            cp.wait()


CHIP_RELATIONS = (4, 2, 6)


def _gather2_copy(src, out, block, to, send_sems, recv_sems, g, k):
    return pltpu.make_async_remote_copy(src_ref=src, dst_ref=out.at[block], send_sem=send_sems.at[g, k],
                                        recv_sem=recv_sems.at[g, k], device_id=to, device_id_type=MESH)


def _gather2_start(srcs, outs, send_sems, recv_sems, local_sems):
    me = _my_id()
    sib, _ = _peer(1)
    for g, (src, out) in enumerate(zip(srcs, outs)):
        pltpu.make_async_copy(src, out.at[me], local_sems.at[g]).start()
        _gather2_copy(src, out, me, sib, send_sems, recv_sems, g, 0).start()
        for t, m in enumerate(CHIP_RELATIONS):
            _gather2_copy(src, out, me, _peer(m)[0], send_sems, recv_sems, g, 1 + t).start()


def _gather2_finish(srcs, outs, send_sems, recv_sems, local_sems):
    me = _my_id()
    sib, sib_lin = _peer(1)
    for t, m in enumerate(CHIP_RELATIONS):
        peer, plin = _peer(m)
        for g, (src, out) in enumerate(zip(srcs, outs)):
            _gather2_copy(src, out, plin, peer, send_sems, recv_sems, g, 1 + t).wait_recv()
            _gather2_copy(out.at[plin], out, plin, sib, send_sems, recv_sems, g, 4 + t).start()
    for g, (src, out) in enumerate(zip(srcs, outs)):
        _gather2_copy(src, out, sib_lin, sib, send_sems, recv_sems, g, 0).wait_recv()
        for t, m in enumerate(CHIP_RELATIONS):
            _gather2_copy(src, out, _peer(m | 1)[1], sib, send_sems, recv_sems, g, 4 + t).wait_recv()
    for g, (src, out) in enumerate(zip(srcs, outs)):
        _gather2_copy(src, out, me, sib, send_sems, recv_sems, g, 0).wait_send()
        for t, m in enumerate(CHIP_RELATIONS):
            peer, plin = _peer(m)
            _gather2_copy(src, out, me, peer, send_sems, recv_sems, g, 1 + t).wait_send()
            _gather2_copy(out.at[plin], out, plin, sib, send_sems, recv_sems, g, 4 + t).wait_send()
        pltpu.make_async_copy(src, out.at[me], local_sems.at[g]).wait()


def _gather_scratch(n):
    return [pltpu.SemaphoreType.DMA((n, N_DEV - 1)), pltpu.SemaphoreType.DMA((n, N_DEV - 1)), pltpu.SemaphoreType.DMA((n,))]


def _exchange_copies(srcs, dsts, send_sems, recv_sems):
    copies = []
    for m in range(1, N_DEV):
        peer, plin = _peer(m)
        for g, ((src, per_peer), dst) in enumerate(zip(srcs, dsts)):
            copies.append(pltpu.make_async_remote_copy(
                src_ref=src.at[plin] if per_peer else src, dst_ref=dst.at[m - 1], send_sem=send_sems.at[g, m - 1],
                recv_sem=recv_sems.at[g, m - 1], device_id=peer, device_id_type=MESH))
    return copies


def _exchange_start(*a):
    for cp in _exchange_copies(*a):
        cp.start()


def _exchange_wait(*a):
    copies = _exchange_copies(*a)
    for cp in copies:
        cp.wait_recv()
    for cp in copies:
        cp.wait_send()


def _exchange_scratch(n):
    return [pltpu.SemaphoreType.DMA((n, N_DEV - 1)), pltpu.SemaphoreType.DMA((n, N_DEV - 1))]


MLA_SCALE = (MLA_NOPE + MLA_ROPE) ** -0.5
MLA_C2 = MLA_SCALE * math.log2(math.e)
ONE_LANE = MLA_V
NEG = -1e30
HPS = 4
HPB = 4


def _tri_mask(n, lower_rows_ge_cols=True):
    r = lax.broadcasted_iota(jnp.int32, (n, n), 0)
    c = lax.broadcasted_iota(jnp.int32, (n, n), 1)
    return r >= c if lower_rows_ge_cols else c >= r


def mla_fwd(q, k, v, gather, *, tq):
    T = q.shape[0]
    nq = T // tq
    rep = tq // HP
    ng = len(gather)

    def body(*refs):
        q_ref, k_ref, v_ref = refs[:3]
        srcs = refs[3:3 + ng]
        o_ref, lse_ref = refs[3 + ng:5 + ng]
        outs = refs[5 + ng:5 + 2 * ng]
        m_sc, acc_sc = refs[5 + 2 * ng:7 + 2 * ng]
        comm = (srcs, outs) + tuple(refs[7 + 2 * ng:])
        h, i = pl.program_id(0), pl.program_id(1)

        @pl.when((h == 0) & (i == 0))
        def _():
            _gather_start(*comm)

        m_sc[...] = jnp.full(m_sc.shape, NEG, F32)
        acc_sc[...] = jnp.zeros(acc_sc.shape, F32)
        heads = [slice(t * HP, (t + 1) * HP) for t in range(HPS)]

        def block(j, masked):
            off = pl.multiple_of(j * tq, tq)
            ss = [_dot_nt(q_ref[:, hd], k_ref[pl.ds(off, tq), hd]) * MLA_C2 for hd in heads]
            mask = _tri_mask(tq) if masked else None
            for hd, s in zip(heads, ss):
                if masked:
                    s = jnp.where(mask, s, NEG)
                m_prev = m_sc[:, hd]
                m_next = jnp.maximum(m_prev, jnp.max(s, axis=-1, keepdims=True))
                p = jnp.exp2(s - jnp.tile(m_next, (1, rep)))
                alpha = jnp.exp2(m_prev - m_next)
                acc_sc[:, hd] = alpha * acc_sc[:, hd] + _dot(p.astype(BF16), v_ref[pl.ds(off, tq), hd])
                m_sc[:, hd] = m_next

        def loop_body(j, carry):
            block(j, False)
            return carry

        lax.fori_loop(0, i, loop_body, 0)
        block(i, True)
        lane = lax.broadcasted_iota(jnp.int32, (tq, HP), 1)
        lse_cols = jnp.zeros((tq, HP), F32)
        for t, hd in enumerate(heads):
            acc = acc_sc[:, hd]
            l = acc[:, ONE_LANE:ONE_LANE + 1]
            o_ref[:, hd] = (acc / l).astype(o_ref.dtype)
            lse_cols = jnp.where(lane == t, m_sc[:, hd] + jnp.log(l) * math.log2(math.e), lse_cols)
        lse_rows = lse_cols.T
        for t in range(HPS):
            lse_ref[t] = lse_rows[t:t + 1, :]

        @pl.when((h == MLA_H // HPS - 1) & (i == nq - 1))
        def _():
            _gather_wait(*comm)

    blk = pl.BlockSpec((tq, HPS * HP), lambda h, i: (i, h))
    full = pl.BlockSpec((T, HPS * HP), lambda h, i: (0, h))
    return pl.pallas_call(
        body,
        name="mla_fwd",
        grid=(MLA_H // HPS, nq),
        in_specs=[blk, full, full] + [ANY] * ng,
        out_specs=[blk, pl.BlockSpec((HPS, None, 1, tq), lambda h, i: (h, i, 0, 0))] + [ANY] * ng,
        out_shape=[_sds((T, MLA_H * HP), BF16), _sds((MLA_H, nq, 1, tq), F32)]
        + [_sds((N_DEV,) + p.shape, p.dtype) for p in gather],
        scratch_shapes=[pltpu.VMEM((tq, HPS * HP), F32), pltpu.VMEM((tq, HPS * HP), F32)] + _gather_scratch(ng),
        compiler_params=_cp(2),
    )(q, k, v, *gather)


def mla_prep(delta, *, tq):
    T = delta.shape[0]
    nq = T // tq

    def body(d_ref, row_ref):
        dt = d_ref[...].T
        for h in range(MLA_H):
            row_ref[h] = dt[h:h + 1, :]

    return pl.pallas_call(
        body,
        name="mla_prep",
        grid=(nq,),
        in_specs=[pl.BlockSpec((tq, HP), lambda i: (i, 0))],
        out_specs=pl.BlockSpec((MLA_H, None, 1, tq), lambda i: (0, i, 0, 0)),
        out_shape=_sds((MLA_H, nq, 1, tq), F32),
        compiler_params=_cp(1),
    )(delta)


def mla_bwd(q, k, v, do, lse_row, delta_row, slots, *, tq):
    T = q.shape[0]
    nq = T // tq
    ns = len(slots)

    def body(*refs):
        q_ref, k_ref, v_ref, do_ref, lse_ref, delta_ref = refs[:6]
        srcs = [(r, True) for r in refs[6:6 + ns]]
        dq_ref, dk_ref, dv_ref = refs[6 + ns:9 + ns]
        dsts = refs[9 + ns:9 + 2 * ns]
        dk_sc, dv_sc = refs[9 + 2 * ns:11 + 2 * ns]
        comm = (srcs, dsts) + tuple(refs[11 + 2 * ns:])
        h, j = pl.program_id(0), pl.program_id(1)

        @pl.when((h == 0) & (j == 0))
        def _():
            _exchange_start(*comm)

        dk_sc[...] = jnp.zeros(dk_sc.shape, F32)
        dv_sc[...] = jnp.zeros(dv_sc.shape, F32)
        heads = [slice(t * HP, (t + 1) * HP) for t in range(HPB)]

        @pl.when(j == 0)
        def _():
            dq_ref[...] = jnp.zeros(dq_ref.shape, F32)

        def block(i, masked):
            off = pl.multiple_of(i * tq, tq)
            sts = [_dot_nt(k_ref[:, hd], q_ref[pl.ds(off, tq), hd]) * MLA_C2 for hd in heads]
            dpts = [_dot_nt(v_ref[:, hd], do_ref[pl.ds(off, tq), hd]) for hd in heads]
            mask = _tri_mask(tq, False) if masked else None
            for t, hd in enumerate(heads):
                st = sts[t]
                if masked:
                    st = jnp.where(mask, st, NEG)
                pt = jnp.exp2(st - lse_ref[t, i])
                dv_sc[:, hd] += _dot(pt.astype(BF16), do_ref[pl.ds(off, tq), hd])
                dst = (pt * (dpts[t] - delta_ref[t, i]) * MLA_SCALE).astype(BF16)
                dk_sc[:, hd] += _dot(dst, q_ref[pl.ds(off, tq), hd])
                dq_ref[pl.ds(off, tq), hd] += _dot_tn(dst, k_ref[:, hd])

        block(j, True)

        def loop_body(i, carry):
            block(i, False)
            return carry

        lax.fori_loop(j + 1, nq, loop_body, 0)
        dk_ref[...] = dk_sc[...].astype(dk_ref.dtype)
        dv_ref[...] = dv_sc[...].astype(dv_ref.dtype)

        @pl.when((h == MLA_H // HPB - 1) & (j == nq - 1))
        def _():
            _exchange_wait(*comm)

    blk = pl.BlockSpec((tq, HPB * HP), lambda h, j: (j, h))
    full = pl.BlockSpec((T, HPB * HP), lambda h, j: (0, h), pipeline_mode=pl.Buffered(1))
    rows = pl.BlockSpec((HPB, nq, 1, tq), lambda h, j: (h, 0, 0, 0))
    return pl.pallas_call(
        body,
        name="mla_bwd",
        grid=(MLA_H // HPB, nq),
        in_specs=[full, blk, blk, full, rows, rows] + [ANY] * ns,
        out_specs=[full, blk, blk] + [ANY] * ns,
        out_shape=[_sds((T, MLA_H * HP), F32), _sds((T, MLA_H * HP), BF16), _sds((T, MLA_H * HP), BF16)]
        + [_sds((N_DEV - 1,) + s.shape[1:], s.dtype) for s in slots],
        scratch_shapes=[pltpu.VMEM((tq, HPB * HP), F32), pltpu.VMEM((tq, HPB * HP), F32)] + _exchange_scratch(ns),
        compiler_params=_cp(2),
    )(q, k, v, do, lse_row, delta_row, *slots)


def _ret_consts():
    h = jnp.arange(RET_H, dtype=F32)
    log_g = jnp.log1p(-jnp.exp2(-5.0 - h))
    idx = jnp.arange(RET_C, dtype=F32)
    rel = idx[:, None] - idx[None, :]
    dmask = jnp.where(rel >= 0, jnp.exp(log_g[:, None, None] * jnp.maximum(rel, 0.0)), 0.0)
    zeta = jnp.exp(log_g[:, None] * (RET_C - 1.0 - idx)[None, :])
    xi = jnp.exp(log_g[:, None] * (idx + 1.0)[None, :])
    decay = jnp.exp(log_g * RET_C)
    zb = jnp.broadcast_to(zeta[:, :, None], (RET_H, RET_C, RET_D))
    xb = jnp.broadcast_to(xi[:, :, None], (RET_H, RET_C, RET_D))
    db = jnp.broadcast_to(decay[:, None, None], (RET_H, RET_C, RET_D))
    return dmask.astype(F32), zb.astype(F32), xb.astype(F32), db.astype(F32)


def ret_fwd(rq, rk, rv, consts, *, rb):
    T = rq.shape[0]
    nb = T // rb
    ncb = rb // RET_C

    def body(q_ref, k_ref, v_ref, dm_ref, z_ref, x_ref, dc_ref, o_ref, st_ref, r_sc):
        @pl.when(pl.program_id(0) == 0)
        def _():
            r_sc[...] = jnp.zeros(r_sc.shape, F32)

        for c in range(ncb):
            sl = slice(c * RET_C, (c + 1) * RET_C)
            for h in range(RET_H):
                hd = slice(h * RET_D, (h + 1) * RET_D)
                q, k, v = q_ref[sl, hd], k_ref[sl, hd], v_ref[sl, hd]
                r = r_sc[h]
                rbf = r.astype(BF16)
                st_ref[sl, hd] = rbf
                s = _dot_nt(q, k) * dm_ref[h]
                inner = _dot(s.astype(BF16), v)
                cross = _dot((q.astype(F32) * x_ref[h]).astype(BF16), rbf)
                o_ref[sl, hd] = inner + cross
                kz = (k.astype(F32) * z_ref[h]).T.astype(BF16)
                r_sc[h] = r * dc_ref[h] + _dot(kz, v)

    blk = pl.BlockSpec((rb, RET_H * RET_D), lambda b: (b, 0))
    cst = pl.BlockSpec((RET_H, RET_C, RET_D), lambda b: (0, 0, 0))
    return pl.pallas_call(
        body,
        name="ret_fwd",
        grid=(nb,),
        in_specs=[blk, blk, blk, cst, cst, cst, cst],
        out_specs=[blk, blk],
        out_shape=[_sds((T, RET_H * RET_D), F32), _sds((T, RET_H * RET_D), BF16)],
        scratch_shapes=[pltpu.VMEM((RET_H, RET_D, RET_D), F32)],
        compiler_params=_cp(1),
    )(rq, rk, rv, *consts)


def ret_bwd(rq, rk, rv, st, dret, consts, *, rb):
    T = rq.shape[0]
    nb = T // rb
    ncb = rb // RET_C

    def body(q_ref, k_ref, v_ref, st_ref, do_ref, dm_ref, z_ref, x_ref, dc_ref, dq_ref, dk_ref, dv_ref, g_sc):
        @pl.when(pl.program_id(0) == 0)
        def _():
            g_sc[...] = jnp.zeros(g_sc.shape, F32)

        for c in reversed(range(ncb)):
            sl = slice(c * RET_C, (c + 1) * RET_C)
            for h in range(RET_H):
                hd = slice(h * RET_D, (h + 1) * RET_D)
                dm, zt, xi = dm_ref[h], z_ref[h], x_ref[h]
                q, k, v, rp = q_ref[sl, hd], k_ref[sl, hd], v_ref[sl, hd], st_ref[sl, hd]
                dob = do_ref[sl, hd].astype(BF16)
                qf, kf = q.astype(F32), k.astype(F32)
                gn = g_sc[h]
                gnb = gn.astype(BF16)
                s = _dot_nt(q, k) * dm
                ds = _dot_nt(dob, v) * dm
                dq = _dot(ds.astype(BF16), k) + _dot_nt(dob, rp) * xi
                dk = _dot(ds.T.astype(BF16), q) + _dot_nt(v, gnb) * zt
                dv = _dot(s.T.astype(BF16), dob) + _dot((kf * zt).astype(BF16), gnb)
                dq_ref[sl, hd] = dq.astype(dq_ref.dtype)
                dk_ref[sl, hd] = dk.astype(dk_ref.dtype)
                dv_ref[sl, hd] = dv.astype(dv_ref.dtype)
                g_sc[h] = _dot((qf * xi).T.astype(BF16), dob) + dc_ref[h] * gn

    blk = pl.BlockSpec((rb, RET_H * RET_D), lambda b: (nb - 1 - b, 0))
    cst = pl.BlockSpec((RET_H, RET_C, RET_D), lambda b: (0, 0, 0))
    return pl.pallas_call(
        body,
        name="ret_bwd",
        grid=(nb,),
        in_specs=[blk, blk, blk, blk, blk, cst, cst, cst, cst],
        out_specs=[blk, blk, blk],
        out_shape=[_sds((T, RET_H * RET_D), F32), _sds((T, RET_H * RET_D), F32), _sds((T, RET_H * RET_D), BF16)],
        scratch_shapes=[pltpu.VMEM((RET_H, RET_D, RET_D), F32)],
        compiler_params=_cp(1),
    )(rq, rk, rv, st, dret, *consts)


HALO = 16
EDGE = 8


def conv_act_fwd(up_pre, w_conv, b_conv, *, tile, cw):
    T = up_pre.shape[0]
    nt = T // tile
    ncol = D_FF // cw
    hb = tile // HALO

    def body(pa_ref, a_ref, pb_ref, b_ref, wa_ref, wb_ref, ba_ref, bb_ref, o_ref):
        i = pl.program_id(1)
        keep = (i > 0).astype(F32)

        def conv(prev_ref, cur_ref, w_ref, bias_ref):
            ext = jnp.concatenate([prev_ref[...].astype(F32)[HALO - EDGE:, :] * keep, cur_ref[...].astype(F32)], axis=0)
            w = w_ref[...]
            y = ext * w[2:3, :] + pltpu.roll(ext, 1, 0) * w[1:2, :] + pltpu.roll(ext, 2, 0) * w[0:1, :] + bias_ref[...]
            return y[EDGE:, :]

        a = conv(pa_ref, a_ref, wa_ref, ba_ref)
        b = conv(pb_ref, b_ref, wb_ref, bb_ref)
        o_ref[...] = (a * _sigmoid(a) * b).astype(o_ref.dtype)

    prev_a = pl.BlockSpec((HALO, cw), lambda j, i: (jnp.maximum(i * hb - 1, 0), j))
    cur_a = pl.BlockSpec((tile, cw), lambda j, i: (i, j))
    prev_b = pl.BlockSpec((HALO, cw), lambda j, i: (jnp.maximum(i * hb - 1, 0), j + ncol))
    cur_b = pl.BlockSpec((tile, cw), lambda j, i: (i, j + ncol))
    w_a = pl.BlockSpec((3, cw), lambda j, i: (0, j))
    w_b = pl.BlockSpec((3, cw), lambda j, i: (0, j + ncol))
    bias_a = pl.BlockSpec((1, cw), lambda j, i: (0, j))
    bias_b = pl.BlockSpec((1, cw), lambda j, i: (0, j + ncol))
    return pl.pallas_call(
        body,
        name="conv_act_fwd",
        grid=(ncol, nt),
        in_specs=[prev_a, cur_a, prev_b, cur_b, w_a, w_b, bias_a, bias_b],
        out_specs=pl.BlockSpec((tile, cw), lambda j, i: (i, j)),
        out_shape=_sds((T, D_FF), BF16),
        compiler_params=_cp(2),
    )(up_pre, up_pre, up_pre, up_pre, w_conv, w_conv, b_conv, b_conv)


def conv_act_bwd(up_pre, dact, w_conv, b_conv, *, tile, cw):
    T = up_pre.shape[0]
    nt = T // tile
    ncol = D_FF // cw
    hb = tile // HALO
    ext_rows = tile + 2 * EDGE

    def body(pa_ref, a_ref, na_ref, pb_ref, b_ref, nb_ref, d_ref, nd_ref, wa_ref, wb_ref, ba_ref, bb_ref,
             dxa_ref, dxb_ref, sa_ref, sb_ref):
        i = pl.program_id(1)
        keep_p = (i > 0).astype(F32)
        keep_n = (i < nt - 1).astype(F32)

        def ext_of(prev_ref, cur_ref, next_ref):
            return jnp.concatenate(
                [prev_ref[...].astype(F32)[HALO - EDGE:, :] * keep_p, cur_ref[...].astype(F32),
                 next_ref[...].astype(F32)[:EDGE, :] * keep_n], axis=0)

        def taps(ext):
            return ext, pltpu.roll(ext, 1, 0), pltpu.roll(ext, 2, 0)

        def conv(tp, w, bias):
            return tp[0] * w[2:3, :] + tp[1] * w[1:2, :] + tp[2] * w[0:1, :] + bias

        xa = taps(ext_of(pa_ref, a_ref, na_ref))
        xb = taps(ext_of(pb_ref, b_ref, nb_ref))
        wa, wb = wa_ref[...], wb_ref[...]
        a = conv(xa, wa, ba_ref[...])
        b = conv(xb, wb, bb_ref[...])
        dy = jnp.concatenate(
            [jnp.zeros((EDGE, cw), F32), d_ref[...].astype(F32), nd_ref[...].astype(F32)[:EDGE, :] * keep_n], axis=0)
        sg = _sigmoid(a)
        da = dy * b * (sg * (1.0 + a * (1.0 - sg)))
        db = dy * (a * sg)

        def back(dup, tp, w, dx_ref, s_ref):
            dx = dup * w[2:3, :] + pltpu.roll(dup, ext_rows - 1, 0) * w[1:2, :] + pltpu.roll(dup, ext_rows - 2, 0) * w[0:1, :]
            dx_ref[...] = dx[EDGE:EDGE + tile, :].astype(dx_ref.dtype)
            dc = dup[EDGE:EDGE + tile, :]
            r2 = jnp.sum(dc * tp[0][EDGE:EDGE + tile, :], axis=0, keepdims=True)
            r1 = jnp.sum(dc * tp[1][EDGE:EDGE + tile, :], axis=0, keepdims=True)
            r0 = jnp.sum(dc * tp[2][EDGE:EDGE + tile, :], axis=0, keepdims=True)
            rb = jnp.sum(dc, axis=0, keepdims=True)
            row = lax.broadcasted_iota(jnp.int32, (8, cw), 0)
            upd = (jnp.where(row == 0, r0, 0.0) + jnp.where(row == 1, r1, 0.0) + jnp.where(row == 2, r2, 0.0)
                   + jnp.where(row == 3, rb, 0.0))

            @pl.when(i == 0)
            def _():
                s_ref[...] = upd

            @pl.when(i > 0)
            def _():
                s_ref[...] += upd

        back(da, xa, wa, dxa_ref, sa_ref)
        back(db, xb, wb, dxb_ref, sb_ref)

    def prev_of(shift):
        return pl.BlockSpec((HALO, cw), lambda j, i: (jnp.maximum(i * hb - 1, 0), j + shift))

    def next_of(shift):
        return pl.BlockSpec((HALO, cw), lambda j, i: (jnp.minimum((i + 1) * hb, nt * hb - 1), j + shift))

    def cur_of(shift):
        return pl.BlockSpec((tile, cw), lambda j, i: (i, j + shift))

    def row_of(rows, shift):
        return pl.BlockSpec((rows, cw), lambda j, i: (0, j + shift))

    return pl.pallas_call(
        body,
        name="conv_act_bwd",
        grid=(ncol, nt),
        in_specs=[prev_of(0), cur_of(0), next_of(0), prev_of(ncol), cur_of(ncol), next_of(ncol), cur_of(0), next_of(0),
                  row_of(3, 0), row_of(3, ncol), row_of(1, 0), row_of(1, ncol)],
        out_specs=[cur_of(0), cur_of(0), row_of(8, 0), row_of(8, 0)],
        out_shape=[_sds((T, D_FF), BF16), _sds((T, D_FF), BF16), _sds((8, D_FF), F32), _sds((8, D_FF), F32)],
        compiler_params=_cp(2),
    )(up_pre, up_pre, up_pre, up_pre, up_pre, up_pre, dact, dact, w_conv, w_conv, b_conv, b_conv)


NS_IN, NS_UP, NS_XKV, NS_UQ = 564, 704, 256, 96
E2_ROWS, E2_UQ = 384, 128
L1_W = NS_UP + NS_XKV
L2_ROWS = 736
L2_OUT, L2_XQ, L2_XO, L2_DN = 0, 128, 256, 384
RS_DN = 352
L3_ROWS, L3_PRET = 1024, 512
L4_SHAPE = (8, 768)
WZ_RUNS = ((416, 4096, 0), (0, 384, 4096), (384, 32, 4544))
WZ_ZERO = ((4480, 4544), (4576, 4608))


def _pieces(orig_start, length, dst_start, ns):
    out, c, d, end = [], orig_start, dst_start, orig_start + length
    while c < end:
        j, off = c // ns, c % ns
        ln = min(ns - off, end - c)
        out.append((j, off, d, ln))
        c += ln
        d += ln
    return out


def pack_early(w_in, w_ukv, w_uq):
    def body(in_ref, ukv_ref, uq_ref, e1_ref, e2_ref):
        e1_ref[...] = in_ref[...].astype(BF16)
        e2_ref[0:E2_UQ, :] = ukv_ref[...].astype(BF16)
        e2_ref[E2_UQ:E2_ROWS, 0:NS_UQ] = uq_ref[...].astype(BF16)
        e2_ref[E2_UQ:E2_ROWS, NS_UQ:HP] = jnp.zeros((E2_ROWS - E2_UQ, HP - NS_UQ), BF16)

    return pl.pallas_call(
        body,
        name="pack_early",
        in_specs=[VM] * 3,
        out_specs=[VM] * 2,
        out_shape=[_sds((D, NS_IN), BF16), _sds((E2_ROWS, HP), BF16)],
        compiler_params=_cp0(),
    )(w_in, w_ukv, w_uq)


def pack_late(w_up, w_xkv, w_out, w_xq, w_xo, w_down, w_pmla, w_pret, w_conv):
    def body(up_ref, xkv_ref, o_ref, xq_ref, xo_ref, dn_ref, pm_ref, pr_ref, cv_ref, l1_ref, l2_ref, l3_ref, l4_ref):
        l1_ref[:, 0:NS_UP] = up_ref[...].astype(BF16)
        l1_ref[:, NS_UP:L1_W] = xkv_ref[...].astype(BF16)
        l2_ref[L2_OUT:L2_XQ, :] = o_ref[...].astype(BF16)
        l2_ref[L2_XQ:L2_XO, :] = xq_ref[...].astype(BF16)
        l2_ref[L2_XO:L2_DN, :] = xo_ref[...].astype(BF16)
        l2_ref[L2_DN:L2_ROWS, :] = dn_ref[...].astype(BF16)
        l3_ref[0:L3_PRET, :] = pm_ref[...].astype(BF16)
        l3_ref[L3_PRET:L3_ROWS, :] = pr_ref[...].astype(BF16)
        l4_ref[...] = jnp.zeros(L4_SHAPE, F32)
        l4_ref[0:3, 0:NS_UP] = cv_ref[...]

    return pl.pallas_call(
        body,
        name="pack_late",
        in_specs=[VM] * 9,
        out_specs=[VM] * 4,
        out_shape=[_sds((D, L1_W), BF16), _sds((L2_ROWS, D), BF16), _sds((L3_ROWS, HP), BF16), _sds(L4_SHAPE, F32)],
        compiler_params=_cp0(),
    )(w_up, w_xkv, w_out, w_xq, w_xo, w_down, w_pmla, w_pret, w_conv)


def assemble_early(g1, g2, *, tile):
    def body(g1_ref, g2_ref, wz_ref, wk_ref, wv_ref, wq_ref):
        for lo, hi in WZ_ZERO:
            wz_ref[:, lo:hi] = jnp.zeros((tile, hi - lo), BF16)
        for os_, ln_, ds_ in WZ_RUNS:
            for j, off, d, ln in _pieces(os_, ln_, ds_, NS_IN):
                wz_ref[:, d:d + ln] = g1_ref[j, :, off:off + ln]

        @pl.when(pl.program_id(0) == 0)
        def _():
            half = jnp.zeros((MLA_KVR, HP - MLA_NOPE), BF16)
            for j in range(N_DEV):
                wk_ref[:, j * HP:j * HP + MLA_NOPE] = g2_ref[j, 0:E2_UQ, 0:MLA_NOPE]
                wk_ref[:, j * HP + MLA_NOPE:(j + 1) * HP] = half
                wv_ref[:, j * HP:j * HP + MLA_V] = g2_ref[j, 0:E2_UQ, MLA_NOPE:HP]
                wv_ref[:, j * HP + MLA_V:(j + 1) * HP] = half
                wq_ref[:, j * HP:j * HP + NS_UQ] = g2_ref[j, E2_UQ:E2_ROWS, 0:NS_UQ]
                wq_ref[:, j * HP + NS_UQ:(j + 1) * HP] = jnp.zeros((MLA_QR, HP - NS_UQ), BF16)

    def whole(r):
        return pl.BlockSpec((r, MLA_H * HP), lambda i: (0, 0))

    return pl.pallas_call(
        body,
        name="assemble_early",
        grid=(D // tile,),
        in_specs=[pl.BlockSpec((N_DEV, tile, NS_IN), lambda i: (0, i, 0)),
                  pl.BlockSpec((N_DEV, E2_ROWS, HP), lambda i: (0, 0, 0))],
        out_specs=[pl.BlockSpec((tile, ZW), lambda i: (i, 0)), whole(MLA_KVR), whole(MLA_KVR), whole(MLA_QR)],
        out_shape=[_sds((D, ZW), BF16), _sds((MLA_KVR, MLA_H * HP), BF16), _sds((MLA_KVR, MLA_H * HP), BF16),
                   _sds((MLA_QR, MLA_H * HP), BF16)],
        compiler_params=_cp(1),
    )(g1, g2)


def assemble_l1(g1, *, tile):
    def body(g_ref, wup_ref, wxkv_ref):
        for j in range(N_DEV):
            wup_ref[:, j * NS_UP:(j + 1) * NS_UP] = g_ref[j, :, 0:NS_UP]
            wxkv_ref[:, j * NS_XKV:(j + 1) * NS_XKV] = g_ref[j, :, NS_UP:L1_W]

    return pl.pallas_call(
        body,
        name="assemble_l1",
        grid=(D // tile,),
        in_specs=[pl.BlockSpec((N_DEV, tile, L1_W), lambda i: (0, i, 0))],
        out_specs=[pl.BlockSpec((tile, 2 * D_FF), lambda i: (i, 0)), pl.BlockSpec((tile, 2 * D), lambda i: (i, 0))],
        out_shape=[_sds((D, 2 * D_FF), BF16), _sds((D, 2 * D), BF16)],
        compiler_params=_cp(1),
    )(g1)


def assemble_l234(g2, g3, g4):
    def body(g2_ref, g3_ref, g4_ref, wo_ref, wxq_ref, wxo_ref, wdn_ref, wpa_ref, wpr_ref, wc_ref):
        for j in range(N_DEV):
            wo_ref[j * 128:(j + 1) * 128, :] = g2_ref[j, L2_OUT:L2_XQ, :]
            wxq_ref[j * 128:(j + 1) * 128, :] = g2_ref[j, L2_XQ:L2_XO, :]
            wxo_ref[j * 128:(j + 1) * 128, :] = g2_ref[j, L2_XO:L2_DN, :]
            wdn_ref[j * RS_DN:(j + 1) * RS_DN, :] = g2_ref[j, L2_DN:L2_ROWS, :]
            wpr_ref[:, j * 128:(j + 1) * 128] = g3_ref[j, L3_PRET:L3_ROWS, :]
            wc_ref[:, j * NS_UP:(j + 1) * NS_UP] = g4_ref[j, 0:3, 0:NS_UP]
            for h in range(MLA_H):
                wpa_ref[h * HP:h * HP + MLA_V, j * 128:(j + 1) * 128] = g3_ref[j, h * MLA_V:(h + 1) * MLA_V, :]
        for h in range(MLA_H):
            wpa_ref[h * HP + MLA_V:(h + 1) * HP, :] = jnp.zeros((HP - MLA_V, D), BF16)

    return pl.pallas_call(
        body,
        name="assemble_l234",
        in_specs=[VM] * 3,
        out_specs=[VM] * 7,
        out_shape=[_sds((D, D), BF16), _sds((D, D), BF16), _sds((D, D), BF16), _sds((D_FF, D), BF16),
                   _sds((MLA_H * HP, D), BF16), _sds((RET_H * RET_D, D), BF16), _sds((3, 2 * D_FF), F32)],
        compiler_params=_cp0(),
    )(g2, g3, g4)


def slots_l1(dwup_a, dwup_b, dwxkv, *, tile):
    per_half = D_FF // NS_UP

    def body(ua_ref, ub_ref, x_ref, s_ref):
        for j in range(N_DEV):
            src = ua_ref if j < per_half else ub_ref
            c0 = (j % per_half) * NS_UP
            s_ref[j, :, 0:NS_UP] = src[:, c0:c0 + NS_UP].astype(BF16)
            s_ref[j, :, NS_UP:L1_W] = x_ref[:, j * NS_XKV:(j + 1) * NS_XKV].astype(BF16)

    return pl.pallas_call(
        body,
        name="slots_l1",
        grid=(D // tile,),
        in_specs=[pl.BlockSpec((tile, D_FF), lambda i: (i, 0)), pl.BlockSpec((tile, D_FF), lambda i: (i, 0)),
                  pl.BlockSpec((tile, 2 * D), lambda i: (i, 0))],
        out_specs=pl.BlockSpec((N_DEV, tile, L1_W), lambda i: (0, i, 0)),
        out_shape=_sds((N_DEV, D, L1_W), BF16),
        compiler_params=_cp(1),
    )(dwup_a, dwup_b, dwxkv)


def slots_l23(dwo, dwxq, dwxo, dwdn, dwpa, dwpr):
    def body(o_ref, xq_ref, xo_ref, dn_ref, pa_ref, pr_ref, l2_ref, l3_ref):
        l2_ref[L2_OUT:L2_XQ, :] = o_ref[...].astype(BF16)
        l2_ref[L2_XQ:L2_XO, :] = xq_ref[...].astype(BF16)
        l2_ref[L2_XO:L2_DN, :] = xo_ref[...].astype(BF16)
        l2_ref[L2_DN:L2_ROWS, :] = dn_ref[...].astype(BF16)
        for h in range(MLA_H):
            l3_ref[h * MLA_V:(h + 1) * MLA_V, :] = pa_ref[h * HP:h * HP + MLA_V, :].astype(BF16)
        l3_ref[L3_PRET:L3_ROWS, :] = pr_ref[...].astype(BF16)

    rows128 = pl.BlockSpec((128, D), lambda j: (j, 0))

    def cols(r):
        return pl.BlockSpec((r, 128), lambda j: (0, j))

    return pl.pallas_call(
        body,
        name="slots_l23",
        grid=(N_DEV,),
        in_specs=[rows128, rows128, rows128, pl.BlockSpec((RS_DN, D), lambda j: (j, 0)),
                  cols(MLA_H * HP), cols(RET_H * RET_D)],
        out_specs=[pl.BlockSpec((None, L2_ROWS, D), lambda j: (j, 0, 0)),
                   pl.BlockSpec((None, L3_ROWS, HP), lambda j: (j, 0, 0))],
        out_shape=[_sds((N_DEV, L2_ROWS, D), BF16), _sds((N_DEV, L3_ROWS, HP), BF16)],
        compiler_params=_cp(1),
    )(dwo, dwxq, dwxo, dwdn, dwpa, dwpr)


def slots_early(dwz, dwk, dwv, dwq, *, tile):
    def body(dz_ref, k_ref, v_ref, q_ref, s1_ref, s2_ref):
        for os_, ln_, ds_ in WZ_RUNS:
            for j, off, d, ln in _pieces(os_, ln_, ds_, NS_IN):
                s1_ref[j, :, off:off + ln] = dz_ref[:, d:d + ln].astype(BF16)

        @pl.when(pl.program_id(0) == 0)
        def _():
            for j in range(N_DEV):
                s2_ref[j, 0:E2_UQ, 0:MLA_NOPE] = k_ref[:, j * HP:j * HP + MLA_NOPE].astype(BF16)
                s2_ref[j, 0:E2_UQ, MLA_NOPE:HP] = v_ref[:, j * HP:j * HP + MLA_V].astype(BF16)
                s2_ref[j, E2_UQ:E2_ROWS, 0:NS_UQ] = q_ref[:, j * HP:j * HP + NS_UQ].astype(BF16)
                s2_ref[j, E2_UQ:E2_ROWS, NS_UQ:HP] = jnp.zeros((E2_ROWS - E2_UQ, HP - NS_UQ), BF16)

    def whole(r):
        return pl.BlockSpec((r, MLA_H * HP), lambda i: (0, 0))

    return pl.pallas_call(
        body,
        name="slots_early",
        grid=(D // tile,),
        in_specs=[pl.BlockSpec((tile, ZW), lambda i: (i, 0)), whole(MLA_KVR), whole(MLA_KVR), whole(MLA_QR)],
        out_specs=[pl.BlockSpec((N_DEV, tile, NS_IN), lambda i: (0, i, 0)),
                   pl.BlockSpec((N_DEV, E2_ROWS, HP), lambda i: (0, 0, 0))],
        out_shape=[_sds((N_DEV, D, NS_IN), BF16), _sds((N_DEV, E2_ROWS, HP), BF16)],
        compiler_params=_cp(1),
    )(dwz, dwk, dwv, dwq)


SMALL = (("g_mix", 1024, 0), ("b_gate", 2048, 1), ("g_q_lat", 256, 3), ("g_kv_lat", 128, 4), ("g_ret", 512, 5),
         ("g_cross", 1024, 6), ("g_mem", 1024, 7), ("g_ffn", 1024, 8), ("b_conv", 5632, 9), ("g_final", 1024, 15))
SMALL_DIRECT = tuple(s for s in SMALL if s[0] != "b_conv")
S_ROWS = 16
LOSS_ROW, LOSS_LANE = 4, 128


def _flat_pieces(n, row0):
    return [(row0 + k // D, k, min(D, n - k)) for k in range(0, n, D)]


def exchange_small(cs_a, cs_b, loss_part, smalls):
    ns = len(smalls)

    def body(*refs):
        ca_ref, cb_ref, loss_ref = refs[:3]
        small_refs = refs[3:3 + ns]
        rd_ref, rs_ref, dsl_ref, own_ref = refs[3 + ns:7 + ns]
        sems = refs[7 + ns:]
        own_ref[...] = jnp.zeros(own_ref.shape, F32)
        own_ref[LOSS_ROW:LOSS_ROW + 1, LOSS_LANE:LOSS_LANE + HP] = loss_ref[0:1, :]
        for (name, n, row0), g_ref in zip(SMALL_DIRECT, small_refs):
            for r, c0, ln in _flat_pieces(n, row0):
                own_ref[r:r + 1, 0:ln] = g_ref[:, c0:c0 + ln]
        row0 = dict((s[0], s[2]) for s in SMALL)["b_conv"]
        for half, c_ref in enumerate((ca_ref, cb_ref)):
            k = half * D_FF
            end = k + D_FF
            while k < end:
                r, lane = row0 + k // D, k % D
                ln = min(D - lane, end - k)
                own_ref[r:r + 1, lane:lane + ln] = c_ref[3:4, k - half * D_FF:k - half * D_FF + ln]
                k += ln
        dsl_ref[...] = jnp.zeros(dsl_ref.shape, F32)
        per_half = D_FF // NS_UP
        for j in range(N_DEV):
            c_ref = ca_ref if j < per_half else cb_ref
            c0 = (j % per_half) * NS_UP
            dsl_ref[j, 0:3, 0:NS_UP] = c_ref[0:3, c0:c0 + NS_UP]
        comm = ([(dsl_ref, True), (own_ref, False)], [rd_ref, rs_ref]) + tuple(sems)
        _exchange_start(*comm)
        _exchange_wait(*comm)

    n1 = N_DEV - 1
    return pl.pallas_call(
        body,
        name="exchange_small",
        in_specs=[VM] * (3 + ns),
        out_specs=[VM, VM, VM, VM],
        out_shape=[_sds((n1,) + L4_SHAPE, F32), _sds((n1, S_ROWS, D), F32), _sds((N_DEV,) + L4_SHAPE, F32),
                   _sds((S_ROWS, D), F32)],
        scratch_shapes=_exchange_scratch(2),
        compiler_params=_cp0(),
    )(cs_a, cs_b, loss_part, *smalls)


def _adamw(w, g, m, v):
    m = ADAM_B1 * m + (1.0 - ADAM_B1) * g
    v = ADAM_B2 * v + (1.0 - ADAM_B2) * (g * g)
    m_hat = m / (1.0 - ADAM_B1 ** ADAM_STEP)
    v_hat = v / (1.0 - ADAM_B2 ** ADAM_STEP)
    delta = -ADAM_LR * (m_hat / (jnp.sqrt(v_hat) + ADAM_EPS) + ADAM_WD * w)
    return delta, m, v


def _apply(g, refs, outs):
    d, mn, vn = _adamw(refs[0][...], g, refs[1][...], refs[2][...])
    outs[0][...] = g
    outs[1][...] = d
    outs[2][...] = mn
    outs[3][...] = vn


def adam_cols(own, recv, wmv, spans, *, name, tile):
    R, W = own.shape
    nw = len(spans)

    def body(*refs):
        own_ref, recv_ref = refs[:2]
        ins, outs = refs[2:2 + 3 * nw], refs[2 + 3 * nw:]
        g = own_ref[...].astype(F32)
        for k in range(N_DEV - 1):
            g = g + recv_ref[k].astype(F32)
        for t, (lo, hi) in enumerate(spans):
            _apply(g[:, lo:hi], ins[3 * t:3 * t + 3], outs[4 * t:4 * t + 4])

    def blk(w):
        return pl.BlockSpec((tile, w), lambda i: (i, 0))

    widths = [hi - lo for lo, hi in spans]
    return pl.pallas_call(
        body,
        name=name,
        grid=(R // tile,),
        in_specs=[blk(W), pl.BlockSpec((N_DEV - 1, tile, W), lambda i: (0, i, 0))] + [blk(w) for w in widths for _ in range(3)],
        out_specs=[blk(w) for w in widths for _ in range(4)],
        out_shape=[_sds((R, w), F32) for w in widths for _ in range(4)],
        compiler_params=_cp(1),
    )(own, recv, *wmv)


def adam_rows(own, recv, wmv, spans, *, name):
    nw = len(spans)

    def body(*refs):
        own_ref, recv_ref = refs[:2]
        ins, outs = refs[2:2 + 3 * nw], refs[2 + 3 * nw:]
        for t, (lo, hi, w) in enumerate(spans):
            g = own_ref[lo:hi, :].astype(F32)
            for k in range(N_DEV - 1):
                g = g + recv_ref[k, lo:hi, :].astype(F32)
            _apply(g[:, 0:w], ins[3 * t:3 * t + 3], outs[4 * t:4 * t + 4])

    return pl.pallas_call(
        body,
        name=name,
        in_specs=[VM] * (2 + 3 * nw),
        out_specs=[VM] * (4 * nw),
        out_shape=[_sds((hi - lo, w), F32) for lo, hi, w in spans for _ in range(4)],
        compiler_params=_cp0(),
    )(own, recv, *wmv)


def adam_small(own_s, recv_s, dslots, recv_d, wmv_small, wmv_conv):
    ns = len(SMALL)

    def body(*refs):
        own_ref, rs_ref, dsl_ref, rd_ref = refs[:4]
        ins = refs[4:4 + 3 * ns + 3]
        outs = refs[4 + 3 * ns + 3:4 + 3 * ns + 3 + 4 * ns + 4]
        loss_ref, all_sc = refs[-2], refs[-1]
        me = _my_id()
        all_sc[0] = own_ref[...]
        for k in range(N_DEV - 1):
            all_sc[k + 1] = rs_ref[k]
        g = all_sc[jnp.bitwise_xor(me, 0)]
        for s in range(1, N_DEV):
            g = g + all_sc[jnp.bitwise_xor(me, s)]
        all_sc[0] = g
        loss_ref[...] = all_sc[0, LOSS_ROW:LOSS_ROW + 1, LOSS_LANE:LOSS_LANE + HP]
        for t, (name, n, row0) in enumerate(SMALL):
            pieces = [all_sc[0, r:r + 1, 0:ln] for r, _, ln in _flat_pieces(n, row0)]
            gt = pieces[0] if len(pieces) == 1 else jnp.concatenate(pieces, axis=1)
            _apply(gt, ins[3 * t:3 * t + 3], outs[4 * t:4 * t + 4])
        gc = dsl_ref[me]
        for k in range(N_DEV - 1):
            gc = gc + rd_ref[k]
        _apply(gc[0:3, 0:NS_UP], ins[3 * ns:3 * ns + 3], outs[4 * ns:4 * ns + 4])

    out_shape = ([_sds((1, n), F32) for _, n, _ in SMALL for _ in range(4)] + [_sds((3, NS_UP), F32)] * 4
                 + [_sds((1, HP), F32)])
    return pl.pallas_call(
        body,
        name="adam_small",
        in_specs=[VM] * (4 + 3 * ns + 3),
        out_specs=[VM] * len(out_shape),
        out_shape=out_shape,
        scratch_shapes=[pltpu.VMEM((N_DEV, S_ROWS, D), F32)],
    )(own_s, recv_s, dslots, recv_d, *wmv_small, *wmv_conv)


def kernel(x, mem, positions, g_mix, w_in, b_gate, g_q_lat, w_uq, g_kv_lat, w_ukv, w_proj_mla, g_ret, w_proj_ret, w_out, g_cross, g_mem, w_xq, w_xkv, w_xo, g_ffn, w_up, w_conv, b_conv, w_down, g_final, loss_target, m_g_mix, m_w_in, m_b_gate, m_g_q_lat, m_w_uq, m_g_kv_lat, m_w_ukv, m_w_proj_mla, m_g_ret, m_w_proj_ret, m_w_out, m_g_cross, m_g_mem, m_w_xq, m_w_xkv, m_w_xo, m_g_ffn, m_w_up, m_w_conv, m_b_conv, m_w_down, m_g_final, v_g_mix, v_w_in, v_b_gate, v_g_q_lat, v_w_uq, v_g_kv_lat, v_w_ukv, v_w_proj_mla, v_g_ret, v_w_proj_ret, v_w_out, v_g_cross, v_g_mem, v_w_xq, v_w_xkv, v_w_xo, v_g_ffn, v_w_up, v_w_conv, v_b_conv, v_w_down, v_g_final):
    args = dict(locals())
    T = x.shape[1]
    M = mem.shape[1]
    tile = min(256, T)
    tile2 = min(512, T)
    tile4 = min(1024, T)
    tq = min(512, T)
    rb = min(1024, T)

    xs = x[0]
    tgt = loss_target[0]
    mems = mem[0]

    def shard(name, prefix=""):
        a = args[prefix + name]
        return a.reshape(a.shape[-2:]) if a.ndim >= 2 else a.reshape(1, -1)

    e1, e2 = pack_early(shard("w_in"), shard("w_ukv"), shard("w_uq"))
    late_parts = pack_late(shard("w_up"), shard("w_xkv"), shard("w_out"), shard("w_xq"), shard("w_xo"), shard("w_down"),
                           shard("w_proj_mla"), shard("w_proj_ret"), shard("w_conv"))

    pos_f = jnp.broadcast_to(positions[0].astype(F32)[:, None], (T, 128))
    inv_m, inv_r = _rot_inv()
    u, cm, s1, s2, cr, sr, ge1, ge2 = rowwise(
        lambda xv, p, im, ir, g: (_rms(xv, g),) + _rot_tables(p, im, ir), [(xs, None), (pos_f, None)],
        [inv_m, inv_r, g_mix], [(D, BF16)] + [(128, F32)] * 5, [], tile=tile2, name="norm_mix_tables", gather=[e1, e2])
    wz, wk, wv, wq = assemble_early(ge1, ge2, tile=tile)
    rconsts = _ret_consts()

    z = matmul(u, wz, name="mm_z", tm=2048, tn=1536)

    def mixers_in(zl, zr, cmv, s1v, s2v, crv, srv, gq, gkv, wqv, wkv_, wvv):
        cq = _rms(zl[:, 0:256], gq).astype(BF16)
        ckv = _rms(zl[:, 256:384], gkv).astype(BF16)
        qv = _rot_mla(_dot(cq, wqv), cmv, s1v, s2v)
        kr = _rot_mla(zl[:, 384:512], cmv, s1v, s2v)
        kn = _dot(ckv, wkv_)
        kv_ = jnp.concatenate([kn[:, h * HP:(h + 1) * HP] + kr for h in range(MLA_H)], axis=1)
        vv = _dot(ckv, wvv)
        lane = lax.broadcasted_iota(jnp.int32, vv.shape, 1)
        vv = jnp.where((lane & (HP - 1)) == ONE_LANE, 1.0, vv)
        rqv = _rot_ret(zr[:, 0:512], crv, srv)
        rkv = _rot_ret(zr[:, 512:1024], crv, srv) * (RET_D ** -0.5)
        return qv, kv_, vv, rqv, rkv, zr[:, 1024:1536]

    q_a, k_a, v_a, rq, rk, rv = rowwise(
        mixers_in, [(z, (512, 8)), (z, (2048, 0)), (cm, None), (s1, None), (s2, None), (cr, None), (sr, None)],
        [g_q_lat, g_kv_lat, wq, wk, wv], [(MLA_H * HP, BF16)] * 3 + [(512, BF16)] * 3, [], tile=tile2, name="mixers_in")

    o_a, lse_row, gl1, gl2, gl3, gl4 = mla_fwd(q_a, k_a, v_a, list(late_parts), tq=tq)
    wup, wxkv = assemble_l1(gl1, tile=tile)
    wo, wxq, wxo, wdn, wpa, wpr, wcv = assemble_l234(gl2, gl3, gl4)
    ret, rstate = ret_fwd(rq, rk, rv, rconsts, rb=rb)

    def gn_parts(r):
        outs = []
        for h in range(RET_H):
            rh = r[:, h * RET_D:(h + 1) * RET_D]
            mu = jnp.mean(rh, axis=-1, keepdims=True)
            dlt = rh - mu
            rstd = lax.rsqrt(jnp.mean(dlt * dlt, axis=-1, keepdims=True) + EPS)
            outs.append((dlt * rstd, rstd))
        return outs

    def mix_fwd(ov, rv_, rg, gt, wpav, wprv, gr, bg):
        ya = _dot(ov, wpav)
        xh = jnp.concatenate([p[0] for p in gn_parts(rv_)], axis=1)
        t = rg * _sigmoid(rg) * (xh * gr)
        yr = _dot(t.astype(BF16), wprv)
        ga_ = _sigmoid(gt[:, :D] + bg[:, :D])
        gr_ = _sigmoid(gt[:, D:] + bg[:, D:])
        return ga_ * ya + gr_ * yr

    mix = rowwise(mix_fwd, [(o_a, None), (ret, None), (z, (512, 3)), (z, (2048, 1))], [wpa, wpr, g_ret, b_gate],
                  [(D, BF16)], [], tile=tile2, name="mix_fwd")[0]
    def proj_norm(av, rv_, wv_, g):
        hv = rv_ + _dot(av, wv_)
        return hv, _rms(hv, g)

    h1, n2 = rowwise(proj_norm, [(mix, None), (xs, None)], [wo, g_cross], [(D, F32), (D, BF16)], [], tile=tile4,
                     name="mm_out_norm")
    xq = matmul(n2, wxq, out_dtype=BF16, name="mm_xq")
    mn = rowwise(lambda mv_, g: _rms(mv_, g), [(mems, None)], [g_mem], [(D, BF16)], [], tile=min(tile, M), name="norm_mem")[0]
    mkv = matmul(mn, wxkv, out_dtype=BF16, name="mm_mkv")

    x_scale = X_HD ** -0.5

    def xattn_fwd(xqv, mkvv):
        outs = []
        for h in range(X_H):
            sl = slice(h * X_HD, (h + 1) * X_HD)
            s = _dot_nt(xqv[:, sl], mkvv[:, sl]) * x_scale
            s = s - jnp.max(s, axis=-1, keepdims=True)
            e = jnp.exp(s)
            p = e / jnp.sum(e, axis=-1, keepdims=True)
            outs.append(_dot(p.astype(BF16), mkvv[:, D + h * X_HD:D + (h + 1) * X_HD]))
        return jnp.concatenate(outs, axis=1)

    xo = rowwise(xattn_fwd, [(xq, None)], [mkv], [(D, BF16)], [], tile=tile4, name="xattn_fwd")[0]
    h2, n3 = rowwise(proj_norm, [(xo, None), (h1, None)], [wxo, g_ffn], [(D, F32), (D, BF16)], [], tile=tile4,
                     name="mm_xo_norm")
    up_pre = matmul(n3, wup, out_dtype=BF16, name="mm_up", tm=2048, tn=1408)
    cw = D_FF // 2
    act = conv_act_fwd(up_pre, wcv, b_conv, tile=tile, cw=cw)

    def down_loss(av, hv2, tv, wv_, g):
        hv = hv2 + _dot(av, wv_)
        y = _rms(hv, g)
        err = y - tv
        part = 0.5 * jnp.sum(jnp.sum(err * err, axis=-1, keepdims=True) / D, axis=0, keepdims=True)
        dx, dg = _rms_bwd(hv, g, err / D)
        return dx, dg, jnp.broadcast_to(part, (8, 128))

    g_fin2 = g_final.reshape(1, D)
    dh3, dg_final, loss_acc = rowwise(down_loss, [(act, None), (h2, None), (tgt, None)], [wdn, g_fin2], [(D, F32)],
                                      [((1, D), F32), ((8, 128), F32)], tile=tile2, name="mm_down_loss")

    dact = matmul(dh3, wdn, tb=True, out_dtype=BF16, name="mm_dact", tm=2048, tn=1408)
    dw_down = matmul_tn(act, dh3, name="mm_dw_down", tm=1408, tk=1024)
    dup_a, dup_b, cs_a, cs_b = conv_act_bwd(up_pre, dact, wcv, b_conv, tile=tile, cw=cw)
    dw_up_a = matmul_tn(n3, dup_a, name="mm_dw_up_a", tn=1408)
    dw_up_b = matmul_tn(n3, dup_b, name="mm_dw_up_b", tn=1408)

    def ffn_in_bwd(da_, db_, hv, drv, wv_, g):
        dn = _dot_nt(da_, wv_[:, :D_FF]) + _dot_nt(db_, wv_[:, D_FF:])
        dx, dg = _rms_bwd(hv, g, dn)
        return dx + drv, dg

    dh2, dg_ffn = rowwise(ffn_in_bwd, [(dup_a, None), (dup_b, None), (h2, None), (dh3, None)], [wup, g_ffn],
                          [(D, F32)], [((1, D), F32)], tile=tile2, name="mm_dn3_norm_bwd")
    dxo = matmul(dh2, wxo, tb=True, out_dtype=BF16, name="mm_dxo")
    dw_xo = matmul_tn(xo, dh2, name="mm_dw_xo")

    def xattn_bwd(xqv, dxov, mkvv):
        dxq, dmk, dmv = [], [], []
        for h in range(X_H):
            sl = slice(h * X_HD, (h + 1) * X_HD)
            slv = slice(D + h * X_HD, D + (h + 1) * X_HD)
            s = _dot_nt(xqv[:, sl], mkvv[:, sl]) * x_scale
            s = s - jnp.max(s, axis=-1, keepdims=True)
            e = jnp.exp(s)
            p = e / jnp.sum(e, axis=-1, keepdims=True)
            dp = _dot_nt(dxov[:, sl], mkvv[:, slv])
            ds = (p * (dp - jnp.sum(dp * p, axis=-1, keepdims=True)) * x_scale).astype(BF16)
            dxq.append(_dot(ds, mkvv[:, sl]))
            dmk.append(_dot_tn(ds, xqv[:, sl]))
            dmv.append(_dot_tn(p.astype(BF16), dxov[:, sl]))
        return jnp.concatenate(dxq, axis=1), jnp.concatenate(dmk + dmv, axis=1)

    dxq, dmkv = rowwise(xattn_bwd, [(xq, None), (dxo, None)], [mkv], [(D, BF16)], [((M, 2 * D), F32)],
                        tile=tile4, name="xattn_bwd")
    dw_xq = matmul_tn(n2, dxq, name="mm_dw_xq")

    def proj_norm_bwd(dyv, hv, drv, wv_, g):
        dx, dg = _rms_bwd(hv, g, _dot_nt(dyv, wv_))
        return dx + drv, dg

    dh1, dg_cross = rowwise(proj_norm_bwd, [(dxq, None), (h1, None), (dh2, None)], [wxq, g_cross], [(D, F32)],
                            [((1, D), F32)], tile=tile4, name="mm_dn2_norm_bwd")
    dw_xkv = matmul_tn(mn, dmkv, name="mm_dw_xkv", tk=M)
    dmn = matmul(dmkv, wxkv, tb=True, name="mm_dmn", tm=M)
    dg_mem = rowwise(lambda mv_, dyv, g: _rms_bwd(mv_, g, dyv)[1], [(mems, None), (dmn, None)], [g_mem], [],
                     [((1, D), F32)], tile=min(tile, M), name="norm_mem_bwd")[0]

    dmix = matmul(dh1, wo, tb=True, out_dtype=BF16, name="mm_dmix")
    dw_out = matmul_tn(mix, dh1, name="mm_dw_out")

    def mix_bwd(ov, rv_, rg, gt, dmv_, wpav, wprv, gr, bg):
        dm_ = dmv_.astype(F32)
        ya = _dot(ov, wpav)
        parts = gn_parts(rv_)
        xh = jnp.concatenate([p[0] for p in parts], axis=1)
        yn = xh * gr
        sg = _sigmoid(rg)
        sl_ = rg * sg
        t = (sl_ * yn).astype(BF16)
        yr = _dot(t, wprv)
        ga_ = _sigmoid(gt[:, :D] + bg[:, :D])
        gr_ = _sigmoid(gt[:, D:] + bg[:, D:])
        dgates = jnp.concatenate([dm_ * ya * ga_ * (1.0 - ga_), dm_ * yr * gr_ * (1.0 - gr_)], axis=1)
        dya = (dm_ * ga_).astype(BF16)
        dyr = (dm_ * gr_).astype(BF16)
        do_ = _dot_nt(dya, wpav)
        dwpa_ = _dot_tn(ov, dya)
        dt = _dot_nt(dyr, wprv)
        dwpr_ = _dot_tn(t, dyr)
        drg_ = dt * yn * (sg * (1.0 + rg * (1.0 - sg)))
        dyn = dt * sl_
        dgr = jnp.sum(dyn * xh, axis=0, keepdims=True)
        dxh = dyn * gr
        drets = []
        for h in range(RET_H):
            sl = slice(h * RET_D, (h + 1) * RET_D)
            xhh, rstd = parts[h]
            dxhh = dxh[:, sl]
            drets.append(rstd * (dxhh - jnp.mean(dxhh, axis=-1, keepdims=True)
                                 - xhh * jnp.mean(dxhh * xhh, axis=-1, keepdims=True)))
        dret_ = jnp.concatenate(drets, axis=1)
        dbg = jnp.sum(dgates, axis=0, keepdims=True)
        prod = ov.astype(F32) * do_.astype(BF16).astype(F32)
        lane = lax.broadcasted_iota(jnp.int32, (prod.shape[0], HP), 1)
        dlt = jnp.zeros((prod.shape[0], HP), F32)
        for h in range(MLA_H):
            dlt = jnp.where(lane == h, jnp.sum(prod[:, h * HP:(h + 1) * HP], axis=-1, keepdims=True), dlt)
        return do_, dret_, drg_, dgates, dlt, dwpa_, dwpr_, dgr, dbg

    do_a, dret, drg, dgates, delta, dwpa, dw_proj_ret, dg_ret, db_gate = rowwise(
        mix_bwd, [(o_a, None), (ret, None), (z, (512, 3)), (z, (2048, 1)), (dmix, None)], [wpa, wpr, g_ret, b_gate],
        [(MLA_H * HP, BF16), (512, F32), (512, BF16), (2 * D, BF16), (HP, F32)],
        [((MLA_H * HP, D), F32), ((512, D), F32), ((1, 512), F32), ((1, 2 * D), F32)], tile=tile2, name="mix_bwd")

    sl1 = slots_l1(dw_up_a, dw_up_b, dw_xkv, tile=tile)
    sl2, sl3 = slots_l23(dw_out, dw_xq, dw_xo, dw_down, dwpa, dw_proj_ret)
    delta_row = mla_prep(delta, tq=tq)
    dq_a, dk_a, dv_a, rl1, rl2, rl3 = mla_bwd(q_a, k_a, v_a, do_a, lse_row, delta_row, [sl1, sl2, sl3], tq=tq)
    drq_r, drk_r, drv = ret_bwd(rq, rk, rv, rstate, dret, rconsts, rb=rb)

    def mixers_in_bwd(zl, cmv, s1v, s2v, dqv, dkv_, dvv, drq_, drk_, drv_, drg_, dgt, crv, srv, gq, gkv, wqv, wkv_, wvv):
        cqf, ckvf = zl[:, 0:256], zl[:, 256:384]
        cq = _rms(cqf, gq).astype(BF16)
        ckv = _rms(ckvf, gkv).astype(BF16)
        dq_pre = _rot_mla(dqv.astype(F32), cmv, -s1v, -s2v).astype(BF16)
        dkf = dkv_.astype(F32)
        dkr = dkf[:, 0:HP]
        for h in range(1, MLA_H):
            dkr = dkr + dkf[:, h * HP:(h + 1) * HP]
        lane = lax.broadcasted_iota(jnp.int32, dkr.shape, 1)
        dzk = _rot_mla(jnp.where((lane >= 64) & (lane < 96), dkr, 0.0), cmv, -s1v, -s2v)
        dkb = dkv_.astype(BF16)
        dvb = dvv.astype(BF16)
        dcq_n = _dot_nt(dq_pre, wqv)
        dckv_n = _dot_nt(dkb, wkv_) + _dot_nt(dvb, wvv)
        dwq_ = _dot_tn(cq, dq_pre)
        dwk_ = _dot_tn(ckv, dkb)
        dwv_ = _dot_tn(ckv, dvb)
        dcq, dgq = _rms_bwd(cqf, gq, dcq_n)
        dckv, dgkv = _rms_bwd(ckvf, gkv, dckv_n)
        a = _rot_ret(drq_, crv, -srv)
        b = _rot_ret(drk_, crv, -srv) * (RET_D ** -0.5)
        dz_ = jnp.concatenate([a.astype(BF16), b.astype(BF16), drv_, drg_, dgt,
                               dcq.astype(BF16), dckv.astype(BF16), dzk.astype(BF16)], axis=1)
        return dz_, dwq_, dwk_, dwv_, dgq, dgkv

    dz, dwq, dwk, dwv, dg_q_lat, dg_kv_lat = rowwise(
        mixers_in_bwd, [(z, (512, 8)), (cm, None), (s1, None), (s2, None), (dq_a, None), (dk_a, None), (dv_a, None),
                        (drq_r, None), (drk_r, None), (drv, None), (drg, None), (dgates, None), (cr, None), (sr, None)],
        [g_q_lat, g_kv_lat, wq, wk, wv], [(ZW, BF16)],
        [((MLA_QR, MLA_H * HP), F32), ((MLA_KVR, MLA_H * HP), F32), ((MLA_KVR, MLA_H * HP), F32),
         ((1, MLA_QR), F32), ((1, MLA_KVR), F32)], tile=tile2, name="mixers_in_bwd")
    dwz = matmul_tn(u, dz, name="mm_dw_z", tn=1536)
    se1, se2 = slots_early(dwz, dwk, dwv, dwq, tile=tile)
    grad_x, dg_mix, re1, re2 = rowwise(proj_norm_bwd, [(dz, None), (xs, None), (dh1, None)], [wz, g_mix], [(D, F32)],
                                       [((1, D), F32)], tile=tile2, name="mm_du_norm_bwd", exchange=[se1, se2])

    small_grads = {"g_mix": dg_mix, "b_gate": db_gate, "g_q_lat": dg_q_lat, "g_kv_lat": dg_kv_lat, "g_ret": dg_ret,
                   "g_cross": dg_cross, "g_mem": dg_mem, "g_ffn": dg_ffn, "g_final": dg_final}
    rd, rs, dslots, own_s = exchange_small(cs_a, cs_b, loss_acc, [small_grads[n] for n, _, _ in SMALL_DIRECT])

    me = _my_id()

    def own(slots):
        return lax.dynamic_index_in_dim(slots, me, axis=0, keepdims=False)

    def wmv(names):
        return [shard(n, p) for n in names for p in ("", "m_", "v_")]

    names_s = tuple(n for n, _, _ in SMALL)
    small_outs = adam_small(own_s, rs, dslots, rd, wmv(names_s), wmv(("w_conv",)))
    groups = (
        (("w_in",), adam_cols(own(se1), re1, wmv(("w_in",)), ((0, NS_IN),), name="adam_e1", tile=128)),
        (("w_ukv", "w_uq"), adam_rows(own(se2), re2, wmv(("w_ukv", "w_uq")),
                                      ((0, E2_UQ, HP), (E2_UQ, E2_ROWS, NS_UQ)), name="adam_e2")),
        (("w_up", "w_xkv"), adam_cols(own(sl1), rl1, wmv(("w_up", "w_xkv")), ((0, NS_UP), (NS_UP, L1_W)),
                                      name="adam_l1", tile=128)),
        (("w_out", "w_xq", "w_xo", "w_down"),
         adam_rows(own(sl2), rl2, wmv(("w_out", "w_xq", "w_xo", "w_down")),
                   ((L2_OUT, L2_XQ, D), (L2_XQ, L2_XO, D), (L2_XO, L2_DN, D), (L2_DN, L2_ROWS, D)), name="adam_l2")),
        (("w_proj_mla", "w_proj_ret"), adam_rows(own(sl3), rl3, wmv(("w_proj_mla", "w_proj_ret")),
                                                 ((0, L3_PRET, HP), (L3_PRET, L3_ROWS, HP)), name="adam_l3")),
        (names_s + ("w_conv",), small_outs),
    )
    loss = small_outs[-1][0, 0]
    res = {}
    for names, outs_ in groups:
        for t, n in enumerate(names):
            res[n] = outs_[4 * t:4 * t + 4]

    order = ["g_mix", "w_in", "b_gate", "g_q_lat", "w_uq", "g_kv_lat", "w_ukv", "w_proj_mla", "g_ret", "w_proj_ret",
             "w_out", "g_cross", "g_mem", "w_xq", "w_xkv", "w_xo", "g_ffn", "w_up", "w_conv", "b_conv", "w_down",
             "g_final"]
    outs = [loss, grad_x[None]]
    for kind in range(4):
        outs += [res[n][kind].reshape(args[n].shape) for n in order]
    return tuple(outs)
```

```python
import functools
import math

import jax
import jax.numpy as jnp
import numpy as np
from jax import lax
from jax.experimental import pallas as pl
from jax.experimental.pallas import tpu as pltpu

F32 = jnp.float32
BF16 = jnp.bfloat16

D = 1024
MLA_H, MLA_NOPE, MLA_ROPE, MLA_V = 8, 64, 32, 64
MLA_QR, MLA_KVR = 256, 128
RET_H, RET_D, RET_C = 4, 128, 128
X_H, X_HD = 4, 256
D_FF = 2816
THETA = 10000.0
EPS = 1e-6
HP = 128
ZW = 4608
N_DEV = 8

ADAM_LR, ADAM_B1, ADAM_B2, ADAM_EPS, ADAM_WD, ADAM_STEP = 0.001, 0.9, 0.999, 1e-08, 0.01, 10

VMEM_LIMIT = 56 * 1024 * 1024
BIG_CONST_BYTES = 4 * 1024 * 1024
MESH = pl.DeviceIdType.MESH
VM = pl.BlockSpec(memory_space=pltpu.VMEM)
ANY = pl.BlockSpec(memory_space=pl.ANY)


def _cp(n_axes):
    return pltpu.CompilerParams(dimension_semantics=("arbitrary",) * n_axes, vmem_limit_bytes=VMEM_LIMIT)


def _cp0():
    return pltpu.CompilerParams(vmem_limit_bytes=VMEM_LIMIT)


def _pick(n, cap, mult=128):
    best = None
    for t in range(mult, min(n, cap) + 1, mult):
        if n % t == 0:
            best = t
    return best if best is not None else n


def _dot(a, b):
    return jnp.dot(a, b, preferred_element_type=F32)


def _dot_nt(a, b):
    return lax.dot_general(a, b, (((1,), (1,)), ((), ())), preferred_element_type=F32)


def _dot_tn(a, b):
    return lax.dot_general(a, b, (((0,), (0,)), ((), ())), preferred_element_type=F32)


def _sds(shape, dtype):
    return jax.ShapeDtypeStruct(shape, dtype)


def matmul(a, b, *, name, tb=False, out_dtype=F32, tm=1024, tn=1024):
    M, K = a.shape
    N = b.shape[0] if tb else b.shape[1]
    tm = _pick(M, tm, 8)
    tn = _pick(N, tn)

    def body(a_ref, b_ref, o_ref):
        av = a_ref[...].astype(BF16)
        bv = b_ref[...].astype(BF16)
        acc = _dot_nt(av, bv) if tb else _dot(av, bv)
        o_ref[...] = acc.astype(o_ref.dtype)

    return pl.pallas_call(
        body,
        name=name,
        grid=(M // tm, N // tn),
        in_specs=[pl.BlockSpec((tm, K), lambda i, j: (i, 0)),
                  pl.BlockSpec((tn, K), lambda i, j: (j, 0)) if tb else pl.BlockSpec((K, tn), lambda i, j: (0, j))],
        out_specs=pl.BlockSpec((tm, tn), lambda i, j: (i, j)),
        out_shape=_sds((M, N), out_dtype),
        compiler_params=_cp(2),
    )(a, b)


def matmul_tn(a, b, *, name, tm=1024, tn=1024, tk=2048):
    R, M = a.shape
    N = b.shape[1]
    tm = _pick(M, tm)
    tn = _pick(N, tn)
    tk = _pick(R, tk, 16)
    nk = R // tk

    def body(a_ref, b_ref, o_ref, acc_ref):
        k = pl.program_id(2)

        @pl.when(k == 0)
        def _():
            acc_ref[...] = jnp.zeros_like(acc_ref)

        acc_ref[...] += _dot_tn(a_ref[...].astype(BF16), b_ref[...].astype(BF16))

        @pl.when(k == nk - 1)
        def _():
            o_ref[...] = acc_ref[...]

    return pl.pallas_call(
        body,
        name=name,
        grid=(M // tm, N // tn, nk),
        in_specs=[pl.BlockSpec((tk, tm), lambda i, j, k: (k, i)), pl.BlockSpec((tk, tn), lambda i, j, k: (k, j))],
        out_specs=pl.BlockSpec((tm, tn), lambda i, j, k: (i, j)),
        out_shape=_sds((M, N), F32),
        scratch_shapes=[pltpu.VMEM((tm, tn), F32)],
        compiler_params=_cp(3),
    )(a, b)


def rowwise(fn, rows, consts, out_rows, out_accs, *, tile, name, gather=(), exchange=()):
    T = rows[0][0].shape[0]
    nt = T // tile
    travel = tuple(gather) + tuple(exchange)
    n_r, n_c, n_o, n_a, n_g = len(rows), len(consts), len(out_rows), len(out_accs), len(travel)
    n_in = n_r + n_c + n_g

    def body(*refs):
        if n_g:
            srcs, dsts = refs[n_r + n_c:n_in], refs[n_in + n_o + n_a:n_in + n_o + n_a + n_g]
            sems = tuple(refs[n_in + n_o + n_a + n_g:])
            if gather:
                comm = (srcs, dsts) + sems
                comm_start, comm_finish = _gather2_start, _gather2_finish
            else:
                comm = ([(r, True) for r in srcs], dsts) + sems
                comm_start, comm_finish = _exchange_start, _exchange_wait

            @pl.when(pl.program_id(0) == 0)
            def _():
                comm_start(*comm)

        ins = [r[...] for r in refs[: n_r + n_c]]
        outs = fn(*ins)
        if not isinstance(outs, (tuple, list)):
            outs = (outs,)
        o_refs = refs[n_in:n_in + n_o]
        a_refs = refs[n_in + n_o:n_in + n_o + n_a]
        for o_ref, o in zip(o_refs, outs[:n_o]):
            o_ref[...] = o.astype(o_ref.dtype)
        if n_a:
            first = pl.program_id(0) == 0

            @pl.when(first)
            def _():
                for a_ref, o in zip(a_refs, outs[n_o:]):
                    a_ref[...] = o.astype(a_ref.dtype)

            @pl.when(jnp.logical_not(first))
            def _():
                for a_ref, o in zip(a_refs, outs[n_o:]):
                    a_ref[...] += o.astype(a_ref.dtype)
        if n_g:

            @pl.when(pl.program_id(0) == nt - 1)
            def _():
                comm_finish(*comm)

    in_specs = []
    args = []
    for arr, win in rows:
        if win is None:
            in_specs.append(pl.BlockSpec((tile, arr.shape[1]), lambda i: (i, 0)))
        else:
            w, cb = win
            in_specs.append(pl.BlockSpec((tile, w), functools.partial(lambda i, cb: (i, cb), cb=cb)))
        args.append(arr)
    for c in consts:
        index_map = functools.partial(lambda i, nd: (0,) * nd, nd=c.ndim)
        if c.size * c.dtype.itemsize >= BIG_CONST_BYTES:
            in_specs.append(pl.BlockSpec(c.shape, index_map, pipeline_mode=pl.Buffered(1)))
        else:
            in_specs.append(pl.BlockSpec(c.shape, index_map))
        args.append(c)
    out_specs = [pl.BlockSpec((tile, w), lambda i: (i, 0)) for w, _ in out_rows]
    out_shape = [_sds((T, w), dt) for w, dt in out_rows]
    for shp, dt in out_accs:
        out_specs.append(pl.BlockSpec(shp, functools.partial(lambda i, nd: (0,) * nd, nd=len(shp))))
        out_shape.append(_sds(shp, dt))
    return pl.pallas_call(
        body,
        name=name,
        grid=(nt,),
        in_specs=in_specs + [ANY] * n_g,
        out_specs=out_specs + [ANY] * n_g,
        out_shape=out_shape + [_sds((N_DEV,) + p.shape, p.dtype) for p in gather]
        + [_sds((N_DEV - 1,) + s.shape[1:], s.dtype) for s in exchange],
        scratch_shapes=(_gather_scratch(n_g) if gather else _exchange_scratch(n_g)) if n_g else [],
        compiler_params=_cp(1),
    )(*args, *travel)


def _rms(x, g):
    r = lax.rsqrt(jnp.mean(x * x, axis=-1, keepdims=True) + EPS)
    return x * r * g


def _rms_bwd(x, g, dy):
    r = lax.rsqrt(jnp.mean(x * x, axis=-1, keepdims=True) + EPS)
    xh = x * r
    dg = jnp.sum(dy * xh, axis=0, keepdims=True)
    dxh = dy * g
    dx = r * (dxh - xh * jnp.mean(dxh * xh, axis=-1, keepdims=True))
    return dx, dg


def _sigmoid(x):
    return 0.5 * jnp.tanh(0.5 * x) + 0.5


def _rot_mla(x, c, s1, s2):
    n = x.shape[1] // HP
    outs = []
    for h in range(n):
        xh = x[:, h * HP : (h + 1) * HP]
        outs.append(xh * c + pltpu.roll(xh, HP - 16, 1) * s1 + pltpu.roll(xh, 16, 1) * s2)
    return outs[0] if n == 1 else jnp.concatenate(outs, axis=1)


def _rot_ret(x, c, s):
    n = x.shape[1] // RET_D
    outs = []
    for h in range(n):
        xh = x[:, h * RET_D : (h + 1) * RET_D]
        outs.append(xh * c + pltpu.roll(xh, RET_D // 2, 1) * s)
    return outs[0] if n == 1 else jnp.concatenate(outs, axis=1)


def _rot_inv():
    lane_np = np.arange(128)
    inv_m = (jnp.asarray(THETA, F32) ** (-jnp.asarray(lane_np & 15, F32) / 16.0)).reshape(1, 128)
    inv_r = (jnp.asarray(THETA, F32) ** (-jnp.asarray(lane_np & 63, F32) / 64.0)).reshape(1, 128)
    return inv_m, inv_r


def _rot_tables(p, im, ir):
    lane = lax.broadcasted_iota(jnp.int32, p.shape, 1)
    ang = p * im
    cm = jnp.where((lane >= 64) & (lane < 96), jnp.cos(ang), 1.0)
    sn = jnp.sin(ang)
    s1 = jnp.where((lane >= 64) & (lane < 80), -sn, 0.0)
    s2 = jnp.where((lane >= 80) & (lane < 96), sn, 0.0)
    angr = p * ir
    snr = jnp.sin(angr)
    return cm, s1, s2, jnp.cos(angr), jnp.where(lane < 64, -snr, snr)


def _peer(m):
    x, y, c = lax.axis_index("x"), lax.axis_index("y"), lax.axis_index("c")
    mx, my, mc = (m >> 2) & 1, (m >> 1) & 1, m & 1
    px = 1 - x if mx else x
    py = 1 - y if my else y
    pc = 1 - c if mc else c
    return (px, py, pc), 4 * px + 2 * py + pc


def _my_id():
    return 4 * lax.axis_index("x") + 2 * lax.axis_index("y") + lax.axis_index("c")


def _gather_copies(srcs, outs, send_sems, recv_sems, local_sems, arriving=False):
    me = _my_id()
    copies = []
    if not arriving:
        for g, (src, out) in enumerate(zip(srcs, outs)):
            copies.append((pltpu.make_async_copy(src, out.at[me], local_sems.at[g]), False))
    for m in range(1, N_DEV):
        peer, plin = _peer(m)
        for g, (src, out) in enumerate(zip(srcs, outs)):
            copies.append((pltpu.make_async_remote_copy(
                src_ref=src, dst_ref=out.at[plin if arriving else me], send_sem=send_sems.at[g, m - 1],
                recv_sem=recv_sems.at[g, m - 1], device_id=peer, device_id_type=MESH), True))
    return copies


def _gather_start(*a):
    for cp, _ in _gather_copies(*a):
        cp.start()


def _gather_wait(*a):
    for cp, _ in _gather_copies(*a, arriving=True):
        cp.wait_recv()
    for cp, remote in _gather_copies(*a):
        if remote:
            cp.wait_send()
        else:
            cp.wait()


CHIP_RELATIONS = (4, 2, 6)


def _gather2_copy(src, out, block, to, send_sems, recv_sems, g, k):
    return pltpu.make_async_remote_copy(src_ref=src, dst_ref=out.at[block], send_sem=send_sems.at[g, k],
                                        recv_sem=recv_sems.at[g, k], device_id=to, device_id_type=MESH)


def _gather2_start(srcs, outs, send_sems, recv_sems, local_sems):
    me = _my_id()
    sib, _ = _peer(1)
    for g, (src, out) in enumerate(zip(srcs, outs)):
        pltpu.make_async_copy(src, out.at[me], local_sems.at[g]).start()
        _gather2_copy(src, out, me, sib, send_sems, recv_sems, g, 0).start()
        for t, m in enumerate(CHIP_RELATIONS):
            _gather2_copy(src, out, me, _peer(m)[0], send_sems, recv_sems, g, 1 + t).start()


def _gather2_finish(srcs, outs, send_sems, recv_sems, local_sems):
    me = _my_id()
    sib, sib_lin = _peer(1)
    for t, m in enumerate(CHIP_RELATIONS):
        peer, plin = _peer(m)
        for g, (src, out) in enumerate(zip(srcs, outs)):
            _gather2_copy(src, out, plin, peer, send_sems, recv_sems, g, 1 + t).wait_recv()
            _gather2_copy(out.at[plin], out, plin, sib, send_sems, recv_sems, g, 4 + t).start()
    for g, (src, out) in enumerate(zip(srcs, outs)):
        _gather2_copy(src, out, sib_lin, sib, send_sems, recv_sems, g, 0).wait_recv()
        for t, m in enumerate(CHIP_RELATIONS):
            _gather2_copy(src, out, _peer(m | 1)[1], sib, send_sems, recv_sems, g, 4 + t).wait_recv()
    for g, (src, out) in enumerate(zip(srcs, outs)):
        _gather2_copy(src, out, me, sib, send_sems, recv_sems, g, 0).wait_send()
        for t, m in enumerate(CHIP_RELATIONS):
            peer, plin = _peer(m)
            _gather2_copy(src, out, me, peer, send_sems, recv_sems, g, 1 + t).wait_send()
            _gather2_copy(out.at[plin], out, plin, sib, send_sems, recv_sems, g, 4 + t).wait_send()
        pltpu.make_async_copy(src, out.at[me], local_sems.at[g]).wait()


def _gather_scratch(n):
    return [pltpu.SemaphoreType.DMA((n, N_DEV - 1)), pltpu.SemaphoreType.DMA((n, N_DEV - 1)), pltpu.SemaphoreType.DMA((n,))]


def _exchange_copies(srcs, dsts, send_sems, recv_sems):
    copies = []
    for m in range(1, N_DEV):
        peer, plin = _peer(m)
        for g, ((src, per_peer), dst) in enumerate(zip(srcs, dsts)):
            copies.append(pltpu.make_async_remote_copy(
                src_ref=src.at[plin] if per_peer else src, dst_ref=dst.at[m - 1], send_sem=send_sems.at[g, m - 1],
                recv_sem=recv_sems.at[g, m - 1], device_id=peer, device_id_type=MESH))
    return copies


def _exchange_start(*a):
    for cp in _exchange_copies(*a):
        cp.start()


def _exchange_wait(*a):
    copies = _exchange_copies(*a)
    for cp in copies:
        cp.wait_recv()
    for cp in copies:
        cp.wait_send()


def _exchange_scratch(n):
    return [pltpu.SemaphoreType.DMA((n, N_DEV - 1)), pltpu.SemaphoreType.DMA((n, N_DEV - 1))]


MLA_SCALE = (MLA_NOPE + MLA_ROPE) ** -0.5
MLA_C2 = MLA_SCALE * math.log2(math.e)
ONE_LANE = MLA_V
NEG = -1e30
HPS = 4
HPB = 4


def _tri_mask(n, lower_rows_ge_cols=True):
    r = lax.broadcasted_iota(jnp.int32, (n, n), 0)
    c = lax.broadcasted_iota(jnp.int32, (n, n), 1)
    return r >= c if lower_rows_ge_cols else c >= r


def mla_fwd(q, k, v, gather, *, tq):
    T = q.shape[0]
    nq = T // tq
    rep = tq // HP
    ng = len(gather)

    def body(*refs):
        q_ref, k_ref, v_ref = refs[:3]
        srcs = refs[3:3 + ng]
        o_ref, lse_ref = refs[3 + ng:5 + ng]
        outs = refs[5 + ng:5 + 2 * ng]
        m_sc, acc_sc = refs[5 + 2 * ng:7 + 2 * ng]
        comm = (srcs, outs) + tuple(refs[7 + 2 * ng:])
        h, i = pl.program_id(0), pl.program_id(1)

        @pl.when((h == 0) & (i == 0))
        def _():
            _gather_start(*comm)

        m_sc[...] = jnp.full(m_sc.shape, NEG, F32)
        acc_sc[...] = jnp.zeros(acc_sc.shape, F32)
        heads = [slice(t * HP, (t + 1) * HP) for t in range(HPS)]

        def block(j, masked):
            off = pl.multiple_of(j * tq, tq)
            ss = [_dot_nt(q_ref[:, hd], k_ref[pl.ds(off, tq), hd]) * MLA_C2 for hd in heads]
            mask = _tri_mask(tq) if masked else None
            for hd, s in zip(heads, ss):
                if masked:
                    s = jnp.where(mask, s, NEG)
                m_prev = m_sc[:, hd]
                m_next = jnp.maximum(m_prev, jnp.max(s, axis=-1, keepdims=True))
                p = jnp.exp2(s - jnp.tile(m_next, (1, rep)))
                alpha = jnp.exp2(m_prev - m_next)
                acc_sc[:, hd] = alpha * acc_sc[:, hd] + _dot(p.astype(BF16), v_ref[pl.ds(off, tq), hd])
                m_sc[:, hd] = m_next

        def loop_body(j, carry):
            block(j, False)
            return carry

        lax.fori_loop(0, i, loop_body, 0)
        block(i, True)
        lane = lax.broadcasted_iota(jnp.int32, (tq, HP), 1)
        lse_cols = jnp.zeros((tq, HP), F32)
        for t, hd in enumerate(heads):
            acc = acc_sc[:, hd]
            l = acc[:, ONE_LANE:ONE_LANE + 1]
            o_ref[:, hd] = (acc / l).astype(o_ref.dtype)
            lse_cols = jnp.where(lane == t, m_sc[:, hd] + jnp.log(l) * math.log2(math.e), lse_cols)
        lse_rows = lse_cols.T
        for t in range(HPS):
            lse_ref[t] = lse_rows[t:t + 1, :]

        @pl.when((h == MLA_H // HPS - 1) & (i == nq - 1))
        def _():
            _gather_wait(*comm)

    blk = pl.BlockSpec((tq, HPS * HP), lambda h, i: (i, h))
    full = pl.BlockSpec((T, HPS * HP), lambda h, i: (0, h))
    return pl.pallas_call(
        body,
        name="mla_fwd",
        grid=(MLA_H // HPS, nq),
        in_specs=[blk, full, full] + [ANY] * ng,
        out_specs=[blk, pl.BlockSpec((HPS, None, 1, tq), lambda h, i: (h, i, 0, 0))] + [ANY] * ng,
        out_shape=[_sds((T, MLA_H * HP), BF16), _sds((MLA_H, nq, 1, tq), F32)]
        + [_sds((N_DEV,) + p.shape, p.dtype) for p in gather],
        scratch_shapes=[pltpu.VMEM((tq, HPS * HP), F32), pltpu.VMEM((tq, HPS * HP), F32)] + _gather_scratch(ng),
        compiler_params=_cp(2),
    )(q, k, v, *gather)


def mla_prep(delta, *, tq):
    T = delta.shape[0]
    nq = T // tq

    def body(d_ref, row_ref):
        dt = d_ref[...].T
        for h in range(MLA_H):
            row_ref[h] = dt[h:h + 1, :]

    return pl.pallas_call(
        body,
        name="mla_prep",
        grid=(nq,),
        in_specs=[pl.BlockSpec((tq, HP), lambda i: (i, 0))],
        out_specs=pl.BlockSpec((MLA_H, None, 1, tq), lambda i: (0, i, 0, 0)),
        out_shape=_sds((MLA_H, nq, 1, tq), F32),
        compiler_params=_cp(1),
    )(delta)


def mla_bwd(q, k, v, do, lse_row, delta_row, slots, *, tq):
    T = q.shape[0]
    nq = T // tq
    ns = len(slots)

    def body(*refs):
        q_ref, k_ref, v_ref, do_ref, lse_ref, delta_ref = refs[:6]
        srcs = [(r, True) for r in refs[6:6 + ns]]
        dq_ref, dk_ref, dv_ref = refs[6 + ns:9 + ns]
        dsts = refs[9 + ns:9 + 2 * ns]
        dk_sc, dv_sc = refs[9 + 2 * ns:11 + 2 * ns]
        comm = (srcs, dsts) + tuple(refs[11 + 2 * ns:])
        h, j = pl.program_id(0), pl.program_id(1)

        @pl.when((h == 0) & (j == 0))
        def _():
            _exchange_start(*comm)

        dk_sc[...] = jnp.zeros(dk_sc.shape, F32)
        dv_sc[...] = jnp.zeros(dv_sc.shape, F32)
        heads = [slice(t * HP, (t + 1) * HP) for t in range(HPB)]

        @pl.when(j == 0)
        def _():
            dq_ref[...] = jnp.zeros(dq_ref.shape, F32)

        def block(i, masked):
            off = pl.multiple_of(i * tq, tq)
            sts = [_dot_nt(k_ref[:, hd], q_ref[pl.ds(off, tq), hd]) * MLA_C2 for hd in heads]
            dpts = [_dot_nt(v_ref[:, hd], do_ref[pl.ds(off, tq), hd]) for hd in heads]
            mask = _tri_mask(tq, False) if masked else None
            for t, hd in enumerate(heads):
                st = sts[t]
                if masked:
                    st = jnp.where(mask, st, NEG)
                pt = jnp.exp2(st - lse_ref[t, i])
                dv_sc[:, hd] += _dot(pt.astype(BF16), do_ref[pl.ds(off, tq), hd])
                dst = (pt * (dpts[t] - delta_ref[t, i]) * MLA_SCALE).astype(BF16)
                dk_sc[:, hd] += _dot(dst, q_ref[pl.ds(off, tq), hd])
                dq_ref[pl.ds(off, tq), hd] += _dot_tn(dst, k_ref[:, hd])

        block(j, True)

        def loop_body(i, carry):
            block(i, False)
            return carry

        lax.fori_loop(j + 1, nq, loop_body, 0)
        dk_ref[...] = dk_sc[...].astype(dk_ref.dtype)
        dv_ref[...] = dv_sc[...].astype(dv_ref.dtype)

        @pl.when((h == MLA_H // HPB - 1) & (j == nq - 1))
        def _():
            _exchange_wait(*comm)

    blk = pl.BlockSpec((tq, HPB * HP), lambda h, j: (j, h))
    full = pl.BlockSpec((T, HPB * HP), lambda h, j: (0, h), pipeline_mode=pl.Buffered(1))
    rows = pl.BlockSpec((HPB, nq, 1, tq), lambda h, j: (h, 0, 0, 0))
    return pl.pallas_call(
        body,
        name="mla_bwd",
        grid=(MLA_H // HPB, nq),
        in_specs=[full, blk, blk, full, rows, rows] + [ANY] * ns,
        out_specs=[full, blk, blk] + [ANY] * ns,
        out_shape=[_sds((T, MLA_H * HP), F32), _sds((T, MLA_H * HP), BF16), _sds((T, MLA_H * HP), BF16)]
        + [_sds((N_DEV - 1,) + s.shape[1:], s.dtype) for s in slots],
        scratch_shapes=[pltpu.VMEM((tq, HPB * HP), F32), pltpu.VMEM((tq, HPB * HP), F32)] + _exchange_scratch(ns),
        compiler_params=_cp(2),
    )(q, k, v, do, lse_row, delta_row, *slots)


def _ret_consts():
    h = jnp.arange(RET_H, dtype=F32)
    log_g = jnp.log1p(-jnp.exp2(-5.0 - h))
    idx = jnp.arange(RET_C, dtype=F32)
    rel = idx[:, None] - idx[None, :]
    dmask = jnp.where(rel >= 0, jnp.exp(log_g[:, None, None] * jnp.maximum(rel, 0.0)), 0.0)
    zeta = jnp.exp(log_g[:, None] * (RET_C - 1.0 - idx)[None, :])
    xi = jnp.exp(log_g[:, None] * (idx + 1.0)[None, :])
    decay = jnp.exp(log_g * RET_C)
    zb = jnp.broadcast_to(zeta[:, :, None], (RET_H, RET_C, RET_D))
    xb = jnp.broadcast_to(xi[:, :, None], (RET_H, RET_C, RET_D))
    db = jnp.broadcast_to(decay[:, None, None], (RET_H, RET_C, RET_D))
    return dmask.astype(F32), zb.astype(F32), xb.astype(F32), db.astype(F32)


def ret_fwd(rq, rk, rv, consts, *, rb):
    T = rq.shape[0]
    nb = T // rb
    ncb = rb // RET_C

    def body(q_ref, k_ref, v_ref, dm_ref, z_ref, x_ref, dc_ref, o_ref, st_ref, r_sc):
        @pl.when(pl.program_id(0) == 0)
        def _():
            r_sc[...] = jnp.zeros(r_sc.shape, F32)

        for c in range(ncb):
            sl = slice(c * RET_C, (c + 1) * RET_C)
            for h in range(RET_H):
                hd = slice(h * RET_D, (h + 1) * RET_D)
                q, k, v = q_ref[sl, hd], k_ref[sl, hd], v_ref[sl, hd]
                r = r_sc[h]
                rbf = r.astype(BF16)
                st_ref[sl, hd] = rbf
                s = _dot_nt(q, k) * dm_ref[h]
                inner = _dot(s.astype(BF16), v)
                cross = _dot((q.astype(F32) * x_ref[h]).astype(BF16), rbf)
                o_ref[sl, hd] = inner + cross
                kz = (k.astype(F32) * z_ref[h]).T.astype(BF16)
                r_sc[h] = r * dc_ref[h] + _dot(kz, v)

    blk = pl.BlockSpec((rb, RET_H * RET_D), lambda b: (b, 0))
    cst = pl.BlockSpec((RET_H, RET_C, RET_D), lambda b: (0, 0, 0))
    return pl.pallas_call(
        body,
        name="ret_fwd",
        grid=(nb,),
        in_specs=[blk, blk, blk, cst, cst, cst, cst],
        out_specs=[blk, blk],
        out_shape=[_sds((T, RET_H * RET_D), F32), _sds((T, RET_H * RET_D), BF16)],
        scratch_shapes=[pltpu.VMEM((RET_H, RET_D, RET_D), F32)],
        compiler_params=_cp(1),
    )(rq, rk, rv, *consts)


def ret_bwd(rq, rk, rv, st, dret, consts, *, rb):
    T = rq.shape[0]
    nb = T // rb
    ncb = rb // RET_C

    def body(q_ref, k_ref, v_ref, st_ref, do_ref, dm_ref, z_ref, x_ref, dc_ref, dq_ref, dk_ref, dv_ref, g_sc):
        @pl.when(pl.program_id(0) == 0)
        def _():
            g_sc[...] = jnp.zeros(g_sc.shape, F32)

        for c in reversed(range(ncb)):
            sl = slice(c * RET_C, (c + 1) * RET_C)
            for h in range(RET_H):
                hd = slice(h * RET_D, (h + 1) * RET_D)
                dm, zt, xi = dm_ref[h], z_ref[h], x_ref[h]
                q, k, v, rp = q_ref[sl, hd], k_ref[sl, hd], v_ref[sl, hd], st_ref[sl, hd]
                dob = do_ref[sl, hd].astype(BF16)
                qf, kf = q.astype(F32), k.astype(F32)
                gn = g_sc[h]
                gnb = gn.astype(BF16)
                s = _dot_nt(q, k) * dm
                ds = _dot_nt(dob, v) * dm
                dq = _dot(ds.astype(BF16), k) + _dot_nt(dob, rp) * xi
                dk = _dot(ds.T.astype(BF16), q) + _dot_nt(v, gnb) * zt
                dv = _dot(s.T.astype(BF16), dob) + _dot((kf * zt).astype(BF16), gnb)
                dq_ref[sl, hd] = dq.astype(dq_ref.dtype)
                dk_ref[sl, hd] = dk.astype(dk_ref.dtype)
                dv_ref[sl, hd] = dv.astype(dv_ref.dtype)
                g_sc[h] = _dot((qf * xi).T.astype(BF16), dob) + dc_ref[h] * gn

    blk = pl.BlockSpec((rb, RET_H * RET_D), lambda b: (nb - 1 - b, 0))
    cst = pl.BlockSpec((RET_H, RET_C, RET_D), lambda b: (0, 0, 0))
    return pl.pallas_call(
        body,
        name="ret_bwd",
        grid=(nb,),
        in_specs=[blk, blk, blk, blk, blk, cst, cst, cst, cst],
        out_specs=[blk, blk, blk],
        out_shape=[_sds((T, RET_H * RET_D), F32), _sds((T, RET_H * RET_D), F32), _sds((T, RET_H * RET_D), BF16)],
        scratch_shapes=[pltpu.VMEM((RET_H, RET_D, RET_D), F32)],
        compiler_params=_cp(1),
    )(rq, rk, rv, st, dret, *consts)


HALO = 16
EDGE = 8


def conv_act_fwd(up_pre, w_conv, b_conv, *, tile, cw):
    T = up_pre.shape[0]
    nt = T // tile
    ncol = D_FF // cw
    hb = tile // HALO

    def body(pa_ref, a_ref, pb_ref, b_ref, wa_ref, wb_ref, ba_ref, bb_ref, o_ref):
        i = pl.program_id(1)
        keep = (i > 0).astype(F32)

        def conv(prev_ref, cur_ref, w_ref, bias_ref):
            ext = jnp.concatenate([prev_ref[...].astype(F32)[HALO - EDGE:, :] * keep, cur_ref[...].astype(F32)], axis=0)
            w = w_ref[...]
            y = ext * w[2:3, :] + pltpu.roll(ext, 1, 0) * w[1:2, :] + pltpu.roll(ext, 2, 0) * w[0:1, :] + bias_ref[...]
            return y[EDGE:, :]

        a = conv(pa_ref, a_ref, wa_ref, ba_ref)
        b = conv(pb_ref, b_ref, wb_ref, bb_ref)
        o_ref[...] = (a * _sigmoid(a) * b).astype(o_ref.dtype)

    prev_a = pl.BlockSpec((HALO, cw), lambda j, i: (jnp.maximum(i * hb - 1, 0), j))
    cur_a = pl.BlockSpec((tile, cw), lambda j, i: (i, j))
    prev_b = pl.BlockSpec((HALO, cw), lambda j, i: (jnp.maximum(i * hb - 1, 0), j + ncol))
    cur_b = pl.BlockSpec((tile, cw), lambda j, i: (i, j + ncol))
    w_a = pl.BlockSpec((3, cw), lambda j, i: (0, j))
    w_b = pl.BlockSpec((3, cw), lambda j, i: (0, j + ncol))
    bias_a = pl.BlockSpec((1, cw), lambda j, i: (0, j))
    bias_b = pl.BlockSpec((1, cw), lambda j, i: (0, j + ncol))
    return pl.pallas_call(
        body,
        name="conv_act_fwd",
        grid=(ncol, nt),
        in_specs=[prev_a, cur_a, prev_b, cur_b, w_a, w_b, bias_a, bias_b],
        out_specs=pl.BlockSpec((tile, cw), lambda j, i: (i, j)),
        out_shape=_sds((T, D_FF), BF16),
        compiler_params=_cp(2),
    )(up_pre, up_pre, up_pre, up_pre, w_conv, w_conv, b_conv, b_conv)


def conv_act_bwd(up_pre, dact, w_conv, b_conv, *, tile, cw):
    T = up_pre.shape[0]
    nt = T // tile
    ncol = D_FF // cw
    hb = tile // HALO
    ext_rows = tile + 2 * EDGE

    def body(pa_ref, a_ref, na_ref, pb_ref, b_ref, nb_ref, d_ref, nd_ref, wa_ref, wb_ref, ba_ref, bb_ref,
             dxa_ref, dxb_ref, sa_ref, sb_ref):
        i = pl.program_id(1)
        keep_p = (i > 0).astype(F32)
        keep_n = (i < nt - 1).astype(F32)

        def ext_of(prev_ref, cur_ref, next_ref):
            return jnp.concatenate(
                [prev_ref[...].astype(F32)[HALO - EDGE:, :] * keep_p, cur_ref[...].astype(F32),
                 next_ref[...].astype(F32)[:EDGE, :] * keep_n], axis=0)

        def taps(ext):
            return ext, pltpu.roll(ext, 1, 0), pltpu.roll(ext, 2, 0)

        def conv(tp, w, bias):
            return tp[0] * w[2:3, :] + tp[1] * w[1:2, :] + tp[2] * w[0:1, :] + bias

        xa = taps(ext_of(pa_ref, a_ref, na_ref))
        xb = taps(ext_of(pb_ref, b_ref, nb_ref))
        wa, wb = wa_ref[...], wb_ref[...]
        a = conv(xa, wa, ba_ref[...])
        b = conv(xb, wb, bb_ref[...])
        dy = jnp.concatenate(
            [jnp.zeros((EDGE, cw), F32), d_ref[...].astype(F32), nd_ref[...].astype(F32)[:EDGE, :] * keep_n], axis=0)
        sg = _sigmoid(a)
        da = dy * b * (sg * (1.0 + a * (1.0 - sg)))
        db = dy * (a * sg)

        def back(dup, tp, w, dx_ref, s_ref):
            dx = dup * w[2:3, :] + pltpu.roll(dup, ext_rows - 1, 0) * w[1:2, :] + pltpu.roll(dup, ext_rows - 2, 0) * w[0:1, :]
            dx_ref[...] = dx[EDGE:EDGE + tile, :].astype(dx_ref.dtype)
            dc = dup[EDGE:EDGE + tile, :]
            r2 = jnp.sum(dc * tp[0][EDGE:EDGE + tile, :], axis=0, keepdims=True)
            r1 = jnp.sum(dc * tp[1][EDGE:EDGE + tile, :], axis=0, keepdims=True)
            r0 = jnp.sum(dc * tp[2][EDGE:EDGE + tile, :], axis=0, keepdims=True)
            rb = jnp.sum(dc, axis=0, keepdims=True)
            row = lax.broadcasted_iota(jnp.int32, (8, cw), 0)
            upd = (jnp.where(row == 0, r0, 0.0) + jnp.where(row == 1, r1, 0.0) + jnp.where(row == 2, r2, 0.0)
                   + jnp.where(row == 3, rb, 0.0))

            @pl.when(i == 0)
            def _():
                s_ref[...] = upd

            @pl.when(i > 0)
            def _():
                s_ref[...] += upd

        back(da, xa, wa, dxa_ref, sa_ref)
        back(db, xb, wb, dxb_ref, sb_ref)

    def prev_of(shift):
        return pl.BlockSpec((HALO, cw), lambda j, i: (jnp.maximum(i * hb - 1, 0), j + shift))

    def next_of(shift):
        return pl.BlockSpec((HALO, cw), lambda j, i: (jnp.minimum((i + 1) * hb, nt * hb - 1), j + shift))

    def cur_of(shift):
        return pl.BlockSpec((tile, cw), lambda j, i: (i, j + shift))

    def row_of(rows, shift):
        return pl.BlockSpec((rows, cw), lambda j, i: (0, j + shift))

    return pl.pallas_call(
        body,
        name="conv_act_bwd",
        grid=(ncol, nt),
        in_specs=[prev_of(0), cur_of(0), next_of(0), prev_of(ncol), cur_of(ncol), next_of(ncol), cur_of(0), next_of(0),
                  row_of(3, 0), row_of(3, ncol), row_of(1, 0), row_of(1, ncol)],
        out_specs=[cur_of(0), cur_of(0), row_of(8, 0), row_of(8, 0)],
        out_shape=[_sds((T, D_FF), BF16), _sds((T, D_FF), BF16), _sds((8, D_FF), F32), _sds((8, D_FF), F32)],
        compiler_params=_cp(2),
    )(up_pre, up_pre, up_pre, up_pre, up_pre, up_pre, dact, dact, w_conv, w_conv, b_conv, b_conv)


NS_IN, NS_UP, NS_XKV, NS_UQ = 564, 704, 256, 96
E2_ROWS, E2_UQ = 384, 128
L1_W = NS_UP + NS_XKV
L2_ROWS = 736
L2_OUT, L2_XQ, L2_XO, L2_DN = 0, 128, 256, 384
RS_DN = 352
L3_ROWS, L3_PRET = 1024, 512
L4_SHAPE = (8, 768)
WZ_RUNS = ((416, 4096, 0), (0, 384, 4096), (384, 32, 4544))
WZ_ZERO = ((4480, 4544), (4576, 4608))


def _pieces(orig_start, length, dst_start, ns):
    out, c, d, end = [], orig_start, dst_start, orig_start + length
    while c < end:
        j, off = c // ns, c % ns
        ln = min(ns - off, end - c)
        out.append((j, off, d, ln))
        c += ln
        d += ln
    return out


def pack_early(w_in, w_ukv, w_uq):
    def body(in_ref, ukv_ref, uq_ref, e1_ref, e2_ref):
        e1_ref[...] = in_ref[0].astype(BF16)
        e2_ref[0:E2_UQ, :] = ukv_ref[...].astype(BF16)
        e2_ref[E2_UQ:E2_ROWS, 0:NS_UQ] = uq_ref[0].astype(BF16)
        e2_ref[E2_UQ:E2_ROWS, NS_UQ:HP] = jnp.zeros((E2_ROWS - E2_UQ, HP - NS_UQ), BF16)

    return pl.pallas_call(
        body,
        name="pack_early",
        in_specs=[VM] * 3,
        out_specs=[VM] * 2,
        out_shape=[_sds((D, NS_IN), BF16), _sds((E2_ROWS, HP), BF16)],
        compiler_params=_cp0(),
    )(w_in, w_ukv, w_uq)


def pack_late(w_up, w_xkv, w_out, w_xq, w_xo, w_down, w_pmla, w_pret, w_conv):
    def body(up_ref, xkv_ref, o_ref, xq_ref, xo_ref, dn_ref, pm_ref, pr_ref, cv_ref, l1_ref, l2_ref, l3_ref, l4_ref):
        l1_ref[:, 0:NS_UP] = up_ref[0].astype(BF16)
        l1_ref[:, NS_UP:L1_W] = xkv_ref[...].astype(BF16)
        l2_ref[L2_OUT:L2_XQ, :] = o_ref[...].astype(BF16)
        l2_ref[L2_XQ:L2_XO, :] = xq_ref[...].astype(BF16)
        l2_ref[L2_XO:L2_DN, :] = xo_ref[...].astype(BF16)
        l2_ref[L2_DN:L2_ROWS, :] = dn_ref[...].astype(BF16)
        l3_ref[0:L3_PRET, :] = pm_ref[...].astype(BF16)
        l3_ref[L3_PRET:L3_ROWS, :] = pr_ref[...].astype(BF16)
        l4_ref[...] = jnp.zeros(L4_SHAPE, F32)
        l4_ref[0:3, 0:NS_UP] = cv_ref[0]

    return pl.pallas_call(
        body,
        name="pack_late",
        in_specs=[VM] * 9,
        out_specs=[VM] * 4,
        out_shape=[_sds((D, L1_W), BF16), _sds((L2_ROWS, D), BF16), _sds((L3_ROWS, HP), BF16), _sds(L4_SHAPE, F32)],
        compiler_params=_cp0(),
    )(w_up, w_xkv, w_out, w_xq, w_xo, w_down, w_pmla, w_pret, w_conv)


def assemble_early(g1, g2, *, tile):
    def body(g1_ref, g2_ref, wz_ref, wk_ref, wv_ref, wq_ref):
        for lo, hi in WZ_ZERO:
            wz_ref[:, lo:hi] = jnp.zeros((tile, hi - lo), BF16)
        for os_, ln_, ds_ in WZ_RUNS:
            for j, off, d, ln in _pieces(os_, ln_, ds_, NS_IN):
                wz_ref[:, d:d + ln] = g1_ref[j, :, off:off + ln]

        @pl.when(pl.program_id(0) == 0)
        def _():
            half = jnp.zeros((MLA_KVR, HP - MLA_NOPE), BF16)
            for j in range(N_DEV):
                wk_ref[:, j * HP:j * HP + MLA_NOPE] = g2_ref[j, 0:E2_UQ, 0:MLA_NOPE]
                wk_ref[:, j * HP + MLA_NOPE:(j + 1) * HP] = half
                wv_ref[:, j * HP:j * HP + MLA_V] = g2_ref[j, 0:E2_UQ, MLA_NOPE:HP]
                wv_ref[:, j * HP + MLA_V:(j + 1) * HP] = half
                wq_ref[:, j * HP:j * HP + NS_UQ] = g2_ref[j, E2_UQ:E2_ROWS, 0:NS_UQ]
                wq_ref[:, j * HP + NS_UQ:(j + 1) * HP] = jnp.zeros((MLA_QR, HP - NS_UQ), BF16)

    def whole(r):
        return pl.BlockSpec((r, MLA_H * HP), lambda i: (0, 0))

    return pl.pallas_call(
        body,
        name="assemble_early",
        grid=(D // tile,),
        in_specs=[pl.BlockSpec((N_DEV, tile, NS_IN), lambda i: (0, i, 0)),
                  pl.BlockSpec((N_DEV, E2_ROWS, HP), lambda i: (0, 0, 0))],
        out_specs=[pl.BlockSpec((tile, ZW), lambda i: (i, 0)), whole(MLA_KVR), whole(MLA_KVR), whole(MLA_QR)],
        out_shape=[_sds((D, ZW), BF16), _sds((MLA_KVR, MLA_H * HP), BF16), _sds((MLA_KVR, MLA_H * HP), BF16),
                   _sds((MLA_QR, MLA_H * HP), BF16)],
        compiler_params=_cp(1),
    )(g1, g2)


def assemble_l1(g1, *, tile):
    def body(g_ref, wup_ref, wxkv_ref):
        for j in range(N_DEV):
            wup_ref[:, j * NS_UP:(j + 1) * NS_UP] = g_ref[j, :, 0:NS_UP]
            wxkv_ref[:, j * NS_XKV:(j + 1) * NS_XKV] = g_ref[j, :, NS_UP:L1_W]

    return pl.pallas_call(
        body,
        name="assemble_l1",
        grid=(D // tile,),
        in_specs=[pl.BlockSpec((N_DEV, tile, L1_W), lambda i: (0, i, 0))],
        out_specs=[pl.BlockSpec((tile, 2 * D_FF), lambda i: (i, 0)), pl.BlockSpec((tile, 2 * D), lambda i: (i, 0))],
        out_shape=[_sds((D, 2 * D_FF), BF16), _sds((D, 2 * D), BF16)],
        compiler_params=_cp(1),
    )(g1)


def assemble_l234(g2, g3, g4):
    def body(g2_ref, g3_ref, g4_ref, wo_ref, wxq_ref, wxo_ref, wdn_ref, wpa_ref, wpr_ref, wc_ref):
        for j in range(N_DEV):
            wo_ref[j * 128:(j + 1) * 128, :] = g2_ref[j, L2_OUT:L2_XQ, :]
            wxq_ref[j * 128:(j + 1) * 128, :] = g2_ref[j, L2_XQ:L2_XO, :]
            wxo_ref[j * 128:(j + 1) * 128, :] = g2_ref[j, L2_XO:L2_DN, :]
            wdn_ref[j * RS_DN:(j + 1) * RS_DN, :] = g2_ref[j, L2_DN:L2_ROWS, :]
            wpr_ref[:, j * 128:(j + 1) * 128] = g3_ref[j, L3_PRET:L3_ROWS, :]
            wc_ref[:, j * NS_UP:(j + 1) * NS_UP] = g4_ref[j, 0:3, 0:NS_UP]
            for h in range(MLA_H):
                wpa_ref[h * HP:h * HP + MLA_V, j * 128:(j + 1) * 128] = g3_ref[j, h * MLA_V:(h + 1) * MLA_V, :]
        for h in range(MLA_H):
            wpa_ref[h * HP + MLA_V:(h + 1) * HP, :] = jnp.zeros((HP - MLA_V, D), BF16)

    return pl.pallas_call(
        body,
        name="assemble_l234",
        in_specs=[VM] * 3,
        out_specs=[VM] * 7,
        out_shape=[_sds((D, D), BF16), _sds((D, D), BF16), _sds((D, D), BF16), _sds((D_FF, D), BF16),
                   _sds((MLA_H * HP, D), BF16), _sds((RET_H * RET_D, D), BF16), _sds((3, 2 * D_FF), F32)],
        compiler_params=_cp0(),
    )(g2, g3, g4)


def slots_l1(dwup_a, dwup_b, dwxkv, *, tile):
    per_half = D_FF // NS_UP

    def body(ua_ref, ub_ref, x_ref, s_ref):
        for j in range(N_DEV):
            src = ua_ref if j < per_half else ub_ref
            c0 = (j % per_half) * NS_UP
            s_ref[j, :, 0:NS_UP] = src[:, c0:c0 + NS_UP].astype(BF16)
            s_ref[j, :, NS_UP:L1_W] = x_ref[:, j * NS_XKV:(j + 1) * NS_XKV].astype(BF16)

    return pl.pallas_call(
        body,
        name="slots_l1",
        grid=(D // tile,),
        in_specs=[pl.BlockSpec((tile, D_FF), lambda i: (i, 0)), pl.BlockSpec((tile, D_FF), lambda i: (i, 0)),
                  pl.BlockSpec((tile, 2 * D), lambda i: (i, 0))],
        out_specs=pl.BlockSpec((N_DEV, tile, L1_W), lambda i: (0, i, 0)),
        out_shape=_sds((N_DEV, D, L1_W), BF16),
        compiler_params=_cp(1),
    )(dwup_a, dwup_b, dwxkv)


def slots_l23(dwo, dwxq, dwxo, dwdn, dwpa, dwpr):
    def body(o_ref, xq_ref, xo_ref, dn_ref, pa_ref, pr_ref, l2_ref, l3_ref):
        l2_ref[L2_OUT:L2_XQ, :] = o_ref[...].astype(BF16)
        l2_ref[L2_XQ:L2_XO, :] = xq_ref[...].astype(BF16)
        l2_ref[L2_XO:L2_DN, :] = xo_ref[...].astype(BF16)
        l2_ref[L2_DN:L2_ROWS, :] = dn_ref[...].astype(BF16)
        for h in range(MLA_H):
            l3_ref[h * MLA_V:(h + 1) * MLA_V, :] = pa_ref[h * HP:h * HP + MLA_V, :].astype(BF16)
        l3_ref[L3_PRET:L3_ROWS, :] = pr_ref[...].astype(BF16)

    rows128 = pl.BlockSpec((128, D), lambda j: (j, 0))

    def cols(r):
        return pl.BlockSpec((r, 128), lambda j: (0, j))

    return pl.pallas_call(
        body,
        name="slots_l23",
        grid=(N_DEV,),
        in_specs=[rows128, rows128, rows128, pl.BlockSpec((RS_DN, D), lambda j: (j, 0)),
                  cols(MLA_H * HP), cols(RET_H * RET_D)],
        out_specs=[pl.BlockSpec((None, L2_ROWS, D), lambda j: (j, 0, 0)),
                   pl.BlockSpec((None, L3_ROWS, HP), lambda j: (j, 0, 0))],
        out_shape=[_sds((N_DEV, L2_ROWS, D), BF16), _sds((N_DEV, L3_ROWS, HP), BF16)],
        compiler_params=_cp(1),
    )(dwo, dwxq, dwxo, dwdn, dwpa, dwpr)


def slots_early(dwz, dwk, dwv, dwq, *, tile):
    def body(dz_ref, k_ref, v_ref, q_ref, s1_ref, s2_ref):
        for os_, ln_, ds_ in WZ_RUNS:
            for j, off, d, ln in _pieces(os_, ln_, ds_, NS_IN):
                s1_ref[j, :, off:off + ln] = dz_ref[:, d:d + ln].astype(BF16)

        @pl.when(pl.program_id(0) == 0)
        def _():
            for j in range(N_DEV):
                s2_ref[j, 0:E2_UQ, 0:MLA_NOPE] = k_ref[:, j * HP:j * HP + MLA_NOPE].astype(BF16)
                s2_ref[j, 0:E2_UQ, MLA_NOPE:HP] = v_ref[:, j * HP:j * HP + MLA_V].astype(BF16)
                s2_ref[j, E2_UQ:E2_ROWS, 0:NS_UQ] = q_ref[:, j * HP:j * HP + NS_UQ].astype(BF16)
                s2_ref[j, E2_UQ:E2_ROWS, NS_UQ:HP] = jnp.zeros((E2_ROWS - E2_UQ, HP - NS_UQ), BF16)

    def whole(r):
        return pl.BlockSpec((r, MLA_H * HP), lambda i: (0, 0))

    return pl.pallas_call(
        body,
        name="slots_early",
        grid=(D // tile,),
        in_specs=[pl.BlockSpec((tile, ZW), lambda i: (i, 0)), whole(MLA_KVR), whole(MLA_KVR), whole(MLA_QR)],
        out_specs=[pl.BlockSpec((N_DEV, tile, NS_IN), lambda i: (0, i, 0)),
                   pl.BlockSpec((N_DEV, E2_ROWS, HP), lambda i: (0, 0, 0))],
        out_shape=[_sds((N_DEV, D, NS_IN), BF16), _sds((N_DEV, E2_ROWS, HP), BF16)],
        compiler_params=_cp(1),
    )(dwz, dwk, dwv, dwq)


SMALL = (("g_mix", 1024, 0), ("b_gate", 2048, 1), ("g_q_lat", 256, 3), ("g_kv_lat", 128, 4), ("g_ret", 512, 5),
         ("g_cross", 1024, 6), ("g_mem", 1024, 7), ("g_ffn", 1024, 8), ("b_conv", 5632, 9), ("g_final", 1024, 15))
SMALL_DIRECT = tuple(s for s in SMALL if s[0] != "b_conv")
S_ROWS = 16
LOSS_ROW, LOSS_LANE = 4, 128


def _flat_pieces(n, row0):
    return [(row0 + k // D, k, min(D, n - k)) for k in range(0, n, D)]


def exchange_small(cs_a, cs_b, loss_part, smalls):
    ns = len(smalls)

    def body(*refs):
        ca_ref, cb_ref, loss_ref = refs[:3]
        small_refs = refs[3:3 + ns]
        rd_ref, rs_ref, dsl_ref, own_ref = refs[3 + ns:7 + ns]
        sems = refs[7 + ns:]
        own_ref[...] = jnp.zeros(own_ref.shape, F32)
        own_ref[LOSS_ROW:LOSS_ROW + 1, LOSS_LANE:LOSS_LANE + HP] = loss_ref[0:1, :]
        for (name, n, row0), g_ref in zip(SMALL_DIRECT, small_refs):
            for r, c0, ln in _flat_pieces(n, row0):
                own_ref[r:r + 1, 0:ln] = g_ref[:, c0:c0 + ln]
        row0 = dict((s[0], s[2]) for s in SMALL)["b_conv"]
        for half, c_ref in enumerate((ca_ref, cb_ref)):
            k = half * D_FF
            end = k + D_FF
            while k < end:
                r, lane = row0 + k // D, k % D
                ln = min(D - lane, end - k)
                own_ref[r:r + 1, lane:lane + ln] = c_ref[3:4, k - half * D_FF:k - half * D_FF + ln]
                k += ln
        dsl_ref[...] = jnp.zeros(dsl_ref.shape, F32)
        per_half = D_FF // NS_UP
        for j in range(N_DEV):
            c_ref = ca_ref if j < per_half else cb_ref
            c0 = (j % per_half) * NS_UP
            dsl_ref[j, 0:3, 0:NS_UP] = c_ref[0:3, c0:c0 + NS_UP]
        comm = ([(dsl_ref, True), (own_ref, False)], [rd_ref, rs_ref]) + tuple(sems)
        _exchange_start(*comm)
        _exchange_wait(*comm)

    n1 = N_DEV - 1
    return pl.pallas_call(
        body,
        name="exchange_small",
        in_specs=[VM] * (3 + ns),
        out_specs=[VM, VM, VM, VM],
        out_shape=[_sds((n1,) + L4_SHAPE, F32), _sds((n1, S_ROWS, D), F32), _sds((N_DEV,) + L4_SHAPE, F32),
                   _sds((S_ROWS, D), F32)],
        scratch_shapes=_exchange_scratch(2),
        compiler_params=_cp0(),
    )(cs_a, cs_b, loss_part, *smalls)


def _adamw(w, g, m, v):
    m = ADAM_B1 * m + (1.0 - ADAM_B1) * g
    v = ADAM_B2 * v + (1.0 - ADAM_B2) * (g * g)
    m_hat = m / (1.0 - ADAM_B1 ** ADAM_STEP)
    v_hat = v / (1.0 - ADAM_B2 ** ADAM_STEP)
    delta = -ADAM_LR * (m_hat / (jnp.sqrt(v_hat) + ADAM_EPS) + ADAM_WD * w)
    return delta, m, v


def _apply(g, refs, outs):
    at = (0,) if len(refs[0].shape) == 3 else (Ellipsis,)
    d, mn, vn = _adamw(refs[0][at], g, refs[1][at], refs[2][at])
    for o_ref, val in zip(outs, (g, d, mn, vn)):
        o_ref[at] = val


def adam_cols(own, recv, wmv, spans, *, name, tile):
    R, W = own.shape
    nw = len(spans)

    def body(*refs):
        own_ref, recv_ref = refs[:2]
        ins, outs = refs[2:2 + 3 * nw], refs[2 + 3 * nw:]
        g = own_ref[...].astype(F32)
        for k in range(N_DEV - 1):
            g = g + recv_ref[k].astype(F32)
        for t, (lo, hi) in enumerate(spans):
            _apply(g[:, lo:hi], ins[3 * t:3 * t + 3], outs[4 * t:4 * t + 4])

    def blk(w, lead=False):
        if lead:
            return pl.BlockSpec((None, tile, w), lambda i: (0, i, 0))
        return pl.BlockSpec((tile, w), lambda i: (i, 0))

    widths = [hi - lo for lo, hi in spans]
    leads = [wmv[3 * t].ndim == 3 for t in range(nw)]
    return pl.pallas_call(
        body,
        name=name,
        grid=(R // tile,),
        in_specs=[blk(W), pl.BlockSpec((N_DEV - 1, tile, W), lambda i: (0, i, 0))]
        + [blk(w, ld) for w, ld in zip(widths, leads) for _ in range(3)],
        out_specs=[blk(w, ld) for w, ld in zip(widths, leads) for _ in range(4)],
        out_shape=[_sds(((1,) if ld else ()) + (R, w), F32) for w, ld in zip(widths, leads) for _ in range(4)],
        compiler_params=_cp(1),
    )(own, recv, *wmv)


def adam_rows(own, recv, wmv, spans, *, name):
    nw = len(spans)

    def body(*refs):
        own_ref, recv_ref = refs[:2]
        ins, outs = refs[2:2 + 3 * nw], refs[2 + 3 * nw:]
        for t, (lo, hi, w) in enumerate(spans):
            g = own_ref[lo:hi, :].astype(F32)
            for k in range(N_DEV - 1):
                g = g + recv_ref[k, lo:hi, :].astype(F32)
            _apply(g[:, 0:w], ins[3 * t:3 * t + 3], outs[4 * t:4 * t + 4])

    return pl.pallas_call(
        body,
        name=name,
        in_specs=[VM] * (2 + 3 * nw),
        out_specs=[VM] * (4 * nw),
        out_shape=[_sds(wmv[3 * t].shape, F32) for t in range(nw) for _ in range(4)],
        compiler_params=_cp0(),
    )(own, recv, *wmv)


def adam_small(own_s, recv_s, dslots, recv_d, wmv_small, wmv_conv):
    ns = len(SMALL)

    def body(*refs):
        own_ref, rs_ref, dsl_ref, rd_ref = refs[:4]
        ins = refs[4:4 + 3 * ns + 3]
        outs = refs[4 + 3 * ns + 3:4 + 3 * ns + 3 + 4 * ns + 4]
        loss_ref, all_sc = refs[-2], refs[-1]
        me = _my_id()
        all_sc[0] = own_ref[...]
        for k in range(N_DEV - 1):
            all_sc[k + 1] = rs_ref[k]
        g = all_sc[jnp.bitwise_xor(me, 0)]
        for s in range(1, N_DEV):
            g = g + all_sc[jnp.bitwise_xor(me, s)]
        all_sc[0] = g
        loss_ref[...] = all_sc[0, LOSS_ROW:LOSS_ROW + 1, LOSS_LANE:LOSS_LANE + HP]
        for t, (name, n, row0) in enumerate(SMALL):
            pieces = [all_sc[0, r:r + 1, 0:ln] for r, _, ln in _flat_pieces(n, row0)]
            gt = pieces[0] if len(pieces) == 1 else jnp.concatenate(pieces, axis=1)
            _apply(gt, ins[3 * t:3 * t + 3], outs[4 * t:4 * t + 4])
        gc = dsl_ref[me]
        for k in range(N_DEV - 1):
            gc = gc + rd_ref[k]
        _apply(gc[0:3, 0:NS_UP], ins[3 * ns:3 * ns + 3], outs[4 * ns:4 * ns + 4])

    out_shape = ([_sds((1, n), F32) for _, n, _ in SMALL for _ in range(4)] + [_sds(wmv_conv[0].shape, F32)] * 4
                 + [_sds((1, HP), F32)])
    return pl.pallas_call(
        body,
        name="adam_small",
        in_specs=[VM] * (4 + 3 * ns + 3),
        out_specs=[VM] * len(out_shape),
        out_shape=out_shape,
        scratch_shapes=[pltpu.VMEM((N_DEV, S_ROWS, D), F32)],
    )(own_s, recv_s, dslots, recv_d, *wmv_small, *wmv_conv)


def kernel(x, mem, positions, g_mix, w_in, b_gate, g_q_lat, w_uq, g_kv_lat, w_ukv, w_proj_mla, g_ret, w_proj_ret, w_out, g_cross, g_mem, w_xq, w_xkv, w_xo, g_ffn, w_up, w_conv, b_conv, w_down, g_final, loss_target, m_g_mix, m_w_in, m_b_gate, m_g_q_lat, m_w_uq, m_g_kv_lat, m_w_ukv, m_w_proj_mla, m_g_ret, m_w_proj_ret, m_w_out, m_g_cross, m_g_mem, m_w_xq, m_w_xkv, m_w_xo, m_g_ffn, m_w_up, m_w_conv, m_b_conv, m_w_down, m_g_final, v_g_mix, v_w_in, v_b_gate, v_g_q_lat, v_w_uq, v_g_kv_lat, v_w_ukv, v_w_proj_mla, v_g_ret, v_w_proj_ret, v_w_out, v_g_cross, v_g_mem, v_w_xq, v_w_xkv, v_w_xo, v_g_ffn, v_w_up, v_w_conv, v_b_conv, v_w_down, v_g_final):
    args = dict(locals())
    T = x.shape[1]
    M = mem.shape[1]
    tile = min(256, T)
    tile2 = min(512, T)
    tile4 = min(1024, T)
    tq = min(512, T)
    rb = min(1024, T)

    xs = x[0]
    tgt = loss_target[0]
    mems = mem[0]

    def shard(name, prefix=""):
        a = args[prefix + name]
        if name in ("w_in", "w_up", "w_uq", "w_conv"):
            return a
        return a.reshape(a.shape[-2:]) if a.ndim >= 2 else a.reshape(1, -1)

    e1, e2 = pack_early(shard("w_in"), shard("w_ukv"), shard("w_uq"))
    late_parts = pack_late(shard("w_up"), shard("w_xkv"), shard("w_out"), shard("w_xq"), shard("w_xo"), shard("w_down"),
                           shard("w_proj_mla"), shard("w_proj_ret"), shard("w_conv"))

    pos_f = jnp.broadcast_to(positions[0].astype(F32)[:, None], (T, 128))
    inv_m, inv_r = _rot_inv()
    u, cm, s1, s2, cr, sr, ge1, ge2 = rowwise(
        lambda xv, p, im, ir, g: (_rms(xv, g),) + _rot_tables(p, im, ir), [(xs, None), (pos_f, None)],
        [inv_m, inv_r, g_mix], [(D, BF16)] + [(128, F32)] * 5, [], tile=tile2, name="norm_mix_tables", gather=[e1, e2])
    wz, wk, wv, wq = assemble_early(ge1, ge2, tile=tile)
    rconsts = _ret_consts()

    z = matmul(u, wz, name="mm_z", tm=2048, tn=1536)

    def mixers_in(zl, zr, cmv, s1v, s2v, crv, srv, gq, gkv, wqv, wkv_, wvv):
        cq = _rms(zl[:, 0:256], gq).astype(BF16)
        ckv = _rms(zl[:, 256:384], gkv).astype(BF16)
        qv = _rot_mla(_dot(cq, wqv), cmv, s1v, s2v)
        kr = _rot_mla(zl[:, 384:512], cmv, s1v, s2v)
        kn = _dot(ckv, wkv_)
        kv_ = jnp.concatenate([kn[:, h * HP:(h + 1) * HP] + kr for h in range(MLA_H)], axis=1)
        vv = _dot(ckv, wvv)
        lane = lax.broadcasted_iota(jnp.int32, vv.shape, 1)
        vv = jnp.where((lane & (HP - 1)) == ONE_LANE, 1.0, vv)
        rqv = _rot_ret(zr[:, 0:512], crv, srv)
        rkv = _rot_ret(zr[:, 512:1024], crv, srv) * (RET_D ** -0.5)
        return qv, kv_, vv, rqv, rkv, zr[:, 1024:1536]

    q_a, k_a, v_a, rq, rk, rv = rowwise(
        mixers_in, [(z, (512, 8)), (z, (2048, 0)), (cm, None), (s1, None), (s2, None), (cr, None), (sr, None)],
        [g_q_lat, g_kv_lat, wq, wk, wv], [(MLA_H * HP, BF16)] * 3 + [(512, BF16)] * 3, [], tile=tile2, name="mixers_in")

    o_a, lse_row, gl1, gl2, gl3, gl4 = mla_fwd(q_a, k_a, v_a, list(late_parts), tq=tq)
    wup, wxkv = assemble_l1(gl1, tile=tile)
    wo, wxq, wxo, wdn, wpa, wpr, wcv = assemble_l234(gl2, gl3, gl4)
    ret, rstate = ret_fwd(rq, rk, rv, rconsts, rb=rb)

    def gn_parts(r):
        outs = []
        for h in range(RET_H):
            rh = r[:, h * RET_D:(h + 1) * RET_D]
            mu = jnp.mean(rh, axis=-1, keepdims=True)
            dlt = rh - mu
            rstd = lax.rsqrt(jnp.mean(dlt * dlt, axis=-1, keepdims=True) + EPS)
            outs.append((dlt * rstd, rstd))
        return outs

    def mix_fwd(ov, rv_, rg, gt, wpav, wprv, gr, bg):
        ya = _dot(ov, wpav)
        xh = jnp.concatenate([p[0] for p in gn_parts(rv_)], axis=1)
        t = rg * _sigmoid(rg) * (xh * gr)
        yr = _dot(t.astype(BF16), wprv)
        ga_ = _sigmoid(gt[:, :D] + bg[:, :D])
        gr_ = _sigmoid(gt[:, D:] + bg[:, D:])
        return ga_ * ya + gr_ * yr

    mix = rowwise(mix_fwd, [(o_a, None), (ret, None), (z, (512, 3)), (z, (2048, 1))], [wpa, wpr, g_ret, b_gate],
                  [(D, BF16)], [], tile=tile2, name="mix_fwd")[0]
    def proj_norm(av, rv_, wv_, g):
        hv = rv_ + _dot(av, wv_)
        return hv, _rms(hv, g)

    h1, n2 = rowwise(proj_norm, [(mix, None), (xs, None)], [wo, g_cross], [(D, F32), (D, BF16)], [], tile=tile4,
                     name="mm_out_norm")
    xq = matmul(n2, wxq, out_dtype=BF16, name="mm_xq")
    mn = rowwise(lambda mv_, g: _rms(mv_, g), [(mems, None)], [g_mem], [(D, BF16)], [], tile=min(tile, M), name="norm_mem")[0]
    mkv = matmul(mn, wxkv, out_dtype=BF16, name="mm_mkv")

    x_scale = X_HD ** -0.5

    def xattn_fwd(xqv, mkvv):
        outs = []
        for h in range(X_H):
            sl = slice(h * X_HD, (h + 1) * X_HD)
            s = _dot_nt(xqv[:, sl], mkvv[:, sl]) * x_scale
            s = s - jnp.max(s, axis=-1, keepdims=True)
            e = jnp.exp(s)
            p = e / jnp.sum(e, axis=-1, keepdims=True)
            outs.append(_dot(p.astype(BF16), mkvv[:, D + h * X_HD:D + (h + 1) * X_HD]))
        return jnp.concatenate(outs, axis=1)

    xo = rowwise(xattn_fwd, [(xq, None)], [mkv], [(D, BF16)], [], tile=tile4, name="xattn_fwd")[0]
    h2, n3 = rowwise(proj_norm, [(xo, None), (h1, None)], [wxo, g_ffn], [(D, F32), (D, BF16)], [], tile=tile4,
                     name="mm_xo_norm")
    up_pre = matmul(n3, wup, out_dtype=BF16, name="mm_up", tm=2048, tn=1408)
    cw = D_FF // 2
    act = conv_act_fwd(up_pre, wcv, b_conv, tile=tile, cw=cw)

    def down_loss(av, hv2, tv, wv_, g):
        hv = hv2 + _dot(av, wv_)
        y = _rms(hv, g)
        err = y - tv
        part = 0.5 * jnp.sum(jnp.sum(err * err, axis=-1, keepdims=True) / D, axis=0, keepdims=True)
        dx, dg = _rms_bwd(hv, g, err / D)
        return dx, dg, jnp.broadcast_to(part, (8, 128))

    g_fin2 = g_final.reshape(1, D)
    dh3, dg_final, loss_acc = rowwise(down_loss, [(act, None), (h2, None), (tgt, None)], [wdn, g_fin2], [(D, F32)],
                                      [((1, D), F32), ((8, 128), F32)], tile=tile2, name="mm_down_loss")

    dact = matmul(dh3, wdn, tb=True, out_dtype=BF16, name="mm_dact", tm=2048, tn=1408)
    dw_down = matmul_tn(act, dh3, name="mm_dw_down", tm=1408, tk=1024)
    dup_a, dup_b, cs_a, cs_b = conv_act_bwd(up_pre, dact, wcv, b_conv, tile=tile, cw=cw)
    dw_up_a = matmul_tn(n3, dup_a, name="mm_dw_up_a", tn=1408)
    dw_up_b = matmul_tn(n3, dup_b, name="mm_dw_up_b", tn=1408)

    def ffn_in_bwd(da_, db_, hv, drv, wv_, g):
        dn = _dot_nt(da_, wv_[:, :D_FF]) + _dot_nt(db_, wv_[:, D_FF:])
        dx, dg = _rms_bwd(hv, g, dn)
        return dx + drv, dg

    dh2, dg_ffn = rowwise(ffn_in_bwd, [(dup_a, None), (dup_b, None), (h2, None), (dh3, None)], [wup, g_ffn],
                          [(D, F32)], [((1, D), F32)], tile=tile2, name="mm_dn3_norm_bwd")
    dxo = matmul(dh2, wxo, tb=True, out_dtype=BF16, name="mm_dxo")
    dw_xo = matmul_tn(xo, dh2, name="mm_dw_xo")

    def xattn_bwd(xqv, dxov, mkvv):
        dxq, dmk, dmv = [], [], []
        for h in range(X_H):
            sl = slice(h * X_HD, (h + 1) * X_HD)
            slv = slice(D + h * X_HD, D + (h + 1) * X_HD)
            s = _dot_nt(xqv[:, sl], mkvv[:, sl]) * x_scale
            s = s - jnp.max(s, axis=-1, keepdims=True)
            e = jnp.exp(s)
            p = e / jnp.sum(e, axis=-1, keepdims=True)
            dp = _dot_nt(dxov[:, sl], mkvv[:, slv])
            ds = (p * (dp - jnp.sum(dp * p, axis=-1, keepdims=True)) * x_scale).astype(BF16)
            dxq.append(_dot(ds, mkvv[:, sl]))
            dmk.append(_dot_tn(ds, xqv[:, sl]))
            dmv.append(_dot_tn(p.astype(BF16), dxov[:, sl]))
        return jnp.concatenate(dxq, axis=1), jnp.concatenate(dmk + dmv, axis=1)

    dxq, dmkv = rowwise(xattn_bwd, [(xq, None), (dxo, None)], [mkv], [(D, BF16)], [((M, 2 * D), F32)],
                        tile=tile4, name="xattn_bwd")
    dw_xq = matmul_tn(n2, dxq, name="mm_dw_xq")

    def proj_norm_bwd(dyv, hv, drv, wv_, g):
        dx, dg = _rms_bwd(hv, g, _dot_nt(dyv, wv_))
        return dx + drv, dg

    dh1, dg_cross = rowwise(proj_norm_bwd, [(dxq, None), (h1, None), (dh2, None)], [wxq, g_cross], [(D, F32)],
                            [((1, D), F32)], tile=tile4, name="mm_dn2_norm_bwd")
    dw_xkv = matmul_tn(mn, dmkv, name="mm_dw_xkv", tk=M)
    dmn = matmul(dmkv, wxkv, tb=True, name="mm_dmn", tm=M)
    dg_mem = rowwise(lambda mv_, dyv, g: _rms_bwd(mv_, g, dyv)[1], [(mems, None), (dmn, None)], [g_mem], [],
                     [((1, D), F32)], tile=min(tile, M), name="norm_mem_bwd")[0]

    dmix = matmul(dh1, wo, tb=True, out_dtype=BF16, name="mm_dmix")
    dw_out = matmul_tn(mix, dh1, name="mm_dw_out")

    def mix_bwd(ov, rv_, rg, gt, dmv_, wpav, wprv, gr, bg):
        dm_ = dmv_.astype(F32)
        ya = _dot(ov, wpav)
        parts = gn_parts(rv_)
        xh = jnp.concatenate([p[0] for p in parts], axis=1)
        yn = xh * gr
        sg = _sigmoid(rg)
        sl_ = rg * sg
        t = (sl_ * yn).astype(BF16)
        yr = _dot(t, wprv)
        ga_ = _sigmoid(gt[:, :D] + bg[:, :D])
        gr_ = _sigmoid(gt[:, D:] + bg[:, D:])
        dgates = jnp.concatenate([dm_ * ya * ga_ * (1.0 - ga_), dm_ * yr * gr_ * (1.0 - gr_)], axis=1)
        dya = (dm_ * ga_).astype(BF16)
        dyr = (dm_ * gr_).astype(BF16)
        do_ = _dot_nt(dya, wpav)
        dwpa_ = _dot_tn(ov, dya)
        dt = _dot_nt(dyr, wprv)
        dwpr_ = _dot_tn(t, dyr)
        drg_ = dt * yn * (sg * (1.0 + rg * (1.0 - sg)))
        dyn = dt * sl_
        dgr = jnp.sum(dyn * xh, axis=0, keepdims=True)
        dxh = dyn * gr
        drets = []
        for h in range(RET_H):
            sl = slice(h * RET_D, (h + 1) * RET_D)
            xhh, rstd = parts[h]
            dxhh = dxh[:, sl]
            drets.append(rstd * (dxhh - jnp.mean(dxhh, axis=-1, keepdims=True)
                                 - xhh * jnp.mean(dxhh * xhh, axis=-1, keepdims=True)))
        dret_ = jnp.concatenate(drets, axis=1)
        dbg = jnp.sum(dgates, axis=0, keepdims=True)
        prod = ov.astype(F32) * do_.astype(BF16).astype(F32)
        lane = lax.broadcasted_iota(jnp.int32, (prod.shape[0], HP), 1)
        dlt = jnp.zeros((prod.shape[0], HP), F32)
        for h in range(MLA_H):
            dlt = jnp.where(lane == h, jnp.sum(prod[:, h * HP:(h + 1) * HP], axis=-1, keepdims=True), dlt)
        return do_, dret_, drg_, dgates, dlt, dwpa_, dwpr_, dgr, dbg

    do_a, dret, drg, dgates, delta, dwpa, dw_proj_ret, dg_ret, db_gate = rowwise(
        mix_bwd, [(o_a, None), (ret, None), (z, (512, 3)), (z, (2048, 1)), (dmix, None)], [wpa, wpr, g_ret, b_gate],
        [(MLA_H * HP, BF16), (512, F32), (512, BF16), (2 * D, BF16), (HP, F32)],
        [((MLA_H * HP, D), F32), ((512, D), F32), ((1, 512), F32), ((1, 2 * D), F32)], tile=tile2, name="mix_bwd")

    sl1 = slots_l1(dw_up_a, dw_up_b, dw_xkv, tile=tile)
    sl2, sl3 = slots_l23(dw_out, dw_xq, dw_xo, dw_down, dwpa, dw_proj_ret)
    delta_row = mla_prep(delta, tq=tq)
    dq_a, dk_a, dv_a, rl1, rl2, rl3 = mla_bwd(q_a, k_a, v_a, do_a, lse_row, delta_row, [sl1, sl2, sl3], tq=tq)
    drq_r, drk_r, drv = ret_bwd(rq, rk, rv, rstate, dret, rconsts, rb=rb)

    def mixers_in_bwd(zl, cmv, s1v, s2v, dqv, dkv_, dvv, drq_, drk_, drv_, drg_, dgt, crv, srv, gq, gkv, wqv, wkv_, wvv):
        cqf, ckvf = zl[:, 0:256], zl[:, 256:384]
        cq = _rms(cqf, gq).astype(BF16)
        ckv = _rms(ckvf, gkv).astype(BF16)
        dq_pre = _rot_mla(dqv.astype(F32), cmv, -s1v, -s2v).astype(BF16)
        dkf = dkv_.astype(F32)
        dkr = dkf[:, 0:HP]
        for h in range(1, MLA_H):
            dkr = dkr + dkf[:, h * HP:(h + 1) * HP]
        lane = lax.broadcasted_iota(jnp.int32, dkr.shape, 1)
        dzk = _rot_mla(jnp.where((lane >= 64) & (lane < 96), dkr, 0.0), cmv, -s1v, -s2v)
        dkb = dkv_.astype(BF16)
        dvb = dvv.astype(BF16)
        dcq_n = _dot_nt(dq_pre, wqv)
        dckv_n = _dot_nt(dkb, wkv_) + _dot_nt(dvb, wvv)
        dwq_ = _dot_tn(cq, dq_pre)
        dwk_ = _dot_tn(ckv, dkb)
        dwv_ = _dot_tn(ckv, dvb)
        dcq, dgq = _rms_bwd(cqf, gq, dcq_n)
        dckv, dgkv = _rms_bwd(ckvf, gkv, dckv_n)
        a = _rot_ret(drq_, crv, -srv)
        b = _rot_ret(drk_, crv, -srv) * (RET_D ** -0.5)
        dz_ = jnp.concatenate([a.astype(BF16), b.astype(BF16), drv_, drg_, dgt,
                               dcq.astype(BF16), dckv.astype(BF16), dzk.astype(BF16)], axis=1)
        return dz_, dwq_, dwk_, dwv_, dgq, dgkv

    dz, dwq, dwk, dwv, dg_q_lat, dg_kv_lat = rowwise(
        mixers_in_bwd, [(z, (512, 8)), (cm, None), (s1, None), (s2, None), (dq_a, None), (dk_a, None), (dv_a, None),
                        (drq_r, None), (drk_r, None), (drv, None), (drg, None), (dgates, None), (cr, None), (sr, None)],
        [g_q_lat, g_kv_lat, wq, wk, wv], [(ZW, BF16)],
        [((MLA_QR, MLA_H * HP), F32), ((MLA_KVR, MLA_H * HP), F32), ((MLA_KVR, MLA_H * HP), F32),
         ((1, MLA_QR), F32), ((1, MLA_KVR), F32)], tile=tile2, name="mixers_in_bwd")
    dwz = matmul_tn(u, dz, name="mm_dw_z", tn=1536)
    se1, se2 = slots_early(dwz, dwk, dwv, dwq, tile=tile)
    grad_x, dg_mix, re1, re2 = rowwise(proj_norm_bwd, [(dz, None), (xs, None), (dh1, None)], [wz, g_mix], [(D, F32)],
                                       [((1, D), F32)], tile=tile2, name="mm_du_norm_bwd", exchange=[se1, se2])

    small_grads = {"g_mix": dg_mix, "b_gate": db_gate, "g_q_lat": dg_q_lat, "g_kv_lat": dg_kv_lat, "g_ret": dg_ret,
                   "g_cross": dg_cross, "g_mem": dg_mem, "g_ffn": dg_ffn, "g_final": dg_final}
    rd, rs, dslots, own_s = exchange_small(cs_a, cs_b, loss_acc, [small_grads[n] for n, _, _ in SMALL_DIRECT])

    me = _my_id()

    def own(slots):
        return lax.dynamic_index_in_dim(slots, me, axis=0, keepdims=False)

    def wmv(names):
        return [shard(n, p) for n in names for p in ("", "m_", "v_")]

    names_s = tuple(n for n, _, _ in SMALL)
    small_outs = adam_small(own_s, rs, dslots, rd, wmv(names_s), wmv(("w_conv",)))
    groups = (
        (("w_in",), adam_cols(own(se1), re1, wmv(("w_in",)), ((0, NS_IN),), name="adam_e1", tile=128)),
        (("w_ukv", "w_uq"), adam_rows(own(se2), re2, wmv(("w_ukv", "w_uq")),
                                      ((0, E2_UQ, HP), (E2_UQ, E2_ROWS, NS_UQ)), name="adam_e2")),
        (("w_up", "w_xkv"), adam_cols(own(sl1), rl1, wmv(("w_up", "w_xkv")), ((0, NS_UP), (NS_UP, L1_W)),
                                      name="adam_l1", tile=128)),
        (("w_out", "w_xq", "w_xo", "w_down"),
         adam_rows(own(sl2), rl2, wmv(("w_out", "w_xq", "w_xo", "w_down")),
                   ((L2_OUT, L2_XQ, D), (L2_XQ, L2_XO, D), (L2_XO, L2_DN, D), (L2_DN, L2_ROWS, D)), name="adam_l2")),
        (("w_proj_mla", "w_proj_ret"), adam_rows(own(sl3), rl3, wmv(("w_proj_mla", "w_proj_ret")),
                                                 ((0, L3_PRET, HP), (L3_PRET, L3_ROWS, HP)), name="adam_l3")),
        (names_s + ("w_conv",), small_outs),
    )
    loss = small_outs[-1][0, 0]
    res = {}
    for names, outs_ in groups:
        for t, n in enumerate(names):
            res[n] = outs_[4 * t:4 * t + 4]

    order = ["g_mix", "w_in", "b_gate", "g_q_lat", "w_uq", "g_kv_lat", "w_ukv", "w_proj_mla", "g_ret", "w_proj_ret",
             "w_out", "g_cross", "g_mem", "w_xq", "w_xkv", "w_xo", "g_ffn", "w_up", "w_conv", "b_conv", "w_down",
             "g_final"]
    outs = [loss, grad_x[None]]
    for kind in range(4):
        outs += [res[n][kind].reshape(args[n].shape) for n in order]
    return tuple(outs)
```

```python
import functools
import math

import jax
import jax.numpy as jnp
import numpy as np
from jax import lax
from jax.experimental import pallas as pl
from jax.experimental.pallas import tpu as pltpu

F32 = jnp.float32
BF16 = jnp.bfloat16

D = 1024
MLA_H, MLA_NOPE, MLA_ROPE, MLA_V = 8, 64, 32, 64
MLA_QR, MLA_KVR = 256, 128
RET_H, RET_D, RET_C = 4, 128, 128
X_H, X_HD = 4, 256
D_FF = 2816
THETA = 10000.0
EPS = 1e-6
HP = 128
ZW = 4608
N_DEV = 8

ADAM_LR, ADAM_B1, ADAM_B2, ADAM_EPS, ADAM_WD, ADAM_STEP = 0.001, 0.9, 0.999, 1e-08, 0.01, 10

VMEM_LIMIT = 56 * 1024 * 1024
BIG_CONST_BYTES = 4 * 1024 * 1024
MESH = pl.DeviceIdType.MESH
VM = pl.BlockSpec(memory_space=pltpu.VMEM)
ANY = pl.BlockSpec(memory_space=pl.ANY)


def _cp(n_axes):
    return pltpu.CompilerParams(dimension_semantics=("arbitrary",) * n_axes, vmem_limit_bytes=VMEM_LIMIT)


def _cp0():
    return pltpu.CompilerParams(vmem_limit_bytes=VMEM_LIMIT)


def _pick(n, cap, mult=128):
    best = None
    for t in range(mult, min(n, cap) + 1, mult):
        if n % t == 0:
            best = t
    return best if best is not None else n


def _dot(a, b):
    return jnp.dot(a, b, preferred_element_type=F32)


def _dot_nt(a, b):
    return lax.dot_general(a, b, (((1,), (1,)), ((), ())), preferred_element_type=F32)


def _dot_tn(a, b):
    return lax.dot_general(a, b, (((0,), (0,)), ((), ())), preferred_element_type=F32)


def _sds(shape, dtype):
    return jax.ShapeDtypeStruct(shape, dtype)


def matmul(a, b, *, name, tb=False, out_dtype=F32, tm=1024, tn=1024):
    M, K = a.shape
    N = b.shape[0] if tb else b.shape[1]
    tm = _pick(M, tm, 8)
    tn = _pick(N, tn)

    def body(a_ref, b_ref, o_ref):
        av = a_ref[...].astype(BF16)
        bv = b_ref[...].astype(BF16)
        acc = _dot_nt(av, bv) if tb else _dot(av, bv)
        o_ref[...] = acc.astype(o_ref.dtype)

    return pl.pallas_call(
        body,
        name=name,
        grid=(M // tm, N // tn),
        in_specs=[pl.BlockSpec((tm, K), lambda i, j: (i, 0)),
                  pl.BlockSpec((tn, K), lambda i, j: (j, 0)) if tb else pl.BlockSpec((K, tn), lambda i, j: (0, j))],
        out_specs=pl.BlockSpec((tm, tn), lambda i, j: (i, j)),
        out_shape=_sds((M, N), out_dtype),
        compiler_params=_cp(2),
    )(a, b)


def matmul_tn(a, b, *, name, tm=1024, tn=1024, tk=2048):
    R, M = a.shape
    N = b.shape[1]
    tm = _pick(M, tm)
    tn = _pick(N, tn)
    tk = _pick(R, tk, 16)
    nk = R // tk

    def body(a_ref, b_ref, o_ref, acc_ref):
        k = pl.program_id(2)

        @pl.when(k == 0)
        def _():
            acc_ref[...] = jnp.zeros_like(acc_ref)

        acc_ref[...] += _dot_tn(a_ref[...].astype(BF16), b_ref[...].astype(BF16))

        @pl.when(k == nk - 1)
        def _():
            o_ref[...] = acc_ref[...]

    return pl.pallas_call(
        body,
        name=name,
        grid=(M // tm, N // tn, nk),
        in_specs=[pl.BlockSpec((tk, tm), lambda i, j, k: (k, i)), pl.BlockSpec((tk, tn), lambda i, j, k: (k, j))],
        out_specs=pl.BlockSpec((tm, tn), lambda i, j, k: (i, j)),
        out_shape=_sds((M, N), F32),
        scratch_shapes=[pltpu.VMEM((tm, tn), F32)],
        compiler_params=_cp(3),
    )(a, b)


def rowwise(fn, rows, consts, out_rows, out_accs, *, tile, name, gather=(), exchange=()):
    T = rows[0][0].shape[0]
    nt = T // tile
    travel = tuple(gather) + tuple(exchange)
    n_r, n_c, n_o, n_a, n_g = len(rows), len(consts), len(out_rows), len(out_accs), len(travel)
    n_in = n_r + n_c + n_g

    def body(*refs):
        if n_g:
            srcs, dsts = refs[n_r + n_c:n_in], refs[n_in + n_o + n_a:n_in + n_o + n_a + n_g]
            sems = tuple(refs[n_in + n_o + n_a + n_g:])
            if gather:
                comm = (srcs, dsts) + sems
                comm_start, comm_finish = _gather2_start, _gather2_finish
            else:
                comm = ([(r, True) for r in srcs], dsts) + sems
                comm_start, comm_finish = _exchange_start, _exchange_wait

            @pl.when(pl.program_id(0) == 0)
            def _():
                comm_start(*comm)

        ins = [r[...] for r in refs[: n_r + n_c]]
        outs = fn(*ins)
        if not isinstance(outs, (tuple, list)):
            outs = (outs,)
        o_refs = refs[n_in:n_in + n_o]
        a_refs = refs[n_in + n_o:n_in + n_o + n_a]
        for o_ref, o in zip(o_refs, outs[:n_o]):
            o_ref[...] = o.astype(o_ref.dtype)
        if n_a:
            first = pl.program_id(0) == 0

            @pl.when(first)
            def _():
                for a_ref, o in zip(a_refs, outs[n_o:]):
                    a_ref[...] = o.astype(a_ref.dtype)

            @pl.when(jnp.logical_not(first))
            def _():
                for a_ref, o in zip(a_refs, outs[n_o:]):
                    a_ref[...] += o.astype(a_ref.dtype)
        if n_g:

            @pl.when(pl.program_id(0) == nt - 1)
            def _():
                comm_finish(*comm)

    in_specs = []
    args = []
    for arr, win in rows:
        if win is None:
            in_specs.append(pl.BlockSpec((tile, arr.shape[1]), lambda i: (i, 0)))
        else:
            w, cb = win
            in_specs.append(pl.BlockSpec((tile, w), functools.partial(lambda i, cb: (i, cb), cb=cb)))
        args.append(arr)
    for c in consts:
        index_map = functools.partial(lambda i, nd: (0,) * nd, nd=c.ndim)
        if c.size * c.dtype.itemsize >= BIG_CONST_BYTES:
            in_specs.append(pl.BlockSpec(c.shape, index_map, pipeline_mode=pl.Buffered(1)))
        else:
            in_specs.append(pl.BlockSpec(c.shape, index_map))
        args.append(c)
    out_specs = [pl.BlockSpec((tile, w), lambda i: (i, 0)) for w, _ in out_rows]
    out_shape = [_sds((T, w), dt) for w, dt in out_rows]
    for shp, dt in out_accs:
        out_specs.append(pl.BlockSpec(shp, functools.partial(lambda i, nd: (0,) * nd, nd=len(shp))))
        out_shape.append(_sds(shp, dt))
    return pl.pallas_call(
        body,
        name=name,
        grid=(nt,),
        in_specs=in_specs + [ANY] * n_g,
        out_specs=out_specs + [ANY] * n_g,
        out_shape=out_shape + [_sds((N_DEV,) + p.shape, p.dtype) for p in gather]
        + [_sds((N_DEV - 1,) + s.shape[1:], s.dtype) for s in exchange],
        scratch_shapes=(_gather_scratch(n_g) if gather else _exchange_scratch(n_g)) if n_g else [],
        compiler_params=_cp(1),
    )(*args, *travel)


def _rms(x, g):
    r = lax.rsqrt(jnp.mean(x * x, axis=-1, keepdims=True) + EPS)
    return x * r * g


def _rms_bwd(x, g, dy):
    r = lax.rsqrt(jnp.mean(x * x, axis=-1, keepdims=True) + EPS)
    xh = x * r
    dg = jnp.sum(dy * xh, axis=0, keepdims=True)
    dxh = dy * g
    dx = r * (dxh - xh * jnp.mean(dxh * xh, axis=-1, keepdims=True))
    return dx, dg


def _sigmoid(x):
    return 0.5 * jnp.tanh(0.5 * x) + 0.5


def _rot_mla(x, c, s1, s2):
    n = x.shape[1] // HP
    outs = []
    for h in range(n):
        xh = x[:, h * HP : (h + 1) * HP]
        outs.append(xh * c + pltpu.roll(xh, HP - 16, 1) * s1 + pltpu.roll(xh, 16, 1) * s2)
    return outs[0] if n == 1 else jnp.concatenate(outs, axis=1)


def _rot_ret(x, c, s):
    n = x.shape[1] // RET_D
    outs = []
    for h in range(n):
        xh = x[:, h * RET_D : (h + 1) * RET_D]
        outs.append(xh * c + pltpu.roll(xh, RET_D // 2, 1) * s)
    return outs[0] if n == 1 else jnp.concatenate(outs, axis=1)


def _rot_inv():
    lane_np = np.arange(128)
    inv_m = (jnp.asarray(THETA, F32) ** (-jnp.asarray(lane_np & 15, F32) / 16.0)).reshape(1, 128)
    inv_r = (jnp.asarray(THETA, F32) ** (-jnp.asarray(lane_np & 63, F32) / 64.0)).reshape(1, 128)
    return inv_m, inv_r


def _rot_tables(p, im, ir):
    lane = lax.broadcasted_iota(jnp.int32, p.shape, 1)
    ang = p * im
    cm = jnp.where((lane >= 64) & (lane < 96), jnp.cos(ang), 1.0)
    sn = jnp.sin(ang)
    s1 = jnp.where((lane >= 64) & (lane < 80), -sn, 0.0)
    s2 = jnp.where((lane >= 80) & (lane < 96), sn, 0.0)
    angr = p * ir
    snr = jnp.sin(angr)
    return cm, s1, s2, jnp.cos(angr), jnp.where(lane < 64, -snr, snr)


def _peer(m):
    x, y, c = lax.axis_index("x"), lax.axis_index("y"), lax.axis_index("c")
    mx, my, mc = (m >> 2) & 1, (m >> 1) & 1, m & 1
    px = 1 - x if mx else x
    py = 1 - y if my else y
    pc = 1 - c if mc else c
    return (px, py, pc), 4 * px + 2 * py + pc


def _my_id():
    return 4 * lax.axis_index("x") + 2 * lax.axis_index("y") + lax.axis_index("c")


def _gather_copies(srcs, outs, send_sems, recv_sems, local_sems, arriving=False):
    me = _my_id()
    copies = []
    if not arriving:
        for g, (src, out) in enumerate(zip(srcs, outs)):
            copies.append((pltpu.make_async_copy(src, out.at[me], local_sems.at[g]), False))
    for m in range(1, N_DEV):
        peer, plin = _peer(m)
        for g, (src, out) in enumerate(zip(srcs, outs)):
            copies.append((pltpu.make_async_remote_copy(
                src_ref=src, dst_ref=out.at[plin if arriving else me], send_sem=send_sems.at[g, m - 1],
                recv_sem=recv_sems.at[g, m - 1], device_id=peer, device_id_type=MESH), True))
    return copies


def _gather_start(*a):
    for cp, _ in _gather_copies(*a):
        cp.start()


def _gather_wait(*a):
    for cp, _ in _gather_copies(*a, arriving=True):
        cp.wait_recv()
    for cp, remote in _gather_copies(*a):
        if remote:
            cp.wait_send()
        else:
            cp.wait()


CHIP_RELATIONS = (4, 2, 6)


def _gather2_copy(src, out, block, to, send_sems, recv_sems, g, k):
    return pltpu.make_async_remote_copy(src_ref=src, dst_ref=out.at[block], send_sem=send_sems.at[g, k],
                                        recv_sem=recv_sems.at[g, k], device_id=to, device_id_type=MESH)


def _gather2_start(srcs, outs, send_sems, recv_sems, local_sems):
    me = _my_id()
    sib, _ = _peer(1)
    for g, (src, out) in enumerate(zip(srcs, outs)):
        pltpu.make_async_copy(src, out.at[me], local_sems.at[g]).start()
        _gather2_copy(src, out, me, sib, send_sems, recv_sems, g, 0).start()
        for t, m in enumerate(CHIP_RELATIONS):
            _gather2_copy(src, out, me, _peer(m)[0], send_sems, recv_sems, g, 1 + t).start()


def _gather2_finish(srcs, outs, send_sems, recv_sems, local_sems):
    me = _my_id()
    sib, sib_lin = _peer(1)
    for t, m in enumerate(CHIP_RELATIONS):
        peer, plin = _peer(m)
        for g, (src, out) in enumerate(zip(srcs, outs)):
            _gather2_copy(src, out, plin, peer, send_sems, recv_sems, g, 1 + t).wait_recv()
            _gather2_copy(out.at[plin], out, plin, sib, send_sems, recv_sems, g, 4 + t).start()
    for g, (src, out) in enumerate(zip(srcs, outs)):
        _gather2_copy(src, out, sib_lin, sib, send_sems, recv_sems, g, 0).wait_recv()
        for t, m in enumerate(CHIP_RELATIONS):
            _gather2_copy(src, out, _peer(m | 1)[1], sib, send_sems, recv_sems, g, 4 + t).wait_recv()
    for g, (src, out) in enumerate(zip(srcs, outs)):
        _gather2_copy(src, out, me, sib, send_sems, recv_sems, g, 0).wait_send()
        for t, m in enumerate(CHIP_RELATIONS):
            peer, plin = _peer(m)
            _gather2_copy(src, out, me, peer, send_sems, recv_sems, g, 1 + t).wait_send()
            _gather2_copy(out.at[plin], out, plin, sib, send_sems, recv_sems, g, 4 + t).wait_send()
        pltpu.make_async_copy(src, out.at[me], local_sems.at[g]).wait()


def _gather_scratch(n):
    return [pltpu.SemaphoreType.DMA((n, N_DEV - 1)), pltpu.SemaphoreType.DMA((n, N_DEV - 1)), pltpu.SemaphoreType.DMA((n,))]


def _exchange_copies(srcs, dsts, send_sems, recv_sems):
    copies = []
    for m in range(1, N_DEV):
        peer, plin = _peer(m)
        for g, ((src, per_peer), dst) in enumerate(zip(srcs, dsts)):
            copies.append(pltpu.make_async_remote_copy(
                src_ref=src.at[plin] if per_peer else src, dst_ref=dst.at[m - 1], send_sem=send_sems.at[g, m - 1],
                recv_sem=recv_sems.at[g, m - 1], device_id=peer, device_id_type=MESH))
    return copies


def _exchange_start(*a):
    for cp in _exchange_copies(*a):
        cp.start()


def _exchange_wait(*a):
    copies = _exchange_copies(*a)
    for cp in copies:
        cp.wait_recv()
    for cp in copies:
        cp.wait_send()


def _exchange_scratch(n):
    return [pltpu.SemaphoreType.DMA((n, N_DEV - 1)), pltpu.SemaphoreType.DMA((n, N_DEV - 1))]


MLA_SCALE = (MLA_NOPE + MLA_ROPE) ** -0.5
MLA_C2 = MLA_SCALE * math.log2(math.e)
ONE_LANE = MLA_V
NEG = -1e30
HPS = 4
HPB = 4


def _tri_mask(n, lower_rows_ge_cols=True):
    r = lax.broadcasted_iota(jnp.int32, (n, n), 0)
    c = lax.broadcasted_iota(jnp.int32, (n, n), 1)
    return r >= c if lower_rows_ge_cols else c >= r


def mla_fwd(q, k, v, gather, *, tq):
    T = q.shape[0]
    nq = T // tq
    rep = tq // HP
    ng = len(gather)

    def body(*refs):
        q_ref, k_ref, v_ref = refs[:3]
        srcs = refs[3:3 + ng]
        o_ref, lse_ref = refs[3 + ng:5 + ng]
        outs = refs[5 + ng:5 + 2 * ng]
        m_sc, acc_sc = refs[5 + 2 * ng:7 + 2 * ng]
        comm = (srcs, outs) + tuple(refs[7 + 2 * ng:])
        h, i = pl.program_id(0), pl.program_id(1)

        @pl.when((h == 0) & (i == 0))
        def _():
            _gather_start(*comm)

        m_sc[...] = jnp.full(m_sc.shape, NEG, F32)
        acc_sc[...] = jnp.zeros(acc_sc.shape, F32)
        heads = [slice(t * HP, (t + 1) * HP) for t in range(HPS)]

        def block(j, masked):
            off = pl.multiple_of(j * tq, tq)
            ss = [_dot_nt(q_ref[:, hd], k_ref[pl.ds(off, tq), hd]) * MLA_C2 for hd in heads]
            mask = _tri_mask(tq) if masked else None
            for hd, s in zip(heads, ss):
                if masked:
                    s = jnp.where(mask, s, NEG)
                m_prev = m_sc[:, hd]
                m_next = jnp.maximum(m_prev, jnp.max(s, axis=-1, keepdims=True))
                p = jnp.exp2(s - jnp.tile(m_next, (1, rep)))
                alpha = jnp.exp2(m_prev - m_next)
                acc_sc[:, hd] = alpha * acc_sc[:, hd] + _dot(p.astype(BF16), v_ref[pl.ds(off, tq), hd])
                m_sc[:, hd] = m_next

        def loop_body(p, carry):
            block(2 * p, False)
            block(2 * p + 1, False)
            return carry

        lax.fori_loop(0, i // 2, loop_body, 0)

        @pl.when(i % 2 == 1)
        def _():
            block(i - 1, False)

        block(i, True)
        lane = lax.broadcasted_iota(jnp.int32, (tq, HP), 1)
        lse_cols = jnp.zeros((tq, HP), F32)
        for t, hd in enumerate(heads):
            acc = acc_sc[:, hd]
            l = acc[:, ONE_LANE:ONE_LANE + 1]
            o_ref[:, hd] = (acc / l).astype(o_ref.dtype)
            lse_cols = jnp.where(lane == t, m_sc[:, hd] + jnp.log(l) * math.log2(math.e), lse_cols)
        lse_rows = lse_cols.T
        for t in range(HPS):
            lse_ref[t] = lse_rows[t:t + 1, :]

        @pl.when((h == MLA_H // HPS - 1) & (i == nq - 1))
        def _():
            _gather_wait(*comm)

    blk = pl.BlockSpec((tq, HPS * HP), lambda h, i: (i, h))
    full = pl.BlockSpec((T, HPS * HP), lambda h, i: (0, h))
    return pl.pallas_call(
        body,
        name="mla_fwd",
        grid=(MLA_H // HPS, nq),
        in_specs=[blk, full, full] + [ANY] * ng,
        out_specs=[blk, pl.BlockSpec((HPS, None, 1, tq), lambda h, i: (h, i, 0, 0))] + [ANY] * ng,
        out_shape=[_sds((T, MLA_H * HP), BF16), _sds((MLA_H, nq, 1, tq), F32)]
        + [_sds((N_DEV,) + p.shape, p.dtype) for p in gather],
        scratch_shapes=[pltpu.VMEM((tq, HPS * HP), F32), pltpu.VMEM((tq, HPS * HP), F32)] + _gather_scratch(ng),
        compiler_params=_cp(2),
    )(q, k, v, *gather)


def mla_prep(delta, *, tq):
    T = delta.shape[0]
    nq = T // tq

    def body(d_ref, row_ref):
        dt = d_ref[...].T
        for h in range(MLA_H):
            row_ref[h] = dt[h:h + 1, :]

    return pl.pallas_call(
        body,
        name="mla_prep",
        grid=(nq,),
        in_specs=[pl.BlockSpec((tq, HP), lambda i: (i, 0))],
        out_specs=pl.BlockSpec((MLA_H, None, 1, tq), lambda i: (0, i, 0, 0)),
        out_shape=_sds((MLA_H, nq, 1, tq), F32),
        compiler_params=_cp(1),
    )(delta)


def mla_bwd(q, k, v, do, lse_row, delta_row, slots, *, tq):
    T = q.shape[0]
    nq = T // tq
    ns = len(slots)

    def body(*refs):
        q_ref, k_ref, v_ref, do_ref, lse_ref, delta_ref = refs[:6]
        srcs = [(r, True) for r in refs[6:6 + ns]]
        dq_ref, dk_ref, dv_ref = refs[6 + ns:9 + ns]
        dsts = refs[9 + ns:9 + 2 * ns]
        dk_sc, dv_sc = refs[9 + 2 * ns:11 + 2 * ns]
        comm = (srcs, dsts) + tuple(refs[11 + 2 * ns:])
        h, j = pl.program_id(0), pl.program_id(1)

        @pl.when((h == 0) & (j == 0))
        def _():
            _exchange_start(*comm)

        dk_sc[...] = jnp.zeros(dk_sc.shape, F32)
        dv_sc[...] = jnp.zeros(dv_sc.shape, F32)
        heads = [slice(t * HP, (t + 1) * HP) for t in range(HPB)]

        @pl.when(j == 0)
        def _():
            dq_ref[...] = jnp.zeros(dq_ref.shape, F32)

        def block(i, masked):
            off = pl.multiple_of(i * tq, tq)
            sts = [_dot_nt(k_ref[:, hd], q_ref[pl.ds(off, tq), hd]) * MLA_C2 for hd in heads]
            dpts = [_dot_nt(v_ref[:, hd], do_ref[pl.ds(off, tq), hd]) for hd in heads]
            mask = _tri_mask(tq, False) if masked else None
            for t, hd in enumerate(heads):
                st = sts[t]
                if masked:
                    st = jnp.where(mask, st, NEG)
                pt = jnp.exp2(st - lse_ref[t, i])
                dv_sc[:, hd] += _dot(pt.astype(BF16), do_ref[pl.ds(off, tq), hd])
                dst = (pt * (dpts[t] - delta_ref[t, i]) * MLA_SCALE).astype(BF16)
                dk_sc[:, hd] += _dot(dst, q_ref[pl.ds(off, tq), hd])
                dq_ref[pl.ds(off, tq), hd] += _dot_tn(dst, k_ref[:, hd])

        block(j, True)

        def loop_body(p, carry):
            block(j + 1 + 2 * p, False)
            block(j + 2 + 2 * p, False)
            return carry

        rest = nq - 1 - j
        lax.fori_loop(0, rest // 2, loop_body, 0)

        @pl.when(rest % 2 == 1)
        def _():
            block(nq - 1, False)

        dk_ref[...] = dk_sc[...].astype(dk_ref.dtype)
        dv_ref[...] = dv_sc[...].astype(dv_ref.dtype)

        @pl.when((h == MLA_H // HPB - 1) & (j == nq - 1))
        def _():
            _exchange_wait(*comm)

    blk = pl.BlockSpec((tq, HPB * HP), lambda h, j: (j, h))
    full = pl.BlockSpec((T, HPB * HP), lambda h, j: (0, h), pipeline_mode=pl.Buffered(1))
    rows = pl.BlockSpec((HPB, nq, 1, tq), lambda h, j: (h, 0, 0, 0))
    return pl.pallas_call(
        body,
        name="mla_bwd",
        grid=(MLA_H // HPB, nq),
        in_specs=[full, blk, blk, full, rows, rows] + [ANY] * ns,
        out_specs=[full, blk, blk] + [ANY] * ns,
        out_shape=[_sds((T, MLA_H * HP), F32), _sds((T, MLA_H * HP), BF16), _sds((T, MLA_H * HP), BF16)]
        + [_sds((N_DEV - 1,) + s.shape[1:], s.dtype) for s in slots],
        scratch_shapes=[pltpu.VMEM((tq, HPB * HP), F32), pltpu.VMEM((tq, HPB * HP), F32)] + _exchange_scratch(ns),
        compiler_params=_cp(2),
    )(q, k, v, do, lse_row, delta_row, *slots)


def _ret_consts():
    h = jnp.arange(RET_H, dtype=F32)
    log_g = jnp.log1p(-jnp.exp2(-5.0 - h))
    idx = jnp.arange(RET_C, dtype=F32)
    rel = idx[:, None] - idx[None, :]
    dmask = jnp.where(rel >= 0, jnp.exp(log_g[:, None, None] * jnp.maximum(rel, 0.0)), 0.0)
    zeta = jnp.exp(log_g[:, None] * (RET_C - 1.0 - idx)[None, :])
    xi = jnp.exp(log_g[:, None] * (idx + 1.0)[None, :])
    decay = jnp.exp(log_g * RET_C)
    zb = jnp.broadcast_to(zeta[:, :, None], (RET_H, RET_C, RET_D))
    xb = jnp.broadcast_to(xi[:, :, None], (RET_H, RET_C, RET_D))
    db = jnp.broadcast_to(decay[:, None, None], (RET_H, RET_C, RET_D))
    return dmask.astype(F32), zb.astype(F32), xb.astype(F32), db.astype(F32)


def ret_fwd(rq, rk, rv, consts, *, rb):
    T = rq.shape[0]
    nb = T // rb
    ncb = rb // RET_C

    def body(q_ref, k_ref, v_ref, dm_ref, z_ref, x_ref, dc_ref, o_ref, st_ref, r_sc):
        @pl.when(pl.program_id(0) == 0)
        def _():
            r_sc[...] = jnp.zeros(r_sc.shape, F32)

        for c in range(ncb):
            sl = slice(c * RET_C, (c + 1) * RET_C)
            for h in range(RET_H):
                hd = slice(h * RET_D, (h + 1) * RET_D)
                q, k, v = q_ref[sl, hd], k_ref[sl, hd], v_ref[sl, hd]
                r = r_sc[h]
                rbf = r.astype(BF16)
                st_ref[sl, hd] = rbf
                s = _dot_nt(q, k) * dm_ref[h]
                inner = _dot(s.astype(BF16), v)
                cross = _dot((q.astype(F32) * x_ref[h]).astype(BF16), rbf)
                o_ref[sl, hd] = inner + cross
                kz = (k.astype(F32) * z_ref[h]).T.astype(BF16)
                r_sc[h] = r * dc_ref[h] + _dot(kz, v)

    blk = pl.BlockSpec((rb, RET_H * RET_D), lambda b: (b, 0))
    cst = pl.BlockSpec((RET_H, RET_C, RET_D), lambda b: (0, 0, 0))
    return pl.pallas_call(
        body,
        name="ret_fwd",
        grid=(nb,),
        in_specs=[blk, blk, blk, cst, cst, cst, cst],
        out_specs=[blk, blk],
        out_shape=[_sds((T, RET_H * RET_D), F32), _sds((T, RET_H * RET_D), BF16)],
        scratch_shapes=[pltpu.VMEM((RET_H, RET_D, RET_D), F32)],
        compiler_params=_cp(1),
    )(rq, rk, rv, *consts)


def ret_bwd(rq, rk, rv, st, dret, consts, *, rb):
    T = rq.shape[0]
    nb = T // rb
    ncb = rb // RET_C

    def body(q_ref, k_ref, v_ref, st_ref, do_ref, dm_ref, z_ref, x_ref, dc_ref, dq_ref, dk_ref, dv_ref, g_sc):
        @pl.when(pl.program_id(0) == 0)
        def _():
            g_sc[...] = jnp.zeros(g_sc.shape, F32)

        for c in reversed(range(ncb)):
            sl = slice(c * RET_C, (c + 1) * RET_C)
            for h in range(RET_H):
                hd = slice(h * RET_D, (h + 1) * RET_D)
                dm, zt, xi = dm_ref[h], z_ref[h], x_ref[h]
                q, k, v, rp = q_ref[sl, hd], k_ref[sl, hd], v_ref[sl, hd], st_ref[sl, hd]
                dob = do_ref[sl, hd].astype(BF16)
                qf, kf = q.astype(F32), k.astype(F32)
                gn = g_sc[h]
                gnb = gn.astype(BF16)
                s = _dot_nt(q, k) * dm
                ds = _dot_nt(dob, v) * dm
                dq = _dot(ds.astype(BF16), k) + _dot_nt(dob, rp) * xi
                dk = _dot(ds.T.astype(BF16), q) + _dot_nt(v, gnb) * zt
                dv = _dot(s.T.astype(BF16), dob) + _dot((kf * zt).astype(BF16), gnb)
                dq_ref[sl, hd] = dq.astype(dq_ref.dtype)
                dk_ref[sl, hd] = dk.astype(dk_ref.dtype)
                dv_ref[sl, hd] = dv.astype(dv_ref.dtype)
                g_sc[h] = _dot((qf * xi).T.astype(BF16), dob) + dc_ref[h] * gn

    blk = pl.BlockSpec((rb, RET_H * RET_D), lambda b: (nb - 1 - b, 0))
    cst = pl.BlockSpec((RET_H, RET_C, RET_D), lambda b: (0, 0, 0))
    return pl.pallas_call(
        body,
        name="ret_bwd",
        grid=(nb,),
        in_specs=[blk, blk, blk, blk, blk, cst, cst, cst, cst],
        out_specs=[blk, blk, blk],
        out_shape=[_sds((T, RET_H * RET_D), F32), _sds((T, RET_H * RET_D), F32), _sds((T, RET_H * RET_D), BF16)],
        scratch_shapes=[pltpu.VMEM((RET_H, RET_D, RET_D), F32)],
        compiler_params=_cp(1),
    )(rq, rk, rv, st, dret, *consts)


HALO = 16
EDGE = 8


def conv_act_fwd(up_pre, w_conv, b_conv, *, tile, cw):
    T = up_pre.shape[0]
    nt = T // tile
    ncol = D_FF // cw
    hb = tile // HALO

    def body(pa_ref, a_ref, pb_ref, b_ref, wa_ref, wb_ref, ba_ref, bb_ref, o_ref):
        i = pl.program_id(1)
        keep = (i > 0).astype(F32)

        def conv(prev_ref, cur_ref, w_ref, bias_ref):
            ext = jnp.concatenate([prev_ref[...].astype(F32)[HALO - EDGE:, :] * keep, cur_ref[...].astype(F32)], axis=0)
            w = w_ref[...]
            y = ext * w[2:3, :] + pltpu.roll(ext, 1, 0) * w[1:2, :] + pltpu.roll(ext, 2, 0) * w[0:1, :] + bias_ref[...]
            return y[EDGE:, :]

        a = conv(pa_ref, a_ref, wa_ref, ba_ref)
        b = conv(pb_ref, b_ref, wb_ref, bb_ref)
        o_ref[...] = (a * _sigmoid(a) * b).astype(o_ref.dtype)

    prev_a = pl.BlockSpec((HALO, cw), lambda j, i: (jnp.maximum(i * hb - 1, 0), j))
    cur_a = pl.BlockSpec((tile, cw), lambda j, i: (i, j))
    prev_b = pl.BlockSpec((HALO, cw), lambda j, i: (jnp.maximum(i * hb - 1, 0), j + ncol))
    cur_b = pl.BlockSpec((tile, cw), lambda j, i: (i, j + ncol))
    w_a = pl.BlockSpec((3, cw), lambda j, i: (0, j))
    w_b = pl.BlockSpec((3, cw), lambda j, i: (0, j + ncol))
    bias_a = pl.BlockSpec((1, cw), lambda j, i: (0, j))
    bias_b = pl.BlockSpec((1, cw), lambda j, i: (0, j + ncol))
    return pl.pallas_call(
        body,
        name="conv_act_fwd",
        grid=(ncol, nt),
        in_specs=[prev_a, cur_a, prev_b, cur_b, w_a, w_b, bias_a, bias_b],
        out_specs=pl.BlockSpec((tile, cw), lambda j, i: (i, j)),
        out_shape=_sds((T, D_FF), BF16),
        compiler_params=_cp(2),
    )(up_pre, up_pre, up_pre, up_pre, w_conv, w_conv, b_conv, b_conv)


def conv_act_bwd(up_pre, dact, w_conv, b_conv, *, tile, cw):
    T = up_pre.shape[0]
    nt = T // tile
    ncol = D_FF // cw
    hb = tile // HALO
    ext_rows = tile + 2 * EDGE

    def body(pa_ref, a_ref, na_ref, pb_ref, b_ref, nb_ref, d_ref, nd_ref, wa_ref, wb_ref, ba_ref, bb_ref,
             dxa_ref, dxb_ref, sa_ref, sb_ref):
        i = pl.program_id(1)
        keep_p = (i > 0).astype(F32)
        keep_n = (i < nt - 1).astype(F32)

        def ext_of(prev_ref, cur_ref, next_ref):
            return jnp.concatenate(
                [prev_ref[...].astype(F32)[HALO - EDGE:, :] * keep_p, cur_ref[...].astype(F32),
                 next_ref[...].astype(F32)[:EDGE, :] * keep_n], axis=0)

        def taps(ext):
            return ext, pltpu.roll(ext, 1, 0), pltpu.roll(ext, 2, 0)

        def conv(tp, w, bias):
            return tp[0] * w[2:3, :] + tp[1] * w[1:2, :] + tp[2] * w[0:1, :] + bias

        xa = taps(ext_of(pa_ref, a_ref, na_ref))
        xb = taps(ext_of(pb_ref, b_ref, nb_ref))
        wa, wb = wa_ref[...], wb_ref[...]
        a = conv(xa, wa, ba_ref[...])
        b = conv(xb, wb, bb_ref[...])
        dy = jnp.concatenate(
            [jnp.zeros((EDGE, cw), F32), d_ref[...].astype(F32), nd_ref[...].astype(F32)[:EDGE, :] * keep_n], axis=0)
        sg = _sigmoid(a)
        da = dy * b * (sg * (1.0 + a * (1.0 - sg)))
        db = dy * (a * sg)

        def back(dup, tp, w, dx_ref, s_ref):
            dx = dup * w[2:3, :] + pltpu.roll(dup, ext_rows - 1, 0) * w[1:2, :] + pltpu.roll(dup, ext_rows - 2, 0) * w[0:1, :]
            dx_ref[...] = dx[EDGE:EDGE + tile, :].astype(dx_ref.dtype)
            dc = dup[EDGE:EDGE + tile, :]
            r2 = jnp.sum(dc * tp[0][EDGE:EDGE + tile, :], axis=0, keepdims=True)
            r1 = jnp.sum(dc * tp[1][EDGE:EDGE + tile, :], axis=0, keepdims=True)
            r0 = jnp.sum(dc * tp[2][EDGE:EDGE + tile, :], axis=0, keepdims=True)
            rb = jnp.sum(dc, axis=0, keepdims=True)
            row = lax.broadcasted_iota(jnp.int32, (8, cw), 0)
            upd = (jnp.where(row == 0, r0, 0.0) + jnp.where(row == 1, r1, 0.0) + jnp.where(row == 2, r2, 0.0)
                   + jnp.where(row == 3, rb, 0.0))

            @pl.when(i == 0)
            def _():
                s_ref[...] = upd

            @pl.when(i > 0)
            def _():
                s_ref[...] += upd

        back(da, xa, wa, dxa_ref, sa_ref)
        back(db, xb, wb, dxb_ref, sb_ref)

    def prev_of(shift):
        return pl.BlockSpec((HALO, cw), lambda j, i: (jnp.maximum(i * hb - 1, 0), j + shift))

    def next_of(shift):
        return pl.BlockSpec((HALO, cw), lambda j, i: (jnp.minimum((i + 1) * hb, nt * hb - 1), j + shift))

    def cur_of(shift):
        return pl.BlockSpec((tile, cw), lambda j, i: (i, j + shift))

    def row_of(rows, shift):
        return pl.BlockSpec((rows, cw), lambda j, i: (0, j + shift))

    return pl.pallas_call(
        body,
        name="conv_act_bwd",
        grid=(ncol, nt),
        in_specs=[prev_of(0), cur_of(0), next_of(0), prev_of(ncol), cur_of(ncol), next_of(ncol), cur_of(0), next_of(0),
                  row_of(3, 0), row_of(3, ncol), row_of(1, 0), row_of(1, ncol)],
        out_specs=[cur_of(0), cur_of(0), row_of(8, 0), row_of(8, 0)],
        out_shape=[_sds((T, D_FF), BF16), _sds((T, D_FF), BF16), _sds((8, D_FF), F32), _sds((8, D_FF), F32)],
        compiler_params=_cp(2),
    )(up_pre, up_pre, up_pre, up_pre, up_pre, up_pre, dact, dact, w_conv, w_conv, b_conv, b_conv)


NS_IN, NS_UP, NS_XKV, NS_UQ = 564, 704, 256, 96
E2_ROWS, E2_UQ = 384, 128
L1_W = NS_UP + NS_XKV
L2_ROWS = 736
L2_OUT, L2_XQ, L2_XO, L2_DN = 0, 128, 256, 384
RS_DN = 352
L3_ROWS, L3_PRET = 1024, 512
L4_SHAPE = (8, 768)
WZ_RUNS = ((416, 4096, 0), (0, 384, 4096), (384, 32, 4544))
WZ_ZERO = ((4480, 4544), (4576, 4608))


def _pieces(orig_start, length, dst_start, ns):
    out, c, d, end = [], orig_start, dst_start, orig_start + length
    while c < end:
        j, off = c // ns, c % ns
        ln = min(ns - off, end - c)
        out.append((j, off, d, ln))
        c += ln
        d += ln
    return out


def pack_early(w_in, w_ukv, w_uq):
    def body(in_ref, ukv_ref, uq_ref, e1_ref, e2_ref):
        e1_ref[...] = in_ref[...].astype(BF16)
        e2_ref[0:E2_UQ, :] = ukv_ref[...].astype(BF16)
        e2_ref[E2_UQ:E2_ROWS, 0:NS_UQ] = uq_ref[...].astype(BF16)
        e2_ref[E2_UQ:E2_ROWS, NS_UQ:HP] = jnp.zeros((E2_ROWS - E2_UQ, HP - NS_UQ), BF16)

    return pl.pallas_call(
        body,
        name="pack_early",
        in_specs=[VM] * 3,
        out_specs=[VM] * 2,
        out_shape=[_sds((D, NS_IN), BF16), _sds((E2_ROWS, HP), BF16)],
        compiler_params=_cp0(),
    )(w_in, w_ukv, w_uq)


def pack_late(w_up, w_xkv, w_out, w_xq, w_xo, w_down, w_pmla, w_pret, w_conv):
    def body(up_ref, xkv_ref, o_ref, xq_ref, xo_ref, dn_ref, pm_ref, pr_ref, cv_ref, l1_ref, l2_ref, l3_ref, l4_ref):
        l1_ref[:, 0:NS_UP] = up_ref[...].astype(BF16)
        l1_ref[:, NS_UP:L1_W] = xkv_ref[...].astype(BF16)
        l2_ref[L2_OUT:L2_XQ, :] = o_ref[...].astype(BF16)
        l2_ref[L2_XQ:L2_XO, :] = xq_ref[...].astype(BF16)
        l2_ref[L2_XO:L2_DN, :] = xo_ref[...].astype(BF16)
        l2_ref[L2_DN:L2_ROWS, :] = dn_ref[...].astype(BF16)
        l3_ref[0:L3_PRET, :] = pm_ref[...].astype(BF16)
        l3_ref[L3_PRET:L3_ROWS, :] = pr_ref[...].astype(BF16)
        l4_ref[...] = jnp.zeros(L4_SHAPE, F32)
        l4_ref[0:3, 0:NS_UP] = cv_ref[...]

    return pl.pallas_call(
        body,
        name="pack_late",
        in_specs=[VM] * 9,
        out_specs=[VM] * 4,
        out_shape=[_sds((D, L1_W), BF16), _sds((L2_ROWS, D), BF16), _sds((L3_ROWS, HP), BF16), _sds(L4_SHAPE, F32)],
        compiler_params=_cp0(),
    )(w_up, w_xkv, w_out, w_xq, w_xo, w_down, w_pmla, w_pret, w_conv)


def assemble_early(g1, g2, *, tile):
    def body(g1_ref, g2_ref, wz_ref, wk_ref, wv_ref, wq_ref):
        for lo, hi in WZ_ZERO:
            wz_ref[:, lo:hi] = jnp.zeros((tile, hi - lo), BF16)
        for os_, ln_, ds_ in WZ_RUNS:
            for j, off, d, ln in _pieces(os_, ln_, ds_, NS_IN):
                wz_ref[:, d:d + ln] = g1_ref[j, :, off:off + ln]

        @pl.when(pl.program_id(0) == 0)
        def _():
            half = jnp.zeros((MLA_KVR, HP - MLA_NOPE), BF16)
            for j in range(N_DEV):
                wk_ref[:, j * HP:j * HP + MLA_NOPE] = g2_ref[j, 0:E2_UQ, 0:MLA_NOPE]
                wk_ref[:, j * HP + MLA_NOPE:(j + 1) * HP] = half
                wv_ref[:, j * HP:j * HP + MLA_V] = g2_ref[j, 0:E2_UQ, MLA_NOPE:HP]
                wv_ref[:, j * HP + MLA_V:(j + 1) * HP] = half
                wq_ref[:, j * HP:j * HP + NS_UQ] = g2_ref[j, E2_UQ:E2_ROWS, 0:NS_UQ]
                wq_ref[:, j * HP + NS_UQ:(j + 1) * HP] = jnp.zeros((MLA_QR, HP - NS_UQ), BF16)

    def whole(r):
        return pl.BlockSpec((r, MLA_H * HP), lambda i: (0, 0))

    return pl.pallas_call(
        body,
        name="assemble_early",
        grid=(D // tile,),
        in_specs=[pl.BlockSpec((N_DEV, tile, NS_IN), lambda i: (0, i, 0)),
                  pl.BlockSpec((N_DEV, E2_ROWS, HP), lambda i: (0, 0, 0))],
        out_specs=[pl.BlockSpec((tile, ZW), lambda i: (i, 0)), whole(MLA_KVR), whole(MLA_KVR), whole(MLA_QR)],
        out_shape=[_sds((D, ZW), BF16), _sds((MLA_KVR, MLA_H * HP), BF16), _sds((MLA_KVR, MLA_H * HP), BF16),
                   _sds((MLA_QR, MLA_H * HP), BF16)],
        compiler_params=_cp(1),
    )(g1, g2)


def assemble_l1(g1, *, tile):
    def body(g_ref, wup_ref, wxkv_ref):
        for j in range(N_DEV):
            wup_ref[:, j * NS_UP:(j + 1) * NS_UP] = g_ref[j, :, 0:NS_UP]
            wxkv_ref[:, j * NS_XKV:(j + 1) * NS_XKV] = g_ref[j, :, NS_UP:L1_W]

    return pl.pallas_call(
        body,
        name="assemble_l1",
        grid=(D // tile,),
        in_specs=[pl.BlockSpec((N_DEV, tile, L1_W), lambda i: (0, i, 0))],
        out_specs=[pl.BlockSpec((tile, 2 * D_FF), lambda i: (i, 0)), pl.BlockSpec((tile, 2 * D), lambda i: (i, 0))],
        out_shape=[_sds((D, 2 * D_FF), BF16), _sds((D, 2 * D), BF16)],
        compiler_params=_cp(1),
    )(g1)


def assemble_l234(g2, g3, g4):
    def body(g2_ref, g3_ref, g4_ref, wo_ref, wxq_ref, wxo_ref, wdn_ref, wpa_ref, wpr_ref, wc_ref):
        for j in range(N_DEV):
            wo_ref[j * 128:(j + 1) * 128, :] = g2_ref[j, L2_OUT:L2_XQ, :]
            wxq_ref[j * 128:(j + 1) * 128, :] = g2_ref[j, L2_XQ:L2_XO, :]
            wxo_ref[j * 128:(j + 1) * 128, :] = g2_ref[j, L2_XO:L2_DN, :]
            wdn_ref[j * RS_DN:(j + 1) * RS_DN, :] = g2_ref[j, L2_DN:L2_ROWS, :]
            wpr_ref[:, j * 128:(j + 1) * 128] = g3_ref[j, L3_PRET:L3_ROWS, :]
            wc_ref[:, j * NS_UP:(j + 1) * NS_UP] = g4_ref[j, 0:3, 0:NS_UP]
            for h in range(MLA_H):
                wpa_ref[h * HP:h * HP + MLA_V, j * 128:(j + 1) * 128] = g3_ref[j, h * MLA_V:(h + 1) * MLA_V, :]
        for h in range(MLA_H):
            wpa_ref[h * HP + MLA_V:(h + 1) * HP, :] = jnp.zeros((HP - MLA_V, D), BF16)

    return pl.pallas_call(
        body,
        name="assemble_l234",
        in_specs=[VM] * 3,
        out_specs=[VM] * 7,
        out_shape=[_sds((D, D), BF16), _sds((D, D), BF16), _sds((D, D), BF16), _sds((D_FF, D), BF16),
                   _sds((MLA_H * HP, D), BF16), _sds((RET_H * RET_D, D), BF16), _sds((3, 2 * D_FF), F32)],
        compiler_params=_cp0(),
    )(g2, g3, g4)


def slots_l1(dwup_a, dwup_b, dwxkv, *, tile):
    per_half = D_FF // NS_UP

    def body(ua_ref, ub_ref, x_ref, s_ref):
        for j in range(N_DEV):
            src = ua_ref if j < per_half else ub_ref
            c0 = (j % per_half) * NS_UP
            s_ref[j, :, 0:NS_UP] = src[:, c0:c0 + NS_UP].astype(BF16)
            s_ref[j, :, NS_UP:L1_W] = x_ref[:, j * NS_XKV:(j + 1) * NS_XKV].astype(BF16)

    return pl.pallas_call(
        body,
        name="slots_l1",
        grid=(D // tile,),
        in_specs=[pl.BlockSpec((tile, D_FF), lambda i: (i, 0)), pl.BlockSpec((tile, D_FF), lambda i: (i, 0)),
                  pl.BlockSpec((tile, 2 * D), lambda i: (i, 0))],
        out_specs=pl.BlockSpec((N_DEV, tile, L1_W), lambda i: (0, i, 0)),
        out_shape=_sds((N_DEV, D, L1_W), BF16),
        compiler_params=_cp(1),
    )(dwup_a, dwup_b, dwxkv)


def slots_l23(dwo, dwxq, dwxo, dwdn, dwpa, dwpr):
    def body(o_ref, xq_ref, xo_ref, dn_ref, pa_ref, pr_ref, l2_ref, l3_ref):
        l2_ref[L2_OUT:L2_XQ, :] = o_ref[...].astype(BF16)
        l2_ref[L2_XQ:L2_XO, :] = xq_ref[...].astype(BF16)
        l2_ref[L2_XO:L2_DN, :] = xo_ref[...].astype(BF16)
        l2_ref[L2_DN:L2_ROWS, :] = dn_ref[...].astype(BF16)
        for h in range(MLA_H):
            l3_ref[h * MLA_V:(h + 1) * MLA_V, :] = pa_ref[h * HP:h * HP + MLA_V, :].astype(BF16)
        l3_ref[L3_PRET:L3_ROWS, :] = pr_ref[...].astype(BF16)

    rows128 = pl.BlockSpec((128, D), lambda j: (j, 0))

    def cols(r):
        return pl.BlockSpec((r, 128), lambda j: (0, j))

    return pl.pallas_call(
        body,
        name="slots_l23",
        grid=(N_DEV,),
        in_specs=[rows128, rows128, rows128, pl.BlockSpec((RS_DN, D), lambda j: (j, 0)),
                  cols(MLA_H * HP), cols(RET_H * RET_D)],
        out_specs=[pl.BlockSpec((None, L2_ROWS, D), lambda j: (j, 0, 0)),
                   pl.BlockSpec((None, L3_ROWS, HP), lambda j: (j, 0, 0))],
        out_shape=[_sds((N_DEV, L2_ROWS, D), BF16), _sds((N_DEV, L3_ROWS, HP), BF16)],
        compiler_params=_cp(1),
    )(dwo, dwxq, dwxo, dwdn, dwpa, dwpr)


def slots_early(dwz, dwk, dwv, dwq, *, tile):
    def body(dz_ref, k_ref, v_ref, q_ref, s1_ref, s2_ref):
        for os_, ln_, ds_ in WZ_RUNS:
            for j, off, d, ln in _pieces(os_, ln_, ds_, NS_IN):
                s1_ref[j, :, off:off + ln] = dz_ref[:, d:d + ln].astype(BF16)

        @pl.when(pl.program_id(0) == 0)
        def _():
            for j in range(N_DEV):
                s2_ref[j, 0:E2_UQ, 0:MLA_NOPE] = k_ref[:, j * HP:j * HP + MLA_NOPE].astype(BF16)
                s2_ref[j, 0:E2_UQ, MLA_NOPE:HP] = v_ref[:, j * HP:j * HP + MLA_V].astype(BF16)
                s2_ref[j, E2_UQ:E2_ROWS, 0:NS_UQ] = q_ref[:, j * HP:j * HP + NS_UQ].astype(BF16)
                s2_ref[j, E2_UQ:E2_ROWS, NS_UQ:HP] = jnp.zeros((E2_ROWS - E2_UQ, HP - NS_UQ), BF16)

    def whole(r):
        return pl.BlockSpec((r, MLA_H * HP), lambda i: (0, 0))

    return pl.pallas_call(
        body,
        name="slots_early",
        grid=(D // tile,),
        in_specs=[pl.BlockSpec((tile, ZW), lambda i: (i, 0)), whole(MLA_KVR), whole(MLA_KVR), whole(MLA_QR)],
        out_specs=[pl.BlockSpec((N_DEV, tile, NS_IN), lambda i: (0, i, 0)),
                   pl.BlockSpec((N_DEV, E2_ROWS, HP), lambda i: (0, 0, 0))],
        out_shape=[_sds((N_DEV, D, NS_IN), BF16), _sds((N_DEV, E2_ROWS, HP), BF16)],
        compiler_params=_cp(1),
    )(dwz, dwk, dwv, dwq)


SMALL = (("g_mix", 1024, 0), ("b_gate", 2048, 1), ("g_q_lat", 256, 3), ("g_kv_lat", 128, 4), ("g_ret", 512, 5),
         ("g_cross", 1024, 6), ("g_mem", 1024, 7), ("g_ffn", 1024, 8), ("b_conv", 5632, 9), ("g_final", 1024, 15))
SMALL_DIRECT = tuple(s for s in SMALL if s[0] != "b_conv")
S_ROWS = 16
LOSS_ROW, LOSS_LANE = 4, 128


def _flat_pieces(n, row0):
    return [(row0 + k // D, k, min(D, n - k)) for k in range(0, n, D)]


def exchange_small(cs_a, cs_b, loss_part, smalls):
    ns = len(smalls)

    def body(*refs):
        ca_ref, cb_ref, loss_ref = refs[:3]
        small_refs = refs[3:3 + ns]
        rd_ref, rs_ref, dsl_ref, own_ref = refs[3 + ns:7 + ns]
        sems = refs[7 + ns:]
        own_ref[...] = jnp.zeros(own_ref.shape, F32)
        own_ref[LOSS_ROW:LOSS_ROW + 1, LOSS_LANE:LOSS_LANE + HP] = loss_ref[0:1, :]
        for (name, n, row0), g_ref in zip(SMALL_DIRECT, small_refs):
            for r, c0, ln in _flat_pieces(n, row0):
                own_ref[r:r + 1, 0:ln] = g_ref[:, c0:c0 + ln]
        row0 = dict((s[0], s[2]) for s in SMALL)["b_conv"]
        for half, c_ref in enumerate((ca_ref, cb_ref)):
            k = half * D_FF
            end = k + D_FF
            while k < end:
                r, lane = row0 + k // D, k % D
                ln = min(D - lane, end - k)
                own_ref[r:r + 1, lane:lane + ln] = c_ref[3:4, k - half * D_FF:k - half * D_FF + ln]
                k += ln
        dsl_ref[...] = jnp.zeros(dsl_ref.shape, F32)
        per_half = D_FF // NS_UP
        for j in range(N_DEV):
            c_ref = ca_ref if j < per_half else cb_ref
            c0 = (j % per_half) * NS_UP
            dsl_ref[j, 0:3, 0:NS_UP] = c_ref[0:3, c0:c0 + NS_UP]
        comm = ([(dsl_ref, True), (own_ref, False)], [rd_ref, rs_ref]) + tuple(sems)
        _exchange_start(*comm)
        _exchange_wait(*comm)

    n1 = N_DEV - 1
    return pl.pallas_call(
        body,
        name="exchange_small",
        in_specs=[VM] * (3 + ns),
        out_specs=[VM, VM, VM, VM],
        out_shape=[_sds((n1,) + L4_SHAPE, F32), _sds((n1, S_ROWS, D), F32), _sds((N_DEV,) + L4_SHAPE, F32),
                   _sds((S_ROWS, D), F32)],
        scratch_shapes=_exchange_scratch(2),
        compiler_params=_cp0(),
    )(cs_a, cs_b, loss_part, *smalls)


def _adamw(w, g, m, v):
    m = ADAM_B1 * m + (1.0 - ADAM_B1) * g
    v = ADAM_B2 * v + (1.0 - ADAM_B2) * (g * g)
    m_hat = m / (1.0 - ADAM_B1 ** ADAM_STEP)
    v_hat = v / (1.0 - ADAM_B2 ** ADAM_STEP)
    delta = -ADAM_LR * (m_hat / (jnp.sqrt(v_hat) + ADAM_EPS) + ADAM_WD * w)
    return delta, m, v


def _apply(g, refs, outs):
    d, mn, vn = _adamw(refs[0][...], g, refs[1][...], refs[2][...])
    outs[0][...] = g
    outs[1][...] = d
    outs[2][...] = mn
    outs[3][...] = vn


def adam_cols(own, recv, wmv, spans, *, name, tile):
    R, W = own.shape
    nw = len(spans)

    def body(*refs):
        own_ref, recv_ref = refs[:2]
        ins, outs = refs[2:2 + 3 * nw], refs[2 + 3 * nw:]
        g = own_ref[...].astype(F32)
        for k in range(N_DEV - 1):
            g = g + recv_ref[k].astype(F32)
        for t, (lo, hi) in enumerate(spans):
            _apply(g[:, lo:hi], ins[3 * t:3 * t + 3], outs[4 * t:4 * t + 4])

    def blk(w):
        return pl.BlockSpec((tile, w), lambda i: (i, 0))

    widths = [hi - lo for lo, hi in spans]
    return pl.pallas_call(
        body,
        name=name,
        grid=(R // tile,),
        in_specs=[blk(W), pl.BlockSpec((N_DEV - 1, tile, W), lambda i: (0, i, 0))] + [blk(w) for w in widths for _ in range(3)],
        out_specs=[blk(w) for w in widths for _ in range(4)],
        out_shape=[_sds((R, w), F32) for w in widths for _ in range(4)],
        compiler_params=_cp(1),
    )(own, recv, *wmv)


def adam_rows(own, recv, wmv, spans, *, name):
    nw = len(spans)

    def body(*refs):
        own_ref, recv_ref = refs[:2]
        ins, outs = refs[2:2 + 3 * nw], refs[2 + 3 * nw:]
        for t, (lo, hi, w) in enumerate(spans):
            g = own_ref[lo:hi, :].astype(F32)
            for k in range(N_DEV - 1):
                g = g + recv_ref[k, lo:hi, :].astype(F32)
            _apply(g[:, 0:w], ins[3 * t:3 * t + 3], outs[4 * t:4 * t + 4])

    return pl.pallas_call(
        body,
        name=name,
        in_specs=[VM] * (2 + 3 * nw),
        out_specs=[VM] * (4 * nw),
        out_shape=[_sds((hi - lo, w), F32) for lo, hi, w in spans for _ in range(4)],
        compiler_params=_cp0(),
    )(own, recv, *wmv)


def adam_small(own_s, recv_s, dslots, recv_d, wmv_small, wmv_conv):
    ns = len(SMALL)

    def body(*refs):
        own_ref, rs_ref, dsl_ref, rd_ref = refs[:4]
        ins = refs[4:4 + 3 * ns + 3]
        outs = refs[4 + 3 * ns + 3:4 + 3 * ns + 3 + 4 * ns + 4]
        loss_ref, all_sc = refs[-2], refs[-1]
        me = _my_id()
        all_sc[0] = own_ref[...]
        for k in range(N_DEV - 1):
            all_sc[k + 1] = rs_ref[k]
        g = all_sc[jnp.bitwise_xor(me, 0)]
        for s in range(1, N_DEV):
            g = g + all_sc[jnp.bitwise_xor(me, s)]
        all_sc[0] = g
        loss_ref[...] = all_sc[0, LOSS_ROW:LOSS_ROW + 1, LOSS_LANE:LOSS_LANE + HP]
        for t, (name, n, row0) in enumerate(SMALL):
            pieces = [all_sc[0, r:r + 1, 0:ln] for r, _, ln in _flat_pieces(n, row0)]
            gt = pieces[0] if len(pieces) == 1 else jnp.concatenate(pieces, axis=1)
            _apply(gt, ins[3 * t:3 * t + 3], outs[4 * t:4 * t + 4])
        gc = dsl_ref[me]
        for k in range(N_DEV - 1):
            gc = gc + rd_ref[k]
        _apply(gc[0:3, 0:NS_UP], ins[3 * ns:3 * ns + 3], outs[4 * ns:4 * ns + 4])

    out_shape = ([_sds((1, n), F32) for _, n, _ in SMALL for _ in range(4)] + [_sds((3, NS_UP), F32)] * 4
                 + [_sds((1, HP), F32)])
    return pl.pallas_call(
        body,
        name="adam_small",
        in_specs=[VM] * (4 + 3 * ns + 3),
        out_specs=[VM] * len(out_shape),
        out_shape=out_shape,
        scratch_shapes=[pltpu.VMEM((N_DEV, S_ROWS, D), F32)],
    )(own_s, recv_s, dslots, recv_d, *wmv_small, *wmv_conv)


def kernel(x, mem, positions, g_mix, w_in, b_gate, g_q_lat, w_uq, g_kv_lat, w_ukv, w_proj_mla, g_ret, w_proj_ret, w_out, g_cross, g_mem, w_xq, w_xkv, w_xo, g_ffn, w_up, w_conv, b_conv, w_down, g_final, loss_target, m_g_mix, m_w_in, m_b_gate, m_g_q_lat, m_w_uq, m_g_kv_lat, m_w_ukv, m_w_proj_mla, m_g_ret, m_w_proj_ret, m_w_out, m_g_cross, m_g_mem, m_w_xq, m_w_xkv, m_w_xo, m_g_ffn, m_w_up, m_w_conv, m_b_conv, m_w_down, m_g_final, v_g_mix, v_w_in, v_b_gate, v_g_q_lat, v_w_uq, v_g_kv_lat, v_w_ukv, v_w_proj_mla, v_g_ret, v_w_proj_ret, v_w_out, v_g_cross, v_g_mem, v_w_xq, v_w_xkv, v_w_xo, v_g_ffn, v_w_up, v_w_conv, v_b_conv, v_w_down, v_g_final):
    args = dict(locals())
    T = x.shape[1]
    M = mem.shape[1]
    tile = min(256, T)
    tile2 = min(512, T)
    tile4 = min(1024, T)
    tq = min(512, T)
    rb = min(1024, T)

    xs = x[0]
    tgt = loss_target[0]
    mems = mem[0]

    def shard(name, prefix=""):
        a = args[prefix + name]
        return a.reshape(a.shape[-2:]) if a.ndim >= 2 else a.reshape(1, -1)

    e1, e2 = pack_early(shard("w_in"), shard("w_ukv"), shard("w_uq"))
    late_parts = pack_late(shard("w_up"), shard("w_xkv"), shard("w_out"), shard("w_xq"), shard("w_xo"), shard("w_down"),
                           shard("w_proj_mla"), shard("w_proj_ret"), shard("w_conv"))

    pos_f = jnp.broadcast_to(positions[0].astype(F32)[:, None], (T, 128))
    inv_m, inv_r = _rot_inv()
    u, cm, s1, s2, cr, sr, ge1, ge2 = rowwise(
        lambda xv, p, im, ir, g: (_rms(xv, g),) + _rot_tables(p, im, ir), [(xs, None), (pos_f, None)],
        [inv_m, inv_r, g_mix], [(D, BF16)] + [(128, F32)] * 5, [], tile=tile2, name="norm_mix_tables", gather=[e1, e2])
    wz, wk, wv, wq = assemble_early(ge1, ge2, tile=tile)
    rconsts = _ret_consts()

    z = matmul(u, wz, name="mm_z", tm=2048, tn=1536)

    def mixers_in(zl, zr, cmv, s1v, s2v, crv, srv, gq, gkv, wqv, wkv_, wvv):
        cq = _rms(zl[:, 0:256], gq).astype(BF16)
        ckv = _rms(zl[:, 256:384], gkv).astype(BF16)
        qv = _rot_mla(_dot(cq, wqv), cmv, s1v, s2v)
        kr = _rot_mla(zl[:, 384:512], cmv, s1v, s2v)
        kn = _dot(ckv, wkv_)
        kv_ = jnp.concatenate([kn[:, h * HP:(h + 1) * HP] + kr for h in range(MLA_H)], axis=1)
        vv = _dot(ckv, wvv)
        lane = lax.broadcasted_iota(jnp.int32, vv.shape, 1)
        vv = jnp.where((lane & (HP - 1)) == ONE_LANE, 1.0, vv)
        rqv = _rot_ret(zr[:, 0:512], crv, srv)
        rkv = _rot_ret(zr[:, 512:1024], crv, srv) * (RET_D ** -0.5)
        return qv, kv_, vv, rqv, rkv, zr[:, 1024:1536]

    q_a, k_a, v_a, rq, rk, rv = rowwise(
        mixers_in, [(z, (512, 8)), (z, (2048, 0)), (cm, None), (s1, None), (s2, None), (cr, None), (sr, None)],
        [g_q_lat, g_kv_lat, wq, wk, wv], [(MLA_H * HP, BF16)] * 3 + [(512, BF16)] * 3, [], tile=tile2, name="mixers_in")

    o_a, lse_row, gl1, gl2, gl3, gl4 = mla_fwd(q_a, k_a, v_a, list(late_parts), tq=tq)
    wup, wxkv = assemble_l1(gl1, tile=tile)
    wo, wxq, wxo, wdn, wpa, wpr, wcv = assemble_l234(gl2, gl3, gl4)
    ret, rstate = ret_fwd(rq, rk, rv, rconsts, rb=rb)

    def gn_parts(r):
        outs = []
        for h in range(RET_H):
            rh = r[:, h * RET_D:(h + 1) * RET_D]
            mu = jnp.mean(rh, axis=-1, keepdims=True)
            dlt = rh - mu
            rstd = lax.rsqrt(jnp.mean(dlt * dlt, axis=-1, keepdims=True) + EPS)
            outs.append((dlt * rstd, rstd))
        return outs

    def mix_fwd(ov, rv_, rg, gt, wpav, wprv, gr, bg):
        ya = _dot(ov, wpav)
        xh = jnp.concatenate([p[0] for p in gn_parts(rv_)], axis=1)
        t = rg * _sigmoid(rg) * (xh * gr)
        yr = _dot(t.astype(BF16), wprv)
        ga_ = _sigmoid(gt[:, :D] + bg[:, :D])
        gr_ = _sigmoid(gt[:, D:] + bg[:, D:])
        return ga_ * ya + gr_ * yr

    mix = rowwise(mix_fwd, [(o_a, None), (ret, None), (z, (512, 3)), (z, (2048, 1))], [wpa, wpr, g_ret, b_gate],
                  [(D, BF16)], [], tile=tile2, name="mix_fwd")[0]
    def proj_norm(av, rv_, wv_, g):
        hv = rv_ + _dot(av, wv_)
        return hv, _rms(hv, g)

    h1, n2 = rowwise(proj_norm, [(mix, None), (xs, None)], [wo, g_cross], [(D, F32), (D, BF16)], [], tile=tile4,
                     name="mm_out_norm")
    xq = matmul(n2, wxq, out_dtype=BF16, name="mm_xq")
    mn = rowwise(lambda mv_, g: _rms(mv_, g), [(mems, None)], [g_mem], [(D, BF16)], [], tile=min(tile, M), name="norm_mem")[0]
    mkv = matmul(mn, wxkv, out_dtype=BF16, name="mm_mkv")

    x_scale = X_HD ** -0.5

    def xattn_fwd(xqv, mkvv):
        outs = []
        for h in range(X_H):
            sl = slice(h * X_HD, (h + 1) * X_HD)
            s = _dot_nt(xqv[:, sl], mkvv[:, sl]) * x_scale
            s = s - jnp.max(s, axis=-1, keepdims=True)
            e = jnp.exp(s)
            p = e / jnp.sum(e, axis=-1, keepdims=True)
            outs.append(_dot(p.astype(BF16), mkvv[:, D + h * X_HD:D + (h + 1) * X_HD]))
        return jnp.concatenate(outs, axis=1)

    xo = rowwise(xattn_fwd, [(xq, None)], [mkv], [(D, BF16)], [], tile=tile4, name="xattn_fwd")[0]
    h2, n3 = rowwise(proj_norm, [(xo, None), (h1, None)], [wxo, g_ffn], [(D, F32), (D, BF16)], [], tile=tile4,
                     name="mm_xo_norm")
    up_pre = matmul(n3, wup, out_dtype=BF16, name="mm_up", tm=2048, tn=1408)
    cw = D_FF // 2
    act = conv_act_fwd(up_pre, wcv, b_conv, tile=tile, cw=cw)

    def down_loss(av, hv2, tv, wv_, g):
        hv = hv2 + _dot(av, wv_)
        y = _rms(hv, g)
        err = y - tv
        part = 0.5 * jnp.sum(jnp.sum(err * err, axis=-1, keepdims=True) / D, axis=0, keepdims=True)
        dx, dg = _rms_bwd(hv, g, err / D)
        return dx, dg, jnp.broadcast_to(part, (8, 128))

    g_fin2 = g_final.reshape(1, D)
    dh3, dg_final, loss_acc = rowwise(down_loss, [(act, None), (h2, None), (tgt, None)], [wdn, g_fin2], [(D, F32)],
                                      [((1, D), F32), ((8, 128), F32)], tile=tile2, name="mm_down_loss")

    dact = matmul(dh3, wdn, tb=True, out_dtype=BF16, name="mm_dact", tm=2048, tn=1408)
    dw_down = matmul_tn(act, dh3, name="mm_dw_down", tm=1408, tk=1024)
    dup_a, dup_b, cs_a, cs_b = conv_act_bwd(up_pre, dact, wcv, b_conv, tile=tile, cw=cw)
    dw_up_a = matmul_tn(n3, dup_a, name="mm_dw_up_a", tn=1408)
    dw_up_b = matmul_tn(n3, dup_b, name="mm_dw_up_b", tn=1408)

    def ffn_in_bwd(da_, db_, hv, drv, wv_, g):
        dn = _dot_nt(da_, wv_[:, :D_FF]) + _dot_nt(db_, wv_[:, D_FF:])
        dx, dg = _rms_bwd(hv, g, dn)
        return dx + drv, dg

    dh2, dg_ffn = rowwise(ffn_in_bwd, [(dup_a, None), (dup_b, None), (h2, None), (dh3, None)], [wup, g_ffn],
                          [(D, F32)], [((1, D), F32)], tile=tile2, name="mm_dn3_norm_bwd")
    dxo = matmul(dh2, wxo, tb=True, out_dtype=BF16, name="mm_dxo")
    dw_xo = matmul_tn(xo, dh2, name="mm_dw_xo")

    def xattn_bwd(xqv, dxov, mkvv):
        dxq, dmk, dmv = [], [], []
        for h in range(X_H):
            sl = slice(h * X_HD, (h + 1) * X_HD)
            slv = slice(D + h * X_HD, D + (h + 1) * X_HD)
            s = _dot_nt(xqv[:, sl], mkvv[:, sl]) * x_scale
            s = s - jnp.max(s, axis=-1, keepdims=True)
            e = jnp.exp(s)
            p = e / jnp.sum(e, axis=-1, keepdims=True)
            dp = _dot_nt(dxov[:, sl], mkvv[:, slv])
            ds = (p * (dp - jnp.sum(dp * p, axis=-1, keepdims=True)) * x_scale).astype(BF16)
            dxq.append(_dot(ds, mkvv[:, sl]))
            dmk.append(_dot_tn(ds, xqv[:, sl]))
            dmv.append(_dot_tn(p.astype(BF16), dxov[:, sl]))
        return jnp.concatenate(dxq, axis=1), jnp.concatenate(dmk + dmv, axis=1)

    dxq, dmkv = rowwise(xattn_bwd, [(xq, None), (dxo, None)], [mkv], [(D, BF16)], [((M, 2 * D), F32)],
                        tile=tile4, name="xattn_bwd")
    dw_xq = matmul_tn(n2, dxq, name="mm_dw_xq")

    def proj_norm_bwd(dyv, hv, drv, wv_, g):
        dx, dg = _rms_bwd(hv, g, _dot_nt(dyv, wv_))
        return dx + drv, dg

    dh1, dg_cross = rowwise(proj_norm_bwd, [(dxq, None), (h1, None), (dh2, None)], [wxq, g_cross], [(D, F32)],
                            [((1, D), F32)], tile=tile4, name="mm_dn2_norm_bwd")
    dw_xkv = matmul_tn(mn, dmkv, name="mm_dw_xkv", tk=M)
    dmn = matmul(dmkv, wxkv, tb=True, name="mm_dmn", tm=M)
    dg_mem = rowwise(lambda mv_, dyv, g: _rms_bwd(mv_, g, dyv)[1], [(mems, None), (dmn, None)], [g_mem], [],
                     [((1, D), F32)], tile=min(tile, M), name="norm_mem_bwd")[0]

    dmix = matmul(dh1, wo, tb=True, out_dtype=BF16, name="mm_dmix")
    dw_out = matmul_tn(mix, dh1, name="mm_dw_out")

    def mix_bwd(ov, rv_, rg, gt, dmv_, wpav, wprv, gr, bg):
        dm_ = dmv_.astype(F32)
        ya = _dot(ov, wpav)
        parts = gn_parts(rv_)
        xh = jnp.concatenate([p[0] for p in parts], axis=1)
        yn = xh * gr
        sg = _sigmoid(rg)
        sl_ = rg * sg
        t = (sl_ * yn).astype(BF16)
        yr = _dot(t, wprv)
        ga_ = _sigmoid(gt[:, :D] + bg[:, :D])
        gr_ = _sigmoid(gt[:, D:] + bg[:, D:])
        dgates = jnp.concatenate([dm_ * ya * ga_ * (1.0 - ga_), dm_ * yr * gr_ * (1.0 - gr_)], axis=1)
        dya = (dm_ * ga_).astype(BF16)
        dyr = (dm_ * gr_).astype(BF16)
        do_ = _dot_nt(dya, wpav)
        dwpa_ = _dot_tn(ov, dya)
        dt = _dot_nt(dyr, wprv)
        dwpr_ = _dot_tn(t, dyr)
        drg_ = dt * yn * (sg * (1.0 + rg * (1.0 - sg)))
        dyn = dt * sl_
        dgr = jnp.sum(dyn * xh, axis=0, keepdims=True)
        dxh = dyn * gr
        drets = []
        for h in range(RET_H):
            sl = slice(h * RET_D, (h + 1) * RET_D)
            xhh, rstd = parts[h]
            dxhh = dxh[:, sl]
            drets.append(rstd * (dxhh - jnp.mean(dxhh, axis=-1, keepdims=True)
                                 - xhh * jnp.mean(dxhh * xhh, axis=-1, keepdims=True)))
        dret_ = jnp.concatenate(drets, axis=1)
        dbg = jnp.sum(dgates, axis=0, keepdims=True)
        prod = ov.astype(F32) * do_.astype(BF16).astype(F32)
        lane = lax.broadcasted_iota(jnp.int32, (prod.shape[0], HP), 1)
        dlt = jnp.zeros((prod.shape[0], HP), F32)
        for h in range(MLA_H):
            dlt = jnp.where(lane == h, jnp.sum(prod[:, h * HP:(h + 1) * HP], axis=-1, keepdims=True), dlt)
        return do_, dret_, drg_, dgates, dlt, dwpa_, dwpr_, dgr, dbg

    do_a, dret, drg, dgates, delta, dwpa, dw_proj_ret, dg_ret, db_gate = rowwise(
        mix_bwd, [(o_a, None), (ret, None), (z, (512, 3)), (z, (2048, 1)), (dmix, None)], [wpa, wpr, g_ret, b_gate],
        [(MLA_H * HP, BF16), (512, F32), (512, BF16), (2 * D, BF16), (HP, F32)],
        [((MLA_H * HP, D), F32), ((512, D), F32), ((1, 512), F32), ((1, 2 * D), F32)], tile=tile2, name="mix_bwd")

    sl1 = slots_l1(dw_up_a, dw_up_b, dw_xkv, tile=tile)
    sl2, sl3 = slots_l23(dw_out, dw_xq, dw_xo, dw_down, dwpa, dw_proj_ret)
    delta_row = mla_prep(delta, tq=tq)
    dq_a, dk_a, dv_a, rl1, rl2, rl3 = mla_bwd(q_a, k_a, v_a, do_a, lse_row, delta_row, [sl1, sl2, sl3], tq=tq)
    drq_r, drk_r, drv = ret_bwd(rq, rk, rv, rstate, dret, rconsts, rb=rb)

    def mixers_in_bwd(zl, cmv, s1v, s2v, dqv, dkv_, dvv, drq_, drk_, drv_, drg_, dgt, crv, srv, gq, gkv, wqv, wkv_, wvv):
        cqf, ckvf = zl[:, 0:256], zl[:, 256:384]
        cq = _rms(cqf, gq).astype(BF16)
        ckv = _rms(ckvf, gkv).astype(BF16)
        dq_pre = _rot_mla(dqv.astype(F32), cmv, -s1v, -s2v).astype(BF16)
        dkf = dkv_.astype(F32)
        dkr = dkf[:, 0:HP]
        for h in range(1, MLA_H):
            dkr = dkr + dkf[:, h * HP:(h + 1) * HP]
        lane = lax.broadcasted_iota(jnp.int32, dkr.shape, 1)
        dzk = _rot_mla(jnp.where((lane >= 64) & (lane < 96), dkr, 0.0), cmv, -s1v, -s2v)
        dkb = dkv_.astype(BF16)
        dvb = dvv.astype(BF16)
        dcq_n = _dot_nt(dq_pre, wqv)
        dckv_n = _dot_nt(dkb, wkv_) + _dot_nt(dvb, wvv)
        dwq_ = _dot_tn(cq, dq_pre)
        dwk_ = _dot_tn(ckv, dkb)
        dwv_ = _dot_tn(ckv, dvb)
        dcq, dgq = _rms_bwd(cqf, gq, dcq_n)
        dckv, dgkv = _rms_bwd(ckvf, gkv, dckv_n)
        a = _rot_ret(drq_, crv, -srv)
        b = _rot_ret(drk_, crv, -srv) * (RET_D ** -0.5)
        dz_ = jnp.concatenate([a.astype(BF16), b.astype(BF16), drv_, drg_, dgt,
                               dcq.astype(BF16), dckv.astype(BF16), dzk.astype(BF16)], axis=1)
        return dz_, dwq_, dwk_, dwv_, dgq, dgkv

    dz, dwq, dwk, dwv, dg_q_lat, dg_kv_lat = rowwise(
        mixers_in_bwd, [(z, (512, 8)), (cm, None), (s1, None), (s2, None), (dq_a, None), (dk_a, None), (dv_a, None),
                        (drq_r, None), (drk_r, None), (drv, None), (drg, None), (dgates, None), (cr, None), (sr, None)],
        [g_q_lat, g_kv_lat, wq, wk, wv], [(ZW, BF16)],
        [((MLA_QR, MLA_H * HP), F32), ((MLA_KVR, MLA_H * HP), F32), ((MLA_KVR, MLA_H * HP), F32),
         ((1, MLA_QR), F32), ((1, MLA_KVR), F32)], tile=tile2, name="mixers_in_bwd")
    dwz = matmul_tn(u, dz, name="mm_dw_z", tn=1536)
    se1, se2 = slots_early(dwz, dwk, dwv, dwq, tile=tile)
    grad_x, dg_mix, re1, re2 = rowwise(proj_norm_bwd, [(dz, None), (xs, None), (dh1, None)], [wz, g_mix], [(D, F32)],
                                       [((1, D), F32)], tile=tile2, name="mm_du_norm_bwd", exchange=[se1, se2])

    small_grads = {"g_mix": dg_mix, "b_gate": db_gate, "g_q_lat": dg_q_lat, "g_kv_lat": dg_kv_lat, "g_ret": dg_ret,
                   "g_cross": dg_cross, "g_mem": dg_mem, "g_ffn": dg_ffn, "g_final": dg_final}
    rd, rs, dslots, own_s = exchange_small(cs_a, cs_b, loss_acc, [small_grads[n] for n, _, _ in SMALL_DIRECT])

    me = _my_id()

    def own(slots):
        return lax.dynamic_index_in_dim(slots, me, axis=0, keepdims=False)

    def wmv(names):
        return [shard(n, p) for n in names for p in ("", "m_", "v_")]

    names_s = tuple(n for n, _, _ in SMALL)
    small_outs = adam_small(own_s, rs, dslots, rd, wmv(names_s), wmv(("w_conv",)))
    groups = (
        (("w_in",), adam_cols(own(se1), re1, wmv(("w_in",)), ((0, NS_IN),), name="adam_e1", tile=128)),
        (("w_ukv", "w_uq"), adam_rows(own(se2), re2, wmv(("w_ukv", "w_uq")),
                                      ((0, E2_UQ, HP), (E2_UQ, E2_ROWS, NS_UQ)), name="adam_e2")),
        (("w_up", "w_xkv"), adam_cols(own(sl1), rl1, wmv(("w_up", "w_xkv")), ((0, NS_UP), (NS_UP, L1_W)),
                                      name="adam_l1", tile=128)),
        (("w_out", "w_xq", "w_xo", "w_down"),
         adam_rows(own(sl2), rl2, wmv(("w_out", "w_xq", "w_xo", "w_down")),
                   ((L2_OUT, L2_XQ, D), (L2_XQ, L2_XO, D), (L2_XO, L2_DN, D), (L2_DN, L2_ROWS, D)), name="adam_l2")),
        (("w_proj_mla", "w_proj_ret"), adam_rows(own(sl3), rl3, wmv(("w_proj_mla", "w_proj_ret")),
                                                 ((0, L3_PRET, HP), (L3_PRET, L3_ROWS, HP)), name="adam_l3")),
        (names_s + ("w_conv",), small_outs),
    )
    loss = small_outs[-1][0, 0]
    res = {}
    for names, outs_ in groups:
        for t, n in enumerate(names):
            res[n] = outs_[4 * t:4 * t + 4]

    order = ["g_mix", "w_in", "b_gate", "g_q_lat", "w_uq", "g_kv_lat", "w_ukv", "w_proj_mla", "g_ret", "w_proj_ret",
             "w_out", "g_cross", "g_mem", "w_xq", "w_xkv", "w_xo", "g_ffn", "w_up", "w_conv", "b_conv", "w_down",
             "g_final"]
    outs = [loss, grad_x[None]]
    for kind in range(4):
        outs += [res[n][kind].reshape(args[n].shape) for n in order]
    return tuple(outs)
```

```python
import functools
import math

import jax
import jax.numpy as jnp
import numpy as np
from jax import lax
from jax.experimental import pallas as pl
from jax.experimental.pallas import tpu as pltpu
from jax.experimental.pallas import tpu_sc as plsc

F32 = jnp.float32
BF16 = jnp.bfloat16

D = 1024
MLA_H, MLA_NOPE, MLA_ROPE, MLA_V = 8, 64, 32, 64
MLA_QR, MLA_KVR = 256, 128
RET_H, RET_D, RET_C = 4, 128, 128
X_H, X_HD = 4, 256
D_FF = 2816
THETA = 10000.0
EPS = 1e-6
HP = 128
ZW = 4608
N_DEV = 8

ADAM_LR, ADAM_B1, ADAM_B2, ADAM_EPS, ADAM_WD, ADAM_STEP = 0.001, 0.9, 0.999, 1e-08, 0.01, 10

VMEM_LIMIT = 56 * 1024 * 1024
BIG_CONST_BYTES = 4 * 1024 * 1024
SEQUENCER_COLLECTIVE_ID = 1
MESH = pl.DeviceIdType.MESH
VM = pl.BlockSpec(memory_space=pltpu.VMEM)
ANY = pl.BlockSpec(memory_space=pl.ANY)


def _cp(n_axes):
    return pltpu.CompilerParams(dimension_semantics=("arbitrary",) * n_axes, vmem_limit_bytes=VMEM_LIMIT)


def _cp0():
    return pltpu.CompilerParams(vmem_limit_bytes=VMEM_LIMIT)


def _pick(n, cap, mult=128):
    best = None
    for t in range(mult, min(n, cap) + 1, mult):
        if n % t == 0:
            best = t
    return best if best is not None else n


def _dot(a, b):
    return jnp.dot(a, b, preferred_element_type=F32)


def _dot_nt(a, b):
    return lax.dot_general(a, b, (((1,), (1,)), ((), ())), preferred_element_type=F32)


def _dot_tn(a, b):
    return lax.dot_general(a, b, (((0,), (0,)), ((), ())), preferred_element_type=F32)


def _sds(shape, dtype):
    return jax.ShapeDtypeStruct(shape, dtype)


def matmul(a, b, *, name, tb=False, out_dtype=F32, tm=1024, tn=1024):
    M, K = a.shape
    N = b.shape[0] if tb else b.shape[1]
    tm = _pick(M, tm, 8)
    tn = _pick(N, tn)

    def body(a_ref, b_ref, o_ref):
        av = a_ref[...].astype(BF16)
        bv = b_ref[...].astype(BF16)
        acc = _dot_nt(av, bv) if tb else _dot(av, bv)
        o_ref[...] = acc.astype(o_ref.dtype)

    return pl.pallas_call(
        body,
        name=name,
        grid=(M // tm, N // tn),
        in_specs=[pl.BlockSpec((tm, K), lambda i, j: (i, 0)),
                  pl.BlockSpec((tn, K), lambda i, j: (j, 0)) if tb else pl.BlockSpec((K, tn), lambda i, j: (0, j))],
        out_specs=pl.BlockSpec((tm, tn), lambda i, j: (i, j)),
        out_shape=_sds((M, N), out_dtype),
        compiler_params=_cp(2),
    )(a, b)


def matmul_tn(a, b, *, name, tm=1024, tn=1024, tk=2048):
    R, M = a.shape
    N = b.shape[1]
    tm = _pick(M, tm)
    tn = _pick(N, tn)
    tk = _pick(R, tk, 16)
    nk = R // tk

    def body(a_ref, b_ref, o_ref, acc_ref):
        k = pl.program_id(2)

        @pl.when(k == 0)
        def _():
            acc_ref[...] = jnp.zeros_like(acc_ref)

        acc_ref[...] += _dot_tn(a_ref[...].astype(BF16), b_ref[...].astype(BF16))

        @pl.when(k == nk - 1)
        def _():
            o_ref[...] = acc_ref[...]

    return pl.pallas_call(
        body,
        name=name,
        grid=(M // tm, N // tn, nk),
        in_specs=[pl.BlockSpec((tk, tm), lambda i, j, k: (k, i)), pl.BlockSpec((tk, tn), lambda i, j, k: (k, j))],
        out_specs=pl.BlockSpec((tm, tn), lambda i, j, k: (i, j)),
        out_shape=_sds((M, N), F32),
        scratch_shapes=[pltpu.VMEM((tm, tn), F32)],
        compiler_params=_cp(3),
    )(a, b)


def rowwise(fn, rows, consts, out_rows, out_accs, *, tile, name, gather=(), exchange=()):
    T = rows[0][0].shape[0]
    nt = T // tile
    travel = tuple(gather) + tuple(exchange)
    n_r, n_c, n_o, n_a, n_g = len(rows), len(consts), len(out_rows), len(out_accs), len(travel)
    n_in = n_r + n_c + n_g

    def body(*refs):
        if n_g:
            srcs, dsts = refs[n_r + n_c:n_in], refs[n_in + n_o + n_a:n_in + n_o + n_a + n_g]
            sems = tuple(refs[n_in + n_o + n_a + n_g:])
            if gather:
                comm = (srcs, dsts) + sems
                comm_start, comm_finish = _gather2_start, _gather2_finish
            else:
                comm = ([(r, True) for r in srcs], dsts) + sems
                comm_start, comm_finish = _exchange_start, _exchange_wait

            @pl.when(pl.program_id(0) == 0)
            def _():
                comm_start(*comm)

        ins = [r[...] for r in refs[: n_r + n_c]]
        outs = fn(*ins)
        if not isinstance(outs, (tuple, list)):
            outs = (outs,)
        o_refs = refs[n_in:n_in + n_o]
        a_refs = refs[n_in + n_o:n_in + n_o + n_a]
        for o_ref, o in zip(o_refs, outs[:n_o]):
            o_ref[...] = o.astype(o_ref.dtype)
        if n_a:
            first = pl.program_id(0) == 0

            @pl.when(first)
            def _():
                for a_ref, o in zip(a_refs, outs[n_o:]):
                    a_ref[...] = o.astype(a_ref.dtype)

            @pl.when(jnp.logical_not(first))
            def _():
                for a_ref, o in zip(a_refs, outs[n_o:]):
                    a_ref[...] += o.astype(a_ref.dtype)
        if n_g:

            @pl.when(pl.program_id(0) == nt - 1)
            def _():
                comm_finish(*comm)

    in_specs = []
    args = []
    for arr, win in rows:
        if win is None:
            in_specs.append(pl.BlockSpec((tile, arr.shape[1]), lambda i: (i, 0)))
        else:
            w, cb = win
            in_specs.append(pl.BlockSpec((tile, w), functools.partial(lambda i, cb: (i, cb), cb=cb)))
        args.append(arr)
    for c in consts:
        index_map = functools.partial(lambda i, nd: (0,) * nd, nd=c.ndim)
        if c.size * c.dtype.itemsize >= BIG_CONST_BYTES:
            in_specs.append(pl.BlockSpec(c.shape, index_map, pipeline_mode=pl.Buffered(1)))
        else:
            in_specs.append(pl.BlockSpec(c.shape, index_map))
        args.append(c)
    out_specs = [pl.BlockSpec((tile, w), lambda i: (i, 0)) for w, _ in out_rows]
    out_shape = [_sds((T, w), dt) for w, dt in out_rows]
    for shp, dt in out_accs:
        out_specs.append(pl.BlockSpec(shp, functools.partial(lambda i, nd: (0,) * nd, nd=len(shp))))
        out_shape.append(_sds(shp, dt))
    return pl.pallas_call(
        body,
        name=name,
        grid=(nt,),
        in_specs=in_specs + [ANY] * n_g,
        out_specs=out_specs + [ANY] * n_g,
        out_shape=out_shape + [_sds((N_DEV,) + p.shape, p.dtype) for p in gather]
        + [_sds((N_DEV - 1,) + s.shape[1:], s.dtype) for s in exchange],
        scratch_shapes=(_gather_scratch(n_g) if gather else _exchange_scratch(n_g)) if n_g else [],
        compiler_params=_cp(1),
    )(*args, *travel)


def _rms(x, g):
    r = lax.rsqrt(jnp.mean(x * x, axis=-1, keepdims=True) + EPS)
    return x * r * g


def _rms_bwd(x, g, dy):
    r = lax.rsqrt(jnp.mean(x * x, axis=-1, keepdims=True) + EPS)
    xh = x * r
    dg = jnp.sum(dy * xh, axis=0, keepdims=True)
    dxh = dy * g
    dx = r * (dxh - xh * jnp.mean(dxh * xh, axis=-1, keepdims=True))
    return dx, dg


def _sigmoid(x):
    return 0.5 * jnp.tanh(0.5 * x) + 0.5


def _rot_mla(x, c, s1, s2):
    n = x.shape[1] // HP
    outs = []
    for h in range(n):
        xh = x[:, h * HP : (h + 1) * HP]
        outs.append(xh * c + pltpu.roll(xh, HP - 16, 1) * s1 + pltpu.roll(xh, 16, 1) * s2)
    return outs[0] if n == 1 else jnp.concatenate(outs, axis=1)


def _rot_ret(x, c, s):
    n = x.shape[1] // RET_D
    outs = []
    for h in range(n):
        xh = x[:, h * RET_D : (h + 1) * RET_D]
        outs.append(xh * c + pltpu.roll(xh, RET_D // 2, 1) * s)
    return outs[0] if n == 1 else jnp.concatenate(outs, axis=1)


def _rot_inv():
    lane_np = np.arange(128)
    inv_m = (jnp.asarray(THETA, F32) ** (-jnp.asarray(lane_np & 15, F32) / 16.0)).reshape(1, 128)
    inv_r = (jnp.asarray(THETA, F32) ** (-jnp.asarray(lane_np & 63, F32) / 64.0)).reshape(1, 128)
    return inv_m, inv_r


def _rot_tables(p, im, ir):
    lane = lax.broadcasted_iota(jnp.int32, p.shape, 1)
    ang = p * im
    cm = jnp.where((lane >= 64) & (lane < 96), jnp.cos(ang), 1.0)
    sn = jnp.sin(ang)
    s1 = jnp.where((lane >= 64) & (lane < 80), -sn, 0.0)
    s2 = jnp.where((lane >= 80) & (lane < 96), sn, 0.0)
    angr = p * ir
    snr = jnp.sin(angr)
    return cm, s1, s2, jnp.cos(angr), jnp.where(lane < 64, -snr, snr)


def _peer(m):
    x, y, c = lax.axis_index("x"), lax.axis_index("y"), lax.axis_index("c")
    mx, my, mc = (m >> 2) & 1, (m >> 1) & 1, m & 1
    px = 1 - x if mx else x
    py = 1 - y if my else y
    pc = 1 - c if mc else c
    return (px, py, pc), 4 * px + 2 * py + pc


def _my_id():
    return 4 * lax.axis_index("x") + 2 * lax.axis_index("y") + lax.axis_index("c")


def _gather_copies(srcs, outs, send_sems, recv_sems, local_sems, arriving=False):
    me = _my_id()
    copies = []
    if not arriving:
        for g, (src, out) in enumerate(zip(srcs, outs)):
            copies.append((pltpu.make_async_copy(src, out.at[me], local_sems.at[g]), False))
    for m in range(1, N_DEV):
        peer, plin = _peer(m)
        for g, (src, out) in enumerate(zip(srcs, outs)):
            copies.append((pltpu.make_async_remote_copy(
                src_ref=src, dst_ref=out.at[plin if arriving else me], send_sem=send_sems.at[g, m - 1],
                recv_sem=recv_sems.at[g, m - 1], device_id=peer, device_id_type=MESH), True))
    return copies


def _gather_start(*a):
    for cp, _ in _gather_copies(*a):
        cp.start()


def _gather_wait(*a):
    for cp, _ in _gather_copies(*a, arriving=True):
        cp.wait_recv()
    for cp, remote in _gather_copies(*a):
        if remote:
            cp.wait_send()
        else:
            cp.wait()


CHIP_RELATIONS = (4, 2, 6)


def _gather2_copy(src, out, block, to, send_sems, recv_sems, g, k):
    return pltpu.make_async_remote_copy(src_ref=src, dst_ref=out.at[block], send_sem=send_sems.at[g, k],
                                        recv_sem=recv_sems.at[g, k], device_id=to, device_id_type=MESH)


def _gather2_start(srcs, outs, send_sems, recv_sems, local_sems):
    me = _my_id()
    sib, _ = _peer(1)
    for g, (src, out) in enumerate(zip(srcs, outs)):
        pltpu.make_async_copy(src, out.at[me], local_sems.at[g]).start()
        _gather2_copy(src, out, me, sib, send_sems, recv_sems, g, 0).start()
        for t, m in enumerate(CHIP_RELATIONS):
            _gather2_copy(src, out, me, _peer(m)[0], send_sems, recv_sems, g, 1 + t).start()


def _gather2_finish(srcs, outs, send_sems, recv_sems, local_sems):
    me = _my_id()
    sib, sib_lin = _peer(1)
    for t, m in enumerate(CHIP_RELATIONS):
        peer, plin = _peer(m)
        for g, (src, out) in enumerate(zip(srcs, outs)):
            _gather2_copy(src, out, plin, peer, send_sems, recv_sems, g, 1 + t).wait_recv()
            _gather2_copy(out.at[plin], out, plin, sib, send_sems, recv_sems, g, 4 + t).start()
    for g, (src, out) in enumerate(zip(srcs, outs)):
        _gather2_copy(src, out, sib_lin, sib, send_sems, recv_sems, g, 0).wait_recv()
        for t, m in enumerate(CHIP_RELATIONS):
            _gather2_copy(src, out, _peer(m | 1)[1], sib, send_sems, recv_sems, g, 4 + t).wait_recv()
    for g, (src, out) in enumerate(zip(srcs, outs)):
        _gather2_copy(src, out, me, sib, send_sems, recv_sems, g, 0).wait_send()
        for t, m in enumerate(CHIP_RELATIONS):
            peer, plin = _peer(m)
            _gather2_copy(src, out, me, peer, send_sems, recv_sems, g, 1 + t).wait_send()
            _gather2_copy(out.at[plin], out, plin, sib, send_sems, recv_sems, g, 4 + t).wait_send()
        pltpu.make_async_copy(src, out.at[me], local_sems.at[g]).wait()


def _gather_scratch(n):
    return [pltpu.SemaphoreType.DMA((n, N_DEV - 1)), pltpu.SemaphoreType.DMA((n, N_DEV - 1)), pltpu.SemaphoreType.DMA((n,))]


def _exchange_copies(srcs, dsts, send_sems, recv_sems):
    copies = []
    for m in range(1, N_DEV):
        peer, plin = _peer(m)
        for g, ((src, per_peer), dst) in enumerate(zip(srcs, dsts)):
            copies.append(pltpu.make_async_remote_copy(
                src_ref=src.at[plin] if per_peer else src, dst_ref=dst.at[m - 1], send_sem=send_sems.at[g, m - 1],
                recv_sem=recv_sems.at[g, m - 1], device_id=peer, device_id_type=MESH))
    return copies


def _exchange_start(*a):
    for cp in _exchange_copies(*a):
        cp.start()


def _exchange_wait(*a):
    copies = _exchange_copies(*a)
    for cp in copies:
        cp.wait_recv()
    for cp in copies:
        cp.wait_send()


def _exchange_scratch(n):
    return [pltpu.SemaphoreType.DMA((n, N_DEV - 1)), pltpu.SemaphoreType.DMA((n, N_DEV - 1))]


def sequencer_exchange(slots):
    n = len(slots)
    src_refs = [jax.new_ref(s, memory_space=pltpu.MemorySpace.HBM) for s in slots]
    got_refs = [jax.empty_ref(_sds((N_DEV - 1,) + s.shape[1:], s.dtype), memory_space=pltpu.MemorySpace.HBM)
                for s in slots]

    @pl.kernel(mesh=plsc.ScalarSubcoreMesh(axis_name="sequencer", num_cores=1), name="sequencer_exchange",
               scratch_types=tuple(pltpu.SemaphoreType.DMA for _ in range(2 * n * (N_DEV - 1))),
               compiler_params=pltpu.CompilerParams(collective_id=SEQUENCER_COLLECTIVE_ID))
    def launch(*sems):
        barrier = pltpu.get_barrier_semaphore()
        for m in range(1, N_DEV):
            pl.semaphore_signal(barrier, inc=1, device_id=_peer(m)[0], device_id_type=MESH)
        pl.semaphore_wait(barrier, N_DEV - 1)
        copies = []
        for m in range(1, N_DEV):
            peer, plin = _peer(m)
            for g in range(n):
                k = 2 * (g * (N_DEV - 1) + m - 1)
                copies.append(pltpu.make_async_remote_copy(
                    src_ref=src_refs[g].at[plin], dst_ref=got_refs[g].at[m - 1], send_sem=sems[k], recv_sem=sems[k + 1],
                    device_id=peer, device_id_type=MESH))
        for cp in copies:
            cp.start()
        for cp in copies:
            cp.wait_recv()
        for cp in copies:
            cp.wait_send()

    launch()
    return [g[...] for g in got_refs]


MLA_SCALE = (MLA_NOPE + MLA_ROPE) ** -0.5
MLA_C2 = MLA_SCALE * math.log2(math.e)
ONE_LANE = MLA_V
NEG = -1e30
HPS = 4
HPB = 4


def _tri_mask(n, lower_rows_ge_cols=True):
    r = lax.broadcasted_iota(jnp.int32, (n, n), 0)
    c = lax.broadcasted_iota(jnp.int32, (n, n), 1)
    return r >= c if lower_rows_ge_cols else c >= r


def mla_fwd(q, k, v, gather, *, tq):
    T = q.shape[0]
    nq = T // tq
    rep = tq // HP
    ng = len(gather)

    def body(*refs):
        q_ref, k_ref, v_ref = refs[:3]
        srcs = refs[3:3 + ng]
        o_ref, lse_ref = refs[3 + ng:5 + ng]
        outs = refs[5 + ng:5 + 2 * ng]
        m_sc, acc_sc = refs[5 + 2 * ng:7 + 2 * ng]
        comm = (srcs, outs) + tuple(refs[7 + 2 * ng:])
        h, i = pl.program_id(0), pl.program_id(1)

        @pl.when((h == 0) & (i == 0))
        def _():
            _gather_start(*comm)

        m_sc[...] = jnp.full(m_sc.shape, NEG, F32)
        acc_sc[...] = jnp.zeros(acc_sc.shape, F32)
        heads = [slice(t * HP, (t + 1) * HP) for t in range(HPS)]

        def block(j, masked):
            off = pl.multiple_of(j * tq, tq)
            ss = [_dot_nt(q_ref[:, hd], k_ref[pl.ds(off, tq), hd]) * MLA_C2 for hd in heads]
            mask = _tri_mask(tq) if masked else None
            for hd, s in zip(heads, ss):
                if masked:
                    s = jnp.where(mask, s, NEG)
                m_prev = m_sc[:, hd]
                m_next = jnp.maximum(m_prev, jnp.max(s, axis=-1, keepdims=True))
                p = jnp.exp2(s - jnp.tile(m_next, (1, rep)))
                alpha = jnp.exp2(m_prev - m_next)
                acc_sc[:, hd] = alpha * acc_sc[:, hd] + _dot(p.astype(BF16), v_ref[pl.ds(off, tq), hd])
                m_sc[:, hd] = m_next

        def loop_body(p, carry):
            block(2 * p, False)
            block(2 * p + 1, False)
            return carry

        lax.fori_loop(0, i // 2, loop_body, 0)

        @pl.when(i % 2 == 1)
        def _():
            block(i - 1, False)

        block(i, True)
        lane = lax.broadcasted_iota(jnp.int32, (tq, HP), 1)
        lse_cols = jnp.zeros((tq, HP), F32)
        for t, hd in enumerate(heads):
            acc = acc_sc[:, hd]
            l = acc[:, ONE_LANE:ONE_LANE + 1]
            o_ref[:, hd] = (acc / l).astype(o_ref.dtype)
            lse_cols = jnp.where(lane == t, m_sc[:, hd] + jnp.log(l) * math.log2(math.e), lse_cols)
        lse_rows = lse_cols.T
        for t in range(HPS):
            lse_ref[t] = lse_rows[t:t + 1, :]

        @pl.when((h == MLA_H // HPS - 1) & (i == nq - 1))
        def _():
            _gather_wait(*comm)

    blk = pl.BlockSpec((tq, HPS * HP), lambda h, i: (i, h))
    full = pl.BlockSpec((T, HPS * HP), lambda h, i: (0, h))
    return pl.pallas_call(
        body,
        name="mla_fwd",
        grid=(MLA_H // HPS, nq),
        in_specs=[blk, full, full] + [ANY] * ng,
        out_specs=[blk, pl.BlockSpec((HPS, None, 1, tq), lambda h, i: (h, i, 0, 0))] + [ANY] * ng,
        out_shape=[_sds((T, MLA_H * HP), BF16), _sds((MLA_H, nq, 1, tq), F32)]
        + [_sds((N_DEV,) + p.shape, p.dtype) for p in gather],
        scratch_shapes=[pltpu.VMEM((tq, HPS * HP), F32), pltpu.VMEM((tq, HPS * HP), F32)] + _gather_scratch(ng),
        compiler_params=_cp(2),
    )(q, k, v, *gather)


def mla_prep(delta, *, tq):
    T = delta.shape[0]
    nq = T // tq

    def body(d_ref, row_ref):
        dt = d_ref[...].T
        for h in range(MLA_H):
            row_ref[h] = dt[h:h + 1, :]

    return pl.pallas_call(
        body,
        name="mla_prep",
        grid=(nq,),
        in_specs=[pl.BlockSpec((tq, HP), lambda i: (i, 0))],
        out_specs=pl.BlockSpec((MLA_H, None, 1, tq), lambda i: (0, i, 0, 0)),
        out_shape=_sds((MLA_H, nq, 1, tq), F32),
        compiler_params=_cp(1),
    )(delta)


def mla_bwd(q, k, v, do, lse_row, delta_row, slots, *, tq):
    T = q.shape[0]
    nq = T // tq
    ns = len(slots)

    def body(*refs):
        q_ref, k_ref, v_ref, do_ref, lse_ref, delta_ref = refs[:6]
        srcs = [(r, True) for r in refs[6:6 + ns]]
        dq_ref, dk_ref, dv_ref = refs[6 + ns:9 + ns]
        dsts = refs[9 + ns:9 + 2 * ns]
        dk_sc, dv_sc = refs[9 + 2 * ns:11 + 2 * ns]
        comm = (srcs, dsts) + tuple(refs[11 + 2 * ns:])
        h, j = pl.program_id(0), pl.program_id(1)

        @pl.when((h == 0) & (j == 0))
        def _():
            _exchange_start(*comm)

        dk_sc[...] = jnp.zeros(dk_sc.shape, F32)
        dv_sc[...] = jnp.zeros(dv_sc.shape, F32)
        heads = [slice(t * HP, (t + 1) * HP) for t in range(HPB)]

        @pl.when(j == 0)
        def _():
            dq_ref[...] = jnp.zeros(dq_ref.shape, F32)

        def block(i, masked):
            off = pl.multiple_of(i * tq, tq)
            sts = [_dot_nt(k_ref[:, hd], q_ref[pl.ds(off, tq), hd]) * MLA_C2 for hd in heads]
            dpts = [_dot_nt(v_ref[:, hd], do_ref[pl.ds(off, tq), hd]) for hd in heads]
            mask = _tri_mask(tq, False) if masked else None
            for t, hd in enumerate(heads):
                st = sts[t]
                if masked:
                    st = jnp.where(mask, st, NEG)
                pt = jnp.exp2(st - lse_ref[t, i])
                dv_sc[:, hd] += _dot(pt.astype(BF16), do_ref[pl.ds(off, tq), hd])
                dst = (pt * (dpts[t] - delta_ref[t, i]) * MLA_SCALE).astype(BF16)
                dk_sc[:, hd] += _dot(dst, q_ref[pl.ds(off, tq), hd])
                dq_ref[pl.ds(off, tq), hd] += _dot_tn(dst, k_ref[:, hd])

        block(j, True)

        def loop_body(p, carry):
            block(j + 1 + 2 * p, False)
            block(j + 2 + 2 * p, False)
            return carry

        rest = nq - 1 - j
        lax.fori_loop(0, rest // 2, loop_body, 0)

        @pl.when(rest % 2 == 1)
        def _():
            block(nq - 1, False)

        dk_ref[...] = dk_sc[...].astype(dk_ref.dtype)
        dv_ref[...] = dv_sc[...].astype(dv_ref.dtype)

        @pl.when((h == MLA_H // HPB - 1) & (j == nq - 1))
        def _():
            _exchange_wait(*comm)

    blk = pl.BlockSpec((tq, HPB * HP), lambda h, j: (j, h))
    full = pl.BlockSpec((T, HPB * HP), lambda h, j: (0, h), pipeline_mode=pl.Buffered(1))
    rows = pl.BlockSpec((HPB, nq, 1, tq), lambda h, j: (h, 0, 0, 0))
    return pl.pallas_call(
        body,
        name="mla_bwd",
        grid=(MLA_H // HPB, nq),
        in_specs=[full, blk, blk, full, rows, rows] + [ANY] * ns,
        out_specs=[full, blk, blk] + [ANY] * ns,
        out_shape=[_sds((T, MLA_H * HP), F32), _sds((T, MLA_H * HP), BF16), _sds((T, MLA_H * HP), BF16)]
        + [_sds((N_DEV - 1,) + s.shape[1:], s.dtype) for s in slots],
        scratch_shapes=[pltpu.VMEM((tq, HPB * HP), F32), pltpu.VMEM((tq, HPB * HP), F32)] + _exchange_scratch(ns),
        compiler_params=_cp(2),
    )(q, k, v, do, lse_row, delta_row, *slots)


def _ret_consts():
    h = jnp.arange(RET_H, dtype=F32)
    log_g = jnp.log1p(-jnp.exp2(-5.0 - h))
    idx = jnp.arange(RET_C, dtype=F32)
    rel = idx[:, None] - idx[None, :]
    dmask = jnp.where(rel >= 0, jnp.exp(log_g[:, None, None] * jnp.maximum(rel, 0.0)), 0.0)
    zeta = jnp.exp(log_g[:, None] * (RET_C - 1.0 - idx)[None, :])
    xi = jnp.exp(log_g[:, None] * (idx + 1.0)[None, :])
    decay = jnp.exp(log_g * RET_C)
    zb = jnp.broadcast_to(zeta[:, :, None], (RET_H, RET_C, RET_D))
    xb = jnp.broadcast_to(xi[:, :, None], (RET_H, RET_C, RET_D))
    db = jnp.broadcast_to(decay[:, None, None], (RET_H, RET_C, RET_D))
    return dmask.astype(F32), zb.astype(F32), xb.astype(F32), db.astype(F32)


def ret_fwd(rq, rk, rv, consts, *, rb):
    T = rq.shape[0]
    nb = T // rb
    ncb = rb // RET_C

    def body(q_ref, k_ref, v_ref, dm_ref, z_ref, x_ref, dc_ref, o_ref, st_ref, r_sc):
        @pl.when(pl.program_id(0) == 0)
        def _():
            r_sc[...] = jnp.zeros(r_sc.shape, F32)

        for c in range(ncb):
            sl = slice(c * RET_C, (c + 1) * RET_C)
            for h in range(RET_H):
                hd = slice(h * RET_D, (h + 1) * RET_D)
                q, k, v = q_ref[sl, hd], k_ref[sl, hd], v_ref[sl, hd]
                r = r_sc[h]
                rbf = r.astype(BF16)
                st_ref[sl, hd] = rbf
                s = _dot_nt(q, k) * dm_ref[h]
                inner = _dot(s.astype(BF16), v)
                cross = _dot((q.astype(F32) * x_ref[h]).astype(BF16), rbf)
                o_ref[sl, hd] = inner + cross
                kz = (k.astype(F32) * z_ref[h]).T.astype(BF16)
                r_sc[h] = r * dc_ref[h] + _dot(kz, v)

    blk = pl.BlockSpec((rb, RET_H * RET_D), lambda b: (b, 0))
    cst = pl.BlockSpec((RET_H, RET_C, RET_D), lambda b: (0, 0, 0))
    return pl.pallas_call(
        body,
        name="ret_fwd",
        grid=(nb,),
        in_specs=[blk, blk, blk, cst, cst, cst, cst],
        out_specs=[blk, blk],
        out_shape=[_sds((T, RET_H * RET_D), F32), _sds((T, RET_H * RET_D), BF16)],
        scratch_shapes=[pltpu.VMEM((RET_H, RET_D, RET_D), F32)],
        compiler_params=_cp(1),
    )(rq, rk, rv, *consts)


def ret_bwd(rq, rk, rv, st, dret, consts, *, rb):
    T = rq.shape[0]
    nb = T // rb
    ncb = rb // RET_C

    def body(q_ref, k_ref, v_ref, st_ref, do_ref, dm_ref, z_ref, x_ref, dc_ref, dq_ref, dk_ref, dv_ref, g_sc):
        @pl.when(pl.program_id(0) == 0)
        def _():
            g_sc[...] = jnp.zeros(g_sc.shape, F32)

        for c in reversed(range(ncb)):
            sl = slice(c * RET_C, (c + 1) * RET_C)
            for h in range(RET_H):
                hd = slice(h * RET_D, (h + 1) * RET_D)
                dm, zt, xi = dm_ref[h], z_ref[h], x_ref[h]
                q, k, v, rp = q_ref[sl, hd], k_ref[sl, hd], v_ref[sl, hd], st_ref[sl, hd]
                dob = do_ref[sl, hd].astype(BF16)
                qf, kf = q.astype(F32), k.astype(F32)
                gn = g_sc[h]
                gnb = gn.astype(BF16)
                s = _dot_nt(q, k) * dm
                ds = _dot_nt(dob, v) * dm
                dq = _dot(ds.astype(BF16), k) + _dot_nt(dob, rp) * xi
                dk = _dot(ds.T.astype(BF16), q) + _dot_nt(v, gnb) * zt
                dv = _dot(s.T.astype(BF16), dob) + _dot((kf * zt).astype(BF16), gnb)
                dq_ref[sl, hd] = dq.astype(dq_ref.dtype)
                dk_ref[sl, hd] = dk.astype(dk_ref.dtype)
                dv_ref[sl, hd] = dv.astype(dv_ref.dtype)
                g_sc[h] = _dot((qf * xi).T.astype(BF16), dob) + dc_ref[h] * gn

    blk = pl.BlockSpec((rb, RET_H * RET_D), lambda b: (nb - 1 - b, 0))
    cst = pl.BlockSpec((RET_H, RET_C, RET_D), lambda b: (0, 0, 0))
    return pl.pallas_call(
        body,
        name="ret_bwd",
        grid=(nb,),
        in_specs=[blk, blk, blk, blk, blk, cst, cst, cst, cst],
        out_specs=[blk, blk, blk],
        out_shape=[_sds((T, RET_H * RET_D), F32), _sds((T, RET_H * RET_D), F32), _sds((T, RET_H * RET_D), BF16)],
        scratch_shapes=[pltpu.VMEM((RET_H, RET_D, RET_D), F32)],
        compiler_params=_cp(1),
    )(rq, rk, rv, st, dret, *consts)


HALO = 16
EDGE = 8


def conv_act_fwd(up_pre, w_conv, b_conv, *, tile, cw):
    T = up_pre.shape[0]
    nt = T // tile
    ncol = D_FF // cw
    hb = tile // HALO

    def body(pa_ref, a_ref, pb_ref, b_ref, wa_ref, wb_ref, ba_ref, bb_ref, o_ref):
        i = pl.program_id(1)
        keep = (i > 0).astype(F32)

        def conv(prev_ref, cur_ref, w_ref, bias_ref):
            ext = jnp.concatenate([prev_ref[...].astype(F32)[HALO - EDGE:, :] * keep, cur_ref[...].astype(F32)], axis=0)
            w = w_ref[...]
            y = ext * w[2:3, :] + pltpu.roll(ext, 1, 0) * w[1:2, :] + pltpu.roll(ext, 2, 0) * w[0:1, :] + bias_ref[...]
            return y[EDGE:, :]

        a = conv(pa_ref, a_ref, wa_ref, ba_ref)
        b = conv(pb_ref, b_ref, wb_ref, bb_ref)
        o_ref[...] = (a * _sigmoid(a) * b).astype(o_ref.dtype)

    prev_a = pl.BlockSpec((HALO, cw), lambda j, i: (jnp.maximum(i * hb - 1, 0), j))
    cur_a = pl.BlockSpec((tile, cw), lambda j, i: (i, j))
    prev_b = pl.BlockSpec((HALO, cw), lambda j, i: (jnp.maximum(i * hb - 1, 0), j + ncol))
    cur_b = pl.BlockSpec((tile, cw), lambda j, i: (i, j + ncol))
    w_a = pl.BlockSpec((3, cw), lambda j, i: (0, j))
    w_b = pl.BlockSpec((3, cw), lambda j, i: (0, j + ncol))
    bias_a = pl.BlockSpec((1, cw), lambda j, i: (0, j))
    bias_b = pl.BlockSpec((1, cw), lambda j, i: (0, j + ncol))
    return pl.pallas_call(
        body,
        name="conv_act_fwd",
        grid=(ncol, nt),
        in_specs=[prev_a, cur_a, prev_b, cur_b, w_a, w_b, bias_a, bias_b],
        out_specs=pl.BlockSpec((tile, cw), lambda j, i: (i, j)),
        out_shape=_sds((T, D_FF), BF16),
        compiler_params=_cp(2),
    )(up_pre, up_pre, up_pre, up_pre, w_conv, w_conv, b_conv, b_conv)


def conv_act_bwd(up_pre, dact, w_conv, b_conv, *, tile, cw):
    T = up_pre.shape[0]
    nt = T // tile
    ncol = D_FF // cw
    hb = tile // HALO
    ext_rows = tile + 2 * EDGE

    def body(pa_ref, a_ref, na_ref, pb_ref, b_ref, nb_ref, d_ref, nd_ref, wa_ref, wb_ref, ba_ref, bb_ref,
             dxa_ref, dxb_ref, sa_ref, sb_ref):
        i = pl.program_id(1)
        keep_p = (i > 0).astype(F32)
        keep_n = (i < nt - 1).astype(F32)

        def ext_of(prev_ref, cur_ref, next_ref):
            return jnp.concatenate(
                [prev_ref[...].astype(F32)[HALO - EDGE:, :] * keep_p, cur_ref[...].astype(F32),
                 next_ref[...].astype(F32)[:EDGE, :] * keep_n], axis=0)

        def taps(ext):
            return ext, pltpu.roll(ext, 1, 0), pltpu.roll(ext, 2, 0)

        def conv(tp, w, bias):
            return tp[0] * w[2:3, :] + tp[1] * w[1:2, :] + tp[2] * w[0:1, :] + bias

        xa = taps(ext_of(pa_ref, a_ref, na_ref))
        xb = taps(ext_of(pb_ref, b_ref, nb_ref))
        wa, wb = wa_ref[...], wb_ref[...]
        a = conv(xa, wa, ba_ref[...])
        b = conv(xb, wb, bb_ref[...])
        dy = jnp.concatenate(
            [jnp.zeros((EDGE, cw), F32), d_ref[...].astype(F32), nd_ref[...].astype(F32)[:EDGE, :] * keep_n], axis=0)
        sg = _sigmoid(a)
        da = dy * b * (sg * (1.0 + a * (1.0 - sg)))
        db = dy * (a * sg)

        def back(dup, tp, w, dx_ref, s_ref):
            dx = dup * w[2:3, :] + pltpu.roll(dup, ext_rows - 1, 0) * w[1:2, :] + pltpu.roll(dup, ext_rows - 2, 0) * w[0:1, :]
            dx_ref[...] = dx[EDGE:EDGE + tile, :].astype(dx_ref.dtype)
            dc = dup[EDGE:EDGE + tile, :]
            r2 = jnp.sum(dc * tp[0][EDGE:EDGE + tile, :], axis=0, keepdims=True)
            r1 = jnp.sum(dc * tp[1][EDGE:EDGE + tile, :], axis=0, keepdims=True)
            r0 = jnp.sum(dc * tp[2][EDGE:EDGE + tile, :], axis=0, keepdims=True)
            rb = jnp.sum(dc, axis=0, keepdims=True)
            row = lax.broadcasted_iota(jnp.int32, (8, cw), 0)
            upd = (jnp.where(row == 0, r0, 0.0) + jnp.where(row == 1, r1, 0.0) + jnp.where(row == 2, r2, 0.0)
                   + jnp.where(row == 3, rb, 0.0))

            @pl.when(i == 0)
            def _():
                s_ref[...] = upd

            @pl.when(i > 0)
            def _():
                s_ref[...] += upd

        back(da, xa, wa, dxa_ref, sa_ref)
        back(db, xb, wb, dxb_ref, sb_ref)

    def prev_of(shift):
        return pl.BlockSpec((HALO, cw), lambda j, i: (jnp.maximum(i * hb - 1, 0), j + shift))

    def next_of(shift):
        return pl.BlockSpec((HALO, cw), lambda j, i: (jnp.minimum((i + 1) * hb, nt * hb - 1), j + shift))

    def cur_of(shift):
        return pl.BlockSpec((tile, cw), lambda j, i: (i, j + shift))

    def row_of(rows, shift):
        return pl.BlockSpec((rows, cw), lambda j, i: (0, j + shift))

    return pl.pallas_call(
        body,
        name="conv_act_bwd",
        grid=(ncol, nt),
        in_specs=[prev_of(0), cur_of(0), next_of(0), prev_of(ncol), cur_of(ncol), next_of(ncol), cur_of(0), next_of(0),
                  row_of(3, 0), row_of(3, ncol), row_of(1, 0), row_of(1, ncol)],
        out_specs=[cur_of(0), cur_of(0), row_of(8, 0), row_of(8, 0)],
        out_shape=[_sds((T, D_FF), BF16), _sds((T, D_FF), BF16), _sds((8, D_FF), F32), _sds((8, D_FF), F32)],
        compiler_params=_cp(2),
    )(up_pre, up_pre, up_pre, up_pre, up_pre, up_pre, dact, dact, w_conv, w_conv, b_conv, b_conv)


NS_IN, NS_UP, NS_XKV, NS_UQ = 564, 704, 256, 96
E2_ROWS, E2_UQ = 384, 128
L1_W = NS_UP + NS_XKV
L2_ROWS = 736
L2_OUT, L2_XQ, L2_XO, L2_DN = 0, 128, 256, 384
RS_DN = 352
L3_ROWS, L3_PRET = 1024, 512
L4_SHAPE = (8, 768)
WZ_RUNS = ((416, 4096, 0), (0, 384, 4096), (384, 32, 4544))
WZ_ZERO = ((4480, 4544), (4576, 4608))


def _pieces(orig_start, length, dst_start, ns):
    out, c, d, end = [], orig_start, dst_start, orig_start + length
    while c < end:
        j, off = c // ns, c % ns
        ln = min(ns - off, end - c)
        out.append((j, off, d, ln))
        c += ln
        d += ln
    return out


def pack_early(w_in, w_ukv, w_uq):
    def body(in_ref, ukv_ref, uq_ref, e1_ref, e2_ref):
        e1_ref[...] = in_ref[...].astype(BF16)
        e2_ref[0:E2_UQ, :] = ukv_ref[...].astype(BF16)
        e2_ref[E2_UQ:E2_ROWS, 0:NS_UQ] = uq_ref[...].astype(BF16)
        e2_ref[E2_UQ:E2_ROWS, NS_UQ:HP] = jnp.zeros((E2_ROWS - E2_UQ, HP - NS_UQ), BF16)

    return pl.pallas_call(
        body,
        name="pack_early",
        in_specs=[VM] * 3,
        out_specs=[VM] * 2,
        out_shape=[_sds((D, NS_IN), BF16), _sds((E2_ROWS, HP), BF16)],
        compiler_params=_cp0(),
    )(w_in, w_ukv, w_uq)


def pack_late(w_up, w_xkv, w_out, w_xq, w_xo, w_down, w_pmla, w_pret, w_conv):
    def body(up_ref, xkv_ref, o_ref, xq_ref, xo_ref, dn_ref, pm_ref, pr_ref, cv_ref, l1_ref, l2_ref, l3_ref, l4_ref):
        l1_ref[:, 0:NS_UP] = up_ref[...].astype(BF16)
        l1_ref[:, NS_UP:L1_W] = xkv_ref[...].astype(BF16)
        l2_ref[L2_OUT:L2_XQ, :] = o_ref[...].astype(BF16)
        l2_ref[L2_XQ:L2_XO, :] = xq_ref[...].astype(BF16)
        l2_ref[L2_XO:L2_DN, :] = xo_ref[...].astype(BF16)
        l2_ref[L2_DN:L2_ROWS, :] = dn_ref[...].astype(BF16)
        l3_ref[0:L3_PRET, :] = pm_ref[...].astype(BF16)
        l3_ref[L3_PRET:L3_ROWS, :] = pr_ref[...].astype(BF16)
        l4_ref[...] = jnp.zeros(L4_SHAPE, F32)
        l4_ref[0:3, 0:NS_UP] = cv_ref[...]

    return pl.pallas_call(
        body,
        name="pack_late",
        in_specs=[VM] * 9,
        out_specs=[VM] * 4,
        out_shape=[_sds((D, L1_W), BF16), _sds((L2_ROWS, D), BF16), _sds((L3_ROWS, HP), BF16), _sds(L4_SHAPE, F32)],
        compiler_params=_cp0(),
    )(w_up, w_xkv, w_out, w_xq, w_xo, w_down, w_pmla, w_pret, w_conv)


def assemble_early(g1, g2, *, tile):
    def body(g1_ref, g2_ref, wz_ref, wk_ref, wv_ref, wq_ref):
        for lo, hi in WZ_ZERO:
            wz_ref[:, lo:hi] = jnp.zeros((tile, hi - lo), BF16)
        for os_, ln_, ds_ in WZ_RUNS:
            for j, off, d, ln in _pieces(os_, ln_, ds_, NS_IN):
                wz_ref[:, d:d + ln] = g1_ref[j, :, off:off + ln]

        @pl.when(pl.program_id(0) == 0)
        def _():
            half = jnp.zeros((MLA_KVR, HP - MLA_NOPE), BF16)
            for j in range(N_DEV):
                wk_ref[:, j * HP:j * HP + MLA_NOPE] = g2_ref[j, 0:E2_UQ, 0:MLA_NOPE]
                wk_ref[:, j * HP + MLA_NOPE:(j + 1) * HP] = half
                wv_ref[:, j * HP:j * HP + MLA_V] = g2_ref[j, 0:E2_UQ, MLA_NOPE:HP]
                wv_ref[:, j * HP + MLA_V:(j + 1) * HP] = half
                wq_ref[:, j * HP:j * HP + NS_UQ] = g2_ref[j, E2_UQ:E2_ROWS, 0:NS_UQ]
                wq_ref[:, j * HP + NS_UQ:(j + 1) * HP] = jnp.zeros((MLA_QR, HP - NS_UQ), BF16)

    def whole(r):
        return pl.BlockSpec((r, MLA_H * HP), lambda i: (0, 0))

    return pl.pallas_call(
        body,
        name="assemble_early",
        grid=(D // tile,),
        in_specs=[pl.BlockSpec((N_DEV, tile, NS_IN), lambda i: (0, i, 0)),
                  pl.BlockSpec((N_DEV, E2_ROWS, HP), lambda i: (0, 0, 0))],
        out_specs=[pl.BlockSpec((tile, ZW), lambda i: (i, 0)), whole(MLA_KVR), whole(MLA_KVR), whole(MLA_QR)],
        out_shape=[_sds((D, ZW), BF16), _sds((MLA_KVR, MLA_H * HP), BF16), _sds((MLA_KVR, MLA_H * HP), BF16),
                   _sds((MLA_QR, MLA_H * HP), BF16)],
        compiler_params=_cp(1),
    )(g1, g2)


def assemble_l1(g1, *, tile):
    def body(g_ref, wup_ref, wxkv_ref):
        for j in range(N_DEV):
            wup_ref[:, j * NS_UP:(j + 1) * NS_UP] = g_ref[j, :, 0:NS_UP]
            wxkv_ref[:, j * NS_XKV:(j + 1) * NS_XKV] = g_ref[j, :, NS_UP:L1_W]

    return pl.pallas_call(
        body,
        name="assemble_l1",
        grid=(D // tile,),
        in_specs=[pl.BlockSpec((N_DEV, tile, L1_W), lambda i: (0, i, 0))],
        out_specs=[pl.BlockSpec((tile, 2 * D_FF), lambda i: (i, 0)), pl.BlockSpec((tile, 2 * D), lambda i: (i, 0))],
        out_shape=[_sds((D, 2 * D_FF), BF16), _sds((D, 2 * D), BF16)],
        compiler_params=_cp(1),
    )(g1)


def assemble_l234(g2, g3, g4):
    def body(g2_ref, g3_ref, g4_ref, wo_ref, wxq_ref, wxo_ref, wdn_ref, wpa_ref, wpr_ref, wc_ref):
        for j in range(N_DEV):
            wo_ref[j * 128:(j + 1) * 128, :] = g2_ref[j, L2_OUT:L2_XQ, :]
            wxq_ref[j * 128:(j + 1) * 128, :] = g2_ref[j, L2_XQ:L2_XO, :]
            wxo_ref[j * 128:(j + 1) * 128, :] = g2_ref[j, L2_XO:L2_DN, :]
            wdn_ref[j * RS_DN:(j + 1) * RS_DN, :] = g2_ref[j, L2_DN:L2_ROWS, :]
            wpr_ref[:, j * 128:(j + 1) * 128] = g3_ref[j, L3_PRET:L3_ROWS, :]
            wc_ref[:, j * NS_UP:(j + 1) * NS_UP] = g4_ref[j, 0:3, 0:NS_UP]
            for h in range(MLA_H):
                wpa_ref[h * HP:h * HP + MLA_V, j * 128:(j + 1) * 128] = g3_ref[j, h * MLA_V:(h + 1) * MLA_V, :]
        for h in range(MLA_H):
            wpa_ref[h * HP + MLA_V:(h + 1) * HP, :] = jnp.zeros((HP - MLA_V, D), BF16)

    return pl.pallas_call(
        body,
        name="assemble_l234",
        in_specs=[VM] * 3,
        out_specs=[VM] * 7,
        out_shape=[_sds((D, D), BF16), _sds((D, D), BF16), _sds((D, D), BF16), _sds((D_FF, D), BF16),
                   _sds((MLA_H * HP, D), BF16), _sds((RET_H * RET_D, D), BF16), _sds((3, 2 * D_FF), F32)],
        compiler_params=_cp0(),
    )(g2, g3, g4)


def slots_l1(dwup_a, dwup_b, dwxkv, *, tile):
    per_half = D_FF // NS_UP

    def body(ua_ref, ub_ref, x_ref, s_ref):
        for j in range(N_DEV):
            src = ua_ref if j < per_half else ub_ref
            c0 = (j % per_half) * NS_UP
            s_ref[j, :, 0:NS_UP] = src[:, c0:c0 + NS_UP].astype(BF16)
            s_ref[j, :, NS_UP:L1_W] = x_ref[:, j * NS_XKV:(j + 1) * NS_XKV].astype(BF16)

    return pl.pallas_call(
        body,
        name="slots_l1",
        grid=(D // tile,),
        in_specs=[pl.BlockSpec((tile, D_FF), lambda i: (i, 0)), pl.BlockSpec((tile, D_FF), lambda i: (i, 0)),
                  pl.BlockSpec((tile, 2 * D), lambda i: (i, 0))],
        out_specs=pl.BlockSpec((N_DEV, tile, L1_W), lambda i: (0, i, 0)),
        out_shape=_sds((N_DEV, D, L1_W), BF16),
        compiler_params=_cp(1),
    )(dwup_a, dwup_b, dwxkv)


def slots_l23(dwo, dwxq, dwxo, dwdn, dwpa, dwpr):
    def body(o_ref, xq_ref, xo_ref, dn_ref, pa_ref, pr_ref, l2_ref, l3_ref):
        l2_ref[L2_OUT:L2_XQ, :] = o_ref[...].astype(BF16)
        l2_ref[L2_XQ:L2_XO, :] = xq_ref[...].astype(BF16)
        l2_ref[L2_XO:L2_DN, :] = xo_ref[...].astype(BF16)
        l2_ref[L2_DN:L2_ROWS, :] = dn_ref[...].astype(BF16)
        for h in range(MLA_H):
            l3_ref[h * MLA_V:(h + 1) * MLA_V, :] = pa_ref[h * HP:h * HP + MLA_V, :].astype(BF16)
        l3_ref[L3_PRET:L3_ROWS, :] = pr_ref[...].astype(BF16)

    rows128 = pl.BlockSpec((128, D), lambda j: (j, 0))

    def cols(r):
        return pl.BlockSpec((r, 128), lambda j: (0, j))

    return pl.pallas_call(
        body,
        name="slots_l23",
        grid=(N_DEV,),
        in_specs=[rows128, rows128, rows128, pl.BlockSpec((RS_DN, D), lambda j: (j, 0)),
                  cols(MLA_H * HP), cols(RET_H * RET_D)],
        out_specs=[pl.BlockSpec((None, L2_ROWS, D), lambda j: (j, 0, 0)),
                   pl.BlockSpec((None, L3_ROWS, HP), lambda j: (j, 0, 0))],
        out_shape=[_sds((N_DEV, L2_ROWS, D), BF16), _sds((N_DEV, L3_ROWS, HP), BF16)],
        compiler_params=_cp(1),
    )(dwo, dwxq, dwxo, dwdn, dwpa, dwpr)


def slots_early(dwz, dwk, dwv, dwq, *, tile):
    def body(dz_ref, k_ref, v_ref, q_ref, s1_ref, s2_ref):
        for os_, ln_, ds_ in WZ_RUNS:
            for j, off, d, ln in _pieces(os_, ln_, ds_, NS_IN):
                s1_ref[j, :, off:off + ln] = dz_ref[:, d:d + ln].astype(BF16)

        @pl.when(pl.program_id(0) == 0)
        def _():
            for j in range(N_DEV):
                s2_ref[j, 0:E2_UQ, 0:MLA_NOPE] = k_ref[:, j * HP:j * HP + MLA_NOPE].astype(BF16)
                s2_ref[j, 0:E2_UQ, MLA_NOPE:HP] = v_ref[:, j * HP:j * HP + MLA_V].astype(BF16)
                s2_ref[j, E2_UQ:E2_ROWS, 0:NS_UQ] = q_ref[:, j * HP:j * HP + NS_UQ].astype(BF16)
                s2_ref[j, E2_UQ:E2_ROWS, NS_UQ:HP] = jnp.zeros((E2_ROWS - E2_UQ, HP - NS_UQ), BF16)

    def whole(r):
        return pl.BlockSpec((r, MLA_H * HP), lambda i: (0, 0))

    return pl.pallas_call(
        body,
        name="slots_early",
        grid=(D // tile,),
        in_specs=[pl.BlockSpec((tile, ZW), lambda i: (i, 0)), whole(MLA_KVR), whole(MLA_KVR), whole(MLA_QR)],
        out_specs=[pl.BlockSpec((N_DEV, tile, NS_IN), lambda i: (0, i, 0)),
                   pl.BlockSpec((N_DEV, E2_ROWS, HP), lambda i: (0, 0, 0))],
        out_shape=[_sds((N_DEV, D, NS_IN), BF16), _sds((N_DEV, E2_ROWS, HP), BF16)],
        compiler_params=_cp(1),
    )(dwz, dwk, dwv, dwq)


SMALL = (("g_mix", 1024, 0), ("b_gate", 2048, 1), ("g_q_lat", 256, 3), ("g_kv_lat", 128, 4), ("g_ret", 512, 5),
         ("g_cross", 1024, 6), ("g_mem", 1024, 7), ("g_ffn", 1024, 8), ("b_conv", 5632, 9), ("g_final", 1024, 15))
SMALL_DIRECT = tuple(s for s in SMALL if s[0] != "b_conv")
S_ROWS = 16
LOSS_ROW, LOSS_LANE = 4, 128


def _flat_pieces(n, row0):
    return [(row0 + k // D, k, min(D, n - k)) for k in range(0, n, D)]


def exchange_small(cs_a, cs_b, loss_part, smalls):
    ns = len(smalls)

    def body(*refs):
        ca_ref, cb_ref, loss_ref = refs[:3]
        small_refs = refs[3:3 + ns]
        rd_ref, rs_ref, dsl_ref, own_ref = refs[3 + ns:7 + ns]
        sems = refs[7 + ns:]
        own_ref[...] = jnp.zeros(own_ref.shape, F32)
        own_ref[LOSS_ROW:LOSS_ROW + 1, LOSS_LANE:LOSS_LANE + HP] = loss_ref[0:1, :]
        for (name, n, row0), g_ref in zip(SMALL_DIRECT, small_refs):
            for r, c0, ln in _flat_pieces(n, row0):
                own_ref[r:r + 1, 0:ln] = g_ref[:, c0:c0 + ln]
        row0 = dict((s[0], s[2]) for s in SMALL)["b_conv"]
        for half, c_ref in enumerate((ca_ref, cb_ref)):
            k = half * D_FF
            end = k + D_FF
            while k < end:
                r, lane = row0 + k // D, k % D
                ln = min(D - lane, end - k)
                own_ref[r:r + 1, lane:lane + ln] = c_ref[3:4, k - half * D_FF:k - half * D_FF + ln]
                k += ln
        dsl_ref[...] = jnp.zeros(dsl_ref.shape, F32)
        per_half = D_FF // NS_UP
        for j in range(N_DEV):
            c_ref = ca_ref if j < per_half else cb_ref
            c0 = (j % per_half) * NS_UP
            dsl_ref[j, 0:3, 0:NS_UP] = c_ref[0:3, c0:c0 + NS_UP]
        comm = ([(dsl_ref, True), (own_ref, False)], [rd_ref, rs_ref]) + tuple(sems)
        _exchange_start(*comm)
        _exchange_wait(*comm)

    n1 = N_DEV - 1
    return pl.pallas_call(
        body,
        name="exchange_small",
        in_specs=[VM] * (3 + ns),
        out_specs=[VM, VM, VM, VM],
        out_shape=[_sds((n1,) + L4_SHAPE, F32), _sds((n1, S_ROWS, D), F32), _sds((N_DEV,) + L4_SHAPE, F32),
                   _sds((S_ROWS, D), F32)],
        scratch_shapes=_exchange_scratch(2),
        compiler_params=_cp0(),
    )(cs_a, cs_b, loss_part, *smalls)


def _adamw(w, g, m, v):
    m = ADAM_B1 * m + (1.0 - ADAM_B1) * g
    v = ADAM_B2 * v + (1.0 - ADAM_B2) * (g * g)
    m_hat = m / (1.0 - ADAM_B1 ** ADAM_STEP)
    v_hat = v / (1.0 - ADAM_B2 ** ADAM_STEP)
    delta = -ADAM_LR * (m_hat / (jnp.sqrt(v_hat) + ADAM_EPS) + ADAM_WD * w)
    return delta, m, v


def _apply(g, refs, outs):
    d, mn, vn = _adamw(refs[0][...], g, refs[1][...], refs[2][...])
    outs[0][...] = g
    outs[1][...] = d
    outs[2][...] = mn
    outs[3][...] = vn


def adam_cols(own, recv, wmv, spans, *, name, tile):
    R, W = own.shape
    nw = len(spans)

    def body(*refs):
        own_ref, recv_ref = refs[:2]
        ins, outs = refs[2:2 + 3 * nw], refs[2 + 3 * nw:]
        g = own_ref[...].astype(F32)
        for k in range(N_DEV - 1):
            g = g + recv_ref[k].astype(F32)
        for t, (lo, hi) in enumerate(spans):
            _apply(g[:, lo:hi], ins[3 * t:3 * t + 3], outs[4 * t:4 * t + 4])

    def blk(w):
        return pl.BlockSpec((tile, w), lambda i: (i, 0))

    widths = [hi - lo for lo, hi in spans]
    return pl.pallas_call(
        body,
        name=name,
        grid=(R // tile,),
        in_specs=[blk(W), pl.BlockSpec((N_DEV - 1, tile, W), lambda i: (0, i, 0))] + [blk(w) for w in widths for _ in range(3)],
        out_specs=[blk(w) for w in widths for _ in range(4)],
        out_shape=[_sds((R, w), F32) for w in widths for _ in range(4)],
        compiler_params=_cp(1),
    )(own, recv, *wmv)


def adam_rows(own, recv, wmv, spans, *, name):
    nw = len(spans)

    def body(*refs):
        own_ref, recv_ref = refs[:2]
        ins, outs = refs[2:2 + 3 * nw], refs[2 + 3 * nw:]
        for t, (lo, hi, w) in enumerate(spans):
            g = own_ref[lo:hi, :].astype(F32)
            for k in range(N_DEV - 1):
                g = g + recv_ref[k, lo:hi, :].astype(F32)
            _apply(g[:, 0:w], ins[3 * t:3 * t + 3], outs[4 * t:4 * t + 4])

    return pl.pallas_call(
        body,
        name=name,
        in_specs=[VM] * (2 + 3 * nw),
        out_specs=[VM] * (4 * nw),
        out_shape=[_sds((hi - lo, w), F32) for lo, hi, w in spans for _ in range(4)],
        compiler_params=_cp0(),
    )(own, recv, *wmv)


def adam_small(own_s, recv_s, dslots, recv_d, wmv_small, wmv_conv):
    ns = len(SMALL)

    def body(*refs):
        own_ref, rs_ref, dsl_ref, rd_ref = refs[:4]
        ins = refs[4:4 + 3 * ns + 3]
        outs = refs[4 + 3 * ns + 3:4 + 3 * ns + 3 + 4 * ns + 4]
        loss_ref, all_sc = refs[-2], refs[-1]
        me = _my_id()
        all_sc[0] = own_ref[...]
        for k in range(N_DEV - 1):
            all_sc[k + 1] = rs_ref[k]
        g = all_sc[jnp.bitwise_xor(me, 0)]
        for s in range(1, N_DEV):
            g = g + all_sc[jnp.bitwise_xor(me, s)]
        all_sc[0] = g
        loss_ref[...] = all_sc[0, LOSS_ROW:LOSS_ROW + 1, LOSS_LANE:LOSS_LANE + HP]
        for t, (name, n, row0) in enumerate(SMALL):
            pieces = [all_sc[0, r:r + 1, 0:ln] for r, _, ln in _flat_pieces(n, row0)]
            gt = pieces[0] if len(pieces) == 1 else jnp.concatenate(pieces, axis=1)
            _apply(gt, ins[3 * t:3 * t + 3], outs[4 * t:4 * t + 4])
        gc = dsl_ref[me]
        for k in range(N_DEV - 1):
            gc = gc + rd_ref[k]
        _apply(gc[0:3, 0:NS_UP], ins[3 * ns:3 * ns + 3], outs[4 * ns:4 * ns + 4])

    out_shape = ([_sds((1, n), F32) for _, n, _ in SMALL for _ in range(4)] + [_sds((3, NS_UP), F32)] * 4
                 + [_sds((1, HP), F32)])
    return pl.pallas_call(
        body,
        name="adam_small",
        in_specs=[VM] * (4 + 3 * ns + 3),
        out_specs=[VM] * len(out_shape),
        out_shape=out_shape,
        scratch_shapes=[pltpu.VMEM((N_DEV, S_ROWS, D), F32)],
    )(own_s, recv_s, dslots, recv_d, *wmv_small, *wmv_conv)


def kernel(x, mem, positions, g_mix, w_in, b_gate, g_q_lat, w_uq, g_kv_lat, w_ukv, w_proj_mla, g_ret, w_proj_ret, w_out, g_cross, g_mem, w_xq, w_xkv, w_xo, g_ffn, w_up, w_conv, b_conv, w_down, g_final, loss_target, m_g_mix, m_w_in, m_b_gate, m_g_q_lat, m_w_uq, m_g_kv_lat, m_w_ukv, m_w_proj_mla, m_g_ret, m_w_proj_ret, m_w_out, m_g_cross, m_g_mem, m_w_xq, m_w_xkv, m_w_xo, m_g_ffn, m_w_up, m_w_conv, m_b_conv, m_w_down, m_g_final, v_g_mix, v_w_in, v_b_gate, v_g_q_lat, v_w_uq, v_g_kv_lat, v_w_ukv, v_w_proj_mla, v_g_ret, v_w_proj_ret, v_w_out, v_g_cross, v_g_mem, v_w_xq, v_w_xkv, v_w_xo, v_g_ffn, v_w_up, v_w_conv, v_b_conv, v_w_down, v_g_final):
    args = dict(locals())
    T = x.shape[1]
    M = mem.shape[1]
    tile = min(256, T)
    tile2 = min(512, T)
    tile4 = min(1024, T)
    tq = min(512, T)
    rb = min(1024, T)

    xs = x[0]
    tgt = loss_target[0]
    mems = mem[0]

    def shard(name, prefix=""):
        a = args[prefix + name]
        return a.reshape(a.shape[-2:]) if a.ndim >= 2 else a.reshape(1, -1)

    e1, e2 = pack_early(shard("w_in"), shard("w_ukv"), shard("w_uq"))
    late_parts = pack_late(shard("w_up"), shard("w_xkv"), shard("w_out"), shard("w_xq"), shard("w_xo"), shard("w_down"),
                           shard("w_proj_mla"), shard("w_proj_ret"), shard("w_conv"))

    pos_f = jnp.broadcast_to(positions[0].astype(F32)[:, None], (T, 128))
    inv_m, inv_r = _rot_inv()
    u, cm, s1, s2, cr, sr, ge1, ge2 = rowwise(
        lambda xv, p, im, ir, g: (_rms(xv, g),) + _rot_tables(p, im, ir), [(xs, None), (pos_f, None)],
        [inv_m, inv_r, g_mix], [(D, BF16)] + [(128, F32)] * 5, [], tile=tile2, name="norm_mix_tables", gather=[e1, e2])
    wz, wk, wv, wq = assemble_early(ge1, ge2, tile=tile)
    rconsts = _ret_consts()

    z = matmul(u, wz, name="mm_z", tm=2048, tn=1536)

    def mixers_in(zl, zr, cmv, s1v, s2v, crv, srv, gq, gkv, wqv, wkv_, wvv):
        cq = _rms(zl[:, 0:256], gq).astype(BF16)
        ckv = _rms(zl[:, 256:384], gkv).astype(BF16)
        qv = _rot_mla(_dot(cq, wqv), cmv, s1v, s2v)
        kr = _rot_mla(zl[:, 384:512], cmv, s1v, s2v)
        kn = _dot(ckv, wkv_)
        kv_ = jnp.concatenate([kn[:, h * HP:(h + 1) * HP] + kr for h in range(MLA_H)], axis=1)
        vv = _dot(ckv, wvv)
        lane = lax.broadcasted_iota(jnp.int32, vv.shape, 1)
        vv = jnp.where((lane & (HP - 1)) == ONE_LANE, 1.0, vv)
        rqv = _rot_ret(zr[:, 0:512], crv, srv)
        rkv = _rot_ret(zr[:, 512:1024], crv, srv) * (RET_D ** -0.5)
        return qv, kv_, vv, rqv, rkv, zr[:, 1024:1536]

    q_a, k_a, v_a, rq, rk, rv = rowwise(
        mixers_in, [(z, (512, 8)), (z, (2048, 0)), (cm, None), (s1, None), (s2, None), (cr, None), (sr, None)],
        [g_q_lat, g_kv_lat, wq, wk, wv], [(MLA_H * HP, BF16)] * 3 + [(512, BF16)] * 3, [], tile=tile2, name="mixers_in")

    o_a, lse_row, gl1, gl2, gl3, gl4 = mla_fwd(q_a, k_a, v_a, list(late_parts), tq=tq)
    wup, wxkv = assemble_l1(gl1, tile=tile)
    wo, wxq, wxo, wdn, wpa, wpr, wcv = assemble_l234(gl2, gl3, gl4)
    ret, rstate = ret_fwd(rq, rk, rv, rconsts, rb=rb)

    def gn_parts(r):
        outs = []
        for h in range(RET_H):
            rh = r[:, h * RET_D:(h + 1) * RET_D]
            mu = jnp.mean(rh, axis=-1, keepdims=True)
            dlt = rh - mu
            rstd = lax.rsqrt(jnp.mean(dlt * dlt, axis=-1, keepdims=True) + EPS)
            outs.append((dlt * rstd, rstd))
        return outs

    def mix_fwd(ov, rv_, rg, gt, wpav, wprv, gr, bg):
        ya = _dot(ov, wpav)
        xh = jnp.concatenate([p[0] for p in gn_parts(rv_)], axis=1)
        t = rg * _sigmoid(rg) * (xh * gr)
        yr = _dot(t.astype(BF16), wprv)
        ga_ = _sigmoid(gt[:, :D] + bg[:, :D])
        gr_ = _sigmoid(gt[:, D:] + bg[:, D:])
        return ga_ * ya + gr_ * yr

    mix = rowwise(mix_fwd, [(o_a, None), (ret, None), (z, (512, 3)), (z, (2048, 1))], [wpa, wpr, g_ret, b_gate],
                  [(D, BF16)], [], tile=tile2, name="mix_fwd")[0]
    def proj_norm(av, rv_, wv_, g):
        hv = rv_ + _dot(av, wv_)
        return hv, _rms(hv, g)

    h1, n2 = rowwise(proj_norm, [(mix, None), (xs, None)], [wo, g_cross], [(D, F32), (D, BF16)], [], tile=tile4,
                     name="mm_out_norm")
    xq = matmul(n2, wxq, out_dtype=BF16, name="mm_xq")
    mn = rowwise(lambda mv_, g: _rms(mv_, g), [(mems, None)], [g_mem], [(D, BF16)], [], tile=min(tile, M), name="norm_mem")[0]
    mkv = matmul(mn, wxkv, out_dtype=BF16, name="mm_mkv")

    x_scale = X_HD ** -0.5

    def xattn_fwd(xqv, mkvv):
        outs = []
        for h in range(X_H):
            sl = slice(h * X_HD, (h + 1) * X_HD)
            s = _dot_nt(xqv[:, sl], mkvv[:, sl]) * x_scale
            s = s - jnp.max(s, axis=-1, keepdims=True)
            e = jnp.exp(s)
            p = e / jnp.sum(e, axis=-1, keepdims=True)
            outs.append(_dot(p.astype(BF16), mkvv[:, D + h * X_HD:D + (h + 1) * X_HD]))
        return jnp.concatenate(outs, axis=1)

    xo = rowwise(xattn_fwd, [(xq, None)], [mkv], [(D, BF16)], [], tile=tile4, name="xattn_fwd")[0]
    h2, n3 = rowwise(proj_norm, [(xo, None), (h1, None)], [wxo, g_ffn], [(D, F32), (D, BF16)], [], tile=tile4,
                     name="mm_xo_norm")
    up_pre = matmul(n3, wup, out_dtype=BF16, name="mm_up", tm=2048, tn=1408)
    cw = D_FF // 2
    act = conv_act_fwd(up_pre, wcv, b_conv, tile=tile, cw=cw)

    def down_loss(av, hv2, tv, wv_, g):
        hv = hv2 + _dot(av, wv_)
        y = _rms(hv, g)
        err = y - tv
        part = 0.5 * jnp.sum(jnp.sum(err * err, axis=-1, keepdims=True) / D, axis=0, keepdims=True)
        dx, dg = _rms_bwd(hv, g, err / D)
        return dx, dg, jnp.broadcast_to(part, (8, 128))

    g_fin2 = g_final.reshape(1, D)
    dh3, dg_final, loss_acc = rowwise(down_loss, [(act, None), (h2, None), (tgt, None)], [wdn, g_fin2], [(D, F32)],
                                      [((1, D), F32), ((8, 128), F32)], tile=tile2, name="mm_down_loss")

    dact = matmul(dh3, wdn, tb=True, out_dtype=BF16, name="mm_dact", tm=2048, tn=1408)
    dw_down = matmul_tn(act, dh3, name="mm_dw_down", tm=1408, tk=1024)
    dup_a, dup_b, cs_a, cs_b = conv_act_bwd(up_pre, dact, wcv, b_conv, tile=tile, cw=cw)
    dw_up_a = matmul_tn(n3, dup_a, name="mm_dw_up_a", tn=1408)
    dw_up_b = matmul_tn(n3, dup_b, name="mm_dw_up_b", tn=1408)

    def ffn_in_bwd(da_, db_, hv, drv, wv_, g):
        dn = _dot_nt(da_, wv_[:, :D_FF]) + _dot_nt(db_, wv_[:, D_FF:])
        dx, dg = _rms_bwd(hv, g, dn)
        return dx + drv, dg

    dh2, dg_ffn = rowwise(ffn_in_bwd, [(dup_a, None), (dup_b, None), (h2, None), (dh3, None)], [wup, g_ffn],
                          [(D, F32)], [((1, D), F32)], tile=tile2, name="mm_dn3_norm_bwd")
    dxo = matmul(dh2, wxo, tb=True, out_dtype=BF16, name="mm_dxo")
    dw_xo = matmul_tn(xo, dh2, name="mm_dw_xo")

    def xattn_bwd(xqv, dxov, mkvv):
        dxq, dmk, dmv = [], [], []
        for h in range(X_H):
            sl = slice(h * X_HD, (h + 1) * X_HD)
            slv = slice(D + h * X_HD, D + (h + 1) * X_HD)
            s = _dot_nt(xqv[:, sl], mkvv[:, sl]) * x_scale
            s = s - jnp.max(s, axis=-1, keepdims=True)
            e = jnp.exp(s)
            p = e / jnp.sum(e, axis=-1, keepdims=True)
            dp = _dot_nt(dxov[:, sl], mkvv[:, slv])
            ds = (p * (dp - jnp.sum(dp * p, axis=-1, keepdims=True)) * x_scale).astype(BF16)
            dxq.append(_dot(ds, mkvv[:, sl]))
            dmk.append(_dot_tn(ds, xqv[:, sl]))
            dmv.append(_dot_tn(p.astype(BF16), dxov[:, sl]))
        return jnp.concatenate(dxq, axis=1), jnp.concatenate(dmk + dmv, axis=1)

    dxq, dmkv = rowwise(xattn_bwd, [(xq, None), (dxo, None)], [mkv], [(D, BF16)], [((M, 2 * D), F32)],
                        tile=tile4, name="xattn_bwd")
    dw_xq = matmul_tn(n2, dxq, name="mm_dw_xq")

    def proj_norm_bwd(dyv, hv, drv, wv_, g):
        dx, dg = _rms_bwd(hv, g, _dot_nt(dyv, wv_))
        return dx + drv, dg

    dh1, dg_cross = rowwise(proj_norm_bwd, [(dxq, None), (h1, None), (dh2, None)], [wxq, g_cross], [(D, F32)],
                            [((1, D), F32)], tile=tile4, name="mm_dn2_norm_bwd")
    dw_xkv = matmul_tn(mn, dmkv, name="mm_dw_xkv", tk=M)
    dmn = matmul(dmkv, wxkv, tb=True, name="mm_dmn", tm=M)
    dg_mem = rowwise(lambda mv_, dyv, g: _rms_bwd(mv_, g, dyv)[1], [(mems, None), (dmn, None)], [g_mem], [],
                     [((1, D), F32)], tile=min(tile, M), name="norm_mem_bwd")[0]

    dmix = matmul(dh1, wo, tb=True, out_dtype=BF16, name="mm_dmix")
    dw_out = matmul_tn(mix, dh1, name="mm_dw_out")

    def mix_bwd(ov, rv_, rg, gt, dmv_, wpav, wprv, gr, bg):
        dm_ = dmv_.astype(F32)
        ya = _dot(ov, wpav)
        parts = gn_parts(rv_)
        xh = jnp.concatenate([p[0] for p in parts], axis=1)
        yn = xh * gr
        sg = _sigmoid(rg)
        sl_ = rg * sg
        t = (sl_ * yn).astype(BF16)
        yr = _dot(t, wprv)
        ga_ = _sigmoid(gt[:, :D] + bg[:, :D])
        gr_ = _sigmoid(gt[:, D:] + bg[:, D:])
        dgates = jnp.concatenate([dm_ * ya * ga_ * (1.0 - ga_), dm_ * yr * gr_ * (1.0 - gr_)], axis=1)
        dya = (dm_ * ga_).astype(BF16)
        dyr = (dm_ * gr_).astype(BF16)
        do_ = _dot_nt(dya, wpav)
        dwpa_ = _dot_tn(ov, dya)
        dt = _dot_nt(dyr, wprv)
        dwpr_ = _dot_tn(t, dyr)
        drg_ = dt * yn * (sg * (1.0 + rg * (1.0 - sg)))
        dyn = dt * sl_
        dgr = jnp.sum(dyn * xh, axis=0, keepdims=True)
        dxh = dyn * gr
        drets = []
        for h in range(RET_H):
            sl = slice(h * RET_D, (h + 1) * RET_D)
            xhh, rstd = parts[h]
            dxhh = dxh[:, sl]
            drets.append(rstd * (dxhh - jnp.mean(dxhh, axis=-1, keepdims=True)
                                 - xhh * jnp.mean(dxhh * xhh, axis=-1, keepdims=True)))
        dret_ = jnp.concatenate(drets, axis=1)
        dbg = jnp.sum(dgates, axis=0, keepdims=True)
        prod = ov.astype(F32) * do_.astype(BF16).astype(F32)
        lane = lax.broadcasted_iota(jnp.int32, (prod.shape[0], HP), 1)
        dlt = jnp.zeros((prod.shape[0], HP), F32)
        for h in range(MLA_H):
            dlt = jnp.where(lane == h, jnp.sum(prod[:, h * HP:(h + 1) * HP], axis=-1, keepdims=True), dlt)
        return do_, dret_, drg_, dgates, dlt, dwpa_, dwpr_, dgr, dbg

    do_a, dret, drg, dgates, delta, dwpa, dw_proj_ret, dg_ret, db_gate = rowwise(
        mix_bwd, [(o_a, None), (ret, None), (z, (512, 3)), (z, (2048, 1)), (dmix, None)], [wpa, wpr, g_ret, b_gate],
        [(MLA_H * HP, BF16), (512, F32), (512, BF16), (2 * D, BF16), (HP, F32)],
        [((MLA_H * HP, D), F32), ((512, D), F32), ((1, 512), F32), ((1, 2 * D), F32)], tile=tile2, name="mix_bwd")

    sl1 = slots_l1(dw_up_a, dw_up_b, dw_xkv, tile=tile)
    sl2, sl3 = slots_l23(dw_out, dw_xq, dw_xo, dw_down, dwpa, dw_proj_ret)
    delta_row = mla_prep(delta, tq=tq)
    dq_a, dk_a, dv_a, rl1, rl2, rl3 = mla_bwd(q_a, k_a, v_a, do_a, lse_row, delta_row, [sl1, sl2, sl3], tq=tq)
    drq_r, drk_r, drv = ret_bwd(rq, rk, rv, rstate, dret, rconsts, rb=rb)

    def mixers_in_bwd(zl, cmv, s1v, s2v, dqv, dkv_, dvv, drq_, drk_, drv_, drg_, dgt, crv, srv, gq, gkv, wqv, wkv_, wvv):
        cqf, ckvf = zl[:, 0:256], zl[:, 256:384]
        cq = _rms(cqf, gq).astype(BF16)
        ckv = _rms(ckvf, gkv).astype(BF16)
        dq_pre = _rot_mla(dqv.astype(F32), cmv, -s1v, -s2v).astype(BF16)
        dkf = dkv_.astype(F32)
        dkr = dkf[:, 0:HP]
        for h in range(1, MLA_H):
            dkr = dkr + dkf[:, h * HP:(h + 1) * HP]
        lane = lax.broadcasted_iota(jnp.int32, dkr.shape, 1)
        dzk = _rot_mla(jnp.where((lane >= 64) & (lane < 96), dkr, 0.0), cmv, -s1v, -s2v)
        dkb = dkv_.astype(BF16)
        dvb = dvv.astype(BF16)
        dcq_n = _dot_nt(dq_pre, wqv)
        dckv_n = _dot_nt(dkb, wkv_) + _dot_nt(dvb, wvv)
        dwq_ = _dot_tn(cq, dq_pre)
        dwk_ = _dot_tn(ckv, dkb)
        dwv_ = _dot_tn(ckv, dvb)
        dcq, dgq = _rms_bwd(cqf, gq, dcq_n)
        dckv, dgkv = _rms_bwd(ckvf, gkv, dckv_n)
        a = _rot_ret(drq_, crv, -srv)
        b = _rot_ret(drk_, crv, -srv) * (RET_D ** -0.5)
        dz_ = jnp.concatenate([a.astype(BF16), b.astype(BF16), drv_, drg_, dgt,
                               dcq.astype(BF16), dckv.astype(BF16), dzk.astype(BF16)], axis=1)
        return dz_, dwq_, dwk_, dwv_, dgq, dgkv

    dz, dwq, dwk, dwv, dg_q_lat, dg_kv_lat = rowwise(
        mixers_in_bwd, [(z, (512, 8)), (cm, None), (s1, None), (s2, None), (dq_a, None), (dk_a, None), (dv_a, None),
                        (drq_r, None), (drk_r, None), (drv, None), (drg, None), (dgates, None), (cr, None), (sr, None)],
        [g_q_lat, g_kv_lat, wq, wk, wv], [(ZW, BF16)],
        [((MLA_QR, MLA_H * HP), F32), ((MLA_KVR, MLA_H * HP), F32), ((MLA_KVR, MLA_H * HP), F32),
         ((1, MLA_QR), F32), ((1, MLA_KVR), F32)], tile=tile2, name="mixers_in_bwd")
    dwz = matmul_tn(u, dz, name="mm_dw_z", tn=1536)
    se1, se2 = slots_early(dwz, dwk, dwv, dwq, tile=tile)
    re1, re2 = sequencer_exchange([se1, se2])
    grad_x, dg_mix = rowwise(proj_norm_bwd, [(dz, None), (xs, None), (dh1, None)], [wz, g_mix], [(D, F32)],
                             [((1, D), F32)], tile=tile2, name="mm_du_norm_bwd")

    small_grads = {"g_mix": dg_mix, "b_gate": db_gate, "g_q_lat": dg_q_lat, "g_kv_lat": dg_kv_lat, "g_ret": dg_ret,
                   "g_cross": dg_cross, "g_mem": dg_mem, "g_ffn": dg_ffn, "g_final": dg_final}
    rd, rs, dslots, own_s = exchange_small(cs_a, cs_b, loss_acc, [small_grads[n] for n, _, _ in SMALL_DIRECT])

    me = _my_id()

    def own(slots):
        return lax.dynamic_index_in_dim(slots, me, axis=0, keepdims=False)

    def wmv(names):
        return [shard(n, p) for n in names for p in ("", "m_", "v_")]

    names_s = tuple(n for n, _, _ in SMALL)
    small_outs = adam_small(own_s, rs, dslots, rd, wmv(names_s), wmv(("w_conv",)))
    groups = (
        (("w_in",), adam_cols(own(se1), re1, wmv(("w_in",)), ((0, NS_IN),), name="adam_e1", tile=128)),
        (("w_ukv", "w_uq"), adam_rows(own(se2), re2, wmv(("w_ukv", "w_uq")),
                                      ((0, E2_UQ, HP), (E2_UQ, E2_ROWS, NS_UQ)), name="adam_e2")),
        (("w_up", "w_xkv"), adam_cols(own(sl1), rl1, wmv(("w_up", "w_xkv")), ((0, NS_UP), (NS_UP, L1_W)),
                                      name="adam_l1", tile=128)),
        (("w_out", "w_xq", "w_xo", "w_down"),
         adam_rows(own(sl2), rl2, wmv(("w_out", "w_xq", "w_xo", "w_down")),
                   ((L2_OUT, L2_XQ, D), (L2_XQ, L2_XO, D), (L2_XO, L2_DN, D), (L2_DN, L2_ROWS, D)), name="adam_l2")),
        (("w_proj_mla", "w_proj_ret"), adam_rows(own(sl3), rl3, wmv(("w_proj_mla", "w_proj_ret")),
                                                 ((0, L3_PRET, HP), (L3_PRET, L3_ROWS, HP)), name="adam_l3")),
        (names_s + ("w_conv",), small_outs),
    )
    loss = small_outs[-1][0, 0]
    res = {}
    for names, outs_ in groups:
        for t, n in enumerate(names):
            res[n] = outs_[4 * t:4 * t + 4]

    order = ["g_mix", "w_in", "b_gate", "g_q_lat", "w_uq", "g_kv_lat", "w_ukv", "w_proj_mla", "g_ret", "w_proj_ret",
             "w_out", "g_cross", "g_mem", "w_xq", "w_xkv", "w_xo", "g_ffn", "w_up", "w_conv", "b_conv", "w_down",
             "g_final"]
    outs = [loss, grad_x[None]]
    for kind in range(4):
        outs += [res[n][kind].reshape(args[n].shape) for n in order]
    return tuple(outs)
```

```python
import functools
import math

import jax
import jax.numpy as jnp
import numpy as np
from jax import lax
from jax.experimental import pallas as pl
from jax.experimental.pallas import tpu as pltpu
from jax.experimental.pallas import tpu_sc as plsc

F32 = jnp.float32
BF16 = jnp.bfloat16

D = 1024
MLA_H, MLA_NOPE, MLA_ROPE, MLA_V = 8, 64, 32, 64
MLA_QR, MLA_KVR = 256, 128
RET_H, RET_D, RET_C = 4, 128, 128
X_H, X_HD = 4, 256
D_FF = 2816
THETA = 10000.0
EPS = 1e-6
HP = 128
ZW = 4608
N_DEV = 8

ADAM_LR, ADAM_B1, ADAM_B2, ADAM_EPS, ADAM_WD, ADAM_STEP = 0.001, 0.9, 0.999, 1e-08, 0.01, 10

VMEM_LIMIT = 56 * 1024 * 1024
BIG_CONST_BYTES = 4 * 1024 * 1024
LATE_EXCHANGE_ID, EARLY_EXCHANGE_ID = 1, 2
MESH = pl.DeviceIdType.MESH
VM = pl.BlockSpec(memory_space=pltpu.VMEM)
ANY = pl.BlockSpec(memory_space=pl.ANY)


def _cp(n_axes):
    return pltpu.CompilerParams(dimension_semantics=("arbitrary",) * n_axes, vmem_limit_bytes=VMEM_LIMIT)


def _cp0():
    return pltpu.CompilerParams(vmem_limit_bytes=VMEM_LIMIT)


def _pick(n, cap, mult=128):
    best = None
    for t in range(mult, min(n, cap) + 1, mult):
        if n % t == 0:
            best = t
    return best if best is not None else n


def _dot(a, b):
    return jnp.dot(a, b, preferred_element_type=F32)


def _dot_nt(a, b):
    return lax.dot_general(a, b, (((1,), (1,)), ((), ())), preferred_element_type=F32)


def _dot_tn(a, b):
    return lax.dot_general(a, b, (((0,), (0,)), ((), ())), preferred_element_type=F32)


def _sds(shape, dtype):
    return jax.ShapeDtypeStruct(shape, dtype)


def matmul(a, b, *, name, tb=False, out_dtype=F32, tm=1024, tn=1024):
    M, K = a.shape
    N = b.shape[0] if tb else b.shape[1]
    tm = _pick(M, tm, 8)
    tn = _pick(N, tn)

    def body(a_ref, b_ref, o_ref):
        av = a_ref[...].astype(BF16)
        bv = b_ref[...].astype(BF16)
        acc = _dot_nt(av, bv) if tb else _dot(av, bv)
        o_ref[...] = acc.astype(o_ref.dtype)

    return pl.pallas_call(
        body,
        name=name,
        grid=(M // tm, N // tn),
        in_specs=[pl.BlockSpec((tm, K), lambda i, j: (i, 0)),
                  pl.BlockSpec((tn, K), lambda i, j: (j, 0)) if tb else pl.BlockSpec((K, tn), lambda i, j: (0, j))],
        out_specs=pl.BlockSpec((tm, tn), lambda i, j: (i, j)),
        out_shape=_sds((M, N), out_dtype),
        compiler_params=_cp(2),
    )(a, b)


def matmul_tn(a, b, *, name, tm=1024, tn=1024, tk=2048):
    R, M = a.shape
    N = b.shape[1]
    tm = _pick(M, tm)
    tn = _pick(N, tn)
    tk = _pick(R, tk, 16)
    nk = R // tk

    def body(a_ref, b_ref, o_ref, acc_ref):
        k = pl.program_id(2)

        @pl.when(k == 0)
        def _():
            acc_ref[...] = jnp.zeros_like(acc_ref)

        acc_ref[...] += _dot_tn(a_ref[...].astype(BF16), b_ref[...].astype(BF16))

        @pl.when(k == nk - 1)
        def _():
            o_ref[...] = acc_ref[...]

    return pl.pallas_call(
        body,
        name=name,
        grid=(M // tm, N // tn, nk),
        in_specs=[pl.BlockSpec((tk, tm), lambda i, j, k: (k, i)), pl.BlockSpec((tk, tn), lambda i, j, k: (k, j))],
        out_specs=pl.BlockSpec((tm, tn), lambda i, j, k: (i, j)),
        out_shape=_sds((M, N), F32),
        scratch_shapes=[pltpu.VMEM((tm, tn), F32)],
        compiler_params=_cp(3),
    )(a, b)


def rowwise(fn, rows, consts, out_rows, out_accs, *, tile, name, gather=(), exchange=()):
    T = rows[0][0].shape[0]
    nt = T // tile
    travel = tuple(gather) + tuple(exchange)
    n_r, n_c, n_o, n_a, n_g = len(rows), len(consts), len(out_rows), len(out_accs), len(travel)
    n_in = n_r + n_c + n_g

    def body(*refs):
        if n_g:
            srcs, dsts = refs[n_r + n_c:n_in], refs[n_in + n_o + n_a:n_in + n_o + n_a + n_g]
            sems = tuple(refs[n_in + n_o + n_a + n_g:])
            if gather:
                comm = (srcs, dsts) + sems
                comm_start, comm_finish = _gather2_start, _gather2_finish
            else:
                comm = ([(r, True) for r in srcs], dsts) + sems
                comm_start, comm_finish = _exchange_start, _exchange_wait

            @pl.when(pl.program_id(0) == 0)
            def _():
                comm_start(*comm)

        ins = [r[...] for r in refs[: n_r + n_c]]
        outs = fn(*ins)
        if not isinstance(outs, (tuple, list)):
            outs = (outs,)
        o_refs = refs[n_in:n_in + n_o]
        a_refs = refs[n_in + n_o:n_in + n_o + n_a]
        for o_ref, o in zip(o_refs, outs[:n_o]):
            o_ref[...] = o.astype(o_ref.dtype)
        if n_a:
            first = pl.program_id(0) == 0

            @pl.when(first)
            def _():
                for a_ref, o in zip(a_refs, outs[n_o:]):
                    a_ref[...] = o.astype(a_ref.dtype)

            @pl.when(jnp.logical_not(first))
            def _():
                for a_ref, o in zip(a_refs, outs[n_o:]):
                    a_ref[...] += o.astype(a_ref.dtype)
        if n_g:

            @pl.when(pl.program_id(0) == nt - 1)
            def _():
                comm_finish(*comm)

    in_specs = []
    args = []
    for arr, win in rows:
        if win is None:
            in_specs.append(pl.BlockSpec((tile, arr.shape[1]), lambda i: (i, 0)))
        else:
            w, cb = win
            in_specs.append(pl.BlockSpec((tile, w), functools.partial(lambda i, cb: (i, cb), cb=cb)))
        args.append(arr)
    for c in consts:
        index_map = functools.partial(lambda i, nd: (0,) * nd, nd=c.ndim)
        if c.size * c.dtype.itemsize >= BIG_CONST_BYTES:
            in_specs.append(pl.BlockSpec(c.shape, index_map, pipeline_mode=pl.Buffered(1)))
        else:
            in_specs.append(pl.BlockSpec(c.shape, index_map))
        args.append(c)
    out_specs = [pl.BlockSpec((tile, w), lambda i: (i, 0)) for w, _ in out_rows]
    out_shape = [_sds((T, w), dt) for w, dt in out_rows]
    for shp, dt in out_accs:
        out_specs.append(pl.BlockSpec(shp, functools.partial(lambda i, nd: (0,) * nd, nd=len(shp))))
        out_shape.append(_sds(shp, dt))
    return pl.pallas_call(
        body,
        name=name,
        grid=(nt,),
        in_specs=in_specs + [ANY] * n_g,
        out_specs=out_specs + [ANY] * n_g,
        out_shape=out_shape + [_sds((N_DEV,) + p.shape, p.dtype) for p in gather]
        + [_sds((N_DEV - 1,) + s.shape[1:], s.dtype) for s in exchange],
        scratch_shapes=(_gather_scratch(n_g) if gather else _exchange_scratch(n_g)) if n_g else [],
        compiler_params=_cp(1),
    )(*args, *travel)


def _rms(x, g):
    r = lax.rsqrt(jnp.mean(x * x, axis=-1, keepdims=True) + EPS)
    return x * r * g


def _rms_bwd(x, g, dy):
    r = lax.rsqrt(jnp.mean(x * x, axis=-1, keepdims=True) + EPS)
    xh = x * r
    dg = jnp.sum(dy * xh, axis=0, keepdims=True)
    dxh = dy * g
    dx = r * (dxh - xh * jnp.mean(dxh * xh, axis=-1, keepdims=True))
    return dx, dg


def _sigmoid(x):
    return 0.5 * jnp.tanh(0.5 * x) + 0.5


def _rot_mla(x, c, s1, s2):
    n = x.shape[1] // HP
    outs = []
    for h in range(n):
        xh = x[:, h * HP : (h + 1) * HP]
        outs.append(xh * c + pltpu.roll(xh, HP - 16, 1) * s1 + pltpu.roll(xh, 16, 1) * s2)
    return outs[0] if n == 1 else jnp.concatenate(outs, axis=1)


def _rot_ret(x, c, s):
    n = x.shape[1] // RET_D
    outs = []
    for h in range(n):
        xh = x[:, h * RET_D : (h + 1) * RET_D]
        outs.append(xh * c + pltpu.roll(xh, RET_D // 2, 1) * s)
    return outs[0] if n == 1 else jnp.concatenate(outs, axis=1)


def _rot_inv():
    lane_np = np.arange(128)
    inv_m = (jnp.asarray(THETA, F32) ** (-jnp.asarray(lane_np & 15, F32) / 16.0)).reshape(1, 128)
    inv_r = (jnp.asarray(THETA, F32) ** (-jnp.asarray(lane_np & 63, F32) / 64.0)).reshape(1, 128)
    return inv_m, inv_r


def _rot_tables(p, im, ir):
    lane = lax.broadcasted_iota(jnp.int32, p.shape, 1)
    ang = p * im
    cm = jnp.where((lane >= 64) & (lane < 96), jnp.cos(ang), 1.0)
    sn = jnp.sin(ang)
    s1 = jnp.where((lane >= 64) & (lane < 80), -sn, 0.0)
    s2 = jnp.where((lane >= 80) & (lane < 96), sn, 0.0)
    angr = p * ir
    snr = jnp.sin(angr)
    return cm, s1, s2, jnp.cos(angr), jnp.where(lane < 64, -snr, snr)


def _peer(m):
    x, y, c = lax.axis_index("x"), lax.axis_index("y"), lax.axis_index("c")
    mx, my, mc = (m >> 2) & 1, (m >> 1) & 1, m & 1
    px = 1 - x if mx else x
    py = 1 - y if my else y
    pc = 1 - c if mc else c
    return (px, py, pc), 4 * px + 2 * py + pc


def _my_id():
    return 4 * lax.axis_index("x") + 2 * lax.axis_index("y") + lax.axis_index("c")


def _gather_copies(srcs, outs, send_sems, recv_sems, local_sems, arriving=False):
    me = _my_id()
    copies = []
    if not arriving:
        for g, (src, out) in enumerate(zip(srcs, outs)):
            copies.append((pltpu.make_async_copy(src, out.at[me], local_sems.at[g]), False))
    for m in range(1, N_DEV):
        peer, plin = _peer(m)
        for g, (src, out) in enumerate(zip(srcs, outs)):
            copies.append((pltpu.make_async_remote_copy(
                src_ref=src, dst_ref=out.at[plin if arriving else me], send_sem=send_sems.at[g, m - 1],
                recv_sem=recv_sems.at[g, m - 1], device_id=peer, device_id_type=MESH), True))
    return copies


def _gather_start(*a):
    for cp, _ in _gather_copies(*a):
        cp.start()


def _gather_wait(*a):
    for cp, _ in _gather_copies(*a, arriving=True):
        cp.wait_recv()
    for cp, remote in _gather_copies(*a):
        if remote:
            cp.wait_send()
        else:
            cp.wait()


CHIP_RELATIONS = (4, 2, 6)


def _gather2_copy(src, out, block, to, send_sems, recv_sems, g, k):
    return pltpu.make_async_remote_copy(src_ref=src, dst_ref=out.at[block], send_sem=send_sems.at[g, k],
                                        recv_sem=recv_sems.at[g, k], device_id=to, device_id_type=MESH)


def _gather2_start(srcs, outs, send_sems, recv_sems, local_sems):
    me = _my_id()
    sib, _ = _peer(1)
    for g, (src, out) in enumerate(zip(srcs, outs)):
        pltpu.make_async_copy(src, out.at[me], local_sems.at[g]).start()
        _gather2_copy(src, out, me, sib, send_sems, recv_sems, g, 0).start()
        for t, m in enumerate(CHIP_RELATIONS):
            _gather2_copy(src, out, me, _peer(m)[0], send_sems, recv_sems, g, 1 + t).start()


def _gather2_finish(srcs, outs, send_sems, recv_sems, local_sems):
    me = _my_id()
    sib, sib_lin = _peer(1)
    for t, m in enumerate(CHIP_RELATIONS):
        peer, plin = _peer(m)
        for g, (src, out) in enumerate(zip(srcs, outs)):
            _gather2_copy(src, out, plin, peer, send_sems, recv_sems, g, 1 + t).wait_recv()
            _gather2_copy(out.at[plin], out, plin, sib, send_sems, recv_sems, g, 4 + t).start()
    for g, (src, out) in enumerate(zip(srcs, outs)):
        _gather2_copy(src, out, sib_lin, sib, send_sems, recv_sems, g, 0).wait_recv()
        for t, m in enumerate(CHIP_RELATIONS):
            _gather2_copy(src, out, _peer(m | 1)[1], sib, send_sems, recv_sems, g, 4 + t).wait_recv()
    for g, (src, out) in enumerate(zip(srcs, outs)):
        _gather2_copy(src, out, me, sib, send_sems, recv_sems, g, 0).wait_send()
        for t, m in enumerate(CHIP_RELATIONS):
            peer, plin = _peer(m)
            _gather2_copy(src, out, me, peer, send_sems, recv_sems, g, 1 + t).wait_send()
            _gather2_copy(out.at[plin], out, plin, sib, send_sems, recv_sems, g, 4 + t).wait_send()
        pltpu.make_async_copy(src, out.at[me], local_sems.at[g]).wait()


def _gather_scratch(n):
    return [pltpu.SemaphoreType.DMA((n, N_DEV - 1)), pltpu.SemaphoreType.DMA((n, N_DEV - 1)), pltpu.SemaphoreType.DMA((n,))]


def _exchange_copies(srcs, dsts, send_sems, recv_sems):
    copies = []
    for m in range(1, N_DEV):
        peer, plin = _peer(m)
        for g, ((src, per_peer), dst) in enumerate(zip(srcs, dsts)):
            copies.append(pltpu.make_async_remote_copy(
                src_ref=src.at[plin] if per_peer else src, dst_ref=dst.at[m - 1], send_sem=send_sems.at[g, m - 1],
                recv_sem=recv_sems.at[g, m - 1], device_id=peer, device_id_type=MESH))
    return copies


def _exchange_start(*a):
    for cp in _exchange_copies(*a):
        cp.start()


def _exchange_wait(*a):
    copies = _exchange_copies(*a)
    for cp in copies:
        cp.wait_recv()
    for cp in copies:
        cp.wait_send()


def _exchange_scratch(n):
    return [pltpu.SemaphoreType.DMA((n, N_DEV - 1)), pltpu.SemaphoreType.DMA((n, N_DEV - 1))]


def sequencer_exchange(slots, *, name, collective_id):
    n = len(slots)
    src_refs = [jax.new_ref(s, memory_space=pltpu.MemorySpace.HBM) for s in slots]
    got_refs = [jax.empty_ref(_sds((N_DEV - 1,) + s.shape[1:], s.dtype), memory_space=pltpu.MemorySpace.HBM)
                for s in slots]

    @pl.kernel(mesh=plsc.ScalarSubcoreMesh(axis_name="sequencer", num_cores=1), name=name,
               scratch_types=tuple(pltpu.SemaphoreType.DMA for _ in range(2 * n * (N_DEV - 1))),
               compiler_params=pltpu.CompilerParams(collective_id=collective_id))
    def launch(*sems):
        barrier = pltpu.get_barrier_semaphore()
        for m in range(1, N_DEV):
            pl.semaphore_signal(barrier, inc=1, device_id=_peer(m)[0], device_id_type=MESH)
        pl.semaphore_wait(barrier, N_DEV - 1)
        copies = []
        for m in range(1, N_DEV):
            peer, plin = _peer(m)
            for g in range(n):
                k = 2 * (g * (N_DEV - 1) + m - 1)
                copies.append(pltpu.make_async_remote_copy(
                    src_ref=src_refs[g].at[plin], dst_ref=got_refs[g].at[m - 1], send_sem=sems[k], recv_sem=sems[k + 1],
                    device_id=peer, device_id_type=MESH))
        for cp in copies:
            cp.start()
        for cp in copies:
            cp.wait_recv()
        for cp in copies:
            cp.wait_send()

    launch()
    return [g[...] for g in got_refs]


MLA_SCALE = (MLA_NOPE + MLA_ROPE) ** -0.5
MLA_C2 = MLA_SCALE * math.log2(math.e)
ONE_LANE = MLA_V
NEG = -1e30
HPS = 4
HPB = 4


def _tri_mask(n, lower_rows_ge_cols=True):
    r = lax.broadcasted_iota(jnp.int32, (n, n), 0)
    c = lax.broadcasted_iota(jnp.int32, (n, n), 1)
    return r >= c if lower_rows_ge_cols else c >= r


def mla_fwd(q, k, v, gather, *, tq):
    T = q.shape[0]
    nq = T // tq
    rep = tq // HP
    ng = len(gather)

    def body(*refs):
        q_ref, k_ref, v_ref = refs[:3]
        srcs = refs[3:3 + ng]
        o_ref, lse_ref = refs[3 + ng:5 + ng]
        outs = refs[5 + ng:5 + 2 * ng]
        m_sc, acc_sc = refs[5 + 2 * ng:7 + 2 * ng]
        comm = (srcs, outs) + tuple(refs[7 + 2 * ng:])
        h, i = pl.program_id(0), pl.program_id(1)

        @pl.when((h == 0) & (i == 0))
        def _():
            _gather_start(*comm)

        m_sc[...] = jnp.full(m_sc.shape, NEG, F32)
        acc_sc[...] = jnp.zeros(acc_sc.shape, F32)
        heads = [slice(t * HP, (t + 1) * HP) for t in range(HPS)]

        def block(j, masked):
            off = pl.multiple_of(j * tq, tq)
            ss = [_dot_nt(q_ref[:, hd], k_ref[pl.ds(off, tq), hd]) * MLA_C2 for hd in heads]
            mask = _tri_mask(tq) if masked else None
            for hd, s in zip(heads, ss):
                if masked:
                    s = jnp.where(mask, s, NEG)
                m_prev = m_sc[:, hd]
                m_next = jnp.maximum(m_prev, jnp.max(s, axis=-1, keepdims=True))
                p = jnp.exp2(s - jnp.tile(m_next, (1, rep)))
                alpha = jnp.exp2(m_prev - m_next)
                acc_sc[:, hd] = alpha * acc_sc[:, hd] + _dot(p.astype(BF16), v_ref[pl.ds(off, tq), hd])
                m_sc[:, hd] = m_next

        def loop_body(p, carry):
            block(2 * p, False)
            block(2 * p + 1, False)
            return carry

        lax.fori_loop(0, i // 2, loop_body, 0)

        @pl.when(i % 2 == 1)
        def _():
            block(i - 1, False)

        block(i, True)
        lane = lax.broadcasted_iota(jnp.int32, (tq, HP), 1)
        lse_cols = jnp.zeros((tq, HP), F32)
        for t, hd in enumerate(heads):
            acc = acc_sc[:, hd]
            l = acc[:, ONE_LANE:ONE_LANE + 1]
            o_ref[:, hd] = (acc / l).astype(o_ref.dtype)
            lse_cols = jnp.where(lane == t, m_sc[:, hd] + jnp.log(l) * math.log2(math.e), lse_cols)
        lse_rows = lse_cols.T
        for t in range(HPS):
            lse_ref[t] = lse_rows[t:t + 1, :]

        @pl.when((h == MLA_H // HPS - 1) & (i == nq - 1))
        def _():
            _gather_wait(*comm)

    blk = pl.BlockSpec((tq, HPS * HP), lambda h, i: (i, h))
    full = pl.BlockSpec((T, HPS * HP), lambda h, i: (0, h))
    return pl.pallas_call(
        body,
        name="mla_fwd",
        grid=(MLA_H // HPS, nq),
        in_specs=[blk, full, full] + [ANY] * ng,
        out_specs=[blk, pl.BlockSpec((HPS, None, 1, tq), lambda h, i: (h, i, 0, 0))] + [ANY] * ng,
        out_shape=[_sds((T, MLA_H * HP), BF16), _sds((MLA_H, nq, 1, tq), F32)]
        + [_sds((N_DEV,) + p.shape, p.dtype) for p in gather],
        scratch_shapes=[pltpu.VMEM((tq, HPS * HP), F32), pltpu.VMEM((tq, HPS * HP), F32)] + _gather_scratch(ng),
        compiler_params=_cp(2),
    )(q, k, v, *gather)


def mla_prep(delta, *, tq):
    T = delta.shape[0]
    nq = T // tq

    def body(d_ref, row_ref):
        dt = d_ref[...].T
        for h in range(MLA_H):
            row_ref[h] = dt[h:h + 1, :]

    return pl.pallas_call(
        body,
        name="mla_prep",
        grid=(nq,),
        in_specs=[pl.BlockSpec((tq, HP), lambda i: (i, 0))],
        out_specs=pl.BlockSpec((MLA_H, None, 1, tq), lambda i: (0, i, 0, 0)),
        out_shape=_sds((MLA_H, nq, 1, tq), F32),
        compiler_params=_cp(1),
    )(delta)


def mla_bwd(q, k, v, do, lse_row, delta_row, slots, *, tq):
    T = q.shape[0]
    nq = T // tq
    ns = len(slots)

    def body(*refs):
        q_ref, k_ref, v_ref, do_ref, lse_ref, delta_ref = refs[:6]
        srcs = [(r, True) for r in refs[6:6 + ns]]
        dq_ref, dk_ref, dv_ref = refs[6 + ns:9 + ns]
        dsts = refs[9 + ns:9 + 2 * ns]
        dk_sc, dv_sc = refs[9 + 2 * ns:11 + 2 * ns]
        comm = (srcs, dsts) + tuple(refs[11 + 2 * ns:])
        h, j = pl.program_id(0), pl.program_id(1)
        if ns:

            @pl.when((h == 0) & (j == 0))
            def _():
                _exchange_start(*comm)

        dk_sc[...] = jnp.zeros(dk_sc.shape, F32)
        dv_sc[...] = jnp.zeros(dv_sc.shape, F32)
        heads = [slice(t * HP, (t + 1) * HP) for t in range(HPB)]

        @pl.when(j == 0)
        def _():
            dq_ref[...] = jnp.zeros(dq_ref.shape, F32)

        def block(i, masked):
            off = pl.multiple_of(i * tq, tq)
            sts = [_dot_nt(k_ref[:, hd], q_ref[pl.ds(off, tq), hd]) * MLA_C2 for hd in heads]
            dpts = [_dot_nt(v_ref[:, hd], do_ref[pl.ds(off, tq), hd]) for hd in heads]
            mask = _tri_mask(tq, False) if masked else None
            for t, hd in enumerate(heads):
                st = sts[t]
                if masked:
                    st = jnp.where(mask, st, NEG)
                pt = jnp.exp2(st - lse_ref[t, i])
                dv_sc[:, hd] += _dot(pt.astype(BF16), do_ref[pl.ds(off, tq), hd])
                dst = (pt * (dpts[t] - delta_ref[t, i]) * MLA_SCALE).astype(BF16)
                dk_sc[:, hd] += _dot(dst, q_ref[pl.ds(off, tq), hd])
                dq_ref[pl.ds(off, tq), hd] += _dot_tn(dst, k_ref[:, hd])

        block(j, True)

        def loop_body(p, carry):
            block(j + 1 + 2 * p, False)
            block(j + 2 + 2 * p, False)
            return carry

        rest = nq - 1 - j
        lax.fori_loop(0, rest // 2, loop_body, 0)

        @pl.when(rest % 2 == 1)
        def _():
            block(nq - 1, False)

        dk_ref[...] = dk_sc[...].astype(dk_ref.dtype)
        dv_ref[...] = dv_sc[...].astype(dv_ref.dtype)
        if ns:

            @pl.when((h == MLA_H // HPB - 1) & (j == nq - 1))
            def _():
                _exchange_wait(*comm)

    blk = pl.BlockSpec((tq, HPB * HP), lambda h, j: (j, h))
    full = pl.BlockSpec((T, HPB * HP), lambda h, j: (0, h), pipeline_mode=pl.Buffered(1))
    rows = pl.BlockSpec((HPB, nq, 1, tq), lambda h, j: (h, 0, 0, 0))
    return pl.pallas_call(
        body,
        name="mla_bwd",
        grid=(MLA_H // HPB, nq),
        in_specs=[full, blk, blk, full, rows, rows] + [ANY] * ns,
        out_specs=[full, blk, blk] + [ANY] * ns,
        out_shape=[_sds((T, MLA_H * HP), F32), _sds((T, MLA_H * HP), BF16), _sds((T, MLA_H * HP), BF16)]
        + [_sds((N_DEV - 1,) + s.shape[1:], s.dtype) for s in slots],
        scratch_shapes=[pltpu.VMEM((tq, HPB * HP), F32), pltpu.VMEM((tq, HPB * HP), F32)]
        + (_exchange_scratch(ns) if ns else []),
        compiler_params=_cp(2),
    )(q, k, v, do, lse_row, delta_row, *slots)


def _ret_consts():
    h = jnp.arange(RET_H, dtype=F32)
    log_g = jnp.log1p(-jnp.exp2(-5.0 - h))
    idx = jnp.arange(RET_C, dtype=F32)
    rel = idx[:, None] - idx[None, :]
    dmask = jnp.where(rel >= 0, jnp.exp(log_g[:, None, None] * jnp.maximum(rel, 0.0)), 0.0)
    zeta = jnp.exp(log_g[:, None] * (RET_C - 1.0 - idx)[None, :])
    xi = jnp.exp(log_g[:, None] * (idx + 1.0)[None, :])
    decay = jnp.exp(log_g * RET_C)
    zb = jnp.broadcast_to(zeta[:, :, None], (RET_H, RET_C, RET_D))
    xb = jnp.broadcast_to(xi[:, :, None], (RET_H, RET_C, RET_D))
    db = jnp.broadcast_to(decay[:, None, None], (RET_H, RET_C, RET_D))
    return dmask.astype(F32), zb.astype(F32), xb.astype(F32), db.astype(F32)


def ret_fwd(rq, rk, rv, consts, *, rb):
    T = rq.shape[0]
    nb = T // rb
    ncb = rb // RET_C

    def body(q_ref, k_ref, v_ref, dm_ref, z_ref, x_ref, dc_ref, o_ref, st_ref, r_sc):
        @pl.when(pl.program_id(0) == 0)
        def _():
            r_sc[...] = jnp.zeros(r_sc.shape, F32)

        for c in range(ncb):
            sl = slice(c * RET_C, (c + 1) * RET_C)
            for h in range(RET_H):
                hd = slice(h * RET_D, (h + 1) * RET_D)
                q, k, v = q_ref[sl, hd], k_ref[sl, hd], v_ref[sl, hd]
                r = r_sc[h]
                rbf = r.astype(BF16)
                st_ref[sl, hd] = rbf
                s = _dot_nt(q, k) * dm_ref[h]
                inner = _dot(s.astype(BF16), v)
                cross = _dot((q.astype(F32) * x_ref[h]).astype(BF16), rbf)
                o_ref[sl, hd] = inner + cross
                kz = (k.astype(F32) * z_ref[h]).T.astype(BF16)
                r_sc[h] = r * dc_ref[h] + _dot(kz, v)

    blk = pl.BlockSpec((rb, RET_H * RET_D), lambda b: (b, 0))
    cst = pl.BlockSpec((RET_H, RET_C, RET_D), lambda b: (0, 0, 0))
    return pl.pallas_call(
        body,
        name="ret_fwd",
        grid=(nb,),
        in_specs=[blk, blk, blk, cst, cst, cst, cst],
        out_specs=[blk, blk],
        out_shape=[_sds((T, RET_H * RET_D), F32), _sds((T, RET_H * RET_D), BF16)],
        scratch_shapes=[pltpu.VMEM((RET_H, RET_D, RET_D), F32)],
        compiler_params=_cp(1),
    )(rq, rk, rv, *consts)


def ret_bwd(rq, rk, rv, st, dret, consts, *, rb):
    T = rq.shape[0]
    nb = T // rb
    ncb = rb // RET_C

    def body(q_ref, k_ref, v_ref, st_ref, do_ref, dm_ref, z_ref, x_ref, dc_ref, dq_ref, dk_ref, dv_ref, g_sc):
        @pl.when(pl.program_id(0) == 0)
        def _():
            g_sc[...] = jnp.zeros(g_sc.shape, F32)

        for c in reversed(range(ncb)):
            sl = slice(c * RET_C, (c + 1) * RET_C)
            for h in range(RET_H):
                hd = slice(h * RET_D, (h + 1) * RET_D)
                dm, zt, xi = dm_ref[h], z_ref[h], x_ref[h]
                q, k, v, rp = q_ref[sl, hd], k_ref[sl, hd], v_ref[sl, hd], st_ref[sl, hd]
                dob = do_ref[sl, hd].astype(BF16)
                qf, kf = q.astype(F32), k.astype(F32)
                gn = g_sc[h]
                gnb = gn.astype(BF16)
                s = _dot_nt(q, k) * dm
                ds = _dot_nt(dob, v) * dm
                dq = _dot(ds.astype(BF16), k) + _dot_nt(dob, rp) * xi
                dk = _dot(ds.T.astype(BF16), q) + _dot_nt(v, gnb) * zt
                dv = _dot(s.T.astype(BF16), dob) + _dot((kf * zt).astype(BF16), gnb)
                dq_ref[sl, hd] = dq.astype(dq_ref.dtype)
                dk_ref[sl, hd] = dk.astype(dk_ref.dtype)
                dv_ref[sl, hd] = dv.astype(dv_ref.dtype)
                g_sc[h] = _dot((qf * xi).T.astype(BF16), dob) + dc_ref[h] * gn

    blk = pl.BlockSpec((rb, RET_H * RET_D), lambda b: (nb - 1 - b, 0))
    cst = pl.BlockSpec((RET_H, RET_C, RET_D), lambda b: (0, 0, 0))
    return pl.pallas_call(
        body,
        name="ret_bwd",
        grid=(nb,),
        in_specs=[blk, blk, blk, blk, blk, cst, cst, cst, cst],
        out_specs=[blk, blk, blk],
        out_shape=[_sds((T, RET_H * RET_D), F32), _sds((T, RET_H * RET_D), F32), _sds((T, RET_H * RET_D), BF16)],
        scratch_shapes=[pltpu.VMEM((RET_H, RET_D, RET_D), F32)],
        compiler_params=_cp(1),
    )(rq, rk, rv, st, dret, *consts)


HALO = 16
EDGE = 8


def conv_act_fwd(up_pre, w_conv, b_conv, *, tile, cw):
    T = up_pre.shape[0]
    nt = T // tile
    ncol = D_FF // cw
    hb = tile // HALO

    def body(pa_ref, a_ref, pb_ref, b_ref, wa_ref, wb_ref, ba_ref, bb_ref, o_ref):
        i = pl.program_id(1)
        keep = (i > 0).astype(F32)

        def conv(prev_ref, cur_ref, w_ref, bias_ref):
            ext = jnp.concatenate([prev_ref[...].astype(F32)[HALO - EDGE:, :] * keep, cur_ref[...].astype(F32)], axis=0)
            w = w_ref[...]
            y = ext * w[2:3, :] + pltpu.roll(ext, 1, 0) * w[1:2, :] + pltpu.roll(ext, 2, 0) * w[0:1, :] + bias_ref[...]
            return y[EDGE:, :]

        a = conv(pa_ref, a_ref, wa_ref, ba_ref)
        b = conv(pb_ref, b_ref, wb_ref, bb_ref)
        o_ref[...] = (a * _sigmoid(a) * b).astype(o_ref.dtype)

    prev_a = pl.BlockSpec((HALO, cw), lambda j, i: (jnp.maximum(i * hb - 1, 0), j))
    cur_a = pl.BlockSpec((tile, cw), lambda j, i: (i, j))
    prev_b = pl.BlockSpec((HALO, cw), lambda j, i: (jnp.maximum(i * hb - 1, 0), j + ncol))
    cur_b = pl.BlockSpec((tile, cw), lambda j, i: (i, j + ncol))
    w_a = pl.BlockSpec((3, cw), lambda j, i: (0, j))
    w_b = pl.BlockSpec((3, cw), lambda j, i: (0, j + ncol))
    bias_a = pl.BlockSpec((1, cw), lambda j, i: (0, j))
    bias_b = pl.BlockSpec((1, cw), lambda j, i: (0, j + ncol))
    return pl.pallas_call(
        body,
        name="conv_act_fwd",
        grid=(ncol, nt),
        in_specs=[prev_a, cur_a, prev_b, cur_b, w_a, w_b, bias_a, bias_b],
        out_specs=pl.BlockSpec((tile, cw), lambda j, i: (i, j)),
        out_shape=_sds((T, D_FF), BF16),
        compiler_params=_cp(2),
    )(up_pre, up_pre, up_pre, up_pre, w_conv, w_conv, b_conv, b_conv)


def conv_act_bwd(up_pre, dact, w_conv, b_conv, *, tile, cw):
    T = up_pre.shape[0]
    nt = T // tile
    ncol = D_FF // cw
    hb = tile // HALO
    ext_rows = tile + 2 * EDGE

    def body(pa_ref, a_ref, na_ref, pb_ref, b_ref, nb_ref, d_ref, nd_ref, wa_ref, wb_ref, ba_ref, bb_ref,
             dxa_ref, dxb_ref, sa_ref, sb_ref):
        i = pl.program_id(1)
        keep_p = (i > 0).astype(F32)
        keep_n = (i < nt - 1).astype(F32)

        def ext_of(prev_ref, cur_ref, next_ref):
            return jnp.concatenate(
                [prev_ref[...].astype(F32)[HALO - EDGE:, :] * keep_p, cur_ref[...].astype(F32),
                 next_ref[...].astype(F32)[:EDGE, :] * keep_n], axis=0)

        def taps(ext):
            return ext, pltpu.roll(ext, 1, 0), pltpu.roll(ext, 2, 0)

        def conv(tp, w, bias):
            return tp[0] * w[2:3, :] + tp[1] * w[1:2, :] + tp[2] * w[0:1, :] + bias

        xa = taps(ext_of(pa_ref, a_ref, na_ref))
        xb = taps(ext_of(pb_ref, b_ref, nb_ref))
        wa, wb = wa_ref[...], wb_ref[...]
        a = conv(xa, wa, ba_ref[...])
        b = conv(xb, wb, bb_ref[...])
        dy = jnp.concatenate(
            [jnp.zeros((EDGE, cw), F32), d_ref[...].astype(F32), nd_ref[...].astype(F32)[:EDGE, :] * keep_n], axis=0)
        sg = _sigmoid(a)
        da = dy * b * (sg * (1.0 + a * (1.0 - sg)))
        db = dy * (a * sg)

        def back(dup, tp, w, dx_ref, s_ref):
            dx = dup * w[2:3, :] + pltpu.roll(dup, ext_rows - 1, 0) * w[1:2, :] + pltpu.roll(dup, ext_rows - 2, 0) * w[0:1, :]
            dx_ref[...] = dx[EDGE:EDGE + tile, :].astype(dx_ref.dtype)
            dc = dup[EDGE:EDGE + tile, :]
            r2 = jnp.sum(dc * tp[0][EDGE:EDGE + tile, :], axis=0, keepdims=True)
            r1 = jnp.sum(dc * tp[1][EDGE:EDGE + tile, :], axis=0, keepdims=True)
            r0 = jnp.sum(dc * tp[2][EDGE:EDGE + tile, :], axis=0, keepdims=True)
            rb = jnp.sum(dc, axis=0, keepdims=True)
            row = lax.broadcasted_iota(jnp.int32, (8, cw), 0)
            upd = (jnp.where(row == 0, r0, 0.0) + jnp.where(row == 1, r1, 0.0) + jnp.where(row == 2, r2, 0.0)
                   + jnp.where(row == 3, rb, 0.0))

            @pl.when(i == 0)
            def _():
                s_ref[...] = upd

            @pl.when(i > 0)
            def _():
                s_ref[...] += upd

        back(da, xa, wa, dxa_ref, sa_ref)
        back(db, xb, wb, dxb_ref, sb_ref)

    def prev_of(shift):
        return pl.BlockSpec((HALO, cw), lambda j, i: (jnp.maximum(i * hb - 1, 0), j + shift))

    def next_of(shift):
        return pl.BlockSpec((HALO, cw), lambda j, i: (jnp.minimum((i + 1) * hb, nt * hb - 1), j + shift))

    def cur_of(shift):
        return pl.BlockSpec((tile, cw), lambda j, i: (i, j + shift))

    def row_of(rows, shift):
        return pl.BlockSpec((rows, cw), lambda j, i: (0, j + shift))

    return pl.pallas_call(
        body,
        name="conv_act_bwd",
        grid=(ncol, nt),
        in_specs=[prev_of(0), cur_of(0), next_of(0), prev_of(ncol), cur_of(ncol), next_of(ncol), cur_of(0), next_of(0),
                  row_of(3, 0), row_of(3, ncol), row_of(1, 0), row_of(1, ncol)],
        out_specs=[cur_of(0), cur_of(0), row_of(8, 0), row_of(8, 0)],
        out_shape=[_sds((T, D_FF), BF16), _sds((T, D_FF), BF16), _sds((8, D_FF), F32), _sds((8, D_FF), F32)],
        compiler_params=_cp(2),
    )(up_pre, up_pre, up_pre, up_pre, up_pre, up_pre, dact, dact, w_conv, w_conv, b_conv, b_conv)


NS_IN, NS_UP, NS_XKV, NS_UQ = 564, 704, 256, 96
E2_ROWS, E2_UQ = 384, 128
L1_W = NS_UP + NS_XKV
L2_ROWS = 736
L2_OUT, L2_XQ, L2_XO, L2_DN = 0, 128, 256, 384
RS_DN = 352
L3_ROWS, L3_PRET = 1024, 512
L4_SHAPE = (8, 768)
WZ_RUNS = ((416, 4096, 0), (0, 384, 4096), (384, 32, 4544))
WZ_ZERO = ((4480, 4544), (4576, 4608))


def _pieces(orig_start, length, dst_start, ns):
    out, c, d, end = [], orig_start, dst_start, orig_start + length
    while c < end:
        j, off = c // ns, c % ns
        ln = min(ns - off, end - c)
        out.append((j, off, d, ln))
        c += ln
        d += ln
    return out


def pack_early(w_in, w_ukv, w_uq):
    def body(in_ref, ukv_ref, uq_ref, e1_ref, e2_ref):
        e1_ref[...] = in_ref[...].astype(BF16)
        e2_ref[0:E2_UQ, :] = ukv_ref[...].astype(BF16)
        e2_ref[E2_UQ:E2_ROWS, 0:NS_UQ] = uq_ref[...].astype(BF16)
        e2_ref[E2_UQ:E2_ROWS, NS_UQ:HP] = jnp.zeros((E2_ROWS - E2_UQ, HP - NS_UQ), BF16)

    return pl.pallas_call(
        body,
        name="pack_early",
        in_specs=[VM] * 3,
        out_specs=[VM] * 2,
        out_shape=[_sds((D, NS_IN), BF16), _sds((E2_ROWS, HP), BF16)],
        compiler_params=_cp0(),
    )(w_in, w_ukv, w_uq)


def pack_late(w_up, w_xkv, w_out, w_xq, w_xo, w_down, w_pmla, w_pret, w_conv):
    def body(up_ref, xkv_ref, o_ref, xq_ref, xo_ref, dn_ref, pm_ref, pr_ref, cv_ref, l1_ref, l2_ref, l3_ref, l4_ref):
        l1_ref[:, 0:NS_UP] = up_ref[...].astype(BF16)
        l1_ref[:, NS_UP:L1_W] = xkv_ref[...].astype(BF16)
        l2_ref[L2_OUT:L2_XQ, :] = o_ref[...].astype(BF16)
        l2_ref[L2_XQ:L2_XO, :] = xq_ref[...].astype(BF16)
        l2_ref[L2_XO:L2_DN, :] = xo_ref[...].astype(BF16)
        l2_ref[L2_DN:L2_ROWS, :] = dn_ref[...].astype(BF16)
        l3_ref[0:L3_PRET, :] = pm_ref[...].astype(BF16)
        l3_ref[L3_PRET:L3_ROWS, :] = pr_ref[...].astype(BF16)
        l4_ref[...] = jnp.zeros(L4_SHAPE, F32)
        l4_ref[0:3, 0:NS_UP] = cv_ref[...]

    return pl.pallas_call(
        body,
        name="pack_late",
        in_specs=[VM] * 9,
        out_specs=[VM] * 4,
        out_shape=[_sds((D, L1_W), BF16), _sds((L2_ROWS, D), BF16), _sds((L3_ROWS, HP), BF16), _sds(L4_SHAPE, F32)],
        compiler_params=_cp0(),
    )(w_up, w_xkv, w_out, w_xq, w_xo, w_down, w_pmla, w_pret, w_conv)


def assemble_early(g1, g2, *, tile):
    def body(g1_ref, g2_ref, wz_ref, wk_ref, wv_ref, wq_ref):
        for lo, hi in WZ_ZERO:
            wz_ref[:, lo:hi] = jnp.zeros((tile, hi - lo), BF16)
        for os_, ln_, ds_ in WZ_RUNS:
            for j, off, d, ln in _pieces(os_, ln_, ds_, NS_IN):
                wz_ref[:, d:d + ln] = g1_ref[j, :, off:off + ln]

        @pl.when(pl.program_id(0) == 0)
        def _():
            half = jnp.zeros((MLA_KVR, HP - MLA_NOPE), BF16)
            for j in range(N_DEV):
                wk_ref[:, j * HP:j * HP + MLA_NOPE] = g2_ref[j, 0:E2_UQ, 0:MLA_NOPE]
                wk_ref[:, j * HP + MLA_NOPE:(j + 1) * HP] = half
                wv_ref[:, j * HP:j * HP + MLA_V] = g2_ref[j, 0:E2_UQ, MLA_NOPE:HP]
                wv_ref[:, j * HP + MLA_V:(j + 1) * HP] = half
                wq_ref[:, j * HP:j * HP + NS_UQ] = g2_ref[j, E2_UQ:E2_ROWS, 0:NS_UQ]
                wq_ref[:, j * HP + NS_UQ:(j + 1) * HP] = jnp.zeros((MLA_QR, HP - NS_UQ), BF16)

    def whole(r):
        return pl.BlockSpec((r, MLA_H * HP), lambda i: (0, 0))

    return pl.pallas_call(
        body,
        name="assemble_early",
        grid=(D // tile,),
        in_specs=[pl.BlockSpec((N_DEV, tile, NS_IN), lambda i: (0, i, 0)),
                  pl.BlockSpec((N_DEV, E2_ROWS, HP), lambda i: (0, 0, 0))],
        out_specs=[pl.BlockSpec((tile, ZW), lambda i: (i, 0)), whole(MLA_KVR), whole(MLA_KVR), whole(MLA_QR)],
        out_shape=[_sds((D, ZW), BF16), _sds((MLA_KVR, MLA_H * HP), BF16), _sds((MLA_KVR, MLA_H * HP), BF16),
                   _sds((MLA_QR, MLA_H * HP), BF16)],
        compiler_params=_cp(1),
    )(g1, g2)


def assemble_l1(g1, *, tile):
    def body(g_ref, wup_ref, wxkv_ref):
        for j in range(N_DEV):
            wup_ref[:, j * NS_UP:(j + 1) * NS_UP] = g_ref[j, :, 0:NS_UP]
            wxkv_ref[:, j * NS_XKV:(j + 1) * NS_XKV] = g_ref[j, :, NS_UP:L1_W]

    return pl.pallas_call(
        body,
        name="assemble_l1",
        grid=(D // tile,),
        in_specs=[pl.BlockSpec((N_DEV, tile, L1_W), lambda i: (0, i, 0))],
        out_specs=[pl.BlockSpec((tile, 2 * D_FF), lambda i: (i, 0)), pl.BlockSpec((tile, 2 * D), lambda i: (i, 0))],
        out_shape=[_sds((D, 2 * D_FF), BF16), _sds((D, 2 * D), BF16)],
        compiler_params=_cp(1),
    )(g1)


def assemble_l234(g2, g3, g4):
    def body(g2_ref, g3_ref, g4_ref, wo_ref, wxq_ref, wxo_ref, wdn_ref, wpa_ref, wpr_ref, wc_ref):
        for j in range(N_DEV):
            wo_ref[j * 128:(j + 1) * 128, :] = g2_ref[j, L2_OUT:L2_XQ, :]
            wxq_ref[j * 128:(j + 1) * 128, :] = g2_ref[j, L2_XQ:L2_XO, :]
            wxo_ref[j * 128:(j + 1) * 128, :] = g2_ref[j, L2_XO:L2_DN, :]
            wdn_ref[j * RS_DN:(j + 1) * RS_DN, :] = g2_ref[j, L2_DN:L2_ROWS, :]
            wpr_ref[:, j * 128:(j + 1) * 128] = g3_ref[j, L3_PRET:L3_ROWS, :]
            wc_ref[:, j * NS_UP:(j + 1) * NS_UP] = g4_ref[j, 0:3, 0:NS_UP]
            for h in range(MLA_H):
                wpa_ref[h * HP:h * HP + MLA_V, j * 128:(j + 1) * 128] = g3_ref[j, h * MLA_V:(h + 1) * MLA_V, :]
        for h in range(MLA_H):
            wpa_ref[h * HP + MLA_V:(h + 1) * HP, :] = jnp.zeros((HP - MLA_V, D), BF16)

    return pl.pallas_call(
        body,
        name="assemble_l234",
        in_specs=[VM] * 3,
        out_specs=[VM] * 7,
        out_shape=[_sds((D, D), BF16), _sds((D, D), BF16), _sds((D, D), BF16), _sds((D_FF, D), BF16),
                   _sds((MLA_H * HP, D), BF16), _sds((RET_H * RET_D, D), BF16), _sds((3, 2 * D_FF), F32)],
        compiler_params=_cp0(),
    )(g2, g3, g4)


def slots_l1(dwup_a, dwup_b, dwxkv, *, tile):
    per_half = D_FF // NS_UP

    def body(ua_ref, ub_ref, x_ref, s_ref):
        for j in range(N_DEV):
            src = ua_ref if j < per_half else ub_ref
            c0 = (j % per_half) * NS_UP
            s_ref[j, :, 0:NS_UP] = src[:, c0:c0 + NS_UP].astype(BF16)
            s_ref[j, :, NS_UP:L1_W] = x_ref[:, j * NS_XKV:(j + 1) * NS_XKV].astype(BF16)

    return pl.pallas_call(
        body,
        name="slots_l1",
        grid=(D // tile,),
        in_specs=[pl.BlockSpec((tile, D_FF), lambda i: (i, 0)), pl.BlockSpec((tile, D_FF), lambda i: (i, 0)),
                  pl.BlockSpec((tile, 2 * D), lambda i: (i, 0))],
        out_specs=pl.BlockSpec((N_DEV, tile, L1_W), lambda i: (0, i, 0)),
        out_shape=_sds((N_DEV, D, L1_W), BF16),
        compiler_params=_cp(1),
    )(dwup_a, dwup_b, dwxkv)


def slots_l23(dwo, dwxq, dwxo, dwdn, dwpa, dwpr):
    def body(o_ref, xq_ref, xo_ref, dn_ref, pa_ref, pr_ref, l2_ref, l3_ref):
        l2_ref[L2_OUT:L2_XQ, :] = o_ref[...].astype(BF16)
        l2_ref[L2_XQ:L2_XO, :] = xq_ref[...].astype(BF16)
        l2_ref[L2_XO:L2_DN, :] = xo_ref[...].astype(BF16)
        l2_ref[L2_DN:L2_ROWS, :] = dn_ref[...].astype(BF16)
        for h in range(MLA_H):
            l3_ref[h * MLA_V:(h + 1) * MLA_V, :] = pa_ref[h * HP:h * HP + MLA_V, :].astype(BF16)
        l3_ref[L3_PRET:L3_ROWS, :] = pr_ref[...].astype(BF16)

    rows128 = pl.BlockSpec((128, D), lambda j: (j, 0))

    def cols(r):
        return pl.BlockSpec((r, 128), lambda j: (0, j))

    return pl.pallas_call(
        body,
        name="slots_l23",
        grid=(N_DEV,),
        in_specs=[rows128, rows128, rows128, pl.BlockSpec((RS_DN, D), lambda j: (j, 0)),
                  cols(MLA_H * HP), cols(RET_H * RET_D)],
        out_specs=[pl.BlockSpec((None, L2_ROWS, D), lambda j: (j, 0, 0)),
                   pl.BlockSpec((None, L3_ROWS, HP), lambda j: (j, 0, 0))],
        out_shape=[_sds((N_DEV, L2_ROWS, D), BF16), _sds((N_DEV, L3_ROWS, HP), BF16)],
        compiler_params=_cp(1),
    )(dwo, dwxq, dwxo, dwdn, dwpa, dwpr)


def slots_early(dwz, dwk, dwv, dwq, *, tile):
    def body(dz_ref, k_ref, v_ref, q_ref, s1_ref, s2_ref):
        for os_, ln_, ds_ in WZ_RUNS:
            for j, off, d, ln in _pieces(os_, ln_, ds_, NS_IN):
                s1_ref[j, :, off:off + ln] = dz_ref[:, d:d + ln].astype(BF16)

        @pl.when(pl.program_id(0) == 0)
        def _():
            for j in range(N_DEV):
                s2_ref[j, 0:E2_UQ, 0:MLA_NOPE] = k_ref[:, j * HP:j * HP + MLA_NOPE].astype(BF16)
                s2_ref[j, 0:E2_UQ, MLA_NOPE:HP] = v_ref[:, j * HP:j * HP + MLA_V].astype(BF16)
                s2_ref[j, E2_UQ:E2_ROWS, 0:NS_UQ] = q_ref[:, j * HP:j * HP + NS_UQ].astype(BF16)
                s2_ref[j, E2_UQ:E2_ROWS, NS_UQ:HP] = jnp.zeros((E2_ROWS - E2_UQ, HP - NS_UQ), BF16)

    def whole(r):
        return pl.BlockSpec((r, MLA_H * HP), lambda i: (0, 0))

    return pl.pallas_call(
        body,
        name="slots_early",
        grid=(D // tile,),
        in_specs=[pl.BlockSpec((tile, ZW), lambda i: (i, 0)), whole(MLA_KVR), whole(MLA_KVR), whole(MLA_QR)],
        out_specs=[pl.BlockSpec((N_DEV, tile, NS_IN), lambda i: (0, i, 0)),
                   pl.BlockSpec((N_DEV, E2_ROWS, HP), lambda i: (0, 0, 0))],
        out_shape=[_sds((N_DEV, D, NS_IN), BF16), _sds((N_DEV, E2_ROWS, HP), BF16)],
        compiler_params=_cp(1),
    )(dwz, dwk, dwv, dwq)


SMALL = (("g_mix", 1024, 0), ("b_gate", 2048, 1), ("g_q_lat", 256, 3), ("g_kv_lat", 128, 4), ("g_ret", 512, 5),
         ("g_cross", 1024, 6), ("g_mem", 1024, 7), ("g_ffn", 1024, 8), ("b_conv", 5632, 9), ("g_final", 1024, 15))
SMALL_DIRECT = tuple(s for s in SMALL if s[0] != "b_conv")
S_ROWS = 16
LOSS_ROW, LOSS_LANE = 4, 128


def _flat_pieces(n, row0):
    return [(row0 + k // D, k, min(D, n - k)) for k in range(0, n, D)]


def exchange_small(cs_a, cs_b, loss_part, smalls):
    ns = len(smalls)

    def body(*refs):
        ca_ref, cb_ref, loss_ref = refs[:3]
        small_refs = refs[3:3 + ns]
        rd_ref, rs_ref, dsl_ref, own_ref = refs[3 + ns:7 + ns]
        sems = refs[7 + ns:]
        own_ref[...] = jnp.zeros(own_ref.shape, F32)
        own_ref[LOSS_ROW:LOSS_ROW + 1, LOSS_LANE:LOSS_LANE + HP] = loss_ref[0:1, :]
        for (name, n, row0), g_ref in zip(SMALL_DIRECT, small_refs):
            for r, c0, ln in _flat_pieces(n, row0):
                own_ref[r:r + 1, 0:ln] = g_ref[:, c0:c0 + ln]
        row0 = dict((s[0], s[2]) for s in SMALL)["b_conv"]
        for half, c_ref in enumerate((ca_ref, cb_ref)):
            k = half * D_FF
            end = k + D_FF
            while k < end:
                r, lane = row0 + k // D, k % D
                ln = min(D - lane, end - k)
                own_ref[r:r + 1, lane:lane + ln] = c_ref[3:4, k - half * D_FF:k - half * D_FF + ln]
                k += ln
        dsl_ref[...] = jnp.zeros(dsl_ref.shape, F32)
        per_half = D_FF // NS_UP
        for j in range(N_DEV):
            c_ref = ca_ref if j < per_half else cb_ref
            c0 = (j % per_half) * NS_UP
            dsl_ref[j, 0:3, 0:NS_UP] = c_ref[0:3, c0:c0 + NS_UP]
        comm = ([(dsl_ref, True), (own_ref, False)], [rd_ref, rs_ref]) + tuple(sems)
        _exchange_start(*comm)
        _exchange_wait(*comm)

    n1 = N_DEV - 1
    return pl.pallas_call(
        body,
        name="exchange_small",
        in_specs=[VM] * (3 + ns),
        out_specs=[VM, VM, VM, VM],
        out_shape=[_sds((n1,) + L4_SHAPE, F32), _sds((n1, S_ROWS, D), F32), _sds((N_DEV,) + L4_SHAPE, F32),
                   _sds((S_ROWS, D), F32)],
        scratch_shapes=_exchange_scratch(2),
        compiler_params=_cp0(),
    )(cs_a, cs_b, loss_part, *smalls)


def _adamw(w, g, m, v):
    m = ADAM_B1 * m + (1.0 - ADAM_B1) * g
    v = ADAM_B2 * v + (1.0 - ADAM_B2) * (g * g)
    m_hat = m / (1.0 - ADAM_B1 ** ADAM_STEP)
    v_hat = v / (1.0 - ADAM_B2 ** ADAM_STEP)
    delta = -ADAM_LR * (m_hat / (jnp.sqrt(v_hat) + ADAM_EPS) + ADAM_WD * w)
    return delta, m, v


def _apply(g, refs, outs):
    d, mn, vn = _adamw(refs[0][...], g, refs[1][...], refs[2][...])
    outs[0][...] = g
    outs[1][...] = d
    outs[2][...] = mn
    outs[3][...] = vn


def adam_cols(own, recv, wmv, spans, *, name, tile):
    R, W = own.shape
    nw = len(spans)

    def body(*refs):
        own_ref, recv_ref = refs[:2]
        ins, outs = refs[2:2 + 3 * nw], refs[2 + 3 * nw:]
        g = own_ref[...].astype(F32)
        for k in range(N_DEV - 1):
            g = g + recv_ref[k].astype(F32)
        for t, (lo, hi) in enumerate(spans):
            _apply(g[:, lo:hi], ins[3 * t:3 * t + 3], outs[4 * t:4 * t + 4])

    def blk(w):
        return pl.BlockSpec((tile, w), lambda i: (i, 0))

    widths = [hi - lo for lo, hi in spans]
    return pl.pallas_call(
        body,
        name=name,
        grid=(R // tile,),
        in_specs=[blk(W), pl.BlockSpec((N_DEV - 1, tile, W), lambda i: (0, i, 0))] + [blk(w) for w in widths for _ in range(3)],
        out_specs=[blk(w) for w in widths for _ in range(4)],
        out_shape=[_sds((R, w), F32) for w in widths for _ in range(4)],
        compiler_params=_cp(1),
    )(own, recv, *wmv)


def adam_rows(own, recv, wmv, spans, *, name):
    nw = len(spans)

    def body(*refs):
        own_ref, recv_ref = refs[:2]
        ins, outs = refs[2:2 + 3 * nw], refs[2 + 3 * nw:]
        for t, (lo, hi, w) in enumerate(spans):
            g = own_ref[lo:hi, :].astype(F32)
            for k in range(N_DEV - 1):
                g = g + recv_ref[k, lo:hi, :].astype(F32)
            _apply(g[:, 0:w], ins[3 * t:3 * t + 3], outs[4 * t:4 * t + 4])

    return pl.pallas_call(
        body,
        name=name,
        in_specs=[VM] * (2 + 3 * nw),
        out_specs=[VM] * (4 * nw),
        out_shape=[_sds((hi - lo, w), F32) for lo, hi, w in spans for _ in range(4)],
        compiler_params=_cp0(),
    )(own, recv, *wmv)


def adam_small(own_s, recv_s, dslots, recv_d, wmv_small, wmv_conv):
    ns = len(SMALL)

    def body(*refs):
        own_ref, rs_ref, dsl_ref, rd_ref = refs[:4]
        ins = refs[4:4 + 3 * ns + 3]
        outs = refs[4 + 3 * ns + 3:4 + 3 * ns + 3 + 4 * ns + 4]
        loss_ref, all_sc = refs[-2], refs[-1]
        me = _my_id()
        all_sc[0] = own_ref[...]
        for k in range(N_DEV - 1):
            all_sc[k + 1] = rs_ref[k]
        g = all_sc[jnp.bitwise_xor(me, 0)]
        for s in range(1, N_DEV):
            g = g + all_sc[jnp.bitwise_xor(me, s)]
        all_sc[0] = g
        loss_ref[...] = all_sc[0, LOSS_ROW:LOSS_ROW + 1, LOSS_LANE:LOSS_LANE + HP]
        for t, (name, n, row0) in enumerate(SMALL):
            pieces = [all_sc[0, r:r + 1, 0:ln] for r, _, ln in _flat_pieces(n, row0)]
            gt = pieces[0] if len(pieces) == 1 else jnp.concatenate(pieces, axis=1)
            _apply(gt, ins[3 * t:3 * t + 3], outs[4 * t:4 * t + 4])
        gc = dsl_ref[me]
        for k in range(N_DEV - 1):
            gc = gc + rd_ref[k]
        _apply(gc[0:3, 0:NS_UP], ins[3 * ns:3 * ns + 3], outs[4 * ns:4 * ns + 4])

    out_shape = ([_sds((1, n), F32) for _, n, _ in SMALL for _ in range(4)] + [_sds((3, NS_UP), F32)] * 4
                 + [_sds((1, HP), F32)])
    return pl.pallas_call(
        body,
        name="adam_small",
        in_specs=[VM] * (4 + 3 * ns + 3),
        out_specs=[VM] * len(out_shape),
        out_shape=out_shape,
        scratch_shapes=[pltpu.VMEM((N_DEV, S_ROWS, D), F32)],
    )(own_s, recv_s, dslots, recv_d, *wmv_small, *wmv_conv)


def kernel(x, mem, positions, g_mix, w_in, b_gate, g_q_lat, w_uq, g_kv_lat, w_ukv, w_proj_mla, g_ret, w_proj_ret, w_out, g_cross, g_mem, w_xq, w_xkv, w_xo, g_ffn, w_up, w_conv, b_conv, w_down, g_final, loss_target, m_g_mix, m_w_in, m_b_gate, m_g_q_lat, m_w_uq, m_g_kv_lat, m_w_ukv, m_w_proj_mla, m_g_ret, m_w_proj_ret, m_w_out, m_g_cross, m_g_mem, m_w_xq, m_w_xkv, m_w_xo, m_g_ffn, m_w_up, m_w_conv, m_b_conv, m_w_down, m_g_final, v_g_mix, v_w_in, v_b_gate, v_g_q_lat, v_w_uq, v_g_kv_lat, v_w_ukv, v_w_proj_mla, v_g_ret, v_w_proj_ret, v_w_out, v_g_cross, v_g_mem, v_w_xq, v_w_xkv, v_w_xo, v_g_ffn, v_w_up, v_w_conv, v_b_conv, v_w_down, v_g_final):
    args = dict(locals())
    T = x.shape[1]
    M = mem.shape[1]
    tile = min(256, T)
    tile2 = min(512, T)
    tile4 = min(1024, T)
    tq = min(512, T)
    rb = min(1024, T)

    xs = x[0]
    tgt = loss_target[0]
    mems = mem[0]

    def shard(name, prefix=""):
        a = args[prefix + name]
        return a.reshape(a.shape[-2:]) if a.ndim >= 2 else a.reshape(1, -1)

    e1, e2 = pack_early(shard("w_in"), shard("w_ukv"), shard("w_uq"))
    late_parts = pack_late(shard("w_up"), shard("w_xkv"), shard("w_out"), shard("w_xq"), shard("w_xo"), shard("w_down"),
                           shard("w_proj_mla"), shard("w_proj_ret"), shard("w_conv"))

    pos_f = jnp.broadcast_to(positions[0].astype(F32)[:, None], (T, 128))
    inv_m, inv_r = _rot_inv()
    u, cm, s1, s2, cr, sr, ge1, ge2 = rowwise(
        lambda xv, p, im, ir, g: (_rms(xv, g),) + _rot_tables(p, im, ir), [(xs, None), (pos_f, None)],
        [inv_m, inv_r, g_mix], [(D, BF16)] + [(128, F32)] * 5, [], tile=tile2, name="norm_mix_tables", gather=[e1, e2])
    wz, wk, wv, wq = assemble_early(ge1, ge2, tile=tile)
    rconsts = _ret_consts()

    z = matmul(u, wz, name="mm_z", tm=2048, tn=1536)

    def mixers_in(zl, zr, cmv, s1v, s2v, crv, srv, gq, gkv, wqv, wkv_, wvv):
        cq = _rms(zl[:, 0:256], gq).astype(BF16)
        ckv = _rms(zl[:, 256:384], gkv).astype(BF16)
        qv = _rot_mla(_dot(cq, wqv), cmv, s1v, s2v)
        kr = _rot_mla(zl[:, 384:512], cmv, s1v, s2v)
        kn = _dot(ckv, wkv_)
        kv_ = jnp.concatenate([kn[:, h * HP:(h + 1) * HP] + kr for h in range(MLA_H)], axis=1)
        vv = _dot(ckv, wvv)
        lane = lax.broadcasted_iota(jnp.int32, vv.shape, 1)
        vv = jnp.where((lane & (HP - 1)) == ONE_LANE, 1.0, vv)
        rqv = _rot_ret(zr[:, 0:512], crv, srv)
        rkv = _rot_ret(zr[:, 512:1024], crv, srv) * (RET_D ** -0.5)
        return qv, kv_, vv, rqv, rkv, zr[:, 1024:1536]

    q_a, k_a, v_a, rq, rk, rv = rowwise(
        mixers_in, [(z, (512, 8)), (z, (2048, 0)), (cm, None), (s1, None), (s2, None), (cr, None), (sr, None)],
        [g_q_lat, g_kv_lat, wq, wk, wv], [(MLA_H * HP, BF16)] * 3 + [(512, BF16)] * 3, [], tile=tile2, name="mixers_in")

    o_a, lse_row, gl1, gl2, gl3, gl4 = mla_fwd(q_a, k_a, v_a, list(late_parts), tq=tq)
    wup, wxkv = assemble_l1(gl1, tile=tile)
    wo, wxq, wxo, wdn, wpa, wpr, wcv = assemble_l234(gl2, gl3, gl4)
    ret, rstate = ret_fwd(rq, rk, rv, rconsts, rb=rb)

    def gn_parts(r):
        outs = []
        for h in range(RET_H):
            rh = r[:, h * RET_D:(h + 1) * RET_D]
            mu = jnp.mean(rh, axis=-1, keepdims=True)
            dlt = rh - mu
            rstd = lax.rsqrt(jnp.mean(dlt * dlt, axis=-1, keepdims=True) + EPS)
            outs.append((dlt * rstd, rstd))
        return outs

    def mix_fwd(ov, rv_, rg, gt, wpav, wprv, gr, bg):
        ya = _dot(ov, wpav)
        xh = jnp.concatenate([p[0] for p in gn_parts(rv_)], axis=1)
        t = rg * _sigmoid(rg) * (xh * gr)
        yr = _dot(t.astype(BF16), wprv)
        ga_ = _sigmoid(gt[:, :D] + bg[:, :D])
        gr_ = _sigmoid(gt[:, D:] + bg[:, D:])
        return ga_ * ya + gr_ * yr

    mix = rowwise(mix_fwd, [(o_a, None), (ret, None), (z, (512, 3)), (z, (2048, 1))], [wpa, wpr, g_ret, b_gate],
                  [(D, BF16)], [], tile=tile2, name="mix_fwd")[0]
    def proj_norm(av, rv_, wv_, g):
        hv = rv_ + _dot(av, wv_)
        return hv, _rms(hv, g)

    h1, n2 = rowwise(proj_norm, [(mix, None), (xs, None)], [wo, g_cross], [(D, F32), (D, BF16)], [], tile=tile4,
                     name="mm_out_norm")
    xq = matmul(n2, wxq, out_dtype=BF16, name="mm_xq")
    mn = rowwise(lambda mv_, g: _rms(mv_, g), [(mems, None)], [g_mem], [(D, BF16)], [], tile=min(tile, M), name="norm_mem")[0]
    mkv = matmul(mn, wxkv, out_dtype=BF16, name="mm_mkv")

    x_scale = X_HD ** -0.5

    def xattn_fwd(xqv, mkvv):
        outs = []
        for h in range(X_H):
            sl = slice(h * X_HD, (h + 1) * X_HD)
            s = _dot_nt(xqv[:, sl], mkvv[:, sl]) * x_scale
            s = s - jnp.max(s, axis=-1, keepdims=True)
            e = jnp.exp(s)
            p = e / jnp.sum(e, axis=-1, keepdims=True)
            outs.append(_dot(p.astype(BF16), mkvv[:, D + h * X_HD:D + (h + 1) * X_HD]))
        return jnp.concatenate(outs, axis=1)

    xo = rowwise(xattn_fwd, [(xq, None)], [mkv], [(D, BF16)], [], tile=tile4, name="xattn_fwd")[0]
    h2, n3 = rowwise(proj_norm, [(xo, None), (h1, None)], [wxo, g_ffn], [(D, F32), (D, BF16)], [], tile=tile4,
                     name="mm_xo_norm")
    up_pre = matmul(n3, wup, out_dtype=BF16, name="mm_up", tm=2048, tn=1408)
    cw = D_FF // 2
    act = conv_act_fwd(up_pre, wcv, b_conv, tile=tile, cw=cw)

    def down_loss(av, hv2, tv, wv_, g):
        hv = hv2 + _dot(av, wv_)
        y = _rms(hv, g)
        err = y - tv
        part = 0.5 * jnp.sum(jnp.sum(err * err, axis=-1, keepdims=True) / D, axis=0, keepdims=True)
        dx, dg = _rms_bwd(hv, g, err / D)
        return dx, dg, jnp.broadcast_to(part, (8, 128))

    g_fin2 = g_final.reshape(1, D)
    dh3, dg_final, loss_acc = rowwise(down_loss, [(act, None), (h2, None), (tgt, None)], [wdn, g_fin2], [(D, F32)],
                                      [((1, D), F32), ((8, 128), F32)], tile=tile2, name="mm_down_loss")

    dact = matmul(dh3, wdn, tb=True, out_dtype=BF16, name="mm_dact", tm=2048, tn=1408)
    dw_down = matmul_tn(act, dh3, name="mm_dw_down", tm=1408, tk=1024)
    dup_a, dup_b, cs_a, cs_b = conv_act_bwd(up_pre, dact, wcv, b_conv, tile=tile, cw=cw)
    dw_up_a = matmul_tn(n3, dup_a, name="mm_dw_up_a", tn=1408)
    dw_up_b = matmul_tn(n3, dup_b, name="mm_dw_up_b", tn=1408)

    def ffn_in_bwd(da_, db_, hv, drv, wv_, g):
        dn = _dot_nt(da_, wv_[:, :D_FF]) + _dot_nt(db_, wv_[:, D_FF:])
        dx, dg = _rms_bwd(hv, g, dn)
        return dx + drv, dg

    dh2, dg_ffn = rowwise(ffn_in_bwd, [(dup_a, None), (dup_b, None), (h2, None), (dh3, None)], [wup, g_ffn],
                          [(D, F32)], [((1, D), F32)], tile=tile2, name="mm_dn3_norm_bwd")
    dxo = matmul(dh2, wxo, tb=True, out_dtype=BF16, name="mm_dxo")
    dw_xo = matmul_tn(xo, dh2, name="mm_dw_xo")

    def xattn_bwd(xqv, dxov, mkvv):
        dxq, dmk, dmv = [], [], []
        for h in range(X_H):
            sl = slice(h * X_HD, (h + 1) * X_HD)
            slv = slice(D + h * X_HD, D + (h + 1) * X_HD)
            s = _dot_nt(xqv[:, sl], mkvv[:, sl]) * x_scale
            s = s - jnp.max(s, axis=-1, keepdims=True)
            e = jnp.exp(s)
            p = e / jnp.sum(e, axis=-1, keepdims=True)
            dp = _dot_nt(dxov[:, sl], mkvv[:, slv])
            ds = (p * (dp - jnp.sum(dp * p, axis=-1, keepdims=True)) * x_scale).astype(BF16)
            dxq.append(_dot(ds, mkvv[:, sl]))
            dmk.append(_dot_tn(ds, xqv[:, sl]))
            dmv.append(_dot_tn(p.astype(BF16), dxov[:, sl]))
        return jnp.concatenate(dxq, axis=1), jnp.concatenate(dmk + dmv, axis=1)

    dxq, dmkv = rowwise(xattn_bwd, [(xq, None), (dxo, None)], [mkv], [(D, BF16)], [((M, 2 * D), F32)],
                        tile=tile4, name="xattn_bwd")
    dw_xq = matmul_tn(n2, dxq, name="mm_dw_xq")

    def proj_norm_bwd(dyv, hv, drv, wv_, g):
        dx, dg = _rms_bwd(hv, g, _dot_nt(dyv, wv_))
        return dx + drv, dg

    dh1, dg_cross = rowwise(proj_norm_bwd, [(dxq, None), (h1, None), (dh2, None)], [wxq, g_cross], [(D, F32)],
                            [((1, D), F32)], tile=tile4, name="mm_dn2_norm_bwd")
    dw_xkv = matmul_tn(mn, dmkv, name="mm_dw_xkv", tk=M)
    dmn = matmul(dmkv, wxkv, tb=True, name="mm_dmn", tm=M)
    dg_mem = rowwise(lambda mv_, dyv, g: _rms_bwd(mv_, g, dyv)[1], [(mems, None), (dmn, None)], [g_mem], [],
                     [((1, D), F32)], tile=min(tile, M), name="norm_mem_bwd")[0]

    dmix = matmul(dh1, wo, tb=True, out_dtype=BF16, name="mm_dmix")
    dw_out = matmul_tn(mix, dh1, name="mm_dw_out")

    def mix_bwd(ov, rv_, rg, gt, dmv_, wpav, wprv, gr, bg):
        dm_ = dmv_.astype(F32)
        ya = _dot(ov, wpav)
        parts = gn_parts(rv_)
        xh = jnp.concatenate([p[0] for p in parts], axis=1)
        yn = xh * gr
        sg = _sigmoid(rg)
        sl_ = rg * sg
        t = (sl_ * yn).astype(BF16)
        yr = _dot(t, wprv)
        ga_ = _sigmoid(gt[:, :D] + bg[:, :D])
        gr_ = _sigmoid(gt[:, D:] + bg[:, D:])
        dgates = jnp.concatenate([dm_ * ya * ga_ * (1.0 - ga_), dm_ * yr * gr_ * (1.0 - gr_)], axis=1)
        dya = (dm_ * ga_).astype(BF16)
        dyr = (dm_ * gr_).astype(BF16)
        do_ = _dot_nt(dya, wpav)
        dwpa_ = _dot_tn(ov, dya)
        dt = _dot_nt(dyr, wprv)
        dwpr_ = _dot_tn(t, dyr)
        drg_ = dt * yn * (sg * (1.0 + rg * (1.0 - sg)))
        dyn = dt * sl_
        dgr = jnp.sum(dyn * xh, axis=0, keepdims=True)
        dxh = dyn * gr
        drets = []
        for h in range(RET_H):
            sl = slice(h * RET_D, (h + 1) * RET_D)
            xhh, rstd = parts[h]
            dxhh = dxh[:, sl]
            drets.append(rstd * (dxhh - jnp.mean(dxhh, axis=-1, keepdims=True)
                                 - xhh * jnp.mean(dxhh * xhh, axis=-1, keepdims=True)))
        dret_ = jnp.concatenate(drets, axis=1)
        dbg = jnp.sum(dgates, axis=0, keepdims=True)
        prod = ov.astype(F32) * do_.astype(BF16).astype(F32)
        lane = lax.broadcasted_iota(jnp.int32, (prod.shape[0], HP), 1)
        dlt = jnp.zeros((prod.shape[0], HP), F32)
        for h in range(MLA_H):
            dlt = jnp.where(lane == h, jnp.sum(prod[:, h * HP:(h + 1) * HP], axis=-1, keepdims=True), dlt)
        return do_, dret_, drg_, dgates, dlt, dwpa_, dwpr_, dgr, dbg

    do_a, dret, drg, dgates, delta, dwpa, dw_proj_ret, dg_ret, db_gate = rowwise(
        mix_bwd, [(o_a, None), (ret, None), (z, (512, 3)), (z, (2048, 1)), (dmix, None)], [wpa, wpr, g_ret, b_gate],
        [(MLA_H * HP, BF16), (512, F32), (512, BF16), (2 * D, BF16), (HP, F32)],
        [((MLA_H * HP, D), F32), ((512, D), F32), ((1, 512), F32), ((1, 2 * D), F32)], tile=tile2, name="mix_bwd")

    sl1 = slots_l1(dw_up_a, dw_up_b, dw_xkv, tile=tile)
    sl2, sl3 = slots_l23(dw_out, dw_xq, dw_xo, dw_down, dwpa, dw_proj_ret)
    delta_row = mla_prep(delta, tq=tq)
    rl1, rl2, rl3 = sequencer_exchange([sl1, sl2, sl3], name="sequencer_exchange_late", collective_id=LATE_EXCHANGE_ID)
    dq_a, dk_a, dv_a = mla_bwd(q_a, k_a, v_a, do_a, lse_row, delta_row, [], tq=tq)
    drq_r, drk_r, drv = ret_bwd(rq, rk, rv, rstate, dret, rconsts, rb=rb)

    def mixers_in_bwd(zl, cmv, s1v, s2v, dqv, dkv_, dvv, drq_, drk_, drv_, drg_, dgt, crv, srv, gq, gkv, wqv, wkv_, wvv):
        cqf, ckvf = zl[:, 0:256], zl[:, 256:384]
        cq = _rms(cqf, gq).astype(BF16)
        ckv = _rms(ckvf, gkv).astype(BF16)
        dq_pre = _rot_mla(dqv.astype(F32), cmv, -s1v, -s2v).astype(BF16)
        dkf = dkv_.astype(F32)
        dkr = dkf[:, 0:HP]
        for h in range(1, MLA_H):
            dkr = dkr + dkf[:, h * HP:(h + 1) * HP]
        lane = lax.broadcasted_iota(jnp.int32, dkr.shape, 1)
        dzk = _rot_mla(jnp.where((lane >= 64) & (lane < 96), dkr, 0.0), cmv, -s1v, -s2v)
        dkb = dkv_.astype(BF16)
        dvb = dvv.astype(BF16)
        dcq_n = _dot_nt(dq_pre, wqv)
        dckv_n = _dot_nt(dkb, wkv_) + _dot_nt(dvb, wvv)
        dwq_ = _dot_tn(cq, dq_pre)
        dwk_ = _dot_tn(ckv, dkb)
        dwv_ = _dot_tn(ckv, dvb)
        dcq, dgq = _rms_bwd(cqf, gq, dcq_n)
        dckv, dgkv = _rms_bwd(ckvf, gkv, dckv_n)
        a = _rot_ret(drq_, crv, -srv)
        b = _rot_ret(drk_, crv, -srv) * (RET_D ** -0.5)
        dz_ = jnp.concatenate([a.astype(BF16), b.astype(BF16), drv_, drg_, dgt,
                               dcq.astype(BF16), dckv.astype(BF16), dzk.astype(BF16)], axis=1)
        return dz_, dwq_, dwk_, dwv_, dgq, dgkv

    dz, dwq, dwk, dwv, dg_q_lat, dg_kv_lat = rowwise(
        mixers_in_bwd, [(z, (512, 8)), (cm, None), (s1, None), (s2, None), (dq_a, None), (dk_a, None), (dv_a, None),
                        (drq_r, None), (drk_r, None), (drv, None), (drg, None), (dgates, None), (cr, None), (sr, None)],
        [g_q_lat, g_kv_lat, wq, wk, wv], [(ZW, BF16)],
        [((MLA_QR, MLA_H * HP), F32), ((MLA_KVR, MLA_H * HP), F32), ((MLA_KVR, MLA_H * HP), F32),
         ((1, MLA_QR), F32), ((1, MLA_KVR), F32)], tile=tile2, name="mixers_in_bwd")
    dwz = matmul_tn(u, dz, name="mm_dw_z", tn=1536)
    se1, se2 = slots_early(dwz, dwk, dwv, dwq, tile=tile)
    re1, re2 = sequencer_exchange([se1, se2], name="sequencer_exchange_early", collective_id=EARLY_EXCHANGE_ID)
    grad_x, dg_mix = rowwise(proj_norm_bwd, [(dz, None), (xs, None), (dh1, None)], [wz, g_mix], [(D, F32)],
                             [((1, D), F32)], tile=tile2, name="mm_du_norm_bwd")

    small_grads = {"g_mix": dg_mix, "b_gate": db_gate, "g_q_lat": dg_q_lat, "g_kv_lat": dg_kv_lat, "g_ret": dg_ret,
                   "g_cross": dg_cross, "g_mem": dg_mem, "g_ffn": dg_ffn, "g_final": dg_final}
    rd, rs, dslots, own_s = exchange_small(cs_a, cs_b, loss_acc, [small_grads[n] for n, _, _ in SMALL_DIRECT])

    me = _my_id()

    def own(slots):
        return lax.dynamic_index_in_dim(slots, me, axis=0, keepdims=False)

    def wmv(names):
        return [shard(n, p) for n in names for p in ("", "m_", "v_")]

    names_s = tuple(n for n, _, _ in SMALL)
    small_outs = adam_small(own_s, rs, dslots, rd, wmv(names_s), wmv(("w_conv",)))
    groups = (
        (("w_in",), adam_cols(own(se1), re1, wmv(("w_in",)), ((0, NS_IN),), name="adam_e1", tile=128)),
        (("w_ukv", "w_uq"), adam_rows(own(se2), re2, wmv(("w_ukv", "w_uq")),
                                      ((0, E2_UQ, HP), (E2_UQ, E2_ROWS, NS_UQ)), name="adam_e2")),
        (("w_up", "w_xkv"), adam_cols(own(sl1), rl1, wmv(("w_up", "w_xkv")), ((0, NS_UP), (NS_UP, L1_W)),
                                      name="adam_l1", tile=128)),
        (("w_out", "w_xq", "w_xo", "w_down"),
         adam_rows(own(sl2), rl2, wmv(("w_out", "w_xq", "w_xo", "w_down")),
                   ((L2_OUT, L2_XQ, D), (L2_XQ, L2_XO, D), (L2_XO, L2_DN, D), (L2_DN, L2_ROWS, D)), name="adam_l2")),
        (("w_proj_mla", "w_proj_ret"), adam_rows(own(sl3), rl3, wmv(("w_proj_mla", "w_proj_ret")),
                                                 ((0, L3_PRET, HP), (L3_PRET, L3_ROWS, HP)), name="adam_l3")),
        (names_s + ("w_conv",), small_outs),
    )
    loss = small_outs[-1][0, 0]
    res = {}
    for names, outs_ in groups:
        for t, n in enumerate(names):
            res[n] = outs_[4 * t:4 * t + 4]

    order = ["g_mix", "w_in", "b_gate", "g_q_lat", "w_uq", "g_kv_lat", "w_ukv", "w_proj_mla", "g_ret", "w_proj_ret",
             "w_out", "g_cross", "g_mem", "w_xq", "w_xkv", "w_xo", "g_ffn", "w_up", "w_conv", "b_conv", "w_down",
             "g_final"]
    outs = [loss, grad_x[None]]
    for kind in range(4):
        outs += [res[n][kind].reshape(args[n].shape) for n in order]
    return tuple(outs)
```

```python
import functools
import math

import jax
import jax.numpy as jnp
import numpy as np
from jax import lax
from jax.experimental import pallas as pl
from jax.experimental.pallas import tpu as pltpu
from jax.experimental.pallas import tpu_sc as plsc

F32 = jnp.float32
BF16 = jnp.bfloat16

D = 1024
MLA_H, MLA_NOPE, MLA_ROPE, MLA_V = 8, 64, 32, 64
MLA_QR, MLA_KVR = 256, 128
RET_H, RET_D, RET_C = 4, 128, 128
X_H, X_HD = 4, 256
D_FF = 2816
THETA = 10000.0
EPS = 1e-6
HP = 128
ZW = 4608
N_DEV = 8

ADAM_LR, ADAM_B1, ADAM_B2, ADAM_EPS, ADAM_WD, ADAM_STEP = 0.001, 0.9, 0.999, 1e-08, 0.01, 10

VMEM_LIMIT = 56 * 1024 * 1024
BIG_CONST_BYTES = 4 * 1024 * 1024
LATE_EXCHANGE_ID, EARLY_EXCHANGE_ID, LATE_GATHER_ID = 1, 2, 3
MESH = pl.DeviceIdType.MESH
VM = pl.BlockSpec(memory_space=pltpu.VMEM)
ANY = pl.BlockSpec(memory_space=pl.ANY)


def _cp(n_axes):
    return pltpu.CompilerParams(dimension_semantics=("arbitrary",) * n_axes, vmem_limit_bytes=VMEM_LIMIT)


def _cp0():
    return pltpu.CompilerParams(vmem_limit_bytes=VMEM_LIMIT)


def _pick(n, cap, mult=128):
    best = None
    for t in range(mult, min(n, cap) + 1, mult):
        if n % t == 0:
            best = t
    return best if best is not None else n


def _dot(a, b):
    return jnp.dot(a, b, preferred_element_type=F32)


def _dot_nt(a, b):
    return lax.dot_general(a, b, (((1,), (1,)), ((), ())), preferred_element_type=F32)


def _dot_tn(a, b):
    return lax.dot_general(a, b, (((0,), (0,)), ((), ())), preferred_element_type=F32)


def _sds(shape, dtype):
    return jax.ShapeDtypeStruct(shape, dtype)


def matmul(a, b, *, name, tb=False, out_dtype=F32, tm=1024, tn=1024):
    M, K = a.shape
    N = b.shape[0] if tb else b.shape[1]
    tm = _pick(M, tm, 8)
    tn = _pick(N, tn)

    def body(a_ref, b_ref, o_ref):
        av = a_ref[...].astype(BF16)
        bv = b_ref[...].astype(BF16)
        acc = _dot_nt(av, bv) if tb else _dot(av, bv)
        o_ref[...] = acc.astype(o_ref.dtype)

    return pl.pallas_call(
        body,
        name=name,
        grid=(M // tm, N // tn),
        in_specs=[pl.BlockSpec((tm, K), lambda i, j: (i, 0)),
                  pl.BlockSpec((tn, K), lambda i, j: (j, 0)) if tb else pl.BlockSpec((K, tn), lambda i, j: (0, j))],
        out_specs=pl.BlockSpec((tm, tn), lambda i, j: (i, j)),
        out_shape=_sds((M, N), out_dtype),
        compiler_params=_cp(2),
    )(a, b)


def matmul_tn(a, b, *, name, tm=1024, tn=1024, tk=2048):
    R, M = a.shape
    N = b.shape[1]
    tm = _pick(M, tm)
    tn = _pick(N, tn)
    tk = _pick(R, tk, 16)
    nk = R // tk

    def body(a_ref, b_ref, o_ref, acc_ref):
        k = pl.program_id(2)

        @pl.when(k == 0)
        def _():
            acc_ref[...] = jnp.zeros_like(acc_ref)

        acc_ref[...] += _dot_tn(a_ref[...].astype(BF16), b_ref[...].astype(BF16))

        @pl.when(k == nk - 1)
        def _():
            o_ref[...] = acc_ref[...]

    return pl.pallas_call(
        body,
        name=name,
        grid=(M // tm, N // tn, nk),
        in_specs=[pl.BlockSpec((tk, tm), lambda i, j, k: (k, i)), pl.BlockSpec((tk, tn), lambda i, j, k: (k, j))],
        out_specs=pl.BlockSpec((tm, tn), lambda i, j, k: (i, j)),
        out_shape=_sds((M, N), F32),
        scratch_shapes=[pltpu.VMEM((tm, tn), F32)],
        compiler_params=_cp(3),
    )(a, b)


def rowwise(fn, rows, consts, out_rows, out_accs, *, tile, name, gather=(), exchange=()):
    T = rows[0][0].shape[0]
    nt = T // tile
    travel = tuple(gather) + tuple(exchange)
    n_r, n_c, n_o, n_a, n_g = len(rows), len(consts), len(out_rows), len(out_accs), len(travel)
    n_in = n_r + n_c + n_g

    def body(*refs):
        if n_g:
            srcs, dsts = refs[n_r + n_c:n_in], refs[n_in + n_o + n_a:n_in + n_o + n_a + n_g]
            sems = tuple(refs[n_in + n_o + n_a + n_g:])
            if gather:
                comm = (srcs, dsts) + sems
                comm_start, comm_finish = _gather2_start, _gather2_finish
            else:
                comm = ([(r, True) for r in srcs], dsts) + sems
                comm_start, comm_finish = _exchange_start, _exchange_wait

            @pl.when(pl.program_id(0) == 0)
            def _():
                comm_start(*comm)

        ins = [r[...] for r in refs[: n_r + n_c]]
        outs = fn(*ins)
        if not isinstance(outs, (tuple, list)):
            outs = (outs,)
        o_refs = refs[n_in:n_in + n_o]
        a_refs = refs[n_in + n_o:n_in + n_o + n_a]
        for o_ref, o in zip(o_refs, outs[:n_o]):
            o_ref[...] = o.astype(o_ref.dtype)
        if n_a:
            first = pl.program_id(0) == 0

            @pl.when(first)
            def _():
                for a_ref, o in zip(a_refs, outs[n_o:]):
                    a_ref[...] = o.astype(a_ref.dtype)

            @pl.when(jnp.logical_not(first))
            def _():
                for a_ref, o in zip(a_refs, outs[n_o:]):
                    a_ref[...] += o.astype(a_ref.dtype)
        if n_g:

            @pl.when(pl.program_id(0) == nt - 1)
            def _():
                comm_finish(*comm)

    in_specs = []
    args = []
    for arr, win in rows:
        if win is None:
            in_specs.append(pl.BlockSpec((tile, arr.shape[1]), lambda i: (i, 0)))
        else:
            w, cb = win
            in_specs.append(pl.BlockSpec((tile, w), functools.partial(lambda i, cb: (i, cb), cb=cb)))
        args.append(arr)
    for c in consts:
        index_map = functools.partial(lambda i, nd: (0,) * nd, nd=c.ndim)
        if c.size * c.dtype.itemsize >= BIG_CONST_BYTES:
            in_specs.append(pl.BlockSpec(c.shape, index_map, pipeline_mode=pl.Buffered(1)))
        else:
            in_specs.append(pl.BlockSpec(c.shape, index_map))
        args.append(c)
    out_specs = [pl.BlockSpec((tile, w), lambda i: (i, 0)) for w, _ in out_rows]
    out_shape = [_sds((T, w), dt) for w, dt in out_rows]
    for shp, dt in out_accs:
        out_specs.append(pl.BlockSpec(shp, functools.partial(lambda i, nd: (0,) * nd, nd=len(shp))))
        out_shape.append(_sds(shp, dt))
    return pl.pallas_call(
        body,
        name=name,
        grid=(nt,),
        in_specs=in_specs + [ANY] * n_g,
        out_specs=out_specs + [ANY] * n_g,
        out_shape=out_shape + [_sds((N_DEV,) + p.shape, p.dtype) for p in gather]
        + [_sds((N_DEV - 1,) + s.shape[1:], s.dtype) for s in exchange],
        scratch_shapes=(_gather_scratch(n_g) if gather else _exchange_scratch(n_g)) if n_g else [],
        compiler_params=_cp(1),
    )(*args, *travel)


def _rms(x, g):
    r = lax.rsqrt(jnp.mean(x * x, axis=-1, keepdims=True) + EPS)
    return x * r * g


def _rms_bwd(x, g, dy):
    r = lax.rsqrt(jnp.mean(x * x, axis=-1, keepdims=True) + EPS)
    xh = x * r
    dg = jnp.sum(dy * xh, axis=0, keepdims=True)
    dxh = dy * g
    dx = r * (dxh - xh * jnp.mean(dxh * xh, axis=-1, keepdims=True))
    return dx, dg


def _sigmoid(x):
    return 0.5 * jnp.tanh(0.5 * x) + 0.5


def _rot_mla(x, c, s1, s2):
    n = x.shape[1] // HP
    outs = []
    for h in range(n):
        xh = x[:, h * HP : (h + 1) * HP]
        outs.append(xh * c + pltpu.roll(xh, HP - 16, 1) * s1 + pltpu.roll(xh, 16, 1) * s2)
    return outs[0] if n == 1 else jnp.concatenate(outs, axis=1)


def _rot_ret(x, c, s):
    n = x.shape[1] // RET_D
    outs = []
    for h in range(n):
        xh = x[:, h * RET_D : (h + 1) * RET_D]
        outs.append(xh * c + pltpu.roll(xh, RET_D // 2, 1) * s)
    return outs[0] if n == 1 else jnp.concatenate(outs, axis=1)


def _rot_inv():
    lane_np = np.arange(128)
    inv_m = (jnp.asarray(THETA, F32) ** (-jnp.asarray(lane_np & 15, F32) / 16.0)).reshape(1, 128)
    inv_r = (jnp.asarray(THETA, F32) ** (-jnp.asarray(lane_np & 63, F32) / 64.0)).reshape(1, 128)
    return inv_m, inv_r


def _rot_tables(p, im, ir):
    lane = lax.broadcasted_iota(jnp.int32, p.shape, 1)
    ang = p * im
    cm = jnp.where((lane >= 64) & (lane < 96), jnp.cos(ang), 1.0)
    sn = jnp.sin(ang)
    s1 = jnp.where((lane >= 64) & (lane < 80), -sn, 0.0)
    s2 = jnp.where((lane >= 80) & (lane < 96), sn, 0.0)
    angr = p * ir
    snr = jnp.sin(angr)
    return cm, s1, s2, jnp.cos(angr), jnp.where(lane < 64, -snr, snr)


def _peer(m):
    x, y, c = lax.axis_index("x"), lax.axis_index("y"), lax.axis_index("c")
    mx, my, mc = (m >> 2) & 1, (m >> 1) & 1, m & 1
    px = 1 - x if mx else x
    py = 1 - y if my else y
    pc = 1 - c if mc else c
    return (px, py, pc), 4 * px + 2 * py + pc


def _my_id():
    return 4 * lax.axis_index("x") + 2 * lax.axis_index("y") + lax.axis_index("c")


def _gather_copies(srcs, outs, send_sems, recv_sems, local_sems, arriving=False):
    me = _my_id()
    copies = []
    if not arriving:
        for g, (src, out) in enumerate(zip(srcs, outs)):
            copies.append((pltpu.make_async_copy(src, out.at[me], local_sems.at[g]), False))
    for m in range(1, N_DEV):
        peer, plin = _peer(m)
        for g, (src, out) in enumerate(zip(srcs, outs)):
            copies.append((pltpu.make_async_remote_copy(
                src_ref=src, dst_ref=out.at[plin if arriving else me], send_sem=send_sems.at[g, m - 1],
                recv_sem=recv_sems.at[g, m - 1], device_id=peer, device_id_type=MESH), True))
    return copies


def _gather_start(*a):
    for cp, _ in _gather_copies(*a):
        cp.start()


def _gather_wait(*a):
    for cp, _ in _gather_copies(*a, arriving=True):
        cp.wait_recv()
    for cp, remote in _gather_copies(*a):
        if remote:
            cp.wait_send()
        else:
            cp.wait()


CHIP_RELATIONS = (4, 2, 6)


def _gather2_copy(src, out, block, to, send_sems, recv_sems, g, k):
    return pltpu.make_async_remote_copy(src_ref=src, dst_ref=out.at[block], send_sem=send_sems.at[g, k],
                                        recv_sem=recv_sems.at[g, k], device_id=to, device_id_type=MESH)


def _gather2_start(srcs, outs, send_sems, recv_sems, local_sems):
    me = _my_id()
    sib, _ = _peer(1)
    for g, (src, out) in enumerate(zip(srcs, outs)):
        pltpu.make_async_copy(src, out.at[me], local_sems.at[g]).start()
        _gather2_copy(src, out, me, sib, send_sems, recv_sems, g, 0).start()
        for t, m in enumerate(CHIP_RELATIONS):
            _gather2_copy(src, out, me, _peer(m)[0], send_sems, recv_sems, g, 1 + t).start()


def _gather2_finish(srcs, outs, send_sems, recv_sems, local_sems):
    me = _my_id()
    sib, sib_lin = _peer(1)
    for t, m in enumerate(CHIP_RELATIONS):
        peer, plin = _peer(m)
        for g, (src, out) in enumerate(zip(srcs, outs)):
            _gather2_copy(src, out, plin, peer, send_sems, recv_sems, g, 1 + t).wait_recv()
            _gather2_copy(out.at[plin], out, plin, sib, send_sems, recv_sems, g, 4 + t).start()
    for g, (src, out) in enumerate(zip(srcs, outs)):
        _gather2_copy(src, out, sib_lin, sib, send_sems, recv_sems, g, 0).wait_recv()
        for t, m in enumerate(CHIP_RELATIONS):
            _gather2_copy(src, out, _peer(m | 1)[1], sib, send_sems, recv_sems, g, 4 + t).wait_recv()
    for g, (src, out) in enumerate(zip(srcs, outs)):
        _gather2_copy(src, out, me, sib, send_sems, recv_sems, g, 0).wait_send()
        for t, m in enumerate(CHIP_RELATIONS):
            peer, plin = _peer(m)
            _gather2_copy(src, out, me, peer, send_sems, recv_sems, g, 1 + t).wait_send()
            _gather2_copy(out.at[plin], out, plin, sib, send_sems, recv_sems, g, 4 + t).wait_send()
        pltpu.make_async_copy(src, out.at[me], local_sems.at[g]).wait()


def _gather_scratch(n):
    return [pltpu.SemaphoreType.DMA((n, N_DEV - 1)), pltpu.SemaphoreType.DMA((n, N_DEV - 1)), pltpu.SemaphoreType.DMA((n,))]


def _exchange_copies(srcs, dsts, send_sems, recv_sems):
    copies = []
    for m in range(1, N_DEV):
        peer, plin = _peer(m)
        for g, ((src, per_peer), dst) in enumerate(zip(srcs, dsts)):
            copies.append(pltpu.make_async_remote_copy(
                src_ref=src.at[plin] if per_peer else src, dst_ref=dst.at[m - 1], send_sem=send_sems.at[g, m - 1],
                recv_sem=recv_sems.at[g, m - 1], device_id=peer, device_id_type=MESH))
    return copies


def _exchange_start(*a):
    for cp in _exchange_copies(*a):
        cp.start()


def _exchange_wait(*a):
    copies = _exchange_copies(*a)
    for cp in copies:
        cp.wait_recv()
    for cp in copies:
        cp.wait_send()


def _exchange_scratch(n):
    return [pltpu.SemaphoreType.DMA((n, N_DEV - 1)), pltpu.SemaphoreType.DMA((n, N_DEV - 1))]


def sequencer_exchange(slots, *, name, collective_id, gather=False):
    n = len(slots)
    src_refs = [jax.new_ref(s, memory_space=pltpu.MemorySpace.HBM) for s in slots]
    got_refs = [jax.empty_ref(_sds((N_DEV,) + s.shape if gather else (N_DEV - 1,) + s.shape[1:], s.dtype),
                              memory_space=pltpu.MemorySpace.HBM) for s in slots]

    @pl.kernel(mesh=plsc.ScalarSubcoreMesh(axis_name="sequencer", num_cores=1), name=name,
               scratch_types=tuple(pltpu.SemaphoreType.DMA for _ in range(2 * n * (N_DEV - 1))),
               compiler_params=pltpu.CompilerParams(collective_id=collective_id))
    def launch(*sems):
        barrier = pltpu.get_barrier_semaphore()
        for m in range(1, N_DEV):
            pl.semaphore_signal(barrier, inc=1, device_id=_peer(m)[0], device_id_type=MESH)
        pl.semaphore_wait(barrier, N_DEV - 1)
        copies = []
        me = _my_id()
        for m in range(1, N_DEV):
            peer, plin = _peer(m)
            for g in range(n):
                k = 2 * (g * (N_DEV - 1) + m - 1)
                copies.append(pltpu.make_async_remote_copy(
                    src_ref=src_refs[g] if gather else src_refs[g].at[plin],
                    dst_ref=got_refs[g].at[me] if gather else got_refs[g].at[m - 1],
                    send_sem=sems[k], recv_sem=sems[k + 1], device_id=peer, device_id_type=MESH))
        for cp in copies:
            cp.start()
        for cp in copies:
            cp.wait_recv()
        for cp in copies:
            cp.wait_send()

    launch()
    return [g[...] for g in got_refs]


MLA_SCALE = (MLA_NOPE + MLA_ROPE) ** -0.5
MLA_C2 = MLA_SCALE * math.log2(math.e)
ONE_LANE = MLA_V
NEG = -1e30
HPS = 4
HPB = 4


def _tri_mask(n, lower_rows_ge_cols=True):
    r = lax.broadcasted_iota(jnp.int32, (n, n), 0)
    c = lax.broadcasted_iota(jnp.int32, (n, n), 1)
    return r >= c if lower_rows_ge_cols else c >= r


def mla_fwd(q, k, v, gather, *, tq):
    T = q.shape[0]
    nq = T // tq
    rep = tq // HP
    ng = len(gather)

    def body(*refs):
        q_ref, k_ref, v_ref = refs[:3]
        srcs = refs[3:3 + ng]
        o_ref, lse_ref = refs[3 + ng:5 + ng]
        outs = refs[5 + ng:5 + 2 * ng]
        m_sc, acc_sc = refs[5 + 2 * ng:7 + 2 * ng]
        comm = (srcs, outs) + tuple(refs[7 + 2 * ng:])
        h, i = pl.program_id(0), pl.program_id(1)

        if ng:

            @pl.when((h == 0) & (i == 0))
            def _():
                _gather_start(*comm)

        m_sc[...] = jnp.full(m_sc.shape, NEG, F32)
        acc_sc[...] = jnp.zeros(acc_sc.shape, F32)
        heads = [slice(t * HP, (t + 1) * HP) for t in range(HPS)]

        def block(j, masked):
            off = pl.multiple_of(j * tq, tq)
            ss = [_dot_nt(q_ref[:, hd], k_ref[pl.ds(off, tq), hd]) * MLA_C2 for hd in heads]
            mask = _tri_mask(tq) if masked else None
            for hd, s in zip(heads, ss):
                if masked:
                    s = jnp.where(mask, s, NEG)
                m_prev = m_sc[:, hd]
                m_next = jnp.maximum(m_prev, jnp.max(s, axis=-1, keepdims=True))
                p = jnp.exp2(s - jnp.tile(m_next, (1, rep)))
                alpha = jnp.exp2(m_prev - m_next)
                acc_sc[:, hd] = alpha * acc_sc[:, hd] + _dot(p.astype(BF16), v_ref[pl.ds(off, tq), hd])
                m_sc[:, hd] = m_next

        def loop_body(p, carry):
            block(2 * p, False)
            block(2 * p + 1, False)
            return carry

        lax.fori_loop(0, i // 2, loop_body, 0)

        @pl.when(i % 2 == 1)
        def _():
            block(i - 1, False)

        block(i, True)
        lane = lax.broadcasted_iota(jnp.int32, (tq, HP), 1)
        lse_cols = jnp.zeros((tq, HP), F32)
        for t, hd in enumerate(heads):
            acc = acc_sc[:, hd]
            l = acc[:, ONE_LANE:ONE_LANE + 1]
            o_ref[:, hd] = (acc / l).astype(o_ref.dtype)
            lse_cols = jnp.where(lane == t, m_sc[:, hd] + jnp.log(l) * math.log2(math.e), lse_cols)
        lse_rows = lse_cols.T
        for t in range(HPS):
            lse_ref[t] = lse_rows[t:t + 1, :]

        if ng:

            @pl.when((h == MLA_H // HPS - 1) & (i == nq - 1))
            def _():
                _gather_wait(*comm)

    blk = pl.BlockSpec((tq, HPS * HP), lambda h, i: (i, h))
    full = pl.BlockSpec((T, HPS * HP), lambda h, i: (0, h))
    return pl.pallas_call(
        body,
        name="mla_fwd",
        grid=(MLA_H // HPS, nq),
        in_specs=[blk, full, full] + [ANY] * ng,
        out_specs=[blk, pl.BlockSpec((HPS, None, 1, tq), lambda h, i: (h, i, 0, 0))] + [ANY] * ng,
        out_shape=[_sds((T, MLA_H * HP), BF16), _sds((MLA_H, nq, 1, tq), F32)]
        + [_sds((N_DEV,) + p.shape, p.dtype) for p in gather],
        scratch_shapes=[pltpu.VMEM((tq, HPS * HP), F32), pltpu.VMEM((tq, HPS * HP), F32)]
        + (_gather_scratch(ng) if ng else []),
        compiler_params=_cp(2),
    )(q, k, v, *gather)


def mla_prep(delta, *, tq):
    T = delta.shape[0]
    nq = T // tq

    def body(d_ref, row_ref):
        dt = d_ref[...].T
        for h in range(MLA_H):
            row_ref[h] = dt[h:h + 1, :]

    return pl.pallas_call(
        body,
        name="mla_prep",
        grid=(nq,),
        in_specs=[pl.BlockSpec((tq, HP), lambda i: (i, 0))],
        out_specs=pl.BlockSpec((MLA_H, None, 1, tq), lambda i: (0, i, 0, 0)),
        out_shape=_sds((MLA_H, nq, 1, tq), F32),
        compiler_params=_cp(1),
    )(delta)


def mla_bwd(q, k, v, do, lse_row, delta_row, slots, *, tq):
    T = q.shape[0]
    nq = T // tq
    ns = len(slots)

    def body(*refs):
        q_ref, k_ref, v_ref, do_ref, lse_ref, delta_ref = refs[:6]
        srcs = [(r, True) for r in refs[6:6 + ns]]
        dq_ref, dk_ref, dv_ref = refs[6 + ns:9 + ns]
        dsts = refs[9 + ns:9 + 2 * ns]
        dk_sc, dv_sc = refs[9 + 2 * ns:11 + 2 * ns]
        comm = (srcs, dsts) + tuple(refs[11 + 2 * ns:])
        h, j = pl.program_id(0), pl.program_id(1)
        if ns:

            @pl.when((h == 0) & (j == 0))
            def _():
                _exchange_start(*comm)

        dk_sc[...] = jnp.zeros(dk_sc.shape, F32)
        dv_sc[...] = jnp.zeros(dv_sc.shape, F32)
        heads = [slice(t * HP, (t + 1) * HP) for t in range(HPB)]

        @pl.when(j == 0)
        def _():
            dq_ref[...] = jnp.zeros(dq_ref.shape, F32)

        def block(i, masked):
            off = pl.multiple_of(i * tq, tq)
            sts = [_dot_nt(k_ref[:, hd], q_ref[pl.ds(off, tq), hd]) * MLA_C2 for hd in heads]
            dpts = [_dot_nt(v_ref[:, hd], do_ref[pl.ds(off, tq), hd]) for hd in heads]
            mask = _tri_mask(tq, False) if masked else None
            for t, hd in enumerate(heads):
                st = sts[t]
                if masked:
                    st = jnp.where(mask, st, NEG)
                pt = jnp.exp2(st - lse_ref[t, i])
                dv_sc[:, hd] += _dot(pt.astype(BF16), do_ref[pl.ds(off, tq), hd])
                dst = (pt * (dpts[t] - delta_ref[t, i]) * MLA_SCALE).astype(BF16)
                dk_sc[:, hd] += _dot(dst, q_ref[pl.ds(off, tq), hd])
                dq_ref[pl.ds(off, tq), hd] += _dot_tn(dst, k_ref[:, hd])

        block(j, True)

        def loop_body(p, carry):
            block(j + 1 + 2 * p, False)
            block(j + 2 + 2 * p, False)
            return carry

        rest = nq - 1 - j
        lax.fori_loop(0, rest // 2, loop_body, 0)

        @pl.when(rest % 2 == 1)
        def _():
            block(nq - 1, False)

        dk_ref[...] = dk_sc[...].astype(dk_ref.dtype)
        dv_ref[...] = dv_sc[...].astype(dv_ref.dtype)
        if ns:

            @pl.when((h == MLA_H // HPB - 1) & (j == nq - 1))
            def _():
                _exchange_wait(*comm)

    blk = pl.BlockSpec((tq, HPB * HP), lambda h, j: (j, h))
    full = pl.BlockSpec((T, HPB * HP), lambda h, j: (0, h), pipeline_mode=pl.Buffered(1))
    rows = pl.BlockSpec((HPB, nq, 1, tq), lambda h, j: (h, 0, 0, 0))
    return pl.pallas_call(
        body,
        name="mla_bwd",
        grid=(MLA_H // HPB, nq),
        in_specs=[full, blk, blk, full, rows, rows] + [ANY] * ns,
        out_specs=[full, blk, blk] + [ANY] * ns,
        out_shape=[_sds((T, MLA_H * HP), F32), _sds((T, MLA_H * HP), BF16), _sds((T, MLA_H * HP), BF16)]
        + [_sds((N_DEV - 1,) + s.shape[1:], s.dtype) for s in slots],
        scratch_shapes=[pltpu.VMEM((tq, HPB * HP), F32), pltpu.VMEM((tq, HPB * HP), F32)]
        + (_exchange_scratch(ns) if ns else []),
        compiler_params=_cp(2),
    )(q, k, v, do, lse_row, delta_row, *slots)


def _ret_consts():
    h = jnp.arange(RET_H, dtype=F32)
    log_g = jnp.log1p(-jnp.exp2(-5.0 - h))
    idx = jnp.arange(RET_C, dtype=F32)
    rel = idx[:, None] - idx[None, :]
    dmask = jnp.where(rel >= 0, jnp.exp(log_g[:, None, None] * jnp.maximum(rel, 0.0)), 0.0)
    zeta = jnp.exp(log_g[:, None] * (RET_C - 1.0 - idx)[None, :])
    xi = jnp.exp(log_g[:, None] * (idx + 1.0)[None, :])
    decay = jnp.exp(log_g * RET_C)
    zb = jnp.broadcast_to(zeta[:, :, None], (RET_H, RET_C, RET_D))
    xb = jnp.broadcast_to(xi[:, :, None], (RET_H, RET_C, RET_D))
    db = jnp.broadcast_to(decay[:, None, None], (RET_H, RET_C, RET_D))
    return dmask.astype(F32), zb.astype(F32), xb.astype(F32), db.astype(F32)


def ret_fwd(rq, rk, rv, consts, *, rb):
    T = rq.shape[0]
    nb = T // rb
    ncb = rb // RET_C

    def body(q_ref, k_ref, v_ref, dm_ref, z_ref, x_ref, dc_ref, o_ref, st_ref, r_sc):
        @pl.when(pl.program_id(0) == 0)
        def _():
            r_sc[...] = jnp.zeros(r_sc.shape, F32)

        for c in range(ncb):
            sl = slice(c * RET_C, (c + 1) * RET_C)
            for h in range(RET_H):
                hd = slice(h * RET_D, (h + 1) * RET_D)
                q, k, v = q_ref[sl, hd], k_ref[sl, hd], v_ref[sl, hd]
                r = r_sc[h]
                rbf = r.astype(BF16)
                st_ref[sl, hd] = rbf
                s = _dot_nt(q, k) * dm_ref[h]
                inner = _dot(s.astype(BF16), v)
                cross = _dot((q.astype(F32) * x_ref[h]).astype(BF16), rbf)
                o_ref[sl, hd] = inner + cross
                kz = (k.astype(F32) * z_ref[h]).T.astype(BF16)
                r_sc[h] = r * dc_ref[h] + _dot(kz, v)

    blk = pl.BlockSpec((rb, RET_H * RET_D), lambda b: (b, 0))
    cst = pl.BlockSpec((RET_H, RET_C, RET_D), lambda b: (0, 0, 0))
    return pl.pallas_call(
        body,
        name="ret_fwd",
        grid=(nb,),
        in_specs=[blk, blk, blk, cst, cst, cst, cst],
        out_specs=[blk, blk],
        out_shape=[_sds((T, RET_H * RET_D), F32), _sds((T, RET_H * RET_D), BF16)],
        scratch_shapes=[pltpu.VMEM((RET_H, RET_D, RET_D), F32)],
        compiler_params=_cp(1),
    )(rq, rk, rv, *consts)


def ret_bwd(rq, rk, rv, st, dret, consts, *, rb):
    T = rq.shape[0]
    nb = T // rb
    ncb = rb // RET_C

    def body(q_ref, k_ref, v_ref, st_ref, do_ref, dm_ref, z_ref, x_ref, dc_ref, dq_ref, dk_ref, dv_ref, g_sc):
        @pl.when(pl.program_id(0) == 0)
        def _():
            g_sc[...] = jnp.zeros(g_sc.shape, F32)

        for c in reversed(range(ncb)):
            sl = slice(c * RET_C, (c + 1) * RET_C)
            for h in range(RET_H):
                hd = slice(h * RET_D, (h + 1) * RET_D)
                dm, zt, xi = dm_ref[h], z_ref[h], x_ref[h]
                q, k, v, rp = q_ref[sl, hd], k_ref[sl, hd], v_ref[sl, hd], st_ref[sl, hd]
                dob = do_ref[sl, hd].astype(BF16)
                qf, kf = q.astype(F32), k.astype(F32)
                gn = g_sc[h]
                gnb = gn.astype(BF16)
                s = _dot_nt(q, k) * dm
                ds = _dot_nt(dob, v) * dm
                dq = _dot(ds.astype(BF16), k) + _dot_nt(dob, rp) * xi
                dk = _dot(ds.T.astype(BF16), q) + _dot_nt(v, gnb) * zt
                dv = _dot(s.T.astype(BF16), dob) + _dot((kf * zt).astype(BF16), gnb)
                dq_ref[sl, hd] = dq.astype(dq_ref.dtype)
                dk_ref[sl, hd] = dk.astype(dk_ref.dtype)
                dv_ref[sl, hd] = dv.astype(dv_ref.dtype)
                g_sc[h] = _dot((qf * xi).T.astype(BF16), dob) + dc_ref[h] * gn

    blk = pl.BlockSpec((rb, RET_H * RET_D), lambda b: (nb - 1 - b, 0))
    cst = pl.BlockSpec((RET_H, RET_C, RET_D), lambda b: (0, 0, 0))
    return pl.pallas_call(
        body,
        name="ret_bwd",
        grid=(nb,),
        in_specs=[blk, blk, blk, blk, blk, cst, cst, cst, cst],
        out_specs=[blk, blk, blk],
        out_shape=[_sds((T, RET_H * RET_D), F32), _sds((T, RET_H * RET_D), F32), _sds((T, RET_H * RET_D), BF16)],
        scratch_shapes=[pltpu.VMEM((RET_H, RET_D, RET_D), F32)],
        compiler_params=_cp(1),
    )(rq, rk, rv, st, dret, *consts)


HALO = 16
EDGE = 8


def conv_act_fwd(up_pre, w_conv, b_conv, *, tile, cw):
    T = up_pre.shape[0]
    nt = T // tile
    ncol = D_FF // cw
    hb = tile // HALO

    def body(pa_ref, a_ref, pb_ref, b_ref, wa_ref, wb_ref, ba_ref, bb_ref, o_ref):
        i = pl.program_id(1)
        keep = (i > 0).astype(F32)

        def conv(prev_ref, cur_ref, w_ref, bias_ref):
            ext = jnp.concatenate([prev_ref[...].astype(F32)[HALO - EDGE:, :] * keep, cur_ref[...].astype(F32)], axis=0)
            w = w_ref[...]
            y = ext * w[2:3, :] + pltpu.roll(ext, 1, 0) * w[1:2, :] + pltpu.roll(ext, 2, 0) * w[0:1, :] + bias_ref[...]
            return y[EDGE:, :]

        a = conv(pa_ref, a_ref, wa_ref, ba_ref)
        b = conv(pb_ref, b_ref, wb_ref, bb_ref)
        o_ref[...] = (a * _sigmoid(a) * b).astype(o_ref.dtype)

    prev_a = pl.BlockSpec((HALO, cw), lambda j, i: (jnp.maximum(i * hb - 1, 0), j))
    cur_a = pl.BlockSpec((tile, cw), lambda j, i: (i, j))
    prev_b = pl.BlockSpec((HALO, cw), lambda j, i: (jnp.maximum(i * hb - 1, 0), j + ncol))
    cur_b = pl.BlockSpec((tile, cw), lambda j, i: (i, j + ncol))
    w_a = pl.BlockSpec((3, cw), lambda j, i: (0, j))
    w_b = pl.BlockSpec((3, cw), lambda j, i: (0, j + ncol))
    bias_a = pl.BlockSpec((1, cw), lambda j, i: (0, j))
    bias_b = pl.BlockSpec((1, cw), lambda j, i: (0, j + ncol))
    return pl.pallas_call(
        body,
        name="conv_act_fwd",
        grid=(ncol, nt),
        in_specs=[prev_a, cur_a, prev_b, cur_b, w_a, w_b, bias_a, bias_b],
        out_specs=pl.BlockSpec((tile, cw), lambda j, i: (i, j)),
        out_shape=_sds((T, D_FF), BF16),
        compiler_params=_cp(2),
    )(up_pre, up_pre, up_pre, up_pre, w_conv, w_conv, b_conv, b_conv)


def conv_act_bwd(up_pre, dact, w_conv, b_conv, *, tile, cw):
    T = up_pre.shape[0]
    nt = T // tile
    ncol = D_FF // cw
    hb = tile // HALO
    ext_rows = tile + 2 * EDGE

    def body(pa_ref, a_ref, na_ref, pb_ref, b_ref, nb_ref, d_ref, nd_ref, wa_ref, wb_ref, ba_ref, bb_ref,
             dxa_ref, dxb_ref, sa_ref, sb_ref):
        i = pl.program_id(1)
        keep_p = (i > 0).astype(F32)
        keep_n = (i < nt - 1).astype(F32)

        def ext_of(prev_ref, cur_ref, next_ref):
            return jnp.concatenate(
                [prev_ref[...].astype(F32)[HALO - EDGE:, :] * keep_p, cur_ref[...].astype(F32),
                 next_ref[...].astype(F32)[:EDGE, :] * keep_n], axis=0)

        def taps(ext):
            return ext, pltpu.roll(ext, 1, 0), pltpu.roll(ext, 2, 0)

        def conv(tp, w, bias):
            return tp[0] * w[2:3, :] + tp[1] * w[1:2, :] + tp[2] * w[0:1, :] + bias

        xa = taps(ext_of(pa_ref, a_ref, na_ref))
        xb = taps(ext_of(pb_ref, b_ref, nb_ref))
        wa, wb = wa_ref[...], wb_ref[...]
        a = conv(xa, wa, ba_ref[...])
        b = conv(xb, wb, bb_ref[...])
        dy = jnp.concatenate(
            [jnp.zeros((EDGE, cw), F32), d_ref[...].astype(F32), nd_ref[...].astype(F32)[:EDGE, :] * keep_n], axis=0)
        sg = _sigmoid(a)
        da = dy * b * (sg * (1.0 + a * (1.0 - sg)))
        db = dy * (a * sg)

        def back(dup, tp, w, dx_ref, s_ref):
            dx = dup * w[2:3, :] + pltpu.roll(dup, ext_rows - 1, 0) * w[1:2, :] + pltpu.roll(dup, ext_rows - 2, 0) * w[0:1, :]
            dx_ref[...] = dx[EDGE:EDGE + tile, :].astype(dx_ref.dtype)
            dc = dup[EDGE:EDGE + tile, :]
            r2 = jnp.sum(dc * tp[0][EDGE:EDGE + tile, :], axis=0, keepdims=True)
            r1 = jnp.sum(dc * tp[1][EDGE:EDGE + tile, :], axis=0, keepdims=True)
            r0 = jnp.sum(dc * tp[2][EDGE:EDGE + tile, :], axis=0, keepdims=True)
            rb = jnp.sum(dc, axis=0, keepdims=True)
            row = lax.broadcasted_iota(jnp.int32, (8, cw), 0)
            upd = (jnp.where(row == 0, r0, 0.0) + jnp.where(row == 1, r1, 0.0) + jnp.where(row == 2, r2, 0.0)
                   + jnp.where(row == 3, rb, 0.0))

            @pl.when(i == 0)
            def _():
                s_ref[...] = upd

            @pl.when(i > 0)
            def _():
                s_ref[...] += upd

        back(da, xa, wa, dxa_ref, sa_ref)
        back(db, xb, wb, dxb_ref, sb_ref)

    def prev_of(shift):
        return pl.BlockSpec((HALO, cw), lambda j, i: (jnp.maximum(i * hb - 1, 0), j + shift))

    def next_of(shift):
        return pl.BlockSpec((HALO, cw), lambda j, i: (jnp.minimum((i + 1) * hb, nt * hb - 1), j + shift))

    def cur_of(shift):
        return pl.BlockSpec((tile, cw), lambda j, i: (i, j + shift))

    def row_of(rows, shift):
        return pl.BlockSpec((rows, cw), lambda j, i: (0, j + shift))

    return pl.pallas_call(
        body,
        name="conv_act_bwd",
        grid=(ncol, nt),
        in_specs=[prev_of(0), cur_of(0), next_of(0), prev_of(ncol), cur_of(ncol), next_of(ncol), cur_of(0), next_of(0),
                  row_of(3, 0), row_of(3, ncol), row_of(1, 0), row_of(1, ncol)],
        out_specs=[cur_of(0), cur_of(0), row_of(8, 0), row_of(8, 0)],
        out_shape=[_sds((T, D_FF), BF16), _sds((T, D_FF), BF16), _sds((8, D_FF), F32), _sds((8, D_FF), F32)],
        compiler_params=_cp(2),
    )(up_pre, up_pre, up_pre, up_pre, up_pre, up_pre, dact, dact, w_conv, w_conv, b_conv, b_conv)


NS_IN, NS_UP, NS_XKV, NS_UQ = 564, 704, 256, 96
E2_ROWS, E2_UQ = 384, 128
L1_W = NS_UP + NS_XKV
L2_ROWS = 736
L2_OUT, L2_XQ, L2_XO, L2_DN = 0, 128, 256, 384
RS_DN = 352
L3_ROWS, L3_PRET = 1024, 512
L4_SHAPE = (8, 768)
WZ_RUNS = ((416, 4096, 0), (0, 384, 4096), (384, 32, 4544))
WZ_ZERO = ((4480, 4544), (4576, 4608))


def _pieces(orig_start, length, dst_start, ns):
    out, c, d, end = [], orig_start, dst_start, orig_start + length
    while c < end:
        j, off = c // ns, c % ns
        ln = min(ns - off, end - c)
        out.append((j, off, d, ln))
        c += ln
        d += ln
    return out


def pack_early(w_in, w_ukv, w_uq):
    def body(in_ref, ukv_ref, uq_ref, e1_ref, e2_ref):
        e1_ref[...] = in_ref[...].astype(BF16)
        e2_ref[0:E2_UQ, :] = ukv_ref[...].astype(BF16)
        e2_ref[E2_UQ:E2_ROWS, 0:NS_UQ] = uq_ref[...].astype(BF16)
        e2_ref[E2_UQ:E2_ROWS, NS_UQ:HP] = jnp.zeros((E2_ROWS - E2_UQ, HP - NS_UQ), BF16)

    return pl.pallas_call(
        body,
        name="pack_early",
        in_specs=[VM] * 3,
        out_specs=[VM] * 2,
        out_shape=[_sds((D, NS_IN), BF16), _sds((E2_ROWS, HP), BF16)],
        compiler_params=_cp0(),
    )(w_in, w_ukv, w_uq)


def pack_late(w_up, w_xkv, w_out, w_xq, w_xo, w_down, w_pmla, w_pret, w_conv):
    def body(up_ref, xkv_ref, o_ref, xq_ref, xo_ref, dn_ref, pm_ref, pr_ref, cv_ref, l1_ref, l2_ref, l3_ref, l4_ref):
        l1_ref[:, 0:NS_UP] = up_ref[...].astype(BF16)
        l1_ref[:, NS_UP:L1_W] = xkv_ref[...].astype(BF16)
        l2_ref[L2_OUT:L2_XQ, :] = o_ref[...].astype(BF16)
        l2_ref[L2_XQ:L2_XO, :] = xq_ref[...].astype(BF16)
        l2_ref[L2_XO:L2_DN, :] = xo_ref[...].astype(BF16)
        l2_ref[L2_DN:L2_ROWS, :] = dn_ref[...].astype(BF16)
        l3_ref[0:L3_PRET, :] = pm_ref[...].astype(BF16)
        l3_ref[L3_PRET:L3_ROWS, :] = pr_ref[...].astype(BF16)
        l4_ref[...] = jnp.zeros(L4_SHAPE, F32)
        l4_ref[0:3, 0:NS_UP] = cv_ref[...]

    return pl.pallas_call(
        body,
        name="pack_late",
        in_specs=[VM] * 9,
        out_specs=[VM] * 4,
        out_shape=[_sds((D, L1_W), BF16), _sds((L2_ROWS, D), BF16), _sds((L3_ROWS, HP), BF16), _sds(L4_SHAPE, F32)],
        compiler_params=_cp0(),
    )(w_up, w_xkv, w_out, w_xq, w_xo, w_down, w_pmla, w_pret, w_conv)


def assemble_early(g1, g2, *, tile):
    def body(g1_ref, g2_ref, wz_ref, wk_ref, wv_ref, wq_ref):
        for lo, hi in WZ_ZERO:
            wz_ref[:, lo:hi] = jnp.zeros((tile, hi - lo), BF16)
        for os_, ln_, ds_ in WZ_RUNS:
            for j, off, d, ln in _pieces(os_, ln_, ds_, NS_IN):
                wz_ref[:, d:d + ln] = g1_ref[j, :, off:off + ln]

        @pl.when(pl.program_id(0) == 0)
        def _():
            half = jnp.zeros((MLA_KVR, HP - MLA_NOPE), BF16)
            for j in range(N_DEV):
                wk_ref[:, j * HP:j * HP + MLA_NOPE] = g2_ref[j, 0:E2_UQ, 0:MLA_NOPE]
                wk_ref[:, j * HP + MLA_NOPE:(j + 1) * HP] = half
                wv_ref[:, j * HP:j * HP + MLA_V] = g2_ref[j, 0:E2_UQ, MLA_NOPE:HP]
                wv_ref[:, j * HP + MLA_V:(j + 1) * HP] = half
                wq_ref[:, j * HP:j * HP + NS_UQ] = g2_ref[j, E2_UQ:E2_ROWS, 0:NS_UQ]
                wq_ref[:, j * HP + NS_UQ:(j + 1) * HP] = jnp.zeros((MLA_QR, HP - NS_UQ), BF16)

    def whole(r):
        return pl.BlockSpec((r, MLA_H * HP), lambda i: (0, 0))

    return pl.pallas_call(
        body,
        name="assemble_early",
        grid=(D // tile,),
        in_specs=[pl.BlockSpec((N_DEV, tile, NS_IN), lambda i: (0, i, 0)),
                  pl.BlockSpec((N_DEV, E2_ROWS, HP), lambda i: (0, 0, 0))],
        out_specs=[pl.BlockSpec((tile, ZW), lambda i: (i, 0)), whole(MLA_KVR), whole(MLA_KVR), whole(MLA_QR)],
        out_shape=[_sds((D, ZW), BF16), _sds((MLA_KVR, MLA_H * HP), BF16), _sds((MLA_KVR, MLA_H * HP), BF16),
                   _sds((MLA_QR, MLA_H * HP), BF16)],
        compiler_params=_cp(1),
    )(g1, g2)


def assemble_l1(g1, *, tile):
    def body(g_ref, wup_ref, wxkv_ref):
        for j in range(N_DEV):
            wup_ref[:, j * NS_UP:(j + 1) * NS_UP] = g_ref[j, :, 0:NS_UP]
            wxkv_ref[:, j * NS_XKV:(j + 1) * NS_XKV] = g_ref[j, :, NS_UP:L1_W]

    return pl.pallas_call(
        body,
        name="assemble_l1",
        grid=(D // tile,),
        in_specs=[pl.BlockSpec((N_DEV, tile, L1_W), lambda i: (0, i, 0))],
        out_specs=[pl.BlockSpec((tile, 2 * D_FF), lambda i: (i, 0)), pl.BlockSpec((tile, 2 * D), lambda i: (i, 0))],
        out_shape=[_sds((D, 2 * D_FF), BF16), _sds((D, 2 * D), BF16)],
        compiler_params=_cp(1),
    )(g1)


def assemble_l234(g2, g3, g4):
    def body(g2_ref, g3_ref, g4_ref, wo_ref, wxq_ref, wxo_ref, wdn_ref, wpa_ref, wpr_ref, wc_ref):
        for j in range(N_DEV):
            wo_ref[j * 128:(j + 1) * 128, :] = g2_ref[j, L2_OUT:L2_XQ, :]
            wxq_ref[j * 128:(j + 1) * 128, :] = g2_ref[j, L2_XQ:L2_XO, :]
            wxo_ref[j * 128:(j + 1) * 128, :] = g2_ref[j, L2_XO:L2_DN, :]
            wdn_ref[j * RS_DN:(j + 1) * RS_DN, :] = g2_ref[j, L2_DN:L2_ROWS, :]
            wpr_ref[:, j * 128:(j + 1) * 128] = g3_ref[j, L3_PRET:L3_ROWS, :]
            wc_ref[:, j * NS_UP:(j + 1) * NS_UP] = g4_ref[j, 0:3, 0:NS_UP]
            for h in range(MLA_H):
                wpa_ref[h * HP:h * HP + MLA_V, j * 128:(j + 1) * 128] = g3_ref[j, h * MLA_V:(h + 1) * MLA_V, :]
        for h in range(MLA_H):
            wpa_ref[h * HP + MLA_V:(h + 1) * HP, :] = jnp.zeros((HP - MLA_V, D), BF16)

    return pl.pallas_call(
        body,
        name="assemble_l234",
        in_specs=[VM] * 3,
        out_specs=[VM] * 7,
        out_shape=[_sds((D, D), BF16), _sds((D, D), BF16), _sds((D, D), BF16), _sds((D_FF, D), BF16),
                   _sds((MLA_H * HP, D), BF16), _sds((RET_H * RET_D, D), BF16), _sds((3, 2 * D_FF), F32)],
        compiler_params=_cp0(),
    )(g2, g3, g4)


def slots_l1(dwup_a, dwup_b, dwxkv, *, tile):
    per_half = D_FF // NS_UP

    def body(ua_ref, ub_ref, x_ref, s_ref):
        for j in range(N_DEV):
            src = ua_ref if j < per_half else ub_ref
            c0 = (j % per_half) * NS_UP
            s_ref[j, :, 0:NS_UP] = src[:, c0:c0 + NS_UP].astype(BF16)
            s_ref[j, :, NS_UP:L1_W] = x_ref[:, j * NS_XKV:(j + 1) * NS_XKV].astype(BF16)

    return pl.pallas_call(
        body,
        name="slots_l1",
        grid=(D // tile,),
        in_specs=[pl.BlockSpec((tile, D_FF), lambda i: (i, 0)), pl.BlockSpec((tile, D_FF), lambda i: (i, 0)),
                  pl.BlockSpec((tile, 2 * D), lambda i: (i, 0))],
        out_specs=pl.BlockSpec((N_DEV, tile, L1_W), lambda i: (0, i, 0)),
        out_shape=_sds((N_DEV, D, L1_W), BF16),
        compiler_params=_cp(1),
    )(dwup_a, dwup_b, dwxkv)


def slots_l23(dwo, dwxq, dwxo, dwdn, dwpa, dwpr):
    def body(o_ref, xq_ref, xo_ref, dn_ref, pa_ref, pr_ref, l2_ref, l3_ref):
        l2_ref[L2_OUT:L2_XQ, :] = o_ref[...].astype(BF16)
        l2_ref[L2_XQ:L2_XO, :] = xq_ref[...].astype(BF16)
        l2_ref[L2_XO:L2_DN, :] = xo_ref[...].astype(BF16)
        l2_ref[L2_DN:L2_ROWS, :] = dn_ref[...].astype(BF16)
        for h in range(MLA_H):
            l3_ref[h * MLA_V:(h + 1) * MLA_V, :] = pa_ref[h * HP:h * HP + MLA_V, :].astype(BF16)
        l3_ref[L3_PRET:L3_ROWS, :] = pr_ref[...].astype(BF16)

    rows128 = pl.BlockSpec((128, D), lambda j: (j, 0))

    def cols(r):
        return pl.BlockSpec((r, 128), lambda j: (0, j))

    return pl.pallas_call(
        body,
        name="slots_l23",
        grid=(N_DEV,),
        in_specs=[rows128, rows128, rows128, pl.BlockSpec((RS_DN, D), lambda j: (j, 0)),
                  cols(MLA_H * HP), cols(RET_H * RET_D)],
        out_specs=[pl.BlockSpec((None, L2_ROWS, D), lambda j: (j, 0, 0)),
                   pl.BlockSpec((None, L3_ROWS, HP), lambda j: (j, 0, 0))],
        out_shape=[_sds((N_DEV, L2_ROWS, D), BF16), _sds((N_DEV, L3_ROWS, HP), BF16)],
        compiler_params=_cp(1),
    )(dwo, dwxq, dwxo, dwdn, dwpa, dwpr)


def slots_early(dwz, dwk, dwv, dwq, *, tile):
    def body(dz_ref, k_ref, v_ref, q_ref, s1_ref, s2_ref):
        for os_, ln_, ds_ in WZ_RUNS:
            for j, off, d, ln in _pieces(os_, ln_, ds_, NS_IN):
                s1_ref[j, :, off:off + ln] = dz_ref[:, d:d + ln].astype(BF16)

        @pl.when(pl.program_id(0) == 0)
        def _():
            for j in range(N_DEV):
                s2_ref[j, 0:E2_UQ, 0:MLA_NOPE] = k_ref[:, j * HP:j * HP + MLA_NOPE].astype(BF16)
                s2_ref[j, 0:E2_UQ, MLA_NOPE:HP] = v_ref[:, j * HP:j * HP + MLA_V].astype(BF16)
                s2_ref[j, E2_UQ:E2_ROWS, 0:NS_UQ] = q_ref[:, j * HP:j * HP + NS_UQ].astype(BF16)
                s2_ref[j, E2_UQ:E2_ROWS, NS_UQ:HP] = jnp.zeros((E2_ROWS - E2_UQ, HP - NS_UQ), BF16)

    def whole(r):
        return pl.BlockSpec((r, MLA_H * HP), lambda i: (0, 0))

    return pl.pallas_call(
        body,
        name="slots_early",
        grid=(D // tile,),
        in_specs=[pl.BlockSpec((tile, ZW), lambda i: (i, 0)), whole(MLA_KVR), whole(MLA_KVR), whole(MLA_QR)],
        out_specs=[pl.BlockSpec((N_DEV, tile, NS_IN), lambda i: (0, i, 0)),
                   pl.BlockSpec((N_DEV, E2_ROWS, HP), lambda i: (0, 0, 0))],
        out_shape=[_sds((N_DEV, D, NS_IN), BF16), _sds((N_DEV, E2_ROWS, HP), BF16)],
        compiler_params=_cp(1),
    )(dwz, dwk, dwv, dwq)


SMALL = (("g_mix", 1024, 0), ("b_gate", 2048, 1), ("g_q_lat", 256, 3), ("g_kv_lat", 128, 4), ("g_ret", 512, 5),
         ("g_cross", 1024, 6), ("g_mem", 1024, 7), ("g_ffn", 1024, 8), ("b_conv", 5632, 9), ("g_final", 1024, 15))
SMALL_DIRECT = tuple(s for s in SMALL if s[0] != "b_conv")
S_ROWS = 16
LOSS_ROW, LOSS_LANE = 4, 128


def _flat_pieces(n, row0):
    return [(row0 + k // D, k, min(D, n - k)) for k in range(0, n, D)]


def exchange_small(cs_a, cs_b, loss_part, smalls):
    ns = len(smalls)

    def body(*refs):
        ca_ref, cb_ref, loss_ref = refs[:3]
        small_refs = refs[3:3 + ns]
        rd_ref, rs_ref, dsl_ref, own_ref = refs[3 + ns:7 + ns]
        sems = refs[7 + ns:]
        own_ref[...] = jnp.zeros(own_ref.shape, F32)
        own_ref[LOSS_ROW:LOSS_ROW + 1, LOSS_LANE:LOSS_LANE + HP] = loss_ref[0:1, :]
        for (name, n, row0), g_ref in zip(SMALL_DIRECT, small_refs):
            for r, c0, ln in _flat_pieces(n, row0):
                own_ref[r:r + 1, 0:ln] = g_ref[:, c0:c0 + ln]
        row0 = dict((s[0], s[2]) for s in SMALL)["b_conv"]
        for half, c_ref in enumerate((ca_ref, cb_ref)):
            k = half * D_FF
            end = k + D_FF
            while k < end:
                r, lane = row0 + k // D, k % D
                ln = min(D - lane, end - k)
                own_ref[r:r + 1, lane:lane + ln] = c_ref[3:4, k - half * D_FF:k - half * D_FF + ln]
                k += ln
        dsl_ref[...] = jnp.zeros(dsl_ref.shape, F32)
        per_half = D_FF // NS_UP
        for j in range(N_DEV):
            c_ref = ca_ref if j < per_half else cb_ref
            c0 = (j % per_half) * NS_UP
            dsl_ref[j, 0:3, 0:NS_UP] = c_ref[0:3, c0:c0 + NS_UP]
        comm = ([(dsl_ref, True), (own_ref, False)], [rd_ref, rs_ref]) + tuple(sems)
        _exchange_start(*comm)
        _exchange_wait(*comm)

    n1 = N_DEV - 1
    return pl.pallas_call(
        body,
        name="exchange_small",
        in_specs=[VM] * (3 + ns),
        out_specs=[VM, VM, VM, VM],
        out_shape=[_sds((n1,) + L4_SHAPE, F32), _sds((n1, S_ROWS, D), F32), _sds((N_DEV,) + L4_SHAPE, F32),
                   _sds((S_ROWS, D), F32)],
        scratch_shapes=_exchange_scratch(2),
        compiler_params=_cp0(),
    )(cs_a, cs_b, loss_part, *smalls)


def _adamw(w, g, m, v):
    m = ADAM_B1 * m + (1.0 - ADAM_B1) * g
    v = ADAM_B2 * v + (1.0 - ADAM_B2) * (g * g)
    m_hat = m / (1.0 - ADAM_B1 ** ADAM_STEP)
    v_hat = v / (1.0 - ADAM_B2 ** ADAM_STEP)
    delta = -ADAM_LR * (m_hat / (jnp.sqrt(v_hat) + ADAM_EPS) + ADAM_WD * w)
    return delta, m, v


def _apply(g, refs, outs):
    d, mn, vn = _adamw(refs[0][...], g, refs[1][...], refs[2][...])
    outs[0][...] = g
    outs[1][...] = d
    outs[2][...] = mn
    outs[3][...] = vn


def adam_cols(own, recv, wmv, spans, *, name, tile):
    R, W = own.shape
    nw = len(spans)

    def body(*refs):
        own_ref, recv_ref = refs[:2]
        ins, outs = refs[2:2 + 3 * nw], refs[2 + 3 * nw:]
        g = own_ref[...].astype(F32)
        for k in range(N_DEV - 1):
            g = g + recv_ref[k].astype(F32)
        for t, (lo, hi) in enumerate(spans):
            _apply(g[:, lo:hi], ins[3 * t:3 * t + 3], outs[4 * t:4 * t + 4])

    def blk(w):
        return pl.BlockSpec((tile, w), lambda i: (i, 0))

    widths = [hi - lo for lo, hi in spans]
    return pl.pallas_call(
        body,
        name=name,
        grid=(R // tile,),
        in_specs=[blk(W), pl.BlockSpec((N_DEV - 1, tile, W), lambda i: (0, i, 0))] + [blk(w) for w in widths for _ in range(3)],
        out_specs=[blk(w) for w in widths for _ in range(4)],
        out_shape=[_sds((R, w), F32) for w in widths for _ in range(4)],
        compiler_params=_cp(1),
    )(own, recv, *wmv)


def adam_rows(own, recv, wmv, spans, *, name):
    nw = len(spans)

    def body(*refs):
        own_ref, recv_ref = refs[:2]
        ins, outs = refs[2:2 + 3 * nw], refs[2 + 3 * nw:]
        for t, (lo, hi, w) in enumerate(spans):
            g = own_ref[lo:hi, :].astype(F32)
            for k in range(N_DEV - 1):
                g = g + recv_ref[k, lo:hi, :].astype(F32)
            _apply(g[:, 0:w], ins[3 * t:3 * t + 3], outs[4 * t:4 * t + 4])

    return pl.pallas_call(
        body,
        name=name,
        in_specs=[VM] * (2 + 3 * nw),
        out_specs=[VM] * (4 * nw),
        out_shape=[_sds((hi - lo, w), F32) for lo, hi, w in spans for _ in range(4)],
        compiler_params=_cp0(),
    )(own, recv, *wmv)


def adam_small(own_s, recv_s, dslots, recv_d, wmv_small, wmv_conv):
    ns = len(SMALL)

    def body(*refs):
        own_ref, rs_ref, dsl_ref, rd_ref = refs[:4]
        ins = refs[4:4 + 3 * ns + 3]
        outs = refs[4 + 3 * ns + 3:4 + 3 * ns + 3 + 4 * ns + 4]
        loss_ref, all_sc = refs[-2], refs[-1]
        me = _my_id()
        all_sc[0] = own_ref[...]
        for k in range(N_DEV - 1):
            all_sc[k + 1] = rs_ref[k]
        g = all_sc[jnp.bitwise_xor(me, 0)]
        for s in range(1, N_DEV):
            g = g + all_sc[jnp.bitwise_xor(me, s)]
        all_sc[0] = g
        loss_ref[...] = all_sc[0, LOSS_ROW:LOSS_ROW + 1, LOSS_LANE:LOSS_LANE + HP]
        for t, (name, n, row0) in enumerate(SMALL):
            pieces = [all_sc[0, r:r + 1, 0:ln] for r, _, ln in _flat_pieces(n, row0)]
            gt = pieces[0] if len(pieces) == 1 else jnp.concatenate(pieces, axis=1)
            _apply(gt, ins[3 * t:3 * t + 3], outs[4 * t:4 * t + 4])
        gc = dsl_ref[me]
        for k in range(N_DEV - 1):
            gc = gc + rd_ref[k]
        _apply(gc[0:3, 0:NS_UP], ins[3 * ns:3 * ns + 3], outs[4 * ns:4 * ns + 4])

    out_shape = ([_sds((1, n), F32) for _, n, _ in SMALL for _ in range(4)] + [_sds((3, NS_UP), F32)] * 4
                 + [_sds((1, HP), F32)])
    return pl.pallas_call(
        body,
        name="adam_small",
        in_specs=[VM] * (4 + 3 * ns + 3),
        out_specs=[VM] * len(out_shape),
        out_shape=out_shape,
        scratch_shapes=[pltpu.VMEM((N_DEV, S_ROWS, D), F32)],
    )(own_s, recv_s, dslots, recv_d, *wmv_small, *wmv_conv)


def kernel(x, mem, positions, g_mix, w_in, b_gate, g_q_lat, w_uq, g_kv_lat, w_ukv, w_proj_mla, g_ret, w_proj_ret, w_out, g_cross, g_mem, w_xq, w_xkv, w_xo, g_ffn, w_up, w_conv, b_conv, w_down, g_final, loss_target, m_g_mix, m_w_in, m_b_gate, m_g_q_lat, m_w_uq, m_g_kv_lat, m_w_ukv, m_w_proj_mla, m_g_ret, m_w_proj_ret, m_w_out, m_g_cross, m_g_mem, m_w_xq, m_w_xkv, m_w_xo, m_g_ffn, m_w_up, m_w_conv, m_b_conv, m_w_down, m_g_final, v_g_mix, v_w_in, v_b_gate, v_g_q_lat, v_w_uq, v_g_kv_lat, v_w_ukv, v_w_proj_mla, v_g_ret, v_w_proj_ret, v_w_out, v_g_cross, v_g_mem, v_w_xq, v_w_xkv, v_w_xo, v_g_ffn, v_w_up, v_w_conv, v_b_conv, v_w_down, v_g_final):
    args = dict(locals())
    T = x.shape[1]
    M = mem.shape[1]
    tile = min(256, T)
    tile2 = min(512, T)
    tile4 = min(1024, T)
    tq = min(512, T)
    rb = min(1024, T)

    xs = x[0]
    tgt = loss_target[0]
    mems = mem[0]

    def shard(name, prefix=""):
        a = args[prefix + name]
        return a.reshape(a.shape[-2:]) if a.ndim >= 2 else a.reshape(1, -1)

    e1, e2 = pack_early(shard("w_in"), shard("w_ukv"), shard("w_uq"))
    late_parts = pack_late(shard("w_up"), shard("w_xkv"), shard("w_out"), shard("w_xq"), shard("w_xo"), shard("w_down"),
                           shard("w_proj_mla"), shard("w_proj_ret"), shard("w_conv"))

    pos_f = jnp.broadcast_to(positions[0].astype(F32)[:, None], (T, 128))
    inv_m, inv_r = _rot_inv()
    u, cm, s1, s2, cr, sr, ge1, ge2 = rowwise(
        lambda xv, p, im, ir, g: (_rms(xv, g),) + _rot_tables(p, im, ir), [(xs, None), (pos_f, None)],
        [inv_m, inv_r, g_mix], [(D, BF16)] + [(128, F32)] * 5, [], tile=tile2, name="norm_mix_tables", gather=[e1, e2])
    wz, wk, wv, wq = assemble_early(ge1, ge2, tile=tile)
    rconsts = _ret_consts()

    z = matmul(u, wz, name="mm_z", tm=2048, tn=1536)

    def mixers_in(zl, zr, cmv, s1v, s2v, crv, srv, gq, gkv, wqv, wkv_, wvv):
        cq = _rms(zl[:, 0:256], gq).astype(BF16)
        ckv = _rms(zl[:, 256:384], gkv).astype(BF16)
        qv = _rot_mla(_dot(cq, wqv), cmv, s1v, s2v)
        kr = _rot_mla(zl[:, 384:512], cmv, s1v, s2v)
        kn = _dot(ckv, wkv_)
        kv_ = jnp.concatenate([kn[:, h * HP:(h + 1) * HP] + kr for h in range(MLA_H)], axis=1)
        vv = _dot(ckv, wvv)
        lane = lax.broadcasted_iota(jnp.int32, vv.shape, 1)
        vv = jnp.where((lane & (HP - 1)) == ONE_LANE, 1.0, vv)
        rqv = _rot_ret(zr[:, 0:512], crv, srv)
        rkv = _rot_ret(zr[:, 512:1024], crv, srv) * (RET_D ** -0.5)
        return qv, kv_, vv, rqv, rkv, zr[:, 1024:1536]

    q_a, k_a, v_a, rq, rk, rv = rowwise(
        mixers_in, [(z, (512, 8)), (z, (2048, 0)), (cm, None), (s1, None), (s2, None), (cr, None), (sr, None)],
        [g_q_lat, g_kv_lat, wq, wk, wv], [(MLA_H * HP, BF16)] * 3 + [(512, BF16)] * 3, [], tile=tile2, name="mixers_in")

    gathered_late = sequencer_exchange(list(late_parts), name="sequencer_gather_late", collective_id=LATE_GATHER_ID,
                                       gather=True)
    o_a, lse_row = mla_fwd(q_a, k_a, v_a, [], tq=tq)
    me0 = _my_id()
    gl1, gl2, gl3, gl4 = [lax.dynamic_update_slice_in_dim(g, p[None], me0, axis=0)
                          for g, p in zip(gathered_late, late_parts)]
    wup, wxkv = assemble_l1(gl1, tile=tile)
    wo, wxq, wxo, wdn, wpa, wpr, wcv = assemble_l234(gl2, gl3, gl4)
    ret, rstate = ret_fwd(rq, rk, rv, rconsts, rb=rb)

    def gn_parts(r):
        outs = []
        for h in range(RET_H):
            rh = r[:, h * RET_D:(h + 1) * RET_D]
            mu = jnp.mean(rh, axis=-1, keepdims=True)
            dlt = rh - mu
            rstd = lax.rsqrt(jnp.mean(dlt * dlt, axis=-1, keepdims=True) + EPS)
            outs.append((dlt * rstd, rstd))
        return outs

    def mix_fwd(ov, rv_, rg, gt, wpav, wprv, gr, bg):
        ya = _dot(ov, wpav)
        xh = jnp.concatenate([p[0] for p in gn_parts(rv_)], axis=1)
        t = rg * _sigmoid(rg) * (xh * gr)
        yr = _dot(t.astype(BF16), wprv)
        ga_ = _sigmoid(gt[:, :D] + bg[:, :D])
        gr_ = _sigmoid(gt[:, D:] + bg[:, D:])
        return ga_ * ya + gr_ * yr

    mix = rowwise(mix_fwd, [(o_a, None), (ret, None), (z, (512, 3)), (z, (2048, 1))], [wpa, wpr, g_ret, b_gate],
                  [(D, BF16)], [], tile=tile2, name="mix_fwd")[0]
    def proj_norm(av, rv_, wv_, g):
        hv = rv_ + _dot(av, wv_)
        return hv, _rms(hv, g)

    h1, n2 = rowwise(proj_norm, [(mix, None), (xs, None)], [wo, g_cross], [(D, F32), (D, BF16)], [], tile=tile4,
                     name="mm_out_norm")
    xq = matmul(n2, wxq, out_dtype=BF16, name="mm_xq")
    mn = rowwise(lambda mv_, g: _rms(mv_, g), [(mems, None)], [g_mem], [(D, BF16)], [], tile=min(tile, M), name="norm_mem")[0]
    mkv = matmul(mn, wxkv, out_dtype=BF16, name="mm_mkv")

    x_scale = X_HD ** -0.5

    def xattn_fwd(xqv, mkvv):
        outs = []
        for h in range(X_H):
            sl = slice(h * X_HD, (h + 1) * X_HD)
            s = _dot_nt(xqv[:, sl], mkvv[:, sl]) * x_scale
            s = s - jnp.max(s, axis=-1, keepdims=True)
            e = jnp.exp(s)
            p = e / jnp.sum(e, axis=-1, keepdims=True)
            outs.append(_dot(p.astype(BF16), mkvv[:, D + h * X_HD:D + (h + 1) * X_HD]))
        return jnp.concatenate(outs, axis=1)

    xo = rowwise(xattn_fwd, [(xq, None)], [mkv], [(D, BF16)], [], tile=tile4, name="xattn_fwd")[0]
    h2, n3 = rowwise(proj_norm, [(xo, None), (h1, None)], [wxo, g_ffn], [(D, F32), (D, BF16)], [], tile=tile4,
                     name="mm_xo_norm")
    up_pre = matmul(n3, wup, out_dtype=BF16, name="mm_up", tm=2048, tn=1408)
    cw = D_FF // 2
    act = conv_act_fwd(up_pre, wcv, b_conv, tile=tile, cw=cw)

    def down_loss(av, hv2, tv, wv_, g):
        hv = hv2 + _dot(av, wv_)
        y = _rms(hv, g)
        err = y - tv
        part = 0.5 * jnp.sum(jnp.sum(err * err, axis=-1, keepdims=True) / D, axis=0, keepdims=True)
        dx, dg = _rms_bwd(hv, g, err / D)
        return dx, dg, jnp.broadcast_to(part, (8, 128))

    g_fin2 = g_final.reshape(1, D)
    dh3, dg_final, loss_acc = rowwise(down_loss, [(act, None), (h2, None), (tgt, None)], [wdn, g_fin2], [(D, F32)],
                                      [((1, D), F32), ((8, 128), F32)], tile=tile2, name="mm_down_loss")

    dact = matmul(dh3, wdn, tb=True, out_dtype=BF16, name="mm_dact", tm=2048, tn=1408)
    dw_down = matmul_tn(act, dh3, name="mm_dw_down", tm=1408, tk=1024)
    dup_a, dup_b, cs_a, cs_b = conv_act_bwd(up_pre, dact, wcv, b_conv, tile=tile, cw=cw)
    dw_up_a = matmul_tn(n3, dup_a, name="mm_dw_up_a", tn=1408)
    dw_up_b = matmul_tn(n3, dup_b, name="mm_dw_up_b", tn=1408)

    def ffn_in_bwd(da_, db_, hv, drv, wv_, g):
        dn = _dot_nt(da_, wv_[:, :D_FF]) + _dot_nt(db_, wv_[:, D_FF:])
        dx, dg = _rms_bwd(hv, g, dn)
        return dx + drv, dg

    dh2, dg_ffn = rowwise(ffn_in_bwd, [(dup_a, None), (dup_b, None), (h2, None), (dh3, None)], [wup, g_ffn],
                          [(D, F32)], [((1, D), F32)], tile=tile2, name="mm_dn3_norm_bwd")
    dxo = matmul(dh2, wxo, tb=True, out_dtype=BF16, name="mm_dxo")
    dw_xo = matmul_tn(xo, dh2, name="mm_dw_xo")

    def xattn_bwd(xqv, dxov, mkvv):
        dxq, dmk, dmv = [], [], []
        for h in range(X_H):
            sl = slice(h * X_HD, (h + 1) * X_HD)
            slv = slice(D + h * X_HD, D + (h + 1) * X_HD)
            s = _dot_nt(xqv[:, sl], mkvv[:, sl]) * x_scale
            s = s - jnp.max(s, axis=-1, keepdims=True)
            e = jnp.exp(s)
            p = e / jnp.sum(e, axis=-1, keepdims=True)
            dp = _dot_nt(dxov[:, sl], mkvv[:, slv])
            ds = (p * (dp - jnp.sum(dp * p, axis=-1, keepdims=True)) * x_scale).astype(BF16)
            dxq.append(_dot(ds, mkvv[:, sl]))
            dmk.append(_dot_tn(ds, xqv[:, sl]))
            dmv.append(_dot_tn(p.astype(BF16), dxov[:, sl]))
        return jnp.concatenate(dxq, axis=1), jnp.concatenate(dmk + dmv, axis=1)

    dxq, dmkv = rowwise(xattn_bwd, [(xq, None), (dxo, None)], [mkv], [(D, BF16)], [((M, 2 * D), F32)],
                        tile=tile4, name="xattn_bwd")
    dw_xq = matmul_tn(n2, dxq, name="mm_dw_xq")

    def proj_norm_bwd(dyv, hv, drv, wv_, g):
        dx, dg = _rms_bwd(hv, g, _dot_nt(dyv, wv_))
        return dx + drv, dg

    dh1, dg_cross = rowwise(proj_norm_bwd, [(dxq, None), (h1, None), (dh2, None)], [wxq, g_cross], [(D, F32)],
                            [((1, D), F32)], tile=tile4, name="mm_dn2_norm_bwd")
    dw_xkv = matmul_tn(mn, dmkv, name="mm_dw_xkv", tk=M)
    dmn = matmul(dmkv, wxkv, tb=True, name="mm_dmn", tm=M)
    dg_mem = rowwise(lambda mv_, dyv, g: _rms_bwd(mv_, g, dyv)[1], [(mems, None), (dmn, None)], [g_mem], [],
                     [((1, D), F32)], tile=min(tile, M), name="norm_mem_bwd")[0]

    dmix = matmul(dh1, wo, tb=True, out_dtype=BF16, name="mm_dmix")
    dw_out = matmul_tn(mix, dh1, name="mm_dw_out")

    def mix_bwd(ov, rv_, rg, gt, dmv_, wpav, wprv, gr, bg):
        dm_ = dmv_.astype(F32)
        ya = _dot(ov, wpav)
        parts = gn_parts(rv_)
        xh = jnp.concatenate([p[0] for p in parts], axis=1)
        yn = xh * gr
        sg = _sigmoid(rg)
        sl_ = rg * sg
        t = (sl_ * yn).astype(BF16)
        yr = _dot(t, wprv)
        ga_ = _sigmoid(gt[:, :D] + bg[:, :D])
        gr_ = _sigmoid(gt[:, D:] + bg[:, D:])
        dgates = jnp.concatenate([dm_ * ya * ga_ * (1.0 - ga_), dm_ * yr * gr_ * (1.0 - gr_)], axis=1)
        dya = (dm_ * ga_).astype(BF16)
        dyr = (dm_ * gr_).astype(BF16)
        do_ = _dot_nt(dya, wpav)
        dwpa_ = _dot_tn(ov, dya)
        dt = _dot_nt(dyr, wprv)
        dwpr_ = _dot_tn(t, dyr)
        drg_ = dt * yn * (sg * (1.0 + rg * (1.0 - sg)))
        dyn = dt * sl_
        dgr = jnp.sum(dyn * xh, axis=0, keepdims=True)
        dxh = dyn * gr
        drets = []
        for h in range(RET_H):
            sl = slice(h * RET_D, (h + 1) * RET_D)
            xhh, rstd = parts[h]
            dxhh = dxh[:, sl]
            drets.append(rstd * (dxhh - jnp.mean(dxhh, axis=-1, keepdims=True)
                                 - xhh * jnp.mean(dxhh * xhh, axis=-1, keepdims=True)))
        dret_ = jnp.concatenate(drets, axis=1)
        dbg = jnp.sum(dgates, axis=0, keepdims=True)
        prod = ov.astype(F32) * do_.astype(BF16).astype(F32)
        lane = lax.broadcasted_iota(jnp.int32, (prod.shape[0], HP), 1)
        dlt = jnp.zeros((prod.shape[0], HP), F32)
        for h in range(MLA_H):
            dlt = jnp.where(lane == h, jnp.sum(prod[:, h * HP:(h + 1) * HP], axis=-1, keepdims=True), dlt)
        return do_, dret_, drg_, dgates, dlt, dwpa_, dwpr_, dgr, dbg

    do_a, dret, drg, dgates, delta, dwpa, dw_proj_ret, dg_ret, db_gate = rowwise(
        mix_bwd, [(o_a, None), (ret, None), (z, (512, 3)), (z, (2048, 1)), (dmix, None)], [wpa, wpr, g_ret, b_gate],
        [(MLA_H * HP, BF16), (512, F32), (512, BF16), (2 * D, BF16), (HP, F32)],
        [((MLA_H * HP, D), F32), ((512, D), F32), ((1, 512), F32), ((1, 2 * D), F32)], tile=tile2, name="mix_bwd")

    sl1 = slots_l1(dw_up_a, dw_up_b, dw_xkv, tile=tile)
    sl2, sl3 = slots_l23(dw_out, dw_xq, dw_xo, dw_down, dwpa, dw_proj_ret)
    delta_row = mla_prep(delta, tq=tq)
    rl1, rl2, rl3 = sequencer_exchange([sl1, sl2, sl3], name="sequencer_exchange_late", collective_id=LATE_EXCHANGE_ID)
    dq_a, dk_a, dv_a = mla_bwd(q_a, k_a, v_a, do_a, lse_row, delta_row, [], tq=tq)
    drq_r, drk_r, drv = ret_bwd(rq, rk, rv, rstate, dret, rconsts, rb=rb)

    def mixers_in_bwd(zl, cmv, s1v, s2v, dqv, dkv_, dvv, drq_, drk_, drv_, drg_, dgt, crv, srv, gq, gkv, wqv, wkv_, wvv):
        cqf, ckvf = zl[:, 0:256], zl[:, 256:384]
        cq = _rms(cqf, gq).astype(BF16)
        ckv = _rms(ckvf, gkv).astype(BF16)
        dq_pre = _rot_mla(dqv.astype(F32), cmv, -s1v, -s2v).astype(BF16)
        dkf = dkv_.astype(F32)
        dkr = dkf[:, 0:HP]
        for h in range(1, MLA_H):
            dkr = dkr + dkf[:, h * HP:(h + 1) * HP]
        lane = lax.broadcasted_iota(jnp.int32, dkr.shape, 1)
        dzk = _rot_mla(jnp.where((lane >= 64) & (lane < 96), dkr, 0.0), cmv, -s1v, -s2v)
        dkb = dkv_.astype(BF16)
        dvb = dvv.astype(BF16)
        dcq_n = _dot_nt(dq_pre, wqv)
        dckv_n = _dot_nt(dkb, wkv_) + _dot_nt(dvb, wvv)
        dwq_ = _dot_tn(cq, dq_pre)
        dwk_ = _dot_tn(ckv, dkb)
        dwv_ = _dot_tn(ckv, dvb)
        dcq, dgq = _rms_bwd(cqf, gq, dcq_n)
        dckv, dgkv = _rms_bwd(ckvf, gkv, dckv_n)
        a = _rot_ret(drq_, crv, -srv)
        b = _rot_ret(drk_, crv, -srv) * (RET_D ** -0.5)
        dz_ = jnp.concatenate([a.astype(BF16), b.astype(BF16), drv_, drg_, dgt,
                               dcq.astype(BF16), dckv.astype(BF16), dzk.astype(BF16)], axis=1)
        return dz_, dwq_, dwk_, dwv_, dgq, dgkv

    dz, dwq, dwk, dwv, dg_q_lat, dg_kv_lat = rowwise(
        mixers_in_bwd, [(z, (512, 8)), (cm, None), (s1, None), (s2, None), (dq_a, None), (dk_a, None), (dv_a, None),
                        (drq_r, None), (drk_r, None), (drv, None), (drg, None), (dgates, None), (cr, None), (sr, None)],
        [g_q_lat, g_kv_lat, wq, wk, wv], [(ZW, BF16)],
        [((MLA_QR, MLA_H * HP), F32), ((MLA_KVR, MLA_H * HP), F32), ((MLA_KVR, MLA_H * HP), F32),
         ((1, MLA_QR), F32), ((1, MLA_KVR), F32)], tile=tile2, name="mixers_in_bwd")
    dwz = matmul_tn(u, dz, name="mm_dw_z", tn=1536)
    se1, se2 = slots_early(dwz, dwk, dwv, dwq, tile=tile)
    re1, re2 = sequencer_exchange([se1, se2], name="sequencer_exchange_early", collective_id=EARLY_EXCHANGE_ID)
    grad_x, dg_mix = rowwise(proj_norm_bwd, [(dz, None), (xs, None), (dh1, None)], [wz, g_mix], [(D, F32)],
                             [((1, D), F32)], tile=tile2, name="mm_du_norm_bwd")

    small_grads = {"g_mix": dg_mix, "b_gate": db_gate, "g_q_lat": dg_q_lat, "g_kv_lat": dg_kv_lat, "g_ret": dg_ret,
                   "g_cross": dg_cross, "g_mem": dg_mem, "g_ffn": dg_ffn, "g_final": dg_final}
    rd, rs, dslots, own_s = exchange_small(cs_a, cs_b, loss_acc, [small_grads[n] for n, _, _ in SMALL_DIRECT])

    me = _my_id()

    def own(slots):
        return lax.dynamic_index_in_dim(slots, me, axis=0, keepdims=False)

    def wmv(names):
        return [shard(n, p) for n in names for p in ("", "m_", "v_")]

    names_s = tuple(n for n, _, _ in SMALL)
    small_outs = adam_small(own_s, rs, dslots, rd, wmv(names_s), wmv(("w_conv",)))
    groups = (
        (("w_in",), adam_cols(own(se1), re1, wmv(("w_in",)), ((0, NS_IN),), name="adam_e1", tile=128)),
        (("w_ukv", "w_uq"), adam_rows(own(se2), re2, wmv(("w_ukv", "w_uq")),
                                      ((0, E2_UQ, HP), (E2_UQ, E2_ROWS, NS_UQ)), name="adam_e2")),
        (("w_up", "w_xkv"), adam_cols(own(sl1), rl1, wmv(("w_up", "w_xkv")), ((0, NS_UP), (NS_UP, L1_W)),
                                      name="adam_l1", tile=128)),
        (("w_out", "w_xq", "w_xo", "w_down"),
         adam_rows(own(sl2), rl2, wmv(("w_out", "w_xq", "w_xo", "w_down")),
                   ((L2_OUT, L2_XQ, D), (L2_XQ, L2_XO, D), (L2_XO, L2_DN, D), (L2_DN, L2_ROWS, D)), name="adam_l2")),
        (("w_proj_mla", "w_proj_ret"), adam_rows(own(sl3), rl3, wmv(("w_proj_mla", "w_proj_ret")),
                                                 ((0, L3_PRET, HP), (L3_PRET, L3_ROWS, HP)), name="adam_l3")),
        (names_s + ("w_conv",), small_outs),
    )
    loss = small_outs[-1][0, 0]
    res = {}
    for names, outs_ in groups:
        for t, n in enumerate(names):
            res[n] = outs_[4 * t:4 * t + 4]

    order = ["g_mix", "w_in", "b_gate", "g_q_lat", "w_uq", "g_kv_lat", "w_ukv", "w_proj_mla", "g_ret", "w_proj_ret",
             "w_out", "g_cross", "g_mem", "w_xq", "w_xkv", "w_xo", "g_ffn", "w_up", "w_conv", "b_conv", "w_down",
             "g_final"]
    outs = [loss, grad_x[None]]
    for kind in range(4):
        outs += [res[n][kind].reshape(args[n].shape) for n in order]
    return tuple(outs)
```
